```python
import math
import jax, jax.numpy as jnp
from jax import lax
import numpy as np

D_MODEL = 2048
BATCH = 2
SEQ = 4096
DEPTH = 1

D_MIX = D_MODEL
EPS = 1e-6
M_WIDTH = D_MIX // 2
M_HEADS = 4
M_HD = M_WIDTH // M_HEADS
M_CHUNK = 64
CONV_K = 4
N_WIDTH = D_MIX - M_WIDTH
N_HEADS = 16
N_HD = N_WIDTH // N_HEADS
N_KV = 4
N_HPG = N_HEADS // N_KV
KV_W = N_KV * N_HD
CMP_LEN = 32
CMP_STRIDE = 16
CMP_HIDDEN = 2 * N_HD
SLC_LEN = 64
SLC_TOPN = 16
WIN = 512
QBLK = 128

IN_SPLITS = (M_WIDTH, M_WIDTH, M_WIDTH, M_WIDTH, M_HEADS, M_HEADS,
             N_WIDTH, KV_W, KV_W, KV_W, KV_W, KV_W, KV_W, 3 * N_HEADS, N_WIDTH)
D_IN = sum(IN_SPLITS)

kernel_name = "hymba_mlstm_nsa_adaln_layer"


def rmsnorm(x, g):
    x = x.astype(jnp.float32)
    return x * lax.rsqrt(jnp.mean(x * x, axis=-1, keepdims=True) + EPS) * g


def masked_softmax(s, mask, axis):
    s = jnp.where(mask, s.astype(jnp.float32), -jnp.inf)
    m = jnp.max(s, axis=axis, keepdims=True)
    m = jnp.where(jnp.isfinite(m), m, 0.0)
    e = jnp.where(mask, jnp.exp(s - m), 0.0)
    z = jnp.sum(e, axis=axis, keepdims=True)
    return e / jnp.where(z > 0, z, 1.0)


def alibi_slopes():
    return np.array([2.0 ** (-8.0 * (h + 1) / N_HEADS) for h in range(N_HEADS)], np.float32)


def mlstm_chunkwise(q, k, v, i_pre, f_pre):
    B, H, T, D = q.shape
    L = M_CHUNK
    nch = T // L
    k = k * (D ** -0.5)
    logf = jax.nn.log_sigmoid(f_pre)
    to_c = lambda a: jnp.moveaxis(a.reshape(a.shape[:2] + (nch, L) + a.shape[3:]), 2, 0)
    xs = (to_c(q), to_c(k), to_c(v), to_c(i_pre), to_c(logf))
    causal = jnp.tril(jnp.ones((L, L), bool))

    def step(carry, inp):
        C, n, m = carry
        qb, kb, vb, ib, fb = inp
        b = jnp.cumsum(fb, axis=-1)
        Dm = jnp.where(causal, b[..., :, None] - b[..., None, :] + ib[..., None, :], -jnp.inf)
        inter = b + m[..., None]
        m_t = jnp.maximum(inter, jnp.max(Dm, axis=-1))
        w_in = jnp.exp(Dm - m_t[..., None])
        w_st = jnp.exp(inter - m_t)
        s = jnp.einsum('bhld,bhsd->bhls', qb, kb) * w_in
        num = w_st[..., None] * jnp.einsum('bhld,bhde->bhle', qb, C) + jnp.einsum('bhls,bhse->bhle', s, vb)
        den = w_st * jnp.einsum('bhld,bhd->bhl', qb, n) + jnp.sum(s, axis=-1)
        h = num / jnp.maximum(jnp.abs(den), jnp.exp(-m_t))[..., None]
        bL = b[..., -1]
        w_end = bL[..., None] - b + ib
        m_new = jnp.maximum(bL + m, jnp.max(w_end, axis=-1))
        decay = jnp.exp(bL + m - m_new)
        wk = jnp.exp(w_end - m_new[..., None])
        C_new = decay[..., None, None] * C + jnp.einsum('bhs,bhsd,bhse->bhde', wk, kb, vb)
        n_new = decay[..., None] * n + jnp.einsum('bhs,bhsd->bhd', wk, kb)
        return (C_new, n_new, m_new), h

    init = (jnp.zeros((B, H, D, D), jnp.float32), jnp.zeros((B, H, D), jnp.float32),
            jnp.zeros((B, H), jnp.float32))
    _, hs = lax.scan(step, init, xs)
    return jnp.moveaxis(hs, 0, 2).reshape(B, H, T, D)


def mlstm_group(xm, vm, om, zm, im, fm, conv_w, conv_b, wq, wk, norm_w, skip, f_bias):
    B, T, _ = xm.shape
    xpad = jnp.pad(xm, ((0, 0), (CONV_K - 1, 0), (0, 0)))
    xc = conv_b + xpad[:, 0:T] * conv_w[0]
    for j in range(1, CONV_K):
        xc = xc + xpad[:, j:j + T] * conv_w[j]
    xc = jax.nn.silu(xc)
    xch = xc.reshape(B, T, M_HEADS, M_HD)
    q = jnp.einsum('bthd,hde->bhte', xch, wq)
    k = jnp.einsum('bthd,hde->bhte', xch, wk)
    v = vm.reshape(B, T, M_HEADS, M_HD).transpose(0, 2, 1, 3)
    h = mlstm_chunkwise(q, k, v, im.transpose(0, 2, 1), (fm + f_bias).transpose(0, 2, 1))
    h = h.transpose(0, 2, 1, 3)
    mu = jnp.mean(h, axis=-1, keepdims=True)
    var = jnp.mean(jnp.square(h - mu), axis=-1, keepdims=True)
    hn = ((h - mu) * lax.rsqrt(var + EPS)).reshape(B, T, M_WIDTH) * norm_w
    out = jax.nn.sigmoid(om) * hn + skip * xc
    return out * jax.nn.silu(zm)


def nsa_group(qn, kc, vc, ks, vs, kw, vw, gl, zn, pos_k, pos_v, w1_k, w2_k, w1_v, w2_v):
    B, T, _ = qn.shape
    G, HPG, dh = N_KV, N_HPG, N_HD
    n_cmp = (T - CMP_LEN) // CMP_STRIDE + 1
    n_slc = T // SLC_LEN
    n_top = min(SLC_TOPN, n_slc)
    n_qb = T // QBLK
    q = qn.reshape(B, T, G, HPG, dh) * (dh ** -0.5)
    kc = kc.reshape(B, T, G, dh); vc = vc.reshape(B, T, G, dh)
    ks = ks.reshape(B, T, G, dh); vs = vs.reshape(B, T, G, dh)
    kw = kw.reshape(B, T, G, dh); vw = vw.reshape(B, T, G, dh)

    cidx = np.arange(n_cmp)[:, None] * CMP_STRIDE + np.arange(CMP_LEN)[None, :]

    def compress(a, pos, w1, w2):
        blk = a[:, cidx] + pos[None, None, :, None, :]
        blk = blk.transpose(0, 1, 3, 2, 4).reshape(B, n_cmp, G, CMP_LEN * dh)
        return jax.nn.gelu(blk @ w1) @ w2

    k_cmp = compress(kc, pos_k, w1_k, w2_k)
    v_cmp = compress(vc, pos_v, w1_v, w2_v)
    cmp_end = jnp.asarray(cidx[:, -1], jnp.int32)
    cmp_mid = jnp.asarray(cidx.mean(-1), jnp.float32)
    s_start = np.arange(n_slc) * SLC_LEN
    overlap = jnp.asarray((cidx[:, 0][:, None] <= s_start[None, :] + SLC_LEN - 1)
                          & (cidx[:, -1][:, None] >= s_start[None, :]), jnp.float32)
    slopes = jnp.asarray(alibi_slopes()).reshape(G, HPG)

    ks_blk = ks.reshape(B, n_slc, SLC_LEN, G, dh).transpose(0, 3, 1, 2, 4)
    vs_blk = vs.reshape(B, n_slc, SLC_LEN, G, dh).transpose(0, 3, 1, 2, 4)
    kw_pad = jnp.pad(kw, ((0, 0), (WIN, 0), (0, 0), (0, 0)))
    vw_pad = jnp.pad(vw, ((0, 0), (WIN, 0), (0, 0), (0, 0)))
    gates = jax.nn.sigmoid(gl).reshape(B, T, G, HPG, 3)
    q_blocks = q.reshape(B, n_qb, QBLK, G, HPG, dh).swapaxes(0, 1)
    g_blocks = gates.reshape(B, n_qb, QBLK, G, HPG, 3).swapaxes(0, 1)
    starts = jnp.arange(n_qb, dtype=jnp.int32) * QBLK
    bi = jnp.arange(B)[:, None, None, None]
    gi = jnp.arange(G)[None, :, None, None]
    sidx = jnp.arange(n_slc, dtype=jnp.int32)

    def block(args):
        qb, gb, start = args
        t = start + jnp.arange(QBLK, dtype=jnp.int32)
        tf = t.astype(jnp.float32)
        s_c = jnp.einsum('bqghd,bjgd->bqghj', qb, k_cmp)
        s_c = s_c - slopes[None, None, :, :, None] * (tf[:, None] - cmp_mid[None, :])[None, :, None, None, :]
        mask_c = (cmp_end[None, :] <= t[:, None])[None, :, None, None, :]
        p_c = masked_softmax(s_c, mask_c, -1)
        o_c = jnp.einsum('bqghj,bjgd->bqghd', p_c, v_cmp)
        imp = jnp.einsum('bqghj,js->bgqs', p_c, overlap)
        cur = t // SLC_LEN
        imp = jnp.where(sidx[None, :] <= cur[:, None], imp, -jnp.inf)
        imp = jnp.where((sidx[None, :] == 0) | (sidx[None, :] == cur[:, None]), jnp.inf, imp)
        top_val, top_idx = lax.top_k(imp, n_top)
        sel_ok = top_val > -jnp.inf
        k_sel = ks_blk[bi, gi, top_idx]
        v_sel = vs_blk[bi, gi, top_idx]
        qg = qb.transpose(0, 2, 1, 3, 4)
        s_s = jnp.einsum('bgqhd,bgqnld->bgqhnl', qg, k_sel)
        pos = top_idx[..., None] * SLC_LEN + jnp.arange(SLC_LEN, dtype=jnp.int32)
        dist_s = (t[None, None, :, None, None] - pos).astype(jnp.float32)
        s_s = s_s - slopes[None, :, None, :, None, None] * dist_s[:, :, :, None]
        mask_s = ((pos <= t[None, None, :, None, None]) & sel_ok[..., None])[:, :, :, None]
        p_s = masked_softmax(s_s, mask_s, (-2, -1))
        o_s = jnp.einsum('bgqhnl,bgqnld->bgqhd', p_s, v_sel).transpose(0, 2, 1, 3, 4)
        k_win = lax.dynamic_slice_in_dim(kw_pad, start, QBLK + WIN, axis=1)
        v_win = lax.dynamic_slice_in_dim(vw_pad, start, QBLK + WIN, axis=1)
        pos_w = start - WIN + jnp.arange(QBLK + WIN, dtype=jnp.int32)
        dw = t[:, None] - pos_w[None, :]
        mask_w = ((dw >= 0) & (dw < WIN) & (pos_w[None, :] >= 0))[None, :, None, None, :]
        s_w = jnp.einsum('bqghd,bkgd->bqghk', qb, k_win)
        s_w = s_w - slopes[None, None, :, :, None] * dw.astype(jnp.float32)[None, :, None, None, :]
        p_w = masked_softmax(s_w, mask_w, -1)
        o_w = jnp.einsum('bqghk,bkgd->bqghd', p_w, v_win)
        return gb[..., 0:1] * o_c + gb[..., 1:2] * o_s + gb[..., 2:3] * o_w

    o = lax.map(block, (q_blocks, g_blocks, starts))
    o = o.swapaxes(0, 1).reshape(B, T, N_WIDTH)
    return o * jax.nn.silu(zn)


def setup_inputs(seed: int = 0) -> dict:
    key = jax.random.key(seed)
    k = jax.random.split(key, 24)
    f32 = jnp.float32
    nrm = lambda kk, shape, scale: jax.random.normal(kk, shape, f32) * scale
    L = DEPTH
    return {
        'x': nrm(k[0], (BATCH, SEQ, D_MODEL), 1.0),
        'c': nrm(k[1], (BATCH, D_MODEL), 1.0),
        'ln_g': 1.0 + nrm(k[2], (L, D_MODEL), 0.02),
        'w_ada': nrm(k[3], (L, D_MODEL, 3 * D_MODEL), 0.5 * D_MODEL ** -0.5),
        'b_ada': nrm(k[4], (L, 3 * D_MODEL), 0.02),
        'w_in': nrm(k[5], (L, D_MODEL, D_IN), D_MODEL ** -0.5),
        'b_in': nrm(k[6], (L, D_IN), 0.02),
        'm_conv_w': nrm(k[7], (L, CONV_K, M_WIDTH), CONV_K ** -0.5),
        'm_conv_b': nrm(k[8], (L, M_WIDTH), 0.02),
        'm_wq': nrm(k[9], (L, M_HEADS, M_HD, M_HD), M_HD ** -0.5),
        'm_wk': nrm(k[10], (L, M_HEADS, M_HD, M_HD), M_HD ** -0.5),
        'm_norm_w': 1.0 + nrm(k[11], (L, M_WIDTH), 0.02),
        'm_skip': 1.0 + nrm(k[12], (L, M_WIDTH), 0.02),
        'm_f_bias': jnp.linspace(3.0, 6.0, M_HEADS, dtype=f32)[None, :] + nrm(k[13], (L, M_HEADS), 0.02),
        'n_pos_k': nrm(k[14], (L, CMP_LEN, N_HD), 0.5),
        'n_pos_v': nrm(k[15], (L, CMP_LEN, N_HD), 0.5),
        'n_w1_k': nrm(k[16], (L, CMP_LEN * N_HD, CMP_HIDDEN), (CMP_LEN * N_HD) ** -0.5),
        'n_w2_k': nrm(k[17], (L, CMP_HIDDEN, N_HD), CMP_HIDDEN ** -0.5),
        'n_w1_v': nrm(k[18], (L, CMP_LEN * N_HD, CMP_HIDDEN), (CMP_LEN * N_HD) ** -0.5),
        'n_w2_v': nrm(k[19], (L, CMP_HIDDEN, N_HD), CMP_HIDDEN ** -0.5),
        'w_out': nrm(k[20], (L, D_MIX, D_MODEL), D_MIX ** -0.5),
        'final_g': 1.0 + nrm(k[21], (D_MODEL,), 0.02),
    }


def reference(x, c, ln_g, w_ada, b_ada, w_in, b_in, m_conv_w, m_conv_b, m_wq, m_wk, m_norm_w,
              m_skip, m_f_bias, n_pos_k, n_pos_v, n_w1_k, n_w2_k, n_w1_v, n_w2_v, w_out, final_g):
    out_dtype = x.dtype
    h_res = x.astype(jnp.float32)
    cf = c.astype(jnp.float32)
    offsets = [int(o) for o in np.cumsum(IN_SPLITS)[:-1]]
    for l in range(DEPTH):
        mod = jax.nn.silu(cf) @ w_ada[l] + b_ada[l]
        shift, scale, gate = jnp.split(mod, 3, axis=-1)
        h = rmsnorm(h_res, ln_g[l]) * (1.0 + scale[:, None]) + shift[:, None]
        p = h @ w_in[l] + b_in[l]
        (m_x, m_v, m_o, m_z, m_i, m_f, n_q, n_kc, n_vc, n_ks, n_vs, n_kw, n_vw, n_g, n_z) = \
            jnp.split(p, offsets, axis=-1)
        y_m = mlstm_group(m_x, m_v, m_o, m_z, m_i, m_f, m_conv_w[l], m_conv_b[l], m_wq[l], m_wk[l],
                          m_norm_w[l], m_skip[l], m_f_bias[l])
        y_n = nsa_group(n_q, n_kc, n_vc, n_ks, n_vs, n_kw, n_vw, n_g, n_z, n_pos_k[l], n_pos_v[l],
                        n_w1_k[l], n_w2_k[l], n_w1_v[l], n_w2_v[l])
        y = jnp.concatenate([y_m, y_n], axis=-1) @ w_out[l]
        h_res = h_res + gate[:, None] * y
    return rmsnorm(h_res, final_g).astype(out_dtype)
```

```python
import functools

import numpy as np
import jax
import jax.numpy as jnp
from jax import lax
from jax.experimental import pallas as pl
from jax.experimental.pallas import tpu as pltpu

F32 = jnp.float32
BF16 = jnp.bfloat16
HIGHEST = lax.Precision.HIGHEST

EPS = 1e-6
M_HEADS = 4
M_HD = 256
M_WIDTH = M_HEADS * M_HD
CONV_K = 4
M_CHUNK = 256
N_HEADS = 16
N_HD = 64
N_KV = 4
N_HPG = N_HEADS // N_KV
N_WIDTH = N_HEADS * N_HD
KV_W = N_KV * N_HD
CMP_LEN = 32
CMP_STRIDE = 16
CMP_HIDDEN = 2 * N_HD
SLC_LEN = 64
SLC_TOPN = 16
WIN = 512
QBLK = 128
KTILE = 128

COL_MX, COL_MV, COL_MO, COL_MZ = 0, 1024, 2048, 3072
COL_NQ = 4096
COL_KC, COL_VC, COL_KS, COL_VS, COL_KW, COL_VW = 5120, 5376, 5632, 5888, 6144, 6400
COL_NZ = 6656
COL_GATES = 7680
GATE_NG = 2 * M_HEADS
NP_PAD = 8192
LANES = 128
NEG = -1e30
VMEM_LIMIT = 56 * 1024 * 1024


def _cparams(sem):
    return pltpu.CompilerParams(dimension_semantics=sem, vmem_limit_bytes=VMEM_LIMIT)


def _silu(x):
    return x * jax.nn.sigmoid(x)


def _log_sigmoid(x):
    return jnp.minimum(x, 0.0) - jnp.log1p(jnp.exp(-jnp.abs(x)))


def _ada_kernel(c_ref, w_ref, b_ref, o_ref):
    s = _silu(c_ref[...])
    o_ref[...] = jnp.dot(s, w_ref[...], precision=HIGHEST, preferred_element_type=F32) + b_ref[...]


def _ada(c8, w, b):
    d, n = w.shape
    tn = 1024
    return pl.pallas_call(
        _ada_kernel,
        grid=(n // tn,),
        in_specs=[pl.BlockSpec((8, d), lambda j: (0, 0)),
                  pl.BlockSpec((d, tn), lambda j: (0, j)),
                  pl.BlockSpec((1, tn), lambda j: (0, j))],
        out_specs=pl.BlockSpec((8, tn), lambda j: (0, j)),
        out_shape=jax.ShapeDtypeStruct((8, n), F32),
        compiler_params=_cparams(("parallel",)),
        name="ada_mod",
    )(c8, w, b)


def _inproj_kernel(x_ref, g_ref, sc_ref, sh_ref, w_ref, b_ref, o_ref, h_ref):
    @pl.when(pl.program_id(2) == 0)
    def _():
        x = x_ref[0]
        ms = jnp.mean(x * x, axis=-1, keepdims=True)
        h = x * lax.rsqrt(ms + EPS) * g_ref[...]
        h = h * (1.0 + sc_ref[0]) + sh_ref[0]
        h_ref[...] = h.astype(BF16)

    o_ref[0] = jnp.dot(h_ref[...], w_ref[...], preferred_element_type=F32) + b_ref[...]


def _inproj(x, g, scale, shift, w, b):
    bsz, t, d = x.shape
    n = w.shape[1]
    tm, tn = 1024, 1024
    return pl.pallas_call(
        _inproj_kernel,
        grid=(bsz, t // tm, n // tn),
        in_specs=[pl.BlockSpec((1, tm, d), lambda bi, i, j: (bi, i, 0)),
                  pl.BlockSpec((1, d), lambda bi, i, j: (0, 0)),
                  pl.BlockSpec((1, 1, d), lambda bi, i, j: (bi, 0, 0)),
                  pl.BlockSpec((1, 1, d), lambda bi, i, j: (bi, 0, 0)),
                  pl.BlockSpec((d, tn), lambda bi, i, j: (0, j)),
                  pl.BlockSpec((1, tn), lambda bi, i, j: (0, j))],
        out_specs=pl.BlockSpec((1, tm, tn), lambda bi, i, j: (bi, i, j)),
        out_shape=jax.ShapeDtypeStruct((bsz, t, n), F32),
        scratch_shapes=[pltpu.VMEM((tm, d), BF16)],
        compiler_params=_cparams(("parallel", "parallel", "arbitrary")),
        name="norm_inproj",
    )(x, g, scale, shift, w, b)


def _mlstm_kernel(x_ref, v_ref, o_ref, z_ref, gt_ref, cw_ref, cb_ref, wq_ref, wk_ref, nw_ref, sk_ref, fb_ref,
                  y_ref, c_scr, n_scr, m_scr, xp_scr):
    L = M_CHUNK

    @pl.when(pl.program_id(1) == 0)
    def _():
        c_scr[...] = jnp.zeros_like(c_scr)
        n_scr[...] = jnp.zeros_like(n_scr)
        m_scr[...] = jnp.zeros_like(m_scr)
        xp_scr[...] = jnp.zeros_like(xp_scr)

    x = x_ref[0]
    prev = xp_scr[...]
    row8 = lax.broadcasted_iota(jnp.int32, (8, M_WIDTH), 0)
    cw = cw_ref[...]
    xc = cb_ref[...] + x * cw[CONV_K - 1:CONV_K, :]
    for sft in range(1, CONV_K):
        xr = pltpu.roll(x, sft, 0)
        top = jnp.where(row8 < sft, pltpu.roll(prev, sft, 0), xr[0:8])
        xs = jnp.concatenate([top, xr[8:]], axis=0)
        xc = xc + xs * cw[CONV_K - 1 - sft:CONV_K - sft, :]
    xp_scr[...] = x[L - 8:L]
    xc = _silu(xc)

    gt = gt_ref[0]
    col = lax.broadcasted_iota(jnp.int32, (L, LANES), 1)
    logf = _log_sigmoid(gt + fb_ref[...])
    a_c = jnp.where((col >= M_HEADS) & (col < 2 * M_HEADS), logf, gt)
    ri = lax.broadcasted_iota(jnp.int32, (L, L), 0)
    ci = lax.broadcasted_iota(jnp.int32, (L, L), 1)
    causal = ri >= ci
    tri = causal.astype(F32)
    tri_t = (ri <= ci).astype(F32)
    b_c = jnp.dot(tri, a_c, precision=HIGHEST, preferred_element_type=F32)
    a_r = a_c.T
    b_r = jnp.dot(a_r[0:8], tri_t, precision=HIGHEST, preferred_element_type=F32)

    for h in range(M_HEADS):
        sl = slice(h * M_HD, (h + 1) * M_HD)
        xh = xc[:, sl]
        xb = xh.astype(BF16)
        q = jnp.dot(xb, wq_ref[h], preferred_element_type=F32)
        k = jnp.dot(xb, wk_ref[h], preferred_element_type=F32) * (M_HD ** -0.5)
        vb = v_ref[0, :, sl].astype(BF16)
        qb = q.astype(BF16)
        kb = k.astype(BF16)

        bt = b_c[:, M_HEADS + h:M_HEADS + h + 1]
        ic = a_c[:, h:h + 1]
        bs = b_r[M_HEADS + h:M_HEADS + h + 1, :]
        ir = a_r[h:h + 1, :]
        m_prev = m_scr[h][:, 0:1]

        dm = jnp.where(causal, bt - bs + ir, -jnp.inf)
        inter = bt + m_prev
        m_t = jnp.maximum(inter, jnp.max(dm, axis=-1, keepdims=True))
        w_in = jnp.exp(dm - m_t)
        w_st = jnp.exp(inter - m_t)
        s = lax.dot_general(qb, kb, (((1,), (1,)), ((), ())), preferred_element_type=F32) * w_in
        cmat = c_scr[h]
        nvec = n_scr[h]
        num = w_st * jnp.dot(qb, cmat.astype(BF16), preferred_element_type=F32) \
            + jnp.dot(s.astype(BF16), vb, preferred_element_type=F32)
        den = w_st * jnp.sum(q * nvec, axis=-1, keepdims=True) + jnp.sum(s, axis=-1, keepdims=True)
        hh = num / jnp.maximum(jnp.abs(den), jnp.exp(-m_t))

        b_last = bt[L - 1:L, :]
        w_end = b_last - bt + ic
        m_new = jnp.maximum(b_last + m_prev, jnp.max(w_end, axis=0, keepdims=True))
        decay = jnp.exp(b_last + m_prev - m_new)
        kwt = k * jnp.exp(w_end - m_new)
        c_scr[h] = decay * cmat + lax.dot_general(kwt.astype(BF16), vb, (((0,), (0,)), ((), ())),
                                                  preferred_element_type=F32)
        n_scr[h] = decay * nvec + jnp.sum(kwt, axis=0, keepdims=True)
        m_scr[h] = jnp.broadcast_to(m_new, (1, LANES))

        mu = jnp.mean(hh, axis=-1, keepdims=True)
        hc = hh - mu
        var = jnp.mean(hc * hc, axis=-1, keepdims=True)
        hn = hc * lax.rsqrt(var + EPS) * nw_ref[:, sl]
        out = jax.nn.sigmoid(o_ref[0, :, sl]) * hn + sk_ref[:, sl] * xh
        y_ref[0, :, sl] = (out * _silu(z_ref[0, :, sl])).astype(BF16)


def _mlstm(p, conv_w, conv_b, wq, wk, norm_w, skip, fb_row):
    bsz, t, _ = p.shape
    L = M_CHUNK
    cb = lambda c: (lambda bi, i: (bi, i, c))
    full2 = lambda bi, i: (0, 0)
    full3 = lambda bi, i: (0, 0, 0)
    return pl.pallas_call(
        _mlstm_kernel,
        grid=(bsz, t // L),
        in_specs=[pl.BlockSpec((1, L, M_WIDTH), cb(COL_MX // M_WIDTH)),
                  pl.BlockSpec((1, L, M_WIDTH), cb(COL_MV // M_WIDTH)),
                  pl.BlockSpec((1, L, M_WIDTH), cb(COL_MO // M_WIDTH)),
                  pl.BlockSpec((1, L, M_WIDTH), cb(COL_MZ // M_WIDTH)),
                  pl.BlockSpec((1, L, LANES), cb(COL_GATES // LANES)),
                  pl.BlockSpec((CONV_K, M_WIDTH), full2),
                  pl.BlockSpec((1, M_WIDTH), full2),
                  pl.BlockSpec((M_HEADS, M_HD, M_HD), full3),
                  pl.BlockSpec((M_HEADS, M_HD, M_HD), full3),
                  pl.BlockSpec((1, M_WIDTH), full2),
                  pl.BlockSpec((1, M_WIDTH), full2),
                  pl.BlockSpec((1, LANES), full2)],
        out_specs=pl.BlockSpec((1, L, M_WIDTH), lambda bi, i: (bi, i, 0)),
        out_shape=jax.ShapeDtypeStruct((bsz, t, M_WIDTH), BF16),
        scratch_shapes=[pltpu.VMEM((M_HEADS, M_HD, M_HD), F32),
                        pltpu.VMEM((M_HEADS, 1, M_HD), F32),
                        pltpu.VMEM((M_HEADS, 1, LANES), F32),
                        pltpu.VMEM((8, M_WIDTH), F32)],
        compiler_params=_cparams(("parallel", "arbitrary")),
        name="mlstm_group",
    )(p, p, p, p, p, conv_w, conv_b, wq, wk, norm_w, skip, fb_row)


def _compress_kernel(kc0_ref, kc1_ref, vc0_ref, vc1_ref, posk_ref, posv_ref, w1k_ref, w1kc_ref, w2k_ref,
                     w1v_ref, w1vc_ref, w2v_ref, kcmp_ref, vcmpt_ref):
    nb = kcmp_ref.shape[2]

    def hidden(src_refs, pos_ref, w1_ref, w1c_ref):
        acc = jnp.zeros((N_KV * nb, 2 * CMP_HIDDEN), F32)
        for l in range(CMP_STRIDE):
            xl = [r[0, pl.ds(l, nb, stride=CMP_STRIDE), :] for r in src_refs]
            xs = jnp.concatenate([x[:, g * N_HD:(g + 1) * N_HD] for x in xl for g in range(LANES // N_HD)], axis=0)
            acc = acc + jnp.dot(xs.astype(BF16), w1c_ref[l], preferred_element_type=F32)
        first = acc[:, :CMP_HIDDEN]
        second = acc[:, CMP_HIDDEN:]
        posb = jnp.dot(jnp.broadcast_to(pos_ref[...], (8, CMP_LEN * N_HD)).astype(BF16), w1_ref[...],
                       preferred_element_type=F32)[0:1]
        hid = first + pltpu.roll(second, N_KV * nb - 1, 0) + posb
        return jax.nn.gelu(hid).astype(BF16)

    hk = hidden((kc0_ref, kc1_ref), posk_ref, w1k_ref, w1kc_ref)
    kc = jnp.dot(hk, w2k_ref[...], preferred_element_type=F32)
    hv = hidden((vc0_ref, vc1_ref), posv_ref, w1v_ref, w1vc_ref)
    vc = jnp.dot(hv, w2v_ref[...], preferred_element_type=F32)
    for g in range(N_KV):
        kcmp_ref[0, g] = kc[g * nb:(g + 1) * nb]
        vcmpt_ref[0, g] = vc[g * nb:(g + 1) * nb].T


def _compress(p, posk, posv, w1k, w1kc, w2k, w1v, w1vc, w2v):
    bsz, t, _ = p.shape
    nb = t // CMP_STRIDE
    cb = lambda c: (lambda bi: (bi, 0, c))
    f2 = lambda bi: (0, 0)
    f3 = lambda bi: (0, 0, 0)
    wspecs = [pl.BlockSpec((CMP_LEN * N_HD, CMP_HIDDEN), f2),
              pl.BlockSpec((CMP_STRIDE, N_HD, 2 * CMP_HIDDEN), f3),
              pl.BlockSpec((CMP_HIDDEN, LANES), f2)]
    return pl.pallas_call(
        _compress_kernel,
        grid=(bsz,),
        in_specs=[pl.BlockSpec((1, t, LANES), cb(COL_KC // LANES)),
                  pl.BlockSpec((1, t, LANES), cb(COL_KC // LANES + 1)),
                  pl.BlockSpec((1, t, LANES), cb(COL_VC // LANES)),
                  pl.BlockSpec((1, t, LANES), cb(COL_VC // LANES + 1)),
                  pl.BlockSpec((1, CMP_LEN * N_HD), f2),
                  pl.BlockSpec((1, CMP_LEN * N_HD), f2)] + wspecs + wspecs,
        out_specs=[pl.BlockSpec((1, N_KV, nb, LANES), lambda bi: (bi, 0, 0, 0)),
                   pl.BlockSpec((1, N_KV, LANES, nb), lambda bi: (bi, 0, 0, 0))],
        out_shape=[jax.ShapeDtypeStruct((bsz, N_KV, nb, LANES), F32),
                   jax.ShapeDtypeStruct((bsz, N_KV, LANES, nb), F32)],
        compiler_params=_cparams(("parallel",)),
        name="nsa_compress",
    )(p, p, p, p, posk, posv, w1k, w1kc, w2k, w1v, w1vc, w2v)


def _relayout_kernel(ks_ref, vs_ref, kw_ref, vw_ref, kso_ref, vso_ref, kwo_ref, vwo_ref):
    for src, dst in ((ks_ref, kso_ref), (kw_ref, kwo_ref)):
        a = src[0]
        for g in range(N_KV):
            dst[0, g] = a[:, g * N_HD:(g + 1) * N_HD].astype(BF16)
    for src, dst in ((vs_ref, vso_ref), (vw_ref, vwo_ref)):
        a = src[0].T
        for g in range(N_KV):
            dst[0, g, 0] = a[g * N_HD:(g + 1) * N_HD].astype(BF16)


def _relayout(p):
    bsz, t, _ = p.shape
    nt = t // KTILE
    cb = lambda c: (lambda bi, i: (bi, i, c))
    kspec = pl.BlockSpec((1, N_KV, KTILE, N_HD), lambda bi, i: (bi, 0, i, 0))
    vspec = pl.BlockSpec((1, N_KV, 1, N_HD, KTILE), lambda bi, i: (bi, 0, i, 0, 0))
    kshape = jax.ShapeDtypeStruct((bsz, N_KV, t, N_HD), BF16)
    vshape = jax.ShapeDtypeStruct((bsz, N_KV, nt, N_HD, KTILE), BF16)
    return pl.pallas_call(
        _relayout_kernel,
        grid=(bsz, nt),
        in_specs=[pl.BlockSpec((1, KTILE, KV_W), cb(COL_KS // KV_W)),
                  pl.BlockSpec((1, KTILE, KV_W), cb(COL_VS // KV_W)),
                  pl.BlockSpec((1, KTILE, KV_W), cb(COL_KW // KV_W)),
                  pl.BlockSpec((1, KTILE, KV_W), cb(COL_VW // KV_W))],
        out_specs=[kspec, vspec, kspec, vspec],
        out_shape=[kshape, vshape, kshape, vshape],
        compiler_params=_cparams(("parallel", "parallel")),
        name="nsa_relayout",
    )(p, p, p, p)


def _nsa_kernel(slopes_ref, q_ref, gt_ref, z_ref, kcmp_ref, vcmpt_ref, ovt_ref, ks_ref, vs_ref, kw_ref, vw_ref,
                y_ref, sel_scr, g_scr):
    g = pl.program_id(1)
    qb = pl.program_id(2)
    start = qb * QBLK
    nb = kcmp_ref.shape[2]
    n_slc = ovt_ref.shape[0]

    qt = (q_ref[0] * (N_HD ** -0.5)).T
    q4t = jnp.concatenate([qt[h * N_HD:(h + 1) * N_HD] for h in range(N_HPG)], axis=1)
    q4b = q4t.astype(BF16)
    t_row = start + lax.broadcasted_iota(jnp.int32, (1, QBLK), 1)
    tf_row = t_row.astype(F32)
    slopes = [slopes_ref[g * N_HPG + h] for h in range(N_HPG)]

    g_scr[...] = jax.nn.sigmoid(gt_ref[0]).T

    jcol = lax.broadcasted_iota(jnp.int32, (nb, 1), 0)
    cmp_mid = jcol.astype(F32) * CMP_STRIDE + (CMP_LEN - 1) * 0.5
    cmp_end = jcol * CMP_STRIDE + (CMP_LEN - 1)
    mask_c = cmp_end <= t_row
    dist_c = tf_row - cmp_mid
    kcmp = kcmp_ref[0, 0][:, 0:N_HD]
    vcmpt = vcmpt_ref[0, 0][0:N_HD, :].astype(BF16)
    s_c = jnp.dot(kcmp, q4t, precision=HIGHEST, preferred_element_type=F32)
    psum = jnp.zeros((nb, QBLK), F32)
    o_c = []
    for h in range(N_HPG):
        sh = s_c[:, h * QBLK:(h + 1) * QBLK] - slopes[h] * dist_c
        sh = jnp.where(mask_c, sh, -jnp.inf)
        mx = jnp.max(sh, axis=0, keepdims=True)
        mx = jnp.where(jnp.isfinite(mx), mx, 0.0)
        e = jnp.where(mask_c, jnp.exp(sh - mx), 0.0)
        zs = jnp.sum(e, axis=0, keepdims=True)
        pc = e / jnp.where(zs > 0, zs, 1.0)
        psum = psum + pc
        o_c.append(jnp.dot(vcmpt, pc.astype(BF16), preferred_element_type=F32))

    imp = jnp.dot(ovt_ref[...], psum, precision=HIGHEST, preferred_element_type=F32)
    sidx = lax.broadcasted_iota(jnp.int32, (n_slc, QBLK), 0)
    cur = t_row // SLC_LEN
    valid = sidx <= cur
    imp = jnp.where(valid, imp, -jnp.inf)
    imp = jnp.where((sidx == 0) | (sidx == cur), jnp.inf, imp)
    rank = jnp.zeros((n_slc, QBLK), F32)
    for i in range(n_slc):
        row = imp[i:i + 1, :]
        before = (row > imp) | ((row == imp) & (sidx > i))
        rank = rank + jnp.where(before, 1.0, 0.0)
    n_top = min(SLC_TOPN, n_slc)
    sel_scr[...] = jnp.where((rank < n_top) & valid, 1.0, 0.0)

    kcol = lax.broadcasted_iota(jnp.int32, (KTILE, 1), 0)

    def attend(k_ref, v_ref, lo, hi, mask_fn):
        def body(kt, carry):
            ms, ls, os_ = carry
            kbase = pl.multiple_of(kt * KTILE, KTILE)
            k_t = k_ref[0, 0, pl.ds(kbase, KTILE), :]
            v_t = v_ref[0, 0, kt]
            st = jnp.dot(k_t, q4b, preferred_element_type=F32)
            dist = t_row - (kbase + kcol)
            mask = mask_fn(kt, dist)
            distf = dist.astype(F32)
            nm, nl, no = [], [], []
            for h in range(N_HPG):
                sh = st[:, h * QBLK:(h + 1) * QBLK] - slopes[h] * distf
                sh = jnp.where(mask, sh, NEG)
                m_new = jnp.maximum(ms[h], jnp.max(sh, axis=0, keepdims=True))
                alpha = jnp.exp(ms[h] - m_new)
                pr = jnp.where(mask, jnp.exp(sh - m_new), 0.0)
                nl.append(alpha * ls[h] + jnp.sum(pr, axis=0, keepdims=True))
                no.append(alpha * os_[h] + jnp.dot(v_t, pr.astype(BF16), preferred_element_type=F32))
                nm.append(m_new)
            return tuple(nm), tuple(nl), tuple(no)

        init = (tuple(jnp.full((1, QBLK), NEG, F32) for _ in range(N_HPG)),
                tuple(jnp.zeros((1, QBLK), F32) for _ in range(N_HPG)),
                tuple(jnp.zeros((N_HD, QBLK), F32) for _ in range(N_HPG)))
        _, ls, os_ = lax.fori_loop(lo, hi, body, init)
        return [os_[h] / jnp.where(ls[h] > 0, ls[h], 1.0) for h in range(N_HPG)]

    blocks_per_tile = KTILE // SLC_LEN

    def sel_mask(kt, dist):
        rows = [jnp.broadcast_to(sel_scr[pl.ds(kt * blocks_per_tile + r, 1), :], (SLC_LEN, QBLK))
                for r in range(blocks_per_tile)]
        return (jnp.concatenate(rows, axis=0) > 0.5) & (dist >= 0)

    def win_mask(kt, dist):
        return (dist >= 0) & (dist < WIN)

    o_s = attend(ks_ref, vs_ref, 0, (start + QBLK + KTILE - 1) // KTILE, sel_mask)
    w_lo = jnp.maximum(start - WIN, 0) // KTILE
    o_w = attend(kw_ref, vw_ref, w_lo, (start + QBLK + KTILE - 1) // KTILE, win_mask)

    tot = []
    for h in range(N_HPG):
        r0 = GATE_NG + (g * N_HPG + h) * 3
        tot.append(g_scr[pl.ds(r0, 1), :] * o_c[h] + g_scr[pl.ds(r0 + 1, 1), :] * o_s[h]
                   + g_scr[pl.ds(r0 + 2, 1), :] * o_w[h])
    o = jnp.concatenate(tot, axis=0).T
    y_ref[0] = (o * _silu(z_ref[0])).astype(BF16)


def _nsa(p, slopes, kcmp, vcmpt, ovt, ks, vs, kw, vw):
    bsz, t, _ = p.shape
    nb = kcmp.shape[2]
    n_slc = t // SLC_LEN
    nt = t // KTILE
    gw = N_HPG * N_HD
    grid_spec = pltpu.PrefetchScalarGridSpec(
        num_scalar_prefetch=1,
        grid=(bsz, N_KV, t // QBLK),
        in_specs=[pl.BlockSpec((1, QBLK, gw), lambda bi, g, i, s: (bi, i, COL_NQ // gw + g)),
                  pl.BlockSpec((1, QBLK, LANES), lambda bi, g, i, s: (bi, i, COL_GATES // LANES)),
                  pl.BlockSpec((1, QBLK, gw), lambda bi, g, i, s: (bi, i, COL_NZ // gw + g)),
                  pl.BlockSpec((1, 1, nb, LANES), lambda bi, g, i, s: (bi, g, 0, 0)),
                  pl.BlockSpec((1, 1, LANES, nb), lambda bi, g, i, s: (bi, g, 0, 0)),
                  pl.BlockSpec((n_slc, nb), lambda bi, g, i, s: (0, 0)),
                  pl.BlockSpec((1, 1, t, N_HD), lambda bi, g, i, s: (bi, g, 0, 0)),
                  pl.BlockSpec((1, 1, nt, N_HD, KTILE), lambda bi, g, i, s: (bi, g, 0, 0, 0)),
                  pl.BlockSpec((1, 1, t, N_HD), lambda bi, g, i, s: (bi, g, 0, 0)),
                  pl.BlockSpec((1, 1, nt, N_HD, KTILE), lambda bi, g, i, s: (bi, g, 0, 0, 0))],
        out_specs=pl.BlockSpec((1, QBLK, gw), lambda bi, g, i, s: (bi, i, g)),
        scratch_shapes=[pltpu.VMEM((n_slc, QBLK), F32),
                        pltpu.VMEM((LANES, QBLK), F32)],
    )
    return pl.pallas_call(
        _nsa_kernel,
        grid_spec=grid_spec,
        out_shape=jax.ShapeDtypeStruct((bsz, t, N_WIDTH), BF16),
        compiler_params=_cparams(("parallel", "parallel", "arbitrary")),
        name="nsa_attention",
    )(slopes, p, p, p, kcmp, vcmpt, ovt, ks, vs, kw, vw)


def _outproj_kernel(ym_ref, yn_ref, w_ref, x_ref, gate_ref, fg_ref, o_ref, *, final):
    y = jnp.dot(ym_ref[0], w_ref[0:M_WIDTH, :], preferred_element_type=F32)
    y = y + jnp.dot(yn_ref[0], w_ref[M_WIDTH:, :], preferred_element_type=F32)
    hres = x_ref[0] + gate_ref[0] * y
    if final:
        ms = jnp.mean(hres * hres, axis=-1, keepdims=True)
        hres = hres * lax.rsqrt(ms + EPS) * fg_ref[...]
    o_ref[0] = hres


def _outproj(ym, yn, w, x, gate, fg, final):
    bsz, t, d = x.shape
    tm = 512
    return pl.pallas_call(
        functools.partial(_outproj_kernel, final=final),
        grid=(bsz, t // tm),
        in_specs=[pl.BlockSpec((1, tm, M_WIDTH), lambda bi, i: (bi, i, 0)),
                  pl.BlockSpec((1, tm, N_WIDTH), lambda bi, i: (bi, i, 0)),
                  pl.BlockSpec((M_WIDTH + N_WIDTH, d), lambda bi, i: (0, 0)),
                  pl.BlockSpec((1, tm, d), lambda bi, i: (bi, i, 0)),
                  pl.BlockSpec((1, 1, d), lambda bi, i: (bi, 0, 0)),
                  pl.BlockSpec((1, d), lambda bi, i: (0, 0))],
        out_specs=pl.BlockSpec((1, tm, d), lambda bi, i: (bi, i, 0)),
        out_shape=jax.ShapeDtypeStruct((bsz, t, d), F32),
        compiler_params=_cparams(("parallel", "parallel")),
        name="outproj_residual",
    )(ym, yn, w, x, gate, fg)


def _reorder_cols(a):
    o_mi = 4 * M_WIDTH
    o_nq = o_mi + 2 * M_HEADS
    o_kv = o_nq + N_WIDTH
    o_ng = o_kv + 6 * KV_W
    o_nz = o_ng + 3 * N_HEADS
    parts = [a[..., 0:o_mi], a[..., o_nq:o_ng], a[..., o_nz:o_nz + N_WIDTH], a[..., o_mi:o_nq], a[..., o_ng:o_nz]]
    used = sum(x.shape[-1] for x in parts)
    parts.append(jnp.zeros(a.shape[:-1] + (NP_PAD - used,), a.dtype))
    return jnp.concatenate(parts, axis=-1)


def _overlap_t(t):
    n_cmp_rows = t // CMP_STRIDE
    n_slc = t // SLC_LEN
    c0 = np.arange(n_cmp_rows) * CMP_STRIDE
    s0 = np.arange(n_slc) * SLC_LEN
    ov = (c0[None, :] <= s0[:, None] + SLC_LEN - 1) & (c0[None, :] + CMP_LEN - 1 >= s0[:, None])
    ov[:, (t - CMP_LEN) // CMP_STRIDE + 1:] = False
    return jnp.asarray(ov, F32)


def kernel(x, c, ln_g, w_ada, b_ada, w_in, b_in, m_conv_w, m_conv_b, m_wq, m_wk, m_norm_w, m_skip, m_f_bias,
           n_pos_k, n_pos_v, n_w1_k, n_w2_k, n_w1_v, n_w2_v, w_out, final_g):
    out_dtype = x.dtype
    bsz, t, d = x.shape
    depth = ln_g.shape[0]
    h_res = x.astype(F32)
    c8 = jnp.zeros((8, d), F32).at[:bsz].set(c.astype(F32))
    slopes = jnp.asarray(np.array([2.0 ** (-8.0 * (h + 1) / N_HEADS) for h in range(N_HEADS)], np.float32))
    ovt = _overlap_t(t)

    def w1cat(w1):
        w = w1.reshape(2, CMP_STRIDE, N_HD, CMP_HIDDEN)
        return jnp.concatenate([w[0], w[1]], axis=-1).astype(BF16)

    def w2pad(w2):
        return jnp.pad(w2, ((0, 0), (0, LANES - N_HD))).astype(BF16)

    for l in range(depth):
        mod = _ada(c8, w_ada[l], b_ada[l][None, :])[:bsz]
        shift, scale, gate = mod[:, None, 0:d], mod[:, None, d:2 * d], mod[:, None, 2 * d:3 * d]
        p = _inproj(h_res, ln_g[l][None, :], scale, shift,
                    _reorder_cols(w_in[l]).astype(BF16), _reorder_cols(b_in[l])[None, :])
        fb_row = jnp.zeros((1, LANES), F32).at[0, M_HEADS:2 * M_HEADS].set(m_f_bias[l])
        y_m = _mlstm(p, m_conv_w[l], m_conv_b[l][None, :], m_wq[l].astype(BF16), m_wk[l].astype(BF16),
                     m_norm_w[l][None, :], m_skip[l][None, :], fb_row)
        kcmp, vcmpt = _compress(p, n_pos_k[l].reshape(1, -1), n_pos_v[l].reshape(1, -1),
                                n_w1_k[l].astype(BF16), w1cat(n_w1_k[l]), w2pad(n_w2_k[l]),
                                n_w1_v[l].astype(BF16), w1cat(n_w1_v[l]), w2pad(n_w2_v[l]))
        ks, vs, kw, vw = _relayout(p)
        y_n = _nsa(p, slopes, kcmp, vcmpt, ovt, ks, vs, kw, vw)
        h_res = _outproj(y_m, y_n, w_out[l].astype(BF16), h_res, gate, final_g[None, :], l == depth - 1)
    return h_res.astype(out_dtype)
```

```python
import functools

import numpy as np
import jax
import jax.numpy as jnp
from jax import lax
from jax.experimental import pallas as pl
from jax.experimental.pallas import tpu as pltpu

F32 = jnp.float32
BF16 = jnp.bfloat16
HIGHEST = lax.Precision.HIGHEST

EPS = 1e-6
M_HEADS = 4
M_HD = 256
M_WIDTH = M_HEADS * M_HD
CONV_K = 4
M_CHUNK = 256
N_HEADS = 16
N_HD = 64
N_KV = 4
N_HPG = N_HEADS // N_KV
N_WIDTH = N_HEADS * N_HD
KV_W = N_KV * N_HD
CMP_LEN = 32
CMP_STRIDE = 16
CMP_HIDDEN = 2 * N_HD
SLC_LEN = 64
SLC_TOPN = 16
WIN = 512
QBLK = 128
SKT = 256
WKT = 128
PAD_TILES = WIN // SKT
OH_W = 64
KA_SEL = 256
KA_WIN = 128
VROWS = 80
UNROLL = 4
AUX_ROWS = 16
LOG2E = 1.4426950408889634

COL_MX, COL_MV, COL_MO, COL_MZ = 0, 1024, 2048, 3072
COL_NQ = 4096
COL_KC, COL_VC, COL_KS, COL_VS, COL_KW, COL_VW = 5120, 5376, 5632, 5888, 6144, 6400
COL_NZ = 6656
COL_GATES = 7680
GATE_NG = 2 * M_HEADS
NP_PAD = 8192
LANES = 128
NEG = -1e30
VMEM_LIMIT = 56 * 1024 * 1024


def _cparams(sem):
    return pltpu.CompilerParams(dimension_semantics=sem, vmem_limit_bytes=VMEM_LIMIT)


def _silu(x):
    return x * jax.nn.sigmoid(x)


def _log_sigmoid(x):
    return jnp.minimum(x, 0.0) - jnp.log1p(jnp.exp(-jnp.abs(x)))


def _ada_kernel(c_ref, w_ref, b_ref, o_ref):
    s = _silu(c_ref[...])
    o_ref[...] = jnp.dot(s, w_ref[...], precision=HIGHEST, preferred_element_type=F32) + b_ref[...]


def _ada(c8, w, b):
    d, n = w.shape
    tn = 1024
    return pl.pallas_call(
        _ada_kernel,
        grid=(n // tn,),
        in_specs=[pl.BlockSpec((8, d), lambda j: (0, 0)),
                  pl.BlockSpec((d, tn), lambda j: (0, j)),
                  pl.BlockSpec((1, tn), lambda j: (0, j))],
        out_specs=pl.BlockSpec((8, tn), lambda j: (0, j)),
        out_shape=jax.ShapeDtypeStruct((8, n), F32),
        compiler_params=_cparams(("parallel",)),
        name="ada_mod",
    )(c8, w, b)


def _inproj_kernel(x_ref, g_ref, sc_ref, sh_ref, w_ref, b_ref, o_ref, h_ref):
    @pl.when(pl.program_id(2) == 0)
    def _():
        x = x_ref[0]
        ms = jnp.mean(x * x, axis=-1, keepdims=True)
        h = x * lax.rsqrt(ms + EPS) * g_ref[...]
        h = h * (1.0 + sc_ref[0]) + sh_ref[0]
        h_ref[...] = h.astype(BF16)

    o_ref[0] = jnp.dot(h_ref[...], w_ref[...], preferred_element_type=F32) + b_ref[...]


def _inproj(x, g, scale, shift, w, b):
    bsz, t, d = x.shape
    n = w.shape[1]
    tm, tn = 1024, 1024
    return pl.pallas_call(
        _inproj_kernel,
        grid=(bsz, t // tm, n // tn),
        in_specs=[pl.BlockSpec((1, tm, d), lambda bi, i, j: (bi, i, 0)),
                  pl.BlockSpec((1, d), lambda bi, i, j: (0, 0)),
                  pl.BlockSpec((1, 1, d), lambda bi, i, j: (bi, 0, 0)),
                  pl.BlockSpec((1, 1, d), lambda bi, i, j: (bi, 0, 0)),
                  pl.BlockSpec((d, tn), lambda bi, i, j: (0, j)),
                  pl.BlockSpec((1, tn), lambda bi, i, j: (0, j))],
        out_specs=pl.BlockSpec((1, tm, tn), lambda bi, i, j: (bi, i, j)),
        out_shape=jax.ShapeDtypeStruct((bsz, t, n), F32),
        scratch_shapes=[pltpu.VMEM((tm, d), BF16)],
        compiler_params=_cparams(("parallel", "parallel", "arbitrary")),
        name="norm_inproj",
    )(x, g, scale, shift, w, b)


def _mlstm_kernel(x_ref, v_ref, o_ref, z_ref, gt_ref, cw_ref, cb_ref, wq_ref, wk_ref, nw_ref, sk_ref, fb_ref,
                  y_ref, c_scr, n_scr, m_scr, xp_scr):
    L = M_CHUNK

    @pl.when(pl.program_id(1) == 0)
    def _():
        c_scr[...] = jnp.zeros_like(c_scr)
        n_scr[...] = jnp.zeros_like(n_scr)
        m_scr[...] = jnp.zeros_like(m_scr)
        xp_scr[...] = jnp.zeros_like(xp_scr)

    x = x_ref[0]
    prev = xp_scr[...]
    row8 = lax.broadcasted_iota(jnp.int32, (8, M_WIDTH), 0)
    cw = cw_ref[...]
    xc = cb_ref[...] + x * cw[CONV_K - 1:CONV_K, :]
    for sft in range(1, CONV_K):
        xr = pltpu.roll(x, sft, 0)
        top = jnp.where(row8 < sft, pltpu.roll(prev, sft, 0), xr[0:8])
        xs = jnp.concatenate([top, xr[8:]], axis=0)
        xc = xc + xs * cw[CONV_K - 1 - sft:CONV_K - sft, :]
    xp_scr[...] = x[L - 8:L]
    xc = _silu(xc)

    gt = gt_ref[0]
    col = lax.broadcasted_iota(jnp.int32, (L, LANES), 1)
    logf = _log_sigmoid(gt + fb_ref[...])
    a_c = jnp.where((col >= M_HEADS) & (col < 2 * M_HEADS), logf, gt)
    ri = lax.broadcasted_iota(jnp.int32, (L, L), 0)
    ci = lax.broadcasted_iota(jnp.int32, (L, L), 1)
    causal = ri >= ci
    tri = causal.astype(F32)
    tri_t = (ri <= ci).astype(F32)
    b_c = jnp.dot(tri, a_c, precision=HIGHEST, preferred_element_type=F32)
    a_r = a_c.T
    b_r = jnp.dot(a_r[0:8], tri_t, precision=HIGHEST, preferred_element_type=F32)

    for h in range(M_HEADS):
        sl = slice(h * M_HD, (h + 1) * M_HD)
        xh = xc[:, sl]
        xb = xh.astype(BF16)
        q = jnp.dot(xb, wq_ref[h], preferred_element_type=F32)
        k = jnp.dot(xb, wk_ref[h], preferred_element_type=F32) * (M_HD ** -0.5)
        vb = v_ref[0, :, sl].astype(BF16)
        qb = q.astype(BF16)
        kb = k.astype(BF16)

        bt = b_c[:, M_HEADS + h:M_HEADS + h + 1]
        ic = a_c[:, h:h + 1]
        bs = b_r[M_HEADS + h:M_HEADS + h + 1, :]
        ir = a_r[h:h + 1, :]
        m_prev = m_scr[h][:, 0:1]

        dm = jnp.where(causal, bt - bs + ir, -jnp.inf)
        inter = bt + m_prev
        m_t = jnp.maximum(inter, jnp.max(dm, axis=-1, keepdims=True))
        w_in = jnp.exp(dm - m_t)
        w_st = jnp.exp(inter - m_t)
        s = lax.dot_general(qb, kb, (((1,), (1,)), ((), ())), preferred_element_type=F32) * w_in
        cmat = c_scr[h]
        nvec = n_scr[h]
        num = w_st * jnp.dot(qb, cmat.astype(BF16), preferred_element_type=F32) \
            + jnp.dot(s.astype(BF16), vb, preferred_element_type=F32)
        den = w_st * jnp.sum(q * nvec, axis=-1, keepdims=True) + jnp.sum(s, axis=-1, keepdims=True)
        hh = num / jnp.maximum(jnp.abs(den), jnp.exp(-m_t))

        b_last = bt[L - 1:L, :]
        w_end = b_last - bt + ic
        m_new = jnp.maximum(b_last + m_prev, jnp.max(w_end, axis=0, keepdims=True))
        decay = jnp.exp(b_last + m_prev - m_new)
        kwt = k * jnp.exp(w_end - m_new)
        c_scr[h] = decay * cmat + lax.dot_general(kwt.astype(BF16), vb, (((0,), (0,)), ((), ())),
                                                  preferred_element_type=F32)
        n_scr[h] = decay * nvec + jnp.sum(kwt, axis=0, keepdims=True)
        m_scr[h] = jnp.broadcast_to(m_new, (1, LANES))

        mu = jnp.mean(hh, axis=-1, keepdims=True)
        hc = hh - mu
        var = jnp.mean(hc * hc, axis=-1, keepdims=True)
        hn = hc * lax.rsqrt(var + EPS) * nw_ref[:, sl]
        out = jax.nn.sigmoid(o_ref[0, :, sl]) * hn + sk_ref[:, sl] * xh
        y_ref[0, :, sl] = (out * _silu(z_ref[0, :, sl])).astype(BF16)


def _mlstm(p, conv_w, conv_b, wq, wk, norm_w, skip, fb_row):
    bsz, t, _ = p.shape
    L = M_CHUNK
    cb = lambda c: (lambda bi, i: (bi, i, c))
    full2 = lambda bi, i: (0, 0)
    full3 = lambda bi, i: (0, 0, 0)
    return pl.pallas_call(
        _mlstm_kernel,
        grid=(bsz, t // L),
        in_specs=[pl.BlockSpec((1, L, M_WIDTH), cb(COL_MX // M_WIDTH)),
                  pl.BlockSpec((1, L, M_WIDTH), cb(COL_MV // M_WIDTH)),
                  pl.BlockSpec((1, L, M_WIDTH), cb(COL_MO // M_WIDTH)),
                  pl.BlockSpec((1, L, M_WIDTH), cb(COL_MZ // M_WIDTH)),
                  pl.BlockSpec((1, L, LANES), cb(COL_GATES // LANES)),
                  pl.BlockSpec((CONV_K, M_WIDTH), full2),
                  pl.BlockSpec((1, M_WIDTH), full2),
                  pl.BlockSpec((M_HEADS, M_HD, M_HD), full3),
                  pl.BlockSpec((M_HEADS, M_HD, M_HD), full3),
                  pl.BlockSpec((1, M_WIDTH), full2),
                  pl.BlockSpec((1, M_WIDTH), full2),
                  pl.BlockSpec((1, LANES), full2)],
        out_specs=pl.BlockSpec((1, L, M_WIDTH), lambda bi, i: (bi, i, 0)),
        out_shape=jax.ShapeDtypeStruct((bsz, t, M_WIDTH), BF16),
        scratch_shapes=[pltpu.VMEM((M_HEADS, M_HD, M_HD), F32),
                        pltpu.VMEM((M_HEADS, 1, M_HD), F32),
                        pltpu.VMEM((M_HEADS, 1, LANES), F32),
                        pltpu.VMEM((8, M_WIDTH), F32)],
        compiler_params=_cparams(("parallel", "arbitrary")),
        name="mlstm_group",
    )(p, p, p, p, p, conv_w, conv_b, wq, wk, norm_w, skip, fb_row)


def _compress_kernel(kc0_ref, kc1_ref, vc0_ref, vc1_ref, posk_ref, posv_ref, w1k_ref, w1kc_ref, w2k_ref,
                     w1v_ref, w1vc_ref, w2v_ref, kcmp_ref, vcmpt_ref):
    nb = kcmp_ref.shape[2]

    def hidden(src_refs, pos_ref, w1_ref, w1c_ref):
        acc = jnp.zeros((N_KV * nb, 2 * CMP_HIDDEN), F32)
        for l in range(CMP_STRIDE):
            xl = [r[0, pl.ds(l, nb, stride=CMP_STRIDE), :] for r in src_refs]
            xs = jnp.concatenate([x[:, g * N_HD:(g + 1) * N_HD] for x in xl for g in range(LANES // N_HD)], axis=0)
            acc = acc + jnp.dot(xs.astype(BF16), w1c_ref[l], preferred_element_type=F32)
        first = acc[:, :CMP_HIDDEN]
        second = acc[:, CMP_HIDDEN:]
        posb = jnp.dot(jnp.broadcast_to(pos_ref[...], (8, CMP_LEN * N_HD)).astype(BF16), w1_ref[...],
                       preferred_element_type=F32)[0:1]
        hid = first + pltpu.roll(second, N_KV * nb - 1, 0) + posb
        return jax.nn.gelu(hid).astype(BF16)

    hk = hidden((kc0_ref, kc1_ref), posk_ref, w1k_ref, w1kc_ref)
    kc = jnp.dot(hk, w2k_ref[...], preferred_element_type=F32)
    hv = hidden((vc0_ref, vc1_ref), posv_ref, w1v_ref, w1vc_ref)
    vc = jnp.dot(hv, w2v_ref[...], preferred_element_type=F32)
    j = lax.broadcasted_iota(jnp.int32, (nb, 1), 0)
    lane = lax.broadcasted_iota(jnp.int32, (nb, N_HD), 1)
    n_real = (nb * CMP_STRIDE - CMP_LEN) // CMP_STRIDE + 1
    aux = jnp.where(lane < 3, (j * CMP_STRIDE).astype(F32),
                    jnp.where(lane < 6, (CMP_LEN - 1) * 0.5,
                              jnp.where((lane == 6) & (j >= n_real), 1.0, 0.0))).astype(BF16)
    ones_blk = jnp.where(lax.broadcasted_iota(jnp.int32, (VROWS - N_HD, nb), 0) == 0, 1.0, 0.0)
    for g in range(N_KV):
        kg = kc[g * nb:(g + 1) * nb, 0:N_HD]
        k_hi = kg.astype(BF16)
        k_lo = (kg - k_hi.astype(F32)).astype(BF16)
        kcmp_ref[0, g] = jnp.concatenate([k_hi, k_lo, k_hi, aux], axis=1)
        vt = vc[g * nb:(g + 1) * nb].T
        vcmpt_ref[0, g] = jnp.concatenate([vt[0:N_HD], ones_blk], axis=0).astype(BF16)


def _compress(p, posk, posv, w1k, w1kc, w2k, w1v, w1vc, w2v):
    bsz, t, _ = p.shape
    nb = t // CMP_STRIDE
    cb = lambda c: (lambda bi: (bi, 0, c))
    f2 = lambda bi: (0, 0)
    f3 = lambda bi: (0, 0, 0)
    wspecs = [pl.BlockSpec((CMP_LEN * N_HD, CMP_HIDDEN), f2),
              pl.BlockSpec((CMP_STRIDE, N_HD, 2 * CMP_HIDDEN), f3),
              pl.BlockSpec((CMP_HIDDEN, LANES), f2)]
    return pl.pallas_call(
        _compress_kernel,
        grid=(bsz,),
        in_specs=[pl.BlockSpec((1, t, LANES), cb(COL_KC // LANES)),
                  pl.BlockSpec((1, t, LANES), cb(COL_KC // LANES + 1)),
                  pl.BlockSpec((1, t, LANES), cb(COL_VC // LANES)),
                  pl.BlockSpec((1, t, LANES), cb(COL_VC // LANES + 1)),
                  pl.BlockSpec((1, CMP_LEN * N_HD), f2),
                  pl.BlockSpec((1, CMP_LEN * N_HD), f2)] + wspecs + wspecs,
        out_specs=[pl.BlockSpec((1, N_KV, nb, 4 * N_HD), lambda bi: (bi, 0, 0, 0)),
                   pl.BlockSpec((1, N_KV, VROWS, nb), lambda bi: (bi, 0, 0, 0))],
        out_shape=[jax.ShapeDtypeStruct((bsz, N_KV, nb, 4 * N_HD), BF16),
                   jax.ShapeDtypeStruct((bsz, N_KV, VROWS, nb), BF16)],
        compiler_params=_cparams(("parallel",)),
        name="nsa_compress",
    )(p, p, p, p, posk, posv, w1k, w1kc, w2k, w1v, w1vc, w2v)


def _relayout_kernel(ks_ref, vs_ref, kw_ref, vw_ref, ksa_ref, vsa_ref, kwa_ref, vwa_ref):
    i = pl.program_id(1)
    is_pad = i < PAD_TILES
    base = (i - PAD_TILES) * SKT
    row = lax.broadcasted_iota(jnp.int32, (SKT, 1), 0)
    pos = jnp.where(is_pad, 0, base + row)
    blk = pos // SLC_LEN
    p_hi = (blk * SLC_LEN).astype(F32)
    p_lo = (pos - blk * SLC_LEN).astype(F32)
    flag = jnp.where(is_pad, 1.0, 0.0)
    lane = lax.broadcasted_iota(jnp.int32, (SKT, LANES), 1)
    aux = jnp.where(lane < 3, p_hi, jnp.where(lane < 6, p_lo, jnp.where(lane == 6, flag, 0.0)))
    onehot = jnp.where(blk == lax.broadcasted_iota(jnp.int32, (SKT, OH_W), 1), 1.0, 0.0)
    ks = ks_ref[0]
    kw = kw_ref[0]
    for g in range(N_KV):
        sl = slice(g * N_HD, (g + 1) * N_HD)
        ksa_ref[0, g] = jnp.concatenate([ks[:, sl], onehot, aux], axis=1).astype(BF16)
        kwa_ref[0, g] = jnp.concatenate([kw[:, sl], aux[:, 0:N_HD]], axis=1).astype(BF16)
    ones_blk = jnp.where(lax.broadcasted_iota(jnp.int32, (VROWS - N_HD, SKT), 0) == 0, 1.0, 0.0)
    vst = vs_ref[0].T
    vwt = vw_ref[0].T
    for g in range(N_KV):
        sl = slice(g * N_HD, (g + 1) * N_HD)
        vsa_ref[0, g, 0] = jnp.concatenate([vst[sl], ones_blk], axis=0).astype(BF16)
        vwg = jnp.concatenate([vwt[sl], ones_blk], axis=0).astype(BF16)
        for j in range(SKT // WKT):
            vwa_ref[0, g, j] = vwg[:, j * WKT:(j + 1) * WKT]


def _relayout(p):
    bsz, t, _ = p.shape
    nt = t // SKT + PAD_TILES
    wpt = SKT // WKT
    cb = lambda c: (lambda bi, i: (bi, jnp.maximum(i - PAD_TILES, 0), c))
    return pl.pallas_call(
        _relayout_kernel,
        grid=(bsz, nt),
        in_specs=[pl.BlockSpec((1, SKT, KV_W), cb(COL_KS // KV_W)),
                  pl.BlockSpec((1, SKT, KV_W), cb(COL_VS // KV_W)),
                  pl.BlockSpec((1, SKT, KV_W), cb(COL_KW // KV_W)),
                  pl.BlockSpec((1, SKT, KV_W), cb(COL_VW // KV_W))],
        out_specs=[pl.BlockSpec((1, N_KV, SKT, KA_SEL), lambda bi, i: (bi, 0, i, 0)),
                   pl.BlockSpec((1, N_KV, 1, VROWS, SKT), lambda bi, i: (bi, 0, i, 0, 0)),
                   pl.BlockSpec((1, N_KV, SKT, KA_WIN), lambda bi, i: (bi, 0, i, 0)),
                   pl.BlockSpec((1, N_KV, wpt, VROWS, WKT), lambda bi, i: (bi, 0, i, 0, 0))],
        out_shape=[jax.ShapeDtypeStruct((bsz, N_KV, nt * SKT, KA_SEL), BF16),
                   jax.ShapeDtypeStruct((bsz, N_KV, nt, VROWS, SKT), BF16),
                   jax.ShapeDtypeStruct((bsz, N_KV, nt * SKT, KA_WIN), BF16),
                   jax.ShapeDtypeStruct((bsz, N_KV, nt * wpt, VROWS, WKT), BF16)],
        compiler_params=_cparams(("parallel", "parallel")),
        name="nsa_relayout",
    )(p, p, p, p)


def _nsa_kernel(sp_ref, q_ref, gt_ref, z_ref, kcmp_ref, vcmpt_ref, ovt_ref, ks_ref, vs_ref, kw_ref,
                vw_ref, tri_ref, wb_ref, y_ref, s_scr, g_scr, imp_scr, rank_scr):
    g = pl.program_id(1)
    qb = pl.program_id(2)
    start = qb * QBLK
    nb = kcmp_ref.shape[2]
    n_slc = ovt_ref.shape[0]

    qt = (q_ref[0] * (N_HD ** -0.5)).T
    q4t = jnp.concatenate([qt[h * N_HD:(h + 1) * N_HD] for h in range(N_HPG)], axis=1)
    t_row = start + lax.broadcasted_iota(jnp.int32, (1, QBLK), 1)

    g_scr[...] = jax.nn.sigmoid(gt_ref[0]).T

    nw = N_HPG * QBLK
    hl = lax.broadcasted_iota(jnp.int32, (AUX_ROWS, nw), 1) // QBLK
    ar = lax.broadcasted_iota(jnp.int32, (AUX_ROWS, nw), 0)

    def per_head(vals):
        out = jnp.full((AUX_ROWS, nw), vals[N_HPG - 1], F32)
        for h in range(N_HPG - 2, -1, -1):
            out = jnp.where(hl == h, vals[h], out)
        return out

    pieces = [per_head([sp_ref[(g * N_HPG + h) * 3 + j] for h in range(N_HPG)]) for j in range(3)]
    aux = jnp.where(ar == 6, NEG, 0.0)
    for j in range(3):
        aux = jnp.where((ar == j) | (ar == j + 3), pieces[j], aux)
    aux_b = aux.astype(BF16)
    q4l = q4t * LOG2E
    q4s = q4l.astype(BF16)
    q4lo = (q4l - q4s.astype(F32)).astype(BF16)

    def tile4(a):
        return jnp.concatenate([a] * N_HPG, axis=1)

    def col_reduce8(s, op):
        out = s[0:8]
        for r in range(1, s.shape[0] // 8):
            out = op(out, s[8 * r:8 * r + 8])
        return out

    def col_max(s):
        return jnp.max(col_reduce8(s, jnp.maximum), axis=0, keepdims=True)

    def normalise(acc):
        den = acc[N_HD:N_HD + 1]
        return acc[0:N_HD] / jnp.where(den > 0, den, 1.0)

    q_cmp = jnp.concatenate([q4s, q4s, q4lo, aux_b, jnp.zeros((N_HD - AUX_ROWS, nw), BF16)], axis=0)
    cmp_end = lax.broadcasted_iota(jnp.int32, (nb, 1), 0) * CMP_STRIDE + (CMP_LEN - 1)
    s_c = jnp.dot(kcmp_ref[0, 0], q_cmp, preferred_element_type=F32)
    s_c = s_c + tile4(jnp.where(cmp_end <= t_row, 0.0, NEG))
    e_c = jnp.exp2(s_c - col_max(s_c))
    z_c = jnp.sum(col_reduce8(e_c, jnp.add), axis=0, keepdims=True)
    inv_c = jnp.where(tile4(t_row >= CMP_LEN - 1) & (z_c > 0), 1.0 / z_c, 0.0)
    o_c = jnp.dot(vcmpt_ref[0, 0], e_c.astype(BF16), preferred_element_type=F32)[0:N_HD] * inv_c
    p_c = e_c * inv_c
    psum = p_c[:, 0:QBLK]
    for h in range(1, N_HPG):
        psum = psum + p_c[:, h * QBLK:(h + 1) * QBLK]
    parts, rest = [], psum
    for _ in range(3):
        parts.append(rest.astype(BF16))
        rest = rest - parts[-1].astype(F32)
    imp = jnp.dot(ovt_ref[...], jnp.concatenate(parts, axis=0), preferred_element_type=F32)
    sidx = lax.broadcasted_iota(jnp.int32, (n_slc, QBLK), 0)
    cur = t_row // SLC_LEN
    valid = sidx <= cur
    imp = jnp.where(valid, imp, -jnp.inf)
    imp_scr[...] = jnp.where((sidx == 0) | (sidx == cur), jnp.inf, imp)
    n_top = min(SLC_TOPN, n_slc)
    rank_scr[...] = jnp.zeros_like(rank_scr)
    sub8 = lax.broadcasted_iota(jnp.int32, (8, QBLK), 0)
    last_blk = (start + QBLK - 1) // SLC_LEN
    for ri in range(n_slc // 8):
        @pl.when((last_blk >= n_top) & (8 * ri <= last_blk))
        def _():
            rows = imp_scr[8 * ri:8 * ri + 8]
            for r in range(n_slc // 8):
                blk8 = imp_scr[8 * r:8 * r + 8]
                acc = rank_scr[8 * r:8 * r + 8]
                for ii in range(8):
                    row = rows[ii:ii + 1]
                    if ri < r:
                        before = row >= blk8
                    elif ri > r:
                        before = row > blk8
                    else:
                        before = (row > blk8) | ((row == blk8) & (sub8 > ii))
                    acc = acc + jnp.where(before, 1.0, 0.0)
                rank_scr[8 * r:8 * r + 8] = acc
    selbias = jnp.where((rank_scr[...] < n_top) & valid, 0.0, NEG)

    selb4 = tile4(selbias.astype(BF16))
    oh_pad = [jnp.zeros((OH_W - n_slc, nw), BF16)] if n_slc < OH_W else []
    q_sel = jnp.concatenate([q4s, selb4] + oh_pad + [aux_b, jnp.zeros((KA_SEL - N_HD - OH_W - AUX_ROWS, nw), BF16)],
                            axis=0)
    n_tiles = qb // (SKT // QBLK) + 1

    def sel_scores(kt):
        krow = pl.multiple_of((kt + PAD_TILES) * SKT, SKT)
        return jnp.dot(ks_ref[0, 0, pl.ds(krow, SKT), :], q_sel, preferred_element_type=F32)

    def pass1(kt, mrun):
        s = sel_scores(kt)
        s_scr[pl.ds(pl.multiple_of(kt * SKT, SKT), SKT), :] = s
        return jnp.maximum(mrun, col_reduce8(s, jnp.maximum))

    def unrolled(fn):
        def body(i, carry):
            for u in range(UNROLL):
                carry = fn(i * UNROLL + u, carry)
            return carry
        return body

    def run_tiles(fn, n, init):
        n_u = n // UNROLL
        carry = lax.fori_loop(0, n_u, unrolled(fn), init)
        return lax.fori_loop(n_u * UNROLL, n, fn, carry)

    mrun = run_tiles(pass1, n_tiles - 1, jnp.full((8, nw), NEG, F32))
    last = n_tiles - 1
    s_last = sel_scores(last) + tile4(tri_ref[qb % (SKT // QBLK)])
    s_scr[pl.ds(pl.multiple_of(last * SKT, SKT), SKT), :] = s_last
    m_sel = jnp.max(jnp.maximum(mrun, col_reduce8(s_last, jnp.maximum)), axis=0, keepdims=True)

    def pass2(kt, acc):
        s = s_scr[pl.ds(pl.multiple_of(kt * SKT, SKT), SKT), :]
        pr = jnp.exp2(s - m_sel).astype(BF16)
        return acc + jnp.dot(vs_ref[0, 0, kt + PAD_TILES], pr, preferred_element_type=F32)

    o_s = normalise(run_tiles(pass2, n_tiles, jnp.zeros((VROWS, nw), F32)))

    wrows = WIN + QBLK
    q_win = jnp.concatenate([q4s, aux_b, jnp.zeros((KA_WIN - N_HD - AUX_ROWS, nw), BF16)], axis=0)
    s_w = jnp.dot(kw_ref[0, 0, pl.ds(pl.multiple_of(start, QBLK), wrows), :], q_win,
                  preferred_element_type=F32)
    s_w = jnp.concatenate([s_w[0:WKT] + tile4(wb_ref[0]), s_w[WKT:wrows - WKT],
                           s_w[wrows - WKT:] + tile4(wb_ref[1])], axis=0)
    p_w = jnp.exp2(s_w - col_max(s_w)).astype(BF16)
    v_w = jnp.concatenate([vw_ref[0, 0, qb + j] for j in range(wrows // WKT)], axis=1)
    o_w = normalise(jnp.dot(v_w, p_w, preferred_element_type=F32))

    tot = []
    for h in range(N_HPG):
        r0 = GATE_NG + (g * N_HPG + h) * 3
        hs = slice(h * QBLK, (h + 1) * QBLK)
        tot.append(g_scr[pl.ds(r0, 1), :] * o_c[:, hs] + g_scr[pl.ds(r0 + 1, 1), :] * o_s[:, hs]
                   + g_scr[pl.ds(r0 + 2, 1), :] * o_w[:, hs])
    o = jnp.concatenate(tot, axis=0).T
    y_ref[0] = (o * _silu(z_ref[0])).astype(BF16)


def _edge_biases():
    kl = np.arange(SKT)[:, None]
    ql = np.arange(QBLK)[None, :]
    tri = np.stack([np.where(kl <= par * QBLK + ql, 0.0, NEG) for par in range(SKT // QBLK)])
    kk = np.arange(WKT)[:, None]
    wb = np.stack([np.where(kk > ql, 0.0, NEG), np.where(kk <= ql, 0.0, NEG)])
    return jnp.asarray(tri, F32), jnp.asarray(wb, F32)


def _nsa(p, spieces, kcmp, vcmpt, ovt, ks, vs, kw, vw):
    bsz, t, _ = p.shape
    n_slc = t // SLC_LEN
    gw = N_HPG * N_HD
    tri, wb = _edge_biases()
    per_bg = lambda a: pl.BlockSpec((1, 1) + a.shape[2:], lambda bi, g, i, sp: (bi, g) + (0,) * (a.ndim - 2))
    const = lambda a: pl.BlockSpec(a.shape, lambda bi, g, i, sp: (0,) * a.ndim)
    grid_spec = pltpu.PrefetchScalarGridSpec(
        num_scalar_prefetch=1,
        grid=(bsz, N_KV, t // QBLK),
        in_specs=[pl.BlockSpec((1, QBLK, gw), lambda bi, g, i, sp: (bi, i, COL_NQ // gw + g)),
                  pl.BlockSpec((1, QBLK, LANES), lambda bi, g, i, sp: (bi, i, COL_GATES // LANES)),
                  pl.BlockSpec((1, QBLK, gw), lambda bi, g, i, sp: (bi, i, COL_NZ // gw + g)),
                  per_bg(kcmp), per_bg(vcmpt), const(ovt),
                  per_bg(ks), per_bg(vs), per_bg(kw), per_bg(vw), const(tri), const(wb)],
        out_specs=pl.BlockSpec((1, QBLK, gw), lambda bi, g, i, sp: (bi, i, g)),
        scratch_shapes=[pltpu.VMEM((t, N_HPG * QBLK), F32),
                        pltpu.VMEM((LANES, QBLK), F32),
                        pltpu.VMEM((n_slc, QBLK), F32),
                        pltpu.VMEM((n_slc, QBLK), F32)],
    )
    return pl.pallas_call(
        _nsa_kernel,
        grid_spec=grid_spec,
        out_shape=jax.ShapeDtypeStruct((bsz, t, N_WIDTH), BF16),
        compiler_params=_cparams(("parallel", "parallel", "arbitrary")),
        name="nsa_attention",
    )(spieces, p, p, p, kcmp, vcmpt, ovt, ks, vs, kw, vw, tri, wb)


def _outproj_kernel(ym_ref, yn_ref, w_ref, x_ref, gate_ref, fg_ref, o_ref, *, final):
    y = jnp.dot(ym_ref[0], w_ref[0:M_WIDTH, :], preferred_element_type=F32)
    y = y + jnp.dot(yn_ref[0], w_ref[M_WIDTH:, :], preferred_element_type=F32)
    hres = x_ref[0] + gate_ref[0] * y
    if final:
        ms = jnp.mean(hres * hres, axis=-1, keepdims=True)
        hres = hres * lax.rsqrt(ms + EPS) * fg_ref[...]
    o_ref[0] = hres


def _outproj(ym, yn, w, x, gate, fg, final):
    bsz, t, d = x.shape
    tm = 512
    return pl.pallas_call(
        functools.partial(_outproj_kernel, final=final),
        grid=(bsz, t // tm),
        in_specs=[pl.BlockSpec((1, tm, M_WIDTH), lambda bi, i: (bi, i, 0)),
                  pl.BlockSpec((1, tm, N_WIDTH), lambda bi, i: (bi, i, 0)),
                  pl.BlockSpec((M_WIDTH + N_WIDTH, d), lambda bi, i: (0, 0)),
                  pl.BlockSpec((1, tm, d), lambda bi, i: (bi, i, 0)),
                  pl.BlockSpec((1, 1, d), lambda bi, i: (bi, 0, 0)),
                  pl.BlockSpec((1, d), lambda bi, i: (0, 0))],
        out_specs=pl.BlockSpec((1, tm, d), lambda bi, i: (bi, i, 0)),
        out_shape=jax.ShapeDtypeStruct((bsz, t, d), F32),
        compiler_params=_cparams(("parallel", "parallel")),
        name="outproj_residual",
    )(ym, yn, w, x, gate, fg)


def _reorder_cols(a):
    o_mi = 4 * M_WIDTH
    o_nq = o_mi + 2 * M_HEADS
    o_kv = o_nq + N_WIDTH
    o_ng = o_kv + 6 * KV_W
    o_nz = o_ng + 3 * N_HEADS
    parts = [a[..., 0:o_mi], a[..., o_nq:o_ng], a[..., o_nz:o_nz + N_WIDTH], a[..., o_mi:o_nq], a[..., o_ng:o_nz]]
    used = sum(x.shape[-1] for x in parts)
    parts.append(jnp.zeros(a.shape[:-1] + (NP_PAD - used,), a.dtype))
    return jnp.concatenate(parts, axis=-1)


def _overlap_t(t):
    n_cmp_rows = t // CMP_STRIDE
    n_slc = t // SLC_LEN
    c0 = np.arange(n_cmp_rows) * CMP_STRIDE
    s0 = np.arange(n_slc) * SLC_LEN
    ov = (c0[None, :] <= s0[:, None] + SLC_LEN - 1) & (c0[None, :] + CMP_LEN - 1 >= s0[:, None])
    ov[:, (t - CMP_LEN) // CMP_STRIDE + 1:] = False
    return jnp.asarray(np.concatenate([ov] * 3, axis=1), BF16)


def kernel(x, c, ln_g, w_ada, b_ada, w_in, b_in, m_conv_w, m_conv_b, m_wq, m_wk, m_norm_w, m_skip, m_f_bias,
           n_pos_k, n_pos_v, n_w1_k, n_w2_k, n_w1_v, n_w2_v, w_out, final_g):
    out_dtype = x.dtype
    bsz, t, d = x.shape
    depth = ln_g.shape[0]
    h_res = x.astype(F32)
    c8 = jnp.zeros((8, d), F32).at[:bsz].set(c.astype(F32))
    slopes_np = np.array([2.0 ** (-8.0 * (h + 1) / N_HEADS) for h in range(N_HEADS)], np.float32)
    rest = (slopes_np.astype(np.float64) * LOG2E).astype(np.float32)
    pieces = []
    for _ in range(3):
        pieces.append(rest.astype(jnp.bfloat16).astype(np.float32))
        rest = rest - pieces[-1]
    spieces = jnp.asarray(np.stack(pieces, axis=1).reshape(-1))
    ovt = _overlap_t(t)

    def w1cat(w1):
        w = w1.reshape(2, CMP_STRIDE, N_HD, CMP_HIDDEN)
        return jnp.concatenate([w[0], w[1]], axis=-1).astype(BF16)

    def w2pad(w2):
        return jnp.pad(w2, ((0, 0), (0, LANES - N_HD))).astype(BF16)

    for l in range(depth):
        mod = _ada(c8, w_ada[l], b_ada[l][None, :])[:bsz]
        shift, scale, gate = mod[:, None, 0:d], mod[:, None, d:2 * d], mod[:, None, 2 * d:3 * d]
        p = _inproj(h_res, ln_g[l][None, :], scale, shift,
                    _reorder_cols(w_in[l]).astype(BF16), _reorder_cols(b_in[l])[None, :])
        fb_row = jnp.zeros((1, LANES), F32).at[0, M_HEADS:2 * M_HEADS].set(m_f_bias[l])
        y_m = _mlstm(p, m_conv_w[l], m_conv_b[l][None, :], m_wq[l].astype(BF16), m_wk[l].astype(BF16),
                     m_norm_w[l][None, :], m_skip[l][None, :], fb_row)
        kcmp, vcmpt = _compress(p, n_pos_k[l].reshape(1, -1), n_pos_v[l].reshape(1, -1),
                                n_w1_k[l].astype(BF16), w1cat(n_w1_k[l]), w2pad(n_w2_k[l]),
                                n_w1_v[l].astype(BF16), w1cat(n_w1_v[l]), w2pad(n_w2_v[l]))
        ks, vs, kw, vw = _relayout(p)
        y_n = _nsa(p, spieces, kcmp, vcmpt, ovt, ks, vs, kw, vw)
        h_res = _outproj(y_m, y_n, w_out[l].astype(BF16), h_res, gate, final_g[None, :], l == depth - 1)
    return h_res.astype(out_dtype)
```

```python
import functools

import numpy as np
import jax
import jax.numpy as jnp
from jax import lax
from jax.experimental import pallas as pl
from jax.experimental.pallas import tpu as pltpu

F32 = jnp.float32
BF16 = jnp.bfloat16
HIGHEST = lax.Precision.HIGHEST

EPS = 1e-6
M_HEADS = 4
M_HD = 256
M_WIDTH = M_HEADS * M_HD
CONV_K = 4
M_CHUNK = 256
N_HEADS = 16
N_HD = 64
N_KV = 4
N_HPG = N_HEADS // N_KV
N_WIDTH = N_HEADS * N_HD
KV_W = N_KV * N_HD
CMP_LEN = 32
CMP_STRIDE = 16
CMP_HIDDEN = 2 * N_HD
SLC_LEN = 64
SLC_TOPN = 16
WIN = 512
QBLK = 256
SKT = 256
WKT = 256
PAD_TILES = WIN // SKT
OH_W = 64
KA_SEL = 256
KA_WIN = 128
VROWS = 80
UNROLL = 4
AUX_ROWS = 16
LOG2E = 1.4426950408889634

COL_MX, COL_MV, COL_MO, COL_MZ = 0, 1024, 2048, 3072
COL_NQ = 4096
COL_KC, COL_VC, COL_KS, COL_VS, COL_KW, COL_VW = 5120, 5376, 5632, 5888, 6144, 6400
COL_NZ = 6656
COL_GATES = 7680
GATE_NG = 2 * M_HEADS
NP_PAD = 8192
LANES = 128
NEG = -1e30
VMEM_LIMIT = 56 * 1024 * 1024


def _cparams(sem):
    return pltpu.CompilerParams(dimension_semantics=sem, vmem_limit_bytes=VMEM_LIMIT)


def _silu(x):
    return x * jax.nn.sigmoid(x)


def _log_sigmoid(x):
    return jnp.minimum(x, 0.0) - jnp.log1p(jnp.exp(-jnp.abs(x)))


def _ada_kernel(c_ref, w_ref, b_ref, o_ref):
    s = _silu(c_ref[...])
    o_ref[...] = jnp.dot(s, w_ref[...], precision=HIGHEST, preferred_element_type=F32) + b_ref[...]


def _ada(c8, w, b):
    d, n = w.shape
    tn = 1024
    return pl.pallas_call(
        _ada_kernel,
        grid=(n // tn,),
        in_specs=[pl.BlockSpec((8, d), lambda j: (0, 0)),
                  pl.BlockSpec((d, tn), lambda j: (0, j)),
                  pl.BlockSpec((1, tn), lambda j: (0, j))],
        out_specs=pl.BlockSpec((8, tn), lambda j: (0, j)),
        out_shape=jax.ShapeDtypeStruct((8, n), F32),
        compiler_params=_cparams(("parallel",)),
        name="ada_mod",
    )(c8, w, b)


def _inproj_kernel(x_ref, g_ref, sc_ref, sh_ref, w_ref, b_ref, o_ref, h_ref):
    @pl.when(pl.program_id(2) == 0)
    def _():
        x = x_ref[0]
        ms = jnp.mean(x * x, axis=-1, keepdims=True)
        h = x * lax.rsqrt(ms + EPS) * g_ref[...]
        h = h * (1.0 + sc_ref[0]) + sh_ref[0]
        h_ref[...] = h.astype(BF16)

    o_ref[0] = jnp.dot(h_ref[...], w_ref[...], preferred_element_type=F32) + b_ref[...]


def _inproj(x, g, scale, shift, w, b):
    bsz, t, d = x.shape
    n = w.shape[1]
    tm, tn = 1024, 1024
    return pl.pallas_call(
        _inproj_kernel,
        grid=(bsz, t // tm, n // tn),
        in_specs=[pl.BlockSpec((1, tm, d), lambda bi, i, j: (bi, i, 0)),
                  pl.BlockSpec((1, d), lambda bi, i, j: (0, 0)),
                  pl.BlockSpec((1, 1, d), lambda bi, i, j: (bi, 0, 0)),
                  pl.BlockSpec((1, 1, d), lambda bi, i, j: (bi, 0, 0)),
                  pl.BlockSpec((d, tn), lambda bi, i, j: (0, j)),
                  pl.BlockSpec((1, tn), lambda bi, i, j: (0, j))],
        out_specs=pl.BlockSpec((1, tm, tn), lambda bi, i, j: (bi, i, j)),
        out_shape=jax.ShapeDtypeStruct((bsz, t, n), F32),
        scratch_shapes=[pltpu.VMEM((tm, d), BF16)],
        compiler_params=_cparams(("parallel", "parallel", "arbitrary")),
        name="norm_inproj",
    )(x, g, scale, shift, w, b)


def _mlstm_kernel(x_ref, v_ref, o_ref, z_ref, gt_ref, cw_ref, cb_ref, wq_ref, wk_ref, nw_ref, sk_ref, fb_ref,
                  y_ref, c_scr, n_scr, m_scr, xp_scr):
    L = M_CHUNK

    @pl.when(pl.program_id(1) == 0)
    def _():
        c_scr[...] = jnp.zeros_like(c_scr)
        n_scr[...] = jnp.zeros_like(n_scr)
        m_scr[...] = jnp.zeros_like(m_scr)
        xp_scr[...] = jnp.zeros_like(xp_scr)

    x = x_ref[0]
    prev = xp_scr[...]
    row8 = lax.broadcasted_iota(jnp.int32, (8, M_WIDTH), 0)
    cw = cw_ref[...]
    xc = cb_ref[...] + x * cw[CONV_K - 1:CONV_K, :]
    for sft in range(1, CONV_K):
        xr = pltpu.roll(x, sft, 0)
        top = jnp.where(row8 < sft, pltpu.roll(prev, sft, 0), xr[0:8])
        xs = jnp.concatenate([top, xr[8:]], axis=0)
        xc = xc + xs * cw[CONV_K - 1 - sft:CONV_K - sft, :]
    xp_scr[...] = x[L - 8:L]
    xc = _silu(xc)

    gt = gt_ref[0]
    col = lax.broadcasted_iota(jnp.int32, (L, LANES), 1)
    logf = _log_sigmoid(gt + fb_ref[...])
    a_c = jnp.where((col >= M_HEADS) & (col < 2 * M_HEADS), logf, gt)
    ri = lax.broadcasted_iota(jnp.int32, (L, L), 0)
    ci = lax.broadcasted_iota(jnp.int32, (L, L), 1)
    causal = ri >= ci
    tri = causal.astype(F32)
    tri_t = (ri <= ci).astype(F32)
    b_c = jnp.dot(tri, a_c, precision=HIGHEST, preferred_element_type=F32)
    a_r = a_c.T
    b_r = jnp.dot(a_r[0:8], tri_t, precision=HIGHEST, preferred_element_type=F32)

    for h in range(M_HEADS):
        sl = slice(h * M_HD, (h + 1) * M_HD)
        xh = xc[:, sl]
        xb = xh.astype(BF16)
        q = jnp.dot(xb, wq_ref[h], preferred_element_type=F32)
        k = jnp.dot(xb, wk_ref[h], preferred_element_type=F32) * (M_HD ** -0.5)
        vb = v_ref[0, :, sl].astype(BF16)
        qb = q.astype(BF16)
        kb = k.astype(BF16)

        bt = b_c[:, M_HEADS + h:M_HEADS + h + 1]
        ic = a_c[:, h:h + 1]
        bs = b_r[M_HEADS + h:M_HEADS + h + 1, :]
        ir = a_r[h:h + 1, :]
        m_prev = m_scr[h][:, 0:1]

        dm = jnp.where(causal, bt - bs + ir, -jnp.inf)
        inter = bt + m_prev
        m_t = jnp.maximum(inter, jnp.max(dm, axis=-1, keepdims=True))
        w_in = jnp.exp(dm - m_t)
        w_st = jnp.exp(inter - m_t)
        s = lax.dot_general(qb, kb, (((1,), (1,)), ((), ())), preferred_element_type=F32) * w_in
        cmat = c_scr[h]
        nvec = n_scr[h]
        num = w_st * jnp.dot(qb, cmat.astype(BF16), preferred_element_type=F32) \
            + jnp.dot(s.astype(BF16), vb, preferred_element_type=F32)
        den = w_st * jnp.sum(q * nvec, axis=-1, keepdims=True) + jnp.sum(s, axis=-1, keepdims=True)
        hh = num / jnp.maximum(jnp.abs(den), jnp.exp(-m_t))

        b_last = bt[L - 1:L, :]
        w_end = b_last - bt + ic
        m_new = jnp.maximum(b_last + m_prev, jnp.max(w_end, axis=0, keepdims=True))
        decay = jnp.exp(b_last + m_prev - m_new)
        kwt = k * jnp.exp(w_end - m_new)
        c_scr[h] = decay * cmat + lax.dot_general(kwt.astype(BF16), vb, (((0,), (0,)), ((), ())),
                                                  preferred_element_type=F32)
        n_scr[h] = decay * nvec + jnp.sum(kwt, axis=0, keepdims=True)
        m_scr[h] = jnp.broadcast_to(m_new, (1, LANES))

        mu = jnp.mean(hh, axis=-1, keepdims=True)
        hc = hh - mu
        var = jnp.mean(hc * hc, axis=-1, keepdims=True)
        hn = hc * lax.rsqrt(var + EPS) * nw_ref[:, sl]
        out = jax.nn.sigmoid(o_ref[0, :, sl]) * hn + sk_ref[:, sl] * xh
        y_ref[0, :, sl] = (out * _silu(z_ref[0, :, sl])).astype(BF16)


def _mlstm(p, conv_w, conv_b, wq, wk, norm_w, skip, fb_row):
    bsz, t, _ = p.shape
    L = M_CHUNK
    cb = lambda c: (lambda bi, i: (bi, i, c))
    full2 = lambda bi, i: (0, 0)
    full3 = lambda bi, i: (0, 0, 0)
    return pl.pallas_call(
        _mlstm_kernel,
        grid=(bsz, t // L),
        in_specs=[pl.BlockSpec((1, L, M_WIDTH), cb(COL_MX // M_WIDTH)),
                  pl.BlockSpec((1, L, M_WIDTH), cb(COL_MV // M_WIDTH)),
                  pl.BlockSpec((1, L, M_WIDTH), cb(COL_MO // M_WIDTH)),
                  pl.BlockSpec((1, L, M_WIDTH), cb(COL_MZ // M_WIDTH)),
                  pl.BlockSpec((1, L, LANES), cb(COL_GATES // LANES)),
                  pl.BlockSpec((CONV_K, M_WIDTH), full2),
                  pl.BlockSpec((1, M_WIDTH), full2),
                  pl.BlockSpec((M_HEADS, M_HD, M_HD), full3),
                  pl.BlockSpec((M_HEADS, M_HD, M_HD), full3),
                  pl.BlockSpec((1, M_WIDTH), full2),
                  pl.BlockSpec((1, M_WIDTH), full2),
                  pl.BlockSpec((1, LANES), full2)],
        out_specs=pl.BlockSpec((1, L, M_WIDTH), lambda bi, i: (bi, i, 0)),
        out_shape=jax.ShapeDtypeStruct((bsz, t, M_WIDTH), BF16),
        scratch_shapes=[pltpu.VMEM((M_HEADS, M_HD, M_HD), F32),
                        pltpu.VMEM((M_HEADS, 1, M_HD), F32),
                        pltpu.VMEM((M_HEADS, 1, LANES), F32),
                        pltpu.VMEM((8, M_WIDTH), F32)],
        compiler_params=_cparams(("parallel", "arbitrary")),
        name="mlstm_group",
    )(p, p, p, p, p, conv_w, conv_b, wq, wk, norm_w, skip, fb_row)


def _compress_kernel(kc0_ref, kc1_ref, vc0_ref, vc1_ref, posk_ref, posv_ref, w1k_ref, w1kc_ref, w2k_ref,
                     w1v_ref, w1vc_ref, w2v_ref, kcmp_ref, vcmpt_ref):
    nb = kcmp_ref.shape[2]

    def hidden(src_refs, pos_ref, w1_ref, w1c_ref):
        acc = jnp.zeros((N_KV * nb, 2 * CMP_HIDDEN), F32)
        for l in range(CMP_STRIDE):
            xl = [r[0, pl.ds(l, nb, stride=CMP_STRIDE), :] for r in src_refs]
            xs = jnp.concatenate([x[:, g * N_HD:(g + 1) * N_HD] for x in xl for g in range(LANES // N_HD)], axis=0)
            acc = acc + jnp.dot(xs.astype(BF16), w1c_ref[l], preferred_element_type=F32)
        first = acc[:, :CMP_HIDDEN]
        second = acc[:, CMP_HIDDEN:]
        posb = jnp.dot(jnp.broadcast_to(pos_ref[...], (8, CMP_LEN * N_HD)).astype(BF16), w1_ref[...],
                       preferred_element_type=F32)[0:1]
        hid = first + pltpu.roll(second, N_KV * nb - 1, 0) + posb
        return jax.nn.gelu(hid).astype(BF16)

    hk = hidden((kc0_ref, kc1_ref), posk_ref, w1k_ref, w1kc_ref)
    kc = jnp.dot(hk, w2k_ref[...], preferred_element_type=F32)
    hv = hidden((vc0_ref, vc1_ref), posv_ref, w1v_ref, w1vc_ref)
    vc = jnp.dot(hv, w2v_ref[...], preferred_element_type=F32)
    j = lax.broadcasted_iota(jnp.int32, (nb, 1), 0)
    lane = lax.broadcasted_iota(jnp.int32, (nb, N_HD), 1)
    n_real = (nb * CMP_STRIDE - CMP_LEN) // CMP_STRIDE + 1
    aux = jnp.where(lane < 3, (j * CMP_STRIDE).astype(F32),
                    jnp.where(lane < 6, (CMP_LEN - 1) * 0.5,
                              jnp.where((lane == 6) & (j >= n_real), 1.0, 0.0))).astype(BF16)
    ones_blk = jnp.where(lax.broadcasted_iota(jnp.int32, (VROWS - N_HD, nb), 0) == 0, 1.0, 0.0)
    for g in range(N_KV):
        kg = kc[g * nb:(g + 1) * nb, 0:N_HD]
        k_hi = kg.astype(BF16)
        k_lo = (kg - k_hi.astype(F32)).astype(BF16)
        kcmp_ref[0, g] = jnp.concatenate([k_hi, k_lo, k_hi, aux], axis=1)
        vt = vc[g * nb:(g + 1) * nb].T
        vcmpt_ref[0, g] = jnp.concatenate([vt[0:N_HD], ones_blk], axis=0).astype(BF16)


def _compress(p, posk, posv, w1k, w1kc, w2k, w1v, w1vc, w2v):
    bsz, t, _ = p.shape
    nb = t // CMP_STRIDE
    cb = lambda c: (lambda bi: (bi, 0, c))
    f2 = lambda bi: (0, 0)
    f3 = lambda bi: (0, 0, 0)
    wspecs = [pl.BlockSpec((CMP_LEN * N_HD, CMP_HIDDEN), f2),
              pl.BlockSpec((CMP_STRIDE, N_HD, 2 * CMP_HIDDEN), f3),
              pl.BlockSpec((CMP_HIDDEN, LANES), f2)]
    return pl.pallas_call(
        _compress_kernel,
        grid=(bsz,),
        in_specs=[pl.BlockSpec((1, t, LANES), cb(COL_KC // LANES)),
                  pl.BlockSpec((1, t, LANES), cb(COL_KC // LANES + 1)),
                  pl.BlockSpec((1, t, LANES), cb(COL_VC // LANES)),
                  pl.BlockSpec((1, t, LANES), cb(COL_VC // LANES + 1)),
                  pl.BlockSpec((1, CMP_LEN * N_HD), f2),
                  pl.BlockSpec((1, CMP_LEN * N_HD), f2)] + wspecs + wspecs,
        out_specs=[pl.BlockSpec((1, N_KV, nb, 4 * N_HD), lambda bi: (bi, 0, 0, 0)),
                   pl.BlockSpec((1, N_KV, VROWS, nb), lambda bi: (bi, 0, 0, 0))],
        out_shape=[jax.ShapeDtypeStruct((bsz, N_KV, nb, 4 * N_HD), BF16),
                   jax.ShapeDtypeStruct((bsz, N_KV, VROWS, nb), BF16)],
        compiler_params=_cparams(("parallel",)),
        name="nsa_compress",
    )(p, p, p, p, posk, posv, w1k, w1kc, w2k, w1v, w1vc, w2v)


def _relayout_kernel(ks_ref, vs_ref, kw_ref, vw_ref, ksa_ref, vsa_ref, kwa_ref, vwa_ref):
    i = pl.program_id(1)
    is_pad = i < PAD_TILES
    base = (i - PAD_TILES) * SKT
    row = lax.broadcasted_iota(jnp.int32, (SKT, 1), 0)
    pos = jnp.where(is_pad, 0, base + row)
    blk = pos // SLC_LEN
    p_hi = (blk * SLC_LEN).astype(F32)
    p_lo = (pos - blk * SLC_LEN).astype(F32)
    flag = jnp.where(is_pad, 1.0, 0.0)
    lane = lax.broadcasted_iota(jnp.int32, (SKT, LANES), 1)
    aux = jnp.where(lane < 3, p_hi, jnp.where(lane < 6, p_lo, jnp.where(lane == 6, flag, 0.0)))
    onehot = jnp.where(blk == lax.broadcasted_iota(jnp.int32, (SKT, OH_W), 1), 1.0, 0.0)
    ks = ks_ref[0]
    kw = kw_ref[0]
    for g in range(N_KV):
        sl = slice(g * N_HD, (g + 1) * N_HD)
        ksa_ref[0, g] = jnp.concatenate([ks[:, sl], onehot, aux], axis=1).astype(BF16)
        kwa_ref[0, g] = jnp.concatenate([kw[:, sl], aux[:, 0:N_HD]], axis=1).astype(BF16)
    ones_blk = jnp.where(lax.broadcasted_iota(jnp.int32, (VROWS - N_HD, SKT), 0) == 0, 1.0, 0.0)
    vst = vs_ref[0].T
    vwt = vw_ref[0].T
    for g in range(N_KV):
        sl = slice(g * N_HD, (g + 1) * N_HD)
        vsa_ref[0, g, 0] = jnp.concatenate([vst[sl], ones_blk], axis=0).astype(BF16)
        vwg = jnp.concatenate([vwt[sl], ones_blk], axis=0).astype(BF16)
        for j in range(SKT // WKT):
            vwa_ref[0, g, j] = vwg[:, j * WKT:(j + 1) * WKT]


def _relayout(p):
    bsz, t, _ = p.shape
    nt = t // SKT + PAD_TILES
    wpt = SKT // WKT
    cb = lambda c: (lambda bi, i: (bi, jnp.maximum(i - PAD_TILES, 0), c))
    return pl.pallas_call(
        _relayout_kernel,
        grid=(bsz, nt),
        in_specs=[pl.BlockSpec((1, SKT, KV_W), cb(COL_KS // KV_W)),
                  pl.BlockSpec((1, SKT, KV_W), cb(COL_VS // KV_W)),
                  pl.BlockSpec((1, SKT, KV_W), cb(COL_KW // KV_W)),
                  pl.BlockSpec((1, SKT, KV_W), cb(COL_VW // KV_W))],
        out_specs=[pl.BlockSpec((1, N_KV, SKT, KA_SEL), lambda bi, i: (bi, 0, i, 0)),
                   pl.BlockSpec((1, N_KV, 1, VROWS, SKT), lambda bi, i: (bi, 0, i, 0, 0)),
                   pl.BlockSpec((1, N_KV, SKT, KA_WIN), lambda bi, i: (bi, 0, i, 0)),
                   pl.BlockSpec((1, N_KV, wpt, VROWS, WKT), lambda bi, i: (bi, 0, i, 0, 0))],
        out_shape=[jax.ShapeDtypeStruct((bsz, N_KV, nt * SKT, KA_SEL), BF16),
                   jax.ShapeDtypeStruct((bsz, N_KV, nt, VROWS, SKT), BF16),
                   jax.ShapeDtypeStruct((bsz, N_KV, nt * SKT, KA_WIN), BF16),
                   jax.ShapeDtypeStruct((bsz, N_KV, nt * wpt, VROWS, WKT), BF16)],
        compiler_params=_cparams(("parallel", "parallel")),
        name="nsa_relayout",
    )(p, p, p, p)


def _nsa_kernel(sp_ref, q_ref, gt_ref, z_ref, kcmp_ref, vcmpt_ref, ovt_ref, ks_ref, vs_ref, kw_ref,
                vw_ref, tri_ref, wb_ref, y_ref, s_scr, g_scr, imp_scr, rank_scr, part_scr):
    g = pl.program_id(1)
    qb = pl.program_id(2)
    start = qb * QBLK
    nb = kcmp_ref.shape[2]
    n_slc = ovt_ref.shape[0]

    qt = (q_ref[0] * (N_HD ** -0.5)).T
    q4t = jnp.concatenate([qt[h * N_HD:(h + 1) * N_HD] for h in range(N_HPG)], axis=1)
    t_row = start + lax.broadcasted_iota(jnp.int32, (1, QBLK), 1)

    g_scr[...] = jax.nn.sigmoid(gt_ref[0]).T

    nw = N_HPG * QBLK
    hl = lax.broadcasted_iota(jnp.int32, (AUX_ROWS, nw), 1) // QBLK
    ar = lax.broadcasted_iota(jnp.int32, (AUX_ROWS, nw), 0)

    def per_head(vals):
        out = jnp.full((AUX_ROWS, nw), vals[N_HPG - 1], F32)
        for h in range(N_HPG - 2, -1, -1):
            out = jnp.where(hl == h, vals[h], out)
        return out

    pieces = [per_head([sp_ref[(g * N_HPG + h) * 3 + j] for h in range(N_HPG)]) for j in range(3)]
    aux = jnp.where(ar == 6, NEG, 0.0)
    for j in range(3):
        aux = jnp.where((ar == j) | (ar == j + 3), pieces[j], aux)
    aux_b = aux.astype(BF16)
    q4l = q4t * LOG2E
    q4s = q4l.astype(BF16)
    q4lo = (q4l - q4s.astype(F32)).astype(BF16)

    def tile4(a):
        return jnp.concatenate([a] * N_HPG, axis=1)

    def col_reduce8(s, op):
        out = s[0:8]
        for r in range(1, s.shape[0] // 8):
            out = op(out, s[8 * r:8 * r + 8])
        return out

    def col_max(s):
        return jnp.max(col_reduce8(s, jnp.maximum), axis=0, keepdims=True)

    def normalise(acc):
        den = acc[N_HD:N_HD + 1]
        return acc[0:N_HD] / jnp.where(den > 0, den, 1.0)

    q_cmp = jnp.concatenate([q4s, q4s, q4lo, aux_b, jnp.zeros((N_HD - AUX_ROWS, nw), BF16)], axis=0)
    cmp_end = lax.broadcasted_iota(jnp.int32, (nb, 1), 0) * CMP_STRIDE + (CMP_LEN - 1)
    s_c = jnp.dot(kcmp_ref[0, 0], q_cmp, preferred_element_type=F32)
    s_c = s_c + tile4(jnp.where(cmp_end <= t_row, 0.0, NEG))
    e_c = jnp.exp2(s_c - col_max(s_c))
    z_c = jnp.sum(col_reduce8(e_c, jnp.add), axis=0, keepdims=True)
    inv_c = jnp.where(tile4(t_row >= CMP_LEN - 1) & (z_c > 0), 1.0 / z_c, 0.0)
    o_c = jnp.dot(vcmpt_ref[0, 0], e_c.astype(BF16), preferred_element_type=F32)[0:N_HD] * inv_c
    p_c = e_c * inv_c
    psum = p_c[:, 0:QBLK]
    for h in range(1, N_HPG):
        psum = psum + p_c[:, h * QBLK:(h + 1) * QBLK]
    parts, rest = [], psum
    for _ in range(3):
        parts.append(rest.astype(BF16))
        rest = rest - parts[-1].astype(F32)
    imp = jnp.dot(ovt_ref[...], jnp.concatenate(parts, axis=0), preferred_element_type=F32)
    sidx = lax.broadcasted_iota(jnp.int32, (n_slc, QBLK), 0)
    cur = t_row // SLC_LEN
    valid = sidx <= cur
    imp = jnp.where(valid, imp, -jnp.inf)
    imp_scr[...] = jnp.where((sidx == 0) | (sidx == cur), jnp.inf, imp)

    wrows = WIN + QBLK
    q_win = jnp.concatenate([q4s, aux_b, jnp.zeros((KA_WIN - N_HD - AUX_ROWS, nw), BF16)], axis=0)
    s_w = jnp.dot(kw_ref[0, 0, pl.ds(pl.multiple_of(start, QBLK), wrows), :], q_win,
                  preferred_element_type=F32)
    s_w = jnp.concatenate([s_w[0:WKT] + tile4(wb_ref[0]), s_w[WKT:wrows - WKT],
                           s_w[wrows - WKT:] + tile4(wb_ref[1])], axis=0)
    p_w = jnp.exp2(s_w - col_max(s_w)).astype(BF16)
    v_w = jnp.concatenate([vw_ref[0, 0, qb * (QBLK // WKT) + j] for j in range(wrows // WKT)], axis=1)
    o_w = normalise(jnp.dot(v_w, p_w, preferred_element_type=F32))

    def gate_row(h, branch):
        return g_scr[pl.ds(GATE_NG + (g * N_HPG + h) * 3 + branch, 1), :]

    for h in range(N_HPG):
        hs = slice(h * QBLK, (h + 1) * QBLK)
        part_scr[:, hs] = gate_row(h, 0) * o_c[:, hs] + gate_row(h, 2) * o_w[:, hs]

    n_top = min(SLC_TOPN, n_slc)
    rank_scr[...] = jnp.zeros_like(rank_scr)
    sub8 = lax.broadcasted_iota(jnp.int32, (8, QBLK), 0)
    last_blk = (start + QBLK - 1) // SLC_LEN
    for ri in range(n_slc // 8):
        @pl.when((last_blk >= n_top) & (8 * ri <= last_blk))
        def _():
            rows = imp_scr[8 * ri:8 * ri + 8]
            for r in range(n_slc // 8):
                blk8 = imp_scr[8 * r:8 * r + 8]
                acc = rank_scr[8 * r:8 * r + 8]
                for ii in range(8):
                    row = rows[ii:ii + 1]
                    if ri < r:
                        before = row >= blk8
                    elif ri > r:
                        before = row > blk8
                    else:
                        before = (row > blk8) | ((row == blk8) & (sub8 > ii))
                    acc = acc + jnp.where(before, 1.0, 0.0)
                rank_scr[8 * r:8 * r + 8] = acc
    selbias = jnp.where((rank_scr[...] < n_top) & valid, 0.0, NEG)

    selb4 = tile4(selbias.astype(BF16))
    oh_pad = [jnp.zeros((OH_W - n_slc, nw), BF16)] if n_slc < OH_W else []
    q_sel = jnp.concatenate([q4s, selb4] + oh_pad + [aux_b, jnp.zeros((KA_SEL - N_HD - OH_W - AUX_ROWS, nw), BF16)],
                            axis=0)
    n_tiles = qb // (SKT // QBLK) + 1

    def sel_scores(kt):
        krow = pl.multiple_of((kt + PAD_TILES) * SKT, SKT)
        return jnp.dot(ks_ref[0, 0, pl.ds(krow, SKT), :], q_sel, preferred_element_type=F32)

    def pass1(kt, mrun):
        s = sel_scores(kt)
        s_scr[pl.ds(pl.multiple_of(kt * SKT, SKT), SKT), :] = s
        return jnp.maximum(mrun, col_reduce8(s, jnp.maximum))

    def unrolled(fn):
        def body(i, carry):
            for u in range(UNROLL):
                carry = fn(i * UNROLL + u, carry)
            return carry
        return body

    def run_tiles(fn, n, init):
        n_u = n // UNROLL
        carry = lax.fori_loop(0, n_u, unrolled(fn), init)
        return lax.fori_loop(n_u * UNROLL, n, fn, carry)

    mrun = run_tiles(pass1, n_tiles - 1, jnp.full((8, nw), NEG, F32))
    last = n_tiles - 1
    s_last = sel_scores(last) + tile4(tri_ref[qb % (SKT // QBLK)])
    s_scr[pl.ds(pl.multiple_of(last * SKT, SKT), SKT), :] = s_last
    m_sel = jnp.max(jnp.maximum(mrun, col_reduce8(s_last, jnp.maximum)), axis=0, keepdims=True)

    def pass2(kt, acc):
        s = s_scr[pl.ds(pl.multiple_of(kt * SKT, SKT), SKT), :]
        pr = jnp.exp2(s - m_sel).astype(BF16)
        return acc + jnp.dot(vs_ref[0, 0, kt + PAD_TILES], pr, preferred_element_type=F32)

    o_s = normalise(run_tiles(pass2, n_tiles, jnp.zeros((VROWS, nw), F32)))

    tot = [part_scr[:, h * QBLK:(h + 1) * QBLK] + gate_row(h, 1) * o_s[:, h * QBLK:(h + 1) * QBLK]
           for h in range(N_HPG)]
    o = jnp.concatenate(tot, axis=0).T
    y_ref[0] = (o * _silu(z_ref[0])).astype(BF16)


def _edge_biases():
    kl = np.arange(SKT)[:, None]
    ql = np.arange(QBLK)[None, :]
    tri = np.stack([np.where(kl <= par * QBLK + ql, 0.0, NEG) for par in range(SKT // QBLK)])
    kk = np.arange(WKT)[:, None]
    wb = np.stack([np.where(kk > ql, 0.0, NEG), np.where(kk <= ql + WKT - QBLK, 0.0, NEG)])
    return jnp.asarray(tri, F32), jnp.asarray(wb, F32)


def _nsa(p, spieces, kcmp, vcmpt, ovt, ks, vs, kw, vw):
    bsz, t, _ = p.shape
    n_slc = t // SLC_LEN
    gw = N_HPG * N_HD
    tri, wb = _edge_biases()
    per_bg = lambda a: pl.BlockSpec((1, 1) + a.shape[2:], lambda bi, g, i, sp: (bi, g) + (0,) * (a.ndim - 2))
    const = lambda a: pl.BlockSpec(a.shape, lambda bi, g, i, sp: (0,) * a.ndim)
    grid_spec = pltpu.PrefetchScalarGridSpec(
        num_scalar_prefetch=1,
        grid=(bsz, N_KV, t // QBLK),
        in_specs=[pl.BlockSpec((1, QBLK, gw), lambda bi, g, i, sp: (bi, i, COL_NQ // gw + g)),
                  pl.BlockSpec((1, QBLK, LANES), lambda bi, g, i, sp: (bi, i, COL_GATES // LANES)),
                  pl.BlockSpec((1, QBLK, gw), lambda bi, g, i, sp: (bi, i, COL_NZ // gw + g)),
                  per_bg(kcmp), per_bg(vcmpt), const(ovt),
                  per_bg(ks), per_bg(vs), per_bg(kw), per_bg(vw), const(tri), const(wb)],
        out_specs=pl.BlockSpec((1, QBLK, gw), lambda bi, g, i, sp: (bi, i, g)),
        scratch_shapes=[pltpu.VMEM((t, N_HPG * QBLK), F32),
                        pltpu.VMEM((LANES, QBLK), F32),
                        pltpu.VMEM((n_slc, QBLK), F32),
                        pltpu.VMEM((n_slc, QBLK), F32),
                        pltpu.VMEM((N_HD, N_HPG * QBLK), F32)],
    )
    return pl.pallas_call(
        _nsa_kernel,
        grid_spec=grid_spec,
        out_shape=jax.ShapeDtypeStruct((bsz, t, N_WIDTH), BF16),
        compiler_params=_cparams(("parallel", "parallel", "arbitrary")),
        name="nsa_attention",
    )(spieces, p, p, p, kcmp, vcmpt, ovt, ks, vs, kw, vw, tri, wb)


def _outproj_kernel(ym_ref, yn_ref, w_ref, x_ref, gate_ref, fg_ref, o_ref, *, final):
    y = jnp.dot(ym_ref[0], w_ref[0:M_WIDTH, :], preferred_element_type=F32)
    y = y + jnp.dot(yn_ref[0], w_ref[M_WIDTH:, :], preferred_element_type=F32)
    hres = x_ref[0] + gate_ref[0] * y
    if final:
        ms = jnp.mean(hres * hres, axis=-1, keepdims=True)
        hres = hres * lax.rsqrt(ms + EPS) * fg_ref[...]
    o_ref[0] = hres


def _outproj(ym, yn, w, x, gate, fg, final):
    bsz, t, d = x.shape
    tm = 512
    return pl.pallas_call(
        functools.partial(_outproj_kernel, final=final),
        grid=(bsz, t // tm),
        in_specs=[pl.BlockSpec((1, tm, M_WIDTH), lambda bi, i: (bi, i, 0)),
                  pl.BlockSpec((1, tm, N_WIDTH), lambda bi, i: (bi, i, 0)),
                  pl.BlockSpec((M_WIDTH + N_WIDTH, d), lambda bi, i: (0, 0)),
                  pl.BlockSpec((1, tm, d), lambda bi, i: (bi, i, 0)),
                  pl.BlockSpec((1, 1, d), lambda bi, i: (bi, 0, 0)),
                  pl.BlockSpec((1, d), lambda bi, i: (0, 0))],
        out_specs=pl.BlockSpec((1, tm, d), lambda bi, i: (bi, i, 0)),
        out_shape=jax.ShapeDtypeStruct((bsz, t, d), F32),
        compiler_params=_cparams(("parallel", "parallel")),
        name="outproj_residual",
    )(ym, yn, w, x, gate, fg)


def _reorder_cols(a):
    o_mi = 4 * M_WIDTH
    o_nq = o_mi + 2 * M_HEADS
    o_kv = o_nq + N_WIDTH
    o_ng = o_kv + 6 * KV_W
    o_nz = o_ng + 3 * N_HEADS
    parts = [a[..., 0:o_mi], a[..., o_nq:o_ng], a[..., o_nz:o_nz + N_WIDTH], a[..., o_mi:o_nq], a[..., o_ng:o_nz]]
    used = sum(x.shape[-1] for x in parts)
    parts.append(jnp.zeros(a.shape[:-1] + (NP_PAD - used,), a.dtype))
    return jnp.concatenate(parts, axis=-1)


def _overlap_t(t):
    n_cmp_rows = t // CMP_STRIDE
    n_slc = t // SLC_LEN
    c0 = np.arange(n_cmp_rows) * CMP_STRIDE
    s0 = np.arange(n_slc) * SLC_LEN
    ov = (c0[None, :] <= s0[:, None] + SLC_LEN - 1) & (c0[None, :] + CMP_LEN - 1 >= s0[:, None])
    ov[:, (t - CMP_LEN) // CMP_STRIDE + 1:] = False
    return jnp.asarray(np.concatenate([ov] * 3, axis=1), BF16)


def kernel(x, c, ln_g, w_ada, b_ada, w_in, b_in, m_conv_w, m_conv_b, m_wq, m_wk, m_norm_w, m_skip, m_f_bias,
           n_pos_k, n_pos_v, n_w1_k, n_w2_k, n_w1_v, n_w2_v, w_out, final_g):
    out_dtype = x.dtype
    bsz, t, d = x.shape
    depth = ln_g.shape[0]
    h_res = x.astype(F32)
    c8 = jnp.zeros((8, d), F32).at[:bsz].set(c.astype(F32))
    slopes_np = np.array([2.0 ** (-8.0 * (h + 1) / N_HEADS) for h in range(N_HEADS)], np.float32)
    rest = (slopes_np.astype(np.float64) * LOG2E).astype(np.float32)
    pieces = []
    for _ in range(3):
        pieces.append(rest.astype(jnp.bfloat16).astype(np.float32))
        rest = rest - pieces[-1]
    spieces = jnp.asarray(np.stack(pieces, axis=1).reshape(-1))
    ovt = _overlap_t(t)

    def w1cat(w1):
        w = w1.reshape(2, CMP_STRIDE, N_HD, CMP_HIDDEN)
        return jnp.concatenate([w[0], w[1]], axis=-1).astype(BF16)

    def w2pad(w2):
        return jnp.pad(w2, ((0, 0), (0, LANES - N_HD))).astype(BF16)

    for l in range(depth):
        mod = _ada(c8, w_ada[l], b_ada[l][None, :])[:bsz]
        shift, scale, gate = mod[:, None, 0:d], mod[:, None, d:2 * d], mod[:, None, 2 * d:3 * d]
        p = _inproj(h_res, ln_g[l][None, :], scale, shift,
                    _reorder_cols(w_in[l]).astype(BF16), _reorder_cols(b_in[l])[None, :])
        fb_row = jnp.zeros((1, LANES), F32).at[0, M_HEADS:2 * M_HEADS].set(m_f_bias[l])
        y_m = _mlstm(p, m_conv_w[l], m_conv_b[l][None, :], m_wq[l].astype(BF16), m_wk[l].astype(BF16),
                     m_norm_w[l][None, :], m_skip[l][None, :], fb_row)
        kcmp, vcmpt = _compress(p, n_pos_k[l].reshape(1, -1), n_pos_v[l].reshape(1, -1),
                                n_w1_k[l].astype(BF16), w1cat(n_w1_k[l]), w2pad(n_w2_k[l]),
                                n_w1_v[l].astype(BF16), w1cat(n_w1_v[l]), w2pad(n_w2_v[l]))
        ks, vs, kw, vw = _relayout(p)
        y_n = _nsa(p, spieces, kcmp, vcmpt, ovt, ks, vs, kw, vw)
        h_res = _outproj(y_m, y_n, w_out[l].astype(BF16), h_res, gate, final_g[None, :], l == depth - 1)
    return h_res.astype(out_dtype)
```

```python
import functools

import numpy as np
import jax
import jax.numpy as jnp
from jax import lax
from jax.experimental import pallas as pl
from jax.experimental.pallas import tpu as pltpu

F32 = jnp.float32
BF16 = jnp.bfloat16
HIGHEST = lax.Precision.HIGHEST

EPS = 1e-6
M_HEADS = 4
M_HD = 256
M_WIDTH = M_HEADS * M_HD
CONV_K = 4
M_CHUNK = 256
N_HEADS = 16
N_HD = 64
N_KV = 4
N_HPG = N_HEADS // N_KV
N_WIDTH = N_HEADS * N_HD
KV_W = N_KV * N_HD
CMP_LEN = 32
CMP_STRIDE = 16
CMP_HIDDEN = 2 * N_HD
SLC_LEN = 64
SLC_TOPN = 16
WIN = 512
QBLK = 256
SKT = 256
WKT = 256
PAD_TILES = WIN // SKT
SEL_GRP = 8
KA_SEL = 128
KA_WIN = 128
VROWS = 80
UNROLL = 4
AUX_ROWS = 16
LOG2E = 1.4426950408889634

COL_MX, COL_MV, COL_MO, COL_MZ = 0, 1024, 2048, 3072
COL_NQ = 4096
COL_KC, COL_VC, COL_KS, COL_VS, COL_KW, COL_VW = 5120, 5376, 5632, 5888, 6144, 6400
COL_NZ = 6656
COL_GATES = 7680
GATE_NG = 2 * M_HEADS
NP_PAD = 8192
LANES = 128
NEG = -1e30
VMEM_LIMIT = 56 * 1024 * 1024


def _cparams(sem):
    return pltpu.CompilerParams(dimension_semantics=sem, vmem_limit_bytes=VMEM_LIMIT)


def _silu(x):
    return x * jax.nn.sigmoid(x)


def _log_sigmoid(x):
    return jnp.minimum(x, 0.0) - jnp.log1p(jnp.exp(-jnp.abs(x)))


def _ada_kernel(c_ref, w_ref, b_ref, o_ref):
    s = _silu(c_ref[...])
    o_ref[...] = jnp.dot(s, w_ref[...], precision=HIGHEST, preferred_element_type=F32) + b_ref[...]


def _ada(c8, w, b):
    d, n = w.shape
    tn = 1024
    return pl.pallas_call(
        _ada_kernel,
        grid=(n // tn,),
        in_specs=[pl.BlockSpec((8, d), lambda j: (0, 0)),
                  pl.BlockSpec((d, tn), lambda j: (0, j)),
                  pl.BlockSpec((1, tn), lambda j: (0, j))],
        out_specs=pl.BlockSpec((8, tn), lambda j: (0, j)),
        out_shape=jax.ShapeDtypeStruct((8, n), F32),
        compiler_params=_cparams(("parallel",)),
        name="ada_mod",
    )(c8, w, b)


def _inproj_kernel(x_ref, g_ref, sc_ref, sh_ref, w_ref, b_ref, o_ref, h_ref):
    @pl.when(pl.program_id(2) == 0)
    def _():
        x = x_ref[0]
        ms = jnp.mean(x * x, axis=-1, keepdims=True)
        h = x * lax.rsqrt(ms + EPS) * g_ref[...]
        h = h * (1.0 + sc_ref[0]) + sh_ref[0]
        h_ref[...] = h.astype(BF16)

    o_ref[0] = jnp.dot(h_ref[...], w_ref[...], preferred_element_type=F32) + b_ref[...]


def _inproj(x, g, scale, shift, w, b):
    bsz, t, d = x.shape
    n = w.shape[1]
    tm, tn = 1024, 1024
    return pl.pallas_call(
        _inproj_kernel,
        grid=(bsz, t // tm, n // tn),
        in_specs=[pl.BlockSpec((1, tm, d), lambda bi, i, j: (bi, i, 0)),
                  pl.BlockSpec((1, d), lambda bi, i, j: (0, 0)),
                  pl.BlockSpec((1, 1, d), lambda bi, i, j: (bi, 0, 0)),
                  pl.BlockSpec((1, 1, d), lambda bi, i, j: (bi, 0, 0)),
                  pl.BlockSpec((d, tn), lambda bi, i, j: (0, j)),
                  pl.BlockSpec((1, tn), lambda bi, i, j: (0, j))],
        out_specs=pl.BlockSpec((1, tm, tn), lambda bi, i, j: (bi, i, j)),
        out_shape=jax.ShapeDtypeStruct((bsz, t, n), F32),
        scratch_shapes=[pltpu.VMEM((tm, d), BF16)],
        compiler_params=_cparams(("parallel", "parallel", "arbitrary")),
        name="norm_inproj",
    )(x, g, scale, shift, w, b)


def _mlstm_kernel(x_ref, v_ref, o_ref, z_ref, gt_ref, cw_ref, cb_ref, wq_ref, wk_ref, nw_ref, sk_ref, fb_ref,
                  y_ref, c_scr, n_scr, m_scr, xp_scr):
    L = M_CHUNK

    @pl.when(pl.program_id(1) == 0)
    def _():
        c_scr[...] = jnp.zeros_like(c_scr)
        n_scr[...] = jnp.zeros_like(n_scr)
        m_scr[...] = jnp.zeros_like(m_scr)
        xp_scr[...] = jnp.zeros_like(xp_scr)

    x = x_ref[0]
    prev = xp_scr[...]
    row8 = lax.broadcasted_iota(jnp.int32, (8, M_WIDTH), 0)
    cw = cw_ref[...]
    xc = cb_ref[...] + x * cw[CONV_K - 1:CONV_K, :]
    for sft in range(1, CONV_K):
        xr = pltpu.roll(x, sft, 0)
        top = jnp.where(row8 < sft, pltpu.roll(prev, sft, 0), xr[0:8])
        xs = jnp.concatenate([top, xr[8:]], axis=0)
        xc = xc + xs * cw[CONV_K - 1 - sft:CONV_K - sft, :]
    xp_scr[...] = x[L - 8:L]
    xc = _silu(xc)

    gt = gt_ref[0]
    col = lax.broadcasted_iota(jnp.int32, (L, LANES), 1)
    logf = _log_sigmoid(gt + fb_ref[...])
    a_c = jnp.where((col >= M_HEADS) & (col < 2 * M_HEADS), logf, gt)
    ri = lax.broadcasted_iota(jnp.int32, (L, L), 0)
    ci = lax.broadcasted_iota(jnp.int32, (L, L), 1)
    causal = ri >= ci
    tri = causal.astype(F32)
    tri_t = (ri <= ci).astype(F32)
    b_c = jnp.dot(tri, a_c, precision=HIGHEST, preferred_element_type=F32)
    a_r = a_c.T
    b_r = jnp.dot(a_r[0:8], tri_t, precision=HIGHEST, preferred_element_type=F32)

    for h in range(M_HEADS):
        sl = slice(h * M_HD, (h + 1) * M_HD)
        xh = xc[:, sl]
        xb = xh.astype(BF16)
        q = jnp.dot(xb, wq_ref[h], preferred_element_type=F32)
        k = jnp.dot(xb, wk_ref[h], preferred_element_type=F32) * (M_HD ** -0.5)
        vb = v_ref[0, :, sl].astype(BF16)
        qb = q.astype(BF16)
        kb = k.astype(BF16)

        bt = b_c[:, M_HEADS + h:M_HEADS + h + 1]
        ic = a_c[:, h:h + 1]
        bs = b_r[M_HEADS + h:M_HEADS + h + 1, :]
        ir = a_r[h:h + 1, :]
        m_prev = m_scr[h][:, 0:1]

        dm = jnp.where(causal, bt - bs + ir, -jnp.inf)
        inter = bt + m_prev
        m_t = jnp.maximum(inter, jnp.max(dm, axis=-1, keepdims=True))
        w_in = jnp.exp(dm - m_t)
        w_st = jnp.exp(inter - m_t)
        s = lax.dot_general(qb, kb, (((1,), (1,)), ((), ())), preferred_element_type=F32) * w_in
        cmat = c_scr[h]
        nvec = n_scr[h]
        num = w_st * jnp.dot(qb, cmat.astype(BF16), preferred_element_type=F32) \
            + jnp.dot(s.astype(BF16), vb, preferred_element_type=F32)
        den = w_st * jnp.sum(q * nvec, axis=-1, keepdims=True) + jnp.sum(s, axis=-1, keepdims=True)
        hh = num / jnp.maximum(jnp.abs(den), jnp.exp(-m_t))

        b_last = bt[L - 1:L, :]
        w_end = b_last - bt + ic
        m_new = jnp.maximum(b_last + m_prev, jnp.max(w_end, axis=0, keepdims=True))
        decay = jnp.exp(b_last + m_prev - m_new)
        kwt = k * jnp.exp(w_end - m_new)
        c_scr[h] = decay * cmat + lax.dot_general(kwt.astype(BF16), vb, (((0,), (0,)), ((), ())),
                                                  preferred_element_type=F32)
        n_scr[h] = decay * nvec + jnp.sum(kwt, axis=0, keepdims=True)
        m_scr[h] = jnp.broadcast_to(m_new, (1, LANES))

        mu = jnp.mean(hh, axis=-1, keepdims=True)
        hc = hh - mu
        var = jnp.mean(hc * hc, axis=-1, keepdims=True)
        hn = hc * lax.rsqrt(var + EPS) * nw_ref[:, sl]
        out = jax.nn.sigmoid(o_ref[0, :, sl]) * hn + sk_ref[:, sl] * xh
        y_ref[0, :, sl] = (out * _silu(z_ref[0, :, sl])).astype(BF16)


def _mlstm(p, conv_w, conv_b, wq, wk, norm_w, skip, fb_row):
    bsz, t, _ = p.shape
    L = M_CHUNK
    cb = lambda c: (lambda bi, i: (bi, i, c))
    full2 = lambda bi, i: (0, 0)
    full3 = lambda bi, i: (0, 0, 0)
    return pl.pallas_call(
        _mlstm_kernel,
        grid=(bsz, t // L),
        in_specs=[pl.BlockSpec((1, L, M_WIDTH), cb(COL_MX // M_WIDTH)),
                  pl.BlockSpec((1, L, M_WIDTH), cb(COL_MV // M_WIDTH)),
                  pl.BlockSpec((1, L, M_WIDTH), cb(COL_MO // M_WIDTH)),
                  pl.BlockSpec((1, L, M_WIDTH), cb(COL_MZ // M_WIDTH)),
                  pl.BlockSpec((1, L, LANES), cb(COL_GATES // LANES)),
                  pl.BlockSpec((CONV_K, M_WIDTH), full2),
                  pl.BlockSpec((1, M_WIDTH), full2),
                  pl.BlockSpec((M_HEADS, M_HD, M_HD), full3),
                  pl.BlockSpec((M_HEADS, M_HD, M_HD), full3),
                  pl.BlockSpec((1, M_WIDTH), full2),
                  pl.BlockSpec((1, M_WIDTH), full2),
                  pl.BlockSpec((1, LANES), full2)],
        out_specs=pl.BlockSpec((1, L, M_WIDTH), lambda bi, i: (bi, i, 0)),
        out_shape=jax.ShapeDtypeStruct((bsz, t, M_WIDTH), BF16),
        scratch_shapes=[pltpu.VMEM((M_HEADS, M_HD, M_HD), F32),
                        pltpu.VMEM((M_HEADS, 1, M_HD), F32),
                        pltpu.VMEM((M_HEADS, 1, LANES), F32),
                        pltpu.VMEM((8, M_WIDTH), F32)],
        compiler_params=_cparams(("parallel", "arbitrary")),
        name="mlstm_group",
    )(p, p, p, p, p, conv_w, conv_b, wq, wk, norm_w, skip, fb_row)


def _compress_kernel(kc0_ref, kc1_ref, vc0_ref, vc1_ref, posk_ref, posv_ref, w1k_ref, w1kc_ref, w2k_ref,
                     w1v_ref, w1vc_ref, w2v_ref, kcmp_ref, vcmpt_ref):
    nb = kcmp_ref.shape[2]

    def hidden(src_refs, pos_ref, w1_ref, w1c_ref):
        acc = jnp.zeros((N_KV * nb, 2 * CMP_HIDDEN), F32)
        for l in range(CMP_STRIDE):
            xl = [r[0, pl.ds(l, nb, stride=CMP_STRIDE), :] for r in src_refs]
            xs = jnp.concatenate([x[:, g * N_HD:(g + 1) * N_HD] for x in xl for g in range(LANES // N_HD)], axis=0)
            acc = acc + jnp.dot(xs.astype(BF16), w1c_ref[l], preferred_element_type=F32)
        first = acc[:, :CMP_HIDDEN]
        second = acc[:, CMP_HIDDEN:]
        posb = jnp.dot(jnp.broadcast_to(pos_ref[...], (8, CMP_LEN * N_HD)).astype(BF16), w1_ref[...],
                       preferred_element_type=F32)[0:1]
        hid = first + pltpu.roll(second, N_KV * nb - 1, 0) + posb
        return jax.nn.gelu(hid).astype(BF16)

    hk = hidden((kc0_ref, kc1_ref), posk_ref, w1k_ref, w1kc_ref)
    kc = jnp.dot(hk, w2k_ref[...], preferred_element_type=F32)
    hv = hidden((vc0_ref, vc1_ref), posv_ref, w1v_ref, w1vc_ref)
    vc = jnp.dot(hv, w2v_ref[...], preferred_element_type=F32)
    j = lax.broadcasted_iota(jnp.int32, (nb, 1), 0)
    lane = lax.broadcasted_iota(jnp.int32, (nb, N_HD), 1)
    n_real = (nb * CMP_STRIDE - CMP_LEN) // CMP_STRIDE + 1
    aux = jnp.where(lane < 3, (j * CMP_STRIDE).astype(F32),
                    jnp.where(lane < 6, (CMP_LEN - 1) * 0.5,
                              jnp.where((lane == 6) & (j >= n_real), 1.0, 0.0))).astype(BF16)
    ones_blk = jnp.where(lax.broadcasted_iota(jnp.int32, (VROWS - N_HD, nb), 0) == 0, 1.0, 0.0)
    for g in range(N_KV):
        kg = kc[g * nb:(g + 1) * nb, 0:N_HD]
        k_hi = kg.astype(BF16)
        k_lo = (kg - k_hi.astype(F32)).astype(BF16)
        kcmp_ref[0, g] = jnp.concatenate([k_hi, k_lo, k_hi, aux], axis=1)
        vt = vc[g * nb:(g + 1) * nb].T
        vcmpt_ref[0, g] = jnp.concatenate([vt[0:N_HD], ones_blk], axis=0).astype(BF16)


def _compress(p, posk, posv, w1k, w1kc, w2k, w1v, w1vc, w2v):
    bsz, t, _ = p.shape
    nb = t // CMP_STRIDE
    cb = lambda c: (lambda bi: (bi, 0, c))
    f2 = lambda bi: (0, 0)
    f3 = lambda bi: (0, 0, 0)
    wspecs = [pl.BlockSpec((CMP_LEN * N_HD, CMP_HIDDEN), f2),
              pl.BlockSpec((CMP_STRIDE, N_HD, 2 * CMP_HIDDEN), f3),
              pl.BlockSpec((CMP_HIDDEN, LANES), f2)]
    return pl.pallas_call(
        _compress_kernel,
        grid=(bsz,),
        in_specs=[pl.BlockSpec((1, t, LANES), cb(COL_KC // LANES)),
                  pl.BlockSpec((1, t, LANES), cb(COL_KC // LANES + 1)),
                  pl.BlockSpec((1, t, LANES), cb(COL_VC // LANES)),
                  pl.BlockSpec((1, t, LANES), cb(COL_VC // LANES + 1)),
                  pl.BlockSpec((1, CMP_LEN * N_HD), f2),
                  pl.BlockSpec((1, CMP_LEN * N_HD), f2)] + wspecs + wspecs,
        out_specs=[pl.BlockSpec((1, N_KV, nb, 4 * N_HD), lambda bi: (bi, 0, 0, 0)),
                   pl.BlockSpec((1, N_KV, VROWS, nb), lambda bi: (bi, 0, 0, 0))],
        out_shape=[jax.ShapeDtypeStruct((bsz, N_KV, nb, 4 * N_HD), BF16),
                   jax.ShapeDtypeStruct((bsz, N_KV, VROWS, nb), BF16)],
        compiler_params=_cparams(("parallel",)),
        name="nsa_compress",
    )(p, p, p, p, posk, posv, w1k, w1kc, w2k, w1v, w1vc, w2v)


def _relayout_kernel(ks_ref, vs_ref, kw_ref, vw_ref, ksa_ref, vsa_ref, kwa_ref, vwa_ref):
    i = pl.program_id(1)
    is_pad = i < PAD_TILES
    base = (i - PAD_TILES) * SKT
    row = lax.broadcasted_iota(jnp.int32, (SKT, 1), 0)
    pos = jnp.where(is_pad, 0, base + row)
    blk = pos // SLC_LEN
    p_hi = (blk * SLC_LEN).astype(F32)
    p_lo = (pos - blk * SLC_LEN).astype(F32)
    flag = jnp.where(is_pad, 1.0, 0.0)
    lane = lax.broadcasted_iota(jnp.int32, (SKT, N_HD), 1)
    al = lane - AUX_ROWS
    mid_w = jnp.where((al >= 0) & (al < 3), p_hi,
                      jnp.where((al >= 3) & (al < 6), p_lo, jnp.where(al == 6, flag, 0.0)))
    mid_s = jnp.where(lane == blk % SEL_GRP, 1.0, mid_w)
    ks = ks_ref[0]
    kw = kw_ref[0]
    for g in range(N_KV):
        sl = slice(g * N_HD, (g + 1) * N_HD)
        ksa_ref[0, g] = jnp.concatenate([ks[:, sl], mid_s], axis=1).astype(BF16)
        kwa_ref[0, g] = jnp.concatenate([kw[:, sl], mid_w], axis=1).astype(BF16)
    ones_blk = jnp.where(lax.broadcasted_iota(jnp.int32, (VROWS - N_HD, SKT), 0) == 0, 1.0, 0.0)
    vst = vs_ref[0].T
    vwt = vw_ref[0].T
    for g in range(N_KV):
        sl = slice(g * N_HD, (g + 1) * N_HD)
        vsa_ref[0, g, 0] = jnp.concatenate([vst[sl], ones_blk], axis=0).astype(BF16)
        vwg = jnp.concatenate([vwt[sl], ones_blk], axis=0).astype(BF16)
        for j in range(SKT // WKT):
            vwa_ref[0, g, j] = vwg[:, j * WKT:(j + 1) * WKT]


def _relayout(p):
    bsz, t, _ = p.shape
    nt = t // SKT + PAD_TILES
    wpt = SKT // WKT
    cb = lambda c: (lambda bi, i: (bi, jnp.maximum(i - PAD_TILES, 0), c))
    return pl.pallas_call(
        _relayout_kernel,
        grid=(bsz, nt),
        in_specs=[pl.BlockSpec((1, SKT, KV_W), cb(COL_KS // KV_W)),
                  pl.BlockSpec((1, SKT, KV_W), cb(COL_VS // KV_W)),
                  pl.BlockSpec((1, SKT, KV_W), cb(COL_KW // KV_W)),
                  pl.BlockSpec((1, SKT, KV_W), cb(COL_VW // KV_W))],
        out_specs=[pl.BlockSpec((1, N_KV, SKT, KA_SEL), lambda bi, i: (bi, 0, i, 0)),
                   pl.BlockSpec((1, N_KV, 1, VROWS, SKT), lambda bi, i: (bi, 0, i, 0, 0)),
                   pl.BlockSpec((1, N_KV, SKT, KA_WIN), lambda bi, i: (bi, 0, i, 0)),
                   pl.BlockSpec((1, N_KV, wpt, VROWS, WKT), lambda bi, i: (bi, 0, i, 0, 0))],
        out_shape=[jax.ShapeDtypeStruct((bsz, N_KV, nt * SKT, KA_SEL), BF16),
                   jax.ShapeDtypeStruct((bsz, N_KV, nt, VROWS, SKT), BF16),
                   jax.ShapeDtypeStruct((bsz, N_KV, nt * SKT, KA_WIN), BF16),
                   jax.ShapeDtypeStruct((bsz, N_KV, nt * wpt, VROWS, WKT), BF16)],
        compiler_params=_cparams(("parallel", "parallel")),
        name="nsa_relayout",
    )(p, p, p, p)


def _nsa_kernel(sp_ref, q_ref, gt_ref, z_ref, kcmp_ref, vcmpt_ref, ovt_ref, ks_ref, vs_ref, kw_ref,
                vw_ref, tri_ref, wb_ref, y_ref, s_scr, g_scr, imp_scr, rank_scr, part_scr, selb_scr):
    g = pl.program_id(1)
    qb = pl.program_id(2)
    start = qb * QBLK
    nb = kcmp_ref.shape[2]
    n_slc = ovt_ref.shape[0]

    qt = (q_ref[0] * (N_HD ** -0.5)).T
    q4t = jnp.concatenate([qt[h * N_HD:(h + 1) * N_HD] for h in range(N_HPG)], axis=1)
    t_row = start + lax.broadcasted_iota(jnp.int32, (1, QBLK), 1)

    g_scr[...] = jax.nn.sigmoid(gt_ref[0]).T

    nw = N_HPG * QBLK
    hl = lax.broadcasted_iota(jnp.int32, (AUX_ROWS, nw), 1) // QBLK
    ar = lax.broadcasted_iota(jnp.int32, (AUX_ROWS, nw), 0)

    def per_head(vals):
        out = jnp.full((AUX_ROWS, nw), vals[N_HPG - 1], F32)
        for h in range(N_HPG - 2, -1, -1):
            out = jnp.where(hl == h, vals[h], out)
        return out

    pieces = [per_head([sp_ref[(g * N_HPG + h) * 3 + j] for h in range(N_HPG)]) for j in range(3)]
    aux = jnp.where(ar == 6, NEG, 0.0)
    for j in range(3):
        aux = jnp.where((ar == j) | (ar == j + 3), pieces[j], aux)
    aux_b = aux.astype(BF16)
    q4l = q4t * LOG2E
    q4s = q4l.astype(BF16)
    q4lo = (q4l - q4s.astype(F32)).astype(BF16)

    def tile4(a):
        return jnp.concatenate([a] * N_HPG, axis=1)

    def col_reduce8(s, op):
        out = s[0:8]
        for r in range(1, s.shape[0] // 8):
            out = op(out, s[8 * r:8 * r + 8])
        return out

    def col_max(s):
        return jnp.max(col_reduce8(s, jnp.maximum), axis=0, keepdims=True)

    def normalise(acc):
        den = acc[N_HD:N_HD + 1]
        return acc[0:N_HD] / jnp.where(den > 0, den, 1.0)

    q_cmp = jnp.concatenate([q4s, q4s, q4lo, aux_b, jnp.zeros((N_HD - AUX_ROWS, nw), BF16)], axis=0)
    cmp_end = lax.broadcasted_iota(jnp.int32, (nb, 1), 0) * CMP_STRIDE + (CMP_LEN - 1)
    s_c = jnp.dot(kcmp_ref[0, 0], q_cmp, preferred_element_type=F32)
    s_c = s_c + tile4(jnp.where(cmp_end <= t_row, 0.0, NEG))
    e_c = jnp.exp2(s_c - col_max(s_c))
    z_c = jnp.sum(col_reduce8(e_c, jnp.add), axis=0, keepdims=True)
    inv_c = jnp.where(tile4(t_row >= CMP_LEN - 1) & (z_c > 0), 1.0 / z_c, 0.0)
    o_c = jnp.dot(vcmpt_ref[0, 0], e_c.astype(BF16), preferred_element_type=F32)[0:N_HD] * inv_c
    p_c = e_c * inv_c
    psum = p_c[:, 0:QBLK]
    for h in range(1, N_HPG):
        psum = psum + p_c[:, h * QBLK:(h + 1) * QBLK]
    parts, rest = [], psum
    for _ in range(3):
        parts.append(rest.astype(BF16))
        rest = rest - parts[-1].astype(F32)
    imp = jnp.dot(ovt_ref[...], jnp.concatenate(parts, axis=0), preferred_element_type=F32)
    sidx = lax.broadcasted_iota(jnp.int32, (n_slc, QBLK), 0)
    cur = t_row // SLC_LEN
    valid = sidx <= cur
    imp = jnp.where(valid, imp, -jnp.inf)
    imp_scr[...] = jnp.where((sidx == 0) | (sidx == cur), jnp.inf, imp)

    wrows = WIN + QBLK
    q_win = jnp.concatenate([q4s, jnp.zeros((AUX_ROWS, nw), BF16), aux_b,
                             jnp.zeros((KA_WIN - N_HD - 2 * AUX_ROWS, nw), BF16)], axis=0)
    s_w = jnp.dot(kw_ref[0, 0, pl.ds(pl.multiple_of(start, QBLK), wrows), :], q_win,
                  preferred_element_type=F32)
    s_w = jnp.concatenate([s_w[0:WKT] + tile4(wb_ref[0]), s_w[WKT:wrows - WKT],
                           s_w[wrows - WKT:] + tile4(wb_ref[1])], axis=0)
    p_w = jnp.exp2(s_w - col_max(s_w)).astype(BF16)
    v_w = jnp.concatenate([vw_ref[0, 0, qb * (QBLK // WKT) + j] for j in range(wrows // WKT)], axis=1)
    o_w = normalise(jnp.dot(v_w, p_w, preferred_element_type=F32))

    def gate_row(h, branch):
        return g_scr[pl.ds(GATE_NG + (g * N_HPG + h) * 3 + branch, 1), :]

    for h in range(N_HPG):
        hs = slice(h * QBLK, (h + 1) * QBLK)
        part_scr[:, hs] = gate_row(h, 0) * o_c[:, hs] + gate_row(h, 2) * o_w[:, hs]

    n_top = min(SLC_TOPN, n_slc)
    rank_scr[...] = jnp.zeros_like(rank_scr)
    sub8 = lax.broadcasted_iota(jnp.int32, (8, QBLK), 0)
    last_blk = (start + QBLK - 1) // SLC_LEN
    for ri in range(n_slc // 8):
        @pl.when((last_blk >= n_top) & (8 * ri <= last_blk))
        def _():
            rows = imp_scr[8 * ri:8 * ri + 8]
            for r in range(n_slc // 8):
                blk8 = imp_scr[8 * r:8 * r + 8]
                acc = rank_scr[8 * r:8 * r + 8]
                for ii in range(8):
                    row = rows[ii:ii + 1]
                    if ri < r:
                        before = row >= blk8
                    elif ri > r:
                        before = row > blk8
                    else:
                        before = (row > blk8) | ((row == blk8) & (sub8 > ii))
                    acc = acc + jnp.where(before, 1.0, 0.0)
                rank_scr[8 * r:8 * r + 8] = acc
    selbias = jnp.where((rank_scr[...] < n_top) & valid, 0.0, NEG)

    selb_scr[...] = tile4(selbias)
    n_tiles = qb // (SKT // QBLK) + 1
    q_tail = jnp.zeros((KA_SEL - N_HD - 2 * AUX_ROWS, nw), BF16)

    def sel_scores(kt):
        grp = pl.multiple_of((kt * SKT // SLC_LEN) // SEL_GRP * SEL_GRP, SEL_GRP)
        sel_rows = jnp.concatenate([selb_scr[pl.ds(grp, SEL_GRP), :], jnp.zeros((AUX_ROWS - SEL_GRP, nw), F32)], axis=0)
        q_sel = jnp.concatenate([q4s, sel_rows.astype(BF16), aux_b, q_tail], axis=0)
        krow = pl.multiple_of((kt + PAD_TILES) * SKT, SKT)
        return jnp.dot(ks_ref[0, 0, pl.ds(krow, SKT), :], q_sel, preferred_element_type=F32)

    def pass1(kt, mrun):
        s = sel_scores(kt)
        s_scr[pl.ds(pl.multiple_of(kt * SKT, SKT), SKT), :] = s
        return jnp.maximum(mrun, col_reduce8(s, jnp.maximum))

    def unrolled(fn):
        def body(i, carry):
            for u in range(UNROLL):
                carry = fn(i * UNROLL + u, carry)
            return carry
        return body

    def run_tiles(fn, n, init):
        n_u = n // UNROLL
        carry = lax.fori_loop(0, n_u, unrolled(fn), init)
        return lax.fori_loop(n_u * UNROLL, n, fn, carry)

    mrun = run_tiles(pass1, n_tiles - 1, jnp.full((8, nw), NEG, F32))
    last = n_tiles - 1
    s_last = sel_scores(last) + tile4(tri_ref[qb % (SKT // QBLK)])
    s_scr[pl.ds(pl.multiple_of(last * SKT, SKT), SKT), :] = s_last
    m_sel = jnp.max(jnp.maximum(mrun, col_reduce8(s_last, jnp.maximum)), axis=0, keepdims=True)

    def pass2(kt, acc):
        s = s_scr[pl.ds(pl.multiple_of(kt * SKT, SKT), SKT), :]
        pr = jnp.exp2(s - m_sel).astype(BF16)
        return acc + jnp.dot(vs_ref[0, 0, kt + PAD_TILES], pr, preferred_element_type=F32)

    o_s = normalise(run_tiles(pass2, n_tiles, jnp.zeros((VROWS, nw), F32)))

    tot = [part_scr[:, h * QBLK:(h + 1) * QBLK] + gate_row(h, 1) * o_s[:, h * QBLK:(h + 1) * QBLK]
           for h in range(N_HPG)]
    o = jnp.concatenate(tot, axis=0).T
    y_ref[0] = (o * _silu(z_ref[0])).astype(BF16)


def _edge_biases():
    kl = np.arange(SKT)[:, None]
    ql = np.arange(QBLK)[None, :]
    tri = np.stack([np.where(kl <= par * QBLK + ql, 0.0, NEG) for par in range(SKT // QBLK)])
    kk = np.arange(WKT)[:, None]
    wb = np.stack([np.where(kk > ql, 0.0, NEG), np.where(kk <= ql + WKT - QBLK, 0.0, NEG)])
    return jnp.asarray(tri, F32), jnp.asarray(wb, F32)


def _nsa(p, spieces, kcmp, vcmpt, ovt, ks, vs, kw, vw):
    bsz, t, _ = p.shape
    n_slc = t // SLC_LEN
    gw = N_HPG * N_HD
    tri, wb = _edge_biases()
    per_bg = lambda a: pl.BlockSpec((1, 1) + a.shape[2:], lambda bi, g, i, sp: (bi, g) + (0,) * (a.ndim - 2))
    const = lambda a: pl.BlockSpec(a.shape, lambda bi, g, i, sp: (0,) * a.ndim)
    grid_spec = pltpu.PrefetchScalarGridSpec(
        num_scalar_prefetch=1,
        grid=(bsz, N_KV, t // QBLK),
        in_specs=[pl.BlockSpec((1, QBLK, gw), lambda bi, g, i, sp: (bi, i, COL_NQ // gw + g)),
                  pl.BlockSpec((1, QBLK, LANES), lambda bi, g, i, sp: (bi, i, COL_GATES // LANES)),
                  pl.BlockSpec((1, QBLK, gw), lambda bi, g, i, sp: (bi, i, COL_NZ // gw + g)),
                  per_bg(kcmp), per_bg(vcmpt), const(ovt),
                  per_bg(ks), per_bg(vs), per_bg(kw), per_bg(vw), const(tri), const(wb)],
        out_specs=pl.BlockSpec((1, QBLK, gw), lambda bi, g, i, sp: (bi, i, g)),
        scratch_shapes=[pltpu.VMEM((t, N_HPG * QBLK), F32),
                        pltpu.VMEM((LANES, QBLK), F32),
                        pltpu.VMEM((n_slc, QBLK), F32),
                        pltpu.VMEM((n_slc, QBLK), F32),
                        pltpu.VMEM((N_HD, N_HPG * QBLK), F32),
                        pltpu.VMEM((n_slc, N_HPG * QBLK), F32)],
    )
    return pl.pallas_call(
        _nsa_kernel,
        grid_spec=grid_spec,
        out_shape=jax.ShapeDtypeStruct((bsz, t, N_WIDTH), BF16),
        compiler_params=_cparams(("parallel", "parallel", "arbitrary")),
        name="nsa_attention",
    )(spieces, p, p, p, kcmp, vcmpt, ovt, ks, vs, kw, vw, tri, wb)


def _outproj_kernel(ym_ref, yn_ref, w_ref, x_ref, gate_ref, fg_ref, o_ref, *, final):
    y = jnp.dot(ym_ref[0], w_ref[0:M_WIDTH, :], preferred_element_type=F32)
    y = y + jnp.dot(yn_ref[0], w_ref[M_WIDTH:, :], preferred_element_type=F32)
    hres = x_ref[0] + gate_ref[0] * y
    if final:
        ms = jnp.mean(hres * hres, axis=-1, keepdims=True)
        hres = hres * lax.rsqrt(ms + EPS) * fg_ref[...]
    o_ref[0] = hres


def _outproj(ym, yn, w, x, gate, fg, final):
    bsz, t, d = x.shape
    tm = 512
    return pl.pallas_call(
        functools.partial(_outproj_kernel, final=final),
        grid=(bsz, t // tm),
        in_specs=[pl.BlockSpec((1, tm, M_WIDTH), lambda bi, i: (bi, i, 0)),
                  pl.BlockSpec((1, tm, N_WIDTH), lambda bi, i: (bi, i, 0)),
                  pl.BlockSpec((M_WIDTH + N_WIDTH, d), lambda bi, i: (0, 0)),
                  pl.BlockSpec((1, tm, d), lambda bi, i: (bi, i, 0)),
                  pl.BlockSpec((1, 1, d), lambda bi, i: (bi, 0, 0)),
                  pl.BlockSpec((1, d), lambda bi, i: (0, 0))],
        out_specs=pl.BlockSpec((1, tm, d), lambda bi, i: (bi, i, 0)),
        out_shape=jax.ShapeDtypeStruct((bsz, t, d), F32),
        compiler_params=_cparams(("parallel", "parallel")),
        name="outproj_residual",
    )(ym, yn, w, x, gate, fg)


def _reorder_cols(a):
    o_mi = 4 * M_WIDTH
    o_nq = o_mi + 2 * M_HEADS
    o_kv = o_nq + N_WIDTH
    o_ng = o_kv + 6 * KV_W
    o_nz = o_ng + 3 * N_HEADS
    parts = [a[..., 0:o_mi], a[..., o_nq:o_ng], a[..., o_nz:o_nz + N_WIDTH], a[..., o_mi:o_nq], a[..., o_ng:o_nz]]
    used = sum(x.shape[-1] for x in parts)
    parts.append(jnp.zeros(a.shape[:-1] + (NP_PAD - used,), a.dtype))
    return jnp.concatenate(parts, axis=-1)


def _overlap_t(t):
    n_cmp_rows = t // CMP_STRIDE
    n_slc = t // SLC_LEN
    c0 = np.arange(n_cmp_rows) * CMP_STRIDE
    s0 = np.arange(n_slc) * SLC_LEN
    ov = (c0[None, :] <= s0[:, None] + SLC_LEN - 1) & (c0[None, :] + CMP_LEN - 1 >= s0[:, None])
    ov[:, (t - CMP_LEN) // CMP_STRIDE + 1:] = False
    return jnp.asarray(np.concatenate([ov] * 3, axis=1), BF16)


def kernel(x, c, ln_g, w_ada, b_ada, w_in, b_in, m_conv_w, m_conv_b, m_wq, m_wk, m_norm_w, m_skip, m_f_bias,
           n_pos_k, n_pos_v, n_w1_k, n_w2_k, n_w1_v, n_w2_v, w_out, final_g):
    out_dtype = x.dtype
    bsz, t, d = x.shape
    depth = ln_g.shape[0]
    h_res = x.astype(F32)
    c8 = jnp.zeros((8, d), F32).at[:bsz].set(c.astype(F32))
    slopes_np = np.array([2.0 ** (-8.0 * (h + 1) / N_HEADS) for h in range(N_HEADS)], np.float32)
    rest = (slopes_np.astype(np.float64) * LOG2E).astype(np.float32)
    pieces = []
    for _ in range(3):
        pieces.append(rest.astype(jnp.bfloat16).astype(np.float32))
        rest = rest - pieces[-1]
    spieces = jnp.asarray(np.stack(pieces, axis=1).reshape(-1))
    ovt = _overlap_t(t)

    def w1cat(w1):
        w = w1.reshape(2, CMP_STRIDE, N_HD, CMP_HIDDEN)
        return jnp.concatenate([w[0], w[1]], axis=-1).astype(BF16)

    def w2pad(w2):
        return jnp.pad(w2, ((0, 0), (0, LANES - N_HD))).astype(BF16)

    for l in range(depth):
        mod = _ada(c8, w_ada[l], b_ada[l][None, :])[:bsz]
        shift, scale, gate = mod[:, None, 0:d], mod[:, None, d:2 * d], mod[:, None, 2 * d:3 * d]
        p = _inproj(h_res, ln_g[l][None, :], scale, shift,
                    _reorder_cols(w_in[l].astype(BF16)), _reorder_cols(b_in[l])[None, :])
        fb_row = jnp.zeros((1, LANES), F32).at[0, M_HEADS:2 * M_HEADS].set(m_f_bias[l])
        y_m = _mlstm(p, m_conv_w[l], m_conv_b[l][None, :], m_wq[l].astype(BF16), m_wk[l].astype(BF16),
                     m_norm_w[l][None, :], m_skip[l][None, :], fb_row)
        kcmp, vcmpt = _compress(p, n_pos_k[l].reshape(1, -1), n_pos_v[l].reshape(1, -1),
                                n_w1_k[l].astype(BF16), w1cat(n_w1_k[l]), w2pad(n_w2_k[l]),
                                n_w1_v[l].astype(BF16), w1cat(n_w1_v[l]), w2pad(n_w2_v[l]))
        ks, vs, kw, vw = _relayout(p)
        y_n = _nsa(p, spieces, kcmp, vcmpt, ovt, ks, vs, kw, vw)
        h_res = _outproj(y_m, y_n, w_out[l].astype(BF16), h_res, gate, final_g[None, :], l == depth - 1)
    return h_res.astype(out_dtype)
```

```python
import functools

import numpy as np
import jax
import jax.numpy as jnp
from jax import lax
from jax.experimental import pallas as pl
from jax.experimental.pallas import tpu as pltpu

F32 = jnp.float32
BF16 = jnp.bfloat16
HIGHEST = lax.Precision.HIGHEST

EPS = 1e-6
M_HEADS = 4
M_HD = 256
M_WIDTH = M_HEADS * M_HD
CONV_K = 4
M_CHUNK = 256
N_HEADS = 16
N_HD = 64
N_KV = 4
N_HPG = N_HEADS // N_KV
N_WIDTH = N_HEADS * N_HD
KV_W = N_KV * N_HD
CMP_LEN = 32
CMP_STRIDE = 16
CMP_HIDDEN = 2 * N_HD
SLC_LEN = 64
SLC_TOPN = 16
WIN = 512
QBLK = 256
SKT = 256
WKT = 256
PAD_TILES = WIN // SKT
SEL_GRP = 8
KA_SEL = 128
KA_WIN = 128
VROWS = 80
UNROLL = 4
AUX_ROWS = 16
LOG2E = 1.4426950408889634

COL_MX, COL_MV, COL_MO, COL_MZ = 0, 1024, 2048, 3072
COL_NQ = 4096
COL_KC, COL_VC, COL_KS, COL_VS, COL_KW, COL_VW = 5120, 5376, 5632, 5888, 6144, 6400
COL_NZ = 6656
COL_GATES = 7680
GATE_NG = 2 * M_HEADS
NP_PAD = 8192
LANES = 128
NEG = -1e30
VMEM_LIMIT = 56 * 1024 * 1024


def _cparams(sem):
    return pltpu.CompilerParams(dimension_semantics=sem, vmem_limit_bytes=VMEM_LIMIT)


def _silu(x):
    return x * jax.nn.sigmoid(x)


def _log_sigmoid(x):
    return jnp.minimum(x, 0.0) - jnp.log1p(jnp.exp(-jnp.abs(x)))


def _ada_kernel(c_ref, w_ref, b_ref, o_ref):
    s = _silu(c_ref[...])
    o_ref[...] = jnp.dot(s, w_ref[...], precision=HIGHEST, preferred_element_type=F32) + b_ref[...]


def _ada(c8, w, b):
    d, n = w.shape
    tn = 1024
    return pl.pallas_call(
        _ada_kernel,
        grid=(n // tn,),
        in_specs=[pl.BlockSpec((8, d), lambda j: (0, 0)),
                  pl.BlockSpec((d, tn), lambda j: (0, j)),
                  pl.BlockSpec((1, tn), lambda j: (0, j))],
        out_specs=pl.BlockSpec((8, tn), lambda j: (0, j)),
        out_shape=jax.ShapeDtypeStruct((8, n), F32),
        compiler_params=_cparams(("parallel",)),
        name="ada_mod",
    )(c8, w, b)


def _inproj_kernel(x_ref, g_ref, sc_ref, sh_ref, w_ref, b_ref, o_ref, h_ref):
    @pl.when(pl.program_id(2) == 0)
    def _():
        x = x_ref[0]
        ms = jnp.mean(x * x, axis=-1, keepdims=True)
        h = x * lax.rsqrt(ms + EPS) * g_ref[...]
        h = h * (1.0 + sc_ref[0]) + sh_ref[0]
        h_ref[...] = h.astype(BF16)

    o_ref[0] = jnp.dot(h_ref[...], w_ref[...], preferred_element_type=F32) + b_ref[...]


def _inproj(x, g, scale, shift, w, b):
    bsz, t, d = x.shape
    n = w.shape[1]
    tm, tn = 1024, 1024
    return pl.pallas_call(
        _inproj_kernel,
        grid=(bsz, t // tm, n // tn),
        in_specs=[pl.BlockSpec((1, tm, d), lambda bi, i, j: (bi, i, 0)),
                  pl.BlockSpec((1, d), lambda bi, i, j: (0, 0)),
                  pl.BlockSpec((1, 1, d), lambda bi, i, j: (bi, 0, 0)),
                  pl.BlockSpec((1, 1, d), lambda bi, i, j: (bi, 0, 0)),
                  pl.BlockSpec((d, tn), lambda bi, i, j: (0, j)),
                  pl.BlockSpec((1, tn), lambda bi, i, j: (0, j))],
        out_specs=pl.BlockSpec((1, tm, tn), lambda bi, i, j: (bi, i, j)),
        out_shape=jax.ShapeDtypeStruct((bsz, t, n), F32),
        scratch_shapes=[pltpu.VMEM((tm, d), BF16)],
        compiler_params=_cparams(("parallel", "parallel", "arbitrary")),
        name="norm_inproj",
    )(x, g, scale, shift, w, b)


def _mlstm_kernel(x_ref, v_ref, o_ref, z_ref, gt_ref, cw_ref, cb_ref, wq_ref, wk_ref, nw_ref, sk_ref, fb_ref,
                  y_ref, c_scr, n_scr, m_scr, xp_scr):
    L = M_CHUNK

    @pl.when(pl.program_id(1) == 0)
    def _():
        c_scr[...] = jnp.zeros_like(c_scr)
        n_scr[...] = jnp.zeros_like(n_scr)
        m_scr[...] = jnp.zeros_like(m_scr)
        xp_scr[...] = jnp.zeros_like(xp_scr)

    x = x_ref[0]
    prev = xp_scr[...]
    row8 = lax.broadcasted_iota(jnp.int32, (8, M_WIDTH), 0)
    cw = cw_ref[...]
    xc = cb_ref[...] + x * cw[CONV_K - 1:CONV_K, :]
    for sft in range(1, CONV_K):
        xr = pltpu.roll(x, sft, 0)
        top = jnp.where(row8 < sft, pltpu.roll(prev, sft, 0), xr[0:8])
        xs = jnp.concatenate([top, xr[8:]], axis=0)
        xc = xc + xs * cw[CONV_K - 1 - sft:CONV_K - sft, :]
    xp_scr[...] = x[L - 8:L]
    xc = _silu(xc)

    gt = gt_ref[0]
    col = lax.broadcasted_iota(jnp.int32, (L, LANES), 1)
    logf = _log_sigmoid(gt + fb_ref[...])
    a_c = jnp.where((col >= M_HEADS) & (col < 2 * M_HEADS), logf, gt)
    ri = lax.broadcasted_iota(jnp.int32, (L, L), 0)
    ci = lax.broadcasted_iota(jnp.int32, (L, L), 1)
    causal = ri >= ci
    tri = causal.astype(F32)
    tri_t = (ri <= ci).astype(F32)
    b_c = jnp.dot(tri, a_c, precision=HIGHEST, preferred_element_type=F32)
    a_r = a_c.T
    b_r = jnp.dot(a_r[0:8], tri_t, precision=HIGHEST, preferred_element_type=F32)

    for h in range(M_HEADS):
        sl = slice(h * M_HD, (h + 1) * M_HD)
        xh = xc[:, sl]
        xb = xh.astype(BF16)
        q = jnp.dot(xb, wq_ref[h], preferred_element_type=F32)
        k = jnp.dot(xb, wk_ref[h], preferred_element_type=F32) * (M_HD ** -0.5)
        vb = v_ref[0, :, sl].astype(BF16)
        qb = q.astype(BF16)
        kb = k.astype(BF16)

        bt = b_c[:, M_HEADS + h:M_HEADS + h + 1]
        ic = a_c[:, h:h + 1]
        bs = b_r[M_HEADS + h:M_HEADS + h + 1, :]
        ir = a_r[h:h + 1, :]
        m_prev = m_scr[h][:, 0:1]

        dm = jnp.where(causal, bt - bs + ir, -jnp.inf)
        inter = bt + m_prev
        m_t = jnp.maximum(inter, jnp.max(dm, axis=-1, keepdims=True))
        w_in = jnp.exp(dm - m_t)
        w_st = jnp.exp(inter - m_t)
        s = lax.dot_general(qb, kb, (((1,), (1,)), ((), ())), preferred_element_type=F32) * w_in
        cmat = c_scr[h]
        nvec = n_scr[h]
        num = w_st * jnp.dot(qb, cmat.astype(BF16), preferred_element_type=F32) \
            + jnp.dot(s.astype(BF16), vb, preferred_element_type=F32)
        den = w_st * jnp.sum(q * nvec, axis=-1, keepdims=True) + jnp.sum(s, axis=-1, keepdims=True)
        hh = num / jnp.maximum(jnp.abs(den), jnp.exp(-m_t))

        b_last = bt[L - 1:L, :]
        w_end = b_last - bt + ic
        m_new = jnp.maximum(b_last + m_prev, jnp.max(w_end, axis=0, keepdims=True))
        decay = jnp.exp(b_last + m_prev - m_new)
        kwt = k * jnp.exp(w_end - m_new)
        c_scr[h] = decay * cmat + lax.dot_general(kwt.astype(BF16), vb, (((0,), (0,)), ((), ())),
                                                  preferred_element_type=F32)
        n_scr[h] = decay * nvec + jnp.sum(kwt, axis=0, keepdims=True)
        m_scr[h] = jnp.broadcast_to(m_new, (1, LANES))

        mu = jnp.mean(hh, axis=-1, keepdims=True)
        hc = hh - mu
        var = jnp.mean(hc * hc, axis=-1, keepdims=True)
        hn = hc * lax.rsqrt(var + EPS) * nw_ref[:, sl]
        out = jax.nn.sigmoid(o_ref[0, :, sl]) * hn + sk_ref[:, sl] * xh
        y_ref[0, :, sl] = (out * _silu(z_ref[0, :, sl])).astype(BF16)


def _mlstm(p, conv_w, conv_b, wq, wk, norm_w, skip, fb_row):
    bsz, t, _ = p.shape
    L = M_CHUNK
    cb = lambda c: (lambda bi, i: (bi, i, c))
    full2 = lambda bi, i: (0, 0)
    full3 = lambda bi, i: (0, 0, 0)
    return pl.pallas_call(
        _mlstm_kernel,
        grid=(bsz, t // L),
        in_specs=[pl.BlockSpec((1, L, M_WIDTH), cb(COL_MX // M_WIDTH)),
                  pl.BlockSpec((1, L, M_WIDTH), cb(COL_MV // M_WIDTH)),
                  pl.BlockSpec((1, L, M_WIDTH), cb(COL_MO // M_WIDTH)),
                  pl.BlockSpec((1, L, M_WIDTH), cb(COL_MZ // M_WIDTH)),
                  pl.BlockSpec((1, L, LANES), cb(COL_GATES // LANES)),
                  pl.BlockSpec((CONV_K, M_WIDTH), full2),
                  pl.BlockSpec((1, M_WIDTH), full2),
                  pl.BlockSpec((M_HEADS, M_HD, M_HD), full3),
                  pl.BlockSpec((M_HEADS, M_HD, M_HD), full3),
                  pl.BlockSpec((1, M_WIDTH), full2),
                  pl.BlockSpec((1, M_WIDTH), full2),
                  pl.BlockSpec((1, LANES), full2)],
        out_specs=pl.BlockSpec((1, L, M_WIDTH), lambda bi, i: (bi, i, 0)),
        out_shape=jax.ShapeDtypeStruct((bsz, t, M_WIDTH), BF16),
        scratch_shapes=[pltpu.VMEM((M_HEADS, M_HD, M_HD), F32),
                        pltpu.VMEM((M_HEADS, 1, M_HD), F32),
                        pltpu.VMEM((M_HEADS, 1, LANES), F32),
                        pltpu.VMEM((8, M_WIDTH), F32)],
        compiler_params=_cparams(("parallel", "arbitrary")),
        name="mlstm_group",
    )(p, p, p, p, p, conv_w, conv_b, wq, wk, norm_w, skip, fb_row)


def _compress_kernel(kc0_ref, kc1_ref, vc0_ref, vc1_ref, posk_ref, posv_ref, w1k_ref, w1kc_ref, w2k_ref,
                     w1v_ref, w1vc_ref, w2v_ref, kcmp_ref, vcmpt_ref):
    nb = kcmp_ref.shape[2]

    def hidden(src_refs, pos_ref, w1_ref, w1c_ref):
        acc = jnp.zeros((N_KV * nb, 2 * CMP_HIDDEN), F32)
        for l in range(CMP_STRIDE):
            xl = [r[0, pl.ds(l, nb, stride=CMP_STRIDE), :] for r in src_refs]
            xs = jnp.concatenate([x[:, g * N_HD:(g + 1) * N_HD] for x in xl for g in range(LANES // N_HD)], axis=0)
            acc = acc + jnp.dot(xs.astype(BF16), w1c_ref[l], preferred_element_type=F32)
        first = acc[:, :CMP_HIDDEN]
        second = acc[:, CMP_HIDDEN:]
        posb = jnp.dot(jnp.broadcast_to(pos_ref[...], (8, CMP_LEN * N_HD)).astype(BF16), w1_ref[...],
                       preferred_element_type=F32)[0:1]
        hid = first + pltpu.roll(second, N_KV * nb - 1, 0) + posb
        return jax.nn.gelu(hid).astype(BF16)

    hk = hidden((kc0_ref, kc1_ref), posk_ref, w1k_ref, w1kc_ref)
    kc = jnp.dot(hk, w2k_ref[...], preferred_element_type=F32)
    hv = hidden((vc0_ref, vc1_ref), posv_ref, w1v_ref, w1vc_ref)
    vc = jnp.dot(hv, w2v_ref[...], preferred_element_type=F32)
    j = lax.broadcasted_iota(jnp.int32, (nb, 1), 0)
    lane = lax.broadcasted_iota(jnp.int32, (nb, N_HD), 1)
    n_real = (nb * CMP_STRIDE - CMP_LEN) // CMP_STRIDE + 1
    aux = jnp.where(lane < 3, (j * CMP_STRIDE).astype(F32),
                    jnp.where(lane < 6, (CMP_LEN - 1) * 0.5,
                              jnp.where((lane == 6) & (j >= n_real), 1.0, 0.0))).astype(BF16)
    ones_blk = jnp.where(lax.broadcasted_iota(jnp.int32, (VROWS - N_HD, nb), 0) == 0, 1.0, 0.0)
    for g in range(N_KV):
        kg = kc[g * nb:(g + 1) * nb, 0:N_HD]
        k_hi = kg.astype(BF16)
        k_lo = (kg - k_hi.astype(F32)).astype(BF16)
        kcmp_ref[0, g] = jnp.concatenate([k_hi, k_lo, k_hi, aux], axis=1)
        vt = vc[g * nb:(g + 1) * nb].T
        vcmpt_ref[0, g] = jnp.concatenate([vt[0:N_HD], ones_blk], axis=0).astype(BF16)


def _compress(p, posk, posv, w1k, w1kc, w2k, w1v, w1vc, w2v):
    bsz, t, _ = p.shape
    nb = t // CMP_STRIDE
    cb = lambda c: (lambda bi: (bi, 0, c))
    f2 = lambda bi: (0, 0)
    f3 = lambda bi: (0, 0, 0)
    wspecs = [pl.BlockSpec((CMP_LEN * N_HD, CMP_HIDDEN), f2),
              pl.BlockSpec((CMP_STRIDE, N_HD, 2 * CMP_HIDDEN), f3),
              pl.BlockSpec((CMP_HIDDEN, LANES), f2)]
    return pl.pallas_call(
        _compress_kernel,
        grid=(bsz,),
        in_specs=[pl.BlockSpec((1, t, LANES), cb(COL_KC // LANES)),
                  pl.BlockSpec((1, t, LANES), cb(COL_KC // LANES + 1)),
                  pl.BlockSpec((1, t, LANES), cb(COL_VC // LANES)),
                  pl.BlockSpec((1, t, LANES), cb(COL_VC // LANES + 1)),
                  pl.BlockSpec((1, CMP_LEN * N_HD), f2),
                  pl.BlockSpec((1, CMP_LEN * N_HD), f2)] + wspecs + wspecs,
        out_specs=[pl.BlockSpec((1, N_KV, nb, 4 * N_HD), lambda bi: (bi, 0, 0, 0)),
                   pl.BlockSpec((1, N_KV, VROWS, nb), lambda bi: (bi, 0, 0, 0))],
        out_shape=[jax.ShapeDtypeStruct((bsz, N_KV, nb, 4 * N_HD), BF16),
                   jax.ShapeDtypeStruct((bsz, N_KV, VROWS, nb), BF16)],
        compiler_params=_cparams(("parallel",)),
        name="nsa_compress",
    )(p, p, p, p, posk, posv, w1k, w1kc, w2k, w1v, w1vc, w2v)


def _relayout_kernel(ks_ref, vs_ref, kw_ref, vw_ref, ksa_ref, vsa_ref, kwa_ref, vwa_ref):
    i = pl.program_id(1)
    is_pad = i < PAD_TILES
    base = (i - PAD_TILES) * SKT
    row = lax.broadcasted_iota(jnp.int32, (SKT, 1), 0)
    pos = jnp.where(is_pad, 0, base + row)
    blk = pos // SLC_LEN
    p_hi = (blk * SLC_LEN).astype(F32)
    p_lo = (pos - blk * SLC_LEN).astype(F32)
    flag = jnp.where(is_pad, 1.0, 0.0)
    lane = lax.broadcasted_iota(jnp.int32, (SKT, N_HD), 1)
    al = lane - AUX_ROWS
    mid_w = jnp.where((al >= 0) & (al < 3), p_hi,
                      jnp.where((al >= 3) & (al < 6), p_lo, jnp.where(al == 6, flag, 0.0)))
    mid_s = jnp.where(lane == blk % SEL_GRP, 1.0, mid_w)
    ks = ks_ref[0]
    kw = kw_ref[0]
    for g in range(N_KV):
        sl = slice(g * N_HD, (g + 1) * N_HD)
        ksa_ref[0, g] = jnp.concatenate([ks[:, sl], mid_s], axis=1).astype(BF16)
        kwa_ref[0, g] = jnp.concatenate([kw[:, sl], mid_w], axis=1).astype(BF16)
    ones_blk = jnp.where(lax.broadcasted_iota(jnp.int32, (VROWS - N_HD, SKT), 0) == 0, 1.0, 0.0)
    vst = vs_ref[0].T
    vwt = vw_ref[0].T
    for g in range(N_KV):
        sl = slice(g * N_HD, (g + 1) * N_HD)
        vsa_ref[0, g, 0] = jnp.concatenate([vst[sl], ones_blk], axis=0).astype(BF16)
        vwg = jnp.concatenate([vwt[sl], ones_blk], axis=0).astype(BF16)
        for j in range(SKT // WKT):
            vwa_ref[0, g, j] = vwg[:, j * WKT:(j + 1) * WKT]


def _relayout(p):
    bsz, t, _ = p.shape
    nt = t // SKT + PAD_TILES
    wpt = SKT // WKT
    cb = lambda c: (lambda bi, i: (bi, jnp.maximum(i - PAD_TILES, 0), c))
    return pl.pallas_call(
        _relayout_kernel,
        grid=(bsz, nt),
        in_specs=[pl.BlockSpec((1, SKT, KV_W), cb(COL_KS // KV_W)),
                  pl.BlockSpec((1, SKT, KV_W), cb(COL_VS // KV_W)),
                  pl.BlockSpec((1, SKT, KV_W), cb(COL_KW // KV_W)),
                  pl.BlockSpec((1, SKT, KV_W), cb(COL_VW // KV_W))],
        out_specs=[pl.BlockSpec((1, N_KV, SKT, KA_SEL), lambda bi, i: (bi, 0, i, 0)),
                   pl.BlockSpec((1, N_KV, 1, VROWS, SKT), lambda bi, i: (bi, 0, i, 0, 0)),
                   pl.BlockSpec((1, N_KV, SKT, KA_WIN), lambda bi, i: (bi, 0, i, 0)),
                   pl.BlockSpec((1, N_KV, wpt, VROWS, WKT), lambda bi, i: (bi, 0, i, 0, 0))],
        out_shape=[jax.ShapeDtypeStruct((bsz, N_KV, nt * SKT, KA_SEL), BF16),
                   jax.ShapeDtypeStruct((bsz, N_KV, nt, VROWS, SKT), BF16),
                   jax.ShapeDtypeStruct((bsz, N_KV, nt * SKT, KA_WIN), BF16),
                   jax.ShapeDtypeStruct((bsz, N_KV, nt * wpt, VROWS, WKT), BF16)],
        compiler_params=_cparams(("parallel", "parallel")),
        name="nsa_relayout",
    )(p, p, p, p)


def _nsa_kernel(sp_ref, q_ref, gt_ref, z_ref, kcmp_ref, vcmpt_ref, ovt_ref, ks_ref, vs_ref, kw_ref,
                vw_ref, tri_ref, wb_ref, y_ref, s_scr, g_scr, imp_scr, rank_scr, part_scr, selb_scr):
    g = pl.program_id(1)
    qb = pl.program_id(2)
    start = qb * QBLK
    nb = kcmp_ref.shape[2]
    n_slc = ovt_ref.shape[0]

    qt = (q_ref[0] * (N_HD ** -0.5)).T
    q4t = jnp.concatenate([qt[h * N_HD:(h + 1) * N_HD] for h in range(N_HPG)], axis=1)
    t_row = start + lax.broadcasted_iota(jnp.int32, (1, QBLK), 1)

    g_scr[...] = jax.nn.sigmoid(gt_ref[0]).T

    nw = N_HPG * QBLK
    hl = lax.broadcasted_iota(jnp.int32, (AUX_ROWS, nw), 1) // QBLK
    ar = lax.broadcasted_iota(jnp.int32, (AUX_ROWS, nw), 0)

    def per_head(vals):
        out = jnp.full((AUX_ROWS, nw), vals[N_HPG - 1], F32)
        for h in range(N_HPG - 2, -1, -1):
            out = jnp.where(hl == h, vals[h], out)
        return out

    pieces = [per_head([sp_ref[(g * N_HPG + h) * 3 + j] for h in range(N_HPG)]) for j in range(3)]
    aux = jnp.where(ar == 6, NEG, 0.0)
    for j in range(3):
        aux = jnp.where((ar == j) | (ar == j + 3), pieces[j], aux)
    aux_b = aux.astype(BF16)
    q4l = q4t * LOG2E
    q4s = q4l.astype(BF16)
    q4lo = (q4l - q4s.astype(F32)).astype(BF16)

    def tile4(a):
        return jnp.concatenate([a] * N_HPG, axis=1)

    def col_reduce8(s, op):
        out = s[0:8]
        for r in range(1, s.shape[0] // 8):
            out = op(out, s[8 * r:8 * r + 8])
        return out

    def col_max(s):
        return jnp.max(col_reduce8(s, jnp.maximum), axis=0, keepdims=True)

    def normalise(acc):
        den = acc[N_HD:N_HD + 1]
        return acc[0:N_HD] / jnp.where(den > 0, den, 1.0)

    q_cmp = jnp.concatenate([q4s, q4s, q4lo, aux_b, jnp.zeros((N_HD - AUX_ROWS, nw), BF16)], axis=0)
    cmp_end = lax.broadcasted_iota(jnp.int32, (nb, 1), 0) * CMP_STRIDE + (CMP_LEN - 1)
    s_c = jnp.dot(kcmp_ref[0, 0], q_cmp, preferred_element_type=F32)
    s_c = s_c + tile4(jnp.where(cmp_end <= t_row, 0.0, NEG))
    e_c = jnp.exp2(s_c - col_max(s_c))
    z_c = jnp.sum(col_reduce8(e_c, jnp.add), axis=0, keepdims=True)
    inv_c = jnp.where(tile4(t_row >= CMP_LEN - 1) & (z_c > 0), 1.0 / z_c, 0.0)
    o_c = jnp.dot(vcmpt_ref[0, 0], e_c.astype(BF16), preferred_element_type=F32)[0:N_HD] * inv_c
    p_c = e_c * inv_c
    psum = p_c[:, 0:QBLK]
    for h in range(1, N_HPG):
        psum = psum + p_c[:, h * QBLK:(h + 1) * QBLK]
    parts, rest = [], psum
    for _ in range(3):
        parts.append(rest.astype(BF16))
        rest = rest - parts[-1].astype(F32)
    imp = jnp.dot(ovt_ref[...], jnp.concatenate(parts, axis=0), preferred_element_type=F32)
    sidx = lax.broadcasted_iota(jnp.int32, (n_slc, QBLK), 0)
    cur = t_row // SLC_LEN
    valid = sidx <= cur
    imp = jnp.where(valid, imp, -jnp.inf)
    imp_scr[...] = jnp.where((sidx == 0) | (sidx == cur), jnp.inf, imp)

    wrows = WIN + QBLK
    q_win = jnp.concatenate([q4s, jnp.zeros((AUX_ROWS, nw), BF16), aux_b,
                             jnp.zeros((KA_WIN - N_HD - 2 * AUX_ROWS, nw), BF16)], axis=0)
    s_w = jnp.dot(kw_ref[0, 0, pl.ds(pl.multiple_of(start, QBLK), wrows), :], q_win,
                  preferred_element_type=F32)
    s_w = jnp.concatenate([s_w[0:WKT] + tile4(wb_ref[0]), s_w[WKT:wrows - WKT],
                           s_w[wrows - WKT:] + tile4(wb_ref[1])], axis=0)
    p_w = jnp.exp2(s_w - col_max(s_w)).astype(BF16)
    v_w = jnp.concatenate([vw_ref[0, 0, qb * (QBLK // WKT) + j] for j in range(wrows // WKT)], axis=1)
    o_w = normalise(jnp.dot(v_w, p_w, preferred_element_type=F32))

    def gate_row(h, branch):
        return g_scr[pl.ds(GATE_NG + (g * N_HPG + h) * 3 + branch, 1), :]

    for h in range(N_HPG):
        hs = slice(h * QBLK, (h + 1) * QBLK)
        part_scr[:, hs] = gate_row(h, 0) * o_c[:, hs] + gate_row(h, 2) * o_w[:, hs]

    n_top = min(SLC_TOPN, n_slc)
    rank_scr[...] = jnp.zeros_like(rank_scr)
    sub8 = lax.broadcasted_iota(jnp.int32, (8, QBLK), 0)
    last_blk = (start + QBLK - 1) // SLC_LEN
    for ri in range(n_slc // 8):
        @pl.when((last_blk >= n_top) & (8 * ri <= last_blk))
        def _():
            rows = imp_scr[8 * ri:8 * ri + 8]
            for r in range(n_slc // 8):
                blk8 = imp_scr[8 * r:8 * r + 8]
                acc = rank_scr[8 * r:8 * r + 8]
                for ii in range(8):
                    row = rows[ii:ii + 1]
                    if ri < r:
                        before = row >= blk8
                    elif ri > r:
                        before = row > blk8
                    else:
                        before = (row > blk8) | ((row == blk8) & (sub8 > ii))
                    acc = acc + jnp.where(before, 1.0, 0.0)
                rank_scr[8 * r:8 * r + 8] = acc
    selbias = jnp.where((rank_scr[...] < n_top) & valid, 0.0, NEG)

    selb_scr[...] = tile4(selbias)
    n_tiles = qb // (SKT // QBLK) + 1
    q_tail = jnp.zeros((KA_SEL - N_HD - 2 * AUX_ROWS, nw), BF16)

    def sel_scores(kt):
        grp = pl.multiple_of((kt * SKT // SLC_LEN) // SEL_GRP * SEL_GRP, SEL_GRP)
        sel_rows = jnp.concatenate([selb_scr[pl.ds(grp, SEL_GRP), :], jnp.zeros((AUX_ROWS - SEL_GRP, nw), F32)], axis=0)
        q_sel = jnp.concatenate([q4s, sel_rows.astype(BF16), aux_b, q_tail], axis=0)
        krow = pl.multiple_of((kt + PAD_TILES) * SKT, SKT)
        return jnp.dot(ks_ref[0, 0, pl.ds(krow, SKT), :], q_sel, preferred_element_type=F32)

    def pass1(kt, mrun):
        s = sel_scores(kt)
        s_scr[pl.ds(pl.multiple_of(kt * SKT, SKT), SKT), :] = s
        return jnp.maximum(mrun, col_reduce8(s, jnp.maximum))

    def unrolled(fn):
        def body(i, carry):
            for u in range(UNROLL):
                carry = fn(i * UNROLL + u, carry)
            return carry
        return body

    def run_tiles(fn, n, init):
        n_u = n // UNROLL
        carry = lax.fori_loop(0, n_u, unrolled(fn), init)
        return lax.fori_loop(n_u * UNROLL, n, fn, carry)

    mrun = run_tiles(pass1, n_tiles - 1, jnp.full((8, nw), NEG, F32))
    last = n_tiles - 1
    s_last = sel_scores(last) + tile4(tri_ref[qb % (SKT // QBLK)])
    s_scr[pl.ds(pl.multiple_of(last * SKT, SKT), SKT), :] = s_last
    m_sel = jnp.max(jnp.maximum(mrun, col_reduce8(s_last, jnp.maximum)), axis=0, keepdims=True)

    def pass2(kt, acc):
        s = s_scr[pl.ds(pl.multiple_of(kt * SKT, SKT), SKT), :]
        pr = jnp.exp2(s - m_sel).astype(BF16)
        return acc + jnp.dot(vs_ref[0, 0, kt + PAD_TILES], pr, preferred_element_type=F32)

    o_s = normalise(run_tiles(pass2, n_tiles, jnp.zeros((VROWS, nw), F32)))

    tot = [part_scr[:, h * QBLK:(h + 1) * QBLK] + gate_row(h, 1) * o_s[:, h * QBLK:(h + 1) * QBLK]
           for h in range(N_HPG)]
    o = jnp.concatenate(tot, axis=0).T
    y_ref[0] = (o * _silu(z_ref[0])).astype(BF16)


def _edge_biases():
    kl = np.arange(SKT)[:, None]
    ql = np.arange(QBLK)[None, :]
    tri = np.stack([np.where(kl <= par * QBLK + ql, 0.0, NEG) for par in range(SKT // QBLK)])
    kk = np.arange(WKT)[:, None]
    wb = np.stack([np.where(kk > ql, 0.0, NEG), np.where(kk <= ql + WKT - QBLK, 0.0, NEG)])
    return jnp.asarray(tri, F32), jnp.asarray(wb, F32)


def _nsa(p, spieces, kcmp, vcmpt, ovt, ks, vs, kw, vw):
    bsz, t, _ = p.shape
    n_slc = t // SLC_LEN
    gw = N_HPG * N_HD
    tri, wb = _edge_biases()
    per_bg = lambda a: pl.BlockSpec((1, 1) + a.shape[2:], lambda bi, g, i, sp: (bi, g) + (0,) * (a.ndim - 2))
    const = lambda a: pl.BlockSpec(a.shape, lambda bi, g, i, sp: (0,) * a.ndim)
    grid_spec = pltpu.PrefetchScalarGridSpec(
        num_scalar_prefetch=1,
        grid=(bsz, N_KV, t // QBLK),
        in_specs=[pl.BlockSpec((1, QBLK, gw), lambda bi, g, i, sp: (bi, i, COL_NQ // gw + g)),
                  pl.BlockSpec((1, QBLK, LANES), lambda bi, g, i, sp: (bi, i, COL_GATES // LANES)),
                  pl.BlockSpec((1, QBLK, gw), lambda bi, g, i, sp: (bi, i, COL_NZ // gw + g)),
                  per_bg(kcmp), per_bg(vcmpt), const(ovt),
                  per_bg(ks), per_bg(vs), per_bg(kw), per_bg(vw), const(tri), const(wb)],
        out_specs=pl.BlockSpec((1, QBLK, gw), lambda bi, g, i, sp: (bi, i, g)),
        scratch_shapes=[pltpu.VMEM((t, N_HPG * QBLK), F32),
                        pltpu.VMEM((LANES, QBLK), F32),
                        pltpu.VMEM((n_slc, QBLK), F32),
                        pltpu.VMEM((n_slc, QBLK), F32),
                        pltpu.VMEM((N_HD, N_HPG * QBLK), F32),
                        pltpu.VMEM((n_slc, N_HPG * QBLK), F32)],
    )
    return pl.pallas_call(
        _nsa_kernel,
        grid_spec=grid_spec,
        out_shape=jax.ShapeDtypeStruct((bsz, t, N_WIDTH), BF16),
        compiler_params=_cparams(("parallel", "parallel", "arbitrary")),
        name="nsa_attention",
    )(spieces, p, p, p, kcmp, vcmpt, ovt, ks, vs, kw, vw, tri, wb)


def _outproj_kernel(ym_ref, yn_ref, w_ref, x_ref, gate_ref, fg_ref, o_ref, *, final):
    y = jnp.dot(ym_ref[0], w_ref[0:M_WIDTH, :], preferred_element_type=F32)
    y = y + jnp.dot(yn_ref[0], w_ref[M_WIDTH:, :], preferred_element_type=F32)
    hres = x_ref[0] + gate_ref[0] * y
    if final:
        ms = jnp.mean(hres * hres, axis=-1, keepdims=True)
        hres = hres * lax.rsqrt(ms + EPS) * fg_ref[...]
    o_ref[0] = hres


def _outproj(ym, yn, w, x, gate, fg, final):
    bsz, t, d = x.shape
    tm = 512
    return pl.pallas_call(
        functools.partial(_outproj_kernel, final=final),
        grid=(bsz, t // tm),
        in_specs=[pl.BlockSpec((1, tm, M_WIDTH), lambda bi, i: (bi, i, 0)),
                  pl.BlockSpec((1, tm, N_WIDTH), lambda bi, i: (bi, i, 0)),
                  pl.BlockSpec((M_WIDTH + N_WIDTH, d), lambda bi, i: (0, 0)),
                  pl.BlockSpec((1, tm, d), lambda bi, i: (bi, i, 0)),
                  pl.BlockSpec((1, 1, d), lambda bi, i: (bi, 0, 0)),
                  pl.BlockSpec((1, d), lambda bi, i: (0, 0))],
        out_specs=pl.BlockSpec((1, tm, d), lambda bi, i: (bi, i, 0)),
        out_shape=jax.ShapeDtypeStruct((bsz, t, d), F32),
        compiler_params=_cparams(("parallel", "parallel")),
        name="outproj_residual",
    )(ym, yn, w, x, gate, fg)


SRC_MI = 4 * M_WIDTH
SRC_NQ = SRC_MI + 2 * M_HEADS
SRC_NG = SRC_NQ + N_WIDTH + 6 * KV_W
SRC_NZ = SRC_NG + 3 * N_HEADS


def _reorder_cols(a):
    parts = [a[..., 0:SRC_MI], a[..., SRC_NQ:SRC_NG], a[..., SRC_NZ:SRC_NZ + N_WIDTH], a[..., SRC_MI:SRC_NQ],
             a[..., SRC_NG:SRC_NZ]]
    used = sum(x.shape[-1] for x in parts)
    parts.append(jnp.zeros(a.shape[:-1] + (NP_PAD - used,), a.dtype))
    return jnp.concatenate(parts, axis=-1)


def _wprep_kernel(w_ref, o_ref):
    rows = w_ref.shape[0]
    o_ref[:, 0:COL_NQ] = w_ref[:, 0:SRC_MI].astype(BF16)
    o_ref[:, COL_NQ:COL_NZ] = w_ref[:, SRC_NQ:SRC_NG].astype(BF16)
    o_ref[:, COL_NZ:COL_GATES] = w_ref[:, SRC_NZ:SRC_NZ + N_WIDTH].astype(BF16)
    n_gate = SRC_NQ - SRC_MI + SRC_NZ - SRC_NG
    gates = jnp.concatenate([w_ref[:, SRC_MI:SRC_NQ], w_ref[:, SRC_NG:SRC_NZ],
                             jnp.zeros((rows, LANES - n_gate), F32)], axis=1)
    o_ref[:, COL_GATES:COL_GATES + LANES] = gates.astype(BF16)
    o_ref[:, COL_GATES + LANES:] = jnp.zeros((rows, NP_PAD - COL_GATES - LANES), BF16)


def _wprep(w):
    d, n = w.shape
    tr = 256
    return pl.pallas_call(
        _wprep_kernel,
        grid=(d // tr,),
        in_specs=[pl.BlockSpec((tr, n), lambda i: (i, 0))],
        out_specs=pl.BlockSpec((tr, NP_PAD), lambda i: (i, 0)),
        out_shape=jax.ShapeDtypeStruct((d, NP_PAD), BF16),
        compiler_params=_cparams(("parallel",)),
        name="inproj_weight_prep",
    )(w)


def _overlap_t(t):
    n_cmp_rows = t // CMP_STRIDE
    n_slc = t // SLC_LEN
    c0 = np.arange(n_cmp_rows) * CMP_STRIDE
    s0 = np.arange(n_slc) * SLC_LEN
    ov = (c0[None, :] <= s0[:, None] + SLC_LEN - 1) & (c0[None, :] + CMP_LEN - 1 >= s0[:, None])
    ov[:, (t - CMP_LEN) // CMP_STRIDE + 1:] = False
    return jnp.asarray(np.concatenate([ov] * 3, axis=1), BF16)


def kernel(x, c, ln_g, w_ada, b_ada, w_in, b_in, m_conv_w, m_conv_b, m_wq, m_wk, m_norm_w, m_skip, m_f_bias,
           n_pos_k, n_pos_v, n_w1_k, n_w2_k, n_w1_v, n_w2_v, w_out, final_g):
    out_dtype = x.dtype
    bsz, t, d = x.shape
    depth = ln_g.shape[0]
    h_res = x.astype(F32)
    c8 = jnp.zeros((8, d), F32).at[:bsz].set(c.astype(F32))
    slopes_np = np.array([2.0 ** (-8.0 * (h + 1) / N_HEADS) for h in range(N_HEADS)], np.float32)
    rest = (slopes_np.astype(np.float64) * LOG2E).astype(np.float32)
    pieces = []
    for _ in range(3):
        pieces.append(rest.astype(jnp.bfloat16).astype(np.float32))
        rest = rest - pieces[-1]
    spieces = jnp.asarray(np.stack(pieces, axis=1).reshape(-1))
    ovt = _overlap_t(t)

    def w1cat(w1):
        w = w1.reshape(2, CMP_STRIDE, N_HD, CMP_HIDDEN)
        return jnp.concatenate([w[0], w[1]], axis=-1).astype(BF16)

    def w2pad(w2):
        return jnp.pad(w2, ((0, 0), (0, LANES - N_HD))).astype(BF16)

    for l in range(depth):
        mod = _ada(c8, w_ada[l], b_ada[l][None, :])[:bsz]
        shift, scale, gate = mod[:, None, 0:d], mod[:, None, d:2 * d], mod[:, None, 2 * d:3 * d]
        p = _inproj(h_res, ln_g[l][None, :], scale, shift,
                    _wprep(w_in[l]), _reorder_cols(b_in[l])[None, :])
        fb_row = jnp.zeros((1, LANES), F32).at[0, M_HEADS:2 * M_HEADS].set(m_f_bias[l])
        y_m = _mlstm(p, m_conv_w[l], m_conv_b[l][None, :], m_wq[l].astype(BF16), m_wk[l].astype(BF16),
                     m_norm_w[l][None, :], m_skip[l][None, :], fb_row)
        kcmp, vcmpt = _compress(p, n_pos_k[l].reshape(1, -1), n_pos_v[l].reshape(1, -1),
                                n_w1_k[l].astype(BF16), w1cat(n_w1_k[l]), w2pad(n_w2_k[l]),
                                n_w1_v[l].astype(BF16), w1cat(n_w1_v[l]), w2pad(n_w2_v[l]))
        ks, vs, kw, vw = _relayout(p)
        y_n = _nsa(p, spieces, kcmp, vcmpt, ovt, ks, vs, kw, vw)
        h_res = _outproj(y_m, y_n, w_out[l].astype(BF16), h_res, gate, final_g[None, :], l == depth - 1)
    return h_res.astype(out_dtype)
```

```python
import functools

import numpy as np
import jax
import jax.numpy as jnp
from jax import lax
from jax.experimental import pallas as pl
from jax.experimental.pallas import tpu as pltpu

F32 = jnp.float32
BF16 = jnp.bfloat16
HIGHEST = lax.Precision.HIGHEST

EPS = 1e-6
M_HEADS = 4
M_HD = 256
M_WIDTH = M_HEADS * M_HD
CONV_K = 4
M_CHUNK = 256
N_HEADS = 16
N_HD = 64
N_KV = 4
N_HPG = N_HEADS // N_KV
N_WIDTH = N_HEADS * N_HD
KV_W = N_KV * N_HD
CMP_LEN = 32
CMP_STRIDE = 16
CMP_HIDDEN = 2 * N_HD
SLC_LEN = 64
SLC_TOPN = 16
WIN = 512
QBLK = 256
SKT = 256
WKT = 256
PAD_TILES = WIN // SKT
SEL_GRP = 8
KA_SEL = 128
KA_WIN = 128
VROWS = 80
UNROLL = 4
AUX_ROWS = 16
LOG2E = 1.4426950408889634

COL_MX, COL_MV, COL_MO, COL_MZ = 0, 1024, 2048, 3072
COL_NQ = 4096
COL_KC, COL_VC, COL_KS, COL_VS, COL_KW, COL_VW = 5120, 5376, 5632, 5888, 6144, 6400
COL_NZ = 6656
COL_GATES = 7680
GATE_NG = 2 * M_HEADS
NP_PAD = 8192
NORM_CHUNKS = 4
LANES = 128
NEG = -1e30
VMEM_LIMIT = 56 * 1024 * 1024


def _cparams(sem):
    return pltpu.CompilerParams(dimension_semantics=sem, vmem_limit_bytes=VMEM_LIMIT)


def _silu(x):
    return x * jax.nn.sigmoid(x)


def _log_sigmoid(x):
    return jnp.minimum(x, 0.0) - jnp.log1p(jnp.exp(-jnp.abs(x)))


def _ada_kernel(c_ref, w_ref, b_ref, o_ref):
    s = _silu(c_ref[...])
    o_ref[...] = jnp.dot(s, w_ref[0], precision=HIGHEST, preferred_element_type=F32) + b_ref[...]


def _ada(c8, w, b, layer):
    _, d, n = w.shape
    tn = 1024
    return pl.pallas_call(
        _ada_kernel,
        grid=(n // tn,),
        in_specs=[pl.BlockSpec((8, d), lambda j: (0, 0)),
                  pl.BlockSpec((1, d, tn), lambda j: (layer, 0, j)),
                  pl.BlockSpec((1, tn), lambda j: (0, j))],
        out_specs=pl.BlockSpec((8, tn), lambda j: (0, j)),
        out_shape=jax.ShapeDtypeStruct((8, n), F32),
        compiler_params=_cparams(("parallel",)),
        name="ada_mod",
    )(c8, w, b)


def _inproj_kernel(x_ref, g_ref, sc_ref, sh_ref, w_ref, b_ref, o_ref, h_ref):
    first = pl.program_id(2) == 0

    @pl.when(first)
    def _():
        tm = x_ref.shape[1]
        ck = tm // NORM_CHUNKS
        for c in range(NORM_CHUNKS):
            rows = slice(c * ck, (c + 1) * ck)
            x = x_ref[0, rows, :]
            ms = jnp.mean(x * x, axis=-1, keepdims=True)
            h = x * lax.rsqrt(ms + EPS) * g_ref[...]
            h = (h * (1.0 + sc_ref[0]) + sh_ref[0]).astype(BF16)
            h_ref[rows, :] = h
            o_ref[0, rows, :] = jnp.dot(h, w_ref[...], preferred_element_type=F32) + b_ref[...]

    @pl.when(jnp.logical_not(first))
    def _():
        o_ref[0] = jnp.dot(h_ref[...], w_ref[...], preferred_element_type=F32) + b_ref[...]


def _inproj(x, g, scale, shift, w, b):
    bsz, t, d = x.shape
    n = w.shape[1]
    tm, tn = 1024, 1024
    return pl.pallas_call(
        _inproj_kernel,
        grid=(bsz, t // tm, n // tn),
        in_specs=[pl.BlockSpec((1, tm, d), lambda bi, i, j: (bi, i, 0)),
                  pl.BlockSpec((1, d), lambda bi, i, j: (0, 0)),
                  pl.BlockSpec((1, 1, d), lambda bi, i, j: (bi, 0, 0)),
                  pl.BlockSpec((1, 1, d), lambda bi, i, j: (bi, 0, 0)),
                  pl.BlockSpec((d, tn), lambda bi, i, j: (0, j)),
                  pl.BlockSpec((1, tn), lambda bi, i, j: (0, j))],
        out_specs=pl.BlockSpec((1, tm, tn), lambda bi, i, j: (bi, i, j)),
        out_shape=jax.ShapeDtypeStruct((bsz, t, n), F32),
        scratch_shapes=[pltpu.VMEM((tm, d), BF16)],
        compiler_params=_cparams(("parallel", "parallel", "arbitrary")),
        name="norm_inproj",
    )(x, g, scale, shift, w, b)


def _mlstm_kernel(x_ref, v_ref, o_ref, z_ref, gt_ref, cw_ref, cb_ref, wq_ref, wk_ref, nw_ref, sk_ref, fb_ref,
                  y_ref, c_scr, n_scr, m_scr, xp_scr):
    L = M_CHUNK

    @pl.when(pl.program_id(1) == 0)
    def _():
        c_scr[...] = jnp.zeros_like(c_scr)
        n_scr[...] = jnp.zeros_like(n_scr)
        m_scr[...] = jnp.zeros_like(m_scr)
        xp_scr[...] = jnp.zeros_like(xp_scr)

    x = x_ref[0]
    prev = xp_scr[...]
    row8 = lax.broadcasted_iota(jnp.int32, (8, M_WIDTH), 0)
    cw = cw_ref[...]
    xc = cb_ref[...] + x * cw[CONV_K - 1:CONV_K, :]
    for sft in range(1, CONV_K):
        xr = pltpu.roll(x, sft, 0)
        top = jnp.where(row8 < sft, pltpu.roll(prev, sft, 0), xr[0:8])
        xs = jnp.concatenate([top, xr[8:]], axis=0)
        xc = xc + xs * cw[CONV_K - 1 - sft:CONV_K - sft, :]
    xp_scr[...] = x[L - 8:L]
    xc = _silu(xc)

    gt = gt_ref[0]
    col = lax.broadcasted_iota(jnp.int32, (L, LANES), 1)
    logf = _log_sigmoid(gt + fb_ref[...])
    a_c = jnp.where((col >= M_HEADS) & (col < 2 * M_HEADS), logf, gt)
    ri = lax.broadcasted_iota(jnp.int32, (L, L), 0)
    ci = lax.broadcasted_iota(jnp.int32, (L, L), 1)
    causal = ri >= ci
    tri = causal.astype(F32)
    tri_t = (ri <= ci).astype(F32)
    b_c = jnp.dot(tri, a_c, precision=HIGHEST, preferred_element_type=F32)
    a_r = a_c.T
    b_r = jnp.dot(a_r[0:8], tri_t, precision=HIGHEST, preferred_element_type=F32)

    for h in range(M_HEADS):
        sl = slice(h * M_HD, (h + 1) * M_HD)
        xh = xc[:, sl]
        xb = xh.astype(BF16)
        q = jnp.dot(xb, wq_ref[h], preferred_element_type=F32)
        k = jnp.dot(xb, wk_ref[h], preferred_element_type=F32) * (M_HD ** -0.5)
        vb = v_ref[0, :, sl].astype(BF16)
        qb = q.astype(BF16)
        kb = k.astype(BF16)

        bt = b_c[:, M_HEADS + h:M_HEADS + h + 1]
        ic = a_c[:, h:h + 1]
        bs = b_r[M_HEADS + h:M_HEADS + h + 1, :]
        ir = a_r[h:h + 1, :]
        m_prev = m_scr[h][:, 0:1]

        dm = jnp.where(causal, bt - bs + ir, -jnp.inf)
        inter = bt + m_prev
        m_t = jnp.maximum(inter, jnp.max(dm, axis=-1, keepdims=True))
        w_in = jnp.exp(dm - m_t)
        w_st = jnp.exp(inter - m_t)
        s = lax.dot_general(qb, kb, (((1,), (1,)), ((), ())), preferred_element_type=F32) * w_in
        cmat = c_scr[h]
        nvec = n_scr[h]
        num = w_st * jnp.dot(qb, cmat.astype(BF16), preferred_element_type=F32) \
            + jnp.dot(s.astype(BF16), vb, preferred_element_type=F32)
        den = w_st * jnp.sum(q * nvec, axis=-1, keepdims=True) + jnp.sum(s, axis=-1, keepdims=True)
        hh = num / jnp.maximum(jnp.abs(den), jnp.exp(-m_t))

        b_last = bt[L - 1:L, :]
        w_end = b_last - bt + ic
        m_new = jnp.maximum(b_last + m_prev, jnp.max(w_end, axis=0, keepdims=True))
        decay = jnp.exp(b_last + m_prev - m_new)
        kwt = k * jnp.exp(w_end - m_new)
        c_scr[h] = decay * cmat + lax.dot_general(kwt.astype(BF16), vb, (((0,), (0,)), ((), ())),
                                                  preferred_element_type=F32)
        n_scr[h] = decay * nvec + jnp.sum(kwt, axis=0, keepdims=True)
        m_scr[h] = jnp.broadcast_to(m_new, (1, LANES))

        mu = jnp.mean(hh, axis=-1, keepdims=True)
        hc = hh - mu
        var = jnp.mean(hc * hc, axis=-1, keepdims=True)
        hn = hc * lax.rsqrt(var + EPS) * nw_ref[:, sl]
        out = jax.nn.sigmoid(o_ref[0, :, sl]) * hn + sk_ref[:, sl] * xh
        y_ref[0, :, sl] = (out * _silu(z_ref[0, :, sl])).astype(BF16)


def _mlstm(p, conv_w, conv_b, wq, wk, norm_w, skip, fb_row):
    bsz, t, _ = p.shape
    L = M_CHUNK
    cb = lambda c: (lambda bi, i: (bi, i, c))
    full2 = lambda bi, i: (0, 0)
    full3 = lambda bi, i: (0, 0, 0)
    return pl.pallas_call(
        _mlstm_kernel,
        grid=(bsz, t // L),
        in_specs=[pl.BlockSpec((1, L, M_WIDTH), cb(COL_MX // M_WIDTH)),
                  pl.BlockSpec((1, L, M_WIDTH), cb(COL_MV // M_WIDTH)),
                  pl.BlockSpec((1, L, M_WIDTH), cb(COL_MO // M_WIDTH)),
                  pl.BlockSpec((1, L, M_WIDTH), cb(COL_MZ // M_WIDTH)),
                  pl.BlockSpec((1, L, LANES), cb(COL_GATES // LANES)),
                  pl.BlockSpec((CONV_K, M_WIDTH), full2),
                  pl.BlockSpec((1, M_WIDTH), full2),
                  pl.BlockSpec((M_HEADS, M_HD, M_HD), full3),
                  pl.BlockSpec((M_HEADS, M_HD, M_HD), full3),
                  pl.BlockSpec((1, M_WIDTH), full2),
                  pl.BlockSpec((1, M_WIDTH), full2),
                  pl.BlockSpec((1, LANES), full2)],
        out_specs=pl.BlockSpec((1, L, M_WIDTH), lambda bi, i: (bi, i, 0)),
        out_shape=jax.ShapeDtypeStruct((bsz, t, M_WIDTH), BF16),
        scratch_shapes=[pltpu.VMEM((M_HEADS, M_HD, M_HD), F32),
                        pltpu.VMEM((M_HEADS, 1, M_HD), F32),
                        pltpu.VMEM((M_HEADS, 1, LANES), F32),
                        pltpu.VMEM((8, M_WIDTH), F32)],
        compiler_params=_cparams(("parallel", "arbitrary")),
        name="mlstm_group",
    )(p, p, p, p, p, conv_w, conv_b, wq, wk, norm_w, skip, fb_row)


def _compress_kernel(kc0_ref, kc1_ref, vc0_ref, vc1_ref, posk_ref, posv_ref, w1k_ref, w1kc_ref, w2k_ref,
                     w1v_ref, w1vc_ref, w2v_ref, kcmp_ref, vcmpt_ref):
    nb = kcmp_ref.shape[2]

    def hidden(src_refs, pos_ref, w1_ref, w1c_ref):
        acc = jnp.zeros((N_KV * nb, 2 * CMP_HIDDEN), F32)
        for l in range(CMP_STRIDE):
            xl = [r[0, pl.ds(l, nb, stride=CMP_STRIDE), :] for r in src_refs]
            xs = jnp.concatenate([x[:, g * N_HD:(g + 1) * N_HD] for x in xl for g in range(LANES // N_HD)], axis=0)
            acc = acc + jnp.dot(xs.astype(BF16), w1c_ref[l], preferred_element_type=F32)
        first = acc[:, :CMP_HIDDEN]
        second = acc[:, CMP_HIDDEN:]
        posb = jnp.dot(jnp.broadcast_to(pos_ref[...], (8, CMP_LEN * N_HD)).astype(BF16), w1_ref[...],
                       preferred_element_type=F32)[0:1]
        hid = first + pltpu.roll(second, N_KV * nb - 1, 0) + posb
        return jax.nn.gelu(hid).astype(BF16)

    hk = hidden((kc0_ref, kc1_ref), posk_ref, w1k_ref, w1kc_ref)
    kc = jnp.dot(hk, w2k_ref[...], preferred_element_type=F32)
    hv = hidden((vc0_ref, vc1_ref), posv_ref, w1v_ref, w1vc_ref)
    vc = jnp.dot(hv, w2v_ref[...], preferred_element_type=F32)
    j = lax.broadcasted_iota(jnp.int32, (nb, 1), 0)
    lane = lax.broadcasted_iota(jnp.int32, (nb, N_HD), 1)
    n_real = (nb * CMP_STRIDE - CMP_LEN) // CMP_STRIDE + 1
    aux = jnp.where(lane < 3, (j * CMP_STRIDE).astype(F32),
                    jnp.where(lane < 6, (CMP_LEN - 1) * 0.5,
                              jnp.where((lane == 6) & (j >= n_real), 1.0, 0.0))).astype(BF16)
    ones_blk = jnp.where(lax.broadcasted_iota(jnp.int32, (VROWS - N_HD, nb), 0) == 0, 1.0, 0.0)
    for g in range(N_KV):
        kg = kc[g * nb:(g + 1) * nb, 0:N_HD]
        k_hi = kg.astype(BF16)
        k_lo = (kg - k_hi.astype(F32)).astype(BF16)
        kcmp_ref[0, g] = jnp.concatenate([k_hi, k_lo, k_hi, aux], axis=1)
        vt = vc[g * nb:(g + 1) * nb].T
        vcmpt_ref[0, g] = jnp.concatenate([vt[0:N_HD], ones_blk], axis=0).astype(BF16)


def _compress(p, posk, posv, w1k, w1kc, w2k, w1v, w1vc, w2v):
    bsz, t, _ = p.shape
    nb = t // CMP_STRIDE
    cb = lambda c: (lambda bi: (bi, 0, c))
    f2 = lambda bi: (0, 0)
    f3 = lambda bi: (0, 0, 0)
    wspecs = [pl.BlockSpec((CMP_LEN * N_HD, CMP_HIDDEN), f2),
              pl.BlockSpec((CMP_STRIDE, N_HD, 2 * CMP_HIDDEN), f3),
              pl.BlockSpec((CMP_HIDDEN, LANES), f2)]
    return pl.pallas_call(
        _compress_kernel,
        grid=(bsz,),
        in_specs=[pl.BlockSpec((1, t, LANES), cb(COL_KC // LANES)),
                  pl.BlockSpec((1, t, LANES), cb(COL_KC // LANES + 1)),
                  pl.BlockSpec((1, t, LANES), cb(COL_VC // LANES)),
                  pl.BlockSpec((1, t, LANES), cb(COL_VC // LANES + 1)),
                  pl.BlockSpec((1, CMP_LEN * N_HD), f2),
                  pl.BlockSpec((1, CMP_LEN * N_HD), f2)] + wspecs + wspecs,
        out_specs=[pl.BlockSpec((1, N_KV, nb, 4 * N_HD), lambda bi: (bi, 0, 0, 0)),
                   pl.BlockSpec((1, N_KV, VROWS, nb), lambda bi: (bi, 0, 0, 0))],
        out_shape=[jax.ShapeDtypeStruct((bsz, N_KV, nb, 4 * N_HD), BF16),
                   jax.ShapeDtypeStruct((bsz, N_KV, VROWS, nb), BF16)],
        compiler_params=_cparams(("parallel",)),
        name="nsa_compress",
    )(p, p, p, p, posk, posv, w1k, w1kc, w2k, w1v, w1vc, w2v)


def _relayout_kernel(ks_ref, vs_ref, kw_ref, vw_ref, ksa_ref, vsa_ref, kwa_ref, vwa_ref):
    i = pl.program_id(1)
    is_pad = i < PAD_TILES
    base = (i - PAD_TILES) * SKT
    row = lax.broadcasted_iota(jnp.int32, (SKT, 1), 0)
    pos = jnp.where(is_pad, 0, base + row)
    blk = pos // SLC_LEN
    p_hi = (blk * SLC_LEN).astype(F32)
    p_lo = (pos - blk * SLC_LEN).astype(F32)
    flag = jnp.where(is_pad, 1.0, 0.0)
    lane = lax.broadcasted_iota(jnp.int32, (SKT, N_HD), 1)
    al = lane - AUX_ROWS
    mid_w = jnp.where((al >= 0) & (al < 3), p_hi,
                      jnp.where((al >= 3) & (al < 6), p_lo, jnp.where(al == 6, flag, 0.0)))
    mid_s = jnp.where(lane == blk % SEL_GRP, 1.0, mid_w)
    ks = ks_ref[0]
    kw = kw_ref[0]
    for g in range(N_KV):
        sl = slice(g * N_HD, (g + 1) * N_HD)
        ksa_ref[0, g] = jnp.concatenate([ks[:, sl], mid_s], axis=1).astype(BF16)
        kwa_ref[0, g] = jnp.concatenate([kw[:, sl], mid_w], axis=1).astype(BF16)
    ones_blk = jnp.where(lax.broadcasted_iota(jnp.int32, (VROWS - N_HD, SKT), 0) == 0, 1.0, 0.0)
    vst = vs_ref[0].T
    vwt = vw_ref[0].T
    for g in range(N_KV):
        sl = slice(g * N_HD, (g + 1) * N_HD)
        vsa_ref[0, g, 0] = jnp.concatenate([vst[sl], ones_blk], axis=0).astype(BF16)
        vwg = jnp.concatenate([vwt[sl], ones_blk], axis=0).astype(BF16)
        for j in range(SKT // WKT):
            vwa_ref[0, g, j] = vwg[:, j * WKT:(j + 1) * WKT]


def _relayout(p):
    bsz, t, _ = p.shape
    nt = t // SKT + PAD_TILES
    wpt = SKT // WKT
    cb = lambda c: (lambda bi, i: (bi, jnp.maximum(i - PAD_TILES, 0), c))
    return pl.pallas_call(
        _relayout_kernel,
        grid=(bsz, nt),
        in_specs=[pl.BlockSpec((1, SKT, KV_W), cb(COL_KS // KV_W)),
                  pl.BlockSpec((1, SKT, KV_W), cb(COL_VS // KV_W)),
                  pl.BlockSpec((1, SKT, KV_W), cb(COL_KW // KV_W)),
                  pl.BlockSpec((1, SKT, KV_W), cb(COL_VW // KV_W))],
        out_specs=[pl.BlockSpec((1, N_KV, SKT, KA_SEL), lambda bi, i: (bi, 0, i, 0)),
                   pl.BlockSpec((1, N_KV, 1, VROWS, SKT), lambda bi, i: (bi, 0, i, 0, 0)),
                   pl.BlockSpec((1, N_KV, SKT, KA_WIN), lambda bi, i: (bi, 0, i, 0)),
                   pl.BlockSpec((1, N_KV, wpt, VROWS, WKT), lambda bi, i: (bi, 0, i, 0, 0))],
        out_shape=[jax.ShapeDtypeStruct((bsz, N_KV, nt * SKT, KA_SEL), BF16),
                   jax.ShapeDtypeStruct((bsz, N_KV, nt, VROWS, SKT), BF16),
                   jax.ShapeDtypeStruct((bsz, N_KV, nt * SKT, KA_WIN), BF16),
                   jax.ShapeDtypeStruct((bsz, N_KV, nt * wpt, VROWS, WKT), BF16)],
        compiler_params=_cparams(("parallel", "parallel")),
        name="nsa_relayout",
    )(p, p, p, p)


def _nsa_kernel(sp_ref, q_ref, gt_ref, z_ref, kcmp_ref, vcmpt_ref, ovt_ref, ks_ref, vs_ref, kw_ref,
                vw_ref, tri_ref, wb_ref, y_ref, s_scr, g_scr, imp_scr, rank_scr, part_scr, selb_scr):
    g = pl.program_id(1)
    qb = pl.program_id(2)
    start = qb * QBLK
    nb = kcmp_ref.shape[2]
    n_slc = ovt_ref.shape[0]

    qt = (q_ref[0] * (N_HD ** -0.5)).T
    q4t = jnp.concatenate([qt[h * N_HD:(h + 1) * N_HD] for h in range(N_HPG)], axis=1)
    t_row = start + lax.broadcasted_iota(jnp.int32, (1, QBLK), 1)

    g_scr[...] = jax.nn.sigmoid(gt_ref[0]).T

    nw = N_HPG * QBLK
    hl = lax.broadcasted_iota(jnp.int32, (AUX_ROWS, nw), 1) // QBLK
    ar = lax.broadcasted_iota(jnp.int32, (AUX_ROWS, nw), 0)

    def per_head(vals):
        out = jnp.full((AUX_ROWS, nw), vals[N_HPG - 1], F32)
        for h in range(N_HPG - 2, -1, -1):
            out = jnp.where(hl == h, vals[h], out)
        return out

    pieces = [per_head([sp_ref[(g * N_HPG + h) * 3 + j] for h in range(N_HPG)]) for j in range(3)]
    aux = jnp.where(ar == 6, NEG, 0.0)
    for j in range(3):
        aux = jnp.where((ar == j) | (ar == j + 3), pieces[j], aux)
    aux_b = aux.astype(BF16)
    q4l = q4t * LOG2E
    q4s = q4l.astype(BF16)
    q4lo = (q4l - q4s.astype(F32)).astype(BF16)

    def tile4(a):
        return jnp.concatenate([a] * N_HPG, axis=1)

    def col_reduce8(s, op):
        out = s[0:8]
        for r in range(1, s.shape[0] // 8):
            out = op(out, s[8 * r:8 * r + 8])
        return out

    def col_max(s):
        return jnp.max(col_reduce8(s, jnp.maximum), axis=0, keepdims=True)

    def normalise(acc):
        den = acc[N_HD:N_HD + 1]
        return acc[0:N_HD] / jnp.where(den > 0, den, 1.0)

    q_cmp = jnp.concatenate([q4s, q4s, q4lo, aux_b, jnp.zeros((N_HD - AUX_ROWS, nw), BF16)], axis=0)
    cmp_end = lax.broadcasted_iota(jnp.int32, (nb, 1), 0) * CMP_STRIDE + (CMP_LEN - 1)
    s_c = jnp.dot(kcmp_ref[0, 0], q_cmp, preferred_element_type=F32)
    s_c = s_c + tile4(jnp.where(cmp_end <= t_row, 0.0, NEG))
    e_c = jnp.exp2(s_c - col_max(s_c))
    z_c = jnp.sum(col_reduce8(e_c, jnp.add), axis=0, keepdims=True)
    inv_c = jnp.where(tile4(t_row >= CMP_LEN - 1) & (z_c > 0), 1.0 / z_c, 0.0)
    o_c = jnp.dot(vcmpt_ref[0, 0], e_c.astype(BF16), preferred_element_type=F32)[0:N_HD] * inv_c
    p_c = e_c * inv_c
    psum = p_c[:, 0:QBLK]
    for h in range(1, N_HPG):
        psum = psum + p_c[:, h * QBLK:(h + 1) * QBLK]
    parts, rest = [], psum
    for _ in range(3):
        parts.append(rest.astype(BF16))
        rest = rest - parts[-1].astype(F32)
    imp = jnp.dot(ovt_ref[...], jnp.concatenate(parts, axis=0), preferred_element_type=F32)
    sidx = lax.broadcasted_iota(jnp.int32, (n_slc, QBLK), 0)
    cur = t_row // SLC_LEN
    valid = sidx <= cur
    imp = jnp.where(valid, imp, -jnp.inf)
    imp_scr[...] = jnp.where((sidx == 0) | (sidx == cur), jnp.inf, imp)

    wrows = WIN + QBLK
    q_win = jnp.concatenate([q4s, jnp.zeros((AUX_ROWS, nw), BF16), aux_b,
                             jnp.zeros((KA_WIN - N_HD - 2 * AUX_ROWS, nw), BF16)], axis=0)
    s_w = jnp.dot(kw_ref[0, 0, pl.ds(pl.multiple_of(start, QBLK), wrows), :], q_win,
                  preferred_element_type=F32)
    s_w = jnp.concatenate([s_w[0:WKT] + tile4(wb_ref[0]), s_w[WKT:wrows - WKT],
                           s_w[wrows - WKT:] + tile4(wb_ref[1])], axis=0)
    p_w = jnp.exp2(s_w - col_max(s_w)).astype(BF16)
    v_w = jnp.concatenate([vw_ref[0, 0, qb * (QBLK // WKT) + j] for j in range(wrows // WKT)], axis=1)
    o_w = normalise(jnp.dot(v_w, p_w, preferred_element_type=F32))

    def gate_row(h, branch):
        return g_scr[pl.ds(GATE_NG + (g * N_HPG + h) * 3 + branch, 1), :]

    for h in range(N_HPG):
        hs = slice(h * QBLK, (h + 1) * QBLK)
        part_scr[:, hs] = gate_row(h, 0) * o_c[:, hs] + gate_row(h, 2) * o_w[:, hs]

    n_top = min(SLC_TOPN, n_slc)
    rank_scr[...] = jnp.zeros_like(rank_scr)
    sub8 = lax.broadcasted_iota(jnp.int32, (8, QBLK), 0)
    last_blk = (start + QBLK - 1) // SLC_LEN
    for ri in range(n_slc // 8):
        @pl.when((last_blk >= n_top) & (8 * ri <= last_blk))
        def _():
            rows = imp_scr[8 * ri:8 * ri + 8]
            for r in range(n_slc // 8):
                blk8 = imp_scr[8 * r:8 * r + 8]
                acc = rank_scr[8 * r:8 * r + 8]
                for ii in range(8):
                    row = rows[ii:ii + 1]
                    if ri < r:
                        before = row >= blk8
                    elif ri > r:
                        before = row > blk8
                    else:
                        before = (row > blk8) | ((row == blk8) & (sub8 > ii))
                    acc = acc + jnp.where(before, 1.0, 0.0)
                rank_scr[8 * r:8 * r + 8] = acc
    selbias = jnp.where((rank_scr[...] < n_top) & valid, 0.0, NEG)

    selb_scr[...] = tile4(selbias)
    n_tiles = qb // (SKT // QBLK) + 1
    q_tail = jnp.zeros((KA_SEL - N_HD - 2 * AUX_ROWS, nw), BF16)

    def sel_scores(kt):
        grp = pl.multiple_of((kt * SKT // SLC_LEN) // SEL_GRP * SEL_GRP, SEL_GRP)
        sel_rows = jnp.concatenate([selb_scr[pl.ds(grp, SEL_GRP), :], jnp.zeros((AUX_ROWS - SEL_GRP, nw), F32)], axis=0)
        q_sel = jnp.concatenate([q4s, sel_rows.astype(BF16), aux_b, q_tail], axis=0)
        krow = pl.multiple_of((kt + PAD_TILES) * SKT, SKT)
        return jnp.dot(ks_ref[0, 0, pl.ds(krow, SKT), :], q_sel, preferred_element_type=F32)

    def pass1(kt, mrun):
        s = sel_scores(kt)
        s_scr[pl.ds(pl.multiple_of(kt * SKT, SKT), SKT), :] = s
        return jnp.maximum(mrun, col_reduce8(s, jnp.maximum))

    def unrolled(fn):
        def body(i, carry):
            for u in range(UNROLL):
                carry = fn(i * UNROLL + u, carry)
            return carry
        return body

    def run_tiles(fn, n, init):
        n_u = n // UNROLL
        carry = lax.fori_loop(0, n_u, unrolled(fn), init)
        return lax.fori_loop(n_u * UNROLL, n, fn, carry)

    mrun = run_tiles(pass1, n_tiles - 1, jnp.full((8, nw), NEG, F32))
    last = n_tiles - 1
    s_last = sel_scores(last) + tile4(tri_ref[qb % (SKT // QBLK)])
    s_scr[pl.ds(pl.multiple_of(last * SKT, SKT), SKT), :] = s_last
    m_sel = jnp.max(jnp.maximum(mrun, col_reduce8(s_last, jnp.maximum)), axis=0, keepdims=True)

    def pass2(kt, acc):
        s = s_scr[pl.ds(pl.multiple_of(kt * SKT, SKT), SKT), :]
        pr = jnp.exp2(s - m_sel).astype(BF16)
        return acc + jnp.dot(vs_ref[0, 0, kt + PAD_TILES], pr, preferred_element_type=F32)

    o_s = normalise(run_tiles(pass2, n_tiles, jnp.zeros((VROWS, nw), F32)))

    tot = [part_scr[:, h * QBLK:(h + 1) * QBLK] + gate_row(h, 1) * o_s[:, h * QBLK:(h + 1) * QBLK]
           for h in range(N_HPG)]
    o = jnp.concatenate(tot, axis=0).T
    y_ref[0] = (o * _silu(z_ref[0])).astype(BF16)


def _edge_biases():
    kl = np.arange(SKT)[:, None]
    ql = np.arange(QBLK)[None, :]
    tri = np.stack([np.where(kl <= par * QBLK + ql, 0.0, NEG) for par in range(SKT // QBLK)])
    kk = np.arange(WKT)[:, None]
    wb = np.stack([np.where(kk > ql, 0.0, NEG), np.where(kk <= ql + WKT - QBLK, 0.0, NEG)])
    return jnp.asarray(tri, F32), jnp.asarray(wb, F32)


def _nsa(p, spieces, kcmp, vcmpt, ovt, ks, vs, kw, vw):
    bsz, t, _ = p.shape
    n_slc = t // SLC_LEN
    gw = N_HPG * N_HD
    tri, wb = _edge_biases()
    per_bg = lambda a: pl.BlockSpec((1, 1) + a.shape[2:], lambda bi, g, i, sp: (bi, g) + (0,) * (a.ndim - 2))
    const = lambda a: pl.BlockSpec(a.shape, lambda bi, g, i, sp: (0,) * a.ndim)
    grid_spec = pltpu.PrefetchScalarGridSpec(
        num_scalar_prefetch=1,
        grid=(bsz, N_KV, t // QBLK),
        in_specs=[pl.BlockSpec((1, QBLK, gw), lambda bi, g, i, sp: (bi, i, COL_NQ // gw + g)),
                  pl.BlockSpec((1, QBLK, LANES), lambda bi, g, i, sp: (bi, i, COL_GATES // LANES)),
                  pl.BlockSpec((1, QBLK, gw), lambda bi, g, i, sp: (bi, i, COL_NZ // gw + g)),
                  per_bg(kcmp), per_bg(vcmpt), const(ovt),
                  per_bg(ks), per_bg(vs), per_bg(kw), per_bg(vw), const(tri), const(wb)],
        out_specs=pl.BlockSpec((1, QBLK, gw), lambda bi, g, i, sp: (bi, i, g)),
        scratch_shapes=[pltpu.VMEM((t, N_HPG * QBLK), F32),
                        pltpu.VMEM((LANES, QBLK), F32),
                        pltpu.VMEM((n_slc, QBLK), F32),
                        pltpu.VMEM((n_slc, QBLK), F32),
                        pltpu.VMEM((N_HD, N_HPG * QBLK), F32),
                        pltpu.VMEM((n_slc, N_HPG * QBLK), F32)],
    )
    return pl.pallas_call(
        _nsa_kernel,
        grid_spec=grid_spec,
        out_shape=jax.ShapeDtypeStruct((bsz, t, N_WIDTH), BF16),
        compiler_params=_cparams(("parallel", "parallel", "arbitrary")),
        name="nsa_attention",
    )(spieces, p, p, p, kcmp, vcmpt, ovt, ks, vs, kw, vw, tri, wb)


def _outproj_kernel(ym_ref, yn_ref, w_ref, x_ref, gate_ref, fg_ref, o_ref, *, final):
    y = jnp.dot(ym_ref[0], w_ref[0:M_WIDTH, :], preferred_element_type=F32)
    y = y + jnp.dot(yn_ref[0], w_ref[M_WIDTH:, :], preferred_element_type=F32)
    hres = x_ref[0] + gate_ref[0] * y
    if final:
        ms = jnp.mean(hres * hres, axis=-1, keepdims=True)
        hres = hres * lax.rsqrt(ms + EPS) * fg_ref[...]
    o_ref[0] = hres


def _outproj(ym, yn, w, x, gate, fg, final):
    bsz, t, d = x.shape
    tm = 512
    return pl.pallas_call(
        functools.partial(_outproj_kernel, final=final),
        grid=(bsz, t // tm),
        in_specs=[pl.BlockSpec((1, tm, M_WIDTH), lambda bi, i: (bi, i, 0)),
                  pl.BlockSpec((1, tm, N_WIDTH), lambda bi, i: (bi, i, 0)),
                  pl.BlockSpec((M_WIDTH + N_WIDTH, d), lambda bi, i: (0, 0)),
                  pl.BlockSpec((1, tm, d), lambda bi, i: (bi, i, 0)),
                  pl.BlockSpec((1, 1, d), lambda bi, i: (bi, 0, 0)),
                  pl.BlockSpec((1, d), lambda bi, i: (0, 0))],
        out_specs=pl.BlockSpec((1, tm, d), lambda bi, i: (bi, i, 0)),
        out_shape=jax.ShapeDtypeStruct((bsz, t, d), F32),
        compiler_params=_cparams(("parallel", "parallel")),
        name="outproj_residual",
    )(ym, yn, w, x, gate, fg)


SRC_MI = 4 * M_WIDTH
SRC_NQ = SRC_MI + 2 * M_HEADS
SRC_NG = SRC_NQ + N_WIDTH + 6 * KV_W
SRC_NZ = SRC_NG + 3 * N_HEADS


def _reorder_cols(a):
    parts = [a[..., 0:SRC_MI], a[..., SRC_NQ:SRC_NG], a[..., SRC_NZ:SRC_NZ + N_WIDTH], a[..., SRC_MI:SRC_NQ],
             a[..., SRC_NG:SRC_NZ]]
    used = sum(x.shape[-1] for x in parts)
    parts.append(jnp.zeros(a.shape[:-1] + (NP_PAD - used,), a.dtype))
    return jnp.concatenate(parts, axis=-1)


def _wprep_kernel(w_ref, o_ref):
    rows = w_ref.shape[1]
    o_ref[:, 0:COL_NQ] = w_ref[0, :, 0:SRC_MI].astype(BF16)
    o_ref[:, COL_NQ:COL_NZ] = w_ref[0, :, SRC_NQ:SRC_NG].astype(BF16)
    o_ref[:, COL_NZ:COL_GATES] = w_ref[0, :, SRC_NZ:SRC_NZ + N_WIDTH].astype(BF16)
    n_gate = SRC_NQ - SRC_MI + SRC_NZ - SRC_NG
    gates = jnp.concatenate([w_ref[0, :, SRC_MI:SRC_NQ], w_ref[0, :, SRC_NG:SRC_NZ],
                             jnp.zeros((rows, LANES - n_gate), F32)], axis=1)
    o_ref[:, COL_GATES:COL_GATES + LANES] = gates.astype(BF16)
    o_ref[:, COL_GATES + LANES:] = jnp.zeros((rows, NP_PAD - COL_GATES - LANES), BF16)


def _wprep(w, layer):
    _, d, n = w.shape
    tr = 256
    return pl.pallas_call(
        _wprep_kernel,
        grid=(d // tr,),
        in_specs=[pl.BlockSpec((1, tr, n), lambda i: (layer, i, 0))],
        out_specs=pl.BlockSpec((tr, NP_PAD), lambda i: (i, 0)),
        out_shape=jax.ShapeDtypeStruct((d, NP_PAD), BF16),
        compiler_params=_cparams(("parallel",)),
        name="inproj_weight_prep",
    )(w)


def _overlap_t(t):
    n_cmp_rows = t // CMP_STRIDE
    n_slc = t // SLC_LEN
    c0 = np.arange(n_cmp_rows) * CMP_STRIDE
    s0 = np.arange(n_slc) * SLC_LEN
    ov = (c0[None, :] <= s0[:, None] + SLC_LEN - 1) & (c0[None, :] + CMP_LEN - 1 >= s0[:, None])
    ov[:, (t - CMP_LEN) // CMP_STRIDE + 1:] = False
    return jnp.asarray(np.concatenate([ov] * 3, axis=1), BF16)


def kernel(x, c, ln_g, w_ada, b_ada, w_in, b_in, m_conv_w, m_conv_b, m_wq, m_wk, m_norm_w, m_skip, m_f_bias,
           n_pos_k, n_pos_v, n_w1_k, n_w2_k, n_w1_v, n_w2_v, w_out, final_g):
    out_dtype = x.dtype
    bsz, t, d = x.shape
    depth = ln_g.shape[0]
    h_res = x.astype(F32)
    c8 = jnp.zeros((8, d), F32).at[:bsz].set(c.astype(F32))
    slopes_np = np.array([2.0 ** (-8.0 * (h + 1) / N_HEADS) for h in range(N_HEADS)], np.float32)
    rest = (slopes_np.astype(np.float64) * LOG2E).astype(np.float32)
    pieces = []
    for _ in range(3):
        pieces.append(rest.astype(jnp.bfloat16).astype(np.float32))
        rest = rest - pieces[-1]
    spieces = jnp.asarray(np.stack(pieces, axis=1).reshape(-1))
    ovt = _overlap_t(t)

    def w1cat(w1):
        w = w1.reshape(2, CMP_STRIDE, N_HD, CMP_HIDDEN)
        return jnp.concatenate([w[0], w[1]], axis=-1).astype(BF16)

    def w2pad(w2):
        return jnp.pad(w2, ((0, 0), (0, LANES - N_HD))).astype(BF16)

    for l in range(depth):
        mod = _ada(c8, w_ada, b_ada[l][None, :], l)[:bsz]
        shift, scale, gate = mod[:, None, 0:d], mod[:, None, d:2 * d], mod[:, None, 2 * d:3 * d]
        p = _inproj(h_res, ln_g[l][None, :], scale, shift,
                    _wprep(w_in, l), _reorder_cols(b_in[l])[None, :])
        fb_row = jnp.zeros((1, LANES), F32).at[0, M_HEADS:2 * M_HEADS].set(m_f_bias[l])
        y_m = _mlstm(p, m_conv_w[l], m_conv_b[l][None, :], m_wq[l].astype(BF16), m_wk[l].astype(BF16),
                     m_norm_w[l][None, :], m_skip[l][None, :], fb_row)
        kcmp, vcmpt = _compress(p, n_pos_k[l].reshape(1, -1), n_pos_v[l].reshape(1, -1),
                                n_w1_k[l].astype(BF16), w1cat(n_w1_k[l]), w2pad(n_w2_k[l]),
                                n_w1_v[l].astype(BF16), w1cat(n_w1_v[l]), w2pad(n_w2_v[l]))
        ks, vs, kw, vw = _relayout(p)
        y_n = _nsa(p, spieces, kcmp, vcmpt, ovt, ks, vs, kw, vw)
        h_res = _outproj(y_m, y_n, w_out[l].astype(BF16), h_res, gate, final_g[None, :], l == depth - 1)
    return h_res.astype(out_dtype)
```

```python
import functools

import numpy as np
import jax
import jax.numpy as jnp
from jax import lax
from jax.experimental import pallas as pl
from jax.experimental.pallas import tpu as pltpu

F32 = jnp.float32
BF16 = jnp.bfloat16
HIGHEST = lax.Precision.HIGHEST

EPS = 1e-6
M_HEADS = 4
M_HD = 256
M_WIDTH = M_HEADS * M_HD
CONV_K = 4
M_CHUNK = 256
N_HEADS = 16
N_HD = 64
N_KV = 4
N_HPG = N_HEADS // N_KV
N_WIDTH = N_HEADS * N_HD
KV_W = N_KV * N_HD
CMP_LEN = 32
CMP_STRIDE = 16
CMP_HIDDEN = 2 * N_HD
SLC_LEN = 64
SLC_TOPN = 16
WIN = 512
QBLK = 256
SKT = 256
WKT = 256
PAD_TILES = WIN // SKT
SEL_GRP = 8
KA_SEL = 128
KA_WIN = 128
VROWS = 80
UNROLL = 4
AUX_ROWS = 16
LOG2E = 1.4426950408889634

COL_MX, COL_MV, COL_MO, COL_MZ = 0, 1024, 2048, 3072
COL_NQ = 4096
COL_KC, COL_VC, COL_KS, COL_VS, COL_KW, COL_VW = 5120, 5376, 5632, 5888, 6144, 6400
COL_NZ = 6656
COL_GATES = 7680
GATE_NG = 2 * M_HEADS
NP_PAD = 8192
NORM_CHUNKS = 4
LANES = 128
NEG = -1e30
VMEM_LIMIT = 56 * 1024 * 1024


def _cparams(sem):
    return pltpu.CompilerParams(dimension_semantics=sem, vmem_limit_bytes=VMEM_LIMIT)


def _silu(x):
    return x * jax.nn.sigmoid(x)


def _log_sigmoid(x):
    return jnp.minimum(x, 0.0) - jnp.log1p(jnp.exp(-jnp.abs(x)))


def _ada_kernel(c_ref, w_ref, b_ref, o_ref):
    s = _silu(c_ref[...])
    o_ref[...] = jnp.dot(s, w_ref[0], precision=HIGHEST, preferred_element_type=F32) + b_ref[...]


def _ada(c8, w, b, layer):
    _, d, n = w.shape
    tn = 1024
    return pl.pallas_call(
        _ada_kernel,
        grid=(n // tn,),
        in_specs=[pl.BlockSpec((8, d), lambda j: (0, 0)),
                  pl.BlockSpec((1, d, tn), lambda j: (layer, 0, j)),
                  pl.BlockSpec((1, tn), lambda j: (0, j))],
        out_specs=pl.BlockSpec((8, tn), lambda j: (0, j)),
        out_shape=jax.ShapeDtypeStruct((8, n), F32),
        compiler_params=_cparams(("parallel",)),
        name="ada_mod",
    )(c8, w, b)


def _inproj_kernel(x_ref, g_ref, sc_ref, sh_ref, w_ref, b_ref, o_ref, h_ref):
    first = pl.program_id(2) == 0

    @pl.when(first)
    def _():
        tm = x_ref.shape[1]
        ck = tm // NORM_CHUNKS
        for c in range(NORM_CHUNKS):
            rows = slice(c * ck, (c + 1) * ck)
            x = x_ref[0, rows, :]
            ms = jnp.mean(x * x, axis=-1, keepdims=True)
            h = x * lax.rsqrt(ms + EPS) * g_ref[...]
            h = (h * (1.0 + sc_ref[0]) + sh_ref[0]).astype(BF16)
            h_ref[rows, :] = h
            o_ref[0, rows, :] = jnp.dot(h, w_ref[...], preferred_element_type=F32) + b_ref[...]

    @pl.when(jnp.logical_not(first))
    def _():
        o_ref[0] = jnp.dot(h_ref[...], w_ref[...], preferred_element_type=F32) + b_ref[...]


def _inproj(x, g, scale, shift, w, b):
    bsz, t, d = x.shape
    n = w.shape[1]
    tm, tn = 1024, 1024
    return pl.pallas_call(
        _inproj_kernel,
        grid=(bsz, t // tm, n // tn),
        in_specs=[pl.BlockSpec((1, tm, d), lambda bi, i, j: (bi, i, 0)),
                  pl.BlockSpec((1, d), lambda bi, i, j: (0, 0)),
                  pl.BlockSpec((1, 1, d), lambda bi, i, j: (bi, 0, 0)),
                  pl.BlockSpec((1, 1, d), lambda bi, i, j: (bi, 0, 0)),
                  pl.BlockSpec((d, tn), lambda bi, i, j: (0, j)),
                  pl.BlockSpec((1, tn), lambda bi, i, j: (0, j))],
        out_specs=pl.BlockSpec((1, tm, tn), lambda bi, i, j: (bi, i, j)),
        out_shape=jax.ShapeDtypeStruct((bsz, t, n), F32),
        scratch_shapes=[pltpu.VMEM((tm, d), BF16)],
        compiler_params=_cparams(("parallel", "parallel", "arbitrary")),
        name="norm_inproj",
    )(x, g, scale, shift, w, b)


def _mlstm_kernel(x_ref, v_ref, o_ref, z_ref, gt_ref, cw_ref, cb_ref, wq_ref, wk_ref, nw_ref, sk_ref, fb_ref,
                  y_ref, c_scr, n_scr, m_scr, xp_scr):
    L = M_CHUNK

    @pl.when(pl.program_id(1) == 0)
    def _():
        c_scr[...] = jnp.zeros_like(c_scr)
        n_scr[...] = jnp.zeros_like(n_scr)
        m_scr[...] = jnp.zeros_like(m_scr)
        xp_scr[...] = jnp.zeros_like(xp_scr)

    x = x_ref[0]
    prev = xp_scr[...]
    row8 = lax.broadcasted_iota(jnp.int32, (8, M_WIDTH), 0)
    cw = cw_ref[...]
    xc = cb_ref[...] + x * cw[CONV_K - 1:CONV_K, :]
    for sft in range(1, CONV_K):
        xr = pltpu.roll(x, sft, 0)
        top = jnp.where(row8 < sft, pltpu.roll(prev, sft, 0), xr[0:8])
        xs = jnp.concatenate([top, xr[8:]], axis=0)
        xc = xc + xs * cw[CONV_K - 1 - sft:CONV_K - sft, :]
    xp_scr[...] = x[L - 8:L]
    xc = _silu(xc)

    gt = gt_ref[0]
    col = lax.broadcasted_iota(jnp.int32, (L, LANES), 1)
    logf = _log_sigmoid(gt + fb_ref[...])
    a_c = jnp.where((col >= M_HEADS) & (col < 2 * M_HEADS), logf, gt)
    ri = lax.broadcasted_iota(jnp.int32, (L, L), 0)
    ci = lax.broadcasted_iota(jnp.int32, (L, L), 1)
    causal = ri >= ci
    tri = causal.astype(F32)
    tri_t = (ri <= ci).astype(F32)
    b_c = jnp.dot(tri, a_c, precision=HIGHEST, preferred_element_type=F32)
    a_r = a_c.T
    b_r = jnp.dot(a_r[0:8], tri_t, precision=HIGHEST, preferred_element_type=F32)

    for h in range(M_HEADS):
        sl = slice(h * M_HD, (h + 1) * M_HD)
        xh = xc[:, sl]
        xb = xh.astype(BF16)
        q = jnp.dot(xb, wq_ref[h], preferred_element_type=F32)
        k = jnp.dot(xb, wk_ref[h], preferred_element_type=F32) * (M_HD ** -0.5)
        vb = v_ref[0, :, sl].astype(BF16)
        qb = q.astype(BF16)
        kb = k.astype(BF16)

        bt = b_c[:, M_HEADS + h:M_HEADS + h + 1]
        ic = a_c[:, h:h + 1]
        bs = b_r[M_HEADS + h:M_HEADS + h + 1, :]
        ir = a_r[h:h + 1, :]
        m_prev = m_scr[h][:, 0:1]

        dm = jnp.where(causal, bt - bs + ir, -jnp.inf)
        inter = bt + m_prev
        m_t = jnp.maximum(inter, jnp.max(dm, axis=-1, keepdims=True))
        w_in = jnp.exp(dm - m_t)
        w_st = jnp.exp(inter - m_t)
        s = lax.dot_general(qb, kb, (((1,), (1,)), ((), ())), preferred_element_type=F32) * w_in
        cmat = c_scr[h]
        nvec = n_scr[h]
        num = w_st * jnp.dot(qb, cmat.astype(BF16), preferred_element_type=F32) \
            + jnp.dot(s.astype(BF16), vb, preferred_element_type=F32)
        den = w_st * jnp.sum(q * nvec, axis=-1, keepdims=True) + jnp.sum(s, axis=-1, keepdims=True)
        hh = num / jnp.maximum(jnp.abs(den), jnp.exp(-m_t))

        b_last = bt[L - 1:L, :]
        w_end = b_last - bt + ic
        m_new = jnp.maximum(b_last + m_prev, jnp.max(w_end, axis=0, keepdims=True))
        decay = jnp.exp(b_last + m_prev - m_new)
        kwt = k * jnp.exp(w_end - m_new)
        c_scr[h] = decay * cmat + lax.dot_general(kwt.astype(BF16), vb, (((0,), (0,)), ((), ())),
                                                  preferred_element_type=F32)
        n_scr[h] = decay * nvec + jnp.sum(kwt, axis=0, keepdims=True)
        m_scr[h] = jnp.broadcast_to(m_new, (1, LANES))

        mu = jnp.mean(hh, axis=-1, keepdims=True)
        hc = hh - mu
        var = jnp.mean(hc * hc, axis=-1, keepdims=True)
        hn = hc * lax.rsqrt(var + EPS) * nw_ref[:, sl]
        out = jax.nn.sigmoid(o_ref[0, :, sl]) * hn + sk_ref[:, sl] * xh
        y_ref[0, :, sl] = (out * _silu(z_ref[0, :, sl])).astype(BF16)


def _mlstm(p, conv_w, conv_b, wq, wk, norm_w, skip, fb_row):
    bsz, t, _ = p.shape
    L = M_CHUNK
    cb = lambda c: (lambda bi, i: (bi, i, c))
    full2 = lambda bi, i: (0, 0)
    full3 = lambda bi, i: (0, 0, 0)
    return pl.pallas_call(
        _mlstm_kernel,
        grid=(bsz, t // L),
        in_specs=[pl.BlockSpec((1, L, M_WIDTH), cb(COL_MX // M_WIDTH)),
                  pl.BlockSpec((1, L, M_WIDTH), cb(COL_MV // M_WIDTH)),
                  pl.BlockSpec((1, L, M_WIDTH), cb(COL_MO // M_WIDTH)),
                  pl.BlockSpec((1, L, M_WIDTH), cb(COL_MZ // M_WIDTH)),
                  pl.BlockSpec((1, L, LANES), cb(COL_GATES // LANES)),
                  pl.BlockSpec((CONV_K, M_WIDTH), full2),
                  pl.BlockSpec((1, M_WIDTH), full2),
                  pl.BlockSpec((M_HEADS, M_HD, M_HD), full3),
                  pl.BlockSpec((M_HEADS, M_HD, M_HD), full3),
                  pl.BlockSpec((1, M_WIDTH), full2),
                  pl.BlockSpec((1, M_WIDTH), full2),
                  pl.BlockSpec((1, LANES), full2)],
        out_specs=pl.BlockSpec((1, L, M_WIDTH), lambda bi, i: (bi, i, 0)),
        out_shape=jax.ShapeDtypeStruct((bsz, t, M_WIDTH), BF16),
        scratch_shapes=[pltpu.VMEM((M_HEADS, M_HD, M_HD), F32),
                        pltpu.VMEM((M_HEADS, 1, M_HD), F32),
                        pltpu.VMEM((M_HEADS, 1, LANES), F32),
                        pltpu.VMEM((8, M_WIDTH), F32)],
        compiler_params=_cparams(("parallel", "arbitrary")),
        name="mlstm_group",
    )(p, p, p, p, p, conv_w, conv_b, wq, wk, norm_w, skip, fb_row)


def _compress_kernel(kc0_ref, kc1_ref, vc0_ref, vc1_ref, posk_ref, posv_ref, w1k_ref, w1kc_ref, w2k_ref,
                     w1v_ref, w1vc_ref, w2v_ref, kcmp_ref, vcmpt_ref):
    nb = kcmp_ref.shape[2]

    def hidden(src_refs, pos_ref, w1_ref, w1c_ref):
        acc = jnp.zeros((N_KV * nb, 2 * CMP_HIDDEN), F32)
        for l in range(CMP_STRIDE):
            xl = [r[0, pl.ds(l, nb, stride=CMP_STRIDE), :] for r in src_refs]
            xs = jnp.concatenate([x[:, g * N_HD:(g + 1) * N_HD] for x in xl for g in range(LANES // N_HD)], axis=0)
            acc = acc + jnp.dot(xs.astype(BF16), w1c_ref[l], preferred_element_type=F32)
        first = acc[:, :CMP_HIDDEN]
        second = acc[:, CMP_HIDDEN:]
        posb = jnp.dot(jnp.broadcast_to(pos_ref[...], (8, CMP_LEN * N_HD)).astype(BF16), w1_ref[...],
                       preferred_element_type=F32)[0:1]
        hid = first + pltpu.roll(second, N_KV * nb - 1, 0) + posb
        return jax.nn.gelu(hid).astype(BF16)

    hk = hidden((kc0_ref, kc1_ref), posk_ref, w1k_ref, w1kc_ref)
    kc = jnp.dot(hk, w2k_ref[...], preferred_element_type=F32)
    hv = hidden((vc0_ref, vc1_ref), posv_ref, w1v_ref, w1vc_ref)
    vc = jnp.dot(hv, w2v_ref[...], preferred_element_type=F32)
    j = lax.broadcasted_iota(jnp.int32, (nb, 1), 0)
    lane = lax.broadcasted_iota(jnp.int32, (nb, N_HD), 1)
    n_real = (nb * CMP_STRIDE - CMP_LEN) // CMP_STRIDE + 1
    aux = jnp.where(lane < 3, (j * CMP_STRIDE).astype(F32),
                    jnp.where(lane < 6, (CMP_LEN - 1) * 0.5,
                              jnp.where((lane == 6) & (j >= n_real), 1.0, 0.0))).astype(BF16)
    ones_blk = jnp.where(lax.broadcasted_iota(jnp.int32, (VROWS - N_HD, nb), 0) == 0, 1.0, 0.0)
    for g in range(N_KV):
        kg = kc[g * nb:(g + 1) * nb, 0:N_HD]
        k_hi = kg.astype(BF16)
        k_lo = (kg - k_hi.astype(F32)).astype(BF16)
        kcmp_ref[0, g] = jnp.concatenate([k_hi, k_lo, k_hi, aux], axis=1)
        vt = vc[g * nb:(g + 1) * nb].T
        vcmpt_ref[0, g] = jnp.concatenate([vt[0:N_HD], ones_blk], axis=0).astype(BF16)


def _compress(p, posk, posv, w1k, w1kc, w2k, w1v, w1vc, w2v):
    bsz, t, _ = p.shape
    nb = t // CMP_STRIDE
    cb = lambda c: (lambda bi: (bi, 0, c))
    f2 = lambda bi: (0, 0)
    f3 = lambda bi: (0, 0, 0)
    wspecs = [pl.BlockSpec((CMP_LEN * N_HD, CMP_HIDDEN), f2),
              pl.BlockSpec((CMP_STRIDE, N_HD, 2 * CMP_HIDDEN), f3),
              pl.BlockSpec((CMP_HIDDEN, LANES), f2)]
    return pl.pallas_call(
        _compress_kernel,
        grid=(bsz,),
        in_specs=[pl.BlockSpec((1, t, LANES), cb(COL_KC // LANES)),
                  pl.BlockSpec((1, t, LANES), cb(COL_KC // LANES + 1)),
                  pl.BlockSpec((1, t, LANES), cb(COL_VC // LANES)),
                  pl.BlockSpec((1, t, LANES), cb(COL_VC // LANES + 1)),
                  pl.BlockSpec((1, CMP_LEN * N_HD), f2),
                  pl.BlockSpec((1, CMP_LEN * N_HD), f2)] + wspecs + wspecs,
        out_specs=[pl.BlockSpec((1, N_KV, nb, 4 * N_HD), lambda bi: (bi, 0, 0, 0)),
                   pl.BlockSpec((1, N_KV, VROWS, nb), lambda bi: (bi, 0, 0, 0))],
        out_shape=[jax.ShapeDtypeStruct((bsz, N_KV, nb, 4 * N_HD), BF16),
                   jax.ShapeDtypeStruct((bsz, N_KV, VROWS, nb), BF16)],
        compiler_params=_cparams(("parallel",)),
        name="nsa_compress",
    )(p, p, p, p, posk, posv, w1k, w1kc, w2k, w1v, w1vc, w2v)


def _relayout_kernel(ks_ref, vs_ref, kw_ref, vw_ref, ksa_ref, vsa_ref, kwa_ref, vwa_ref):
    i = pl.program_id(1)
    is_pad = i < PAD_TILES
    base = (i - PAD_TILES) * SKT
    row = lax.broadcasted_iota(jnp.int32, (SKT, 1), 0)
    pos = jnp.where(is_pad, 0, base + row)
    blk = pos // SLC_LEN
    p_hi = (blk * SLC_LEN).astype(F32)
    p_lo = (pos - blk * SLC_LEN).astype(F32)
    flag = jnp.where(is_pad, 1.0, 0.0)
    lane = lax.broadcasted_iota(jnp.int32, (SKT, N_HD), 1)
    al = lane - AUX_ROWS
    mid_w = jnp.where((al >= 0) & (al < 3), p_hi,
                      jnp.where((al >= 3) & (al < 6), p_lo, jnp.where(al == 6, flag, 0.0)))
    mid_s = jnp.where(lane == blk % SEL_GRP, 1.0, mid_w)
    ks = ks_ref[0]
    kw = kw_ref[0]
    for g in range(N_KV):
        sl = slice(g * N_HD, (g + 1) * N_HD)
        ksa_ref[0, g] = jnp.concatenate([ks[:, sl], mid_s], axis=1).astype(BF16)
        kwa_ref[0, g] = jnp.concatenate([kw[:, sl], mid_w], axis=1).astype(BF16)
    ones_blk = jnp.where(lax.broadcasted_iota(jnp.int32, (VROWS - N_HD, SKT), 0) == 0, 1.0, 0.0)
    vst = vs_ref[0].T
    vwt = vw_ref[0].T
    for g in range(N_KV):
        sl = slice(g * N_HD, (g + 1) * N_HD)
        vsa_ref[0, g, 0] = jnp.concatenate([vst[sl], ones_blk], axis=0).astype(BF16)
        vwg = jnp.concatenate([vwt[sl], ones_blk], axis=0).astype(BF16)
        for j in range(SKT // WKT):
            vwa_ref[0, g, j] = vwg[:, j * WKT:(j + 1) * WKT]


def _relayout(p):
    bsz, t, _ = p.shape
    nt = t // SKT + PAD_TILES
    wpt = SKT // WKT
    cb = lambda c: (lambda bi, i: (bi, jnp.maximum(i - PAD_TILES, 0), c))
    return pl.pallas_call(
        _relayout_kernel,
        grid=(bsz, nt),
        in_specs=[pl.BlockSpec((1, SKT, KV_W), cb(COL_KS // KV_W)),
                  pl.BlockSpec((1, SKT, KV_W), cb(COL_VS // KV_W)),
                  pl.BlockSpec((1, SKT, KV_W), cb(COL_KW // KV_W)),
                  pl.BlockSpec((1, SKT, KV_W), cb(COL_VW // KV_W))],
        out_specs=[pl.BlockSpec((1, N_KV, SKT, KA_SEL), lambda bi, i: (bi, 0, i, 0)),
                   pl.BlockSpec((1, N_KV, 1, VROWS, SKT), lambda bi, i: (bi, 0, i, 0, 0)),
                   pl.BlockSpec((1, N_KV, SKT, KA_WIN), lambda bi, i: (bi, 0, i, 0)),
                   pl.BlockSpec((1, N_KV, wpt, VROWS, WKT), lambda bi, i: (bi, 0, i, 0, 0))],
        out_shape=[jax.ShapeDtypeStruct((bsz, N_KV, nt * SKT, KA_SEL), BF16),
                   jax.ShapeDtypeStruct((bsz, N_KV, nt, VROWS, SKT), BF16),
                   jax.ShapeDtypeStruct((bsz, N_KV, nt * SKT, KA_WIN), BF16),
                   jax.ShapeDtypeStruct((bsz, N_KV, nt * wpt, VROWS, WKT), BF16)],
        compiler_params=_cparams(("parallel", "parallel")),
        name="nsa_relayout",
    )(p, p, p, p)


def _nsa_kernel(sp_ref, q_ref, gt_ref, z_ref, kcmp_ref, vcmpt_ref, ovt_ref, ks_ref, vs_ref, kw_ref,
                vw_ref, tri_ref, wb_ref, y_ref, s_scr, g_scr, imp_scr, rank_scr, part_scr, selb_scr):
    g = pl.program_id(1)
    qb = pl.program_id(2)
    start = qb * QBLK
    nb = kcmp_ref.shape[2]
    n_slc = ovt_ref.shape[0]

    qt = (q_ref[0] * (N_HD ** -0.5)).T
    q4t = jnp.concatenate([qt[h * N_HD:(h + 1) * N_HD] for h in range(N_HPG)], axis=1)
    t_row = start + lax.broadcasted_iota(jnp.int32, (1, QBLK), 1)

    g_scr[...] = jax.nn.sigmoid(gt_ref[0]).T

    nw = N_HPG * QBLK
    hl = lax.broadcasted_iota(jnp.int32, (AUX_ROWS, nw), 1) // QBLK
    ar = lax.broadcasted_iota(jnp.int32, (AUX_ROWS, nw), 0)

    def per_head(vals):
        out = jnp.full((AUX_ROWS, nw), vals[N_HPG - 1], F32)
        for h in range(N_HPG - 2, -1, -1):
            out = jnp.where(hl == h, vals[h], out)
        return out

    pieces = [per_head([sp_ref[(g * N_HPG + h) * 3 + j] for h in range(N_HPG)]) for j in range(3)]
    aux = jnp.where(ar == 6, NEG, 0.0)
    for j in range(3):
        aux = jnp.where((ar == j) | (ar == j + 3), pieces[j], aux)
    aux_b = aux.astype(BF16)
    q4l = q4t * LOG2E
    q4s = q4l.astype(BF16)
    q4lo = (q4l - q4s.astype(F32)).astype(BF16)

    def tile4(a):
        return jnp.concatenate([a] * N_HPG, axis=1)

    def col_reduce8(s, op):
        out = s[0:8]
        for r in range(1, s.shape[0] // 8):
            out = op(out, s[8 * r:8 * r + 8])
        return out

    def col_max(s):
        return jnp.max(col_reduce8(s, jnp.maximum), axis=0, keepdims=True)

    def normalise(acc):
        den = acc[N_HD:N_HD + 1]
        return acc[0:N_HD] / jnp.where(den > 0, den, 1.0)

    q_cmp = jnp.concatenate([q4s, q4s, q4lo, aux_b, jnp.zeros((N_HD - AUX_ROWS, nw), BF16)], axis=0)
    cmp_end = lax.broadcasted_iota(jnp.int32, (nb, 1), 0) * CMP_STRIDE + (CMP_LEN - 1)
    s_c = jnp.dot(kcmp_ref[0, 0], q_cmp, preferred_element_type=F32)
    s_c = s_c + tile4(jnp.where(cmp_end <= t_row, 0.0, NEG))
    e_c = jnp.exp2(s_c - col_max(s_c))
    z_c = jnp.sum(col_reduce8(e_c, jnp.add), axis=0, keepdims=True)
    inv_c = jnp.where(tile4(t_row >= CMP_LEN - 1) & (z_c > 0), 1.0 / z_c, 0.0)
    o_c = jnp.dot(vcmpt_ref[0, 0], e_c.astype(BF16), preferred_element_type=F32)[0:N_HD] * inv_c
    p_c = e_c * inv_c
    psum = p_c[:, 0:QBLK]
    for h in range(1, N_HPG):
        psum = psum + p_c[:, h * QBLK:(h + 1) * QBLK]
    parts, rest = [], psum
    for _ in range(3):
        parts.append(rest.astype(BF16))
        rest = rest - parts[-1].astype(F32)
    imp = jnp.dot(ovt_ref[...], jnp.concatenate(parts, axis=0), preferred_element_type=F32)
    sidx = lax.broadcasted_iota(jnp.int32, (n_slc, QBLK), 0)
    cur = t_row // SLC_LEN
    valid = sidx <= cur
    imp = jnp.where(valid, imp, -jnp.inf)
    imp_scr[...] = jnp.where((sidx == 0) | (sidx == cur), jnp.inf, imp)

    wrows = WIN + QBLK
    q_win = jnp.concatenate([q4s, jnp.zeros((AUX_ROWS, nw), BF16), aux_b,
                             jnp.zeros((KA_WIN - N_HD - 2 * AUX_ROWS, nw), BF16)], axis=0)
    s_w = jnp.dot(kw_ref[0, 0, pl.ds(pl.multiple_of(start, QBLK), wrows), :], q_win,
                  preferred_element_type=F32)
    s_w = jnp.concatenate([s_w[0:WKT] + tile4(wb_ref[0]), s_w[WKT:wrows - WKT],
                           s_w[wrows - WKT:] + tile4(wb_ref[1])], axis=0)
    p_w = jnp.exp2(s_w - col_max(s_w)).astype(BF16)
    v_w = jnp.concatenate([vw_ref[0, 0, qb * (QBLK // WKT) + j] for j in range(wrows // WKT)], axis=1)
    o_w = normalise(jnp.dot(v_w, p_w, preferred_element_type=F32))

    def gate_row(h, branch):
        return g_scr[pl.ds(GATE_NG + (g * N_HPG + h) * 3 + branch, 1), :]

    for h in range(N_HPG):
        hs = slice(h * QBLK, (h + 1) * QBLK)
        part_scr[:, hs] = gate_row(h, 0) * o_c[:, hs] + gate_row(h, 2) * o_w[:, hs]

    n_top = min(SLC_TOPN, n_slc)
    rank_scr[...] = jnp.zeros_like(rank_scr)
    sub8 = lax.broadcasted_iota(jnp.int32, (8, QBLK), 0)
    last_blk = (start + QBLK - 1) // SLC_LEN
    for ri in range(n_slc // 8):
        @pl.when((last_blk >= n_top) & (8 * ri <= last_blk))
        def _():
            rows = imp_scr[8 * ri:8 * ri + 8]
            for r in range(n_slc // 8):
                blk8 = imp_scr[8 * r:8 * r + 8]
                acc = rank_scr[8 * r:8 * r + 8]
                for ii in range(8):
                    row = rows[ii:ii + 1]
                    if ri < r:
                        before = row >= blk8
                    elif ri > r:
                        before = row > blk8
                    else:
                        before = (row > blk8) | ((row == blk8) & (sub8 > ii))
                    acc = acc + jnp.where(before, 1.0, 0.0)
                rank_scr[8 * r:8 * r + 8] = acc
    selbias = jnp.where((rank_scr[...] < n_top) & valid, 0.0, NEG)

    selb_scr[...] = tile4(selbias)
    n_tiles = qb // (SKT // QBLK) + 1
    q_tail = jnp.zeros((KA_SEL - N_HD - 2 * AUX_ROWS, nw), BF16)

    def sel_scores(kt):
        grp = pl.multiple_of((kt * SKT // SLC_LEN) // SEL_GRP * SEL_GRP, SEL_GRP)
        sel_rows = jnp.concatenate([selb_scr[pl.ds(grp, SEL_GRP), :], jnp.zeros((AUX_ROWS - SEL_GRP, nw), F32)], axis=0)
        q_sel = jnp.concatenate([q4s, sel_rows.astype(BF16), aux_b, q_tail], axis=0)
        krow = pl.multiple_of((kt + PAD_TILES) * SKT, SKT)
        return jnp.dot(ks_ref[0, 0, pl.ds(krow, SKT), :], q_sel, preferred_element_type=F32)

    def pass1(kt, mrun):
        s = sel_scores(kt)
        s_scr[pl.ds(pl.multiple_of(kt * SKT, SKT), SKT), :] = s
        return jnp.maximum(mrun, col_reduce8(s, jnp.maximum))

    def unrolled(fn):
        def body(i, carry):
            for u in range(UNROLL):
                carry = fn(i * UNROLL + u, carry)
            return carry
        return body

    def run_tiles(fn, n, init):
        n_u = n // UNROLL
        carry = lax.fori_loop(0, n_u, unrolled(fn), init)
        return lax.fori_loop(n_u * UNROLL, n, fn, carry)

    mrun = run_tiles(pass1, n_tiles - 1, jnp.full((8, nw), NEG, F32))
    last = n_tiles - 1
    s_last = sel_scores(last) + tile4(tri_ref[qb % (SKT // QBLK)])
    s_scr[pl.ds(pl.multiple_of(last * SKT, SKT), SKT), :] = s_last
    m_sel = jnp.max(jnp.maximum(mrun, col_reduce8(s_last, jnp.maximum)), axis=0, keepdims=True)

    def pass2(kt, acc):
        s = s_scr[pl.ds(pl.multiple_of(kt * SKT, SKT), SKT), :]
        pr = jnp.exp2(s - m_sel).astype(BF16)
        return acc + jnp.dot(vs_ref[0, 0, kt + PAD_TILES], pr, preferred_element_type=F32)

    o_s = normalise(run_tiles(pass2, n_tiles, jnp.zeros((VROWS, nw), F32)))

    tot = [part_scr[:, h * QBLK:(h + 1) * QBLK] + gate_row(h, 1) * o_s[:, h * QBLK:(h + 1) * QBLK]
           for h in range(N_HPG)]
    o = jnp.concatenate(tot, axis=0).T
    y_ref[0] = (o * _silu(z_ref[0])).astype(BF16)


def _edge_biases():
    kl = np.arange(SKT)[:, None]
    ql = np.arange(QBLK)[None, :]
    tri = np.stack([np.where(kl <= par * QBLK + ql, 0.0, NEG) for par in range(SKT // QBLK)])
    kk = np.arange(WKT)[:, None]
    wb = np.stack([np.where(kk > ql, 0.0, NEG), np.where(kk <= ql + WKT - QBLK, 0.0, NEG)])
    return jnp.asarray(tri, F32), jnp.asarray(wb, F32)


def _nsa(p, spieces, kcmp, vcmpt, ovt, ks, vs, kw, vw):
    bsz, t, _ = p.shape
    n_slc = t // SLC_LEN
    gw = N_HPG * N_HD
    tri, wb = _edge_biases()
    per_bg = lambda a: pl.BlockSpec((1, 1) + a.shape[2:], lambda bi, g, i, sp: (bi, g) + (0,) * (a.ndim - 2))
    const = lambda a: pl.BlockSpec(a.shape, lambda bi, g, i, sp: (0,) * a.ndim)
    grid_spec = pltpu.PrefetchScalarGridSpec(
        num_scalar_prefetch=1,
        grid=(bsz, N_KV, t // QBLK),
        in_specs=[pl.BlockSpec((1, QBLK, gw), lambda bi, g, i, sp: (bi, i, COL_NQ // gw + g)),
                  pl.BlockSpec((1, QBLK, LANES), lambda bi, g, i, sp: (bi, i, COL_GATES // LANES)),
                  pl.BlockSpec((1, QBLK, gw), lambda bi, g, i, sp: (bi, i, COL_NZ // gw + g)),
                  per_bg(kcmp), per_bg(vcmpt), const(ovt),
                  per_bg(ks), per_bg(vs), per_bg(kw), per_bg(vw), const(tri), const(wb)],
        out_specs=pl.BlockSpec((1, QBLK, gw), lambda bi, g, i, sp: (bi, i, g)),
        scratch_shapes=[pltpu.VMEM((t, N_HPG * QBLK), F32),
                        pltpu.VMEM((LANES, QBLK), F32),
                        pltpu.VMEM((n_slc, QBLK), F32),
                        pltpu.VMEM((n_slc, QBLK), F32),
                        pltpu.VMEM((N_HD, N_HPG * QBLK), F32),
                        pltpu.VMEM((n_slc, N_HPG * QBLK), F32)],
    )
    return pl.pallas_call(
        _nsa_kernel,
        grid_spec=grid_spec,
        out_shape=jax.ShapeDtypeStruct((bsz, t, N_WIDTH), BF16),
        compiler_params=_cparams(("parallel", "parallel", "arbitrary")),
        name="nsa_attention",
    )(spieces, p, p, p, kcmp, vcmpt, ovt, ks, vs, kw, vw, tri, wb)


def _outproj_kernel(ym_ref, yn_ref, w_ref, x_ref, gate_ref, fg_ref, o_ref, *, final):
    y = jnp.dot(ym_ref[0], w_ref[0:M_WIDTH, :], preferred_element_type=F32)
    y = y + jnp.dot(yn_ref[0], w_ref[M_WIDTH:, :], preferred_element_type=F32)
    hres = x_ref[0] + gate_ref[0] * y
    if final:
        ms = jnp.mean(hres * hres, axis=-1, keepdims=True)
        hres = hres * lax.rsqrt(ms + EPS) * fg_ref[...]
    o_ref[0] = hres


def _outproj(ym, yn, w, x, gate, fg, final):
    bsz, t, d = x.shape
    tm = 512
    return pl.pallas_call(
        functools.partial(_outproj_kernel, final=final),
        grid=(bsz, t // tm),
        in_specs=[pl.BlockSpec((1, tm, M_WIDTH), lambda bi, i: (bi, i, 0)),
                  pl.BlockSpec((1, tm, N_WIDTH), lambda bi, i: (bi, i, 0)),
                  pl.BlockSpec((M_WIDTH + N_WIDTH, d), lambda bi, i: (0, 0)),
                  pl.BlockSpec((1, tm, d), lambda bi, i: (bi, i, 0)),
                  pl.BlockSpec((1, 1, d), lambda bi, i: (bi, 0, 0)),
                  pl.BlockSpec((1, d), lambda bi, i: (0, 0))],
        out_specs=pl.BlockSpec((1, tm, d), lambda bi, i: (bi, i, 0)),
        out_shape=jax.ShapeDtypeStruct((bsz, t, d), F32),
        compiler_params=_cparams(("parallel", "parallel")),
        name="outproj_residual",
    )(ym, yn, w, x, gate, fg)


SRC_MI = 4 * M_WIDTH
SRC_NQ = SRC_MI + 2 * M_HEADS
SRC_NG = SRC_NQ + N_WIDTH + 6 * KV_W
SRC_NZ = SRC_NG + 3 * N_HEADS


def _reorder_cols(a):
    parts = [a[..., 0:SRC_MI], a[..., SRC_NQ:SRC_NG], a[..., SRC_NZ:SRC_NZ + N_WIDTH], a[..., SRC_MI:SRC_NQ],
             a[..., SRC_NG:SRC_NZ]]
    used = sum(x.shape[-1] for x in parts)
    parts.append(jnp.zeros(a.shape[:-1] + (NP_PAD - used,), a.dtype))
    return jnp.concatenate(parts, axis=-1)


WP_TILE = 256
WP_GATE_TILE = COL_GATES // WP_TILE


def _wprep_kernel(w_ref, g1_ref, g2_ref, o_ref):
    j = pl.program_id(0)
    d = o_ref.shape[0]

    @pl.when(j < WP_GATE_TILE)
    def _():
        for c in range(d // WP_TILE):
            cs = slice(c * WP_TILE, (c + 1) * WP_TILE)
            o_ref[cs, :] = w_ref[0, :, cs].T.astype(BF16)

    @pl.when(j == WP_GATE_TILE)
    def _():
        n_gate = g1_ref.shape[1] + g2_ref.shape[1]
        gt = jnp.concatenate([g1_ref[0], g2_ref[0], jnp.zeros((LANES - n_gate, d), F32)], axis=0)
        for c in range(d // WP_TILE):
            cs = slice(c * WP_TILE, (c + 1) * WP_TILE)
            o_ref[cs, 0:LANES] = gt[:, cs].T.astype(BF16)
        o_ref[:, LANES:] = jnp.zeros((d, WP_TILE - LANES), BF16)

    @pl.when(j > WP_GATE_TILE)
    def _():
        o_ref[...] = jnp.zeros_like(o_ref)


def _wprep(w_t, layer):
    _, n, d = w_t.shape
    n_big = COL_GATES // WP_TILE

    def src_row(j):
        return jnp.where(j < COL_NQ // WP_TILE, j * WP_TILE,
                         jnp.where(j < COL_NZ // WP_TILE, SRC_NQ + (j - COL_NQ // WP_TILE) * WP_TILE,
                                   jnp.where(j < n_big, SRC_NZ + (j - COL_NZ // WP_TILE) * WP_TILE, 0)))

    el = pl.Element
    return pl.pallas_call(
        _wprep_kernel,
        grid=(NP_PAD // WP_TILE,),
        in_specs=[pl.BlockSpec((el(1), el(WP_TILE), el(d)), lambda j: (layer, pl.multiple_of(src_row(j), 8), 0)),
                  pl.BlockSpec((el(1), el(SRC_NQ - SRC_MI), el(d)), lambda j: (layer, SRC_MI, 0)),
                  pl.BlockSpec((el(1), el(SRC_NZ - SRC_NG), el(d)), lambda j: (layer, SRC_NG, 0))],
        out_specs=pl.BlockSpec((d, WP_TILE), lambda j: (0, j)),
        out_shape=jax.ShapeDtypeStruct((d, NP_PAD), BF16),
        compiler_params=_cparams(("parallel",)),
        name="inproj_weight_prep",
    )(w_t, w_t, w_t)


def _overlap_t(t):
    n_cmp_rows = t // CMP_STRIDE
    n_slc = t // SLC_LEN
    c0 = np.arange(n_cmp_rows) * CMP_STRIDE
    s0 = np.arange(n_slc) * SLC_LEN
    ov = (c0[None, :] <= s0[:, None] + SLC_LEN - 1) & (c0[None, :] + CMP_LEN - 1 >= s0[:, None])
    ov[:, (t - CMP_LEN) // CMP_STRIDE + 1:] = False
    return jnp.asarray(np.concatenate([ov] * 3, axis=1), BF16)


def kernel(x, c, ln_g, w_ada, b_ada, w_in, b_in, m_conv_w, m_conv_b, m_wq, m_wk, m_norm_w, m_skip, m_f_bias,
           n_pos_k, n_pos_v, n_w1_k, n_w2_k, n_w1_v, n_w2_v, w_out, final_g):
    out_dtype = x.dtype
    bsz, t, d = x.shape
    depth = ln_g.shape[0]
    h_res = x.astype(F32)
    c8 = jnp.zeros((8, d), F32).at[:bsz].set(c.astype(F32))
    slopes_np = np.array([2.0 ** (-8.0 * (h + 1) / N_HEADS) for h in range(N_HEADS)], np.float32)
    rest = (slopes_np.astype(np.float64) * LOG2E).astype(np.float32)
    pieces = []
    for _ in range(3):
        pieces.append(rest.astype(jnp.bfloat16).astype(np.float32))
        rest = rest - pieces[-1]
    spieces = jnp.asarray(np.stack(pieces, axis=1).reshape(-1))
    ovt = _overlap_t(t)

    def w1cat(w1):
        w = w1.reshape(2, CMP_STRIDE, N_HD, CMP_HIDDEN)
        return jnp.concatenate([w[0], w[1]], axis=-1).astype(BF16)

    def w2pad(w2):
        return jnp.pad(w2, ((0, 0), (0, LANES - N_HD))).astype(BF16)

    for l in range(depth):
        mod = _ada(c8, w_ada, b_ada[l][None, :], l)[:bsz]
        shift, scale, gate = mod[:, None, 0:d], mod[:, None, d:2 * d], mod[:, None, 2 * d:3 * d]
        p = _inproj(h_res, ln_g[l][None, :], scale, shift,
                    _wprep(jnp.swapaxes(w_in, 1, 2), l), _reorder_cols(b_in[l])[None, :])
        fb_row = jnp.zeros((1, LANES), F32).at[0, M_HEADS:2 * M_HEADS].set(m_f_bias[l])
        y_m = _mlstm(p, m_conv_w[l], m_conv_b[l][None, :], m_wq[l].astype(BF16), m_wk[l].astype(BF16),
                     m_norm_w[l][None, :], m_skip[l][None, :], fb_row)
        kcmp, vcmpt = _compress(p, n_pos_k[l].reshape(1, -1), n_pos_v[l].reshape(1, -1),
                                n_w1_k[l].astype(BF16), w1cat(n_w1_k[l]), w2pad(n_w2_k[l]),
                                n_w1_v[l].astype(BF16), w1cat(n_w1_v[l]), w2pad(n_w2_v[l]))
        ks, vs, kw, vw = _relayout(p)
        y_n = _nsa(p, spieces, kcmp, vcmpt, ovt, ks, vs, kw, vw)
        h_res = _outproj(y_m, y_n, w_out[l].astype(BF16), h_res, gate, final_g[None, :], l == depth - 1)
    return h_res.astype(out_dtype)
```

```python
import functools

import numpy as np
import jax
import jax.numpy as jnp
from jax import lax
from jax.experimental import pallas as pl
from jax.experimental.pallas import tpu as pltpu

F32 = jnp.float32
BF16 = jnp.bfloat16
HIGHEST = lax.Precision.HIGHEST

EPS = 1e-6
M_HEADS = 4
M_HD = 256
M_WIDTH = M_HEADS * M_HD
CONV_K = 4
M_CHUNK = 256
N_HEADS = 16
N_HD = 64
N_KV = 4
N_HPG = N_HEADS // N_KV
N_WIDTH = N_HEADS * N_HD
KV_W = N_KV * N_HD
CMP_LEN = 32
CMP_STRIDE = 16
CMP_HIDDEN = 2 * N_HD
SLC_LEN = 64
SLC_TOPN = 16
WIN = 512
QBLK = 256
SKT = 256
WKT = 256
PAD_TILES = WIN // SKT
SEL_GRP = 8
KA_SEL = 128
KA_WIN = 128
VROWS = 80
UNROLL = 8
AUX_ROWS = 16
LOG2E = 1.4426950408889634

COL_MX, COL_MV, COL_MO, COL_MZ = 0, 1024, 2048, 3072
COL_NQ = 4096
COL_KC, COL_VC, COL_KS, COL_VS, COL_KW, COL_VW = 5120, 5376, 5632, 5888, 6144, 6400
COL_NZ = 6656
COL_GATES = 7680
GATE_NG = 2 * M_HEADS
NP_PAD = 8192
NORM_CHUNKS = 4
LANES = 128
NEG = -1e30
VMEM_LIMIT = 56 * 1024 * 1024


def _cparams(sem):
    return pltpu.CompilerParams(dimension_semantics=sem, vmem_limit_bytes=VMEM_LIMIT)


def _silu(x):
    return x * jax.nn.sigmoid(x)


def _log_sigmoid(x):
    return jnp.minimum(x, 0.0) - jnp.log1p(jnp.exp(-jnp.abs(x)))


def _ada_kernel(c_ref, w_ref, b_ref, o_ref):
    s = _silu(c_ref[...])
    o_ref[...] = jnp.dot(s, w_ref[0], precision=HIGHEST, preferred_element_type=F32) + b_ref[...]


def _ada(c8, w, b, layer):
    _, d, n = w.shape
    tn = 1024
    return pl.pallas_call(
        _ada_kernel,
        grid=(n // tn,),
        in_specs=[pl.BlockSpec((8, d), lambda j: (0, 0)),
                  pl.BlockSpec((1, d, tn), lambda j: (layer, 0, j)),
                  pl.BlockSpec((1, tn), lambda j: (0, j))],
        out_specs=pl.BlockSpec((8, tn), lambda j: (0, j)),
        out_shape=jax.ShapeDtypeStruct((8, n), F32),
        compiler_params=_cparams(("parallel",)),
        name="ada_mod",
    )(c8, w, b)


def _inproj_kernel(x_ref, g_ref, sc_ref, sh_ref, w_ref, b_ref, o_ref, h_ref):
    first = pl.program_id(2) == 0

    @pl.when(first)
    def _():
        tm = x_ref.shape[1]
        ck = tm // NORM_CHUNKS
        for c in range(NORM_CHUNKS):
            rows = slice(c * ck, (c + 1) * ck)
            x = x_ref[0, rows, :]
            ms = jnp.mean(x * x, axis=-1, keepdims=True)
            h = x * lax.rsqrt(ms + EPS) * g_ref[...]
            h = (h * (1.0 + sc_ref[0]) + sh_ref[0]).astype(BF16)
            h_ref[rows, :] = h
            o_ref[0, rows, :] = jnp.dot(h, w_ref[...], preferred_element_type=F32) + b_ref[...]

    @pl.when(jnp.logical_not(first))
    def _():
        o_ref[0] = jnp.dot(h_ref[...], w_ref[...], preferred_element_type=F32) + b_ref[...]


def _inproj(x, g, scale, shift, w, b):
    bsz, t, d = x.shape
    n = w.shape[1]
    tm, tn = 1024, 1024
    return pl.pallas_call(
        _inproj_kernel,
        grid=(bsz, t // tm, n // tn),
        in_specs=[pl.BlockSpec((1, tm, d), lambda bi, i, j: (bi, i, 0)),
                  pl.BlockSpec((1, d), lambda bi, i, j: (0, 0)),
                  pl.BlockSpec((1, 1, d), lambda bi, i, j: (bi, 0, 0)),
                  pl.BlockSpec((1, 1, d), lambda bi, i, j: (bi, 0, 0)),
                  pl.BlockSpec((d, tn), lambda bi, i, j: (0, j)),
                  pl.BlockSpec((1, tn), lambda bi, i, j: (0, j))],
        out_specs=pl.BlockSpec((1, tm, tn), lambda bi, i, j: (bi, i, j)),
        out_shape=jax.ShapeDtypeStruct((bsz, t, n), F32),
        scratch_shapes=[pltpu.VMEM((tm, d), BF16)],
        compiler_params=_cparams(("parallel", "parallel", "arbitrary")),
        name="norm_inproj",
    )(x, g, scale, shift, w, b)


def _mlstm_kernel(x_ref, v_ref, o_ref, z_ref, gt_ref, cw_ref, cb_ref, wq_ref, wk_ref, nw_ref, sk_ref, fb_ref,
                  y_ref, c_scr, n_scr, m_scr, xp_scr):
    L = M_CHUNK

    @pl.when(pl.program_id(1) == 0)
    def _():
        c_scr[...] = jnp.zeros_like(c_scr)
        n_scr[...] = jnp.zeros_like(n_scr)
        m_scr[...] = jnp.zeros_like(m_scr)
        xp_scr[...] = jnp.zeros_like(xp_scr)

    x = x_ref[0]
    prev = xp_scr[...]
    row8 = lax.broadcasted_iota(jnp.int32, (8, M_WIDTH), 0)
    cw = cw_ref[...]
    xc = cb_ref[...] + x * cw[CONV_K - 1:CONV_K, :]
    for sft in range(1, CONV_K):
        xr = pltpu.roll(x, sft, 0)
        top = jnp.where(row8 < sft, pltpu.roll(prev, sft, 0), xr[0:8])
        xs = jnp.concatenate([top, xr[8:]], axis=0)
        xc = xc + xs * cw[CONV_K - 1 - sft:CONV_K - sft, :]
    xp_scr[...] = x[L - 8:L]
    xc = _silu(xc)

    gt = gt_ref[0]
    col = lax.broadcasted_iota(jnp.int32, (L, LANES), 1)
    logf = _log_sigmoid(gt + fb_ref[...])
    a_c = jnp.where((col >= M_HEADS) & (col < 2 * M_HEADS), logf, gt)
    ri = lax.broadcasted_iota(jnp.int32, (L, L), 0)
    ci = lax.broadcasted_iota(jnp.int32, (L, L), 1)
    causal = ri >= ci
    tri = causal.astype(F32)
    tri_t = (ri <= ci).astype(F32)
    b_c = jnp.dot(tri, a_c, precision=HIGHEST, preferred_element_type=F32)
    a_r = a_c.T
    b_r = jnp.dot(a_r[0:8], tri_t, precision=HIGHEST, preferred_element_type=F32)

    for h in range(M_HEADS):
        sl = slice(h * M_HD, (h + 1) * M_HD)
        xh = xc[:, sl]
        xb = xh.astype(BF16)
        q = jnp.dot(xb, wq_ref[h], preferred_element_type=F32)
        k = jnp.dot(xb, wk_ref[h], preferred_element_type=F32) * (M_HD ** -0.5)
        vb = v_ref[0, :, sl].astype(BF16)
        qb = q.astype(BF16)
        kb = k.astype(BF16)

        bt = b_c[:, M_HEADS + h:M_HEADS + h + 1]
        ic = a_c[:, h:h + 1]
        bs = b_r[M_HEADS + h:M_HEADS + h + 1, :]
        ir = a_r[h:h + 1, :]
        m_prev = m_scr[h][:, 0:1]

        dm = jnp.where(causal, bt - bs + ir, -jnp.inf)
        inter = bt + m_prev
        m_t = jnp.maximum(inter, jnp.max(dm, axis=-1, keepdims=True))
        w_in = jnp.exp(dm - m_t)
        w_st = jnp.exp(inter - m_t)
        s = lax.dot_general(qb, kb, (((1,), (1,)), ((), ())), preferred_element_type=F32) * w_in
        cmat = c_scr[h]
        nvec = n_scr[h]
        num = w_st * jnp.dot(qb, cmat.astype(BF16), preferred_element_type=F32) \
            + jnp.dot(s.astype(BF16), vb, preferred_element_type=F32)
        den = w_st * jnp.sum(q * nvec, axis=-1, keepdims=True) + jnp.sum(s, axis=-1, keepdims=True)
        hh = num / jnp.maximum(jnp.abs(den), jnp.exp(-m_t))

        b_last = bt[L - 1:L, :]
        w_end = b_last - bt + ic
        m_new = jnp.maximum(b_last + m_prev, jnp.max(w_end, axis=0, keepdims=True))
        decay = jnp.exp(b_last + m_prev - m_new)
        kwt = k * jnp.exp(w_end - m_new)
        c_scr[h] = decay * cmat + lax.dot_general(kwt.astype(BF16), vb, (((0,), (0,)), ((), ())),
                                                  preferred_element_type=F32)
        n_scr[h] = decay * nvec + jnp.sum(kwt, axis=0, keepdims=True)
        m_scr[h] = jnp.broadcast_to(m_new, (1, LANES))

        mu = jnp.mean(hh, axis=-1, keepdims=True)
        hc = hh - mu
        var = jnp.mean(hc * hc, axis=-1, keepdims=True)
        hn = hc * lax.rsqrt(var + EPS) * nw_ref[:, sl]
        out = jax.nn.sigmoid(o_ref[0, :, sl]) * hn + sk_ref[:, sl] * xh
        y_ref[0, :, sl] = (out * _silu(z_ref[0, :, sl])).astype(BF16)


def _mlstm(p, conv_w, conv_b, wq, wk, norm_w, skip, fb_row):
    bsz, t, _ = p.shape
    L = M_CHUNK
    cb = lambda c: (lambda bi, i: (bi, i, c))
    full2 = lambda bi, i: (0, 0)
    full3 = lambda bi, i: (0, 0, 0)
    return pl.pallas_call(
        _mlstm_kernel,
        grid=(bsz, t // L),
        in_specs=[pl.BlockSpec((1, L, M_WIDTH), cb(COL_MX // M_WIDTH)),
                  pl.BlockSpec((1, L, M_WIDTH), cb(COL_MV // M_WIDTH)),
                  pl.BlockSpec((1, L, M_WIDTH), cb(COL_MO // M_WIDTH)),
                  pl.BlockSpec((1, L, M_WIDTH), cb(COL_MZ // M_WIDTH)),
                  pl.BlockSpec((1, L, LANES), cb(COL_GATES // LANES)),
                  pl.BlockSpec((CONV_K, M_WIDTH), full2),
                  pl.BlockSpec((1, M_WIDTH), full2),
                  pl.BlockSpec((M_HEADS, M_HD, M_HD), full3),
                  pl.BlockSpec((M_HEADS, M_HD, M_HD), full3),
                  pl.BlockSpec((1, M_WIDTH), full2),
                  pl.BlockSpec((1, M_WIDTH), full2),
                  pl.BlockSpec((1, LANES), full2)],
        out_specs=pl.BlockSpec((1, L, M_WIDTH), lambda bi, i: (bi, i, 0)),
        out_shape=jax.ShapeDtypeStruct((bsz, t, M_WIDTH), BF16),
        scratch_shapes=[pltpu.VMEM((M_HEADS, M_HD, M_HD), F32),
                        pltpu.VMEM((M_HEADS, 1, M_HD), F32),
                        pltpu.VMEM((M_HEADS, 1, LANES), F32),
                        pltpu.VMEM((8, M_WIDTH), F32)],
        compiler_params=_cparams(("parallel", "arbitrary")),
        name="mlstm_group",
    )(p, p, p, p, p, conv_w, conv_b, wq, wk, norm_w, skip, fb_row)


def _compress_kernel(kc0_ref, kc1_ref, vc0_ref, vc1_ref, posk_ref, posv_ref, w1k_ref, w1kc_ref, w2k_ref,
                     w1v_ref, w1vc_ref, w2v_ref, kcmp_ref, vcmpt_ref):
    nb = kcmp_ref.shape[2]

    def hidden(src_refs, pos_ref, w1_ref, w1c_ref):
        acc = jnp.zeros((N_KV * nb, 2 * CMP_HIDDEN), F32)
        for l in range(CMP_STRIDE):
            xl = [r[0, pl.ds(l, nb, stride=CMP_STRIDE), :] for r in src_refs]
            xs = jnp.concatenate([x[:, g * N_HD:(g + 1) * N_HD] for x in xl for g in range(LANES // N_HD)], axis=0)
            acc = acc + jnp.dot(xs.astype(BF16), w1c_ref[l], preferred_element_type=F32)
        first = acc[:, :CMP_HIDDEN]
        second = acc[:, CMP_HIDDEN:]
        posb = jnp.dot(jnp.broadcast_to(pos_ref[...], (8, CMP_LEN * N_HD)).astype(BF16), w1_ref[...],
                       preferred_element_type=F32)[0:1]
        hid = first + pltpu.roll(second, N_KV * nb - 1, 0) + posb
        return jax.nn.gelu(hid).astype(BF16)

    hk = hidden((kc0_ref, kc1_ref), posk_ref, w1k_ref, w1kc_ref)
    kc = jnp.dot(hk, w2k_ref[...], preferred_element_type=F32)
    hv = hidden((vc0_ref, vc1_ref), posv_ref, w1v_ref, w1vc_ref)
    vc = jnp.dot(hv, w2v_ref[...], preferred_element_type=F32)
    j = lax.broadcasted_iota(jnp.int32, (nb, 1), 0)
    lane = lax.broadcasted_iota(jnp.int32, (nb, N_HD), 1)
    n_real = (nb * CMP_STRIDE - CMP_LEN) // CMP_STRIDE + 1
    aux = jnp.where(lane < 3, (j * CMP_STRIDE).astype(F32),
                    jnp.where(lane < 6, (CMP_LEN - 1) * 0.5,
                              jnp.where((lane == 6) & (j >= n_real), 1.0, 0.0))).astype(BF16)
    ones_blk = jnp.where(lax.broadcasted_iota(jnp.int32, (VROWS - N_HD, nb), 0) == 0, 1.0, 0.0)
    for g in range(N_KV):
        kg = kc[g * nb:(g + 1) * nb, 0:N_HD]
        k_hi = kg.astype(BF16)
        k_lo = (kg - k_hi.astype(F32)).astype(BF16)
        kcmp_ref[0, g] = jnp.concatenate([k_hi, k_lo, k_hi, aux], axis=1)
        vt = vc[g * nb:(g + 1) * nb].T
        vcmpt_ref[0, g] = jnp.concatenate([vt[0:N_HD], ones_blk], axis=0).astype(BF16)


def _compress(p, posk, posv, w1k, w1kc, w2k, w1v, w1vc, w2v):
    bsz, t, _ = p.shape
    nb = t // CMP_STRIDE
    cb = lambda c: (lambda bi: (bi, 0, c))
    f2 = lambda bi: (0, 0)
    f3 = lambda bi: (0, 0, 0)
    wspecs = [pl.BlockSpec((CMP_LEN * N_HD, CMP_HIDDEN), f2),
              pl.BlockSpec((CMP_STRIDE, N_HD, 2 * CMP_HIDDEN), f3),
              pl.BlockSpec((CMP_HIDDEN, LANES), f2)]
    return pl.pallas_call(
        _compress_kernel,
        grid=(bsz,),
        in_specs=[pl.BlockSpec((1, t, LANES), cb(COL_KC // LANES)),
                  pl.BlockSpec((1, t, LANES), cb(COL_KC // LANES + 1)),
                  pl.BlockSpec((1, t, LANES), cb(COL_VC // LANES)),
                  pl.BlockSpec((1, t, LANES), cb(COL_VC // LANES + 1)),
                  pl.BlockSpec((1, CMP_LEN * N_HD), f2),
                  pl.BlockSpec((1, CMP_LEN * N_HD), f2)] + wspecs + wspecs,
        out_specs=[pl.BlockSpec((1, N_KV, nb, 4 * N_HD), lambda bi: (bi, 0, 0, 0)),
                   pl.BlockSpec((1, N_KV, VROWS, nb), lambda bi: (bi, 0, 0, 0))],
        out_shape=[jax.ShapeDtypeStruct((bsz, N_KV, nb, 4 * N_HD), BF16),
                   jax.ShapeDtypeStruct((bsz, N_KV, VROWS, nb), BF16)],
        compiler_params=_cparams(("parallel",)),
        name="nsa_compress",
    )(p, p, p, p, posk, posv, w1k, w1kc, w2k, w1v, w1vc, w2v)


def _relayout_kernel(ks_ref, vs_ref, kw_ref, vw_ref, ksa_ref, vsa_ref, kwa_ref, vwa_ref):
    i = pl.program_id(1)
    is_pad = i < PAD_TILES
    base = (i - PAD_TILES) * SKT
    row = lax.broadcasted_iota(jnp.int32, (SKT, 1), 0)
    pos = jnp.where(is_pad, 0, base + row)
    blk = pos // SLC_LEN
    p_hi = (blk * SLC_LEN).astype(F32)
    p_lo = (pos - blk * SLC_LEN).astype(F32)
    flag = jnp.where(is_pad, 1.0, 0.0)
    lane = lax.broadcasted_iota(jnp.int32, (SKT, N_HD), 1)
    al = lane - AUX_ROWS
    mid_w = jnp.where((al >= 0) & (al < 3), p_hi,
                      jnp.where((al >= 3) & (al < 6), p_lo, jnp.where(al == 6, flag, 0.0)))
    mid_s = jnp.where(lane == blk % SEL_GRP, 1.0, mid_w)
    ks = ks_ref[0]
    kw = kw_ref[0]
    for g in range(N_KV):
        sl = slice(g * N_HD, (g + 1) * N_HD)
        ksa_ref[0, g] = jnp.concatenate([ks[:, sl], mid_s], axis=1).astype(BF16)
        kwa_ref[0, g] = jnp.concatenate([kw[:, sl], mid_w], axis=1).astype(BF16)
    ones_blk = jnp.where(lax.broadcasted_iota(jnp.int32, (VROWS - N_HD, SKT), 0) == 0, 1.0, 0.0)
    vst = vs_ref[0].T
    vwt = vw_ref[0].T
    for g in range(N_KV):
        sl = slice(g * N_HD, (g + 1) * N_HD)
        vsa_ref[0, g, 0] = jnp.concatenate([vst[sl], ones_blk], axis=0).astype(BF16)
        vwg = jnp.concatenate([vwt[sl], ones_blk], axis=0).astype(BF16)
        for j in range(SKT // WKT):
            vwa_ref[0, g, j] = vwg[:, j * WKT:(j + 1) * WKT]


def _relayout(p):
    bsz, t, _ = p.shape
    nt = t // SKT + PAD_TILES
    wpt = SKT // WKT
    cb = lambda c: (lambda bi, i: (bi, jnp.maximum(i - PAD_TILES, 0), c))
    return pl.pallas_call(
        _relayout_kernel,
        grid=(bsz, nt),
        in_specs=[pl.BlockSpec((1, SKT, KV_W), cb(COL_KS // KV_W)),
                  pl.BlockSpec((1, SKT, KV_W), cb(COL_VS // KV_W)),
                  pl.BlockSpec((1, SKT, KV_W), cb(COL_KW // KV_W)),
                  pl.BlockSpec((1, SKT, KV_W), cb(COL_VW // KV_W))],
        out_specs=[pl.BlockSpec((1, N_KV, SKT, KA_SEL), lambda bi, i: (bi, 0, i, 0)),
                   pl.BlockSpec((1, N_KV, 1, VROWS, SKT), lambda bi, i: (bi, 0, i, 0, 0)),
                   pl.BlockSpec((1, N_KV, SKT, KA_WIN), lambda bi, i: (bi, 0, i, 0)),
                   pl.BlockSpec((1, N_KV, wpt, VROWS, WKT), lambda bi, i: (bi, 0, i, 0, 0))],
        out_shape=[jax.ShapeDtypeStruct((bsz, N_KV, nt * SKT, KA_SEL), BF16),
                   jax.ShapeDtypeStruct((bsz, N_KV, nt, VROWS, SKT), BF16),
                   jax.ShapeDtypeStruct((bsz, N_KV, nt * SKT, KA_WIN), BF16),
                   jax.ShapeDtypeStruct((bsz, N_KV, nt * wpt, VROWS, WKT), BF16)],
        compiler_params=_cparams(("parallel", "parallel")),
        name="nsa_relayout",
    )(p, p, p, p)


def _nsa_kernel(sp_ref, q_ref, gt_ref, z_ref, kcmp_ref, vcmpt_ref, ovt_ref, ks_ref, vs_ref, kw_ref,
                vw_ref, tri_ref, wb_ref, y_ref, s_scr, g_scr, imp_scr, rank_scr, part_scr, selb_scr):
    g = pl.program_id(1)
    qb = pl.program_id(2)
    start = qb * QBLK
    nb = kcmp_ref.shape[2]
    n_slc = ovt_ref.shape[0]

    qt = (q_ref[0] * (N_HD ** -0.5)).T
    q4t = jnp.concatenate([qt[h * N_HD:(h + 1) * N_HD] for h in range(N_HPG)], axis=1)
    t_row = start + lax.broadcasted_iota(jnp.int32, (1, QBLK), 1)

    g_scr[...] = jax.nn.sigmoid(gt_ref[0]).T

    nw = N_HPG * QBLK
    hl = lax.broadcasted_iota(jnp.int32, (AUX_ROWS, nw), 1) // QBLK
    ar = lax.broadcasted_iota(jnp.int32, (AUX_ROWS, nw), 0)

    def per_head(vals):
        out = jnp.full((AUX_ROWS, nw), vals[N_HPG - 1], F32)
        for h in range(N_HPG - 2, -1, -1):
            out = jnp.where(hl == h, vals[h], out)
        return out

    pieces = [per_head([sp_ref[(g * N_HPG + h) * 3 + j] for h in range(N_HPG)]) for j in range(3)]
    aux = jnp.where(ar == 6, NEG, 0.0)
    for j in range(3):
        aux = jnp.where((ar == j) | (ar == j + 3), pieces[j], aux)
    aux_b = aux.astype(BF16)
    q4l = q4t * LOG2E
    q4s = q4l.astype(BF16)
    q4lo = (q4l - q4s.astype(F32)).astype(BF16)

    def tile4(a):
        return jnp.concatenate([a] * N_HPG, axis=1)

    def col_reduce8(s, op):
        out = s[0:8]
        for r in range(1, s.shape[0] // 8):
            out = op(out, s[8 * r:8 * r + 8])
        return out

    def col_max(s):
        return jnp.max(col_reduce8(s, jnp.maximum), axis=0, keepdims=True)

    def normalise(acc):
        den = acc[N_HD:N_HD + 1]
        return acc[0:N_HD] / jnp.where(den > 0, den, 1.0)

    q_cmp = jnp.concatenate([q4s, q4s, q4lo, aux_b, jnp.zeros((N_HD - AUX_ROWS, nw), BF16)], axis=0)
    cmp_end = lax.broadcasted_iota(jnp.int32, (nb, 1), 0) * CMP_STRIDE + (CMP_LEN - 1)
    s_c = jnp.dot(kcmp_ref[0, 0], q_cmp, preferred_element_type=F32)
    s_c = s_c + tile4(jnp.where(cmp_end <= t_row, 0.0, NEG))
    e_c = jnp.exp2(s_c - col_max(s_c))
    z_c = jnp.sum(col_reduce8(e_c, jnp.add), axis=0, keepdims=True)
    inv_c = jnp.where(tile4(t_row >= CMP_LEN - 1) & (z_c > 0), 1.0 / z_c, 0.0)
    o_c = jnp.dot(vcmpt_ref[0, 0], e_c.astype(BF16), preferred_element_type=F32)[0:N_HD] * inv_c
    p_c = e_c * inv_c
    psum = p_c[:, 0:QBLK]
    for h in range(1, N_HPG):
        psum = psum + p_c[:, h * QBLK:(h + 1) * QBLK]
    parts, rest = [], psum
    for _ in range(3):
        parts.append(rest.astype(BF16))
        rest = rest - parts[-1].astype(F32)
    imp = jnp.dot(ovt_ref[...], jnp.concatenate(parts, axis=0), preferred_element_type=F32)
    sidx = lax.broadcasted_iota(jnp.int32, (n_slc, QBLK), 0)
    cur = t_row // SLC_LEN
    valid = sidx <= cur
    imp = jnp.where(valid, imp, -jnp.inf)
    imp_scr[...] = jnp.where((sidx == 0) | (sidx == cur), jnp.inf, imp)

    wrows = WIN + QBLK
    q_win = jnp.concatenate([q4s, jnp.zeros((AUX_ROWS, nw), BF16), aux_b,
                             jnp.zeros((KA_WIN - N_HD - 2 * AUX_ROWS, nw), BF16)], axis=0)
    s_w = jnp.dot(kw_ref[0, 0, pl.ds(pl.multiple_of(start, QBLK), wrows), :], q_win,
                  preferred_element_type=F32)
    s_w = jnp.concatenate([s_w[0:WKT] + tile4(wb_ref[0]), s_w[WKT:wrows - WKT],
                           s_w[wrows - WKT:] + tile4(wb_ref[1])], axis=0)
    p_w = jnp.exp2(s_w - col_max(s_w)).astype(BF16)
    v_w = jnp.concatenate([vw_ref[0, 0, qb * (QBLK // WKT) + j] for j in range(wrows // WKT)], axis=1)
    o_w = normalise(jnp.dot(v_w, p_w, preferred_element_type=F32))

    def gate_row(h, branch):
        return g_scr[pl.ds(GATE_NG + (g * N_HPG + h) * 3 + branch, 1), :]

    for h in range(N_HPG):
        hs = slice(h * QBLK, (h + 1) * QBLK)
        part_scr[:, hs] = gate_row(h, 0) * o_c[:, hs] + gate_row(h, 2) * o_w[:, hs]

    n_top = min(SLC_TOPN, n_slc)
    rank_scr[...] = jnp.zeros_like(rank_scr)
    sub8 = lax.broadcasted_iota(jnp.int32, (8, QBLK), 0)
    last_blk = (start + QBLK - 1) // SLC_LEN
    for ri in range(n_slc // 8):
        @pl.when((last_blk >= n_top) & (8 * ri <= last_blk))
        def _():
            rows = imp_scr[8 * ri:8 * ri + 8]
            for r in range(n_slc // 8):
                blk8 = imp_scr[8 * r:8 * r + 8]
                acc = rank_scr[8 * r:8 * r + 8]
                for ii in range(8):
                    row = rows[ii:ii + 1]
                    if ri < r:
                        before = row >= blk8
                    elif ri > r:
                        before = row > blk8
                    else:
                        before = (row > blk8) | ((row == blk8) & (sub8 > ii))
                    acc = acc + jnp.where(before, 1.0, 0.0)
                rank_scr[8 * r:8 * r + 8] = acc
    selbias = jnp.where((rank_scr[...] < n_top) & valid, 0.0, NEG)

    selb_scr[...] = tile4(selbias)
    n_tiles = qb // (SKT // QBLK) + 1
    q_tail = jnp.zeros((KA_SEL - N_HD - 2 * AUX_ROWS, nw), BF16)

    def sel_scores(kt):
        grp = pl.multiple_of((kt * SKT // SLC_LEN) // SEL_GRP * SEL_GRP, SEL_GRP)
        sel_rows = jnp.concatenate([selb_scr[pl.ds(grp, SEL_GRP), :], jnp.zeros((AUX_ROWS - SEL_GRP, nw), F32)], axis=0)
        q_sel = jnp.concatenate([q4s, sel_rows.astype(BF16), aux_b, q_tail], axis=0)
        krow = pl.multiple_of((kt + PAD_TILES) * SKT, SKT)
        return jnp.dot(ks_ref[0, 0, pl.ds(krow, SKT), :], q_sel, preferred_element_type=F32)

    def pass1(kt, mrun):
        s = sel_scores(kt)
        s_scr[pl.ds(pl.multiple_of(kt * SKT, SKT), SKT), :] = s
        return jnp.maximum(mrun, col_reduce8(s, jnp.maximum))

    def run_tiles(fn, n, init):
        def group(size, first):
            def body(i, carry):
                for u in range(size):
                    carry = fn(first + i * size + u, carry)
                return carry
            return body

        n_u = n // UNROLL
        carry = lax.fori_loop(0, n_u, group(UNROLL, 0), init)
        done = n_u * UNROLL
        size = UNROLL // 2
        while size >= 1:
            take = (n // size) % 2
            carry = lax.fori_loop(0, take, group(size, done), carry)
            done = done + take * size
            size //= 2
        return carry

    mrun = run_tiles(pass1, n_tiles - 1, jnp.full((8, nw), NEG, F32))
    last = n_tiles - 1
    s_last = sel_scores(last) + tile4(tri_ref[qb % (SKT // QBLK)])
    s_scr[pl.ds(pl.multiple_of(last * SKT, SKT), SKT), :] = s_last
    m_sel = jnp.max(jnp.maximum(mrun, col_reduce8(s_last, jnp.maximum)), axis=0, keepdims=True)

    def pass2(kt, acc):
        s = s_scr[pl.ds(pl.multiple_of(kt * SKT, SKT), SKT), :]
        pr = jnp.exp2(s - m_sel).astype(BF16)
        return acc + jnp.dot(vs_ref[0, 0, kt + PAD_TILES], pr, preferred_element_type=F32)

    o_s = normalise(run_tiles(pass2, n_tiles, jnp.zeros((VROWS, nw), F32)))

    tot = [part_scr[:, h * QBLK:(h + 1) * QBLK] + gate_row(h, 1) * o_s[:, h * QBLK:(h + 1) * QBLK]
           for h in range(N_HPG)]
    o = jnp.concatenate(tot, axis=0).T
    y_ref[0] = (o * _silu(z_ref[0])).astype(BF16)


def _edge_biases():
    kl = np.arange(SKT)[:, None]
    ql = np.arange(QBLK)[None, :]
    tri = np.stack([np.where(kl <= par * QBLK + ql, 0.0, NEG) for par in range(SKT // QBLK)])
    kk = np.arange(WKT)[:, None]
    wb = np.stack([np.where(kk > ql, 0.0, NEG), np.where(kk <= ql + WKT - QBLK, 0.0, NEG)])
    return jnp.asarray(tri, F32), jnp.asarray(wb, F32)


def _nsa(p, spieces, kcmp, vcmpt, ovt, ks, vs, kw, vw):
    bsz, t, _ = p.shape
    n_slc = t // SLC_LEN
    gw = N_HPG * N_HD
    tri, wb = _edge_biases()
    per_bg = lambda a: pl.BlockSpec((1, 1) + a.shape[2:], lambda bi, g, i, sp: (bi, g) + (0,) * (a.ndim - 2))
    const = lambda a: pl.BlockSpec(a.shape, lambda bi, g, i, sp: (0,) * a.ndim)
    grid_spec = pltpu.PrefetchScalarGridSpec(
        num_scalar_prefetch=1,
        grid=(bsz, N_KV, t // QBLK),
        in_specs=[pl.BlockSpec((1, QBLK, gw), lambda bi, g, i, sp: (bi, i, COL_NQ // gw + g)),
                  pl.BlockSpec((1, QBLK, LANES), lambda bi, g, i, sp: (bi, i, COL_GATES // LANES)),
                  pl.BlockSpec((1, QBLK, gw), lambda bi, g, i, sp: (bi, i, COL_NZ // gw + g)),
                  per_bg(kcmp), per_bg(vcmpt), const(ovt),
                  per_bg(ks), per_bg(vs), per_bg(kw), per_bg(vw), const(tri), const(wb)],
        out_specs=pl.BlockSpec((1, QBLK, gw), lambda bi, g, i, sp: (bi, i, g)),
        scratch_shapes=[pltpu.VMEM((t, N_HPG * QBLK), F32),
                        pltpu.VMEM((LANES, QBLK), F32),
                        pltpu.VMEM((n_slc, QBLK), F32),
                        pltpu.VMEM((n_slc, QBLK), F32),
                        pltpu.VMEM((N_HD, N_HPG * QBLK), F32),
                        pltpu.VMEM((n_slc, N_HPG * QBLK), F32)],
    )
    return pl.pallas_call(
        _nsa_kernel,
        grid_spec=grid_spec,
        out_shape=jax.ShapeDtypeStruct((bsz, t, N_WIDTH), BF16),
        compiler_params=_cparams(("parallel", "parallel", "arbitrary")),
        name="nsa_attention",
    )(spieces, p, p, p, kcmp, vcmpt, ovt, ks, vs, kw, vw, tri, wb)


def _outproj_kernel(ym_ref, yn_ref, w_ref, x_ref, gate_ref, fg_ref, o_ref, *, final):
    y = jnp.dot(ym_ref[0], w_ref[0:M_WIDTH, :], preferred_element_type=F32)
    y = y + jnp.dot(yn_ref[0], w_ref[M_WIDTH:, :], preferred_element_type=F32)
    hres = x_ref[0] + gate_ref[0] * y
    if final:
        ms = jnp.mean(hres * hres, axis=-1, keepdims=True)
        hres = hres * lax.rsqrt(ms + EPS) * fg_ref[...]
    o_ref[0] = hres


def _outproj(ym, yn, w, x, gate, fg, final):
    bsz, t, d = x.shape
    tm = 512
    return pl.pallas_call(
        functools.partial(_outproj_kernel, final=final),
        grid=(bsz, t // tm),
        in_specs=[pl.BlockSpec((1, tm, M_WIDTH), lambda bi, i: (bi, i, 0)),
                  pl.BlockSpec((1, tm, N_WIDTH), lambda bi, i: (bi, i, 0)),
                  pl.BlockSpec((M_WIDTH + N_WIDTH, d), lambda bi, i: (0, 0)),
                  pl.BlockSpec((1, tm, d), lambda bi, i: (bi, i, 0)),
                  pl.BlockSpec((1, 1, d), lambda bi, i: (bi, 0, 0)),
                  pl.BlockSpec((1, d), lambda bi, i: (0, 0))],
        out_specs=pl.BlockSpec((1, tm, d), lambda bi, i: (bi, i, 0)),
        out_shape=jax.ShapeDtypeStruct((bsz, t, d), F32),
        compiler_params=_cparams(("parallel", "parallel")),
        name="outproj_residual",
    )(ym, yn, w, x, gate, fg)


SRC_MI = 4 * M_WIDTH
SRC_NQ = SRC_MI + 2 * M_HEADS
SRC_NG = SRC_NQ + N_WIDTH + 6 * KV_W
SRC_NZ = SRC_NG + 3 * N_HEADS


def _reorder_cols(a):
    parts = [a[..., 0:SRC_MI], a[..., SRC_NQ:SRC_NG], a[..., SRC_NZ:SRC_NZ + N_WIDTH], a[..., SRC_MI:SRC_NQ],
             a[..., SRC_NG:SRC_NZ]]
    used = sum(x.shape[-1] for x in parts)
    parts.append(jnp.zeros(a.shape[:-1] + (NP_PAD - used,), a.dtype))
    return jnp.concatenate(parts, axis=-1)


WP_TILE = 256
WP_GATE_TILE = COL_GATES // WP_TILE


def _wprep_kernel(w_ref, g1_ref, g2_ref, o_ref):
    j = pl.program_id(0)
    d = o_ref.shape[0]

    @pl.when(j < WP_GATE_TILE)
    def _():
        for c in range(d // WP_TILE):
            cs = slice(c * WP_TILE, (c + 1) * WP_TILE)
            o_ref[cs, :] = w_ref[0, :, cs].T.astype(BF16)

    @pl.when(j == WP_GATE_TILE)
    def _():
        n_gate = g1_ref.shape[1] + g2_ref.shape[1]
        gt = jnp.concatenate([g1_ref[0], g2_ref[0], jnp.zeros((LANES - n_gate, d), F32)], axis=0)
        for c in range(d // WP_TILE):
            cs = slice(c * WP_TILE, (c + 1) * WP_TILE)
            o_ref[cs, 0:LANES] = gt[:, cs].T.astype(BF16)
        o_ref[:, LANES:] = jnp.zeros((d, WP_TILE - LANES), BF16)

    @pl.when(j > WP_GATE_TILE)
    def _():
        o_ref[...] = jnp.zeros_like(o_ref)


def _wprep(w_t, layer):
    _, n, d = w_t.shape
    n_big = COL_GATES // WP_TILE

    def src_row(j):
        return jnp.where(j < COL_NQ // WP_TILE, j * WP_TILE,
                         jnp.where(j < COL_NZ // WP_TILE, SRC_NQ + (j - COL_NQ // WP_TILE) * WP_TILE,
                                   jnp.where(j < n_big, SRC_NZ + (j - COL_NZ // WP_TILE) * WP_TILE, 0)))

    el = pl.Element
    return pl.pallas_call(
        _wprep_kernel,
        grid=(NP_PAD // WP_TILE,),
        in_specs=[pl.BlockSpec((el(1), el(WP_TILE), el(d)), lambda j: (layer, pl.multiple_of(src_row(j), 8), 0)),
                  pl.BlockSpec((el(1), el(SRC_NQ - SRC_MI), el(d)), lambda j: (layer, SRC_MI, 0)),
                  pl.BlockSpec((el(1), el(SRC_NZ - SRC_NG), el(d)), lambda j: (layer, SRC_NG, 0))],
        out_specs=pl.BlockSpec((d, WP_TILE), lambda j: (0, j)),
        out_shape=jax.ShapeDtypeStruct((d, NP_PAD), BF16),
        compiler_params=_cparams(("parallel",)),
        name="inproj_weight_prep",
    )(w_t, w_t, w_t)


def _overlap_t(t):
    n_cmp_rows = t // CMP_STRIDE
    n_slc = t // SLC_LEN
    c0 = np.arange(n_cmp_rows) * CMP_STRIDE
    s0 = np.arange(n_slc) * SLC_LEN
    ov = (c0[None, :] <= s0[:, None] + SLC_LEN - 1) & (c0[None, :] + CMP_LEN - 1 >= s0[:, None])
    ov[:, (t - CMP_LEN) // CMP_STRIDE + 1:] = False
    return jnp.asarray(np.concatenate([ov] * 3, axis=1), BF16)


def kernel(x, c, ln_g, w_ada, b_ada, w_in, b_in, m_conv_w, m_conv_b, m_wq, m_wk, m_norm_w, m_skip, m_f_bias,
           n_pos_k, n_pos_v, n_w1_k, n_w2_k, n_w1_v, n_w2_v, w_out, final_g):
    out_dtype = x.dtype
    bsz, t, d = x.shape
    depth = ln_g.shape[0]
    h_res = x.astype(F32)
    c8 = jnp.zeros((8, d), F32).at[:bsz].set(c.astype(F32))
    slopes_np = np.array([2.0 ** (-8.0 * (h + 1) / N_HEADS) for h in range(N_HEADS)], np.float32)
    rest = (slopes_np.astype(np.float64) * LOG2E).astype(np.float32)
    pieces = []
    for _ in range(3):
        pieces.append(rest.astype(jnp.bfloat16).astype(np.float32))
        rest = rest - pieces[-1]
    spieces = jnp.asarray(np.stack(pieces, axis=1).reshape(-1))
    ovt = _overlap_t(t)

    def w1cat(w1):
        w = w1.reshape(2, CMP_STRIDE, N_HD, CMP_HIDDEN)
        return jnp.concatenate([w[0], w[1]], axis=-1).astype(BF16)

    def w2pad(w2):
        return jnp.pad(w2, ((0, 0), (0, LANES - N_HD))).astype(BF16)

    for l in range(depth):
        mod = _ada(c8, w_ada, b_ada[l][None, :], l)[:bsz]
        shift, scale, gate = mod[:, None, 0:d], mod[:, None, d:2 * d], mod[:, None, 2 * d:3 * d]
        p = _inproj(h_res, ln_g[l][None, :], scale, shift,
                    _wprep(jnp.swapaxes(w_in, 1, 2), l), _reorder_cols(b_in[l])[None, :])
        fb_row = jnp.zeros((1, LANES), F32).at[0, M_HEADS:2 * M_HEADS].set(m_f_bias[l])
        y_m = _mlstm(p, m_conv_w[l], m_conv_b[l][None, :], m_wq[l].astype(BF16), m_wk[l].astype(BF16),
                     m_norm_w[l][None, :], m_skip[l][None, :], fb_row)
        kcmp, vcmpt = _compress(p, n_pos_k[l].reshape(1, -1), n_pos_v[l].reshape(1, -1),
                                n_w1_k[l].astype(BF16), w1cat(n_w1_k[l]), w2pad(n_w2_k[l]),
                                n_w1_v[l].astype(BF16), w1cat(n_w1_v[l]), w2pad(n_w2_v[l]))
        ks, vs, kw, vw = _relayout(p)
        y_n = _nsa(p, spieces, kcmp, vcmpt, ovt, ks, vs, kw, vw)
        h_res = _outproj(y_m, y_n, w_out[l].astype(BF16), h_res, gate, final_g[None, :], l == depth - 1)
    return h_res.astype(out_dtype)
```

```python
import functools

import numpy as np
import jax
import jax.numpy as jnp
from jax import lax
from jax.experimental import pallas as pl
from jax.experimental.pallas import tpu as pltpu

F32 = jnp.float32
BF16 = jnp.bfloat16

EPS = 1e-6
M_HEADS = 4
M_HD = 256
M_WIDTH = M_HEADS * M_HD
CONV_K = 4
M_CHUNK = 256
N_HEADS = 16
N_HD = 64
N_KV = 4
N_HPG = N_HEADS // N_KV
N_WIDTH = N_HEADS * N_HD
KV_W = N_KV * N_HD
CMP_LEN = 32
CMP_STRIDE = 16
CMP_HIDDEN = 2 * N_HD
SLC_LEN = 64
SLC_TOPN = 16
WIN = 512
QBLK = 256
SKT = 256
WKT = 256
PAD_TILES = WIN // SKT
SEL_GRP = 8
KA_SEL = 128
KA_WIN = 128
VROWS = 80
UNROLL = 8
AUX_ROWS = 16
LOG2E = 1.4426950408889634

COL_MX, COL_MV, COL_MO, COL_MZ = 0, 1024, 2048, 3072
COL_NQ = 4096
COL_KC, COL_VC, COL_KS, COL_VS, COL_KW, COL_VW = 5120, 5376, 5632, 5888, 6144, 6400
COL_NZ = 6656
COL_GATES = 7680
GATE_NG = 2 * M_HEADS
NP_PAD = 8192
NORM_CHUNKS = 4
LANES = 128
NEG = -1e30
VMEM_LIMIT = 56 * 1024 * 1024


def _cparams(sem):
    return pltpu.CompilerParams(dimension_semantics=sem, vmem_limit_bytes=VMEM_LIMIT)


def _silu(x):
    return x * jax.nn.sigmoid(x)


def _log_sigmoid(x):
    return jnp.minimum(x, 0.0) - jnp.log1p(jnp.exp(-jnp.abs(x)))


def _ada_kernel(ct_ref, w_ref, b_ref, o_ref, *, bsz):
    s_t = _silu(ct_ref[...])
    w = w_ref[0]
    row = lax.broadcasted_iota(jnp.int32, o_ref.shape, 0)
    out = jnp.zeros(o_ref.shape, F32)
    for b in range(bsz):
        prod = w * s_t[:, b:b + 1]
        acc = prod[0:8]
        for r in range(1, prod.shape[0] // 8):
            acc = acc + prod[8 * r:8 * r + 8]
        out = jnp.where(row == b, jnp.sum(acc, axis=0, keepdims=True) + b_ref[...], out)
    o_ref[...] = out


def _ada(c_t, w, b, layer, bsz):
    _, d, n = w.shape
    tn = 1024
    return pl.pallas_call(
        functools.partial(_ada_kernel, bsz=bsz),
        grid=(n // tn,),
        in_specs=[pl.BlockSpec((d, 8), lambda j: (0, 0)),
                  pl.BlockSpec((1, d, tn), lambda j: (layer, 0, j)),
                  pl.BlockSpec((1, tn), lambda j: (0, j))],
        out_specs=pl.BlockSpec((8, tn), lambda j: (0, j)),
        out_shape=jax.ShapeDtypeStruct((8, n), F32),
        compiler_params=_cparams(("parallel",)),
        name="ada_mod",
    )(c_t, w, b)


def _inproj_kernel(x_ref, g_ref, sc_ref, sh_ref, w_ref, b_ref, o_ref, h_ref):
    first = pl.program_id(2) == 0

    @pl.when(first)
    def _():
        tm = x_ref.shape[1]
        ck = tm // NORM_CHUNKS
        for c in range(NORM_CHUNKS):
            rows = slice(c * ck, (c + 1) * ck)
            x = x_ref[0, rows, :]
            ms = jnp.mean(x * x, axis=-1, keepdims=True)
            h = x * lax.rsqrt(ms + EPS) * g_ref[...]
            h = (h * (1.0 + sc_ref[0]) + sh_ref[0]).astype(BF16)
            h_ref[rows, :] = h
            o_ref[0, rows, :] = jnp.dot(h, w_ref[...], preferred_element_type=F32) + b_ref[...]

    @pl.when(jnp.logical_not(first))
    def _():
        o_ref[0] = jnp.dot(h_ref[...], w_ref[...], preferred_element_type=F32) + b_ref[...]


def _inproj(x, g, scale, shift, w, b):
    bsz, t, d = x.shape
    n = w.shape[1]
    tm, tn = 1024, 1024
    return pl.pallas_call(
        _inproj_kernel,
        grid=(bsz, t // tm, n // tn),
        in_specs=[pl.BlockSpec((1, tm, d), lambda bi, i, j: (bi, i, 0)),
                  pl.BlockSpec((1, d), lambda bi, i, j: (0, 0)),
                  pl.BlockSpec((1, 1, d), lambda bi, i, j: (bi, 0, 0)),
                  pl.BlockSpec((1, 1, d), lambda bi, i, j: (bi, 0, 0)),
                  pl.BlockSpec((d, tn), lambda bi, i, j: (0, j)),
                  pl.BlockSpec((1, tn), lambda bi, i, j: (0, j))],
        out_specs=pl.BlockSpec((1, tm, tn), lambda bi, i, j: (bi, i, j)),
        out_shape=jax.ShapeDtypeStruct((bsz, t, n), F32),
        scratch_shapes=[pltpu.VMEM((tm, d), BF16)],
        compiler_params=_cparams(("parallel", "parallel", "arbitrary")),
        name="norm_inproj",
    )(x, g, scale, shift, w, b)


def _mlstm_kernel(x_ref, v_ref, o_ref, z_ref, gt_ref, cw_ref, cb_ref, wq_ref, wk_ref, nw_ref, sk_ref, fb_ref,
                  y_ref, c_scr, n_scr, m_scr, xp_scr):
    L = M_CHUNK

    @pl.when(pl.program_id(1) == 0)
    def _():
        c_scr[...] = jnp.zeros_like(c_scr)
        n_scr[...] = jnp.zeros_like(n_scr)
        m_scr[...] = jnp.zeros_like(m_scr)
        xp_scr[...] = jnp.zeros_like(xp_scr)

    x = x_ref[0]
    prev = xp_scr[...]
    row8 = lax.broadcasted_iota(jnp.int32, (8, M_WIDTH), 0)
    cw = cw_ref[...]
    xc = cb_ref[...] + x * cw[CONV_K - 1:CONV_K, :]
    for sft in range(1, CONV_K):
        xr = pltpu.roll(x, sft, 0)
        top = jnp.where(row8 < sft, pltpu.roll(prev, sft, 0), xr[0:8])
        xs = jnp.concatenate([top, xr[8:]], axis=0)
        xc = xc + xs * cw[CONV_K - 1 - sft:CONV_K - sft, :]
    xp_scr[...] = x[L - 8:L]
    xc = _silu(xc)

    gt = gt_ref[0]
    col = lax.broadcasted_iota(jnp.int32, (L, LANES), 1)
    logf = _log_sigmoid(gt + fb_ref[...])
    a_c = jnp.where((col >= M_HEADS) & (col < 2 * M_HEADS), logf, gt)
    ri = lax.broadcasted_iota(jnp.int32, (L, L), 0)
    ci = lax.broadcasted_iota(jnp.int32, (L, L), 1)
    causal = ri >= ci
    hp = lax.Precision.HIGHEST
    tri = causal.astype(F32)
    tri_t = (ri <= ci).astype(F32)
    b_c = jnp.dot(tri, a_c, precision=hp, preferred_element_type=F32)
    a_r = a_c.T
    b_r = jnp.dot(a_r[0:8], tri_t, precision=hp, preferred_element_type=F32)

    for h in range(M_HEADS):
        sl = slice(h * M_HD, (h + 1) * M_HD)
        xh = xc[:, sl]
        xb = xh.astype(BF16)
        q = jnp.dot(xb, wq_ref[h], preferred_element_type=F32)
        k = jnp.dot(xb, wk_ref[h], preferred_element_type=F32) * (M_HD ** -0.5)
        vb = v_ref[0, :, sl].astype(BF16)
        qb = q.astype(BF16)
        kb = k.astype(BF16)

        bt = b_c[:, M_HEADS + h:M_HEADS + h + 1]
        ic = a_c[:, h:h + 1]
        bs = b_r[M_HEADS + h:M_HEADS + h + 1, :]
        ir = a_r[h:h + 1, :]
        m_prev = m_scr[h][:, 0:1]

        dm = jnp.where(causal, bt - bs + ir, -jnp.inf)
        inter = bt + m_prev
        m_t = jnp.maximum(inter, jnp.max(dm, axis=-1, keepdims=True))
        w_in = jnp.exp(dm - m_t)
        w_st = jnp.exp(inter - m_t)
        s = lax.dot_general(qb, kb, (((1,), (1,)), ((), ())), preferred_element_type=F32) * w_in
        cmat = c_scr[h]
        nvec = n_scr[h]
        sb = s.astype(BF16)
        num = w_st * jnp.dot(qb, cmat.astype(BF16), preferred_element_type=F32) \
            + jnp.dot(sb, vb, preferred_element_type=F32)
        nt_dims = (((1,), (1,)), ((), ()))
        qn = lax.dot_general(qb, jnp.broadcast_to(nvec, (8, M_HD)).astype(BF16), nt_dims,
                             preferred_element_type=F32)[:, 0:1]
        ssum = lax.dot_general(sb, jnp.ones((8, L), BF16), nt_dims, preferred_element_type=F32)[:, 0:1]
        den = w_st * qn + ssum
        hh = num / jnp.maximum(jnp.abs(den), jnp.exp(-m_t))

        b_last = bt[L - 1:L, :]
        w_end = b_last - bt + ic
        m_new = jnp.maximum(b_last + m_prev, jnp.max(w_end, axis=0, keepdims=True))
        decay = jnp.exp(b_last + m_prev - m_new)
        kwt = k * jnp.exp(w_end - m_new)
        c_scr[h] = decay * cmat + lax.dot_general(kwt.astype(BF16), vb, (((0,), (0,)), ((), ())),
                                                  preferred_element_type=F32)
        n_scr[h] = decay * nvec + jnp.sum(kwt, axis=0, keepdims=True)
        m_scr[h] = jnp.broadcast_to(m_new, (1, LANES))

        mu = jnp.mean(hh, axis=-1, keepdims=True)
        hc = hh - mu
        var = jnp.mean(hc * hc, axis=-1, keepdims=True)
        hn = hc * lax.rsqrt(var + EPS) * nw_ref[:, sl]
        out = jax.nn.sigmoid(o_ref[0, :, sl]) * hn + sk_ref[:, sl] * xh
        y_ref[0, :, sl] = (out * _silu(z_ref[0, :, sl])).astype(BF16)


def _mlstm(p, conv_w, conv_b, wq, wk, norm_w, skip, fb_row):
    bsz, t, _ = p.shape
    L = M_CHUNK
    cb = lambda c: (lambda bi, i: (bi, i, c))
    full2 = lambda bi, i: (0, 0)
    full3 = lambda bi, i: (0, 0, 0)
    return pl.pallas_call(
        _mlstm_kernel,
        grid=(bsz, t // L),
        in_specs=[pl.BlockSpec((1, L, M_WIDTH), cb(COL_MX // M_WIDTH)),
                  pl.BlockSpec((1, L, M_WIDTH), cb(COL_MV // M_WIDTH)),
                  pl.BlockSpec((1, L, M_WIDTH), cb(COL_MO // M_WIDTH)),
                  pl.BlockSpec((1, L, M_WIDTH), cb(COL_MZ // M_WIDTH)),
                  pl.BlockSpec((1, L, LANES), cb(COL_GATES // LANES)),
                  pl.BlockSpec((CONV_K, M_WIDTH), full2),
                  pl.BlockSpec((1, M_WIDTH), full2),
                  pl.BlockSpec((M_HEADS, M_HD, M_HD), full3),
                  pl.BlockSpec((M_HEADS, M_HD, M_HD), full3),
                  pl.BlockSpec((1, M_WIDTH), full2),
                  pl.BlockSpec((1, M_WIDTH), full2),
                  pl.BlockSpec((1, LANES), full2)],
        out_specs=pl.BlockSpec((1, L, M_WIDTH), lambda bi, i: (bi, i, 0)),
        out_shape=jax.ShapeDtypeStruct((bsz, t, M_WIDTH), BF16),
        scratch_shapes=[pltpu.VMEM((M_HEADS, M_HD, M_HD), F32),
                        pltpu.VMEM((M_HEADS, 1, M_HD), F32),
                        pltpu.VMEM((M_HEADS, 1, LANES), F32),
                        pltpu.VMEM((8, M_WIDTH), F32)],
        compiler_params=_cparams(("parallel", "arbitrary")),
        name="mlstm_group",
    )(p, p, p, p, p, conv_w, conv_b, wq, wk, norm_w, skip, fb_row)


def _compress_kernel(kc0_ref, kc1_ref, vc0_ref, vc1_ref, posk_ref, posv_ref, w1k_ref, w1kc_ref, w2k_ref,
                     w1v_ref, w1vc_ref, w2v_ref, kcmp_ref, vcmpt_ref):
    nb = kcmp_ref.shape[2]

    def hidden(src_refs, pos_ref, w1_ref, w1c_ref):
        acc = jnp.zeros((N_KV * nb, 2 * CMP_HIDDEN), F32)
        for l in range(CMP_STRIDE):
            xl = [r[0, pl.ds(l, nb, stride=CMP_STRIDE), :] for r in src_refs]
            xs = jnp.concatenate([x[:, g * N_HD:(g + 1) * N_HD] for x in xl for g in range(LANES // N_HD)], axis=0)
            acc = acc + jnp.dot(xs.astype(BF16), w1c_ref[l], preferred_element_type=F32)
        first = acc[:, :CMP_HIDDEN]
        second = acc[:, CMP_HIDDEN:]
        posb = jnp.dot(jnp.broadcast_to(pos_ref[...], (8, CMP_LEN * N_HD)).astype(BF16), w1_ref[...],
                       preferred_element_type=F32)[0:1]
        hid = first + pltpu.roll(second, N_KV * nb - 1, 0) + posb
        return jax.nn.gelu(hid).astype(BF16)

    hk = hidden((kc0_ref, kc1_ref), posk_ref, w1k_ref, w1kc_ref)
    kc = jnp.dot(hk, w2k_ref[...], preferred_element_type=F32)
    hv = hidden((vc0_ref, vc1_ref), posv_ref, w1v_ref, w1vc_ref)
    vc = jnp.dot(hv, w2v_ref[...], preferred_element_type=F32)
    j = lax.broadcasted_iota(jnp.int32, (nb, 1), 0)
    lane = lax.broadcasted_iota(jnp.int32, (nb, N_HD), 1)
    n_real = (nb * CMP_STRIDE - CMP_LEN) // CMP_STRIDE + 1
    aux = jnp.where(lane < 3, (j * CMP_STRIDE).astype(F32),
                    jnp.where(lane < 6, (CMP_LEN - 1) * 0.5,
                              jnp.where((lane == 6) & (j >= n_real), 1.0, 0.0))).astype(BF16)
    ones_blk = jnp.where(lax.broadcasted_iota(jnp.int32, (VROWS - N_HD, nb), 0) == 0, 1.0, 0.0)
    for g in range(N_KV):
        kg = kc[g * nb:(g + 1) * nb, 0:N_HD]
        k_hi = kg.astype(BF16)
        k_lo = (kg - k_hi.astype(F32)).astype(BF16)
        kcmp_ref[0, g] = jnp.concatenate([k_hi, k_lo, k_hi, aux], axis=1)
        vt = vc[g * nb:(g + 1) * nb].T
        vcmpt_ref[0, g] = jnp.concatenate([vt[0:N_HD], ones_blk], axis=0).astype(BF16)


def _compress(p, posk, posv, w1k, w1kc, w2k, w1v, w1vc, w2v):
    bsz, t, _ = p.shape
    nb = t // CMP_STRIDE
    cb = lambda c: (lambda bi: (bi, 0, c))
    f2 = lambda bi: (0, 0)
    f3 = lambda bi: (0, 0, 0)
    wspecs = [pl.BlockSpec((CMP_LEN * N_HD, CMP_HIDDEN), f2),
              pl.BlockSpec((CMP_STRIDE, N_HD, 2 * CMP_HIDDEN), f3),
              pl.BlockSpec((CMP_HIDDEN, LANES), f2)]
    return pl.pallas_call(
        _compress_kernel,
        grid=(bsz,),
        in_specs=[pl.BlockSpec((1, t, LANES), cb(COL_KC // LANES)),
                  pl.BlockSpec((1, t, LANES), cb(COL_KC // LANES + 1)),
                  pl.BlockSpec((1, t, LANES), cb(COL_VC // LANES)),
                  pl.BlockSpec((1, t, LANES), cb(COL_VC // LANES + 1)),
                  pl.BlockSpec((1, CMP_LEN * N_HD), f2),
                  pl.BlockSpec((1, CMP_LEN * N_HD), f2)] + wspecs + wspecs,
        out_specs=[pl.BlockSpec((1, N_KV, nb, 4 * N_HD), lambda bi: (bi, 0, 0, 0)),
                   pl.BlockSpec((1, N_KV, VROWS, nb), lambda bi: (bi, 0, 0, 0))],
        out_shape=[jax.ShapeDtypeStruct((bsz, N_KV, nb, 4 * N_HD), BF16),
                   jax.ShapeDtypeStruct((bsz, N_KV, VROWS, nb), BF16)],
        compiler_params=_cparams(("parallel",)),
        name="nsa_compress",
    )(p, p, p, p, posk, posv, w1k, w1kc, w2k, w1v, w1vc, w2v)


def _relayout_kernel(ks_ref, vs_ref, kw_ref, vw_ref, ksa_ref, vsa_ref, kwa_ref, vwa_ref):
    i = pl.program_id(1)
    is_pad = i < PAD_TILES
    base = (i - PAD_TILES) * SKT
    row = lax.broadcasted_iota(jnp.int32, (SKT, 1), 0)
    pos = jnp.where(is_pad, 0, base + row)
    blk = pos // SLC_LEN
    p_hi = (blk * SLC_LEN).astype(F32)
    p_lo = (pos - blk * SLC_LEN).astype(F32)
    flag = jnp.where(is_pad, 1.0, 0.0)
    lane = lax.broadcasted_iota(jnp.int32, (SKT, N_HD), 1)
    al = lane - AUX_ROWS
    mid_w = jnp.where((al >= 0) & (al < 3), p_hi,
                      jnp.where((al >= 3) & (al < 6), p_lo, jnp.where(al == 6, flag, 0.0)))
    mid_s = jnp.where(lane == blk % SEL_GRP, 1.0, mid_w)
    ks = ks_ref[0]
    kw = kw_ref[0]
    for g in range(N_KV):
        sl = slice(g * N_HD, (g + 1) * N_HD)
        ksa_ref[0, g] = jnp.concatenate([ks[:, sl], mid_s], axis=1).astype(BF16)
        kwa_ref[0, g] = jnp.concatenate([kw[:, sl], mid_w], axis=1).astype(BF16)
    ones_blk = jnp.where(lax.broadcasted_iota(jnp.int32, (VROWS - N_HD, SKT), 0) == 0, 1.0, 0.0)
    vst = vs_ref[0].T
    vwt = vw_ref[0].T
    for g in range(N_KV):
        sl = slice(g * N_HD, (g + 1) * N_HD)
        vsa_ref[0, g, 0] = jnp.concatenate([vst[sl], ones_blk], axis=0).astype(BF16)
        vwg = jnp.concatenate([vwt[sl], ones_blk], axis=0).astype(BF16)
        for j in range(SKT // WKT):
            vwa_ref[0, g, j] = vwg[:, j * WKT:(j + 1) * WKT]


def _relayout(p):
    bsz, t, _ = p.shape
    nt = t // SKT + PAD_TILES
    wpt = SKT // WKT
    cb = lambda c: (lambda bi, i: (bi, jnp.maximum(i - PAD_TILES, 0), c))
    return pl.pallas_call(
        _relayout_kernel,
        grid=(bsz, nt),
        in_specs=[pl.BlockSpec((1, SKT, KV_W), cb(COL_KS // KV_W)),
                  pl.BlockSpec((1, SKT, KV_W), cb(COL_VS // KV_W)),
                  pl.BlockSpec((1, SKT, KV_W), cb(COL_KW // KV_W)),
                  pl.BlockSpec((1, SKT, KV_W), cb(COL_VW // KV_W))],
        out_specs=[pl.BlockSpec((1, N_KV, SKT, KA_SEL), lambda bi, i: (bi, 0, i, 0)),
                   pl.BlockSpec((1, N_KV, 1, VROWS, SKT), lambda bi, i: (bi, 0, i, 0, 0)),
                   pl.BlockSpec((1, N_KV, SKT, KA_WIN), lambda bi, i: (bi, 0, i, 0)),
                   pl.BlockSpec((1, N_KV, wpt, VROWS, WKT), lambda bi, i: (bi, 0, i, 0, 0))],
        out_shape=[jax.ShapeDtypeStruct((bsz, N_KV, nt * SKT, KA_SEL), BF16),
                   jax.ShapeDtypeStruct((bsz, N_KV, nt, VROWS, SKT), BF16),
                   jax.ShapeDtypeStruct((bsz, N_KV, nt * SKT, KA_WIN), BF16),
                   jax.ShapeDtypeStruct((bsz, N_KV, nt * wpt, VROWS, WKT), BF16)],
        compiler_params=_cparams(("parallel", "parallel")),
        name="nsa_relayout",
    )(p, p, p, p)


def _nsa_kernel(sp_ref, q_ref, gt_ref, z_ref, kcmp_ref, vcmpt_ref, ovt_ref, ks_ref, vs_ref, kw_ref,
                vw_ref, tri_ref, wb_ref, y_ref, s_scr, g_scr, imp_scr, rank_scr, part_scr, selb_scr):
    g = pl.program_id(1)
    qb = pl.program_id(2)
    start = qb * QBLK
    nb = kcmp_ref.shape[2]
    n_slc = ovt_ref.shape[0]

    qt = (q_ref[0] * (N_HD ** -0.5)).T
    q4t = jnp.concatenate([qt[h * N_HD:(h + 1) * N_HD] for h in range(N_HPG)], axis=1)
    t_row = start + lax.broadcasted_iota(jnp.int32, (1, QBLK), 1)

    g_scr[...] = jax.nn.sigmoid(gt_ref[0]).T

    nw = N_HPG * QBLK
    hl = lax.broadcasted_iota(jnp.int32, (AUX_ROWS, nw), 1) // QBLK
    ar = lax.broadcasted_iota(jnp.int32, (AUX_ROWS, nw), 0)

    def per_head(vals):
        out = jnp.full((AUX_ROWS, nw), vals[N_HPG - 1], F32)
        for h in range(N_HPG - 2, -1, -1):
            out = jnp.where(hl == h, vals[h], out)
        return out

    pieces = [per_head([sp_ref[(g * N_HPG + h) * 3 + j] for h in range(N_HPG)]) for j in range(3)]
    aux = jnp.where(ar == 6, NEG, 0.0)
    for j in range(3):
        aux = jnp.where((ar == j) | (ar == j + 3), pieces[j], aux)
    aux_b = aux.astype(BF16)
    q4l = q4t * LOG2E
    q4s = q4l.astype(BF16)
    q4lo = (q4l - q4s.astype(F32)).astype(BF16)

    def tile4(a):
        return jnp.concatenate([a] * N_HPG, axis=1)

    def col_reduce8(s, op):
        out = s[0:8]
        for r in range(1, s.shape[0] // 8):
            out = op(out, s[8 * r:8 * r + 8])
        return out

    def col_max(s):
        return jnp.max(col_reduce8(s, jnp.maximum), axis=0, keepdims=True)

    def normalise(acc):
        den = acc[N_HD:N_HD + 1]
        return acc[0:N_HD] / jnp.where(den > 0, den, 1.0)

    q_cmp = jnp.concatenate([q4s, q4s, q4lo, aux_b, jnp.zeros((N_HD - AUX_ROWS, nw), BF16)], axis=0)
    cmp_end = lax.broadcasted_iota(jnp.int32, (nb, 1), 0) * CMP_STRIDE + (CMP_LEN - 1)
    s_c = jnp.dot(kcmp_ref[0, 0], q_cmp, preferred_element_type=F32)
    s_c = s_c + tile4(jnp.where(cmp_end <= t_row, 0.0, NEG))
    e_c = jnp.exp2(s_c - col_max(s_c))
    z_c = jnp.sum(col_reduce8(e_c, jnp.add), axis=0, keepdims=True)
    inv_c = jnp.where(tile4(t_row >= CMP_LEN - 1) & (z_c > 0), 1.0 / z_c, 0.0)
    o_c = jnp.dot(vcmpt_ref[0, 0], e_c.astype(BF16), preferred_element_type=F32)[0:N_HD] * inv_c
    p_c = e_c * inv_c
    psum = p_c[:, 0:QBLK]
    for h in range(1, N_HPG):
        psum = psum + p_c[:, h * QBLK:(h + 1) * QBLK]
    parts, rest = [], psum
    for _ in range(3):
        parts.append(rest.astype(BF16))
        rest = rest - parts[-1].astype(F32)
    imp = jnp.dot(ovt_ref[...], jnp.concatenate(parts, axis=0), preferred_element_type=F32)
    sidx = lax.broadcasted_iota(jnp.int32, (n_slc, QBLK), 0)
    cur = t_row // SLC_LEN
    valid = sidx <= cur
    imp = jnp.where(valid, imp, -jnp.inf)
    imp_scr[...] = jnp.where((sidx == 0) | (sidx == cur), jnp.inf, imp)

    wrows = WIN + QBLK
    q_win = jnp.concatenate([q4s, jnp.zeros((AUX_ROWS, nw), BF16), aux_b,
                             jnp.zeros((KA_WIN - N_HD - 2 * AUX_ROWS, nw), BF16)], axis=0)
    s_w = jnp.dot(kw_ref[0, 0, pl.ds(pl.multiple_of(start, QBLK), wrows), :], q_win,
                  preferred_element_type=F32)
    s_w = jnp.concatenate([s_w[0:WKT] + tile4(wb_ref[0]), s_w[WKT:wrows - WKT],
                           s_w[wrows - WKT:] + tile4(wb_ref[1])], axis=0)
    p_w = jnp.exp2(s_w - col_max(s_w)).astype(BF16)
    v_w = jnp.concatenate([vw_ref[0, 0, qb * (QBLK // WKT) + j] for j in range(wrows // WKT)], axis=1)
    o_w = normalise(jnp.dot(v_w, p_w, preferred_element_type=F32))

    def gate_row(h, branch):
        return g_scr[pl.ds(GATE_NG + (g * N_HPG + h) * 3 + branch, 1), :]

    for h in range(N_HPG):
        hs = slice(h * QBLK, (h + 1) * QBLK)
        part_scr[:, hs] = gate_row(h, 0) * o_c[:, hs] + gate_row(h, 2) * o_w[:, hs]

    n_top = min(SLC_TOPN, n_slc)
    rank_scr[...] = jnp.zeros_like(rank_scr)
    sub8 = lax.broadcasted_iota(jnp.int32, (8, QBLK), 0)
    last_blk = (start + QBLK - 1) // SLC_LEN
    for ri in range(n_slc // 8):
        @pl.when((last_blk >= n_top) & (8 * ri <= last_blk))
        def _():
            rows = imp_scr[8 * ri:8 * ri + 8]
            for r in range(n_slc // 8):
                blk8 = imp_scr[8 * r:8 * r + 8]
                acc = rank_scr[8 * r:8 * r + 8]
                for ii in range(8):
                    row = rows[ii:ii + 1]
                    if ri < r:
                        before = row >= blk8
                    elif ri > r:
                        before = row > blk8
                    else:
                        before = (row > blk8) | ((row == blk8) & (sub8 > ii))
                    acc = acc + jnp.where(before, 1.0, 0.0)
                rank_scr[8 * r:8 * r + 8] = acc
    selbias = jnp.where((rank_scr[...] < n_top) & valid, 0.0, NEG)

    selb_scr[...] = tile4(selbias)
    n_tiles = qb // (SKT // QBLK) + 1
    q_tail = jnp.zeros((KA_SEL - N_HD - 2 * AUX_ROWS, nw), BF16)

    def sel_scores(kt):
        grp = pl.multiple_of((kt * SKT // SLC_LEN) // SEL_GRP * SEL_GRP, SEL_GRP)
        sel_rows = jnp.concatenate([selb_scr[pl.ds(grp, SEL_GRP), :], jnp.zeros((AUX_ROWS - SEL_GRP, nw), F32)], axis=0)
        q_sel = jnp.concatenate([q4s, sel_rows.astype(BF16), aux_b, q_tail], axis=0)
        krow = pl.multiple_of((kt + PAD_TILES) * SKT, SKT)
        return jnp.dot(ks_ref[0, 0, pl.ds(krow, SKT), :], q_sel, preferred_element_type=F32)

    def pass1(kt, mrun):
        s = sel_scores(kt)
        s_scr[pl.ds(pl.multiple_of(kt * SKT, SKT), SKT), :] = s
        return jnp.maximum(mrun, col_reduce8(s, jnp.maximum))

    def run_tiles(fn, n, init):
        def group(size, first):
            def body(i, carry):
                for u in range(size):
                    carry = fn(first + i * size + u, carry)
                return carry
            return body

        n_u = n // UNROLL
        carry = lax.fori_loop(0, n_u, group(UNROLL, 0), init)
        done = n_u * UNROLL
        size = UNROLL // 2
        while size >= 1:
            take = (n // size) % 2
            carry = lax.fori_loop(0, take, group(size, done), carry)
            done = done + take * size
            size //= 2
        return carry

    mrun = run_tiles(pass1, n_tiles - 1, jnp.full((8, nw), NEG, F32))
    last = n_tiles - 1
    s_last = sel_scores(last) + tile4(tri_ref[qb % (SKT // QBLK)])
    s_scr[pl.ds(pl.multiple_of(last * SKT, SKT), SKT), :] = s_last
    m_sel = jnp.max(jnp.maximum(mrun, col_reduce8(s_last, jnp.maximum)), axis=0, keepdims=True)

    def pass2(kt, acc):
        s = s_scr[pl.ds(pl.multiple_of(kt * SKT, SKT), SKT), :]
        pr = jnp.exp2(s - m_sel).astype(BF16)
        return acc + jnp.dot(vs_ref[0, 0, kt + PAD_TILES], pr, preferred_element_type=F32)

    o_s = normalise(run_tiles(pass2, n_tiles, jnp.zeros((VROWS, nw), F32)))

    tot = [part_scr[:, h * QBLK:(h + 1) * QBLK] + gate_row(h, 1) * o_s[:, h * QBLK:(h + 1) * QBLK]
           for h in range(N_HPG)]
    o = jnp.concatenate(tot, axis=0).T
    y_ref[0] = (o * _silu(z_ref[0])).astype(BF16)


def _edge_biases():
    kl = np.arange(SKT)[:, None]
    ql = np.arange(QBLK)[None, :]
    tri = np.stack([np.where(kl <= par * QBLK + ql, 0.0, NEG) for par in range(SKT // QBLK)])
    kk = np.arange(WKT)[:, None]
    wb = np.stack([np.where(kk > ql, 0.0, NEG), np.where(kk <= ql + WKT - QBLK, 0.0, NEG)])
    return jnp.asarray(tri, F32), jnp.asarray(wb, F32)


def _nsa(p, spieces, kcmp, vcmpt, ovt, ks, vs, kw, vw):
    bsz, t, _ = p.shape
    n_slc = t // SLC_LEN
    gw = N_HPG * N_HD
    tri, wb = _edge_biases()
    per_bg = lambda a: pl.BlockSpec((1, 1) + a.shape[2:], lambda bi, g, i, sp: (bi, g) + (0,) * (a.ndim - 2))
    const = lambda a: pl.BlockSpec(a.shape, lambda bi, g, i, sp: (0,) * a.ndim)
    grid_spec = pltpu.PrefetchScalarGridSpec(
        num_scalar_prefetch=1,
        grid=(bsz, N_KV, t // QBLK),
        in_specs=[pl.BlockSpec((1, QBLK, gw), lambda bi, g, i, sp: (bi, i, COL_NQ // gw + g)),
                  pl.BlockSpec((1, QBLK, LANES), lambda bi, g, i, sp: (bi, i, COL_GATES // LANES)),
                  pl.BlockSpec((1, QBLK, gw), lambda bi, g, i, sp: (bi, i, COL_NZ // gw + g)),
                  per_bg(kcmp), per_bg(vcmpt), const(ovt),
                  per_bg(ks), per_bg(vs), per_bg(kw), per_bg(vw), const(tri), const(wb)],
        out_specs=pl.BlockSpec((1, QBLK, gw), lambda bi, g, i, sp: (bi, i, g)),
        scratch_shapes=[pltpu.VMEM((t, N_HPG * QBLK), F32),
                        pltpu.VMEM((LANES, QBLK), F32),
                        pltpu.VMEM((n_slc, QBLK), F32),
                        pltpu.VMEM((n_slc, QBLK), F32),
                        pltpu.VMEM((N_HD, N_HPG * QBLK), F32),
                        pltpu.VMEM((n_slc, N_HPG * QBLK), F32)],
    )
    return pl.pallas_call(
        _nsa_kernel,
        grid_spec=grid_spec,
        out_shape=jax.ShapeDtypeStruct((bsz, t, N_WIDTH), BF16),
        compiler_params=_cparams(("parallel", "parallel", "arbitrary")),
        name="nsa_attention",
    )(spieces, p, p, p, kcmp, vcmpt, ovt, ks, vs, kw, vw, tri, wb)


def _outproj_kernel(ym_ref, yn_ref, w_ref, x_ref, gate_ref, fg_ref, o_ref, wb_scr, *, final):
    @pl.when((pl.program_id(0) == 0) & (pl.program_id(1) == 0))
    def _():
        rows = w_ref.shape[1]
        for c in range(rows // WP_TILE):
            cs = slice(c * WP_TILE, (c + 1) * WP_TILE)
            wb_scr[cs, :] = w_ref[0, cs, :].astype(BF16)

    y = jnp.dot(ym_ref[0], wb_scr[0:M_WIDTH, :], preferred_element_type=F32)
    y = y + jnp.dot(yn_ref[0], wb_scr[M_WIDTH:, :], preferred_element_type=F32)
    hres = x_ref[0] + gate_ref[0] * y
    if final:
        ms = jnp.mean(hres * hres, axis=-1, keepdims=True)
        hres = hres * lax.rsqrt(ms + EPS) * fg_ref[...]
    o_ref[0] = hres


def _outproj(ym, yn, w, layer, x, gate, fg, final):
    bsz, t, d = x.shape
    tm = 512
    return pl.pallas_call(
        functools.partial(_outproj_kernel, final=final),
        grid=(bsz, t // tm),
        in_specs=[pl.BlockSpec((1, tm, M_WIDTH), lambda bi, i: (bi, i, 0)),
                  pl.BlockSpec((1, tm, N_WIDTH), lambda bi, i: (bi, i, 0)),
                  pl.BlockSpec((1, M_WIDTH + N_WIDTH, d), lambda bi, i: (layer, 0, 0),
                               pipeline_mode=pl.Buffered(1)),
                  pl.BlockSpec((1, tm, d), lambda bi, i: (bi, i, 0)),
                  pl.BlockSpec((1, 1, d), lambda bi, i: (bi, 0, 0)),
                  pl.BlockSpec((1, d), lambda bi, i: (0, 0))],
        out_specs=pl.BlockSpec((1, tm, d), lambda bi, i: (bi, i, 0)),
        out_shape=jax.ShapeDtypeStruct((bsz, t, d), F32),
        scratch_shapes=[pltpu.VMEM((M_WIDTH + N_WIDTH, d), BF16)],
        compiler_params=_cparams(("arbitrary", "arbitrary")),
        name="outproj_residual",
    )(ym, yn, w, x, gate, fg)


SRC_MI = 4 * M_WIDTH
SRC_NQ = SRC_MI + 2 * M_HEADS
SRC_NG = SRC_NQ + N_WIDTH + 6 * KV_W
SRC_NZ = SRC_NG + 3 * N_HEADS


def _reorder_cols(a):
    parts = [a[..., 0:SRC_MI], a[..., SRC_NQ:SRC_NG], a[..., SRC_NZ:SRC_NZ + N_WIDTH], a[..., SRC_MI:SRC_NQ],
             a[..., SRC_NG:SRC_NZ]]
    used = sum(x.shape[-1] for x in parts)
    parts.append(jnp.zeros(a.shape[:-1] + (NP_PAD - used,), a.dtype))
    return jnp.concatenate(parts, axis=-1)


WP_TILE = 256
WP_GATE_TILE = COL_GATES // WP_TILE


def _wprep_kernel(w_ref, g1_ref, g2_ref, o_ref):
    j = pl.program_id(0)
    d = o_ref.shape[0]

    @pl.when(j < WP_GATE_TILE)
    def _():
        for c in range(d // WP_TILE):
            cs = slice(c * WP_TILE, (c + 1) * WP_TILE)
            o_ref[cs, :] = w_ref[0, :, cs].T.astype(BF16)

    @pl.when(j == WP_GATE_TILE)
    def _():
        n_gate = g1_ref.shape[1] + g2_ref.shape[1]
        gt = jnp.concatenate([g1_ref[0], g2_ref[0], jnp.zeros((LANES - n_gate, d), F32)], axis=0)
        for c in range(d // WP_TILE):
            cs = slice(c * WP_TILE, (c + 1) * WP_TILE)
            o_ref[cs, 0:LANES] = gt[:, cs].T.astype(BF16)
        o_ref[:, LANES:] = jnp.zeros((d, WP_TILE - LANES), BF16)

    @pl.when(j > WP_GATE_TILE)
    def _():
        o_ref[...] = jnp.zeros_like(o_ref)


def _wprep(w_t, layer):
    _, n, d = w_t.shape
    n_big = COL_GATES // WP_TILE

    def src_row(j):
        return jnp.where(j < COL_NQ // WP_TILE, j * WP_TILE,
                         jnp.where(j < COL_NZ // WP_TILE, SRC_NQ + (j - COL_NQ // WP_TILE) * WP_TILE,
                                   jnp.where(j < n_big, SRC_NZ + (j - COL_NZ // WP_TILE) * WP_TILE, 0)))

    el = pl.Element
    return pl.pallas_call(
        _wprep_kernel,
        grid=(NP_PAD // WP_TILE,),
        in_specs=[pl.BlockSpec((el(1), el(WP_TILE), el(d)), lambda j: (layer, pl.multiple_of(src_row(j), 8), 0)),
                  pl.BlockSpec((el(1), el(SRC_NQ - SRC_MI), el(d)), lambda j: (layer, SRC_MI, 0)),
                  pl.BlockSpec((el(1), el(SRC_NZ - SRC_NG), el(d)), lambda j: (layer, SRC_NG, 0))],
        out_specs=pl.BlockSpec((d, WP_TILE), lambda j: (0, j)),
        out_shape=jax.ShapeDtypeStruct((d, NP_PAD), BF16),
        compiler_params=_cparams(("parallel",)),
        name="inproj_weight_prep",
    )(w_t, w_t, w_t)


def _overlap_t(t):
    n_cmp_rows = t // CMP_STRIDE
    n_slc = t // SLC_LEN
    c0 = np.arange(n_cmp_rows) * CMP_STRIDE
    s0 = np.arange(n_slc) * SLC_LEN
    ov = (c0[None, :] <= s0[:, None] + SLC_LEN - 1) & (c0[None, :] + CMP_LEN - 1 >= s0[:, None])
    ov[:, (t - CMP_LEN) // CMP_STRIDE + 1:] = False
    return jnp.asarray(np.concatenate([ov] * 3, axis=1), BF16)


def kernel(x, c, ln_g, w_ada, b_ada, w_in, b_in, m_conv_w, m_conv_b, m_wq, m_wk, m_norm_w, m_skip, m_f_bias,
           n_pos_k, n_pos_v, n_w1_k, n_w2_k, n_w1_v, n_w2_v, w_out, final_g):
    out_dtype = x.dtype
    bsz, t, d = x.shape
    depth = ln_g.shape[0]
    h_res = x.astype(F32)
    assert bsz <= 8
    c_t = jnp.zeros((d, 8), F32).at[:, :bsz].set(c.astype(F32).T)
    slopes_np = np.array([2.0 ** (-8.0 * (h + 1) / N_HEADS) for h in range(N_HEADS)], np.float32)
    rest = (slopes_np.astype(np.float64) * LOG2E).astype(np.float32)
    pieces = []
    for _ in range(3):
        pieces.append(rest.astype(jnp.bfloat16).astype(np.float32))
        rest = rest - pieces[-1]
    spieces = jnp.asarray(np.stack(pieces, axis=1).reshape(-1))
    ovt = _overlap_t(t)

    def w1cat(w1):
        w = w1.reshape(2, CMP_STRIDE, N_HD, CMP_HIDDEN)
        return jnp.concatenate([w[0], w[1]], axis=-1).astype(BF16)

    def w2pad(w2):
        return jnp.pad(w2, ((0, 0), (0, LANES - N_HD))).astype(BF16)

    for l in range(depth):
        mod = _ada(c_t, w_ada, b_ada[l][None, :], l, bsz)[:bsz]
        shift, scale, gate = mod[:, None, 0:d], mod[:, None, d:2 * d], mod[:, None, 2 * d:3 * d]
        p = _inproj(h_res, ln_g[l][None, :], scale, shift,
                    _wprep(jnp.swapaxes(w_in, 1, 2), l), _reorder_cols(b_in[l])[None, :])
        fb_row = jnp.zeros((1, LANES), F32).at[0, M_HEADS:2 * M_HEADS].set(m_f_bias[l])
        y_m = _mlstm(p, m_conv_w[l], m_conv_b[l][None, :], m_wq[l].astype(BF16), m_wk[l].astype(BF16),
                     m_norm_w[l][None, :], m_skip[l][None, :], fb_row)
        kcmp, vcmpt = _compress(p, n_pos_k[l].reshape(1, -1), n_pos_v[l].reshape(1, -1),
                                n_w1_k[l].astype(BF16), w1cat(n_w1_k[l]), w2pad(n_w2_k[l]),
                                n_w1_v[l].astype(BF16), w1cat(n_w1_v[l]), w2pad(n_w2_v[l]))
        ks, vs, kw, vw = _relayout(p)
        y_n = _nsa(p, spieces, kcmp, vcmpt, ovt, ks, vs, kw, vw)
        h_res = _outproj(y_m, y_n, w_out, l, h_res, gate, final_g[None, :], l == depth - 1)
    return h_res.astype(out_dtype)
```

```python
import functools

import numpy as np
import jax
import jax.numpy as jnp
from jax import lax
from jax.experimental import pallas as pl
from jax.experimental.pallas import tpu as pltpu

F32 = jnp.float32
BF16 = jnp.bfloat16

EPS = 1e-6
M_HEADS = 4
M_HD = 256
M_WIDTH = M_HEADS * M_HD
CONV_K = 4
M_CHUNK = 256
N_HEADS = 16
N_HD = 64
N_KV = 4
N_HPG = N_HEADS // N_KV
N_WIDTH = N_HEADS * N_HD
KV_W = N_KV * N_HD
CMP_LEN = 32
CMP_STRIDE = 16
CMP_HIDDEN = 2 * N_HD
SLC_LEN = 64
SLC_TOPN = 16
WIN = 512
QBLK = 256
SKT = 256
WKT = 256
PAD_TILES = WIN // SKT
SEL_GRP = 8
KA_SEL = 128
KA_WIN = 128
VROWS = 80
UNROLL = 8
AUX_ROWS = 16
LOG2E = 1.4426950408889634

COL_MX, COL_MV, COL_MO, COL_MZ = 0, 1024, 2048, 3072
COL_NQ = 4096
COL_KC, COL_VC, COL_KS, COL_VS, COL_KW, COL_VW = 5120, 5376, 5632, 5888, 6144, 6400
COL_NZ = 6656
COL_GATES = 7680
GATE_NG = 2 * M_HEADS
NP_PAD = 8192
NORM_CHUNKS = 4
LANES = 128
NEG = -1e30
VMEM_LIMIT = 56 * 1024 * 1024


def _cparams(sem):
    return pltpu.CompilerParams(dimension_semantics=sem, vmem_limit_bytes=VMEM_LIMIT)


def _silu(x):
    return x * jax.nn.sigmoid(x)


def _log_sigmoid(x):
    return jnp.minimum(x, 0.0) - jnp.log1p(jnp.exp(-jnp.abs(x)))


def _ada_kernel(ct_ref, w_ref, b_ref, o_ref, *, bsz):
    s_t = _silu(ct_ref[...])
    w = w_ref[0]
    row = lax.broadcasted_iota(jnp.int32, o_ref.shape, 0)
    out = jnp.zeros(o_ref.shape, F32)
    for b in range(bsz):
        prod = w * s_t[:, b:b + 1]
        acc = prod[0:8]
        for r in range(1, prod.shape[0] // 8):
            acc = acc + prod[8 * r:8 * r + 8]
        out = jnp.where(row == b, jnp.sum(acc, axis=0, keepdims=True) + b_ref[...], out)
    o_ref[...] = out


def _ada(c_t, w, b, layer, bsz):
    _, d, n = w.shape
    tn = 1024
    return pl.pallas_call(
        functools.partial(_ada_kernel, bsz=bsz),
        grid=(n // tn,),
        in_specs=[pl.BlockSpec((d, 8), lambda j: (0, 0)),
                  pl.BlockSpec((1, d, tn), lambda j: (layer, 0, j)),
                  pl.BlockSpec((1, tn), lambda j: (0, j))],
        out_specs=pl.BlockSpec((8, tn), lambda j: (0, j)),
        out_shape=jax.ShapeDtypeStruct((8, n), F32),
        compiler_params=_cparams(("parallel",)),
        name="ada_mod",
    )(c_t, w, b)


def _inproj_kernel(x_ref, g_ref, sc_ref, sh_ref, w_ref, b_ref, o_ref, h_ref):
    first = pl.program_id(2) == 0

    @pl.when(first)
    def _():
        tm = x_ref.shape[1]
        ck = tm // NORM_CHUNKS
        for c in range(NORM_CHUNKS):
            rows = slice(c * ck, (c + 1) * ck)
            x = x_ref[0, rows, :]
            ms = jnp.mean(x * x, axis=-1, keepdims=True)
            h = x * lax.rsqrt(ms + EPS) * g_ref[...]
            h = (h * (1.0 + sc_ref[0]) + sh_ref[0]).astype(BF16)
            h_ref[rows, :] = h
            o_ref[0, rows, :] = jnp.dot(h, w_ref[...], preferred_element_type=F32) + b_ref[...]

    @pl.when(jnp.logical_not(first))
    def _():
        o_ref[0] = jnp.dot(h_ref[...], w_ref[...], preferred_element_type=F32) + b_ref[...]


def _inproj(x, g, scale, shift, w, b):
    bsz, t, d = x.shape
    n = w.shape[1]
    tm, tn = 1024, 1024
    return pl.pallas_call(
        _inproj_kernel,
        grid=(bsz, t // tm, n // tn),
        in_specs=[pl.BlockSpec((1, tm, d), lambda bi, i, j: (bi, i, 0)),
                  pl.BlockSpec((1, d), lambda bi, i, j: (0, 0)),
                  pl.BlockSpec((1, 1, d), lambda bi, i, j: (bi, 0, 0)),
                  pl.BlockSpec((1, 1, d), lambda bi, i, j: (bi, 0, 0)),
                  pl.BlockSpec((d, tn), lambda bi, i, j: (0, j)),
                  pl.BlockSpec((1, tn), lambda bi, i, j: (0, j))],
        out_specs=pl.BlockSpec((1, tm, tn), lambda bi, i, j: (bi, i, j)),
        out_shape=jax.ShapeDtypeStruct((bsz, t, n), F32),
        scratch_shapes=[pltpu.VMEM((tm, d), BF16)],
        compiler_params=_cparams(("parallel", "parallel", "arbitrary")),
        name="norm_inproj",
    )(x, g, scale, shift, w, b)


def _mlstm_kernel(x_ref, v_ref, o_ref, z_ref, gt_ref, cw_ref, cb_ref, wq_ref, wk_ref, nw_ref, sk_ref, fb_ref,
                  y_ref, c_scr, n_scr, m_scr, xp_scr):
    L = M_CHUNK

    @pl.when(pl.program_id(1) == 0)
    def _():
        c_scr[...] = jnp.zeros_like(c_scr)
        n_scr[...] = jnp.zeros_like(n_scr)
        m_scr[...] = jnp.zeros_like(m_scr)
        xp_scr[...] = jnp.zeros_like(xp_scr)

    x = x_ref[0]
    prev = xp_scr[...]
    row8 = lax.broadcasted_iota(jnp.int32, (8, M_WIDTH), 0)
    cw = cw_ref[...]
    xc = cb_ref[...] + x * cw[CONV_K - 1:CONV_K, :]
    for sft in range(1, CONV_K):
        xr = pltpu.roll(x, sft, 0)
        top = jnp.where(row8 < sft, pltpu.roll(prev, sft, 0), xr[0:8])
        xs = jnp.concatenate([top, xr[8:]], axis=0)
        xc = xc + xs * cw[CONV_K - 1 - sft:CONV_K - sft, :]
    xp_scr[...] = x[L - 8:L]
    xc = _silu(xc)

    gt = gt_ref[0]
    col = lax.broadcasted_iota(jnp.int32, (L, LANES), 1)
    logf = _log_sigmoid(gt + fb_ref[...])
    a_c = jnp.where((col >= M_HEADS) & (col < 2 * M_HEADS), logf, gt)
    ri = lax.broadcasted_iota(jnp.int32, (L, L), 0)
    ci = lax.broadcasted_iota(jnp.int32, (L, L), 1)
    causal = ri >= ci
    hp = lax.Precision.HIGHEST
    tri = causal.astype(F32)
    tri_t = (ri <= ci).astype(F32)
    b_c = jnp.dot(tri, a_c, precision=hp, preferred_element_type=F32)
    a_r = a_c.T
    b_r = jnp.dot(a_r[0:8], tri_t, precision=hp, preferred_element_type=F32)

    for h in range(M_HEADS):
        sl = slice(h * M_HD, (h + 1) * M_HD)
        xh = xc[:, sl]
        xb = xh.astype(BF16)
        q = jnp.dot(xb, wq_ref[h], preferred_element_type=F32)
        k = jnp.dot(xb, wk_ref[h], preferred_element_type=F32) * (M_HD ** -0.5)
        vb = v_ref[0, :, sl].astype(BF16)
        qb = q.astype(BF16)
        kb = k.astype(BF16)

        bt = b_c[:, M_HEADS + h:M_HEADS + h + 1]
        ic = a_c[:, h:h + 1]
        bs = b_r[M_HEADS + h:M_HEADS + h + 1, :]
        ir = a_r[h:h + 1, :]
        m_prev = m_scr[h][:, 0:1]

        dm = jnp.where(causal, bt - bs + ir, -jnp.inf)
        inter = bt + m_prev
        m_t = jnp.maximum(inter, jnp.max(dm, axis=-1, keepdims=True))
        w_in = jnp.exp(dm - m_t)
        w_st = jnp.exp(inter - m_t)
        s = lax.dot_general(qb, kb, (((1,), (1,)), ((), ())), preferred_element_type=F32) * w_in
        cmat = c_scr[h]
        nvec = n_scr[h]
        sb = s.astype(BF16)
        num = w_st * jnp.dot(qb, cmat.astype(BF16), preferred_element_type=F32) \
            + jnp.dot(sb, vb, preferred_element_type=F32)
        nt_dims = (((1,), (1,)), ((), ()))
        qn = lax.dot_general(qb, jnp.broadcast_to(nvec, (8, M_HD)).astype(BF16), nt_dims,
                             preferred_element_type=F32)[:, 0:1]
        ssum = lax.dot_general(sb, jnp.ones((8, L), BF16), nt_dims, preferred_element_type=F32)[:, 0:1]
        den = w_st * qn + ssum
        hh = num / jnp.maximum(jnp.abs(den), jnp.exp(-m_t))

        b_last = bt[L - 1:L, :]
        w_end = b_last - bt + ic
        m_new = jnp.maximum(b_last + m_prev, jnp.max(w_end, axis=0, keepdims=True))
        decay = jnp.exp(b_last + m_prev - m_new)
        kwt = k * jnp.exp(w_end - m_new)
        c_scr[h] = decay * cmat + lax.dot_general(kwt.astype(BF16), vb, (((0,), (0,)), ((), ())),
                                                  preferred_element_type=F32)
        n_scr[h] = decay * nvec + jnp.sum(kwt, axis=0, keepdims=True)
        m_scr[h] = jnp.broadcast_to(m_new, (1, LANES))

        mu = jnp.mean(hh, axis=-1, keepdims=True)
        hc = hh - mu
        var = jnp.mean(hc * hc, axis=-1, keepdims=True)
        hn = hc * lax.rsqrt(var + EPS) * nw_ref[:, sl]
        out = jax.nn.sigmoid(o_ref[0, :, sl]) * hn + sk_ref[:, sl] * xh
        y_ref[0, :, sl] = (out * _silu(z_ref[0, :, sl])).astype(BF16)


def _mlstm(p, conv_w, conv_b, wq, wk, norm_w, skip, fb_row):
    bsz, t, _ = p.shape
    L = M_CHUNK
    cb = lambda c: (lambda bi, i: (bi, i, c))
    full2 = lambda bi, i: (0, 0)
    full3 = lambda bi, i: (0, 0, 0)
    return pl.pallas_call(
        _mlstm_kernel,
        grid=(bsz, t // L),
        in_specs=[pl.BlockSpec((1, L, M_WIDTH), cb(COL_MX // M_WIDTH)),
                  pl.BlockSpec((1, L, M_WIDTH), cb(COL_MV // M_WIDTH)),
                  pl.BlockSpec((1, L, M_WIDTH), cb(COL_MO // M_WIDTH)),
                  pl.BlockSpec((1, L, M_WIDTH), cb(COL_MZ // M_WIDTH)),
                  pl.BlockSpec((1, L, LANES), cb(COL_GATES // LANES)),
                  pl.BlockSpec((CONV_K, M_WIDTH), full2),
                  pl.BlockSpec((1, M_WIDTH), full2),
                  pl.BlockSpec((M_HEADS, M_HD, M_HD), full3),
                  pl.BlockSpec((M_HEADS, M_HD, M_HD), full3),
                  pl.BlockSpec((1, M_WIDTH), full2),
                  pl.BlockSpec((1, M_WIDTH), full2),
                  pl.BlockSpec((1, LANES), full2)],
        out_specs=pl.BlockSpec((1, L, M_WIDTH), lambda bi, i: (bi, i, 0)),
        out_shape=jax.ShapeDtypeStruct((bsz, t, M_WIDTH), BF16),
        scratch_shapes=[pltpu.VMEM((M_HEADS, M_HD, M_HD), F32),
                        pltpu.VMEM((M_HEADS, 1, M_HD), F32),
                        pltpu.VMEM((M_HEADS, 1, LANES), F32),
                        pltpu.VMEM((8, M_WIDTH), F32)],
        compiler_params=_cparams(("parallel", "arbitrary")),
        name="mlstm_group",
    )(p, p, p, p, p, conv_w, conv_b, wq, wk, norm_w, skip, fb_row)


def _compress_kernel(kc0_ref, kc1_ref, vc0_ref, vc1_ref, posk_ref, posv_ref, w1k_ref, w1kc_ref, w2k_ref,
                     w1v_ref, w1vc_ref, w2v_ref, kcmp_ref, vcmpt_ref):
    nb = kcmp_ref.shape[2]

    def hidden(src_refs, pos_ref, w1_ref, w1c_ref):
        acc = jnp.zeros((N_KV * nb, 2 * CMP_HIDDEN), F32)
        for l in range(CMP_STRIDE):
            xl = [r[0, pl.ds(l, nb, stride=CMP_STRIDE), :] for r in src_refs]
            xs = jnp.concatenate([x[:, g * N_HD:(g + 1) * N_HD] for x in xl for g in range(LANES // N_HD)], axis=0)
            acc = acc + jnp.dot(xs.astype(BF16), w1c_ref[l], preferred_element_type=F32)
        first = acc[:, :CMP_HIDDEN]
        second = acc[:, CMP_HIDDEN:]
        posb = jnp.dot(jnp.broadcast_to(pos_ref[...], (8, CMP_LEN * N_HD)).astype(BF16), w1_ref[...],
                       preferred_element_type=F32)[0:1]
        hid = first + pltpu.roll(second, N_KV * nb - 1, 0) + posb
        return jax.nn.gelu(hid).astype(BF16)

    hk = hidden((kc0_ref, kc1_ref), posk_ref, w1k_ref, w1kc_ref)
    kc = jnp.dot(hk, w2k_ref[...], preferred_element_type=F32)
    hv = hidden((vc0_ref, vc1_ref), posv_ref, w1v_ref, w1vc_ref)
    vc = jnp.dot(hv, w2v_ref[...], preferred_element_type=F32)
    j = lax.broadcasted_iota(jnp.int32, (nb, 1), 0)
    lane = lax.broadcasted_iota(jnp.int32, (nb, N_HD), 1)
    n_real = (nb * CMP_STRIDE - CMP_LEN) // CMP_STRIDE + 1
    aux = jnp.where(lane < 3, (j * CMP_STRIDE).astype(F32),
                    jnp.where(lane < 6, (CMP_LEN - 1) * 0.5,
                              jnp.where((lane == 6) & (j >= n_real), 1.0, 0.0))).astype(BF16)
    ones_blk = jnp.where(lax.broadcasted_iota(jnp.int32, (VROWS - N_HD, nb), 0) == 0, 1.0, 0.0)
    for g in range(N_KV):
        kg = kc[g * nb:(g + 1) * nb, 0:N_HD]
        k_hi = kg.astype(BF16)
        k_lo = (kg - k_hi.astype(F32)).astype(BF16)
        kcmp_ref[0, g] = jnp.concatenate([k_hi, k_lo, k_hi, aux], axis=1)
        vt = vc[g * nb:(g + 1) * nb].T
        vcmpt_ref[0, g] = jnp.concatenate([vt[0:N_HD], ones_blk], axis=0).astype(BF16)


def _compress(p, posk, posv, w1k, w1kc, w2k, w1v, w1vc, w2v):
    bsz, t, _ = p.shape
    nb = t // CMP_STRIDE
    cb = lambda c: (lambda bi: (bi, 0, c))
    f2 = lambda bi: (0, 0)
    f3 = lambda bi: (0, 0, 0)
    wspecs = [pl.BlockSpec((CMP_LEN * N_HD, CMP_HIDDEN), f2),
              pl.BlockSpec((CMP_STRIDE, N_HD, 2 * CMP_HIDDEN), f3),
              pl.BlockSpec((CMP_HIDDEN, LANES), f2)]
    return pl.pallas_call(
        _compress_kernel,
        grid=(bsz,),
        in_specs=[pl.BlockSpec((1, t, LANES), cb(COL_KC // LANES)),
                  pl.BlockSpec((1, t, LANES), cb(COL_KC // LANES + 1)),
                  pl.BlockSpec((1, t, LANES), cb(COL_VC // LANES)),
                  pl.BlockSpec((1, t, LANES), cb(COL_VC // LANES + 1)),
                  pl.BlockSpec((1, CMP_LEN * N_HD), f2),
                  pl.BlockSpec((1, CMP_LEN * N_HD), f2)] + wspecs + wspecs,
        out_specs=[pl.BlockSpec((1, N_KV, nb, 4 * N_HD), lambda bi: (bi, 0, 0, 0)),
                   pl.BlockSpec((1, N_KV, VROWS, nb), lambda bi: (bi, 0, 0, 0))],
        out_shape=[jax.ShapeDtypeStruct((bsz, N_KV, nb, 4 * N_HD), BF16),
                   jax.ShapeDtypeStruct((bsz, N_KV, VROWS, nb), BF16)],
        compiler_params=_cparams(("parallel",)),
        name="nsa_compress",
    )(p, p, p, p, posk, posv, w1k, w1kc, w2k, w1v, w1vc, w2v)


def _relayout_kernel(ks_ref, vs_ref, kw_ref, vw_ref, ksa_ref, vsa_ref, kwa_ref, vwa_ref):
    i = pl.program_id(1)
    is_pad = i < PAD_TILES
    base = (i - PAD_TILES) * SKT
    row = lax.broadcasted_iota(jnp.int32, (SKT, 1), 0)
    pos = jnp.where(is_pad, 0, base + row)
    blk = pos // SLC_LEN
    p_hi = (blk * SLC_LEN).astype(F32)
    p_lo = (pos - blk * SLC_LEN).astype(F32)
    flag = jnp.where(is_pad, 1.0, 0.0)
    lane = lax.broadcasted_iota(jnp.int32, (SKT, N_HD), 1)
    al = lane - AUX_ROWS
    mid_w = jnp.where((al >= 0) & (al < 3), p_hi,
                      jnp.where((al >= 3) & (al < 6), p_lo, jnp.where(al == 6, flag, 0.0)))
    mid_s = jnp.where(lane == blk % SEL_GRP, 1.0, mid_w)
    ks = ks_ref[0]
    kw = kw_ref[0]
    for g in range(N_KV):
        sl = slice(g * N_HD, (g + 1) * N_HD)
        ksa_ref[0, g] = jnp.concatenate([ks[:, sl], mid_s], axis=1).astype(BF16)
        kwa_ref[0, g] = jnp.concatenate([kw[:, sl], mid_w], axis=1).astype(BF16)
    ones_blk = jnp.where(lax.broadcasted_iota(jnp.int32, (VROWS - N_HD, SKT), 0) == 0, 1.0, 0.0)
    vst = vs_ref[0].T
    vwt = vw_ref[0].T
    for g in range(N_KV):
        sl = slice(g * N_HD, (g + 1) * N_HD)
        vsa_ref[0, g, 0] = jnp.concatenate([vst[sl], ones_blk], axis=0).astype(BF16)
        vwg = jnp.concatenate([vwt[sl], ones_blk], axis=0).astype(BF16)
        for j in range(SKT // WKT):
            vwa_ref[0, g, j] = vwg[:, j * WKT:(j + 1) * WKT]


def _relayout(p):
    bsz, t, _ = p.shape
    nt = t // SKT + PAD_TILES
    wpt = SKT // WKT
    cb = lambda c: (lambda bi, i: (bi, jnp.maximum(i - PAD_TILES, 0), c))
    return pl.pallas_call(
        _relayout_kernel,
        grid=(bsz, nt),
        in_specs=[pl.BlockSpec((1, SKT, KV_W), cb(COL_KS // KV_W)),
                  pl.BlockSpec((1, SKT, KV_W), cb(COL_VS // KV_W)),
                  pl.BlockSpec((1, SKT, KV_W), cb(COL_KW // KV_W)),
                  pl.BlockSpec((1, SKT, KV_W), cb(COL_VW // KV_W))],
        out_specs=[pl.BlockSpec((1, N_KV, SKT, KA_SEL), lambda bi, i: (bi, 0, i, 0)),
                   pl.BlockSpec((1, N_KV, 1, VROWS, SKT), lambda bi, i: (bi, 0, i, 0, 0)),
                   pl.BlockSpec((1, N_KV, SKT, KA_WIN), lambda bi, i: (bi, 0, i, 0)),
                   pl.BlockSpec((1, N_KV, wpt, VROWS, WKT), lambda bi, i: (bi, 0, i, 0, 0))],
        out_shape=[jax.ShapeDtypeStruct((bsz, N_KV, nt * SKT, KA_SEL), BF16),
                   jax.ShapeDtypeStruct((bsz, N_KV, nt, VROWS, SKT), BF16),
                   jax.ShapeDtypeStruct((bsz, N_KV, nt * SKT, KA_WIN), BF16),
                   jax.ShapeDtypeStruct((bsz, N_KV, nt * wpt, VROWS, WKT), BF16)],
        compiler_params=_cparams(("parallel", "parallel")),
        name="nsa_relayout",
    )(p, p, p, p)


def _nsa_kernel(sp_ref, q_ref, gt_ref, z_ref, kcmp_ref, vcmpt_ref, ovt_ref, ks_ref, vs_ref, kw_ref,
                vw_ref, tri_ref, wb_ref, y_ref, s_scr, g_scr, imp_scr, rank_scr, part_scr, selb_scr):
    g = pl.program_id(1)
    qb = pl.program_id(2)
    start = qb * QBLK
    nb = kcmp_ref.shape[2]
    n_slc = ovt_ref.shape[0]

    qt = (q_ref[0] * (N_HD ** -0.5)).T
    q4t = jnp.concatenate([qt[h * N_HD:(h + 1) * N_HD] for h in range(N_HPG)], axis=1)
    t_row = start + lax.broadcasted_iota(jnp.int32, (1, QBLK), 1)

    g_scr[...] = jax.nn.sigmoid(gt_ref[0]).T

    nw = N_HPG * QBLK
    hl = lax.broadcasted_iota(jnp.int32, (AUX_ROWS, nw), 1) // QBLK
    ar = lax.broadcasted_iota(jnp.int32, (AUX_ROWS, nw), 0)

    def per_head(vals):
        out = jnp.full((AUX_ROWS, nw), vals[N_HPG - 1], F32)
        for h in range(N_HPG - 2, -1, -1):
            out = jnp.where(hl == h, vals[h], out)
        return out

    pieces = [per_head([sp_ref[(g * N_HPG + h) * 3 + j] for h in range(N_HPG)]) for j in range(3)]
    aux = jnp.where(ar == 6, NEG, 0.0)
    for j in range(3):
        aux = jnp.where((ar == j) | (ar == j + 3), pieces[j], aux)
    aux_b = aux.astype(BF16)
    q4l = q4t * LOG2E
    q4s = q4l.astype(BF16)
    q4lo = (q4l - q4s.astype(F32)).astype(BF16)

    def tile4(a):
        return jnp.concatenate([a] * N_HPG, axis=1)

    def col_reduce8(s, op):
        out = s[0:8]
        for r in range(1, s.shape[0] // 8):
            out = op(out, s[8 * r:8 * r + 8])
        return out

    def col_max(s):
        return jnp.max(col_reduce8(s, jnp.maximum), axis=0, keepdims=True)

    def normalise(acc):
        den = acc[N_HD:N_HD + 1]
        return acc[0:N_HD] / jnp.where(den > 0, den, 1.0)

    q_cmp = jnp.concatenate([q4s, q4s, q4lo, aux_b, jnp.zeros((N_HD - AUX_ROWS, nw), BF16)], axis=0)
    cmp_end = lax.broadcasted_iota(jnp.int32, (nb, 1), 0) * CMP_STRIDE + (CMP_LEN - 1)
    s_c = jnp.dot(kcmp_ref[0, 0], q_cmp, preferred_element_type=F32)
    s_c = s_c + tile4(jnp.where(cmp_end <= t_row, 0.0, NEG))
    e_c = jnp.exp2(s_c - col_max(s_c))
    z_c = jnp.sum(col_reduce8(e_c, jnp.add), axis=0, keepdims=True)
    inv_c = jnp.where(tile4(t_row >= CMP_LEN - 1) & (z_c > 0), 1.0 / z_c, 0.0)
    o_c = jnp.dot(vcmpt_ref[0, 0], e_c.astype(BF16), preferred_element_type=F32)[0:N_HD] * inv_c
    p_c = e_c * inv_c
    psum = p_c[:, 0:QBLK]
    for h in range(1, N_HPG):
        psum = psum + p_c[:, h * QBLK:(h + 1) * QBLK]
    parts, rest = [], psum
    for _ in range(3):
        parts.append(rest.astype(BF16))
        rest = rest - parts[-1].astype(F32)
    imp = jnp.dot(ovt_ref[...], jnp.concatenate(parts, axis=0), preferred_element_type=F32)
    sidx = lax.broadcasted_iota(jnp.int32, (n_slc, QBLK), 0)
    cur = t_row // SLC_LEN
    valid = sidx <= cur
    imp = jnp.where(valid, imp, -jnp.inf)
    imp_scr[...] = jnp.where((sidx == 0) | (sidx == cur), jnp.inf, imp)

    wrows = WIN + QBLK
    q_win = jnp.concatenate([q4s, jnp.zeros((AUX_ROWS, nw), BF16), aux_b,
                             jnp.zeros((KA_WIN - N_HD - 2 * AUX_ROWS, nw), BF16)], axis=0)
    s_w = jnp.dot(kw_ref[0, 0, pl.ds(pl.multiple_of(start, QBLK), wrows), :], q_win,
                  preferred_element_type=F32)
    s_w = jnp.concatenate([s_w[0:WKT] + tile4(wb_ref[0]), s_w[WKT:wrows - WKT],
                           s_w[wrows - WKT:] + tile4(wb_ref[1])], axis=0)
    p_w = jnp.exp2(s_w - col_max(s_w)).astype(BF16)
    v_w = jnp.concatenate([vw_ref[0, 0, qb * (QBLK // WKT) + j] for j in range(wrows // WKT)], axis=1)
    o_w = normalise(jnp.dot(v_w, p_w, preferred_element_type=F32))

    def gate_row(h, branch):
        return g_scr[pl.ds(GATE_NG + (g * N_HPG + h) * 3 + branch, 1), :]

    for h in range(N_HPG):
        hs = slice(h * QBLK, (h + 1) * QBLK)
        part_scr[:, hs] = gate_row(h, 0) * o_c[:, hs] + gate_row(h, 2) * o_w[:, hs]

    n_top = min(SLC_TOPN, n_slc)
    rank_scr[...] = jnp.zeros_like(rank_scr)
    sub8 = lax.broadcasted_iota(jnp.int32, (8, QBLK), 0)
    last_blk = (start + QBLK - 1) // SLC_LEN
    for ri in range(n_slc // 8):
        @pl.when((last_blk >= n_top) & (8 * ri <= last_blk))
        def _():
            rows = imp_scr[8 * ri:8 * ri + 8]
            for r in range(n_slc // 8):
                blk8 = imp_scr[8 * r:8 * r + 8]
                acc = rank_scr[8 * r:8 * r + 8]
                for ii in range(8):
                    row = rows[ii:ii + 1]
                    if ri < r:
                        before = row >= blk8
                    elif ri > r:
                        before = row > blk8
                    else:
                        before = (row > blk8) | ((row == blk8) & (sub8 > ii))
                    acc = acc + jnp.where(before, 1.0, 0.0)
                rank_scr[8 * r:8 * r + 8] = acc
    selbias = jnp.where((rank_scr[...] < n_top) & valid, 0.0, NEG)

    selb_scr[...] = tile4(selbias)
    n_tiles = qb // (SKT // QBLK) + 1
    q_tail = jnp.zeros((KA_SEL - N_HD - 2 * AUX_ROWS, nw), BF16)

    def sel_scores(kt):
        grp = pl.multiple_of((kt * SKT // SLC_LEN) // SEL_GRP * SEL_GRP, SEL_GRP)
        sel_rows = jnp.concatenate([selb_scr[pl.ds(grp, SEL_GRP), :], jnp.zeros((AUX_ROWS - SEL_GRP, nw), F32)], axis=0)
        q_sel = jnp.concatenate([q4s, sel_rows.astype(BF16), aux_b, q_tail], axis=0)
        krow = pl.multiple_of((kt + PAD_TILES) * SKT, SKT)
        return jnp.dot(ks_ref[0, 0, pl.ds(krow, SKT), :], q_sel, preferred_element_type=F32)

    def pass1(kt, mrun):
        s = sel_scores(kt)
        s_scr[pl.ds(pl.multiple_of(kt * SKT, SKT), SKT), :] = s
        return jnp.maximum(mrun, col_reduce8(s, jnp.maximum))

    def run_tiles(fn, n, init):
        def group(size, first):
            def body(i, carry):
                for u in range(size):
                    carry = fn(first + i * size + u, carry)
                return carry
            return body

        n_u = n // UNROLL
        carry = lax.fori_loop(0, n_u, group(UNROLL, 0), init)
        done = n_u * UNROLL
        size = UNROLL // 2
        while size >= 1:
            take = (n // size) % 2
            carry = lax.fori_loop(0, take, group(size, done), carry)
            done = done + take * size
            size //= 2
        return carry

    mrun = run_tiles(pass1, n_tiles - 1, jnp.full((8, nw), NEG, F32))
    last = n_tiles - 1
    s_last = sel_scores(last) + tile4(tri_ref[qb % (SKT // QBLK)])
    s_scr[pl.ds(pl.multiple_of(last * SKT, SKT), SKT), :] = s_last
    m_sel = jnp.max(jnp.maximum(mrun, col_reduce8(s_last, jnp.maximum)), axis=0, keepdims=True)

    def pass2(kt, acc):
        s = s_scr[pl.ds(pl.multiple_of(kt * SKT, SKT), SKT), :]
        pr = jnp.exp2(s - m_sel).astype(BF16)
        return acc + jnp.dot(vs_ref[0, 0, kt + PAD_TILES], pr, preferred_element_type=F32)

    o_s = normalise(run_tiles(pass2, n_tiles, jnp.zeros((VROWS, nw), F32)))

    tot = [part_scr[:, h * QBLK:(h + 1) * QBLK] + gate_row(h, 1) * o_s[:, h * QBLK:(h + 1) * QBLK]
           for h in range(N_HPG)]
    o = jnp.concatenate(tot, axis=0).T
    y_ref[0] = (o * _silu(z_ref[0])).astype(BF16)


def _nsa_batch_kernel(sp_ref, q_ref, gt_ref, z_ref, kcmp_ref, vcmpt_ref, ovt_ref, ks_ref, vs_ref, kw_ref,
                      vw_ref, tri_ref, wb_ref, y_ref, sa_scr, sb_scr, g_scr, imp_scr, rank_scr, part_scr, selb_scr):
    g = pl.program_id(0)
    qb = pl.program_id(1)
    n_items = q_ref.shape[0]
    start = qb * QBLK
    nb = kcmp_ref.shape[2]
    n_slc = ovt_ref.shape[0]
    nw = N_HPG * QBLK
    n_top = min(SLC_TOPN, n_slc)
    t_row = start + lax.broadcasted_iota(jnp.int32, (1, QBLK), 1)
    s_bufs = (sa_scr, sb_scr)

    def tile4(a):
        return jnp.concatenate([a] * N_HPG, axis=1)

    def col_reduce8(s, op):
        out = s[0:8]
        for r in range(1, s.shape[0] // 8):
            out = op(out, s[8 * r:8 * r + 8])
        return out

    def col_max(s):
        return jnp.max(col_reduce8(s, jnp.maximum), axis=0, keepdims=True)

    def normalise(acc):
        den = acc[N_HD:N_HD + 1]
        return acc[0:N_HD] / jnp.where(den > 0, den, 1.0)

    def gate_row(i, h, branch):
        return g_scr[i, pl.ds(GATE_NG + (g * N_HPG + h) * 3 + branch, 1), :]

    hl = lax.broadcasted_iota(jnp.int32, (AUX_ROWS, nw), 1) // QBLK
    ar = lax.broadcasted_iota(jnp.int32, (AUX_ROWS, nw), 0)

    def per_head(vals):
        out = jnp.full((AUX_ROWS, nw), vals[N_HPG - 1], F32)
        for h in range(N_HPG - 2, -1, -1):
            out = jnp.where(hl == h, vals[h], out)
        return out

    pieces = [per_head([sp_ref[(g * N_HPG + h) * 3 + j] for h in range(N_HPG)]) for j in range(3)]
    aux = jnp.where(ar == 6, NEG, 0.0)
    for j in range(3):
        aux = jnp.where((ar == j) | (ar == j + 3), pieces[j], aux)
    aux_b = aux.astype(BF16)
    q_tail = jnp.zeros((KA_SEL - N_HD - 2 * AUX_ROWS, nw), BF16)
    sidx = lax.broadcasted_iota(jnp.int32, (n_slc, QBLK), 0)
    cur = t_row // SLC_LEN
    valid = sidx <= cur
    sub8 = lax.broadcasted_iota(jnp.int32, (8, QBLK), 0)
    last_blk = (start + QBLK - 1) // SLC_LEN
    cmp_end = lax.broadcasted_iota(jnp.int32, (nb, 1), 0) * CMP_STRIDE + (CMP_LEN - 1)
    cmp_bias = tile4(jnp.where(cmp_end <= t_row, 0.0, NEG))
    wrows = WIN + QBLK

    def prepare(i):
        qt = (q_ref[i] * (N_HD ** -0.5)).T
        q4l = jnp.concatenate([qt[h * N_HD:(h + 1) * N_HD] for h in range(N_HPG)], axis=1) * LOG2E
        q4s = q4l.astype(BF16)
        q4lo = (q4l - q4s.astype(F32)).astype(BF16)
        g_scr[i] = jax.nn.sigmoid(gt_ref[i]).T

        q_cmp = jnp.concatenate([q4s, q4s, q4lo, aux_b, jnp.zeros((N_HD - AUX_ROWS, nw), BF16)], axis=0)
        s_c = jnp.dot(kcmp_ref[i, 0], q_cmp, preferred_element_type=F32) + cmp_bias
        e_c = jnp.exp2(s_c - col_max(s_c))
        z_c = jnp.sum(col_reduce8(e_c, jnp.add), axis=0, keepdims=True)
        inv_c = jnp.where(tile4(t_row >= CMP_LEN - 1) & (z_c > 0), 1.0 / z_c, 0.0)
        o_c = jnp.dot(vcmpt_ref[i, 0], e_c.astype(BF16), preferred_element_type=F32)[0:N_HD] * inv_c
        p_c = e_c * inv_c
        psum = p_c[:, 0:QBLK]
        for h in range(1, N_HPG):
            psum = psum + p_c[:, h * QBLK:(h + 1) * QBLK]
        parts, rest = [], psum
        for _ in range(3):
            parts.append(rest.astype(BF16))
            rest = rest - parts[-1].astype(F32)
        imp = jnp.dot(ovt_ref[...], jnp.concatenate(parts, axis=0), preferred_element_type=F32)
        imp = jnp.where(valid, imp, -jnp.inf)
        imp_scr[...] = jnp.where((sidx == 0) | (sidx == cur), jnp.inf, imp)

        q_win = jnp.concatenate([q4s, jnp.zeros((AUX_ROWS, nw), BF16), aux_b,
                                 jnp.zeros((KA_WIN - N_HD - 2 * AUX_ROWS, nw), BF16)], axis=0)
        s_w = jnp.dot(kw_ref[i, 0, pl.ds(pl.multiple_of(start, QBLK), wrows), :], q_win,
                      preferred_element_type=F32)
        s_w = jnp.concatenate([s_w[0:WKT] + tile4(wb_ref[0]), s_w[WKT:wrows - WKT],
                               s_w[wrows - WKT:] + tile4(wb_ref[1])], axis=0)
        p_w = jnp.exp2(s_w - col_max(s_w)).astype(BF16)
        v_w = jnp.concatenate([vw_ref[i, 0, qb * (QBLK // WKT) + j] for j in range(wrows // WKT)], axis=1)
        o_w = normalise(jnp.dot(v_w, p_w, preferred_element_type=F32))
        for h in range(N_HPG):
            hs = slice(h * QBLK, (h + 1) * QBLK)
            part_scr[i, :, hs] = gate_row(i, h, 0) * o_c[:, hs] + gate_row(i, h, 2) * o_w[:, hs]

        rank_scr[...] = jnp.zeros_like(rank_scr)
        for ri in range(n_slc // 8):
            @pl.when((last_blk >= n_top) & (8 * ri <= last_blk))
            def _():
                rows = imp_scr[8 * ri:8 * ri + 8]
                for r in range(n_slc // 8):
                    blk8 = imp_scr[8 * r:8 * r + 8]
                    acc = rank_scr[8 * r:8 * r + 8]
                    for ii in range(8):
                        row = rows[ii:ii + 1]
                        if ri < r:
                            before = row >= blk8
                        elif ri > r:
                            before = row > blk8
                        else:
                            before = (row > blk8) | ((row == blk8) & (sub8 > ii))
                        acc = acc + jnp.where(before, 1.0, 0.0)
                    rank_scr[8 * r:8 * r + 8] = acc
        selb_scr[i] = tile4(jnp.where((rank_scr[...] < n_top) & valid, 0.0, NEG))
        return q4s

    q4 = [prepare(i) for i in range(n_items)]

    n_tiles = qb // (SKT // QBLK) + 1
    last = n_tiles - 1

    def sel_scores(i, kt):
        grp = pl.multiple_of((kt * SKT // SLC_LEN) // SEL_GRP * SEL_GRP, SEL_GRP)
        sel_rows = jnp.concatenate([selb_scr[i, pl.ds(grp, SEL_GRP), :], jnp.zeros((AUX_ROWS - SEL_GRP, nw), F32)],
                                   axis=0)
        q_sel = jnp.concatenate([q4[i], sel_rows.astype(BF16), aux_b, q_tail], axis=0)
        krow = pl.multiple_of((kt + PAD_TILES) * SKT, SKT)
        return jnp.dot(ks_ref[i, 0, pl.ds(krow, SKT), :], q_sel, preferred_element_type=F32)

    def tile_rows(kt):
        return pl.ds(pl.multiple_of(kt * SKT, SKT), SKT)

    def pass1(i):
        def fn(kt, mrun):
            s = sel_scores(i, kt)
            s_bufs[i % 2][tile_rows(kt), :] = s
            return jnp.maximum(mrun, col_reduce8(s, jnp.maximum))
        return fn

    def pass2(i, m_sel):
        def fn(kt, acc):
            pr = jnp.exp2(s_bufs[i % 2][tile_rows(kt), :] - m_sel).astype(BF16)
            return acc + jnp.dot(vs_ref[i, 0, kt + PAD_TILES], pr, preferred_element_type=F32)
        return fn

    def both(f1, f2):
        def fn(kt, carry):
            return f1(kt, carry[0]), f2(kt, carry[1])
        return fn

    def run_tiles(fn, n, init):
        def group(size, first):
            def body(j, carry):
                for u in range(size):
                    carry = fn(first + j * size + u, carry)
                return carry
            return body

        n_u = n // UNROLL
        carry = lax.fori_loop(0, n_u, group(UNROLL, 0), init)
        done = n_u * UNROLL
        size = UNROLL // 2
        while size >= 1:
            take = (n // size) % 2
            carry = lax.fori_loop(0, take, group(size, done), carry)
            done = done + take * size
            size //= 2
        return carry

    def diag_tile(i, mrun):
        s = sel_scores(i, last) + tile4(tri_ref[qb % (SKT // QBLK)])
        s_bufs[i % 2][tile_rows(last), :] = s
        return jnp.max(jnp.maximum(mrun, col_reduce8(s, jnp.maximum)), axis=0, keepdims=True)

    mrun0 = jnp.full((8, nw), NEG, F32)
    acc0 = jnp.zeros((VROWS, nw), F32)
    m_prev = diag_tile(0, run_tiles(pass1(0), last, mrun0))
    outs = []
    for i in range(1, n_items):
        p2 = pass2(i - 1, m_prev)
        mrun, acc = run_tiles(both(pass1(i), p2), last, (mrun0, acc0))
        m_cur = diag_tile(i, mrun)
        outs.append(normalise(p2(last, acc)))
        m_prev = m_cur
    outs.append(normalise(run_tiles(pass2(n_items - 1, m_prev), n_tiles, acc0)))

    for i in range(n_items):
        tot = [part_scr[i, :, h * QBLK:(h + 1) * QBLK] + gate_row(i, h, 1) * outs[i][:, h * QBLK:(h + 1) * QBLK]
               for h in range(N_HPG)]
        o = jnp.concatenate(tot, axis=0).T
        y_ref[i] = (o * _silu(z_ref[i])).astype(BF16)


def _edge_biases():
    kl = np.arange(SKT)[:, None]
    ql = np.arange(QBLK)[None, :]
    tri = np.stack([np.where(kl <= par * QBLK + ql, 0.0, NEG) for par in range(SKT // QBLK)])
    kk = np.arange(WKT)[:, None]
    wb = np.stack([np.where(kk > ql, 0.0, NEG), np.where(kk <= ql + WKT - QBLK, 0.0, NEG)])
    return jnp.asarray(tri, F32), jnp.asarray(wb, F32)


def _nsa(p, spieces, kcmp, vcmpt, ovt, ks, vs, kw, vw):
    bsz, t, _ = p.shape
    n_slc = t // SLC_LEN
    gw = N_HPG * N_HD
    tri, wb = _edge_biases()
    nw = N_HPG * QBLK
    per_g = lambda a: pl.BlockSpec((bsz, 1) + a.shape[2:], lambda g, i, sp: (0, g) + (0,) * (a.ndim - 2),
                                   pipeline_mode=pl.Buffered(1))
    const = lambda a: pl.BlockSpec(a.shape, lambda g, i, sp: (0,) * a.ndim)
    grid_spec = pltpu.PrefetchScalarGridSpec(
        num_scalar_prefetch=1,
        grid=(N_KV, t // QBLK),
        in_specs=[pl.BlockSpec((bsz, QBLK, gw), lambda g, i, sp: (0, i, COL_NQ // gw + g)),
                  pl.BlockSpec((bsz, QBLK, LANES), lambda g, i, sp: (0, i, COL_GATES // LANES)),
                  pl.BlockSpec((bsz, QBLK, gw), lambda g, i, sp: (0, i, COL_NZ // gw + g)),
                  per_g(kcmp), per_g(vcmpt), const(ovt),
                  per_g(ks), per_g(vs), per_g(kw), per_g(vw), const(tri), const(wb)],
        out_specs=pl.BlockSpec((bsz, QBLK, gw), lambda g, i, sp: (0, i, g)),
        scratch_shapes=[pltpu.VMEM((t, nw), F32),
                        pltpu.VMEM((t, nw), F32),
                        pltpu.VMEM((bsz, LANES, QBLK), F32),
                        pltpu.VMEM((n_slc, QBLK), F32),
                        pltpu.VMEM((n_slc, QBLK), F32),
                        pltpu.VMEM((bsz, N_HD, nw), F32),
                        pltpu.VMEM((bsz, n_slc, nw), F32)],
    )
    return pl.pallas_call(
        _nsa_batch_kernel,
        grid_spec=grid_spec,
        out_shape=jax.ShapeDtypeStruct((bsz, t, N_WIDTH), BF16),
        compiler_params=_cparams(("arbitrary", "arbitrary")),
        name="nsa_attention",
    )(spieces, p, p, p, kcmp, vcmpt, ovt, ks, vs, kw, vw, tri, wb)


def _outproj_kernel(ym_ref, yn_ref, w_ref, x_ref, gate_ref, fg_ref, o_ref, wb_scr, *, final):
    @pl.when((pl.program_id(0) == 0) & (pl.program_id(1) == 0))
    def _():
        rows = w_ref.shape[1]
        for c in range(rows // WP_TILE):
            cs = slice(c * WP_TILE, (c + 1) * WP_TILE)
            wb_scr[cs, :] = w_ref[0, cs, :].astype(BF16)

    y = jnp.dot(ym_ref[0], wb_scr[0:M_WIDTH, :], preferred_element_type=F32)
    y = y + jnp.dot(yn_ref[0], wb_scr[M_WIDTH:, :], preferred_element_type=F32)
    hres = x_ref[0] + gate_ref[0] * y
    if final:
        ms = jnp.mean(hres * hres, axis=-1, keepdims=True)
        hres = hres * lax.rsqrt(ms + EPS) * fg_ref[...]
    o_ref[0] = hres


def _outproj(ym, yn, w, layer, x, gate, fg, final):
    bsz, t, d = x.shape
    tm = 512
    return pl.pallas_call(
        functools.partial(_outproj_kernel, final=final),
        grid=(bsz, t // tm),
        in_specs=[pl.BlockSpec((1, tm, M_WIDTH), lambda bi, i: (bi, i, 0)),
                  pl.BlockSpec((1, tm, N_WIDTH), lambda bi, i: (bi, i, 0)),
                  pl.BlockSpec((1, M_WIDTH + N_WIDTH, d), lambda bi, i: (layer, 0, 0),
                               pipeline_mode=pl.Buffered(1)),
                  pl.BlockSpec((1, tm, d), lambda bi, i: (bi, i, 0)),
                  pl.BlockSpec((1, 1, d), lambda bi, i: (bi, 0, 0)),
                  pl.BlockSpec((1, d), lambda bi, i: (0, 0))],
        out_specs=pl.BlockSpec((1, tm, d), lambda bi, i: (bi, i, 0)),
        out_shape=jax.ShapeDtypeStruct((bsz, t, d), F32),
        scratch_shapes=[pltpu.VMEM((M_WIDTH + N_WIDTH, d), BF16)],
        compiler_params=_cparams(("arbitrary", "arbitrary")),
        name="outproj_residual",
    )(ym, yn, w, x, gate, fg)


SRC_MI = 4 * M_WIDTH
SRC_NQ = SRC_MI + 2 * M_HEADS
SRC_NG = SRC_NQ + N_WIDTH + 6 * KV_W
SRC_NZ = SRC_NG + 3 * N_HEADS


def _reorder_cols(a):
    parts = [a[..., 0:SRC_MI], a[..., SRC_NQ:SRC_NG], a[..., SRC_NZ:SRC_NZ + N_WIDTH], a[..., SRC_MI:SRC_NQ],
             a[..., SRC_NG:SRC_NZ]]
    used = sum(x.shape[-1] for x in parts)
    parts.append(jnp.zeros(a.shape[:-1] + (NP_PAD - used,), a.dtype))
    return jnp.concatenate(parts, axis=-1)


WP_TILE = 256
WP_GATE_TILE = COL_GATES // WP_TILE


def _wprep_kernel(w_ref, g1_ref, g2_ref, o_ref):
    j = pl.program_id(0)
    d = o_ref.shape[0]

    @pl.when(j < WP_GATE_TILE)
    def _():
        for c in range(d // WP_TILE):
            cs = slice(c * WP_TILE, (c + 1) * WP_TILE)
            o_ref[cs, :] = w_ref[0, :, cs].T.astype(BF16)

    @pl.when(j == WP_GATE_TILE)
    def _():
        n_gate = g1_ref.shape[1] + g2_ref.shape[1]
        gt = jnp.concatenate([g1_ref[0], g2_ref[0], jnp.zeros((LANES - n_gate, d), F32)], axis=0)
        for c in range(d // WP_TILE):
            cs = slice(c * WP_TILE, (c + 1) * WP_TILE)
            o_ref[cs, 0:LANES] = gt[:, cs].T.astype(BF16)
        o_ref[:, LANES:] = jnp.zeros((d, WP_TILE - LANES), BF16)

    @pl.when(j > WP_GATE_TILE)
    def _():
        o_ref[...] = jnp.zeros_like(o_ref)


def _wprep(w_t, layer):
    _, n, d = w_t.shape
    n_big = COL_GATES // WP_TILE

    def src_row(j):
        return jnp.where(j < COL_NQ // WP_TILE, j * WP_TILE,
                         jnp.where(j < COL_NZ // WP_TILE, SRC_NQ + (j - COL_NQ // WP_TILE) * WP_TILE,
                                   jnp.where(j < n_big, SRC_NZ + (j - COL_NZ // WP_TILE) * WP_TILE, 0)))

    el = pl.Element
    return pl.pallas_call(
        _wprep_kernel,
        grid=(NP_PAD // WP_TILE,),
        in_specs=[pl.BlockSpec((el(1), el(WP_TILE), el(d)), lambda j: (layer, pl.multiple_of(src_row(j), 8), 0)),
                  pl.BlockSpec((el(1), el(SRC_NQ - SRC_MI), el(d)), lambda j: (layer, SRC_MI, 0)),
                  pl.BlockSpec((el(1), el(SRC_NZ - SRC_NG), el(d)), lambda j: (layer, SRC_NG, 0))],
        out_specs=pl.BlockSpec((d, WP_TILE), lambda j: (0, j)),
        out_shape=jax.ShapeDtypeStruct((d, NP_PAD), BF16),
        compiler_params=_cparams(("parallel",)),
        name="inproj_weight_prep",
    )(w_t, w_t, w_t)


def _overlap_t(t):
    n_cmp_rows = t // CMP_STRIDE
    n_slc = t // SLC_LEN
    c0 = np.arange(n_cmp_rows) * CMP_STRIDE
    s0 = np.arange(n_slc) * SLC_LEN
    ov = (c0[None, :] <= s0[:, None] + SLC_LEN - 1) & (c0[None, :] + CMP_LEN - 1 >= s0[:, None])
    ov[:, (t - CMP_LEN) // CMP_STRIDE + 1:] = False
    return jnp.asarray(np.concatenate([ov] * 3, axis=1), BF16)


def kernel(x, c, ln_g, w_ada, b_ada, w_in, b_in, m_conv_w, m_conv_b, m_wq, m_wk, m_norm_w, m_skip, m_f_bias,
           n_pos_k, n_pos_v, n_w1_k, n_w2_k, n_w1_v, n_w2_v, w_out, final_g):
    out_dtype = x.dtype
    bsz, t, d = x.shape
    depth = ln_g.shape[0]
    h_res = x.astype(F32)
    assert bsz <= 8
    c_t = jnp.zeros((d, 8), F32).at[:, :bsz].set(c.astype(F32).T)
    slopes_np = np.array([2.0 ** (-8.0 * (h + 1) / N_HEADS) for h in range(N_HEADS)], np.float32)
    rest = (slopes_np.astype(np.float64) * LOG2E).astype(np.float32)
    pieces = []
    for _ in range(3):
        pieces.append(rest.astype(jnp.bfloat16).astype(np.float32))
        rest = rest - pieces[-1]
    spieces = jnp.asarray(np.stack(pieces, axis=1).reshape(-1))
    ovt = _overlap_t(t)

    def w1cat(w1):
        w = w1.reshape(2, CMP_STRIDE, N_HD, CMP_HIDDEN)
        return jnp.concatenate([w[0], w[1]], axis=-1).astype(BF16)

    def w2pad(w2):
        return jnp.pad(w2, ((0, 0), (0, LANES - N_HD))).astype(BF16)

    for l in range(depth):
        mod = _ada(c_t, w_ada, b_ada[l][None, :], l, bsz)[:bsz]
        shift, scale, gate = mod[:, None, 0:d], mod[:, None, d:2 * d], mod[:, None, 2 * d:3 * d]
        p = _inproj(h_res, ln_g[l][None, :], scale, shift,
                    _wprep(jnp.swapaxes(w_in, 1, 2), l), _reorder_cols(b_in[l])[None, :])
        fb_row = jnp.zeros((1, LANES), F32).at[0, M_HEADS:2 * M_HEADS].set(m_f_bias[l])
        y_m = _mlstm(p, m_conv_w[l], m_conv_b[l][None, :], m_wq[l].astype(BF16), m_wk[l].astype(BF16),
                     m_norm_w[l][None, :], m_skip[l][None, :], fb_row)
        kcmp, vcmpt = _compress(p, n_pos_k[l].reshape(1, -1), n_pos_v[l].reshape(1, -1),
                                n_w1_k[l].astype(BF16), w1cat(n_w1_k[l]), w2pad(n_w2_k[l]),
                                n_w1_v[l].astype(BF16), w1cat(n_w1_v[l]), w2pad(n_w2_v[l]))
        ks, vs, kw, vw = _relayout(p)
        y_n = _nsa(p, spieces, kcmp, vcmpt, ovt, ks, vs, kw, vw)
        h_res = _outproj(y_m, y_n, w_out, l, h_res, gate, final_g[None, :], l == depth - 1)
    return h_res.astype(out_dtype)
```

```python
import functools

import numpy as np
import jax
import jax.numpy as jnp
from jax import lax
from jax.experimental import pallas as pl
from jax.experimental.pallas import tpu as pltpu

F32 = jnp.float32
BF16 = jnp.bfloat16

EPS = 1e-6
M_HEADS = 4
M_HD = 256
M_WIDTH = M_HEADS * M_HD
CONV_K = 4
M_CHUNK = 256
N_HEADS = 16
N_HD = 64
N_KV = 4
N_HPG = N_HEADS // N_KV
N_WIDTH = N_HEADS * N_HD
KV_W = N_KV * N_HD
CMP_LEN = 32
CMP_STRIDE = 16
CMP_HIDDEN = 2 * N_HD
SLC_LEN = 64
SLC_TOPN = 16
WIN = 512
QBLK = 256
SKT = 256
WKT = 256
PAD_TILES = WIN // SKT
SEL_GRP = 8
KA_SEL = 128
KA_WIN = 128
VROWS = 80
UNROLL = 8
AUX_ROWS = 16
LOG2E = 1.4426950408889634

COL_MX, COL_MV, COL_MO, COL_MZ = 0, 1024, 2048, 3072
COL_NQ = 4096
COL_KC, COL_VC, COL_KS, COL_VS, COL_KW, COL_VW = 5120, 5376, 5632, 5888, 6144, 6400
COL_NZ = 6656
COL_GATES = 7680
GATE_NG = 2 * M_HEADS
NP_PAD = 8192
NORM_CHUNKS = 4
LANES = 128
NEG = -1e30
VMEM_LIMIT = 56 * 1024 * 1024


def _cparams(sem):
    return pltpu.CompilerParams(dimension_semantics=sem, vmem_limit_bytes=VMEM_LIMIT)


def _silu(x):
    return x * jax.nn.sigmoid(x)


def _log_sigmoid(x):
    return jnp.minimum(x, 0.0) - jnp.log1p(jnp.exp(-jnp.abs(x)))


def _ada_kernel(ct_ref, w_ref, b_ref, o_ref, *, bsz):
    s_t = _silu(ct_ref[...])
    w = w_ref[0]
    row = lax.broadcasted_iota(jnp.int32, o_ref.shape, 0)
    out = jnp.zeros(o_ref.shape, F32)
    for b in range(bsz):
        prod = w * s_t[:, b:b + 1]
        acc = prod[0:8]
        for r in range(1, prod.shape[0] // 8):
            acc = acc + prod[8 * r:8 * r + 8]
        out = jnp.where(row == b, jnp.sum(acc, axis=0, keepdims=True) + b_ref[...], out)
    o_ref[...] = out


def _ada(c_t, w, b, layer, bsz):
    _, d, n = w.shape
    tn = 1024
    return pl.pallas_call(
        functools.partial(_ada_kernel, bsz=bsz),
        grid=(n // tn,),
        in_specs=[pl.BlockSpec((d, 8), lambda j: (0, 0)),
                  pl.BlockSpec((1, d, tn), lambda j: (layer, 0, j)),
                  pl.BlockSpec((1, tn), lambda j: (0, j))],
        out_specs=pl.BlockSpec((8, tn), lambda j: (0, j)),
        out_shape=jax.ShapeDtypeStruct((8, n), F32),
        compiler_params=_cparams(("parallel",)),
        name="ada_mod",
    )(c_t, w, b)


def _inproj_kernel(x_ref, g_ref, sc_ref, sh_ref, w_ref, b_ref, o_ref, h_ref):
    first = pl.program_id(2) == 0

    @pl.when(first)
    def _():
        tm = x_ref.shape[1]
        ck = tm // NORM_CHUNKS
        for c in range(NORM_CHUNKS):
            rows = slice(c * ck, (c + 1) * ck)
            x = x_ref[0, rows, :]
            ms = jnp.mean(x * x, axis=-1, keepdims=True)
            h = x * lax.rsqrt(ms + EPS) * g_ref[...]
            h = (h * (1.0 + sc_ref[0]) + sh_ref[0]).astype(BF16)
            h_ref[rows, :] = h
            o_ref[0, rows, :] = jnp.dot(h, w_ref[...], preferred_element_type=F32) + b_ref[...]

    j = pl.program_id(2)
    nj = pl.num_programs(2)
    tn = o_ref.shape[2]
    used = COL_GATES + LANES - (NP_PAD - tn)

    @pl.when(jnp.logical_not(first) & (j < nj - 1))
    def _():
        o_ref[0] = jnp.dot(h_ref[...], w_ref[...], preferred_element_type=F32) + b_ref[...]

    @pl.when(jnp.logical_not(first) & (j == nj - 1))
    def _():
        o_ref[0, :, 0:used] = (jnp.dot(h_ref[...], w_ref[:, 0:used], preferred_element_type=F32)
                               + b_ref[:, 0:used])
        o_ref[0, :, used:] = jnp.zeros((o_ref.shape[1], tn - used), F32)


def _inproj(x, g, scale, shift, w, b):
    bsz, t, d = x.shape
    n = w.shape[1]
    tm, tn = 1024, 1024
    return pl.pallas_call(
        _inproj_kernel,
        grid=(bsz, t // tm, n // tn),
        in_specs=[pl.BlockSpec((1, tm, d), lambda bi, i, j: (bi, i, 0)),
                  pl.BlockSpec((1, d), lambda bi, i, j: (0, 0)),
                  pl.BlockSpec((1, 1, d), lambda bi, i, j: (bi, 0, 0)),
                  pl.BlockSpec((1, 1, d), lambda bi, i, j: (bi, 0, 0)),
                  pl.BlockSpec((d, tn), lambda bi, i, j: (0, j)),
                  pl.BlockSpec((1, tn), lambda bi, i, j: (0, j))],
        out_specs=pl.BlockSpec((1, tm, tn), lambda bi, i, j: (bi, i, j)),
        out_shape=jax.ShapeDtypeStruct((bsz, t, n), F32),
        scratch_shapes=[pltpu.VMEM((tm, d), BF16)],
        compiler_params=_cparams(("parallel", "parallel", "arbitrary")),
        name="norm_inproj",
    )(x, g, scale, shift, w, b)


def _mlstm_kernel(x_ref, v_ref, o_ref, z_ref, gt_ref, cw_ref, cb_ref, wq_ref, wk_ref, nw_ref, sk_ref, fb_ref,
                  y_ref, c_scr, n_scr, m_scr, xp_scr):
    L = M_CHUNK

    @pl.when(pl.program_id(1) == 0)
    def _():
        c_scr[...] = jnp.zeros_like(c_scr)
        n_scr[...] = jnp.zeros_like(n_scr)
        m_scr[...] = jnp.zeros_like(m_scr)
        xp_scr[...] = jnp.zeros_like(xp_scr)

    x = x_ref[0]
    prev = xp_scr[...]
    row8 = lax.broadcasted_iota(jnp.int32, (8, M_WIDTH), 0)
    cw = cw_ref[...]
    xc = cb_ref[...] + x * cw[CONV_K - 1:CONV_K, :]
    for sft in range(1, CONV_K):
        xr = pltpu.roll(x, sft, 0)
        top = jnp.where(row8 < sft, pltpu.roll(prev, sft, 0), xr[0:8])
        xs = jnp.concatenate([top, xr[8:]], axis=0)
        xc = xc + xs * cw[CONV_K - 1 - sft:CONV_K - sft, :]
    xp_scr[...] = x[L - 8:L]
    xc = _silu(xc)

    gt = gt_ref[0]
    col = lax.broadcasted_iota(jnp.int32, (L, LANES), 1)
    logf = _log_sigmoid(gt + fb_ref[...])
    a_c = jnp.where((col >= M_HEADS) & (col < 2 * M_HEADS), logf, gt)
    ri = lax.broadcasted_iota(jnp.int32, (L, L), 0)
    ci = lax.broadcasted_iota(jnp.int32, (L, L), 1)
    causal = ri >= ci
    hp = lax.Precision.HIGHEST
    tri = causal.astype(F32)
    tri_t = (ri <= ci).astype(F32)
    b_c = jnp.dot(tri, a_c, precision=hp, preferred_element_type=F32)
    a_r = a_c.T
    b_r = jnp.dot(a_r[0:8], tri_t, precision=hp, preferred_element_type=F32)

    for h in range(M_HEADS):
        sl = slice(h * M_HD, (h + 1) * M_HD)
        xh = xc[:, sl]
        xb = xh.astype(BF16)
        q = jnp.dot(xb, wq_ref[h], preferred_element_type=F32)
        k = jnp.dot(xb, wk_ref[h], preferred_element_type=F32) * (M_HD ** -0.5)
        vb = v_ref[0, :, sl].astype(BF16)
        qb = q.astype(BF16)
        kb = k.astype(BF16)

        bt = b_c[:, M_HEADS + h:M_HEADS + h + 1]
        ic = a_c[:, h:h + 1]
        bs = b_r[M_HEADS + h:M_HEADS + h + 1, :]
        ir = a_r[h:h + 1, :]
        m_prev = m_scr[h][:, 0:1]

        dm = jnp.where(causal, bt - bs + ir, -jnp.inf)
        inter = bt + m_prev
        m_t = jnp.maximum(inter, jnp.max(dm, axis=-1, keepdims=True))
        w_in = jnp.exp(dm - m_t)
        w_st = jnp.exp(inter - m_t)
        s = lax.dot_general(qb, kb, (((1,), (1,)), ((), ())), preferred_element_type=F32) * w_in
        cmat = c_scr[h]
        nvec = n_scr[h]
        sb = s.astype(BF16)
        num = w_st * jnp.dot(qb, cmat.astype(BF16), preferred_element_type=F32) \
            + jnp.dot(sb, vb, preferred_element_type=F32)
        nt_dims = (((1,), (1,)), ((), ()))
        qn = lax.dot_general(qb, jnp.broadcast_to(nvec, (8, M_HD)).astype(BF16), nt_dims,
                             preferred_element_type=F32)[:, 0:1]
        ssum = lax.dot_general(sb, jnp.ones((8, L), BF16), nt_dims, preferred_element_type=F32)[:, 0:1]
        den = w_st * qn + ssum
        hh = num / jnp.maximum(jnp.abs(den), jnp.exp(-m_t))

        b_last = bt[L - 1:L, :]
        w_end = b_last - bt + ic
        m_new = jnp.maximum(b_last + m_prev, jnp.max(w_end, axis=0, keepdims=True))
        decay = jnp.exp(b_last + m_prev - m_new)
        kwt = k * jnp.exp(w_end - m_new)
        c_scr[h] = decay * cmat + lax.dot_general(kwt.astype(BF16), vb, (((0,), (0,)), ((), ())),
                                                  preferred_element_type=F32)
        n_scr[h] = decay * nvec + jnp.sum(kwt, axis=0, keepdims=True)
        m_scr[h] = jnp.broadcast_to(m_new, (1, LANES))

        mu = jnp.mean(hh, axis=-1, keepdims=True)
        hc = hh - mu
        var = jnp.mean(hc * hc, axis=-1, keepdims=True)
        hn = hc * lax.rsqrt(var + EPS) * nw_ref[:, sl]
        out = jax.nn.sigmoid(o_ref[0, :, sl]) * hn + sk_ref[:, sl] * xh
        y_ref[0, :, sl] = (out * _silu(z_ref[0, :, sl])).astype(BF16)


def _mlstm(p, conv_w, conv_b, wq, wk, norm_w, skip, fb_row):
    bsz, t, _ = p.shape
    L = M_CHUNK
    cb = lambda c: (lambda bi, i: (bi, i, c))
    full2 = lambda bi, i: (0, 0)
    full3 = lambda bi, i: (0, 0, 0)
    return pl.pallas_call(
        _mlstm_kernel,
        grid=(bsz, t // L),
        in_specs=[pl.BlockSpec((1, L, M_WIDTH), cb(COL_MX // M_WIDTH)),
                  pl.BlockSpec((1, L, M_WIDTH), cb(COL_MV // M_WIDTH)),
                  pl.BlockSpec((1, L, M_WIDTH), cb(COL_MO // M_WIDTH)),
                  pl.BlockSpec((1, L, M_WIDTH), cb(COL_MZ // M_WIDTH)),
                  pl.BlockSpec((1, L, LANES), cb(COL_GATES // LANES)),
                  pl.BlockSpec((CONV_K, M_WIDTH), full2),
                  pl.BlockSpec((1, M_WIDTH), full2),
                  pl.BlockSpec((M_HEADS, M_HD, M_HD), full3),
                  pl.BlockSpec((M_HEADS, M_HD, M_HD), full3),
                  pl.BlockSpec((1, M_WIDTH), full2),
                  pl.BlockSpec((1, M_WIDTH), full2),
                  pl.BlockSpec((1, LANES), full2)],
        out_specs=pl.BlockSpec((1, L, M_WIDTH), lambda bi, i: (bi, i, 0)),
        out_shape=jax.ShapeDtypeStruct((bsz, t, M_WIDTH), BF16),
        scratch_shapes=[pltpu.VMEM((M_HEADS, M_HD, M_HD), F32),
                        pltpu.VMEM((M_HEADS, 1, M_HD), F32),
                        pltpu.VMEM((M_HEADS, 1, LANES), F32),
                        pltpu.VMEM((8, M_WIDTH), F32)],
        compiler_params=_cparams(("parallel", "arbitrary")),
        name="mlstm_group",
    )(p, p, p, p, p, conv_w, conv_b, wq, wk, norm_w, skip, fb_row)


def _compress_kernel(kc0_ref, kc1_ref, vc0_ref, vc1_ref, posk_ref, posv_ref, w1k_ref, w1kc_ref, w2k_ref,
                     w1v_ref, w1vc_ref, w2v_ref, kcmp_ref, vcmpt_ref):
    nb = kcmp_ref.shape[2]

    def hidden(src_refs, pos_ref, w1_ref, w1c_ref):
        acc = jnp.zeros((N_KV * nb, 2 * CMP_HIDDEN), F32)
        for l in range(CMP_STRIDE):
            xl = [r[0, pl.ds(l, nb, stride=CMP_STRIDE), :] for r in src_refs]
            xs = jnp.concatenate([x[:, g * N_HD:(g + 1) * N_HD] for x in xl for g in range(LANES // N_HD)], axis=0)
            acc = acc + jnp.dot(xs.astype(BF16), w1c_ref[l], preferred_element_type=F32)
        first = acc[:, :CMP_HIDDEN]
        second = acc[:, CMP_HIDDEN:]
        posb = jnp.dot(jnp.broadcast_to(pos_ref[...], (8, CMP_LEN * N_HD)).astype(BF16), w1_ref[...],
                       preferred_element_type=F32)[0:1]
        hid = first + pltpu.roll(second, N_KV * nb - 1, 0) + posb
        return jax.nn.gelu(hid).astype(BF16)

    hk = hidden((kc0_ref, kc1_ref), posk_ref, w1k_ref, w1kc_ref)
    kc = jnp.dot(hk, w2k_ref[...], preferred_element_type=F32)
    hv = hidden((vc0_ref, vc1_ref), posv_ref, w1v_ref, w1vc_ref)
    vc = jnp.dot(hv, w2v_ref[...], preferred_element_type=F32)
    j = lax.broadcasted_iota(jnp.int32, (nb, 1), 0)
    lane = lax.broadcasted_iota(jnp.int32, (nb, N_HD), 1)
    n_real = (nb * CMP_STRIDE - CMP_LEN) // CMP_STRIDE + 1
    aux = jnp.where(lane < 3, (j * CMP_STRIDE).astype(F32),
                    jnp.where(lane < 6, (CMP_LEN - 1) * 0.5,
                              jnp.where((lane == 6) & (j >= n_real), 1.0, 0.0))).astype(BF16)
    ones_blk = jnp.where(lax.broadcasted_iota(jnp.int32, (VROWS - N_HD, nb), 0) == 0, 1.0, 0.0)
    for g in range(N_KV):
        kg = kc[g * nb:(g + 1) * nb, 0:N_HD]
        k_hi = kg.astype(BF16)
        k_lo = (kg - k_hi.astype(F32)).astype(BF16)
        kcmp_ref[0, g] = jnp.concatenate([k_hi, k_lo, k_hi, aux], axis=1)
        vt = vc[g * nb:(g + 1) * nb].T
        vcmpt_ref[0, g] = jnp.concatenate([vt[0:N_HD], ones_blk], axis=0).astype(BF16)


def _compress(p, posk, posv, w1k, w1kc, w2k, w1v, w1vc, w2v):
    bsz, t, _ = p.shape
    nb = t // CMP_STRIDE
    cb = lambda c: (lambda bi: (bi, 0, c))
    f2 = lambda bi: (0, 0)
    f3 = lambda bi: (0, 0, 0)
    wspecs = [pl.BlockSpec((CMP_LEN * N_HD, CMP_HIDDEN), f2),
              pl.BlockSpec((CMP_STRIDE, N_HD, 2 * CMP_HIDDEN), f3),
              pl.BlockSpec((CMP_HIDDEN, LANES), f2)]
    return pl.pallas_call(
        _compress_kernel,
        grid=(bsz,),
        in_specs=[pl.BlockSpec((1, t, LANES), cb(COL_KC // LANES)),
                  pl.BlockSpec((1, t, LANES), cb(COL_KC // LANES + 1)),
                  pl.BlockSpec((1, t, LANES), cb(COL_VC // LANES)),
                  pl.BlockSpec((1, t, LANES), cb(COL_VC // LANES + 1)),
                  pl.BlockSpec((1, CMP_LEN * N_HD), f2),
                  pl.BlockSpec((1, CMP_LEN * N_HD), f2)] + wspecs + wspecs,
        out_specs=[pl.BlockSpec((1, N_KV, nb, 4 * N_HD), lambda bi: (bi, 0, 0, 0)),
                   pl.BlockSpec((1, N_KV, VROWS, nb), lambda bi: (bi, 0, 0, 0))],
        out_shape=[jax.ShapeDtypeStruct((bsz, N_KV, nb, 4 * N_HD), BF16),
                   jax.ShapeDtypeStruct((bsz, N_KV, VROWS, nb), BF16)],
        compiler_params=_cparams(("parallel",)),
        name="nsa_compress",
    )(p, p, p, p, posk, posv, w1k, w1kc, w2k, w1v, w1vc, w2v)


def _relayout_kernel(ks_ref, vs_ref, kw_ref, vw_ref, ksa_ref, vsa_ref, kwa_ref, vwa_ref):
    i = pl.program_id(1)
    is_pad = i < PAD_TILES
    base = (i - PAD_TILES) * SKT
    row = lax.broadcasted_iota(jnp.int32, (SKT, 1), 0)
    pos = jnp.where(is_pad, 0, base + row)
    blk = pos // SLC_LEN
    p_hi = (blk * SLC_LEN).astype(F32)
    p_lo = (pos - blk * SLC_LEN).astype(F32)
    flag = jnp.where(is_pad, 1.0, 0.0)
    lane = lax.broadcasted_iota(jnp.int32, (SKT, N_HD), 1)
    al = lane - AUX_ROWS
    mid_w = jnp.where((al >= 0) & (al < 3), p_hi,
                      jnp.where((al >= 3) & (al < 6), p_lo, jnp.where(al == 6, flag, 0.0)))
    mid_s = jnp.where(lane == blk % SEL_GRP, 1.0, mid_w)
    ks = ks_ref[0]
    kw = kw_ref[0]
    for g in range(N_KV):
        sl = slice(g * N_HD, (g + 1) * N_HD)
        ksa_ref[0, g] = jnp.concatenate([ks[:, sl], mid_s], axis=1).astype(BF16)
        kwa_ref[0, g] = jnp.concatenate([kw[:, sl], mid_w], axis=1).astype(BF16)
    ones_blk = jnp.where(lax.broadcasted_iota(jnp.int32, (VROWS - N_HD, SKT), 0) == 0, 1.0, 0.0)
    vst = vs_ref[0].T
    vwt = vw_ref[0].T
    for g in range(N_KV):
        sl = slice(g * N_HD, (g + 1) * N_HD)
        vsa_ref[0, g, 0] = jnp.concatenate([vst[sl], ones_blk], axis=0).astype(BF16)
        vwg = jnp.concatenate([vwt[sl], ones_blk], axis=0).astype(BF16)
        for j in range(SKT // WKT):
            vwa_ref[0, g, j] = vwg[:, j * WKT:(j + 1) * WKT]


def _relayout(p):
    bsz, t, _ = p.shape
    nt = t // SKT + PAD_TILES
    wpt = SKT // WKT
    cb = lambda c: (lambda bi, i: (bi, jnp.maximum(i - PAD_TILES, 0), c))
    return pl.pallas_call(
        _relayout_kernel,
        grid=(bsz, nt),
        in_specs=[pl.BlockSpec((1, SKT, KV_W), cb(COL_KS // KV_W)),
                  pl.BlockSpec((1, SKT, KV_W), cb(COL_VS // KV_W)),
                  pl.BlockSpec((1, SKT, KV_W), cb(COL_KW // KV_W)),
                  pl.BlockSpec((1, SKT, KV_W), cb(COL_VW // KV_W))],
        out_specs=[pl.BlockSpec((1, N_KV, SKT, KA_SEL), lambda bi, i: (bi, 0, i, 0)),
                   pl.BlockSpec((1, N_KV, 1, VROWS, SKT), lambda bi, i: (bi, 0, i, 0, 0)),
                   pl.BlockSpec((1, N_KV, SKT, KA_WIN), lambda bi, i: (bi, 0, i, 0)),
                   pl.BlockSpec((1, N_KV, wpt, VROWS, WKT), lambda bi, i: (bi, 0, i, 0, 0))],
        out_shape=[jax.ShapeDtypeStruct((bsz, N_KV, nt * SKT, KA_SEL), BF16),
                   jax.ShapeDtypeStruct((bsz, N_KV, nt, VROWS, SKT), BF16),
                   jax.ShapeDtypeStruct((bsz, N_KV, nt * SKT, KA_WIN), BF16),
                   jax.ShapeDtypeStruct((bsz, N_KV, nt * wpt, VROWS, WKT), BF16)],
        compiler_params=_cparams(("parallel", "parallel")),
        name="nsa_relayout",
    )(p, p, p, p)


def _nsa_batch_kernel(sp_ref, q_ref, gt_ref, z_ref, kcmp_ref, vcmpt_ref, ovt_ref, ks_ref, vs_ref, kw_ref,
                      vw_ref, tri_ref, wb_ref, y_ref, sa_scr, sb_scr, g_scr, imp_scr, rank_scr, part_scr, selb_scr):
    g = pl.program_id(0)
    qb = pl.program_id(1)
    n_items = q_ref.shape[0]
    start = qb * QBLK
    nb = kcmp_ref.shape[2]
    n_slc = ovt_ref.shape[0]
    nw = N_HPG * QBLK
    n_top = min(SLC_TOPN, n_slc)
    t_row = start + lax.broadcasted_iota(jnp.int32, (1, QBLK), 1)
    s_bufs = (sa_scr, sb_scr)

    def tile4(a):
        return jnp.concatenate([a] * N_HPG, axis=1)

    def col_reduce8(s, op):
        out = s[0:8]
        for r in range(1, s.shape[0] // 8):
            out = op(out, s[8 * r:8 * r + 8])
        return out

    def col_max(s):
        return jnp.max(col_reduce8(s, jnp.maximum), axis=0, keepdims=True)

    def normalise(acc):
        den = acc[N_HD:N_HD + 1]
        return acc[0:N_HD] / jnp.where(den > 0, den, 1.0)

    def gate_row(i, h, branch):
        return g_scr[i, pl.ds(GATE_NG + (g * N_HPG + h) * 3 + branch, 1), :]

    hl = lax.broadcasted_iota(jnp.int32, (AUX_ROWS, nw), 1) // QBLK
    ar = lax.broadcasted_iota(jnp.int32, (AUX_ROWS, nw), 0)

    def per_head(vals):
        out = jnp.full((AUX_ROWS, nw), vals[N_HPG - 1], F32)
        for h in range(N_HPG - 2, -1, -1):
            out = jnp.where(hl == h, vals[h], out)
        return out

    pieces = [per_head([sp_ref[(g * N_HPG + h) * 3 + j] for h in range(N_HPG)]) for j in range(3)]
    aux = jnp.where(ar == 6, NEG, 0.0)
    for j in range(3):
        aux = jnp.where((ar == j) | (ar == j + 3), pieces[j], aux)
    aux_b = aux.astype(BF16)
    q_tail = jnp.zeros((KA_SEL - N_HD - 2 * AUX_ROWS, nw), BF16)
    sidx = lax.broadcasted_iota(jnp.int32, (n_slc, QBLK), 0)
    cur = t_row // SLC_LEN
    valid = sidx <= cur
    sub8 = lax.broadcasted_iota(jnp.int32, (8, QBLK), 0)
    last_blk = (start + QBLK - 1) // SLC_LEN
    cmp_end = lax.broadcasted_iota(jnp.int32, (nb, 1), 0) * CMP_STRIDE + (CMP_LEN - 1)
    cmp_bias = tile4(jnp.where(cmp_end <= t_row, 0.0, NEG))
    wrows = WIN + QBLK

    def prepare(i):
        qt = (q_ref[i] * (N_HD ** -0.5)).T
        q4l = jnp.concatenate([qt[h * N_HD:(h + 1) * N_HD] for h in range(N_HPG)], axis=1) * LOG2E
        q4s = q4l.astype(BF16)
        q4lo = (q4l - q4s.astype(F32)).astype(BF16)
        g_scr[i] = jax.nn.sigmoid(gt_ref[i]).T

        q_cmp = jnp.concatenate([q4s, q4s, q4lo, aux_b, jnp.zeros((N_HD - AUX_ROWS, nw), BF16)], axis=0)
        s_c = jnp.dot(kcmp_ref[i, 0], q_cmp, preferred_element_type=F32) + cmp_bias
        e_c = jnp.exp2(s_c - col_max(s_c))
        z_c = jnp.sum(col_reduce8(e_c, jnp.add), axis=0, keepdims=True)
        inv_c = jnp.where(tile4(t_row >= CMP_LEN - 1) & (z_c > 0), 1.0 / z_c, 0.0)
        o_c = jnp.dot(vcmpt_ref[i, 0], e_c.astype(BF16), preferred_element_type=F32)[0:N_HD] * inv_c
        p_c = e_c * inv_c
        psum = p_c[:, 0:QBLK]
        for h in range(1, N_HPG):
            psum = psum + p_c[:, h * QBLK:(h + 1) * QBLK]
        parts, rest = [], psum
        for _ in range(3):
            parts.append(rest.astype(BF16))
            rest = rest - parts[-1].astype(F32)
        imp = jnp.dot(ovt_ref[...], jnp.concatenate(parts, axis=0), preferred_element_type=F32)
        imp = jnp.where(valid, imp, -jnp.inf)
        imp_scr[...] = jnp.where((sidx == 0) | (sidx == cur), jnp.inf, imp)

        q_win = jnp.concatenate([q4s, jnp.zeros((AUX_ROWS, nw), BF16), aux_b,
                                 jnp.zeros((KA_WIN - N_HD - 2 * AUX_ROWS, nw), BF16)], axis=0)
        s_w = jnp.dot(kw_ref[i, 0, pl.ds(pl.multiple_of(start, QBLK), wrows), :], q_win,
                      preferred_element_type=F32)
        s_w = jnp.concatenate([s_w[0:WKT] + tile4(wb_ref[0]), s_w[WKT:wrows - WKT],
                               s_w[wrows - WKT:] + tile4(wb_ref[1])], axis=0)
        p_w = jnp.exp2(s_w - col_max(s_w)).astype(BF16)
        v_w = jnp.concatenate([vw_ref[i, 0, qb * (QBLK // WKT) + j] for j in range(wrows // WKT)], axis=1)
        o_w = normalise(jnp.dot(v_w, p_w, preferred_element_type=F32))
        for h in range(N_HPG):
            hs = slice(h * QBLK, (h + 1) * QBLK)
            part_scr[i, :, hs] = gate_row(i, h, 0) * o_c[:, hs] + gate_row(i, h, 2) * o_w[:, hs]

        rank_scr[...] = jnp.zeros_like(rank_scr)
        for ri in range(n_slc // 8):
            @pl.when((last_blk >= n_top) & (8 * ri <= last_blk))
            def _():
                rows = imp_scr[8 * ri:8 * ri + 8]
                for r in range(n_slc // 8):
                    blk8 = imp_scr[8 * r:8 * r + 8]
                    acc = rank_scr[8 * r:8 * r + 8]
                    for ii in range(8):
                        row = rows[ii:ii + 1]
                        if ri < r:
                            before = row >= blk8
                        elif ri > r:
                            before = row > blk8
                        else:
                            before = (row > blk8) | ((row == blk8) & (sub8 > ii))
                        acc = acc + jnp.where(before, 1.0, 0.0)
                    rank_scr[8 * r:8 * r + 8] = acc
        selb_scr[i] = tile4(jnp.where((rank_scr[...] < n_top) & valid, 0.0, NEG))
        return q4s

    q4 = [prepare(i) for i in range(n_items)]

    n_tiles = qb // (SKT // QBLK) + 1
    last = n_tiles - 1

    def sel_scores(i, kt):
        grp = pl.multiple_of((kt * SKT // SLC_LEN) // SEL_GRP * SEL_GRP, SEL_GRP)
        sel_rows = jnp.concatenate([selb_scr[i, pl.ds(grp, SEL_GRP), :], jnp.zeros((AUX_ROWS - SEL_GRP, nw), F32)],
                                   axis=0)
        q_sel = jnp.concatenate([q4[i], sel_rows.astype(BF16), aux_b, q_tail], axis=0)
        krow = pl.multiple_of((kt + PAD_TILES) * SKT, SKT)
        return jnp.dot(ks_ref[i, 0, pl.ds(krow, SKT), :], q_sel, preferred_element_type=F32)

    def tile_rows(kt):
        return pl.ds(pl.multiple_of(kt * SKT, SKT), SKT)

    def pass1(i):
        def fn(kt, mrun):
            s = sel_scores(i, kt)
            s_bufs[i % 2][tile_rows(kt), :] = s
            return jnp.maximum(mrun, col_reduce8(s, jnp.maximum))
        return fn

    def pass2(i, m_sel):
        def fn(kt, acc):
            pr = jnp.exp2(s_bufs[i % 2][tile_rows(kt), :] - m_sel).astype(BF16)
            return acc + jnp.dot(vs_ref[i, 0, kt + PAD_TILES], pr, preferred_element_type=F32)
        return fn

    def both(f1, f2):
        def fn(kt, carry):
            return f1(kt, carry[0]), f2(kt, carry[1])
        return fn

    def run_tiles(fn, n, init):
        def group(size, first):
            def body(j, carry):
                for u in range(size):
                    carry = fn(first + j * size + u, carry)
                return carry
            return body

        n_u = n // UNROLL
        carry = lax.fori_loop(0, n_u, group(UNROLL, 0), init)
        done = n_u * UNROLL
        size = UNROLL // 2
        while size >= 1:
            take = (n // size) % 2
            carry = lax.fori_loop(0, take, group(size, done), carry)
            done = done + take * size
            size //= 2
        return carry

    def diag_tile(i, mrun):
        s = sel_scores(i, last) + tile4(tri_ref[qb % (SKT // QBLK)])
        s_bufs[i % 2][tile_rows(last), :] = s
        return jnp.max(jnp.maximum(mrun, col_reduce8(s, jnp.maximum)), axis=0, keepdims=True)

    mrun0 = jnp.full((8, nw), NEG, F32)
    acc0 = jnp.zeros((VROWS, nw), F32)
    m_prev = diag_tile(0, run_tiles(pass1(0), last, mrun0))
    outs = []
    for i in range(1, n_items):
        p2 = pass2(i - 1, m_prev)
        mrun, acc = run_tiles(both(pass1(i), p2), last, (mrun0, acc0))
        m_cur = diag_tile(i, mrun)
        outs.append(normalise(p2(last, acc)))
        m_prev = m_cur
    outs.append(normalise(run_tiles(pass2(n_items - 1, m_prev), n_tiles, acc0)))

    for i in range(n_items):
        tot = [part_scr[i, :, h * QBLK:(h + 1) * QBLK] + gate_row(i, h, 1) * outs[i][:, h * QBLK:(h + 1) * QBLK]
               for h in range(N_HPG)]
        o = jnp.concatenate(tot, axis=0).T
        y_ref[i] = (o * _silu(z_ref[i])).astype(BF16)


def _edge_biases():
    kl = np.arange(SKT)[:, None]
    ql = np.arange(QBLK)[None, :]
    tri = np.stack([np.where(kl <= par * QBLK + ql, 0.0, NEG) for par in range(SKT // QBLK)])
    kk = np.arange(WKT)[:, None]
    wb = np.stack([np.where(kk > ql, 0.0, NEG), np.where(kk <= ql + WKT - QBLK, 0.0, NEG)])
    return jnp.asarray(tri, F32), jnp.asarray(wb, F32)


def _nsa(p, spieces, kcmp, vcmpt, ovt, ks, vs, kw, vw):
    bsz, t, _ = p.shape
    n_slc = t // SLC_LEN
    gw = N_HPG * N_HD
    tri, wb = _edge_biases()
    nw = N_HPG * QBLK
    per_g = lambda a: pl.BlockSpec((bsz, 1) + a.shape[2:], lambda g, i, sp: (0, g) + (0,) * (a.ndim - 2),
                                   pipeline_mode=pl.Buffered(1))
    const = lambda a: pl.BlockSpec(a.shape, lambda g, i, sp: (0,) * a.ndim)
    grid_spec = pltpu.PrefetchScalarGridSpec(
        num_scalar_prefetch=1,
        grid=(N_KV, t // QBLK),
        in_specs=[pl.BlockSpec((bsz, QBLK, gw), lambda g, i, sp: (0, i, COL_NQ // gw + g)),
                  pl.BlockSpec((bsz, QBLK, LANES), lambda g, i, sp: (0, i, COL_GATES // LANES)),
                  pl.BlockSpec((bsz, QBLK, gw), lambda g, i, sp: (0, i, COL_NZ // gw + g)),
                  per_g(kcmp), per_g(vcmpt), const(ovt),
                  per_g(ks), per_g(vs), per_g(kw), per_g(vw), const(tri), const(wb)],
        out_specs=pl.BlockSpec((bsz, QBLK, gw), lambda g, i, sp: (0, i, g)),
        scratch_shapes=[pltpu.VMEM((t, nw), F32),
                        pltpu.VMEM((t, nw), F32),
                        pltpu.VMEM((bsz, LANES, QBLK), F32),
                        pltpu.VMEM((n_slc, QBLK), F32),
                        pltpu.VMEM((n_slc, QBLK), F32),
                        pltpu.VMEM((bsz, N_HD, nw), F32),
                        pltpu.VMEM((bsz, n_slc, nw), F32)],
    )
    return pl.pallas_call(
        _nsa_batch_kernel,
        grid_spec=grid_spec,
        out_shape=jax.ShapeDtypeStruct((bsz, t, N_WIDTH), BF16),
        compiler_params=_cparams(("arbitrary", "arbitrary")),
        name="nsa_attention",
    )(spieces, p, p, p, kcmp, vcmpt, ovt, ks, vs, kw, vw, tri, wb)


def _outproj_kernel(ym_ref, yn_ref, w_ref, x_ref, gate_ref, fg_ref, o_ref, wb_scr, *, final):
    @pl.when((pl.program_id(0) == 0) & (pl.program_id(1) == 0))
    def _():
        rows = w_ref.shape[1]
        for c in range(rows // WP_TILE):
            cs = slice(c * WP_TILE, (c + 1) * WP_TILE)
            wb_scr[cs, :] = w_ref[0, cs, :].astype(BF16)

    y = jnp.dot(ym_ref[0], wb_scr[0:M_WIDTH, :], preferred_element_type=F32)
    y = y + jnp.dot(yn_ref[0], wb_scr[M_WIDTH:, :], preferred_element_type=F32)
    hres = x_ref[0] + gate_ref[0] * y
    if final:
        ms = jnp.mean(hres * hres, axis=-1, keepdims=True)
        hres = hres * lax.rsqrt(ms + EPS) * fg_ref[...]
    o_ref[0] = hres


def _outproj(ym, yn, w, layer, x, gate, fg, final):
    bsz, t, d = x.shape
    tm = 512
    return pl.pallas_call(
        functools.partial(_outproj_kernel, final=final),
        grid=(bsz, t // tm),
        in_specs=[pl.BlockSpec((1, tm, M_WIDTH), lambda bi, i: (bi, i, 0)),
                  pl.BlockSpec((1, tm, N_WIDTH), lambda bi, i: (bi, i, 0)),
                  pl.BlockSpec((1, M_WIDTH + N_WIDTH, d), lambda bi, i: (layer, 0, 0),
                               pipeline_mode=pl.Buffered(1)),
                  pl.BlockSpec((1, tm, d), lambda bi, i: (bi, i, 0)),
                  pl.BlockSpec((1, 1, d), lambda bi, i: (bi, 0, 0)),
                  pl.BlockSpec((1, d), lambda bi, i: (0, 0))],
        out_specs=pl.BlockSpec((1, tm, d), lambda bi, i: (bi, i, 0)),
        out_shape=jax.ShapeDtypeStruct((bsz, t, d), F32),
        scratch_shapes=[pltpu.VMEM((M_WIDTH + N_WIDTH, d), BF16)],
        compiler_params=_cparams(("arbitrary", "arbitrary")),
        name="outproj_residual",
    )(ym, yn, w, x, gate, fg)


SRC_MI = 4 * M_WIDTH
SRC_NQ = SRC_MI + 2 * M_HEADS
SRC_NG = SRC_NQ + N_WIDTH + 6 * KV_W
SRC_NZ = SRC_NG + 3 * N_HEADS


def _reorder_cols(a):
    parts = [a[..., 0:SRC_MI], a[..., SRC_NQ:SRC_NG], a[..., SRC_NZ:SRC_NZ + N_WIDTH], a[..., SRC_MI:SRC_NQ],
             a[..., SRC_NG:SRC_NZ]]
    used = sum(x.shape[-1] for x in parts)
    parts.append(jnp.zeros(a.shape[:-1] + (NP_PAD - used,), a.dtype))
    return jnp.concatenate(parts, axis=-1)


WP_TILE = 256
WP_GATE_TILE = COL_GATES // WP_TILE


def _wprep_kernel(w_ref, g1_ref, g2_ref, o_ref):
    j = pl.program_id(0)
    d = o_ref.shape[0]

    @pl.when(j < WP_GATE_TILE)
    def _():
        for c in range(d // WP_TILE):
            cs = slice(c * WP_TILE, (c + 1) * WP_TILE)
            o_ref[cs, :] = w_ref[0, :, cs].T.astype(BF16)

    @pl.when(j == WP_GATE_TILE)
    def _():
        n_gate = g1_ref.shape[1] + g2_ref.shape[1]
        gt = jnp.concatenate([g1_ref[0], g2_ref[0], jnp.zeros((LANES - n_gate, d), F32)], axis=0)
        for c in range(d // WP_TILE):
            cs = slice(c * WP_TILE, (c + 1) * WP_TILE)
            o_ref[cs, 0:LANES] = gt[:, cs].T.astype(BF16)
        o_ref[:, LANES:] = jnp.zeros((d, WP_TILE - LANES), BF16)

    @pl.when(j > WP_GATE_TILE)
    def _():
        o_ref[...] = jnp.zeros_like(o_ref)


def _wprep(w_t, layer):
    _, n, d = w_t.shape
    n_big = COL_GATES // WP_TILE

    def src_row(j):
        return jnp.where(j < COL_NQ // WP_TILE, j * WP_TILE,
                         jnp.where(j < COL_NZ // WP_TILE, SRC_NQ + (j - COL_NQ // WP_TILE) * WP_TILE,
                                   jnp.where(j < n_big, SRC_NZ + (j - COL_NZ // WP_TILE) * WP_TILE, 0)))

    el = pl.Element
    return pl.pallas_call(
        _wprep_kernel,
        grid=(NP_PAD // WP_TILE,),
        in_specs=[pl.BlockSpec((el(1), el(WP_TILE), el(d)), lambda j: (layer, pl.multiple_of(src_row(j), 8), 0)),
                  pl.BlockSpec((el(1), el(SRC_NQ - SRC_MI), el(d)), lambda j: (layer, SRC_MI, 0)),
                  pl.BlockSpec((el(1), el(SRC_NZ - SRC_NG), el(d)), lambda j: (layer, SRC_NG, 0))],
        out_specs=pl.BlockSpec((d, WP_TILE), lambda j: (0, j)),
        out_shape=jax.ShapeDtypeStruct((d, NP_PAD), BF16),
        compiler_params=_cparams(("parallel",)),
        name="inproj_weight_prep",
    )(w_t, w_t, w_t)


def _overlap_t(t):
    n_cmp_rows = t // CMP_STRIDE
    n_slc = t // SLC_LEN
    c0 = np.arange(n_cmp_rows) * CMP_STRIDE
    s0 = np.arange(n_slc) * SLC_LEN
    ov = (c0[None, :] <= s0[:, None] + SLC_LEN - 1) & (c0[None, :] + CMP_LEN - 1 >= s0[:, None])
    ov[:, (t - CMP_LEN) // CMP_STRIDE + 1:] = False
    return jnp.asarray(np.concatenate([ov] * 3, axis=1), BF16)


def kernel(x, c, ln_g, w_ada, b_ada, w_in, b_in, m_conv_w, m_conv_b, m_wq, m_wk, m_norm_w, m_skip, m_f_bias,
           n_pos_k, n_pos_v, n_w1_k, n_w2_k, n_w1_v, n_w2_v, w_out, final_g):
    out_dtype = x.dtype
    bsz, t, d = x.shape
    depth = ln_g.shape[0]
    h_res = x.astype(F32)
    assert bsz <= 8
    c_t = jnp.zeros((d, 8), F32).at[:, :bsz].set(c.astype(F32).T)
    slopes_np = np.array([2.0 ** (-8.0 * (h + 1) / N_HEADS) for h in range(N_HEADS)], np.float32)
    rest = (slopes_np.astype(np.float64) * LOG2E).astype(np.float32)
    pieces = []
    for _ in range(3):
        pieces.append(rest.astype(jnp.bfloat16).astype(np.float32))
        rest = rest - pieces[-1]
    spieces = jnp.asarray(np.stack(pieces, axis=1).reshape(-1))
    ovt = _overlap_t(t)

    def w1cat(w1):
        w = w1.reshape(2, CMP_STRIDE, N_HD, CMP_HIDDEN)
        return jnp.concatenate([w[0], w[1]], axis=-1).astype(BF16)

    def w2pad(w2):
        return jnp.pad(w2, ((0, 0), (0, LANES - N_HD))).astype(BF16)

    for l in range(depth):
        mod = _ada(c_t, w_ada, b_ada[l][None, :], l, bsz)[:bsz]
        shift, scale, gate = mod[:, None, 0:d], mod[:, None, d:2 * d], mod[:, None, 2 * d:3 * d]
        p = _inproj(h_res, ln_g[l][None, :], scale, shift,
                    _wprep(jnp.swapaxes(w_in, 1, 2), l), _reorder_cols(b_in[l])[None, :])
        fb_row = jnp.zeros((1, LANES), F32).at[0, M_HEADS:2 * M_HEADS].set(m_f_bias[l])
        y_m = _mlstm(p, m_conv_w[l], m_conv_b[l][None, :], m_wq[l].astype(BF16), m_wk[l].astype(BF16),
                     m_norm_w[l][None, :], m_skip[l][None, :], fb_row)
        kcmp, vcmpt = _compress(p, n_pos_k[l].reshape(1, -1), n_pos_v[l].reshape(1, -1),
                                n_w1_k[l].astype(BF16), w1cat(n_w1_k[l]), w2pad(n_w2_k[l]),
                                n_w1_v[l].astype(BF16), w1cat(n_w1_v[l]), w2pad(n_w2_v[l]))
        ks, vs, kw, vw = _relayout(p)
        y_n = _nsa(p, spieces, kcmp, vcmpt, ovt, ks, vs, kw, vw)
        h_res = _outproj(y_m, y_n, w_out, l, h_res, gate, final_g[None, :], l == depth - 1)
    return h_res.astype(out_dtype)
```

```python
import functools

import numpy as np
import jax
import jax.numpy as jnp
from jax import lax
from jax.experimental import pallas as pl
from jax.experimental.pallas import tpu as pltpu

F32 = jnp.float32
BF16 = jnp.bfloat16

EPS = 1e-6
M_HEADS = 4
M_HD = 256
M_WIDTH = M_HEADS * M_HD
CONV_K = 4
M_CHUNK = 256
N_HEADS = 16
N_HD = 64
N_KV = 4
N_HPG = N_HEADS // N_KV
N_WIDTH = N_HEADS * N_HD
KV_W = N_KV * N_HD
CMP_LEN = 32
CMP_STRIDE = 16
CMP_HIDDEN = 2 * N_HD
SLC_LEN = 64
SLC_TOPN = 16
WIN = 512
QBLK = 256
SKT = 256
WKT = 256
PAD_TILES = WIN // SKT
SEL_GRP = 8
KA_SEL = 128
KA_WIN = 128
VROWS = 80
UNROLL = 8
AUX_ROWS = 16
LOG2E = 1.4426950408889634

COL_MX, COL_MV, COL_MO, COL_MZ = 0, 1024, 2048, 3072
COL_NQ = 4096
COL_KC, COL_VC, COL_KS, COL_VS, COL_KW, COL_VW = 5120, 5376, 5632, 5888, 6144, 6400
COL_NZ = 6656
COL_GATES = 7680
GATE_NG = 2 * M_HEADS
NP_PAD = 8192
NORM_CHUNKS = 4
OUT_CHUNKS = 4
LANES = 128
NEG = -1e30
VMEM_LIMIT = 56 * 1024 * 1024


def _cparams(sem):
    return pltpu.CompilerParams(dimension_semantics=sem, vmem_limit_bytes=VMEM_LIMIT)


def _silu(x):
    return x * jax.nn.sigmoid(x)


def _log_sigmoid(x):
    return jnp.minimum(x, 0.0) - jnp.log1p(jnp.exp(-jnp.abs(x)))


def _ada_kernel(ct_ref, w_ref, b_ref, o_ref, *, bsz):
    s_t = _silu(ct_ref[...])
    w = w_ref[0]
    row = lax.broadcasted_iota(jnp.int32, o_ref.shape, 0)
    out = jnp.zeros(o_ref.shape, F32)
    for b in range(bsz):
        prod = w * s_t[:, b:b + 1]
        acc = prod[0:8]
        for r in range(1, prod.shape[0] // 8):
            acc = acc + prod[8 * r:8 * r + 8]
        out = jnp.where(row == b, jnp.sum(acc, axis=0, keepdims=True) + b_ref[...], out)
    o_ref[...] = out


def _ada(c_t, w, b, layer, bsz):
    _, d, n = w.shape
    tn = 1024
    return pl.pallas_call(
        functools.partial(_ada_kernel, bsz=bsz),
        grid=(n // tn,),
        in_specs=[pl.BlockSpec((d, 8), lambda j: (0, 0)),
                  pl.BlockSpec((1, d, tn), lambda j: (layer, 0, j)),
                  pl.BlockSpec((1, tn), lambda j: (0, j))],
        out_specs=pl.BlockSpec((8, tn), lambda j: (0, j)),
        out_shape=jax.ShapeDtypeStruct((8, n), F32),
        compiler_params=_cparams(("parallel",)),
        name="ada_mod",
    )(c_t, w, b)


def _inproj_kernel(x_ref, g_ref, sc_ref, sh_ref, w_ref, b_ref, o_ref, h_ref):
    first = pl.program_id(2) == 0

    @pl.when(first)
    def _():
        tm = x_ref.shape[1]
        ck = tm // NORM_CHUNKS
        for c in range(NORM_CHUNKS):
            rows = slice(c * ck, (c + 1) * ck)
            x = x_ref[0, rows, :]
            ms = jnp.mean(x * x, axis=-1, keepdims=True)
            h = x * lax.rsqrt(ms + EPS) * g_ref[...]
            h = (h * (1.0 + sc_ref[0]) + sh_ref[0]).astype(BF16)
            h_ref[rows, :] = h
            o_ref[0, rows, :] = jnp.dot(h, w_ref[...], preferred_element_type=F32) + b_ref[...]

    @pl.when(jnp.logical_not(first))
    def _():
        o_ref[0] = jnp.dot(h_ref[...], w_ref[...], preferred_element_type=F32) + b_ref[...]


def _inproj(x, g, scale, shift, w, b):
    bsz, t, d = x.shape
    n = w.shape[1]
    tm, tn = 1024, 1024
    return pl.pallas_call(
        _inproj_kernel,
        grid=(bsz, t // tm, n // tn),
        in_specs=[pl.BlockSpec((1, tm, d), lambda bi, i, j: (bi, i, 0)),
                  pl.BlockSpec((1, d), lambda bi, i, j: (0, 0)),
                  pl.BlockSpec((1, 1, d), lambda bi, i, j: (bi, 0, 0)),
                  pl.BlockSpec((1, 1, d), lambda bi, i, j: (bi, 0, 0)),
                  pl.BlockSpec((d, tn), lambda bi, i, j: (0, j)),
                  pl.BlockSpec((1, tn), lambda bi, i, j: (0, j))],
        out_specs=pl.BlockSpec((1, tm, tn), lambda bi, i, j: (bi, i, j)),
        out_shape=jax.ShapeDtypeStruct((bsz, t, n), F32),
        scratch_shapes=[pltpu.VMEM((tm, d), BF16)],
        compiler_params=_cparams(("parallel", "parallel", "arbitrary")),
        name="norm_inproj",
    )(x, g, scale, shift, w, b)


def _mlstm_kernel(x_ref, v_ref, o_ref, z_ref, gt_ref, cw_ref, cb_ref, wq_ref, wk_ref, nw_ref, sk_ref, fb_ref,
                  y_ref, c_scr, n_scr, m_scr, xp_scr):
    L = M_CHUNK

    @pl.when(pl.program_id(1) == 0)
    def _():
        c_scr[...] = jnp.zeros_like(c_scr)
        n_scr[...] = jnp.zeros_like(n_scr)
        m_scr[...] = jnp.zeros_like(m_scr)
        xp_scr[...] = jnp.zeros_like(xp_scr)

    x = x_ref[0]
    prev = xp_scr[...]
    row8 = lax.broadcasted_iota(jnp.int32, (8, M_WIDTH), 0)
    cw = cw_ref[...]
    xc = cb_ref[...] + x * cw[CONV_K - 1:CONV_K, :]
    for sft in range(1, CONV_K):
        xr = pltpu.roll(x, sft, 0)
        top = jnp.where(row8 < sft, pltpu.roll(prev, sft, 0), xr[0:8])
        xs = jnp.concatenate([top, xr[8:]], axis=0)
        xc = xc + xs * cw[CONV_K - 1 - sft:CONV_K - sft, :]
    xp_scr[...] = x[L - 8:L]
    xc = _silu(xc)

    gt = gt_ref[0]
    col = lax.broadcasted_iota(jnp.int32, (L, LANES), 1)
    logf = _log_sigmoid(gt + fb_ref[...])
    a_c = jnp.where((col >= M_HEADS) & (col < 2 * M_HEADS), logf, gt)
    ri = lax.broadcasted_iota(jnp.int32, (L, L), 0)
    ci = lax.broadcasted_iota(jnp.int32, (L, L), 1)
    causal = ri >= ci
    hp = lax.Precision.HIGHEST
    tri = causal.astype(F32)
    tri_t = (ri <= ci).astype(F32)
    b_c = jnp.dot(tri, a_c, precision=hp, preferred_element_type=F32)
    a_r = a_c.T
    b_r = jnp.dot(a_r[0:8], tri_t, precision=hp, preferred_element_type=F32)

    for h in range(M_HEADS):
        sl = slice(h * M_HD, (h + 1) * M_HD)
        xh = xc[:, sl]
        xb = xh.astype(BF16)
        q = jnp.dot(xb, wq_ref[h], preferred_element_type=F32)
        k = jnp.dot(xb, wk_ref[h], preferred_element_type=F32) * (M_HD ** -0.5)
        vb = v_ref[0, :, sl].astype(BF16)
        qb = q.astype(BF16)
        kb = k.astype(BF16)

        bt = b_c[:, M_HEADS + h:M_HEADS + h + 1]
        ic = a_c[:, h:h + 1]
        bs = b_r[M_HEADS + h:M_HEADS + h + 1, :]
        ir = a_r[h:h + 1, :]
        m_prev = m_scr[h][:, 0:1]

        dm = jnp.where(causal, bt - bs + ir, -jnp.inf)
        inter = bt + m_prev
        m_t = jnp.maximum(inter, jnp.max(dm, axis=-1, keepdims=True))
        w_in = jnp.exp(dm - m_t)
        w_st = jnp.exp(inter - m_t)
        s = lax.dot_general(qb, kb, (((1,), (1,)), ((), ())), preferred_element_type=F32) * w_in
        cmat = c_scr[h]
        nvec = n_scr[h]
        sb = s.astype(BF16)
        num = w_st * jnp.dot(qb, cmat.astype(BF16), preferred_element_type=F32) \
            + jnp.dot(sb, vb, preferred_element_type=F32)
        nt_dims = (((1,), (1,)), ((), ()))
        qn = lax.dot_general(qb, jnp.broadcast_to(nvec, (8, M_HD)).astype(BF16), nt_dims,
                             preferred_element_type=F32)[:, 0:1]
        ssum = lax.dot_general(sb, jnp.ones((8, L), BF16), nt_dims, preferred_element_type=F32)[:, 0:1]
        den = w_st * qn + ssum
        hh = num / jnp.maximum(jnp.abs(den), jnp.exp(-m_t))

        b_last = bt[L - 1:L, :]
        w_end = b_last - bt + ic
        m_new = jnp.maximum(b_last + m_prev, jnp.max(w_end, axis=0, keepdims=True))
        decay = jnp.exp(b_last + m_prev - m_new)
        kwt = k * jnp.exp(w_end - m_new)
        c_scr[h] = decay * cmat + lax.dot_general(kwt.astype(BF16), vb, (((0,), (0,)), ((), ())),
                                                  preferred_element_type=F32)
        n_scr[h] = decay * nvec + jnp.sum(kwt, axis=0, keepdims=True)
        m_scr[h] = jnp.broadcast_to(m_new, (1, LANES))

        mu = jnp.mean(hh, axis=-1, keepdims=True)
        hc = hh - mu
        var = jnp.mean(hc * hc, axis=-1, keepdims=True)
        hn = hc * lax.rsqrt(var + EPS) * nw_ref[:, sl]
        out = jax.nn.sigmoid(o_ref[0, :, sl]) * hn + sk_ref[:, sl] * xh
        y_ref[0, :, sl] = (out * _silu(z_ref[0, :, sl])).astype(BF16)


def _mlstm(p, conv_w, conv_b, wq, wk, norm_w, skip, fb_row):
    bsz, t, _ = p.shape
    L = M_CHUNK
    cb = lambda c: (lambda bi, i: (bi, i, c))
    full2 = lambda bi, i: (0, 0)
    full3 = lambda bi, i: (0, 0, 0)
    return pl.pallas_call(
        _mlstm_kernel,
        grid=(bsz, t // L),
        in_specs=[pl.BlockSpec((1, L, M_WIDTH), cb(COL_MX // M_WIDTH)),
                  pl.BlockSpec((1, L, M_WIDTH), cb(COL_MV // M_WIDTH)),
                  pl.BlockSpec((1, L, M_WIDTH), cb(COL_MO // M_WIDTH)),
                  pl.BlockSpec((1, L, M_WIDTH), cb(COL_MZ // M_WIDTH)),
                  pl.BlockSpec((1, L, LANES), cb(COL_GATES // LANES)),
                  pl.BlockSpec((CONV_K, M_WIDTH), full2),
                  pl.BlockSpec((1, M_WIDTH), full2),
                  pl.BlockSpec((M_HEADS, M_HD, M_HD), full3),
                  pl.BlockSpec((M_HEADS, M_HD, M_HD), full3),
                  pl.BlockSpec((1, M_WIDTH), full2),
                  pl.BlockSpec((1, M_WIDTH), full2),
                  pl.BlockSpec((1, LANES), full2)],
        out_specs=pl.BlockSpec((1, L, M_WIDTH), lambda bi, i: (bi, i, 0)),
        out_shape=jax.ShapeDtypeStruct((bsz, t, M_WIDTH), BF16),
        scratch_shapes=[pltpu.VMEM((M_HEADS, M_HD, M_HD), F32),
                        pltpu.VMEM((M_HEADS, 1, M_HD), F32),
                        pltpu.VMEM((M_HEADS, 1, LANES), F32),
                        pltpu.VMEM((8, M_WIDTH), F32)],
        compiler_params=_cparams(("parallel", "arbitrary")),
        name="mlstm_group",
    )(p, p, p, p, p, conv_w, conv_b, wq, wk, norm_w, skip, fb_row)


def _compress_kernel(kc0_ref, kc1_ref, vc0_ref, vc1_ref, posk_ref, posv_ref, w1k_ref, w1kc_ref, w2k_ref,
                     w1v_ref, w1vc_ref, w2v_ref, kcmp_ref, vcmpt_ref):
    nb = kcmp_ref.shape[2]

    def hidden(src_refs, pos_ref, w1_ref, w1c_ref):
        acc = jnp.zeros((N_KV * nb, 2 * CMP_HIDDEN), F32)
        for l in range(CMP_STRIDE):
            xl = [r[0, pl.ds(l, nb, stride=CMP_STRIDE), :] for r in src_refs]
            xs = jnp.concatenate([x[:, g * N_HD:(g + 1) * N_HD] for x in xl for g in range(LANES // N_HD)], axis=0)
            acc = acc + jnp.dot(xs.astype(BF16), w1c_ref[l], preferred_element_type=F32)
        first = acc[:, :CMP_HIDDEN]
        second = acc[:, CMP_HIDDEN:]
        posb = jnp.dot(jnp.broadcast_to(pos_ref[...], (8, CMP_LEN * N_HD)).astype(BF16), w1_ref[...],
                       preferred_element_type=F32)[0:1]
        hid = first + pltpu.roll(second, N_KV * nb - 1, 0) + posb
        return jax.nn.gelu(hid).astype(BF16)

    hk = hidden((kc0_ref, kc1_ref), posk_ref, w1k_ref, w1kc_ref)
    kc = jnp.dot(hk, w2k_ref[...], preferred_element_type=F32)
    hv = hidden((vc0_ref, vc1_ref), posv_ref, w1v_ref, w1vc_ref)
    vc = jnp.dot(hv, w2v_ref[...], preferred_element_type=F32)
    j = lax.broadcasted_iota(jnp.int32, (nb, 1), 0)
    lane = lax.broadcasted_iota(jnp.int32, (nb, N_HD), 1)
    n_real = (nb * CMP_STRIDE - CMP_LEN) // CMP_STRIDE + 1
    aux = jnp.where(lane < 3, (j * CMP_STRIDE).astype(F32),
                    jnp.where(lane < 6, (CMP_LEN - 1) * 0.5,
                              jnp.where((lane == 6) & (j >= n_real), 1.0, 0.0))).astype(BF16)
    ones_blk = jnp.where(lax.broadcasted_iota(jnp.int32, (VROWS - N_HD, nb), 0) == 0, 1.0, 0.0)
    for g in range(N_KV):
        kg = kc[g * nb:(g + 1) * nb, 0:N_HD]
        k_hi = kg.astype(BF16)
        k_lo = (kg - k_hi.astype(F32)).astype(BF16)
        kcmp_ref[0, g] = jnp.concatenate([k_hi, k_lo, k_hi, aux], axis=1)
        vt = vc[g * nb:(g + 1) * nb].T
        vcmpt_ref[0, g] = jnp.concatenate([vt[0:N_HD], ones_blk], axis=0).astype(BF16)


def _compress(p, posk, posv, w1k, w1kc, w2k, w1v, w1vc, w2v):
    bsz, t, _ = p.shape
    nb = t // CMP_STRIDE
    cb = lambda c: (lambda bi: (bi, 0, c))
    f2 = lambda bi: (0, 0)
    f3 = lambda bi: (0, 0, 0)
    wspecs = [pl.BlockSpec((CMP_LEN * N_HD, CMP_HIDDEN), f2),
              pl.BlockSpec((CMP_STRIDE, N_HD, 2 * CMP_HIDDEN), f3),
              pl.BlockSpec((CMP_HIDDEN, LANES), f2)]
    return pl.pallas_call(
        _compress_kernel,
        grid=(bsz,),
        in_specs=[pl.BlockSpec((1, t, LANES), cb(COL_KC // LANES)),
                  pl.BlockSpec((1, t, LANES), cb(COL_KC // LANES + 1)),
                  pl.BlockSpec((1, t, LANES), cb(COL_VC // LANES)),
                  pl.BlockSpec((1, t, LANES), cb(COL_VC // LANES + 1)),
                  pl.BlockSpec((1, CMP_LEN * N_HD), f2),
                  pl.BlockSpec((1, CMP_LEN * N_HD), f2)] + wspecs + wspecs,
        out_specs=[pl.BlockSpec((1, N_KV, nb, 4 * N_HD), lambda bi: (bi, 0, 0, 0)),
                   pl.BlockSpec((1, N_KV, VROWS, nb), lambda bi: (bi, 0, 0, 0))],
        out_shape=[jax.ShapeDtypeStruct((bsz, N_KV, nb, 4 * N_HD), BF16),
                   jax.ShapeDtypeStruct((bsz, N_KV, VROWS, nb), BF16)],
        compiler_params=_cparams(("parallel",)),
        name="nsa_compress",
    )(p, p, p, p, posk, posv, w1k, w1kc, w2k, w1v, w1vc, w2v)


def _relayout_kernel(ks_ref, vs_ref, kw_ref, vw_ref, ksa_ref, vsa_ref, kwa_ref, vwa_ref):
    i = pl.program_id(1)
    is_pad = i < PAD_TILES
    base = (i - PAD_TILES) * SKT
    row = lax.broadcasted_iota(jnp.int32, (SKT, 1), 0)
    pos = jnp.where(is_pad, 0, base + row)
    blk = pos // SLC_LEN
    p_hi = (blk * SLC_LEN).astype(F32)
    p_lo = (pos - blk * SLC_LEN).astype(F32)
    flag = jnp.where(is_pad, 1.0, 0.0)
    lane = lax.broadcasted_iota(jnp.int32, (SKT, N_HD), 1)
    al = lane - AUX_ROWS
    mid_w = jnp.where((al >= 0) & (al < 3), p_hi,
                      jnp.where((al >= 3) & (al < 6), p_lo, jnp.where(al == 6, flag, 0.0)))
    mid_s = jnp.where(lane == blk % SEL_GRP, 1.0, mid_w)
    ks = ks_ref[0]
    kw = kw_ref[0]
    for g in range(N_KV):
        sl = slice(g * N_HD, (g + 1) * N_HD)
        ksa_ref[0, g] = jnp.concatenate([ks[:, sl], mid_s], axis=1).astype(BF16)
        kwa_ref[0, g] = jnp.concatenate([kw[:, sl], mid_w], axis=1).astype(BF16)
    ones_blk = jnp.where(lax.broadcasted_iota(jnp.int32, (VROWS - N_HD, SKT), 0) == 0, 1.0, 0.0)
    vst = vs_ref[0].T
    vwt = vw_ref[0].T
    for g in range(N_KV):
        sl = slice(g * N_HD, (g + 1) * N_HD)
        vsa_ref[0, g, 0] = jnp.concatenate([vst[sl], ones_blk], axis=0).astype(BF16)
        vwg = jnp.concatenate([vwt[sl], ones_blk], axis=0).astype(BF16)
        for j in range(SKT // WKT):
            vwa_ref[0, g, j] = vwg[:, j * WKT:(j + 1) * WKT]


def _relayout(p):
    bsz, t, _ = p.shape
    nt = t // SKT + PAD_TILES
    wpt = SKT // WKT
    cb = lambda c: (lambda bi, i: (bi, jnp.maximum(i - PAD_TILES, 0), c))
    return pl.pallas_call(
        _relayout_kernel,
        grid=(bsz, nt),
        in_specs=[pl.BlockSpec((1, SKT, KV_W), cb(COL_KS // KV_W)),
                  pl.BlockSpec((1, SKT, KV_W), cb(COL_VS // KV_W)),
                  pl.BlockSpec((1, SKT, KV_W), cb(COL_KW // KV_W)),
                  pl.BlockSpec((1, SKT, KV_W), cb(COL_VW // KV_W))],
        out_specs=[pl.BlockSpec((1, N_KV, SKT, KA_SEL), lambda bi, i: (bi, 0, i, 0)),
                   pl.BlockSpec((1, N_KV, 1, VROWS, SKT), lambda bi, i: (bi, 0, i, 0, 0)),
                   pl.BlockSpec((1, N_KV, SKT, KA_WIN), lambda bi, i: (bi, 0, i, 0)),
                   pl.BlockSpec((1, N_KV, wpt, VROWS, WKT), lambda bi, i: (bi, 0, i, 0, 0))],
        out_shape=[jax.ShapeDtypeStruct((bsz, N_KV, nt * SKT, KA_SEL), BF16),
                   jax.ShapeDtypeStruct((bsz, N_KV, nt, VROWS, SKT), BF16),
                   jax.ShapeDtypeStruct((bsz, N_KV, nt * SKT, KA_WIN), BF16),
                   jax.ShapeDtypeStruct((bsz, N_KV, nt * wpt, VROWS, WKT), BF16)],
        compiler_params=_cparams(("parallel", "parallel")),
        name="nsa_relayout",
    )(p, p, p, p)


def _nsa_batch_kernel(sp_ref, q_ref, gt_ref, z_ref, kcmp_ref, vcmpt_ref, ovt_ref, ks_ref, vs_ref, kw_ref,
                      vw_ref, tri_ref, wb_ref, y_ref, sa_scr, sb_scr, g_scr, imp_scr, rank_scr, part_scr, selb_scr):
    g = pl.program_id(0)
    qb = pl.program_id(1)
    n_items = q_ref.shape[0]
    start = qb * QBLK
    nb = kcmp_ref.shape[2]
    n_slc = ovt_ref.shape[0]
    nw = N_HPG * QBLK
    n_top = min(SLC_TOPN, n_slc)
    t_row = start + lax.broadcasted_iota(jnp.int32, (1, QBLK), 1)
    s_bufs = (sa_scr, sb_scr)

    def tile4(a):
        return jnp.concatenate([a] * N_HPG, axis=1)

    def col_reduce8(s, op):
        out = s[0:8]
        for r in range(1, s.shape[0] // 8):
            out = op(out, s[8 * r:8 * r + 8])
        return out

    def col_max(s):
        return jnp.max(col_reduce8(s, jnp.maximum), axis=0, keepdims=True)

    def normalise(acc):
        den = acc[N_HD:N_HD + 1]
        return acc[0:N_HD] / jnp.where(den > 0, den, 1.0)

    def gate_row(i, h, branch):
        return g_scr[i, pl.ds(GATE_NG + (g * N_HPG + h) * 3 + branch, 1), :]

    hl = lax.broadcasted_iota(jnp.int32, (AUX_ROWS, nw), 1) // QBLK
    ar = lax.broadcasted_iota(jnp.int32, (AUX_ROWS, nw), 0)

    def per_head(vals):
        out = jnp.full((AUX_ROWS, nw), vals[N_HPG - 1], F32)
        for h in range(N_HPG - 2, -1, -1):
            out = jnp.where(hl == h, vals[h], out)
        return out

    pieces = [per_head([sp_ref[(g * N_HPG + h) * 3 + j] for h in range(N_HPG)]) for j in range(3)]
    aux = jnp.where(ar == 6, NEG, 0.0)
    for j in range(3):
        aux = jnp.where((ar == j) | (ar == j + 3), pieces[j], aux)
    aux_b = aux.astype(BF16)
    q_tail = jnp.zeros((KA_SEL - N_HD - 2 * AUX_ROWS, nw), BF16)
    sidx = lax.broadcasted_iota(jnp.int32, (n_slc, QBLK), 0)
    cur = t_row // SLC_LEN
    valid = sidx <= cur
    sub8 = lax.broadcasted_iota(jnp.int32, (8, QBLK), 0)
    last_blk = (start + QBLK - 1) // SLC_LEN
    cmp_end = lax.broadcasted_iota(jnp.int32, (nb, 1), 0) * CMP_STRIDE + (CMP_LEN - 1)
    cmp_bias = tile4(jnp.where(cmp_end <= t_row, 0.0, NEG))
    wrows = WIN + QBLK

    def prepare(i):
        qt = (q_ref[i] * (N_HD ** -0.5)).T
        q4l = jnp.concatenate([qt[h * N_HD:(h + 1) * N_HD] for h in range(N_HPG)], axis=1) * LOG2E
        q4s = q4l.astype(BF16)
        q4lo = (q4l - q4s.astype(F32)).astype(BF16)
        g_scr[i] = jax.nn.sigmoid(gt_ref[i]).T

        q_cmp = jnp.concatenate([q4s, q4s, q4lo, aux_b, jnp.zeros((N_HD - AUX_ROWS, nw), BF16)], axis=0)
        s_c = jnp.dot(kcmp_ref[i, 0], q_cmp, preferred_element_type=F32) + cmp_bias
        e_c = jnp.exp2(s_c - col_max(s_c))
        z_c = jnp.sum(col_reduce8(e_c, jnp.add), axis=0, keepdims=True)
        inv_c = jnp.where(tile4(t_row >= CMP_LEN - 1) & (z_c > 0), 1.0 / z_c, 0.0)
        o_c = jnp.dot(vcmpt_ref[i, 0], e_c.astype(BF16), preferred_element_type=F32)[0:N_HD] * inv_c
        p_c = e_c * inv_c
        psum = p_c[:, 0:QBLK]
        for h in range(1, N_HPG):
            psum = psum + p_c[:, h * QBLK:(h + 1) * QBLK]
        parts, rest = [], psum
        for _ in range(3):
            parts.append(rest.astype(BF16))
            rest = rest - parts[-1].astype(F32)
        imp = jnp.dot(ovt_ref[...], jnp.concatenate(parts, axis=0), preferred_element_type=F32)
        imp = jnp.where(valid, imp, -jnp.inf)
        imp_scr[...] = jnp.where((sidx == 0) | (sidx == cur), jnp.inf, imp)

        q_win = jnp.concatenate([q4s, jnp.zeros((AUX_ROWS, nw), BF16), aux_b,
                                 jnp.zeros((KA_WIN - N_HD - 2 * AUX_ROWS, nw), BF16)], axis=0)
        s_w = jnp.dot(kw_ref[i, 0, pl.ds(pl.multiple_of(start, QBLK), wrows), :], q_win,
                      preferred_element_type=F32)
        s_w = jnp.concatenate([s_w[0:WKT] + tile4(wb_ref[0]), s_w[WKT:wrows - WKT],
                               s_w[wrows - WKT:] + tile4(wb_ref[1])], axis=0)
        p_w = jnp.exp2(s_w - col_max(s_w)).astype(BF16)
        v_w = jnp.concatenate([vw_ref[i, 0, qb * (QBLK // WKT) + j] for j in range(wrows // WKT)], axis=1)
        o_w = normalise(jnp.dot(v_w, p_w, preferred_element_type=F32))
        for h in range(N_HPG):
            hs = slice(h * QBLK, (h + 1) * QBLK)
            part_scr[i, :, hs] = gate_row(i, h, 0) * o_c[:, hs] + gate_row(i, h, 2) * o_w[:, hs]

        rank_scr[...] = jnp.zeros_like(rank_scr)
        for ri in range(n_slc // 8):
            @pl.when((last_blk >= n_top) & (8 * ri <= last_blk))
            def _():
                rows = imp_scr[8 * ri:8 * ri + 8]
                for r in range(n_slc // 8):
                    blk8 = imp_scr[8 * r:8 * r + 8]
                    acc = rank_scr[8 * r:8 * r + 8]
                    for ii in range(8):
                        row = rows[ii:ii + 1]
                        if ri < r:
                            before = row >= blk8
                        elif ri > r:
                            before = row > blk8
                        else:
                            before = (row > blk8) | ((row == blk8) & (sub8 > ii))
                        acc = acc + jnp.where(before, 1.0, 0.0)
                    rank_scr[8 * r:8 * r + 8] = acc
        selb_scr[i] = tile4(jnp.where((rank_scr[...] < n_top) & valid, 0.0, NEG))
        return q4s

    q4 = [prepare(i) for i in range(n_items)]

    n_tiles = qb // (SKT // QBLK) + 1
    last = n_tiles - 1

    def sel_scores(i, kt):
        grp = pl.multiple_of((kt * SKT // SLC_LEN) // SEL_GRP * SEL_GRP, SEL_GRP)
        sel_rows = jnp.concatenate([selb_scr[i, pl.ds(grp, SEL_GRP), :], jnp.zeros((AUX_ROWS - SEL_GRP, nw), F32)],
                                   axis=0)
        q_sel = jnp.concatenate([q4[i], sel_rows.astype(BF16), aux_b, q_tail], axis=0)
        krow = pl.multiple_of((kt + PAD_TILES) * SKT, SKT)
        return jnp.dot(ks_ref[i, 0, pl.ds(krow, SKT), :], q_sel, preferred_element_type=F32)

    def tile_rows(kt):
        return pl.ds(pl.multiple_of(kt * SKT, SKT), SKT)

    def pass1(i):
        def fn(kt, mrun):
            s = sel_scores(i, kt)
            s_bufs[i % 2][tile_rows(kt), :] = s
            return jnp.maximum(mrun, col_reduce8(s, jnp.maximum))
        return fn

    def pass2(i, m_sel):
        def fn(kt, acc):
            pr = jnp.exp2(s_bufs[i % 2][tile_rows(kt), :] - m_sel).astype(BF16)
            return acc + jnp.dot(vs_ref[i, 0, kt + PAD_TILES], pr, preferred_element_type=F32)
        return fn

    def both(f1, f2):
        def fn(kt, carry):
            return f1(kt, carry[0]), f2(kt, carry[1])
        return fn

    def run_tiles(fn, n, init):
        def group(size, first):
            def body(j, carry):
                for u in range(size):
                    carry = fn(first + j * size + u, carry)
                return carry
            return body

        n_u = n // UNROLL
        carry = lax.fori_loop(0, n_u, group(UNROLL, 0), init)
        done = n_u * UNROLL
        size = UNROLL // 2
        while size >= 1:
            take = (n // size) % 2
            carry = lax.fori_loop(0, take, group(size, done), carry)
            done = done + take * size
            size //= 2
        return carry

    def diag_tile(i, mrun):
        s = sel_scores(i, last) + tile4(tri_ref[qb % (SKT // QBLK)])
        s_bufs[i % 2][tile_rows(last), :] = s
        return jnp.max(jnp.maximum(mrun, col_reduce8(s, jnp.maximum)), axis=0, keepdims=True)

    mrun0 = jnp.full((8, nw), NEG, F32)
    acc0 = jnp.zeros((VROWS, nw), F32)
    m_prev = diag_tile(0, run_tiles(pass1(0), last, mrun0))
    outs = []
    for i in range(1, n_items):
        p2 = pass2(i - 1, m_prev)
        mrun, acc = run_tiles(both(pass1(i), p2), last, (mrun0, acc0))
        m_cur = diag_tile(i, mrun)
        outs.append(normalise(p2(last, acc)))
        m_prev = m_cur
    outs.append(normalise(run_tiles(pass2(n_items - 1, m_prev), n_tiles, acc0)))

    for i in range(n_items):
        tot = [part_scr[i, :, h * QBLK:(h + 1) * QBLK] + gate_row(i, h, 1) * outs[i][:, h * QBLK:(h + 1) * QBLK]
               for h in range(N_HPG)]
        o = jnp.concatenate(tot, axis=0).T
        y_ref[i] = (o * _silu(z_ref[i])).astype(BF16)


def _edge_biases():
    kl = np.arange(SKT)[:, None]
    ql = np.arange(QBLK)[None, :]
    tri = np.stack([np.where(kl <= par * QBLK + ql, 0.0, NEG) for par in range(SKT // QBLK)])
    kk = np.arange(WKT)[:, None]
    wb = np.stack([np.where(kk > ql, 0.0, NEG), np.where(kk <= ql + WKT - QBLK, 0.0, NEG)])
    return jnp.asarray(tri, F32), jnp.asarray(wb, F32)


def _nsa(p, spieces, kcmp, vcmpt, ovt, ks, vs, kw, vw):
    bsz, t, _ = p.shape
    n_slc = t // SLC_LEN
    gw = N_HPG * N_HD
    tri, wb = _edge_biases()
    nw = N_HPG * QBLK
    per_g = lambda a: pl.BlockSpec((bsz, 1) + a.shape[2:], lambda g, i, sp: (0, g) + (0,) * (a.ndim - 2),
                                   pipeline_mode=pl.Buffered(1))
    const = lambda a: pl.BlockSpec(a.shape, lambda g, i, sp: (0,) * a.ndim)
    grid_spec = pltpu.PrefetchScalarGridSpec(
        num_scalar_prefetch=1,
        grid=(N_KV, t // QBLK),
        in_specs=[pl.BlockSpec((bsz, QBLK, gw), lambda g, i, sp: (0, i, COL_NQ // gw + g)),
                  pl.BlockSpec((bsz, QBLK, LANES), lambda g, i, sp: (0, i, COL_GATES // LANES)),
                  pl.BlockSpec((bsz, QBLK, gw), lambda g, i, sp: (0, i, COL_NZ // gw + g)),
                  per_g(kcmp), per_g(vcmpt), const(ovt),
                  per_g(ks), per_g(vs), per_g(kw), per_g(vw), const(tri), const(wb)],
        out_specs=pl.BlockSpec((bsz, QBLK, gw), lambda g, i, sp: (0, i, g)),
        scratch_shapes=[pltpu.VMEM((t, nw), F32),
                        pltpu.VMEM((t, nw), F32),
                        pltpu.VMEM((bsz, LANES, QBLK), F32),
                        pltpu.VMEM((n_slc, QBLK), F32),
                        pltpu.VMEM((n_slc, QBLK), F32),
                        pltpu.VMEM((bsz, N_HD, nw), F32),
                        pltpu.VMEM((bsz, n_slc, nw), F32)],
    )
    return pl.pallas_call(
        _nsa_batch_kernel,
        grid_spec=grid_spec,
        out_shape=jax.ShapeDtypeStruct((bsz, t, N_WIDTH), BF16),
        compiler_params=_cparams(("arbitrary", "arbitrary")),
        name="nsa_attention",
    )(spieces, p, p, p, kcmp, vcmpt, ovt, ks, vs, kw, vw, tri, wb)


def _outproj_kernel(ym_ref, yn_ref, w_ref, x_ref, gate_ref, fg_ref, o_ref, wb_scr, *, final):
    @pl.when((pl.program_id(0) == 0) & (pl.program_id(1) == 0))
    def _():
        rows = w_ref.shape[1]
        for c in range(rows // WP_TILE):
            cs = slice(c * WP_TILE, (c + 1) * WP_TILE)
            wb_scr[cs, :] = w_ref[0, cs, :].astype(BF16)

    ck = o_ref.shape[1] // OUT_CHUNKS
    for c in range(OUT_CHUNKS):
        rows = slice(c * ck, (c + 1) * ck)
        y = jnp.dot(ym_ref[0, rows, :], wb_scr[0:M_WIDTH, :], preferred_element_type=F32)
        y = y + jnp.dot(yn_ref[0, rows, :], wb_scr[M_WIDTH:, :], preferred_element_type=F32)
        hres = x_ref[0, rows, :] + gate_ref[0] * y
        if final:
            ms = jnp.mean(hres * hres, axis=-1, keepdims=True)
            hres = hres * lax.rsqrt(ms + EPS) * fg_ref[...]
        o_ref[0, rows, :] = hres


def _outproj(ym, yn, w, layer, x, gate, fg, final):
    bsz, t, d = x.shape
    tm = 512
    return pl.pallas_call(
        functools.partial(_outproj_kernel, final=final),
        grid=(bsz, t // tm),
        in_specs=[pl.BlockSpec((1, tm, M_WIDTH), lambda bi, i: (bi, i, 0)),
                  pl.BlockSpec((1, tm, N_WIDTH), lambda bi, i: (bi, i, 0)),
                  pl.BlockSpec((1, M_WIDTH + N_WIDTH, d), lambda bi, i: (layer, 0, 0),
                               pipeline_mode=pl.Buffered(1)),
                  pl.BlockSpec((1, tm, d), lambda bi, i: (bi, i, 0)),
                  pl.BlockSpec((1, 1, d), lambda bi, i: (bi, 0, 0)),
                  pl.BlockSpec((1, d), lambda bi, i: (0, 0))],
        out_specs=pl.BlockSpec((1, tm, d), lambda bi, i: (bi, i, 0)),
        out_shape=jax.ShapeDtypeStruct((bsz, t, d), F32),
        scratch_shapes=[pltpu.VMEM((M_WIDTH + N_WIDTH, d), BF16)],
        compiler_params=_cparams(("arbitrary", "arbitrary")),
        name="outproj_residual",
    )(ym, yn, w, x, gate, fg)


SRC_MI = 4 * M_WIDTH
SRC_NQ = SRC_MI + 2 * M_HEADS
SRC_NG = SRC_NQ + N_WIDTH + 6 * KV_W
SRC_NZ = SRC_NG + 3 * N_HEADS


def _reorder_cols(a):
    parts = [a[..., 0:SRC_MI], a[..., SRC_NQ:SRC_NG], a[..., SRC_NZ:SRC_NZ + N_WIDTH], a[..., SRC_MI:SRC_NQ],
             a[..., SRC_NG:SRC_NZ]]
    used = sum(x.shape[-1] for x in parts)
    parts.append(jnp.zeros(a.shape[:-1] + (NP_PAD - used,), a.dtype))
    return jnp.concatenate(parts, axis=-1)


WP_TILE = 256
WP_GATE_TILE = COL_GATES // WP_TILE


def _wprep_kernel(w_ref, g1_ref, g2_ref, o_ref):
    j = pl.program_id(0)
    d = o_ref.shape[0]

    @pl.when(j < WP_GATE_TILE)
    def _():
        for c in range(d // WP_TILE):
            cs = slice(c * WP_TILE, (c + 1) * WP_TILE)
            o_ref[cs, :] = w_ref[0, :, cs].T.astype(BF16)

    @pl.when(j == WP_GATE_TILE)
    def _():
        n_gate = g1_ref.shape[1] + g2_ref.shape[1]
        gt = jnp.concatenate([g1_ref[0], g2_ref[0], jnp.zeros((LANES - n_gate, d), F32)], axis=0)
        for c in range(d // WP_TILE):
            cs = slice(c * WP_TILE, (c + 1) * WP_TILE)
            o_ref[cs, 0:LANES] = gt[:, cs].T.astype(BF16)
        o_ref[:, LANES:] = jnp.zeros((d, WP_TILE - LANES), BF16)

    @pl.when(j > WP_GATE_TILE)
    def _():
        o_ref[...] = jnp.zeros_like(o_ref)


def _wprep(w_t, layer):
    _, n, d = w_t.shape
    n_big = COL_GATES // WP_TILE

    def src_row(j):
        return jnp.where(j < COL_NQ // WP_TILE, j * WP_TILE,
                         jnp.where(j < COL_NZ // WP_TILE, SRC_NQ + (j - COL_NQ // WP_TILE) * WP_TILE,
                                   jnp.where(j < n_big, SRC_NZ + (j - COL_NZ // WP_TILE) * WP_TILE, 0)))

    el = pl.Element
    return pl.pallas_call(
        _wprep_kernel,
        grid=(NP_PAD // WP_TILE,),
        in_specs=[pl.BlockSpec((el(1), el(WP_TILE), el(d)), lambda j: (layer, pl.multiple_of(src_row(j), 8), 0)),
                  pl.BlockSpec((el(1), el(SRC_NQ - SRC_MI), el(d)), lambda j: (layer, SRC_MI, 0)),
                  pl.BlockSpec((el(1), el(SRC_NZ - SRC_NG), el(d)), lambda j: (layer, SRC_NG, 0))],
        out_specs=pl.BlockSpec((d, WP_TILE), lambda j: (0, j)),
        out_shape=jax.ShapeDtypeStruct((d, NP_PAD), BF16),
        compiler_params=_cparams(("parallel",)),
        name="inproj_weight_prep",
    )(w_t, w_t, w_t)


def _overlap_t(t):
    n_cmp_rows = t // CMP_STRIDE
    n_slc = t // SLC_LEN
    c0 = np.arange(n_cmp_rows) * CMP_STRIDE
    s0 = np.arange(n_slc) * SLC_LEN
    ov = (c0[None, :] <= s0[:, None] + SLC_LEN - 1) & (c0[None, :] + CMP_LEN - 1 >= s0[:, None])
    ov[:, (t - CMP_LEN) // CMP_STRIDE + 1:] = False
    return jnp.asarray(np.concatenate([ov] * 3, axis=1), BF16)


def kernel(x, c, ln_g, w_ada, b_ada, w_in, b_in, m_conv_w, m_conv_b, m_wq, m_wk, m_norm_w, m_skip, m_f_bias,
           n_pos_k, n_pos_v, n_w1_k, n_w2_k, n_w1_v, n_w2_v, w_out, final_g):
    out_dtype = x.dtype
    bsz, t, d = x.shape
    depth = ln_g.shape[0]
    h_res = x.astype(F32)
    assert bsz <= 8
    c_t = jnp.zeros((d, 8), F32).at[:, :bsz].set(c.astype(F32).T)
    slopes_np = np.array([2.0 ** (-8.0 * (h + 1) / N_HEADS) for h in range(N_HEADS)], np.float32)
    rest = (slopes_np.astype(np.float64) * LOG2E).astype(np.float32)
    pieces = []
    for _ in range(3):
        pieces.append(rest.astype(jnp.bfloat16).astype(np.float32))
        rest = rest - pieces[-1]
    spieces = jnp.asarray(np.stack(pieces, axis=1).reshape(-1))
    ovt = _overlap_t(t)

    def w1cat(w1):
        w = w1.reshape(2, CMP_STRIDE, N_HD, CMP_HIDDEN)
        return jnp.concatenate([w[0], w[1]], axis=-1).astype(BF16)

    def w2pad(w2):
        return jnp.pad(w2, ((0, 0), (0, LANES - N_HD))).astype(BF16)

    for l in range(depth):
        mod = _ada(c_t, w_ada, b_ada[l][None, :], l, bsz)[:bsz]
        shift, scale, gate = mod[:, None, 0:d], mod[:, None, d:2 * d], mod[:, None, 2 * d:3 * d]
        p = _inproj(h_res, ln_g[l][None, :], scale, shift,
                    _wprep(jnp.swapaxes(w_in, 1, 2), l), _reorder_cols(b_in[l])[None, :])
        fb_row = jnp.zeros((1, LANES), F32).at[0, M_HEADS:2 * M_HEADS].set(m_f_bias[l])
        y_m = _mlstm(p, m_conv_w[l], m_conv_b[l][None, :], m_wq[l].astype(BF16), m_wk[l].astype(BF16),
                     m_norm_w[l][None, :], m_skip[l][None, :], fb_row)
        kcmp, vcmpt = _compress(p, n_pos_k[l].reshape(1, -1), n_pos_v[l].reshape(1, -1),
                                n_w1_k[l].astype(BF16), w1cat(n_w1_k[l]), w2pad(n_w2_k[l]),
                                n_w1_v[l].astype(BF16), w1cat(n_w1_v[l]), w2pad(n_w2_v[l]))
        ks, vs, kw, vw = _relayout(p)
        y_n = _nsa(p, spieces, kcmp, vcmpt, ovt, ks, vs, kw, vw)
        h_res = _outproj(y_m, y_n, w_out, l, h_res, gate, final_g[None, :], l == depth - 1)
    return h_res.astype(out_dtype)
```

```python
import functools

import numpy as np
import jax
import jax.numpy as jnp
from jax import lax
from jax.experimental import pallas as pl
from jax.experimental.pallas import tpu as pltpu

F32 = jnp.float32
BF16 = jnp.bfloat16

EPS = 1e-6
M_HEADS = 4
M_HD = 256
M_WIDTH = M_HEADS * M_HD
CONV_K = 4
M_CHUNK = 256
N_HEADS = 16
N_HD = 64
N_KV = 4
N_HPG = N_HEADS // N_KV
N_WIDTH = N_HEADS * N_HD
KV_W = N_KV * N_HD
CMP_LEN = 32
CMP_STRIDE = 16
CMP_HIDDEN = 2 * N_HD
SLC_LEN = 64
SLC_TOPN = 16
WIN = 512
QBLK = 256
SKT = 256
WKT = 256
PAD_TILES = WIN // SKT
SEL_GRP = 8
KA_SEL = 128
KA_WIN = 128
VROWS = 80
UNROLL = 8
AUX_ROWS = 16
LOG2E = 1.4426950408889634

COL_MX, COL_MV, COL_MO, COL_MZ = 0, 1024, 2048, 3072
COL_NQ = 4096
COL_KC, COL_VC, COL_KS, COL_VS, COL_KW, COL_VW = 5120, 5376, 5632, 5888, 6144, 6400
COL_NZ = 6656
COL_GATES = 7680
GATE_NG = 2 * M_HEADS
NP_PAD = 8192
NORM_CHUNKS = 4
OUT_CHUNKS = 1
LANES = 128
NEG = -1e30
VMEM_LIMIT = 56 * 1024 * 1024


def _cparams(sem):
    return pltpu.CompilerParams(dimension_semantics=sem, vmem_limit_bytes=VMEM_LIMIT)


def _silu(x):
    return x * jax.nn.sigmoid(x)


def _log_sigmoid(x):
    return jnp.minimum(x, 0.0) - jnp.log1p(jnp.exp(-jnp.abs(x)))


def _ada_kernel(ct_ref, w_ref, b_ref, o_ref, *, bsz):
    s_t = _silu(ct_ref[...])
    w = w_ref[0]
    row = lax.broadcasted_iota(jnp.int32, o_ref.shape, 0)
    out = jnp.zeros(o_ref.shape, F32)
    for b in range(bsz):
        prod = w * s_t[:, b:b + 1]
        acc = prod[0:8]
        for r in range(1, prod.shape[0] // 8):
            acc = acc + prod[8 * r:8 * r + 8]
        out = jnp.where(row == b, jnp.sum(acc, axis=0, keepdims=True) + b_ref[...], out)
    o_ref[...] = out


def _ada(c_t, w, b, layer, bsz):
    _, d, n = w.shape
    tn = 1024
    return pl.pallas_call(
        functools.partial(_ada_kernel, bsz=bsz),
        grid=(n // tn,),
        in_specs=[pl.BlockSpec((d, 8), lambda j: (0, 0)),
                  pl.BlockSpec((1, d, tn), lambda j: (layer, 0, j)),
                  pl.BlockSpec((1, tn), lambda j: (0, j))],
        out_specs=pl.BlockSpec((8, tn), lambda j: (0, j)),
        out_shape=jax.ShapeDtypeStruct((8, n), F32),
        compiler_params=_cparams(("parallel",)),
        name="ada_mod",
    )(c_t, w, b)


def _inproj_kernel(x_ref, g_ref, sc_ref, sh_ref, w_ref, b_ref, o_ref, h_ref):
    first = pl.program_id(2) == 0

    @pl.when(first)
    def _():
        tm = x_ref.shape[1]
        ck = tm // NORM_CHUNKS
        for c in range(NORM_CHUNKS):
            rows = slice(c * ck, (c + 1) * ck)
            x = x_ref[0, rows, :]
            ms = jnp.mean(x * x, axis=-1, keepdims=True)
            h = x * lax.rsqrt(ms + EPS) * g_ref[...]
            h = (h * (1.0 + sc_ref[0]) + sh_ref[0]).astype(BF16)
            h_ref[rows, :] = h
            o_ref[0, rows, :] = jnp.dot(h, w_ref[...], preferred_element_type=F32) + b_ref[...]

    @pl.when(jnp.logical_not(first))
    def _():
        o_ref[0] = jnp.dot(h_ref[...], w_ref[...], preferred_element_type=F32) + b_ref[...]


def _inproj(x, g, scale, shift, w, b):
    bsz, t, d = x.shape
    n = w.shape[1]
    tm, tn = 1024, 1024
    return pl.pallas_call(
        _inproj_kernel,
        grid=(bsz, t // tm, n // tn),
        in_specs=[pl.BlockSpec((1, tm, d), lambda bi, i, j: (bi, i, 0)),
                  pl.BlockSpec((1, d), lambda bi, i, j: (0, 0)),
                  pl.BlockSpec((1, 1, d), lambda bi, i, j: (bi, 0, 0)),
                  pl.BlockSpec((1, 1, d), lambda bi, i, j: (bi, 0, 0)),
                  pl.BlockSpec((d, tn), lambda bi, i, j: (0, j)),
                  pl.BlockSpec((1, tn), lambda bi, i, j: (0, j))],
        out_specs=pl.BlockSpec((1, tm, tn), lambda bi, i, j: (bi, i, j)),
        out_shape=jax.ShapeDtypeStruct((bsz, t, n), F32),
        scratch_shapes=[pltpu.VMEM((tm, d), BF16)],
        compiler_params=_cparams(("parallel", "parallel", "arbitrary")),
        name="norm_inproj",
    )(x, g, scale, shift, w, b)


def _mlstm_kernel(x_ref, v_ref, o_ref, z_ref, gt_ref, cw_ref, cb_ref, wq_ref, wk_ref, nw_ref, sk_ref, fb_ref,
                  y_ref, c_scr, n_scr, m_scr, xp_scr):
    L = M_CHUNK

    @pl.when(pl.program_id(1) == 0)
    def _():
        c_scr[...] = jnp.zeros_like(c_scr)
        n_scr[...] = jnp.zeros_like(n_scr)
        m_scr[...] = jnp.zeros_like(m_scr)
        xp_scr[...] = jnp.zeros_like(xp_scr)

    x = x_ref[0]
    prev = xp_scr[...]
    row8 = lax.broadcasted_iota(jnp.int32, (8, M_WIDTH), 0)
    cw = cw_ref[...]
    xc = cb_ref[...] + x * cw[CONV_K - 1:CONV_K, :]
    for sft in range(1, CONV_K):
        xr = pltpu.roll(x, sft, 0)
        top = jnp.where(row8 < sft, pltpu.roll(prev, sft, 0), xr[0:8])
        xs = jnp.concatenate([top, xr[8:]], axis=0)
        xc = xc + xs * cw[CONV_K - 1 - sft:CONV_K - sft, :]
    xp_scr[...] = x[L - 8:L]
    xc = _silu(xc)

    gt = gt_ref[0]
    col = lax.broadcasted_iota(jnp.int32, (L, LANES), 1)
    logf = _log_sigmoid(gt + fb_ref[...])
    a_c = jnp.where((col >= M_HEADS) & (col < 2 * M_HEADS), logf, gt)
    ri = lax.broadcasted_iota(jnp.int32, (L, L), 0)
    ci = lax.broadcasted_iota(jnp.int32, (L, L), 1)
    causal = ri >= ci
    hp = lax.Precision.HIGHEST
    tri = causal.astype(F32)
    tri_t = (ri <= ci).astype(F32)
    b_c = jnp.dot(tri, a_c, precision=hp, preferred_element_type=F32)
    a_r = a_c.T
    b_r = jnp.dot(a_r[0:8], tri_t, precision=hp, preferred_element_type=F32)

    for h in range(M_HEADS):
        sl = slice(h * M_HD, (h + 1) * M_HD)
        xh = xc[:, sl]
        xb = xh.astype(BF16)
        q = jnp.dot(xb, wq_ref[h], preferred_element_type=F32)
        k = jnp.dot(xb, wk_ref[h], preferred_element_type=F32) * (M_HD ** -0.5)
        vb = v_ref[0, :, sl].astype(BF16)
        qb = q.astype(BF16)
        kb = k.astype(BF16)

        bt = b_c[:, M_HEADS + h:M_HEADS + h + 1]
        ic = a_c[:, h:h + 1]
        bs = b_r[M_HEADS + h:M_HEADS + h + 1, :]
        ir = a_r[h:h + 1, :]
        m_prev = m_scr[h][:, 0:1]

        dm = jnp.where(causal, bt - bs + ir, -jnp.inf)
        inter = bt + m_prev
        m_t = jnp.maximum(inter, jnp.max(dm, axis=-1, keepdims=True))
        w_in = jnp.exp(dm - m_t)
        w_st = jnp.exp(inter - m_t)
        s = lax.dot_general(qb, kb, (((1,), (1,)), ((), ())), preferred_element_type=F32) * w_in
        cmat = c_scr[h]
        nvec = n_scr[h]
        sb = s.astype(BF16)
        num = w_st * jnp.dot(qb, cmat.astype(BF16), preferred_element_type=F32) \
            + jnp.dot(sb, vb, preferred_element_type=F32)
        nt_dims = (((1,), (1,)), ((), ()))
        qn = lax.dot_general(qb, jnp.broadcast_to(nvec, (8, M_HD)).astype(BF16), nt_dims,
                             preferred_element_type=F32)[:, 0:1]
        ssum = lax.dot_general(sb, jnp.ones((8, L), BF16), nt_dims, preferred_element_type=F32)[:, 0:1]
        den = w_st * qn + ssum
        hh = num / jnp.maximum(jnp.abs(den), jnp.exp(-m_t))

        b_last = bt[L - 1:L, :]
        w_end = b_last - bt + ic
        m_new = jnp.maximum(b_last + m_prev, jnp.max(w_end, axis=0, keepdims=True))
        decay = jnp.exp(b_last + m_prev - m_new)
        kwt = k * jnp.exp(w_end - m_new)
        c_scr[h] = decay * cmat + lax.dot_general(kwt.astype(BF16), vb, (((0,), (0,)), ((), ())),
                                                  preferred_element_type=F32)
        n_scr[h] = decay * nvec + jnp.sum(kwt, axis=0, keepdims=True)
        m_scr[h] = jnp.broadcast_to(m_new, (1, LANES))

        mu = jnp.mean(hh, axis=-1, keepdims=True)
        hc = hh - mu
        var = jnp.mean(hc * hc, axis=-1, keepdims=True)
        hn = hc * lax.rsqrt(var + EPS) * nw_ref[:, sl]
        out = jax.nn.sigmoid(o_ref[0, :, sl]) * hn + sk_ref[:, sl] * xh
        y_ref[0, :, sl] = (out * _silu(z_ref[0, :, sl])).astype(BF16)


def _mlstm(p, conv_w, conv_b, wq, wk, norm_w, skip, fb_row):
    bsz, t, _ = p.shape
    L = M_CHUNK
    cb = lambda c: (lambda bi, i: (bi, i, c))
    full2 = lambda bi, i: (0, 0)
    full3 = lambda bi, i: (0, 0, 0)
    return pl.pallas_call(
        _mlstm_kernel,
        grid=(bsz, t // L),
        in_specs=[pl.BlockSpec((1, L, M_WIDTH), cb(COL_MX // M_WIDTH)),
                  pl.BlockSpec((1, L, M_WIDTH), cb(COL_MV // M_WIDTH)),
                  pl.BlockSpec((1, L, M_WIDTH), cb(COL_MO // M_WIDTH)),
                  pl.BlockSpec((1, L, M_WIDTH), cb(COL_MZ // M_WIDTH)),
                  pl.BlockSpec((1, L, LANES), cb(COL_GATES // LANES)),
                  pl.BlockSpec((CONV_K, M_WIDTH), full2),
                  pl.BlockSpec((1, M_WIDTH), full2),
                  pl.BlockSpec((M_HEADS, M_HD, M_HD), full3),
                  pl.BlockSpec((M_HEADS, M_HD, M_HD), full3),
                  pl.BlockSpec((1, M_WIDTH), full2),
                  pl.BlockSpec((1, M_WIDTH), full2),
                  pl.BlockSpec((1, LANES), full2)],
        out_specs=pl.BlockSpec((1, L, M_WIDTH), lambda bi, i: (bi, i, 0)),
        out_shape=jax.ShapeDtypeStruct((bsz, t, M_WIDTH), BF16),
        scratch_shapes=[pltpu.VMEM((M_HEADS, M_HD, M_HD), F32),
                        pltpu.VMEM((M_HEADS, 1, M_HD), F32),
                        pltpu.VMEM((M_HEADS, 1, LANES), F32),
                        pltpu.VMEM((8, M_WIDTH), F32)],
        compiler_params=_cparams(("parallel", "arbitrary")),
        name="mlstm_group",
    )(p, p, p, p, p, conv_w, conv_b, wq, wk, norm_w, skip, fb_row)


def _compress_kernel(kc0_ref, kc1_ref, vc0_ref, vc1_ref, posk_ref, posv_ref, w1k_ref, w1kc_ref, w2k_ref,
                     w1v_ref, w1vc_ref, w2v_ref, kcmp_ref, vcmpt_ref):
    nb = kcmp_ref.shape[2]

    def hidden(src_refs, pos_ref, w1_ref, w1c_ref):
        halves = [jnp.zeros((nb, 4 * CMP_HIDDEN), F32) for _ in src_refs]
        for l in range(CMP_STRIDE):
            for hf, r in enumerate(src_refs):
                xl = r[0, pl.ds(l, nb, stride=CMP_STRIDE), :]
                halves[hf] = halves[hf] + jnp.dot(xl.astype(BF16), w1c_ref[l], preferred_element_type=F32)
        acc = jnp.concatenate([hv[:, c * 2 * CMP_HIDDEN:(c + 1) * 2 * CMP_HIDDEN] for hv in halves for c in range(2)],
                              axis=0)
        first = acc[:, :CMP_HIDDEN]
        second = acc[:, CMP_HIDDEN:]
        posb = jnp.dot(jnp.broadcast_to(pos_ref[...], (8, CMP_LEN * N_HD)).astype(BF16), w1_ref[...],
                       preferred_element_type=F32)[0:1]
        hid = first + pltpu.roll(second, N_KV * nb - 1, 0) + posb
        return jax.nn.gelu(hid).astype(BF16)

    hk = hidden((kc0_ref, kc1_ref), posk_ref, w1k_ref, w1kc_ref)
    kc = jnp.dot(hk, w2k_ref[...], preferred_element_type=F32)
    hv = hidden((vc0_ref, vc1_ref), posv_ref, w1v_ref, w1vc_ref)
    vc = jnp.dot(hv, w2v_ref[...], preferred_element_type=F32)
    j = lax.broadcasted_iota(jnp.int32, (nb, 1), 0)
    lane = lax.broadcasted_iota(jnp.int32, (nb, N_HD), 1)
    n_real = (nb * CMP_STRIDE - CMP_LEN) // CMP_STRIDE + 1
    aux = jnp.where(lane < 3, (j * CMP_STRIDE).astype(F32),
                    jnp.where(lane < 6, (CMP_LEN - 1) * 0.5,
                              jnp.where((lane == 6) & (j >= n_real), 1.0, 0.0))).astype(BF16)
    ones_blk = jnp.where(lax.broadcasted_iota(jnp.int32, (VROWS - N_HD, nb), 0) == 0, 1.0, 0.0)
    for g in range(N_KV):
        kg = kc[g * nb:(g + 1) * nb, 0:N_HD]
        k_hi = kg.astype(BF16)
        k_lo = (kg - k_hi.astype(F32)).astype(BF16)
        kcmp_ref[0, g] = jnp.concatenate([k_hi, k_lo, k_hi, aux], axis=1)
        vt = vc[g * nb:(g + 1) * nb].T
        vcmpt_ref[0, g] = jnp.concatenate([vt[0:N_HD], ones_blk], axis=0).astype(BF16)


def _compress(p, posk, posv, w1k, w1kc, w2k, w1v, w1vc, w2v):
    bsz, t, _ = p.shape
    nb = t // CMP_STRIDE
    cb = lambda c: (lambda bi: (bi, 0, c))
    f2 = lambda bi: (0, 0)
    f3 = lambda bi: (0, 0, 0)
    wspecs = [pl.BlockSpec((CMP_LEN * N_HD, CMP_HIDDEN), f2),
              pl.BlockSpec((CMP_STRIDE, LANES, 4 * CMP_HIDDEN), f3),
              pl.BlockSpec((CMP_HIDDEN, LANES), f2)]
    return pl.pallas_call(
        _compress_kernel,
        grid=(bsz,),
        in_specs=[pl.BlockSpec((1, t, LANES), cb(COL_KC // LANES)),
                  pl.BlockSpec((1, t, LANES), cb(COL_KC // LANES + 1)),
                  pl.BlockSpec((1, t, LANES), cb(COL_VC // LANES)),
                  pl.BlockSpec((1, t, LANES), cb(COL_VC // LANES + 1)),
                  pl.BlockSpec((1, CMP_LEN * N_HD), f2),
                  pl.BlockSpec((1, CMP_LEN * N_HD), f2)] + wspecs + wspecs,
        out_specs=[pl.BlockSpec((1, N_KV, nb, 4 * N_HD), lambda bi: (bi, 0, 0, 0)),
                   pl.BlockSpec((1, N_KV, VROWS, nb), lambda bi: (bi, 0, 0, 0))],
        out_shape=[jax.ShapeDtypeStruct((bsz, N_KV, nb, 4 * N_HD), BF16),
                   jax.ShapeDtypeStruct((bsz, N_KV, VROWS, nb), BF16)],
        compiler_params=_cparams(("parallel",)),
        name="nsa_compress",
    )(p, p, p, p, posk, posv, w1k, w1kc, w2k, w1v, w1vc, w2v)


def _relayout_kernel(ks_ref, vs_ref, kw_ref, vw_ref, ksa_ref, vsa_ref, kwa_ref, vwa_ref):
    i = pl.program_id(1)
    is_pad = i == 0
    flag = jnp.where(is_pad, 1.0, 0.0)
    row = lax.broadcasted_iota(jnp.int32, (SKT, 1), 0)
    lane = lax.broadcasted_iota(jnp.int32, (SKT, N_HD), 1)
    al = lane - AUX_ROWS
    ones_blk = jnp.where(lax.broadcasted_iota(jnp.int32, (VROWS - N_HD, SKT), 0) == 0, 1.0, 0.0)
    for u in range(PAD_TILES):
        rows = slice(u * SKT, (u + 1) * SKT)
        base = ((i - 1) * PAD_TILES + u) * SKT
        pos = jnp.where(is_pad, 0, base + row)
        blk = pos // SLC_LEN
        p_hi = (blk * SLC_LEN).astype(F32)
        p_lo = (pos - blk * SLC_LEN).astype(F32)
        mid_w = jnp.where((al >= 0) & (al < 3), p_hi,
                          jnp.where((al >= 3) & (al < 6), p_lo, jnp.where(al == 6, flag, 0.0)))
        mid_s = jnp.where(lane == blk % SEL_GRP, 1.0, mid_w)
        ks = ks_ref[0, rows, :]
        kw = kw_ref[0, rows, :]
        for g in range(N_KV):
            sl = slice(g * N_HD, (g + 1) * N_HD)
            ksa_ref[0, g, rows, :] = jnp.concatenate([ks[:, sl], mid_s], axis=1).astype(BF16)
            kwa_ref[0, g, rows, :] = jnp.concatenate([kw[:, sl], mid_w], axis=1).astype(BF16)
        vst = vs_ref[0, rows, :].T
        vwt = vw_ref[0, rows, :].T
        for g in range(N_KV):
            sl = slice(g * N_HD, (g + 1) * N_HD)
            vsa_ref[0, g, u] = jnp.concatenate([vst[sl], ones_blk], axis=0).astype(BF16)
            vwg = jnp.concatenate([vwt[sl], ones_blk], axis=0).astype(BF16)
            for j in range(SKT // WKT):
                vwa_ref[0, g, u * (SKT // WKT) + j] = vwg[:, j * WKT:(j + 1) * WKT]


def _relayout(p):
    bsz, t, _ = p.shape
    nt = t // SKT + PAD_TILES
    wpt = SKT // WKT
    rt = PAD_TILES
    cb = lambda c: (lambda bi, i: (bi, jnp.maximum(i - 1, 0), c))
    return pl.pallas_call(
        _relayout_kernel,
        grid=(bsz, nt // rt),
        in_specs=[pl.BlockSpec((1, rt * SKT, KV_W), cb(COL_KS // KV_W)),
                  pl.BlockSpec((1, rt * SKT, KV_W), cb(COL_VS // KV_W)),
                  pl.BlockSpec((1, rt * SKT, KV_W), cb(COL_KW // KV_W)),
                  pl.BlockSpec((1, rt * SKT, KV_W), cb(COL_VW // KV_W))],
        out_specs=[pl.BlockSpec((1, N_KV, rt * SKT, KA_SEL), lambda bi, i: (bi, 0, i, 0)),
                   pl.BlockSpec((1, N_KV, rt, VROWS, SKT), lambda bi, i: (bi, 0, i, 0, 0)),
                   pl.BlockSpec((1, N_KV, rt * SKT, KA_WIN), lambda bi, i: (bi, 0, i, 0)),
                   pl.BlockSpec((1, N_KV, rt * wpt, VROWS, WKT), lambda bi, i: (bi, 0, i, 0, 0))],
        out_shape=[jax.ShapeDtypeStruct((bsz, N_KV, nt * SKT, KA_SEL), BF16),
                   jax.ShapeDtypeStruct((bsz, N_KV, nt, VROWS, SKT), BF16),
                   jax.ShapeDtypeStruct((bsz, N_KV, nt * SKT, KA_WIN), BF16),
                   jax.ShapeDtypeStruct((bsz, N_KV, nt * wpt, VROWS, WKT), BF16)],
        compiler_params=_cparams(("parallel", "parallel")),
        name="nsa_relayout",
    )(p, p, p, p)


def _nsa_batch_kernel(sp_ref, q_ref, gt_ref, z_ref, kcmp_ref, vcmpt_ref, ovt_ref, ks_ref, vs_ref, kw_ref,
                      vw_ref, tri_ref, wb_ref, y_ref, sa_scr, sb_scr, g_scr, imp_scr, rank_scr, part_scr, selb_scr):
    g = pl.program_id(0)
    qb = pl.program_id(1)
    n_items = q_ref.shape[0]
    start = qb * QBLK
    nb = kcmp_ref.shape[2]
    n_slc = ovt_ref.shape[0]
    nw = N_HPG * QBLK
    n_top = min(SLC_TOPN, n_slc)
    t_row = start + lax.broadcasted_iota(jnp.int32, (1, QBLK), 1)
    s_bufs = (sa_scr, sb_scr)

    def tile4(a):
        return jnp.concatenate([a] * N_HPG, axis=1)

    def col_reduce8(s, op):
        out = s[0:8]
        for r in range(1, s.shape[0] // 8):
            out = op(out, s[8 * r:8 * r + 8])
        return out

    def col_max(s):
        return jnp.max(col_reduce8(s, jnp.maximum), axis=0, keepdims=True)

    def normalise(acc):
        den = acc[N_HD:N_HD + 1]
        return acc[0:N_HD] / jnp.where(den > 0, den, 1.0)

    def gate_row(i, h, branch):
        return g_scr[i, pl.ds(GATE_NG + (g * N_HPG + h) * 3 + branch, 1), :]

    hl = lax.broadcasted_iota(jnp.int32, (AUX_ROWS, nw), 1) // QBLK
    ar = lax.broadcasted_iota(jnp.int32, (AUX_ROWS, nw), 0)

    def per_head(vals):
        out = jnp.full((AUX_ROWS, nw), vals[N_HPG - 1], F32)
        for h in range(N_HPG - 2, -1, -1):
            out = jnp.where(hl == h, vals[h], out)
        return out

    pieces = [per_head([sp_ref[(g * N_HPG + h) * 3 + j] for h in range(N_HPG)]) for j in range(3)]
    aux = jnp.where(ar == 6, NEG, 0.0)
    for j in range(3):
        aux = jnp.where((ar == j) | (ar == j + 3), pieces[j], aux)
    aux_b = aux.astype(BF16)
    q_tail = jnp.zeros((KA_SEL - N_HD - 2 * AUX_ROWS, nw), BF16)
    sidx = lax.broadcasted_iota(jnp.int32, (n_slc, QBLK), 0)
    cur = t_row // SLC_LEN
    valid = sidx <= cur
    sub8 = lax.broadcasted_iota(jnp.int32, (8, QBLK), 0)
    last_blk = (start + QBLK - 1) // SLC_LEN
    cmp_end = lax.broadcasted_iota(jnp.int32, (nb, 1), 0) * CMP_STRIDE + (CMP_LEN - 1)
    cmp_bias = tile4(jnp.where(cmp_end <= t_row, 0.0, NEG))
    wrows = WIN + QBLK

    def prepare(i):
        qt = (q_ref[i] * (N_HD ** -0.5)).T
        q4l = jnp.concatenate([qt[h * N_HD:(h + 1) * N_HD] for h in range(N_HPG)], axis=1) * LOG2E
        q4s = q4l.astype(BF16)
        q4lo = (q4l - q4s.astype(F32)).astype(BF16)
        g_scr[i] = jax.nn.sigmoid(gt_ref[i]).T

        q_cmp = jnp.concatenate([q4s, q4s, q4lo, aux_b, jnp.zeros((N_HD - AUX_ROWS, nw), BF16)], axis=0)
        s_c = jnp.dot(kcmp_ref[i, 0], q_cmp, preferred_element_type=F32) + cmp_bias
        e_c = jnp.exp2(s_c - col_max(s_c))
        z_c = jnp.sum(col_reduce8(e_c, jnp.add), axis=0, keepdims=True)
        inv_c = jnp.where(tile4(t_row >= CMP_LEN - 1) & (z_c > 0), 1.0 / z_c, 0.0)
        o_c = jnp.dot(vcmpt_ref[i, 0], e_c.astype(BF16), preferred_element_type=F32)[0:N_HD] * inv_c
        p_c = e_c * inv_c
        psum = p_c[:, 0:QBLK]
        for h in range(1, N_HPG):
            psum = psum + p_c[:, h * QBLK:(h + 1) * QBLK]
        parts, rest = [], psum
        for _ in range(3):
            parts.append(rest.astype(BF16))
            rest = rest - parts[-1].astype(F32)
        imp = jnp.dot(ovt_ref[...], jnp.concatenate(parts, axis=0), preferred_element_type=F32)
        imp = jnp.where(valid, imp, -jnp.inf)
        imp_scr[...] = jnp.where((sidx == 0) | (sidx == cur), jnp.inf, imp)

        q_win = jnp.concatenate([q4s, jnp.zeros((AUX_ROWS, nw), BF16), aux_b,
                                 jnp.zeros((KA_WIN - N_HD - 2 * AUX_ROWS, nw), BF16)], axis=0)
        s_w = jnp.dot(kw_ref[i, 0, pl.ds(pl.multiple_of(start, QBLK), wrows), :], q_win,
                      preferred_element_type=F32)
        s_w = jnp.concatenate([s_w[0:WKT] + tile4(wb_ref[0]), s_w[WKT:wrows - WKT],
                               s_w[wrows - WKT:] + tile4(wb_ref[1])], axis=0)
        p_w = jnp.exp2(s_w - col_max(s_w)).astype(BF16)
        v_w = jnp.concatenate([vw_ref[i, 0, qb * (QBLK // WKT) + j] for j in range(wrows // WKT)], axis=1)
        o_w = normalise(jnp.dot(v_w, p_w, preferred_element_type=F32))
        for h in range(N_HPG):
            hs = slice(h * QBLK, (h + 1) * QBLK)
            part_scr[i, :, hs] = gate_row(i, h, 0) * o_c[:, hs] + gate_row(i, h, 2) * o_w[:, hs]

        rank_scr[...] = jnp.zeros_like(rank_scr)
        for ri in range(n_slc // 8):
            @pl.when((last_blk >= n_top) & (8 * ri <= last_blk))
            def _():
                rows = imp_scr[8 * ri:8 * ri + 8]
                for r in range(n_slc // 8):
                    blk8 = imp_scr[8 * r:8 * r + 8]
                    acc = rank_scr[8 * r:8 * r + 8]
                    for ii in range(8):
                        row = rows[ii:ii + 1]
                        if ri < r:
                            before = row >= blk8
                        elif ri > r:
                            before = row > blk8
                        else:
                            before = (row > blk8) | ((row == blk8) & (sub8 > ii))
                        acc = acc + jnp.where(before, 1.0, 0.0)
                    rank_scr[8 * r:8 * r + 8] = acc
        selb_scr[i] = tile4(jnp.where((rank_scr[...] < n_top) & valid, 0.0, NEG))
        return q4s

    q4 = [prepare(i) for i in range(n_items)]

    n_tiles = qb // (SKT // QBLK) + 1
    last = n_tiles - 1

    def sel_scores(i, kt):
        grp = pl.multiple_of((kt * SKT // SLC_LEN) // SEL_GRP * SEL_GRP, SEL_GRP)
        sel_rows = jnp.concatenate([selb_scr[i, pl.ds(grp, SEL_GRP), :], jnp.zeros((AUX_ROWS - SEL_GRP, nw), F32)],
                                   axis=0)
        q_sel = jnp.concatenate([q4[i], sel_rows.astype(BF16), aux_b, q_tail], axis=0)
        krow = pl.multiple_of((kt + PAD_TILES) * SKT, SKT)
        return jnp.dot(ks_ref[i, 0, pl.ds(krow, SKT), :], q_sel, preferred_element_type=F32)

    def tile_rows(kt):
        return pl.ds(pl.multiple_of(kt * SKT, SKT), SKT)

    def pass1(i):
        def fn(kt, mrun):
            s = sel_scores(i, kt)
            s_bufs[i % 2][tile_rows(kt), :] = s
            return jnp.maximum(mrun, col_reduce8(s, jnp.maximum))
        return fn

    def pass2(i, m_sel):
        def fn(kt, acc):
            pr = jnp.exp2(s_bufs[i % 2][tile_rows(kt), :] - m_sel).astype(BF16)
            return acc + jnp.dot(vs_ref[i, 0, kt + PAD_TILES], pr, preferred_element_type=F32)
        return fn

    def both(f1, f2):
        def fn(kt, carry):
            return f1(kt, carry[0]), f2(kt, carry[1])
        return fn

    def run_tiles(fn, n, init):
        def group(size, first):
            def body(j, carry):
                for u in range(size):
                    carry = fn(first + j * size + u, carry)
                return carry
            return body

        n_u = n // UNROLL
        carry = lax.fori_loop(0, n_u, group(UNROLL, 0), init)
        done = n_u * UNROLL
        size = UNROLL // 2
        while size >= 1:
            take = (n // size) % 2
            carry = lax.fori_loop(0, take, group(size, done), carry)
            done = done + take * size
            size //= 2
        return carry

    def diag_tile(i, mrun):
        s = sel_scores(i, last) + tile4(tri_ref[qb % (SKT // QBLK)])
        s_bufs[i % 2][tile_rows(last), :] = s
        return jnp.max(jnp.maximum(mrun, col_reduce8(s, jnp.maximum)), axis=0, keepdims=True)

    mrun0 = jnp.full((8, nw), NEG, F32)
    acc0 = jnp.zeros((VROWS, nw), F32)
    m_prev = diag_tile(0, run_tiles(pass1(0), last, mrun0))
    outs = []
    for i in range(1, n_items):
        p2 = pass2(i - 1, m_prev)
        mrun, acc = run_tiles(both(pass1(i), p2), last, (mrun0, acc0))
        m_cur = diag_tile(i, mrun)
        outs.append(normalise(p2(last, acc)))
        m_prev = m_cur
    outs.append(normalise(run_tiles(pass2(n_items - 1, m_prev), n_tiles, acc0)))

    for i in range(n_items):
        tot = [part_scr[i, :, h * QBLK:(h + 1) * QBLK] + gate_row(i, h, 1) * outs[i][:, h * QBLK:(h + 1) * QBLK]
               for h in range(N_HPG)]
        o = jnp.concatenate(tot, axis=0).T
        y_ref[i] = (o * _silu(z_ref[i])).astype(BF16)


def _edge_biases():
    kl = np.arange(SKT)[:, None]
    ql = np.arange(QBLK)[None, :]
    tri = np.stack([np.where(kl <= par * QBLK + ql, 0.0, NEG) for par in range(SKT // QBLK)])
    kk = np.arange(WKT)[:, None]
    wb = np.stack([np.where(kk > ql, 0.0, NEG), np.where(kk <= ql + WKT - QBLK, 0.0, NEG)])
    return jnp.asarray(tri, F32), jnp.asarray(wb, F32)


def _nsa(p, spieces, kcmp, vcmpt, ovt, ks, vs, kw, vw):
    bsz, t, _ = p.shape
    n_slc = t // SLC_LEN
    gw = N_HPG * N_HD
    tri, wb = _edge_biases()
    nw = N_HPG * QBLK
    per_g = lambda a: pl.BlockSpec((bsz, 1) + a.shape[2:], lambda g, i, sp: (0, g) + (0,) * (a.ndim - 2),
                                   pipeline_mode=pl.Buffered(1))
    const = lambda a: pl.BlockSpec(a.shape, lambda g, i, sp: (0,) * a.ndim)
    grid_spec = pltpu.PrefetchScalarGridSpec(
        num_scalar_prefetch=1,
        grid=(N_KV, t // QBLK),
        in_specs=[pl.BlockSpec((bsz, QBLK, gw), lambda g, i, sp: (0, i, COL_NQ // gw + g)),
                  pl.BlockSpec((bsz, QBLK, LANES), lambda g, i, sp: (0, i, COL_GATES // LANES)),
                  pl.BlockSpec((bsz, QBLK, gw), lambda g, i, sp: (0, i, COL_NZ // gw + g)),
                  per_g(kcmp), per_g(vcmpt), const(ovt),
                  per_g(ks), per_g(vs), per_g(kw), per_g(vw), const(tri), const(wb)],
        out_specs=pl.BlockSpec((bsz, QBLK, gw), lambda g, i, sp: (0, i, g)),
        scratch_shapes=[pltpu.VMEM((t, nw), F32),
                        pltpu.VMEM((t, nw), F32),
                        pltpu.VMEM((bsz, LANES, QBLK), F32),
                        pltpu.VMEM((n_slc, QBLK), F32),
                        pltpu.VMEM((n_slc, QBLK), F32),
                        pltpu.VMEM((bsz, N_HD, nw), F32),
                        pltpu.VMEM((bsz, n_slc, nw), F32)],
    )
    return pl.pallas_call(
        _nsa_batch_kernel,
        grid_spec=grid_spec,
        out_shape=jax.ShapeDtypeStruct((bsz, t, N_WIDTH), BF16),
        compiler_params=_cparams(("arbitrary", "arbitrary")),
        name="nsa_attention",
    )(spieces, p, p, p, kcmp, vcmpt, ovt, ks, vs, kw, vw, tri, wb)


def _outproj_kernel(ym_ref, yn_ref, w_ref, x_ref, gate_ref, fg_ref, o_ref, wb_scr, *, final):
    @pl.when((pl.program_id(0) == 0) & (pl.program_id(1) == 0))
    def _():
        rows = w_ref.shape[1]
        for c in range(rows // WP_TILE):
            cs = slice(c * WP_TILE, (c + 1) * WP_TILE)
            wb_scr[cs, :] = w_ref[0, cs, :].astype(BF16)

    ck = o_ref.shape[1] // OUT_CHUNKS
    for c in range(OUT_CHUNKS):
        rows = slice(c * ck, (c + 1) * ck)
        y = jnp.dot(ym_ref[0, rows, :], wb_scr[0:M_WIDTH, :], preferred_element_type=F32)
        y = y + jnp.dot(yn_ref[0, rows, :], wb_scr[M_WIDTH:, :], preferred_element_type=F32)
        hres = x_ref[0, rows, :] + gate_ref[0] * y
        if final:
            ms = jnp.mean(hres * hres, axis=-1, keepdims=True)
            hres = hres * lax.rsqrt(ms + EPS) * fg_ref[...]
        o_ref[0, rows, :] = hres


def _outproj(ym, yn, w, layer, x, gate, fg, final):
    bsz, t, d = x.shape
    tm = 512
    return pl.pallas_call(
        functools.partial(_outproj_kernel, final=final),
        grid=(bsz, t // tm),
        in_specs=[pl.BlockSpec((1, tm, M_WIDTH), lambda bi, i: (bi, i, 0)),
                  pl.BlockSpec((1, tm, N_WIDTH), lambda bi, i: (bi, i, 0)),
                  pl.BlockSpec((1, M_WIDTH + N_WIDTH, d), lambda bi, i: (layer, 0, 0),
                               pipeline_mode=pl.Buffered(1)),
                  pl.BlockSpec((1, tm, d), lambda bi, i: (bi, i, 0)),
                  pl.BlockSpec((1, 1, d), lambda bi, i: (bi, 0, 0)),
                  pl.BlockSpec((1, d), lambda bi, i: (0, 0))],
        out_specs=pl.BlockSpec((1, tm, d), lambda bi, i: (bi, i, 0)),
        out_shape=jax.ShapeDtypeStruct((bsz, t, d), F32),
        scratch_shapes=[pltpu.VMEM((M_WIDTH + N_WIDTH, d), BF16)],
        compiler_params=_cparams(("arbitrary", "arbitrary")),
        name="outproj_residual",
    )(ym, yn, w, x, gate, fg)


SRC_MI = 4 * M_WIDTH
SRC_NQ = SRC_MI + 2 * M_HEADS
SRC_NG = SRC_NQ + N_WIDTH + 6 * KV_W
SRC_NZ = SRC_NG + 3 * N_HEADS


def _reorder_cols(a):
    parts = [a[..., 0:SRC_MI], a[..., SRC_NQ:SRC_NG], a[..., SRC_NZ:SRC_NZ + N_WIDTH], a[..., SRC_MI:SRC_NQ],
             a[..., SRC_NG:SRC_NZ]]
    used = sum(x.shape[-1] for x in parts)
    parts.append(jnp.zeros(a.shape[:-1] + (NP_PAD - used,), a.dtype))
    return jnp.concatenate(parts, axis=-1)


WP_TILE = 256
WP_GATE_TILE = COL_GATES // WP_TILE


def _wprep_kernel(w_ref, g1_ref, g2_ref, o_ref):
    j = pl.program_id(0)
    d = o_ref.shape[0]

    @pl.when(j < WP_GATE_TILE)
    def _():
        for c in range(d // WP_TILE):
            cs = slice(c * WP_TILE, (c + 1) * WP_TILE)
            o_ref[cs, :] = w_ref[0, :, cs].T.astype(BF16)

    @pl.when(j == WP_GATE_TILE)
    def _():
        n_gate = g1_ref.shape[1] + g2_ref.shape[1]
        gt = jnp.concatenate([g1_ref[0], g2_ref[0], jnp.zeros((LANES - n_gate, d), F32)], axis=0)
        for c in range(d // WP_TILE):
            cs = slice(c * WP_TILE, (c + 1) * WP_TILE)
            o_ref[cs, 0:LANES] = gt[:, cs].T.astype(BF16)
        o_ref[:, LANES:] = jnp.zeros((d, WP_TILE - LANES), BF16)

    @pl.when(j > WP_GATE_TILE)
    def _():
        o_ref[...] = jnp.zeros_like(o_ref)


def _wprep(w_t, layer):
    _, n, d = w_t.shape
    n_big = COL_GATES // WP_TILE

    def src_row(j):
        return jnp.where(j < COL_NQ // WP_TILE, j * WP_TILE,
                         jnp.where(j < COL_NZ // WP_TILE, SRC_NQ + (j - COL_NQ // WP_TILE) * WP_TILE,
                                   jnp.where(j < n_big, SRC_NZ + (j - COL_NZ // WP_TILE) * WP_TILE, 0)))

    el = pl.Element
    return pl.pallas_call(
        _wprep_kernel,
        grid=(NP_PAD // WP_TILE,),
        in_specs=[pl.BlockSpec((el(1), el(WP_TILE), el(d)), lambda j: (layer, pl.multiple_of(src_row(j), 8), 0)),
                  pl.BlockSpec((el(1), el(SRC_NQ - SRC_MI), el(d)), lambda j: (layer, SRC_MI, 0)),
                  pl.BlockSpec((el(1), el(SRC_NZ - SRC_NG), el(d)), lambda j: (layer, SRC_NG, 0))],
        out_specs=pl.BlockSpec((d, WP_TILE), lambda j: (0, j)),
        out_shape=jax.ShapeDtypeStruct((d, NP_PAD), BF16),
        compiler_params=_cparams(("parallel",)),
        name="inproj_weight_prep",
    )(w_t, w_t, w_t)


def _overlap_t(t):
    n_cmp_rows = t // CMP_STRIDE
    n_slc = t // SLC_LEN
    c0 = np.arange(n_cmp_rows) * CMP_STRIDE
    s0 = np.arange(n_slc) * SLC_LEN
    ov = (c0[None, :] <= s0[:, None] + SLC_LEN - 1) & (c0[None, :] + CMP_LEN - 1 >= s0[:, None])
    ov[:, (t - CMP_LEN) // CMP_STRIDE + 1:] = False
    return jnp.asarray(np.concatenate([ov] * 3, axis=1), BF16)


def kernel(x, c, ln_g, w_ada, b_ada, w_in, b_in, m_conv_w, m_conv_b, m_wq, m_wk, m_norm_w, m_skip, m_f_bias,
           n_pos_k, n_pos_v, n_w1_k, n_w2_k, n_w1_v, n_w2_v, w_out, final_g):
    out_dtype = x.dtype
    bsz, t, d = x.shape
    depth = ln_g.shape[0]
    h_res = x.astype(F32)
    assert bsz <= 8
    c_t = jnp.zeros((d, 8), F32).at[:, :bsz].set(c.astype(F32).T)
    slopes_np = np.array([2.0 ** (-8.0 * (h + 1) / N_HEADS) for h in range(N_HEADS)], np.float32)
    rest = (slopes_np.astype(np.float64) * LOG2E).astype(np.float32)
    pieces = []
    for _ in range(3):
        pieces.append(rest.astype(jnp.bfloat16).astype(np.float32))
        rest = rest - pieces[-1]
    spieces = jnp.asarray(np.stack(pieces, axis=1).reshape(-1))
    ovt = _overlap_t(t)

    def w1cat(w1):
        w = w1.reshape(2, CMP_STRIDE, N_HD, CMP_HIDDEN)
        w = jnp.concatenate([w[0], w[1]], axis=-1).astype(BF16)
        z = jnp.zeros_like(w)
        return jnp.concatenate([jnp.concatenate([w, z], axis=-1), jnp.concatenate([z, w], axis=-1)], axis=1)

    def w2pad(w2):
        return jnp.pad(w2, ((0, 0), (0, LANES - N_HD))).astype(BF16)

    for l in range(depth):
        mod = _ada(c_t, w_ada, b_ada[l][None, :], l, bsz)[:bsz]
        shift, scale, gate = mod[:, None, 0:d], mod[:, None, d:2 * d], mod[:, None, 2 * d:3 * d]
        p = _inproj(h_res, ln_g[l][None, :], scale, shift,
                    _wprep(jnp.swapaxes(w_in, 1, 2), l), _reorder_cols(b_in[l])[None, :])
        fb_row = jnp.zeros((1, LANES), F32).at[0, M_HEADS:2 * M_HEADS].set(m_f_bias[l])
        y_m = _mlstm(p, m_conv_w[l], m_conv_b[l][None, :], m_wq[l].astype(BF16), m_wk[l].astype(BF16),
                     m_norm_w[l][None, :], m_skip[l][None, :], fb_row)
        kcmp, vcmpt = _compress(p, n_pos_k[l].reshape(1, -1), n_pos_v[l].reshape(1, -1),
                                n_w1_k[l].astype(BF16), w1cat(n_w1_k[l]), w2pad(n_w2_k[l]),
                                n_w1_v[l].astype(BF16), w1cat(n_w1_v[l]), w2pad(n_w2_v[l]))
        ks, vs, kw, vw = _relayout(p)
        y_n = _nsa(p, spieces, kcmp, vcmpt, ovt, ks, vs, kw, vw)
        h_res = _outproj(y_m, y_n, w_out, l, h_res, gate, final_g[None, :], l == depth - 1)
    return h_res.astype(out_dtype)
```

```python
import functools

import numpy as np
import jax
import jax.numpy as jnp
from jax import lax
from jax.experimental import pallas as pl
from jax.experimental.pallas import tpu as pltpu

F32 = jnp.float32
BF16 = jnp.bfloat16

EPS = 1e-6
M_HEADS = 4
M_HD = 256
M_WIDTH = M_HEADS * M_HD
CONV_K = 4
M_CHUNK = 256
N_HEADS = 16
N_HD = 64
N_KV = 4
N_HPG = N_HEADS // N_KV
N_WIDTH = N_HEADS * N_HD
KV_W = N_KV * N_HD
CMP_LEN = 32
CMP_STRIDE = 16
CMP_HIDDEN = 2 * N_HD
SLC_LEN = 64
SLC_TOPN = 16
WIN = 512
QBLK = 256
SKT = 256
WKT = 256
PAD_TILES = WIN // SKT
SEL_GRP = 8
KA_SEL = 128
KA_WIN = 128
VROWS = 80
UNROLL = 8
AUX_ROWS = 16
LOG2E = 1.4426950408889634

COL_MX, COL_MV, COL_MO, COL_MZ = 0, 1024, 2048, 3072
COL_NQ = 4096
COL_KC, COL_VC, COL_KS, COL_VS, COL_KW, COL_VW = 5120, 5376, 5632, 5888, 6144, 6400
COL_NZ = 6656
COL_GATES = 7680
GATE_NG = 2 * M_HEADS
NP_PAD = 8192
NORM_CHUNKS = 4
OUT_CHUNKS = 1
LANES = 128
NEG = -1e30
VMEM_LIMIT = 56 * 1024 * 1024


def _cparams(sem):
    return pltpu.CompilerParams(dimension_semantics=sem, vmem_limit_bytes=VMEM_LIMIT)


def _silu(x):
    return x * jax.nn.sigmoid(x)


def _log_sigmoid(x):
    return jnp.minimum(x, 0.0) - jnp.log1p(jnp.exp(-jnp.abs(x)))


def _ada_kernel(ct_ref, w_ref, b_ref, o_ref, *, bsz):
    s_t = _silu(ct_ref[...])
    w = w_ref[0]
    row = lax.broadcasted_iota(jnp.int32, o_ref.shape, 0)
    out = jnp.zeros(o_ref.shape, F32)
    for b in range(bsz):
        prod = w * s_t[:, b:b + 1]
        acc = prod[0:8]
        for r in range(1, prod.shape[0] // 8):
            acc = acc + prod[8 * r:8 * r + 8]
        out = jnp.where(row == b, jnp.sum(acc, axis=0, keepdims=True) + b_ref[...], out)
    o_ref[...] = out


def _ada(c_t, w, b, layer, bsz):
    _, d, n = w.shape
    tn = 1024
    return pl.pallas_call(
        functools.partial(_ada_kernel, bsz=bsz),
        grid=(n // tn,),
        in_specs=[pl.BlockSpec((d, 8), lambda j: (0, 0)),
                  pl.BlockSpec((1, d, tn), lambda j: (layer, 0, j)),
                  pl.BlockSpec((1, tn), lambda j: (0, j))],
        out_specs=pl.BlockSpec((8, tn), lambda j: (0, j)),
        out_shape=jax.ShapeDtypeStruct((8, n), F32),
        compiler_params=_cparams(("parallel",)),
        name="ada_mod",
    )(c_t, w, b)


def _inproj_kernel(x_ref, g_ref, sc_ref, sh_ref, w_ref, b_ref, o_ref, h_ref):
    first = pl.program_id(2) == 0

    @pl.when(first)
    def _():
        tm = x_ref.shape[1]
        ck = tm // NORM_CHUNKS
        for c in range(NORM_CHUNKS):
            rows = slice(c * ck, (c + 1) * ck)
            x = x_ref[0, rows, :]
            ms = jnp.mean(x * x, axis=-1, keepdims=True)
            h = x * lax.rsqrt(ms + EPS) * g_ref[...]
            h = (h * (1.0 + sc_ref[0]) + sh_ref[0]).astype(BF16)
            h_ref[rows, :] = h
            o_ref[0, rows, :] = jnp.dot(h, w_ref[...], preferred_element_type=F32) + b_ref[...]

    @pl.when(jnp.logical_not(first))
    def _():
        o_ref[0] = jnp.dot(h_ref[...], w_ref[...], preferred_element_type=F32) + b_ref[...]


def _inproj(x, g, scale, shift, w, b):
    bsz, t, d = x.shape
    n = w.shape[1]
    tm, tn = 1024, 1024
    return pl.pallas_call(
        _inproj_kernel,
        grid=(bsz, t // tm, n // tn),
        in_specs=[pl.BlockSpec((1, tm, d), lambda bi, i, j: (bi, i, 0)),
                  pl.BlockSpec((1, d), lambda bi, i, j: (0, 0)),
                  pl.BlockSpec((1, 1, d), lambda bi, i, j: (bi, 0, 0)),
                  pl.BlockSpec((1, 1, d), lambda bi, i, j: (bi, 0, 0)),
                  pl.BlockSpec((d, tn), lambda bi, i, j: (0, j)),
                  pl.BlockSpec((1, tn), lambda bi, i, j: (0, j))],
        out_specs=pl.BlockSpec((1, tm, tn), lambda bi, i, j: (bi, i, j)),
        out_shape=jax.ShapeDtypeStruct((bsz, t, n), F32),
        scratch_shapes=[pltpu.VMEM((tm, d), BF16)],
        compiler_params=_cparams(("parallel", "parallel", "arbitrary")),
        name="norm_inproj",
    )(x, g, scale, shift, w, b)


def _mlstm_kernel(x_ref, v_ref, o_ref, z_ref, gt_ref, cw_ref, cb_ref, wq_ref, wk_ref, nw_ref, sk_ref, fb_ref,
                  y_ref, c_scr, n_scr, m_scr, xp_scr):
    L = M_CHUNK

    @pl.when(pl.program_id(1) == 0)
    def _():
        c_scr[...] = jnp.zeros_like(c_scr)
        n_scr[...] = jnp.zeros_like(n_scr)
        m_scr[...] = jnp.zeros_like(m_scr)
        xp_scr[...] = jnp.zeros_like(xp_scr)

    x = x_ref[0]
    prev = xp_scr[...]
    row8 = lax.broadcasted_iota(jnp.int32, (8, M_WIDTH), 0)
    cw = cw_ref[...]
    xc = cb_ref[...] + x * cw[CONV_K - 1:CONV_K, :]
    for sft in range(1, CONV_K):
        xr = pltpu.roll(x, sft, 0)
        top = jnp.where(row8 < sft, pltpu.roll(prev, sft, 0), xr[0:8])
        xs = jnp.concatenate([top, xr[8:]], axis=0)
        xc = xc + xs * cw[CONV_K - 1 - sft:CONV_K - sft, :]
    xp_scr[...] = x[L - 8:L]
    xc = _silu(xc)

    gt = gt_ref[0]
    col = lax.broadcasted_iota(jnp.int32, (L, LANES), 1)
    logf = _log_sigmoid(gt + fb_ref[...])
    a_c = jnp.where((col >= M_HEADS) & (col < 2 * M_HEADS), logf, gt)
    ri = lax.broadcasted_iota(jnp.int32, (L, L), 0)
    ci = lax.broadcasted_iota(jnp.int32, (L, L), 1)
    causal = ri >= ci
    hp = lax.Precision.HIGHEST
    tri = causal.astype(F32)
    tri_t = (ri <= ci).astype(F32)
    b_c = jnp.dot(tri, a_c, precision=hp, preferred_element_type=F32)
    a_r = a_c.T
    b_r = jnp.dot(a_r[0:8], tri_t, precision=hp, preferred_element_type=F32)

    for h in range(M_HEADS):
        sl = slice(h * M_HD, (h + 1) * M_HD)
        xh = xc[:, sl]
        xb = xh.astype(BF16)
        q = jnp.dot(xb, wq_ref[h], preferred_element_type=F32)
        k = jnp.dot(xb, wk_ref[h], preferred_element_type=F32) * (M_HD ** -0.5)
        vb = v_ref[0, :, sl].astype(BF16)
        qb = q.astype(BF16)
        kb = k.astype(BF16)

        bt = b_c[:, M_HEADS + h:M_HEADS + h + 1]
        ic = a_c[:, h:h + 1]
        bs = b_r[M_HEADS + h:M_HEADS + h + 1, :]
        ir = a_r[h:h + 1, :]
        m_prev = m_scr[h][:, 0:1]

        dm = jnp.where(causal, bt - bs + ir, -jnp.inf)
        inter = bt + m_prev
        m_t = jnp.maximum(inter, jnp.max(dm, axis=-1, keepdims=True))
        w_in = jnp.exp(dm - m_t)
        w_st = jnp.exp(inter - m_t)
        s = lax.dot_general(qb, kb, (((1,), (1,)), ((), ())), preferred_element_type=F32) * w_in
        cmat = c_scr[h]
        nvec = n_scr[h]
        sb = s.astype(BF16)
        num = w_st * jnp.dot(qb, cmat.astype(BF16), preferred_element_type=F32) \
            + jnp.dot(sb, vb, preferred_element_type=F32)
        nt_dims = (((1,), (1,)), ((), ()))
        qn = lax.dot_general(qb, jnp.broadcast_to(nvec, (8, M_HD)).astype(BF16), nt_dims,
                             preferred_element_type=F32)[:, 0:1]
        ssum = lax.dot_general(sb, jnp.ones((8, L), BF16), nt_dims, preferred_element_type=F32)[:, 0:1]
        den = w_st * qn + ssum
        hh = num / jnp.maximum(jnp.abs(den), jnp.exp(-m_t))

        b_last = bt[L - 1:L, :]
        w_end = b_last - bt + ic
        m_new = jnp.maximum(b_last + m_prev, jnp.max(w_end, axis=0, keepdims=True))
        decay = jnp.exp(b_last + m_prev - m_new)
        kwt = k * jnp.exp(w_end - m_new)
        c_scr[h] = decay * cmat + lax.dot_general(kwt.astype(BF16), vb, (((0,), (0,)), ((), ())),
                                                  preferred_element_type=F32)
        n_scr[h] = decay * nvec + jnp.sum(kwt, axis=0, keepdims=True)
        m_scr[h] = jnp.broadcast_to(m_new, (1, LANES))

        mu = jnp.mean(hh, axis=-1, keepdims=True)
        hc = hh - mu
        var = jnp.mean(hc * hc, axis=-1, keepdims=True)
        hn = hc * lax.rsqrt(var + EPS) * nw_ref[:, sl]
        out = jax.nn.sigmoid(o_ref[0, :, sl]) * hn + sk_ref[:, sl] * xh
        y_ref[0, :, sl] = (out * _silu(z_ref[0, :, sl])).astype(BF16)


def _mlstm(p, conv_w, conv_b, wq, wk, norm_w, skip, fb_row):
    bsz, t, _ = p.shape
    L = M_CHUNK
    cb = lambda c: (lambda bi, i: (bi, i, c))
    full2 = lambda bi, i: (0, 0)
    full3 = lambda bi, i: (0, 0, 0)
    return pl.pallas_call(
        _mlstm_kernel,
        grid=(bsz, t // L),
        in_specs=[pl.BlockSpec((1, L, M_WIDTH), cb(COL_MX // M_WIDTH)),
                  pl.BlockSpec((1, L, M_WIDTH), cb(COL_MV // M_WIDTH)),
                  pl.BlockSpec((1, L, M_WIDTH), cb(COL_MO // M_WIDTH)),
                  pl.BlockSpec((1, L, M_WIDTH), cb(COL_MZ // M_WIDTH)),
                  pl.BlockSpec((1, L, LANES), cb(COL_GATES // LANES)),
                  pl.BlockSpec((CONV_K, M_WIDTH), full2),
                  pl.BlockSpec((1, M_WIDTH), full2),
                  pl.BlockSpec((M_HEADS, M_HD, M_HD), full3),
                  pl.BlockSpec((M_HEADS, M_HD, M_HD), full3),
                  pl.BlockSpec((1, M_WIDTH), full2),
                  pl.BlockSpec((1, M_WIDTH), full2),
                  pl.BlockSpec((1, LANES), full2)],
        out_specs=pl.BlockSpec((1, L, M_WIDTH), lambda bi, i: (bi, i, 0)),
        out_shape=jax.ShapeDtypeStruct((bsz, t, M_WIDTH), BF16),
        scratch_shapes=[pltpu.VMEM((M_HEADS, M_HD, M_HD), F32),
                        pltpu.VMEM((M_HEADS, 1, M_HD), F32),
                        pltpu.VMEM((M_HEADS, 1, LANES), F32),
                        pltpu.VMEM((8, M_WIDTH), F32)],
        compiler_params=_cparams(("parallel", "arbitrary")),
        name="mlstm_group",
    )(p, p, p, p, p, conv_w, conv_b, wq, wk, norm_w, skip, fb_row)


def _compress_kernel(kc0_ref, kc1_ref, vc0_ref, vc1_ref, posk_ref, posv_ref, w1k_ref, w1kc_ref, w2k_ref,
                     w1v_ref, w1vc_ref, w2v_ref, kcmp_ref, vcmpt_ref):
    nb = kcmp_ref.shape[2]

    def hidden(src_refs, pos_ref, w1_ref, w1c_ref):
        halves = [jnp.zeros((nb, 4 * CMP_HIDDEN), F32) for _ in src_refs]
        for l in range(CMP_STRIDE):
            for hf, r in enumerate(src_refs):
                xl = r[0, pl.ds(l, nb, stride=CMP_STRIDE), :]
                halves[hf] = halves[hf] + jnp.dot(xl.astype(BF16), w1c_ref[l], preferred_element_type=F32)
        acc = jnp.concatenate([hv[:, c * 2 * CMP_HIDDEN:(c + 1) * 2 * CMP_HIDDEN] for hv in halves for c in range(2)],
                              axis=0)
        first = acc[:, :CMP_HIDDEN]
        second = acc[:, CMP_HIDDEN:]
        posb = jnp.dot(jnp.broadcast_to(pos_ref[...], (8, CMP_LEN * N_HD)).astype(BF16), w1_ref[...],
                       preferred_element_type=F32)[0:1]
        hid = first + pltpu.roll(second, N_KV * nb - 1, 0) + posb
        return jax.nn.gelu(hid).astype(BF16)

    hk = hidden((kc0_ref, kc1_ref), posk_ref, w1k_ref, w1kc_ref)
    kc = jnp.dot(hk, w2k_ref[...], preferred_element_type=F32)
    hv = hidden((vc0_ref, vc1_ref), posv_ref, w1v_ref, w1vc_ref)
    vc = jnp.dot(hv, w2v_ref[...], preferred_element_type=F32)
    j = lax.broadcasted_iota(jnp.int32, (nb, 1), 0)
    lane = lax.broadcasted_iota(jnp.int32, (nb, N_HD), 1)
    n_real = (nb * CMP_STRIDE - CMP_LEN) // CMP_STRIDE + 1
    aux = jnp.where(lane < 3, (j * CMP_STRIDE).astype(F32),
                    jnp.where(lane < 6, (CMP_LEN - 1) * 0.5,
                              jnp.where((lane == 6) & (j >= n_real), 1.0, 0.0))).astype(BF16)
    ones_blk = jnp.where(lax.broadcasted_iota(jnp.int32, (VROWS - N_HD, nb), 0) == 0, 1.0, 0.0)
    for g in range(N_KV):
        kg = kc[g * nb:(g + 1) * nb, 0:N_HD]
        k_hi = kg.astype(BF16)
        k_lo = (kg - k_hi.astype(F32)).astype(BF16)
        kcmp_ref[0, g] = jnp.concatenate([k_hi, k_lo, k_hi, aux], axis=1)
        vt = vc[g * nb:(g + 1) * nb].T
        vcmpt_ref[0, g] = jnp.concatenate([vt[0:N_HD], ones_blk], axis=0).astype(BF16)


def _compress(p, posk, posv, w1k, w1kc, w2k, w1v, w1vc, w2v):
    bsz, t, _ = p.shape
    nb = t // CMP_STRIDE
    cb = lambda c: (lambda bi: (bi, 0, c))
    f2 = lambda bi: (0, 0)
    f3 = lambda bi: (0, 0, 0)
    wspecs = [pl.BlockSpec((CMP_LEN * N_HD, CMP_HIDDEN), f2),
              pl.BlockSpec((CMP_STRIDE, LANES, 4 * CMP_HIDDEN), f3),
              pl.BlockSpec((CMP_HIDDEN, LANES), f2)]
    return pl.pallas_call(
        _compress_kernel,
        grid=(bsz,),
        in_specs=[pl.BlockSpec((1, t, LANES), cb(COL_KC // LANES)),
                  pl.BlockSpec((1, t, LANES), cb(COL_KC // LANES + 1)),
                  pl.BlockSpec((1, t, LANES), cb(COL_VC // LANES)),
                  pl.BlockSpec((1, t, LANES), cb(COL_VC // LANES + 1)),
                  pl.BlockSpec((1, CMP_LEN * N_HD), f2),
                  pl.BlockSpec((1, CMP_LEN * N_HD), f2)] + wspecs + wspecs,
        out_specs=[pl.BlockSpec((1, N_KV, nb, 4 * N_HD), lambda bi: (bi, 0, 0, 0)),
                   pl.BlockSpec((1, N_KV, VROWS, nb), lambda bi: (bi, 0, 0, 0))],
        out_shape=[jax.ShapeDtypeStruct((bsz, N_KV, nb, 4 * N_HD), BF16),
                   jax.ShapeDtypeStruct((bsz, N_KV, VROWS, nb), BF16)],
        compiler_params=_cparams(("parallel",)),
        name="nsa_compress",
    )(p, p, p, p, posk, posv, w1k, w1kc, w2k, w1v, w1vc, w2v)


def _relayout_kernel(ks_ref, vs_ref, kw_ref, vw_ref, ksa_ref, vsa_ref, kwa_ref, vwa_ref):
    i = pl.program_id(1)
    is_pad = i == 0
    flag = jnp.where(is_pad, 1.0, 0.0)
    row = lax.broadcasted_iota(jnp.int32, (SKT, 1), 0)
    lane = lax.broadcasted_iota(jnp.int32, (SKT, N_HD), 1)
    al = lane - AUX_ROWS
    ones_blk = jnp.where(lax.broadcasted_iota(jnp.int32, (VROWS - N_HD, SKT), 0) == 0, 1.0, 0.0)
    for u in range(PAD_TILES):
        rows = slice(u * SKT, (u + 1) * SKT)
        base = ((i - 1) * PAD_TILES + u) * SKT
        pos = jnp.where(is_pad, 0, base + row)
        blk = pos // SLC_LEN
        p_hi = (blk * SLC_LEN).astype(F32)
        p_lo = (pos - blk * SLC_LEN).astype(F32)
        mid_w = jnp.where((al >= 0) & (al < 3), p_hi,
                          jnp.where((al >= 3) & (al < 6), p_lo, jnp.where(al == 6, flag, 0.0)))
        mid_s = jnp.where(lane == blk % SEL_GRP, 1.0, mid_w)
        ks = ks_ref[0, rows, :]
        kw = kw_ref[0, rows, :]
        for g in range(N_KV):
            sl = slice(g * N_HD, (g + 1) * N_HD)
            ksa_ref[0, g, rows, :] = jnp.concatenate([ks[:, sl], mid_s], axis=1).astype(BF16)
            kwa_ref[0, g, rows, :] = jnp.concatenate([kw[:, sl], mid_w], axis=1).astype(BF16)
        vst = vs_ref[0, rows, :].T
        vwt = vw_ref[0, rows, :].T
        for g in range(N_KV):
            sl = slice(g * N_HD, (g + 1) * N_HD)
            vsa_ref[0, g, u] = jnp.concatenate([vst[sl], ones_blk], axis=0).astype(BF16)
            vwg = jnp.concatenate([vwt[sl], ones_blk], axis=0).astype(BF16)
            for j in range(SKT // WKT):
                vwa_ref[0, g, u * (SKT // WKT) + j] = vwg[:, j * WKT:(j + 1) * WKT]


def _relayout(p):
    bsz, t, _ = p.shape
    nt = t // SKT + PAD_TILES
    wpt = SKT // WKT
    rt = PAD_TILES
    cb = lambda c: (lambda bi, i: (bi, jnp.maximum(i - 1, 0), c))
    return pl.pallas_call(
        _relayout_kernel,
        grid=(bsz, nt // rt),
        in_specs=[pl.BlockSpec((1, rt * SKT, KV_W), cb(COL_KS // KV_W)),
                  pl.BlockSpec((1, rt * SKT, KV_W), cb(COL_VS // KV_W)),
                  pl.BlockSpec((1, rt * SKT, KV_W), cb(COL_KW // KV_W)),
                  pl.BlockSpec((1, rt * SKT, KV_W), cb(COL_VW // KV_W))],
        out_specs=[pl.BlockSpec((1, N_KV, rt * SKT, KA_SEL), lambda bi, i: (bi, 0, i, 0)),
                   pl.BlockSpec((1, N_KV, rt, VROWS, SKT), lambda bi, i: (bi, 0, i, 0, 0)),
                   pl.BlockSpec((1, N_KV, rt * SKT, KA_WIN), lambda bi, i: (bi, 0, i, 0)),
                   pl.BlockSpec((1, N_KV, rt * wpt, VROWS, WKT), lambda bi, i: (bi, 0, i, 0, 0))],
        out_shape=[jax.ShapeDtypeStruct((bsz, N_KV, nt * SKT, KA_SEL), BF16),
                   jax.ShapeDtypeStruct((bsz, N_KV, nt, VROWS, SKT), BF16),
                   jax.ShapeDtypeStruct((bsz, N_KV, nt * SKT, KA_WIN), BF16),
                   jax.ShapeDtypeStruct((bsz, N_KV, nt * wpt, VROWS, WKT), BF16)],
        compiler_params=_cparams(("parallel", "parallel")),
        name="nsa_relayout",
    )(p, p, p, p)


def _nsa_batch_kernel(sp_ref, q_ref, gt_ref, z_ref, kcmp_ref, vcmpt_ref, ovt_ref, ks_ref, vs_ref, kw_ref,
                      vw_ref, tri_ref, wb_ref, y_ref, sa_scr, sb_scr, g_scr, imp_scr, rank_scr, part_scr, selb_scr):
    g = pl.program_id(0)
    qb = pl.program_id(1)
    n_items = q_ref.shape[0]
    start = qb * QBLK
    nb = kcmp_ref.shape[2]
    n_slc = ovt_ref.shape[0]
    nw = N_HPG * QBLK
    n_top = min(SLC_TOPN, n_slc)
    t_row = start + lax.broadcasted_iota(jnp.int32, (1, QBLK), 1)
    s_bufs = (sa_scr, sb_scr)

    def tile4(a):
        return jnp.concatenate([a] * N_HPG, axis=1)

    def col_reduce8(s, op):
        out = s[0:8]
        for r in range(1, s.shape[0] // 8):
            out = op(out, s[8 * r:8 * r + 8])
        return out

    def col_max(s):
        return jnp.max(col_reduce8(s, jnp.maximum), axis=0, keepdims=True)

    def normalise(acc):
        den = acc[N_HD:N_HD + 1]
        return acc[0:N_HD] / jnp.where(den > 0, den, 1.0)

    def gate_row(i, h, branch):
        return g_scr[i, pl.ds(GATE_NG + (g * N_HPG + h) * 3 + branch, 1), :]

    hl = lax.broadcasted_iota(jnp.int32, (AUX_ROWS, nw), 1) // QBLK
    ar = lax.broadcasted_iota(jnp.int32, (AUX_ROWS, nw), 0)

    def per_head(vals):
        out = jnp.full((AUX_ROWS, nw), vals[N_HPG - 1], F32)
        for h in range(N_HPG - 2, -1, -1):
            out = jnp.where(hl == h, vals[h], out)
        return out

    pieces = [per_head([sp_ref[(g * N_HPG + h) * 3 + j] for h in range(N_HPG)]) for j in range(3)]
    aux = jnp.where(ar == 6, NEG, 0.0)
    for j in range(3):
        aux = jnp.where((ar == j) | (ar == j + 3), pieces[j], aux)
    aux_b = aux.astype(BF16)
    q_tail = jnp.zeros((KA_SEL - N_HD - 2 * AUX_ROWS, nw), BF16)
    sidx = lax.broadcasted_iota(jnp.int32, (n_slc, QBLK), 0)
    cur = t_row // SLC_LEN
    valid = sidx <= cur
    sub8 = lax.broadcasted_iota(jnp.int32, (8, QBLK), 0)
    last_blk = (start + QBLK - 1) // SLC_LEN
    cmp_end = lax.broadcasted_iota(jnp.int32, (nb, 1), 0) * CMP_STRIDE + (CMP_LEN - 1)
    cmp_bias = tile4(jnp.where(cmp_end <= t_row, 0.0, NEG))
    wrows = WIN + QBLK

    def prepare(i):
        qt = (q_ref[i] * (N_HD ** -0.5)).T
        q4l = jnp.concatenate([qt[h * N_HD:(h + 1) * N_HD] for h in range(N_HPG)], axis=1) * LOG2E
        q4s = q4l.astype(BF16)
        q4lo = (q4l - q4s.astype(F32)).astype(BF16)
        g_scr[i] = jax.nn.sigmoid(gt_ref[i]).T

        q_cmp = jnp.concatenate([q4s, q4s, q4lo, aux_b, jnp.zeros((N_HD - AUX_ROWS, nw), BF16)], axis=0)
        s_c = jnp.dot(kcmp_ref[i, 0], q_cmp, preferred_element_type=F32) + cmp_bias
        e_c = jnp.exp2(s_c - col_max(s_c))
        z_c = jnp.sum(col_reduce8(e_c, jnp.add), axis=0, keepdims=True)
        inv_c = jnp.where(tile4(t_row >= CMP_LEN - 1) & (z_c > 0), 1.0 / z_c, 0.0)
        o_c = jnp.dot(vcmpt_ref[i, 0], e_c.astype(BF16), preferred_element_type=F32)[0:N_HD] * inv_c
        p_c = e_c * inv_c
        psum = p_c[:, 0:QBLK]
        for h in range(1, N_HPG):
            psum = psum + p_c[:, h * QBLK:(h + 1) * QBLK]
        parts, rest = [], psum
        for _ in range(3):
            parts.append(rest.astype(BF16))
            rest = rest - parts[-1].astype(F32)
        imp = jnp.dot(ovt_ref[...], jnp.concatenate(parts, axis=0), preferred_element_type=F32)
        imp = jnp.where(valid, imp, -jnp.inf)
        imp_scr[...] = jnp.where((sidx == 0) | (sidx == cur), jnp.inf, imp)

        q_win = jnp.concatenate([q4s, jnp.zeros((AUX_ROWS, nw), BF16), aux_b,
                                 jnp.zeros((KA_WIN - N_HD - 2 * AUX_ROWS, nw), BF16)], axis=0)
        s_w = jnp.dot(kw_ref[i, 0, pl.ds(pl.multiple_of(start, QBLK), wrows), :], q_win,
                      preferred_element_type=F32)
        s_w = jnp.concatenate([s_w[0:WKT] + tile4(wb_ref[0]), s_w[WKT:wrows - WKT],
                               s_w[wrows - WKT:] + tile4(wb_ref[1])], axis=0)
        p_w = jnp.exp2(s_w - col_max(s_w)).astype(BF16)
        v_w = jnp.concatenate([vw_ref[i, 0, qb * (QBLK // WKT) + j] for j in range(wrows // WKT)], axis=1)
        o_w = normalise(jnp.dot(v_w, p_w, preferred_element_type=F32))
        for h in range(N_HPG):
            hs = slice(h * QBLK, (h + 1) * QBLK)
            part_scr[i, :, hs] = gate_row(i, h, 0) * o_c[:, hs] + gate_row(i, h, 2) * o_w[:, hs]

        rank_scr[...] = jnp.zeros_like(rank_scr)
        for ri in range(n_slc // 8):
            @pl.when((last_blk >= n_top) & (8 * ri <= last_blk))
            def _():
                rows = imp_scr[8 * ri:8 * ri + 8]
                for r in range(n_slc // 8):
                    blk8 = imp_scr[8 * r:8 * r + 8]
                    acc = rank_scr[8 * r:8 * r + 8]
                    for ii in range(8):
                        row = rows[ii:ii + 1]
                        if ri < r:
                            before = row >= blk8
                        elif ri > r:
                            before = row > blk8
                        else:
                            before = (row > blk8) | ((row == blk8) & (sub8 > ii))
                        acc = acc + jnp.where(before, 1.0, 0.0)
                    rank_scr[8 * r:8 * r + 8] = acc
        selb_scr[i] = tile4(jnp.where((rank_scr[...] < n_top) & valid, 0.0, NEG))
        return q4s

    q4 = [prepare(i) for i in range(n_items)]

    n_tiles = qb // (SKT // QBLK) + 1
    last = n_tiles - 1

    def sel_scores(i, kt):
        grp = pl.multiple_of((kt * SKT // SLC_LEN) // SEL_GRP * SEL_GRP, SEL_GRP)
        sel_rows = jnp.concatenate([selb_scr[i, pl.ds(grp, SEL_GRP), :], jnp.zeros((AUX_ROWS - SEL_GRP, nw), F32)],
                                   axis=0)
        q_sel = jnp.concatenate([q4[i], sel_rows.astype(BF16), aux_b, q_tail], axis=0)
        krow = pl.multiple_of((kt + PAD_TILES) * SKT, SKT)
        return jnp.dot(ks_ref[i, 0, pl.ds(krow, SKT), :], q_sel, preferred_element_type=F32)

    def tile_rows(kt):
        return pl.ds(pl.multiple_of(kt * SKT, SKT), SKT)

    def pass1(i):
        def fn(kt, mrun):
            s = sel_scores(i, kt)
            s_bufs[i % 2][tile_rows(kt), :] = s
            return jnp.maximum(mrun, col_reduce8(s, jnp.maximum))
        return fn

    def pass2(i, m_sel):
        def fn(kt, acc):
            pr = jnp.exp2(s_bufs[i % 2][tile_rows(kt), :] - m_sel).astype(BF16)
            return acc + jnp.dot(vs_ref[i, 0, kt + PAD_TILES], pr, preferred_element_type=F32)
        return fn

    def both(f1, f2):
        def fn(kt, carry):
            return f1(kt, carry[0]), f2(kt, carry[1])
        return fn

    def run_tiles(fn, n, init):
        def group(size, first):
            def body(j, carry):
                for u in range(size):
                    carry = fn(first + j * size + u, carry)
                return carry
            return body

        n_u = n // UNROLL
        carry = lax.fori_loop(0, n_u, group(UNROLL, 0), init)
        done = n_u * UNROLL
        size = UNROLL // 2
        while size >= 1:
            take = (n // size) % 2
            carry = lax.fori_loop(0, take, group(size, done), carry)
            done = done + take * size
            size //= 2
        return carry

    def diag_tile(i, mrun):
        s = sel_scores(i, last) + tile4(tri_ref[qb % (SKT // QBLK)])
        s_bufs[i % 2][tile_rows(last), :] = s
        return jnp.max(jnp.maximum(mrun, col_reduce8(s, jnp.maximum)), axis=0, keepdims=True)

    mrun0 = jnp.full((8, nw), NEG, F32)
    acc0 = jnp.zeros((VROWS, nw), F32)
    m_prev = diag_tile(0, run_tiles(pass1(0), last, mrun0))
    outs = []
    for i in range(1, n_items):
        p2 = pass2(i - 1, m_prev)
        mrun, acc = run_tiles(both(pass1(i), p2), last, (mrun0, acc0))
        m_cur = diag_tile(i, mrun)
        outs.append(normalise(p2(last, acc)))
        m_prev = m_cur
    outs.append(normalise(run_tiles(pass2(n_items - 1, m_prev), n_tiles, acc0)))

    for i in range(n_items):
        tot = [part_scr[i, :, h * QBLK:(h + 1) * QBLK] + gate_row(i, h, 1) * outs[i][:, h * QBLK:(h + 1) * QBLK]
               for h in range(N_HPG)]
        o = jnp.concatenate(tot, axis=0).T
        y_ref[i] = (o * _silu(z_ref[i])).astype(BF16)


def _edge_biases():
    kl = np.arange(SKT)[:, None]
    ql = np.arange(QBLK)[None, :]
    tri = np.stack([np.where(kl <= par * QBLK + ql, 0.0, NEG) for par in range(SKT // QBLK)])
    kk = np.arange(WKT)[:, None]
    wb = np.stack([np.where(kk > ql, 0.0, NEG), np.where(kk <= ql + WKT - QBLK, 0.0, NEG)])
    return jnp.asarray(tri, F32), jnp.asarray(wb, F32)


def _nsa(p, spieces, kcmp, vcmpt, ovt, ks, vs, kw, vw):
    bsz, t, _ = p.shape
    n_slc = t // SLC_LEN
    gw = N_HPG * N_HD
    tri, wb = _edge_biases()
    nw = N_HPG * QBLK
    per_g = lambda a: pl.BlockSpec((bsz, 1) + a.shape[2:], lambda g, i, sp: (0, g) + (0,) * (a.ndim - 2),
                                   pipeline_mode=pl.Buffered(1))
    const = lambda a: pl.BlockSpec(a.shape, lambda g, i, sp: (0,) * a.ndim)
    grid_spec = pltpu.PrefetchScalarGridSpec(
        num_scalar_prefetch=1,
        grid=(N_KV, t // QBLK),
        in_specs=[pl.BlockSpec((bsz, QBLK, gw), lambda g, i, sp: (0, i, COL_NQ // gw + g)),
                  pl.BlockSpec((bsz, QBLK, LANES), lambda g, i, sp: (0, i, COL_GATES // LANES)),
                  pl.BlockSpec((bsz, QBLK, gw), lambda g, i, sp: (0, i, COL_NZ // gw + g)),
                  per_g(kcmp), per_g(vcmpt), const(ovt),
                  per_g(ks), per_g(vs), per_g(kw), per_g(vw), const(tri), const(wb)],
        out_specs=pl.BlockSpec((bsz, QBLK, gw), lambda g, i, sp: (0, i, g)),
        scratch_shapes=[pltpu.VMEM((t, nw), F32),
                        pltpu.VMEM((t, nw), F32),
                        pltpu.VMEM((bsz, LANES, QBLK), F32),
                        pltpu.VMEM((n_slc, QBLK), F32),
                        pltpu.VMEM((n_slc, QBLK), F32),
                        pltpu.VMEM((bsz, N_HD, nw), F32),
                        pltpu.VMEM((bsz, n_slc, nw), F32)],
    )
    return pl.pallas_call(
        _nsa_batch_kernel,
        grid_spec=grid_spec,
        out_shape=jax.ShapeDtypeStruct((bsz, t, N_WIDTH), BF16),
        compiler_params=_cparams(("arbitrary", "arbitrary")),
        name="nsa_attention",
    )(spieces, p, p, p, kcmp, vcmpt, ovt, ks, vs, kw, vw, tri, wb)


def _outproj_kernel(ym_ref, yn_ref, w_ref, x_ref, gate_ref, fg_ref, o_ref, wb_scr, *, final):
    @pl.when((pl.program_id(0) == 0) & (pl.program_id(1) == 0))
    def _():
        rows = w_ref.shape[1]
        for c in range(rows // WP_TILE):
            cs = slice(c * WP_TILE, (c + 1) * WP_TILE)
            wb_scr[cs, :] = w_ref[0, cs, :].astype(BF16)

    ck = o_ref.shape[1] // OUT_CHUNKS
    for c in range(OUT_CHUNKS):
        rows = slice(c * ck, (c + 1) * ck)
        y = jnp.dot(ym_ref[0, rows, :], wb_scr[0:M_WIDTH, :], preferred_element_type=F32)
        y = y + jnp.dot(yn_ref[0, rows, :], wb_scr[M_WIDTH:, :], preferred_element_type=F32)
        hres = x_ref[0, rows, :] + gate_ref[0] * y
        if final:
            ms = jnp.mean(hres * hres, axis=-1, keepdims=True)
            hres = hres * lax.rsqrt(ms + EPS) * fg_ref[...]
        o_ref[0, rows, :] = hres


def _outproj(ym, yn, w, layer, x, gate, fg, final):
    bsz, t, d = x.shape
    tm = 512
    return pl.pallas_call(
        functools.partial(_outproj_kernel, final=final),
        grid=(bsz, t // tm),
        in_specs=[pl.BlockSpec((1, tm, M_WIDTH), lambda bi, i: (bi, i, 0)),
                  pl.BlockSpec((1, tm, N_WIDTH), lambda bi, i: (bi, i, 0)),
                  pl.BlockSpec((1, M_WIDTH + N_WIDTH, d), lambda bi, i: (layer, 0, 0),
                               pipeline_mode=pl.Buffered(1)),
                  pl.BlockSpec((1, tm, d), lambda bi, i: (bi, i, 0)),
                  pl.BlockSpec((1, 1, d), lambda bi, i: (bi, 0, 0)),
                  pl.BlockSpec((1, d), lambda bi, i: (0, 0))],
        out_specs=pl.BlockSpec((1, tm, d), lambda bi, i: (bi, i, 0)),
        out_shape=jax.ShapeDtypeStruct((bsz, t, d), F32),
        scratch_shapes=[pltpu.VMEM((M_WIDTH + N_WIDTH, d), BF16)],
        compiler_params=_cparams(("arbitrary", "arbitrary")),
        name="outproj_residual",
    )(ym, yn, w, x, gate, fg)


SRC_MI = 4 * M_WIDTH
SRC_NQ = SRC_MI + 2 * M_HEADS
SRC_NG = SRC_NQ + N_WIDTH + 6 * KV_W
SRC_NZ = SRC_NG + 3 * N_HEADS


def _reorder_cols(a):
    parts = [a[..., 0:SRC_MI], a[..., SRC_NQ:SRC_NG], a[..., SRC_NZ:SRC_NZ + N_WIDTH], a[..., SRC_MI:SRC_NQ],
             a[..., SRC_NG:SRC_NZ]]
    used = sum(x.shape[-1] for x in parts)
    parts.append(jnp.zeros(a.shape[:-1] + (NP_PAD - used,), a.dtype))
    return jnp.concatenate(parts, axis=-1)


WP_TILE = 512
WP_GATE_TILE = COL_GATES // WP_TILE


def _wprep_kernel(w_ref, g1_ref, g2_ref, o_ref):
    j = pl.program_id(0)
    d = o_ref.shape[0]

    @pl.when(j < WP_GATE_TILE)
    def _():
        for c in range(d // WP_TILE):
            cs = slice(c * WP_TILE, (c + 1) * WP_TILE)
            o_ref[cs, :] = w_ref[0, :, cs].T.astype(BF16)

    @pl.when(j == WP_GATE_TILE)
    def _():
        n_gate = g1_ref.shape[1] + g2_ref.shape[1]
        gt = jnp.concatenate([g1_ref[0], g2_ref[0], jnp.zeros((LANES - n_gate, d), F32)], axis=0)
        for c in range(d // WP_TILE):
            cs = slice(c * WP_TILE, (c + 1) * WP_TILE)
            o_ref[cs, 0:LANES] = gt[:, cs].T.astype(BF16)
        o_ref[:, LANES:] = jnp.zeros((d, WP_TILE - LANES), BF16)

    @pl.when(j > WP_GATE_TILE)
    def _():
        o_ref[...] = jnp.zeros_like(o_ref)


def _wprep(w_t, layer):
    _, n, d = w_t.shape
    n_big = COL_GATES // WP_TILE

    def src_row(j):
        return jnp.where(j < COL_NQ // WP_TILE, j * WP_TILE,
                         jnp.where(j < COL_NZ // WP_TILE, SRC_NQ + (j - COL_NQ // WP_TILE) * WP_TILE,
                                   jnp.where(j < n_big, SRC_NZ + (j - COL_NZ // WP_TILE) * WP_TILE, 0)))

    el = pl.Element
    return pl.pallas_call(
        _wprep_kernel,
        grid=(NP_PAD // WP_TILE,),
        in_specs=[pl.BlockSpec((el(1), el(WP_TILE), el(d)), lambda j: (layer, pl.multiple_of(src_row(j), 8), 0)),
                  pl.BlockSpec((el(1), el(SRC_NQ - SRC_MI), el(d)), lambda j: (layer, SRC_MI, 0)),
                  pl.BlockSpec((el(1), el(SRC_NZ - SRC_NG), el(d)), lambda j: (layer, SRC_NG, 0))],
        out_specs=pl.BlockSpec((d, WP_TILE), lambda j: (0, j)),
        out_shape=jax.ShapeDtypeStruct((d, NP_PAD), BF16),
        compiler_params=_cparams(("parallel",)),
        name="inproj_weight_prep",
    )(w_t, w_t, w_t)


def _overlap_t(t):
    n_cmp_rows = t // CMP_STRIDE
    n_slc = t // SLC_LEN
    c0 = np.arange(n_cmp_rows) * CMP_STRIDE
    s0 = np.arange(n_slc) * SLC_LEN
    ov = (c0[None, :] <= s0[:, None] + SLC_LEN - 1) & (c0[None, :] + CMP_LEN - 1 >= s0[:, None])
    ov[:, (t - CMP_LEN) // CMP_STRIDE + 1:] = False
    return jnp.asarray(np.concatenate([ov] * 3, axis=1), BF16)


def kernel(x, c, ln_g, w_ada, b_ada, w_in, b_in, m_conv_w, m_conv_b, m_wq, m_wk, m_norm_w, m_skip, m_f_bias,
           n_pos_k, n_pos_v, n_w1_k, n_w2_k, n_w1_v, n_w2_v, w_out, final_g):
    out_dtype = x.dtype
    bsz, t, d = x.shape
    depth = ln_g.shape[0]
    h_res = x.astype(F32)
    assert bsz <= 8
    c_t = jnp.zeros((d, 8), F32).at[:, :bsz].set(c.astype(F32).T)
    slopes_np = np.array([2.0 ** (-8.0 * (h + 1) / N_HEADS) for h in range(N_HEADS)], np.float32)
    rest = (slopes_np.astype(np.float64) * LOG2E).astype(np.float32)
    pieces = []
    for _ in range(3):
        pieces.append(rest.astype(jnp.bfloat16).astype(np.float32))
        rest = rest - pieces[-1]
    spieces = jnp.asarray(np.stack(pieces, axis=1).reshape(-1))
    ovt = _overlap_t(t)

    def w1cat(w1):
        w = w1.reshape(2, CMP_STRIDE, N_HD, CMP_HIDDEN)
        w = jnp.concatenate([w[0], w[1]], axis=-1).astype(BF16)
        z = jnp.zeros_like(w)
        return jnp.concatenate([jnp.concatenate([w, z], axis=-1), jnp.concatenate([z, w], axis=-1)], axis=1)

    def w2pad(w2):
        return jnp.pad(w2, ((0, 0), (0, LANES - N_HD))).astype(BF16)

    for l in range(depth):
        mod = _ada(c_t, w_ada, b_ada[l][None, :], l, bsz)[:bsz]
        shift, scale, gate = mod[:, None, 0:d], mod[:, None, d:2 * d], mod[:, None, 2 * d:3 * d]
        p = _inproj(h_res, ln_g[l][None, :], scale, shift,
                    _wprep(jnp.swapaxes(w_in, 1, 2), l), _reorder_cols(b_in[l])[None, :])
        fb_row = jnp.zeros((1, LANES), F32).at[0, M_HEADS:2 * M_HEADS].set(m_f_bias[l])
        y_m = _mlstm(p, m_conv_w[l], m_conv_b[l][None, :], m_wq[l].astype(BF16), m_wk[l].astype(BF16),
                     m_norm_w[l][None, :], m_skip[l][None, :], fb_row)
        kcmp, vcmpt = _compress(p, n_pos_k[l].reshape(1, -1), n_pos_v[l].reshape(1, -1),
                                n_w1_k[l].astype(BF16), w1cat(n_w1_k[l]), w2pad(n_w2_k[l]),
                                n_w1_v[l].astype(BF16), w1cat(n_w1_v[l]), w2pad(n_w2_v[l]))
        ks, vs, kw, vw = _relayout(p)
        y_n = _nsa(p, spieces, kcmp, vcmpt, ovt, ks, vs, kw, vw)
        h_res = _outproj(y_m, y_n, w_out, l, h_res, gate, final_g[None, :], l == depth - 1)
    return h_res.astype(out_dtype)
```

```python
import functools

import numpy as np
import jax
import jax.numpy as jnp
from jax import lax
from jax.experimental import pallas as pl
from jax.experimental.pallas import tpu as pltpu

F32 = jnp.float32
BF16 = jnp.bfloat16

EPS = 1e-6
M_HEADS = 4
M_HD = 256
M_WIDTH = M_HEADS * M_HD
CONV_K = 4
M_CHUNK = 256
N_HEADS = 16
N_HD = 64
N_KV = 4
N_HPG = N_HEADS // N_KV
N_WIDTH = N_HEADS * N_HD
KV_W = N_KV * N_HD
CMP_LEN = 32
CMP_STRIDE = 16
CMP_HIDDEN = 2 * N_HD
SLC_LEN = 64
SLC_TOPN = 16
WIN = 512
QBLK = 256
SKT = 256
WKT = 256
PAD_TILES = WIN // SKT
SEL_GRP = 8
KA_SEL = 128
KA_WIN = 128
VROWS = 80
UNROLL = 8
AUX_ROWS = 16
LOG2E = 1.4426950408889634

COL_MX, COL_MV, COL_MO, COL_MZ = 0, 1024, 2048, 3072
COL_NQ = 4096
COL_KC, COL_VC, COL_KS, COL_VS, COL_KW, COL_VW = 5120, 5376, 5632, 5888, 6144, 6400
COL_NZ = 6656
COL_GATES = 7680
GATE_NG = 2 * M_HEADS
NP_PAD = 8192
NORM_CHUNKS = 4
LANES = 128
NEG = -1e30
VMEM_LIMIT = 56 * 1024 * 1024


def _cparams(sem):
    return pltpu.CompilerParams(dimension_semantics=sem, vmem_limit_bytes=VMEM_LIMIT)


def _silu(x):
    return x * jax.nn.sigmoid(x)


def _log_sigmoid(x):
    return jnp.minimum(x, 0.0) - jnp.log1p(jnp.exp(-jnp.abs(x)))


def _ada_kernel(ct_ref, w_ref, b_ref, o_ref, *, bsz):
    s_t = _silu(ct_ref[...])
    w = w_ref[0]
    row = lax.broadcasted_iota(jnp.int32, o_ref.shape, 0)
    out = jnp.zeros(o_ref.shape, F32)
    for b in range(bsz):
        prod = w * s_t[:, b:b + 1]
        acc = prod[0:8]
        for r in range(1, prod.shape[0] // 8):
            acc = acc + prod[8 * r:8 * r + 8]
        out = jnp.where(row == b, jnp.sum(acc, axis=0, keepdims=True) + b_ref[...], out)
    o_ref[...] = out


def _ada(c_t, w, b, layer, bsz):
    _, d, n = w.shape
    tn = 1024
    return pl.pallas_call(
        functools.partial(_ada_kernel, bsz=bsz),
        grid=(n // tn,),
        in_specs=[pl.BlockSpec((d, 8), lambda j: (0, 0)),
                  pl.BlockSpec((1, d, tn), lambda j: (layer, 0, j)),
                  pl.BlockSpec((1, tn), lambda j: (0, j))],
        out_specs=pl.BlockSpec((8, tn), lambda j: (0, j)),
        out_shape=jax.ShapeDtypeStruct((8, n), F32),
        compiler_params=_cparams(("parallel",)),
        name="ada_mod",
    )(c_t, w, b)


def _inproj_kernel(x_ref, g_ref, sc_ref, sh_ref, w_ref, b_ref, o_ref, h_ref):
    first = pl.program_id(2) == 0

    @pl.when(first)
    def _():
        tm = x_ref.shape[1]
        ck = tm // NORM_CHUNKS
        for c in range(NORM_CHUNKS):
            rows = slice(c * ck, (c + 1) * ck)
            x = x_ref[0, rows, :]
            ms = jnp.mean(x * x, axis=-1, keepdims=True)
            h = x * lax.rsqrt(ms + EPS) * g_ref[...]
            h = (h * (1.0 + sc_ref[0]) + sh_ref[0]).astype(BF16)
            h_ref[rows, :] = h
            o_ref[0, rows, :] = jnp.dot(h, w_ref[...], preferred_element_type=F32) + b_ref[...]

    @pl.when(jnp.logical_not(first))
    def _():
        o_ref[0] = jnp.dot(h_ref[...], w_ref[...], preferred_element_type=F32) + b_ref[...]


def _inproj(x, g, scale, shift, w, b):
    bsz, t, d = x.shape
    n = w.shape[1]
    tm, tn = 1024, 1024
    return pl.pallas_call(
        _inproj_kernel,
        grid=(bsz, t // tm, n // tn),
        in_specs=[pl.BlockSpec((1, tm, d), lambda bi, i, j: (bi, i, 0)),
                  pl.BlockSpec((1, d), lambda bi, i, j: (0, 0)),
                  pl.BlockSpec((1, 1, d), lambda bi, i, j: (bi, 0, 0)),
                  pl.BlockSpec((1, 1, d), lambda bi, i, j: (bi, 0, 0)),
                  pl.BlockSpec((d, tn), lambda bi, i, j: (0, j)),
                  pl.BlockSpec((1, tn), lambda bi, i, j: (0, j))],
        out_specs=pl.BlockSpec((1, tm, tn), lambda bi, i, j: (bi, i, j)),
        out_shape=jax.ShapeDtypeStruct((bsz, t, n), F32),
        scratch_shapes=[pltpu.VMEM((tm, d), BF16)],
        compiler_params=_cparams(("parallel", "parallel", "arbitrary")),
        name="norm_inproj",
    )(x, g, scale, shift, w, b)


def _mlstm_kernel(x_ref, v_ref, o_ref, z_ref, gt_ref, cw_ref, cb_ref, wq_ref, wk_ref, nw_ref, sk_ref, fb_ref,
                  y_ref, c_scr, n_scr, m_scr, xp_scr):
    L = M_CHUNK

    @pl.when(pl.program_id(1) == 0)
    def _():
        c_scr[...] = jnp.zeros_like(c_scr)
        n_scr[...] = jnp.zeros_like(n_scr)
        m_scr[...] = jnp.zeros_like(m_scr)
        xp_scr[...] = jnp.zeros_like(xp_scr)

    x = x_ref[0]
    prev = xp_scr[...]
    row8 = lax.broadcasted_iota(jnp.int32, (8, M_WIDTH), 0)
    cw = cw_ref[...]
    xc = cb_ref[...] + x * cw[CONV_K - 1:CONV_K, :]
    for sft in range(1, CONV_K):
        xr = pltpu.roll(x, sft, 0)
        top = jnp.where(row8 < sft, pltpu.roll(prev, sft, 0), xr[0:8])
        xs = jnp.concatenate([top, xr[8:]], axis=0)
        xc = xc + xs * cw[CONV_K - 1 - sft:CONV_K - sft, :]
    xp_scr[...] = x[L - 8:L]
    xc = _silu(xc)

    gt = gt_ref[0]
    col = lax.broadcasted_iota(jnp.int32, (L, LANES), 1)
    logf = _log_sigmoid(gt + fb_ref[...])
    a_c = jnp.where((col >= M_HEADS) & (col < 2 * M_HEADS), logf, gt)
    ri = lax.broadcasted_iota(jnp.int32, (L, L), 0)
    ci = lax.broadcasted_iota(jnp.int32, (L, L), 1)
    causal = ri >= ci
    hp = lax.Precision.HIGHEST
    tri = causal.astype(F32)
    tri_t = (ri <= ci).astype(F32)
    b_c = jnp.dot(tri, a_c, precision=hp, preferred_element_type=F32)
    a_r = a_c.T
    b_r = jnp.dot(a_r[0:8], tri_t, precision=hp, preferred_element_type=F32)

    for h in range(M_HEADS):
        sl = slice(h * M_HD, (h + 1) * M_HD)
        xh = xc[:, sl]
        xb = xh.astype(BF16)
        q = jnp.dot(xb, wq_ref[h], preferred_element_type=F32)
        k = jnp.dot(xb, wk_ref[h], preferred_element_type=F32) * (M_HD ** -0.5)
        vb = v_ref[0, :, sl].astype(BF16)
        qb = q.astype(BF16)
        kb = k.astype(BF16)

        bt = b_c[:, M_HEADS + h:M_HEADS + h + 1]
        ic = a_c[:, h:h + 1]
        bs = b_r[M_HEADS + h:M_HEADS + h + 1, :]
        ir = a_r[h:h + 1, :]
        m_prev = m_scr[h][:, 0:1]

        dm = jnp.where(causal, bt - bs + ir, -jnp.inf)
        inter = bt + m_prev
        m_t = jnp.maximum(inter, jnp.max(dm, axis=-1, keepdims=True))
        w_in = jnp.exp(dm - m_t)
        w_st = jnp.exp(inter - m_t)
        s = lax.dot_general(qb, kb, (((1,), (1,)), ((), ())), preferred_element_type=F32) * w_in
        cmat = c_scr[h]
        nvec = n_scr[h]
        sb = s.astype(BF16)
        num = w_st * jnp.dot(qb, cmat.astype(BF16), preferred_element_type=F32) \
            + jnp.dot(sb, vb, preferred_element_type=F32)
        nt_dims = (((1,), (1,)), ((), ()))
        qn = lax.dot_general(qb, jnp.broadcast_to(nvec, (8, M_HD)).astype(BF16), nt_dims,
                             preferred_element_type=F32)[:, 0:1]
        ssum = lax.dot_general(sb, jnp.ones((8, L), BF16), nt_dims, preferred_element_type=F32)[:, 0:1]
        den = w_st * qn + ssum
        hh = num / jnp.maximum(jnp.abs(den), jnp.exp(-m_t))

        b_last = bt[L - 1:L, :]
        w_end = b_last - bt + ic
        m_new = jnp.maximum(b_last + m_prev, jnp.max(w_end, axis=0, keepdims=True))
        decay = jnp.exp(b_last + m_prev - m_new)
        kwt = k * jnp.exp(w_end - m_new)
        c_scr[h] = decay * cmat + lax.dot_general(kwt.astype(BF16), vb, (((0,), (0,)), ((), ())),
                                                  preferred_element_type=F32)
        n_scr[h] = decay * nvec + jnp.sum(kwt, axis=0, keepdims=True)
        m_scr[h] = jnp.broadcast_to(m_new, (1, LANES))

        mu = jnp.mean(hh, axis=-1, keepdims=True)
        hc = hh - mu
        var = jnp.mean(hc * hc, axis=-1, keepdims=True)
        hn = hc * lax.rsqrt(var + EPS) * nw_ref[:, sl]
        out = jax.nn.sigmoid(o_ref[0, :, sl]) * hn + sk_ref[:, sl] * xh
        y_ref[0, :, sl] = (out * _silu(z_ref[0, :, sl])).astype(BF16)


def _mlstm(p, conv_w, conv_b, wq, wk, norm_w, skip, fb_row):
    bsz, t, _ = p.shape
    L = M_CHUNK
    cb = lambda c: (lambda bi, i: (bi, i, c))
    full2 = lambda bi, i: (0, 0)
    full3 = lambda bi, i: (0, 0, 0)
    return pl.pallas_call(
        _mlstm_kernel,
        grid=(bsz, t // L),
        in_specs=[pl.BlockSpec((1, L, M_WIDTH), cb(COL_MX // M_WIDTH)),
                  pl.BlockSpec((1, L, M_WIDTH), cb(COL_MV // M_WIDTH)),
                  pl.BlockSpec((1, L, M_WIDTH), cb(COL_MO // M_WIDTH)),
                  pl.BlockSpec((1, L, M_WIDTH), cb(COL_MZ // M_WIDTH)),
                  pl.BlockSpec((1, L, LANES), cb(COL_GATES // LANES)),
                  pl.BlockSpec((CONV_K, M_WIDTH), full2),
                  pl.BlockSpec((1, M_WIDTH), full2),
                  pl.BlockSpec((M_HEADS, M_HD, M_HD), full3),
                  pl.BlockSpec((M_HEADS, M_HD, M_HD), full3),
                  pl.BlockSpec((1, M_WIDTH), full2),
                  pl.BlockSpec((1, M_WIDTH), full2),
                  pl.BlockSpec((1, LANES), full2)],
        out_specs=pl.BlockSpec((1, L, M_WIDTH), lambda bi, i: (bi, i, 0)),
        out_shape=jax.ShapeDtypeStruct((bsz, t, M_WIDTH), BF16),
        scratch_shapes=[pltpu.VMEM((M_HEADS, M_HD, M_HD), F32),
                        pltpu.VMEM((M_HEADS, 1, M_HD), F32),
                        pltpu.VMEM((M_HEADS, 1, LANES), F32),
                        pltpu.VMEM((8, M_WIDTH), F32)],
        compiler_params=_cparams(("parallel", "arbitrary")),
        name="mlstm_group",
    )(p, p, p, p, p, conv_w, conv_b, wq, wk, norm_w, skip, fb_row)


def _compress_kernel(kc0_ref, kc1_ref, vc0_ref, vc1_ref, posk_ref, posv_ref, w1k_ref, w1kc_ref, w2k_ref,
                     w1v_ref, w1vc_ref, w2v_ref, kcmp_ref, vcmpt_ref):
    nb = kcmp_ref.shape[2]

    def hidden(src_refs, pos_ref, w1_ref, w1c_ref):
        halves = [jnp.zeros((nb, 4 * CMP_HIDDEN), F32) for _ in src_refs]
        for l in range(CMP_STRIDE):
            for hf, r in enumerate(src_refs):
                xl = r[0, pl.ds(l, nb, stride=CMP_STRIDE), :]
                halves[hf] = halves[hf] + jnp.dot(xl.astype(BF16), w1c_ref[l], preferred_element_type=F32)
        acc = jnp.concatenate([hv[:, c * 2 * CMP_HIDDEN:(c + 1) * 2 * CMP_HIDDEN] for hv in halves for c in range(2)],
                              axis=0)
        first = acc[:, :CMP_HIDDEN]
        second = acc[:, CMP_HIDDEN:]
        posb = jnp.dot(jnp.broadcast_to(pos_ref[...], (8, CMP_LEN * N_HD)).astype(BF16), w1_ref[...],
                       preferred_element_type=F32)[0:1]
        hid = first + pltpu.roll(second, N_KV * nb - 1, 0) + posb
        return jax.nn.gelu(hid).astype(BF16)

    hk = hidden((kc0_ref, kc1_ref), posk_ref, w1k_ref, w1kc_ref)
    kc = jnp.dot(hk, w2k_ref[...], preferred_element_type=F32)
    hv = hidden((vc0_ref, vc1_ref), posv_ref, w1v_ref, w1vc_ref)
    vc = jnp.dot(hv, w2v_ref[...], preferred_element_type=F32)
    j = lax.broadcasted_iota(jnp.int32, (nb, 1), 0)
    lane = lax.broadcasted_iota(jnp.int32, (nb, N_HD), 1)
    n_real = (nb * CMP_STRIDE - CMP_LEN) // CMP_STRIDE + 1
    aux = jnp.where(lane < 3, (j * CMP_STRIDE).astype(F32),
                    jnp.where(lane < 6, (CMP_LEN - 1) * 0.5,
                              jnp.where((lane == 6) & (j >= n_real), 1.0, 0.0))).astype(BF16)
    ones_blk = jnp.where(lax.broadcasted_iota(jnp.int32, (VROWS - N_HD, nb), 0) == 0, 1.0, 0.0)
    for g in range(N_KV):
        kg = kc[g * nb:(g + 1) * nb, 0:N_HD]
        k_hi = kg.astype(BF16)
        k_lo = (kg - k_hi.astype(F32)).astype(BF16)
        kcmp_ref[0, g] = jnp.concatenate([k_hi, k_lo, k_hi, aux], axis=1)
        vt = vc[g * nb:(g + 1) * nb].T
        vcmpt_ref[0, g] = jnp.concatenate([vt[0:N_HD], ones_blk], axis=0).astype(BF16)


def _compress(p, posk, posv, w1k, w1kc, w2k, w1v, w1vc, w2v):
    bsz, t, _ = p.shape
    nb = t // CMP_STRIDE
    cb = lambda c: (lambda bi: (bi, 0, c))
    f2 = lambda bi: (0, 0)
    f3 = lambda bi: (0, 0, 0)
    wspecs = [pl.BlockSpec((CMP_LEN * N_HD, CMP_HIDDEN), f2),
              pl.BlockSpec((CMP_STRIDE, LANES, 4 * CMP_HIDDEN), f3),
              pl.BlockSpec((CMP_HIDDEN, LANES), f2)]
    return pl.pallas_call(
        _compress_kernel,
        grid=(bsz,),
        in_specs=[pl.BlockSpec((1, t, LANES), cb(COL_KC // LANES)),
                  pl.BlockSpec((1, t, LANES), cb(COL_KC // LANES + 1)),
                  pl.BlockSpec((1, t, LANES), cb(COL_VC // LANES)),
                  pl.BlockSpec((1, t, LANES), cb(COL_VC // LANES + 1)),
                  pl.BlockSpec((1, CMP_LEN * N_HD), f2),
                  pl.BlockSpec((1, CMP_LEN * N_HD), f2)] + wspecs + wspecs,
        out_specs=[pl.BlockSpec((1, N_KV, nb, 4 * N_HD), lambda bi: (bi, 0, 0, 0)),
                   pl.BlockSpec((1, N_KV, VROWS, nb), lambda bi: (bi, 0, 0, 0))],
        out_shape=[jax.ShapeDtypeStruct((bsz, N_KV, nb, 4 * N_HD), BF16),
                   jax.ShapeDtypeStruct((bsz, N_KV, VROWS, nb), BF16)],
        compiler_params=_cparams(("parallel",)),
        name="nsa_compress",
    )(p, p, p, p, posk, posv, w1k, w1kc, w2k, w1v, w1vc, w2v)


def _relayout_kernel(ks_ref, vs_ref, kw_ref, vw_ref, ksa_ref, vsa_ref, kwa_ref, vwa_ref):
    i = pl.program_id(1)
    is_pad = i == 0
    flag = jnp.where(is_pad, 1.0, 0.0)
    row = lax.broadcasted_iota(jnp.int32, (SKT, 1), 0)
    lane = lax.broadcasted_iota(jnp.int32, (SKT, N_HD), 1)
    al = lane - AUX_ROWS
    ones_blk = jnp.where(lax.broadcasted_iota(jnp.int32, (VROWS - N_HD, SKT), 0) == 0, 1.0, 0.0)
    for u in range(PAD_TILES):
        rows = slice(u * SKT, (u + 1) * SKT)
        base = ((i - 1) * PAD_TILES + u) * SKT
        pos = jnp.where(is_pad, 0, base + row)
        blk = pos // SLC_LEN
        p_hi = (blk * SLC_LEN).astype(F32)
        p_lo = (pos - blk * SLC_LEN).astype(F32)
        mid_w = jnp.where((al >= 0) & (al < 3), p_hi,
                          jnp.where((al >= 3) & (al < 6), p_lo, jnp.where(al == 6, flag, 0.0)))
        mid_s = jnp.where(lane == blk % SEL_GRP, 1.0, mid_w)
        ks = ks_ref[0, rows, :]
        kw = kw_ref[0, rows, :]
        for g in range(N_KV):
            sl = slice(g * N_HD, (g + 1) * N_HD)
            ksa_ref[0, g, rows, :] = jnp.concatenate([ks[:, sl], mid_s], axis=1).astype(BF16)
            kwa_ref[0, g, rows, :] = jnp.concatenate([kw[:, sl], mid_w], axis=1).astype(BF16)
        vst = vs_ref[0, rows, :].T
        vwt = vw_ref[0, rows, :].T
        for g in range(N_KV):
            sl = slice(g * N_HD, (g + 1) * N_HD)
            vsa_ref[0, g, u] = jnp.concatenate([vst[sl], ones_blk], axis=0).astype(BF16)
            vwg = jnp.concatenate([vwt[sl], ones_blk], axis=0).astype(BF16)
            for j in range(SKT // WKT):
                vwa_ref[0, g, u * (SKT // WKT) + j] = vwg[:, j * WKT:(j + 1) * WKT]


def _relayout(p):
    bsz, t, _ = p.shape
    nt = t // SKT + PAD_TILES
    wpt = SKT // WKT
    rt = PAD_TILES
    cb = lambda c: (lambda bi, i: (bi, jnp.maximum(i - 1, 0), c))
    return pl.pallas_call(
        _relayout_kernel,
        grid=(bsz, nt // rt),
        in_specs=[pl.BlockSpec((1, rt * SKT, KV_W), cb(COL_KS // KV_W)),
                  pl.BlockSpec((1, rt * SKT, KV_W), cb(COL_VS // KV_W)),
                  pl.BlockSpec((1, rt * SKT, KV_W), cb(COL_KW // KV_W)),
                  pl.BlockSpec((1, rt * SKT, KV_W), cb(COL_VW // KV_W))],
        out_specs=[pl.BlockSpec((1, N_KV, rt * SKT, KA_SEL), lambda bi, i: (bi, 0, i, 0)),
                   pl.BlockSpec((1, N_KV, rt, VROWS, SKT), lambda bi, i: (bi, 0, i, 0, 0)),
                   pl.BlockSpec((1, N_KV, rt * SKT, KA_WIN), lambda bi, i: (bi, 0, i, 0)),
                   pl.BlockSpec((1, N_KV, rt * wpt, VROWS, WKT), lambda bi, i: (bi, 0, i, 0, 0))],
        out_shape=[jax.ShapeDtypeStruct((bsz, N_KV, nt * SKT, KA_SEL), BF16),
                   jax.ShapeDtypeStruct((bsz, N_KV, nt, VROWS, SKT), BF16),
                   jax.ShapeDtypeStruct((bsz, N_KV, nt * SKT, KA_WIN), BF16),
                   jax.ShapeDtypeStruct((bsz, N_KV, nt * wpt, VROWS, WKT), BF16)],
        compiler_params=_cparams(("parallel", "parallel")),
        name="nsa_relayout",
    )(p, p, p, p)


def _nsa_batch_kernel(sp_ref, q_ref, gt_ref, z_ref, kcmp_ref, vcmpt_ref, ovt_ref, ks_ref, vs_ref, kw_ref,
                      vw_ref, tri_ref, wb_ref, y_ref, sa_scr, sb_scr, g_scr, imp_scr, rank_scr, part_scr, selb_scr):
    g = pl.program_id(0)
    qb = pl.program_id(1)
    n_items = q_ref.shape[0]
    start = qb * QBLK
    nb = kcmp_ref.shape[2]
    n_slc = ovt_ref.shape[0]
    nw = N_HPG * QBLK
    n_top = min(SLC_TOPN, n_slc)
    t_row = start + lax.broadcasted_iota(jnp.int32, (1, QBLK), 1)
    s_bufs = (sa_scr, sb_scr)

    def tile4(a):
        return jnp.concatenate([a] * N_HPG, axis=1)

    def col_reduce8(s, op):
        out = s[0:8]
        for r in range(1, s.shape[0] // 8):
            out = op(out, s[8 * r:8 * r + 8])
        return out

    def col_max(s):
        return jnp.max(col_reduce8(s, jnp.maximum), axis=0, keepdims=True)

    def normalise(acc):
        den = acc[N_HD:N_HD + 1]
        return acc[0:N_HD] / jnp.where(den > 0, den, 1.0)

    def gate_row(i, h, branch):
        return g_scr[i, pl.ds(GATE_NG + (g * N_HPG + h) * 3 + branch, 1), :]

    hl = lax.broadcasted_iota(jnp.int32, (AUX_ROWS, nw), 1) // QBLK
    ar = lax.broadcasted_iota(jnp.int32, (AUX_ROWS, nw), 0)

    def per_head(vals):
        out = jnp.full((AUX_ROWS, nw), vals[N_HPG - 1], F32)
        for h in range(N_HPG - 2, -1, -1):
            out = jnp.where(hl == h, vals[h], out)
        return out

    pieces = [per_head([sp_ref[(g * N_HPG + h) * 3 + j] for h in range(N_HPG)]) for j in range(3)]
    aux = jnp.where(ar == 6, NEG, 0.0)
    for j in range(3):
        aux = jnp.where((ar == j) | (ar == j + 3), pieces[j], aux)
    aux_b = aux.astype(BF16)
    q_tail = jnp.zeros((KA_SEL - N_HD - 2 * AUX_ROWS, nw), BF16)
    sidx = lax.broadcasted_iota(jnp.int32, (n_slc, QBLK), 0)
    cur = t_row // SLC_LEN
    valid = sidx <= cur
    sub8 = lax.broadcasted_iota(jnp.int32, (8, QBLK), 0)
    last_blk = (start + QBLK - 1) // SLC_LEN
    cmp_end = lax.broadcasted_iota(jnp.int32, (nb, 1), 0) * CMP_STRIDE + (CMP_LEN - 1)
    cmp_bias = tile4(jnp.where(cmp_end <= t_row, 0.0, NEG))
    wrows = WIN + QBLK

    def prepare(i):
        qt = (q_ref[i] * (N_HD ** -0.5)).T
        q4l = jnp.concatenate([qt[h * N_HD:(h + 1) * N_HD] for h in range(N_HPG)], axis=1) * LOG2E
        q4s = q4l.astype(BF16)
        q4lo = (q4l - q4s.astype(F32)).astype(BF16)
        g_scr[i] = jax.nn.sigmoid(gt_ref[i]).T

        q_cmp = jnp.concatenate([q4s, q4s, q4lo, aux_b, jnp.zeros((N_HD - AUX_ROWS, nw), BF16)], axis=0)
        s_c = jnp.dot(kcmp_ref[i, 0], q_cmp, preferred_element_type=F32) + cmp_bias
        e_c = jnp.exp2(s_c - col_max(s_c))
        z_c = jnp.sum(col_reduce8(e_c, jnp.add), axis=0, keepdims=True)
        inv_c = jnp.where(tile4(t_row >= CMP_LEN - 1) & (z_c > 0), 1.0 / z_c, 0.0)
        o_c = jnp.dot(vcmpt_ref[i, 0], e_c.astype(BF16), preferred_element_type=F32)[0:N_HD] * inv_c
        p_c = e_c * inv_c
        psum = p_c[:, 0:QBLK]
        for h in range(1, N_HPG):
            psum = psum + p_c[:, h * QBLK:(h + 1) * QBLK]
        parts, rest = [], psum
        for _ in range(3):
            parts.append(rest.astype(BF16))
            rest = rest - parts[-1].astype(F32)
        imp = jnp.dot(ovt_ref[...], jnp.concatenate(parts, axis=0), preferred_element_type=F32)
        imp = jnp.where(valid, imp, -jnp.inf)
        imp_scr[...] = jnp.where((sidx == 0) | (sidx == cur), jnp.inf, imp)

        q_win = jnp.concatenate([q4s, jnp.zeros((AUX_ROWS, nw), BF16), aux_b,
                                 jnp.zeros((KA_WIN - N_HD - 2 * AUX_ROWS, nw), BF16)], axis=0)
        s_w = jnp.dot(kw_ref[i, 0, pl.ds(pl.multiple_of(start, QBLK), wrows), :], q_win,
                      preferred_element_type=F32)
        s_w = jnp.concatenate([s_w[0:WKT] + tile4(wb_ref[0]), s_w[WKT:wrows - WKT],
                               s_w[wrows - WKT:] + tile4(wb_ref[1])], axis=0)
        p_w = jnp.exp2(s_w - col_max(s_w)).astype(BF16)
        v_w = jnp.concatenate([vw_ref[i, 0, qb * (QBLK // WKT) + j] for j in range(wrows // WKT)], axis=1)
        o_w = normalise(jnp.dot(v_w, p_w, preferred_element_type=F32))
        for h in range(N_HPG):
            hs = slice(h * QBLK, (h + 1) * QBLK)
            part_scr[i, :, hs] = gate_row(i, h, 0) * o_c[:, hs] + gate_row(i, h, 2) * o_w[:, hs]

        rank_scr[...] = jnp.zeros_like(rank_scr)
        for ri in range(n_slc // 8):
            @pl.when((last_blk >= n_top) & (8 * ri <= last_blk))
            def _():
                rows = imp_scr[8 * ri:8 * ri + 8]
                for r in range(n_slc // 8):
                    blk8 = imp_scr[8 * r:8 * r + 8]
                    acc = rank_scr[8 * r:8 * r + 8]
                    for ii in range(8):
                        row = rows[ii:ii + 1]
                        if ri < r:
                            before = row >= blk8
                        elif ri > r:
                            before = row > blk8
                        else:
                            before = (row > blk8) | ((row == blk8) & (sub8 > ii))
                        acc = acc + jnp.where(before, 1.0, 0.0)
                    rank_scr[8 * r:8 * r + 8] = acc
        selb_scr[i] = tile4(jnp.where((rank_scr[...] < n_top) & valid, 0.0, NEG))
        return q4s

    q4 = [prepare(i) for i in range(n_items)]

    n_tiles = qb // (SKT // QBLK) + 1
    last = n_tiles - 1

    def sel_scores(i, kt):
        grp = pl.multiple_of((kt * SKT // SLC_LEN) // SEL_GRP * SEL_GRP, SEL_GRP)
        sel_rows = jnp.concatenate([selb_scr[i, pl.ds(grp, SEL_GRP), :], jnp.zeros((AUX_ROWS - SEL_GRP, nw), F32)],
                                   axis=0)
        q_sel = jnp.concatenate([q4[i], sel_rows.astype(BF16), aux_b, q_tail], axis=0)
        krow = pl.multiple_of((kt + PAD_TILES) * SKT, SKT)
        return jnp.dot(ks_ref[i, 0, pl.ds(krow, SKT), :], q_sel, preferred_element_type=F32)

    def tile_rows(kt):
        return pl.ds(pl.multiple_of(kt * SKT, SKT), SKT)

    def pass1(i):
        def fn(kt, mrun):
            s = sel_scores(i, kt)
            s_bufs[i % 2][tile_rows(kt), :] = s
            return jnp.maximum(mrun, col_reduce8(s, jnp.maximum))
        return fn

    def pass2(i, m_sel):
        def fn(kt, acc):
            pr = jnp.exp2(s_bufs[i % 2][tile_rows(kt), :] - m_sel).astype(BF16)
            return acc + jnp.dot(vs_ref[i, 0, kt + PAD_TILES], pr, preferred_element_type=F32)
        return fn

    def both(f1, f2):
        def fn(kt, carry):
            return f1(kt, carry[0]), f2(kt, carry[1])
        return fn

    def run_tiles(fn, n, init):
        def group(size, first):
            def body(j, carry):
                for u in range(size):
                    carry = fn(first + j * size + u, carry)
                return carry
            return body

        n_u = n // UNROLL
        carry = lax.fori_loop(0, n_u, group(UNROLL, 0), init)
        done = n_u * UNROLL
        size = UNROLL // 2
        while size >= 1:
            take = (n // size) % 2
            carry = lax.fori_loop(0, take, group(size, done), carry)
            done = done + take * size
            size //= 2
        return carry

    def diag_tile(i, mrun):
        s = sel_scores(i, last) + tile4(tri_ref[qb % (SKT // QBLK)])
        s_bufs[i % 2][tile_rows(last), :] = s
        return jnp.max(jnp.maximum(mrun, col_reduce8(s, jnp.maximum)), axis=0, keepdims=True)

    mrun0 = jnp.full((8, nw), NEG, F32)
    acc0 = jnp.zeros((VROWS, nw), F32)
    m_prev = diag_tile(0, run_tiles(pass1(0), last, mrun0))
    outs = []
    for i in range(1, n_items):
        p2 = pass2(i - 1, m_prev)
        mrun, acc = run_tiles(both(pass1(i), p2), last, (mrun0, acc0))
        m_cur = diag_tile(i, mrun)
        outs.append(normalise(p2(last, acc)))
        m_prev = m_cur
    outs.append(normalise(run_tiles(pass2(n_items - 1, m_prev), n_tiles, acc0)))

    for i in range(n_items):
        tot = [part_scr[i, :, h * QBLK:(h + 1) * QBLK] + gate_row(i, h, 1) * outs[i][:, h * QBLK:(h + 1) * QBLK]
               for h in range(N_HPG)]
        o = jnp.concatenate(tot, axis=0).T
        y_ref[i] = (o * _silu(z_ref[i])).astype(BF16)


def _edge_biases():
    kl = np.arange(SKT)[:, None]
    ql = np.arange(QBLK)[None, :]
    tri = np.stack([np.where(kl <= par * QBLK + ql, 0.0, NEG) for par in range(SKT // QBLK)])
    kk = np.arange(WKT)[:, None]
    wb = np.stack([np.where(kk > ql, 0.0, NEG), np.where(kk <= ql + WKT - QBLK, 0.0, NEG)])
    return jnp.asarray(tri, F32), jnp.asarray(wb, F32)


def _nsa(p, spieces, kcmp, vcmpt, ovt, ks, vs, kw, vw):
    bsz, t, _ = p.shape
    n_slc = t // SLC_LEN
    gw = N_HPG * N_HD
    tri, wb = _edge_biases()
    nw = N_HPG * QBLK
    per_g = lambda a: pl.BlockSpec((bsz, 1) + a.shape[2:], lambda g, i, sp: (0, g) + (0,) * (a.ndim - 2),
                                   pipeline_mode=pl.Buffered(1))
    const = lambda a: pl.BlockSpec(a.shape, lambda g, i, sp: (0,) * a.ndim)
    grid_spec = pltpu.PrefetchScalarGridSpec(
        num_scalar_prefetch=1,
        grid=(N_KV, t // QBLK),
        in_specs=[pl.BlockSpec((bsz, QBLK, gw), lambda g, i, sp: (0, i, COL_NQ // gw + g)),
                  pl.BlockSpec((bsz, QBLK, LANES), lambda g, i, sp: (0, i, COL_GATES // LANES)),
                  pl.BlockSpec((bsz, QBLK, gw), lambda g, i, sp: (0, i, COL_NZ // gw + g)),
                  per_g(kcmp), per_g(vcmpt), const(ovt),
                  per_g(ks), per_g(vs), per_g(kw), per_g(vw), const(tri), const(wb)],
        out_specs=pl.BlockSpec((bsz, QBLK, gw), lambda g, i, sp: (0, i, g)),
        scratch_shapes=[pltpu.VMEM((t, nw), F32),
                        pltpu.VMEM((t, nw), F32),
                        pltpu.VMEM((bsz, LANES, QBLK), F32),
                        pltpu.VMEM((n_slc, QBLK), F32),
                        pltpu.VMEM((n_slc, QBLK), F32),
                        pltpu.VMEM((bsz, N_HD, nw), F32),
                        pltpu.VMEM((bsz, n_slc, nw), F32)],
    )
    return pl.pallas_call(
        _nsa_batch_kernel,
        grid_spec=grid_spec,
        out_shape=jax.ShapeDtypeStruct((bsz, t, N_WIDTH), BF16),
        compiler_params=_cparams(("arbitrary", "arbitrary")),
        name="nsa_attention",
    )(spieces, p, p, p, kcmp, vcmpt, ovt, ks, vs, kw, vw, tri, wb)


def _outproj_kernel(ym_ref, yn_ref, w_ref, x_ref, gate_ref, fg_ref, o_ref, wb_scr, *, final):
    @pl.when((pl.program_id(0) == 0) & (pl.program_id(1) == 0))
    def _():
        rows = w_ref.shape[1]
        for c in range(rows // WP_TILE):
            cs = slice(c * WP_TILE, (c + 1) * WP_TILE)
            wb_scr[cs, :] = w_ref[0, cs, :].astype(BF16)

    y = jnp.dot(ym_ref[0], wb_scr[0:M_WIDTH, :], preferred_element_type=F32)
    y = y + jnp.dot(yn_ref[0], wb_scr[M_WIDTH:, :], preferred_element_type=F32)
    hres = x_ref[0] + gate_ref[0] * y
    if final:
        ms = jnp.mean(hres * hres, axis=-1, keepdims=True)
        hres = hres * lax.rsqrt(ms + EPS) * fg_ref[...]
    o_ref[0] = hres


def _outproj(ym, yn, w, layer, x, gate, fg, final):
    bsz, t, d = x.shape
    tm = 512
    return pl.pallas_call(
        functools.partial(_outproj_kernel, final=final),
        grid=(bsz, t // tm),
        in_specs=[pl.BlockSpec((1, tm, M_WIDTH), lambda bi, i: (bi, i, 0)),
                  pl.BlockSpec((1, tm, N_WIDTH), lambda bi, i: (bi, i, 0)),
                  pl.BlockSpec((1, M_WIDTH + N_WIDTH, d), lambda bi, i: (layer, 0, 0),
                               pipeline_mode=pl.Buffered(1)),
                  pl.BlockSpec((1, tm, d), lambda bi, i: (bi, i, 0)),
                  pl.BlockSpec((1, 1, d), lambda bi, i: (bi, 0, 0)),
                  pl.BlockSpec((1, d), lambda bi, i: (0, 0))],
        out_specs=pl.BlockSpec((1, tm, d), lambda bi, i: (bi, i, 0)),
        out_shape=jax.ShapeDtypeStruct((bsz, t, d), F32),
        scratch_shapes=[pltpu.VMEM((M_WIDTH + N_WIDTH, d), BF16)],
        compiler_params=_cparams(("arbitrary", "arbitrary")),
        name="outproj_residual",
    )(ym, yn, w, x, gate, fg)


SRC_MI = 4 * M_WIDTH
SRC_NQ = SRC_MI + 2 * M_HEADS
SRC_NG = SRC_NQ + N_WIDTH + 6 * KV_W
SRC_NZ = SRC_NG + 3 * N_HEADS


def _reorder_cols(a):
    parts = [a[..., 0:SRC_MI], a[..., SRC_NQ:SRC_NG], a[..., SRC_NZ:SRC_NZ + N_WIDTH], a[..., SRC_MI:SRC_NQ],
             a[..., SRC_NG:SRC_NZ]]
    used = sum(x.shape[-1] for x in parts)
    parts.append(jnp.zeros(a.shape[:-1] + (NP_PAD - used,), a.dtype))
    return jnp.concatenate(parts, axis=-1)


WP_TILE = 512
WP_GATE_TILE = COL_GATES // WP_TILE


def _wprep_kernel(w_ref, g1_ref, g2_ref, o_ref):
    j = pl.program_id(0)
    d = o_ref.shape[0]

    @pl.when(j < WP_GATE_TILE)
    def _():
        for c in range(d // WP_TILE):
            cs = slice(c * WP_TILE, (c + 1) * WP_TILE)
            o_ref[cs, :] = w_ref[0, :, cs].T.astype(BF16)

    @pl.when(j == WP_GATE_TILE)
    def _():
        n_gate = g1_ref.shape[1] + g2_ref.shape[1]
        gt = jnp.concatenate([g1_ref[0], g2_ref[0], jnp.zeros((LANES - n_gate, d), F32)], axis=0)
        for c in range(d // WP_TILE):
            cs = slice(c * WP_TILE, (c + 1) * WP_TILE)
            o_ref[cs, 0:LANES] = gt[:, cs].T.astype(BF16)
        o_ref[:, LANES:] = jnp.zeros((d, WP_TILE - LANES), BF16)


def _wprep(w_t, layer):
    _, n, d = w_t.shape
    n_big = COL_GATES // WP_TILE
    assert n == SRC_NZ + N_WIDTH and NP_PAD // WP_TILE == n_big + 1

    def src_row(j):
        return jnp.where(j < COL_NQ // WP_TILE, j * WP_TILE,
                         jnp.where(j < COL_NZ // WP_TILE, SRC_NQ + (j - COL_NQ // WP_TILE) * WP_TILE,
                                   jnp.where(j < n_big, SRC_NZ + (j - COL_NZ // WP_TILE) * WP_TILE, 0)))

    el = pl.Element
    return pl.pallas_call(
        _wprep_kernel,
        grid=(NP_PAD // WP_TILE,),
        in_specs=[pl.BlockSpec((el(1), el(WP_TILE), el(d)), lambda j: (layer, pl.multiple_of(src_row(j), 8), 0)),
                  pl.BlockSpec((el(1), el(SRC_NQ - SRC_MI), el(d)), lambda j: (layer, SRC_MI, 0)),
                  pl.BlockSpec((el(1), el(SRC_NZ - SRC_NG), el(d)), lambda j: (layer, SRC_NG, 0))],
        out_specs=pl.BlockSpec((d, WP_TILE), lambda j: (0, j)),
        out_shape=jax.ShapeDtypeStruct((d, NP_PAD), BF16),
        compiler_params=_cparams(("parallel",)),
        name="inproj_weight_prep",
    )(w_t, w_t, w_t)


def _overlap_t(t):
    n_cmp_rows = t // CMP_STRIDE
    n_slc = t // SLC_LEN
    c0 = np.arange(n_cmp_rows) * CMP_STRIDE
    s0 = np.arange(n_slc) * SLC_LEN
    ov = (c0[None, :] <= s0[:, None] + SLC_LEN - 1) & (c0[None, :] + CMP_LEN - 1 >= s0[:, None])
    ov[:, (t - CMP_LEN) // CMP_STRIDE + 1:] = False
    return jnp.asarray(np.concatenate([ov] * 3, axis=1), BF16)


def kernel(x, c, ln_g, w_ada, b_ada, w_in, b_in, m_conv_w, m_conv_b, m_wq, m_wk, m_norm_w, m_skip, m_f_bias,
           n_pos_k, n_pos_v, n_w1_k, n_w2_k, n_w1_v, n_w2_v, w_out, final_g):
    out_dtype = x.dtype
    bsz, t, d = x.shape
    depth = ln_g.shape[0]
    h_res = x.astype(F32)
    assert bsz <= 2 and t % 1024 == 0 and t // CMP_STRIDE <= 256
    c_t =jnp.zeros((d, 8), F32).at[:, :bsz].set(c.astype(F32).T)
    slopes_np = np.array([2.0 ** (-8.0 * (h + 1) / N_HEADS) for h in range(N_HEADS)], np.float32)
    rest = (slopes_np.astype(np.float64) * LOG2E).astype(np.float32)
    pieces = []
    for _ in range(3):
        pieces.append(rest.astype(jnp.bfloat16).astype(np.float32))
        rest = rest - pieces[-1]
    spieces = jnp.asarray(np.stack(pieces, axis=1).reshape(-1))
    ovt = _overlap_t(t)

    def w1cat(w1):
        w = w1.reshape(2, CMP_STRIDE, N_HD, CMP_HIDDEN)
        w = jnp.concatenate([w[0], w[1]], axis=-1).astype(BF16)
        z = jnp.zeros_like(w)
        return jnp.concatenate([jnp.concatenate([w, z], axis=-1), jnp.concatenate([z, w], axis=-1)], axis=1)

    def w2pad(w2):
        return jnp.pad(w2, ((0, 0), (0, LANES - N_HD))).astype(BF16)

    for l in range(depth):
        mod = _ada(c_t, w_ada, b_ada[l][None, :], l, bsz)[:bsz]
        shift, scale, gate = mod[:, None, 0:d], mod[:, None, d:2 * d], mod[:, None, 2 * d:3 * d]
        p = _inproj(h_res, ln_g[l][None, :], scale, shift,
                    _wprep(jnp.swapaxes(w_in, 1, 2), l), _reorder_cols(b_in[l])[None, :])
        fb_row = jnp.zeros((1, LANES), F32).at[0, M_HEADS:2 * M_HEADS].set(m_f_bias[l])
        y_m = _mlstm(p, m_conv_w[l], m_conv_b[l][None, :], m_wq[l].astype(BF16), m_wk[l].astype(BF16),
                     m_norm_w[l][None, :], m_skip[l][None, :], fb_row)
        kcmp, vcmpt = _compress(p, n_pos_k[l].reshape(1, -1), n_pos_v[l].reshape(1, -1),
                                n_w1_k[l].astype(BF16), w1cat(n_w1_k[l]), w2pad(n_w2_k[l]),
                                n_w1_v[l].astype(BF16), w1cat(n_w1_v[l]), w2pad(n_w2_v[l]))
        ks, vs, kw, vw = _relayout(p)
        y_n = _nsa(p, spieces, kcmp, vcmpt, ovt, ks, vs, kw, vw)
        h_res = _outproj(y_m, y_n, w_out, l, h_res, gate, final_g[None, :], l == depth - 1)
    return h_res.astype(out_dtype)
```

```python
import functools

import numpy as np
import jax
import jax.numpy as jnp
from jax import lax
from jax.experimental import pallas as pl
from jax.experimental.pallas import tpu as pltpu

F32 = jnp.float32
BF16 = jnp.bfloat16

EPS = 1e-6
M_HEADS = 4
M_HD = 256
M_WIDTH = M_HEADS * M_HD
CONV_K = 4
M_CHUNK = 256
N_HEADS = 16
N_HD = 64
N_KV = 4
N_HPG = N_HEADS // N_KV
N_WIDTH = N_HEADS * N_HD
KV_W = N_KV * N_HD
CMP_LEN = 32
CMP_STRIDE = 16
CMP_HIDDEN = 2 * N_HD
SLC_LEN = 64
SLC_TOPN = 16
WIN = 512
QBLK = 256
SKT = 256
WKT = 256
PAD_TILES = WIN // SKT
SEL_GRP = 8
KA_SEL = 128
KA_WIN = 128
VROWS = 80
UNROLL = 8
AUX_ROWS = 16
LOG2E = 1.4426950408889634

COL_MX, COL_MV, COL_MO, COL_MZ = 0, 1024, 2048, 3072
COL_NQ = 4096
COL_KC, COL_VC, COL_KS, COL_VS, COL_KW, COL_VW = 5120, 5376, 5632, 5888, 6144, 6400
COL_NZ = 6656
COL_GATES = 7680
GATE_NG = 2 * M_HEADS
NP_PAD = 8192
NORM_CHUNKS = 4
LANES = 128
NEG = -1e30
VMEM_LIMIT = 56 * 1024 * 1024


def _cparams(sem):
    return pltpu.CompilerParams(dimension_semantics=sem, vmem_limit_bytes=VMEM_LIMIT)


def _silu(x):
    return x * jax.nn.sigmoid(x)


def _log_sigmoid(x):
    return jnp.minimum(x, 0.0) - jnp.log1p(jnp.exp(-jnp.abs(x)))


def _ada_kernel(ct_ref, w_ref, b_ref, o_ref, *, bsz):
    s_t = _silu(ct_ref[...])
    w = w_ref[0]
    row = lax.broadcasted_iota(jnp.int32, o_ref.shape, 0)
    out = jnp.zeros(o_ref.shape, F32)
    for b in range(bsz):
        prod = w * s_t[:, b:b + 1]
        acc = prod[0:8]
        for r in range(1, prod.shape[0] // 8):
            acc = acc + prod[8 * r:8 * r + 8]
        out = jnp.where(row == b, jnp.sum(acc, axis=0, keepdims=True) + b_ref[...], out)
    o_ref[...] = out


def _ada(c_t, w, b, layer, bsz):
    _, d, n = w.shape
    tn = 1024
    return pl.pallas_call(
        functools.partial(_ada_kernel, bsz=bsz),
        grid=(n // tn,),
        in_specs=[pl.BlockSpec((d, 8), lambda j: (0, 0)),
                  pl.BlockSpec((1, d, tn), lambda j: (layer, 0, j)),
                  pl.BlockSpec((1, tn), lambda j: (0, j))],
        out_specs=pl.BlockSpec((8, tn), lambda j: (0, j)),
        out_shape=jax.ShapeDtypeStruct((8, n), F32),
        compiler_params=_cparams(("parallel",)),
        name="ada_mod",
    )(c_t, w, b)


def _inproj_kernel(x_ref, g_ref, sc_ref, sh_ref, w_ref, b_ref, o_ref, h_ref):
    first = pl.program_id(2) == 0

    @pl.when(first)
    def _():
        tm = x_ref.shape[1]
        ck = tm // NORM_CHUNKS
        for c in range(NORM_CHUNKS):
            rows = slice(c * ck, (c + 1) * ck)
            x = x_ref[0, rows, :]
            ms = jnp.mean(x * x, axis=-1, keepdims=True)
            h = x * lax.rsqrt(ms + EPS) * g_ref[...]
            h = (h * (1.0 + sc_ref[0]) + sh_ref[0]).astype(BF16)
            h_ref[rows, :] = h
            o_ref[0, rows, :] = jnp.dot(h, w_ref[...], preferred_element_type=F32) + b_ref[...]

    @pl.when(jnp.logical_not(first))
    def _():
        o_ref[0] = jnp.dot(h_ref[...], w_ref[...], preferred_element_type=F32) + b_ref[...]


def _inproj(x, g, scale, shift, w, b):
    bsz, t, d = x.shape
    n = w.shape[1]
    tm, tn = 1024, 1024
    return pl.pallas_call(
        _inproj_kernel,
        grid=(bsz, t // tm, n // tn),
        in_specs=[pl.BlockSpec((1, tm, d), lambda bi, i, j: (bi, i, 0)),
                  pl.BlockSpec((1, d), lambda bi, i, j: (0, 0)),
                  pl.BlockSpec((1, 1, d), lambda bi, i, j: (bi, 0, 0)),
                  pl.BlockSpec((1, 1, d), lambda bi, i, j: (bi, 0, 0)),
                  pl.BlockSpec((d, tn), lambda bi, i, j: (0, j)),
                  pl.BlockSpec((1, tn), lambda bi, i, j: (0, j))],
        out_specs=pl.BlockSpec((1, tm, tn), lambda bi, i, j: (bi, i, j)),
        out_shape=jax.ShapeDtypeStruct((bsz, t, n), F32),
        scratch_shapes=[pltpu.VMEM((tm, d), BF16)],
        compiler_params=_cparams(("parallel", "parallel", "arbitrary")),
        name="norm_inproj",
    )(x, g, scale, shift, w, b)


def _mlstm_kernel(x_ref, v_ref, o_ref, z_ref, gt_ref, cw_ref, cb_ref, wq_ref, wk_ref, nw_ref, sk_ref, fb_ref,
                  y_ref, c_scr, n_scr, m_scr, xp_scr):
    L = M_CHUNK

    @pl.when(pl.program_id(1) == 0)
    def _():
        c_scr[...] = jnp.zeros_like(c_scr)
        n_scr[...] = jnp.zeros_like(n_scr)
        m_scr[...] = jnp.zeros_like(m_scr)
        xp_scr[...] = jnp.zeros_like(xp_scr)

    x = x_ref[0]
    prev = xp_scr[...]
    row8 = lax.broadcasted_iota(jnp.int32, (8, M_WIDTH), 0)
    cw = cw_ref[...]
    xc = cb_ref[...] + x * cw[CONV_K - 1:CONV_K, :]
    for sft in range(1, CONV_K):
        xr = pltpu.roll(x, sft, 0)
        top = jnp.where(row8 < sft, pltpu.roll(prev, sft, 0), xr[0:8])
        xs = jnp.concatenate([top, xr[8:]], axis=0)
        xc = xc + xs * cw[CONV_K - 1 - sft:CONV_K - sft, :]
    xp_scr[...] = x[L - 8:L]
    xc = _silu(xc)

    gt = gt_ref[0]
    col = lax.broadcasted_iota(jnp.int32, (L, LANES), 1)
    logf = _log_sigmoid(gt + fb_ref[...])
    a_c = jnp.where((col >= M_HEADS) & (col < 2 * M_HEADS), logf, gt)
    ri = lax.broadcasted_iota(jnp.int32, (L, L), 0)
    ci = lax.broadcasted_iota(jnp.int32, (L, L), 1)
    causal = ri >= ci
    hp = lax.Precision.HIGHEST
    tri = causal.astype(F32)
    tri_t = (ri <= ci).astype(F32)
    b_c = jnp.dot(tri, a_c, precision=hp, preferred_element_type=F32)
    a_r = a_c.T
    b_r = jnp.dot(a_r[0:8], tri_t, precision=hp, preferred_element_type=F32)

    for h in range(M_HEADS):
        sl = slice(h * M_HD, (h + 1) * M_HD)
        xh = xc[:, sl]
        xb = xh.astype(BF16)
        q = jnp.dot(xb, wq_ref[h], preferred_element_type=F32)
        k = jnp.dot(xb, wk_ref[h], preferred_element_type=F32) * (M_HD ** -0.5)
        vb = v_ref[0, :, sl].astype(BF16)
        qb = q.astype(BF16)
        kb = k.astype(BF16)

        bt = b_c[:, M_HEADS + h:M_HEADS + h + 1]
        ic = a_c[:, h:h + 1]
        bs = b_r[M_HEADS + h:M_HEADS + h + 1, :]
        ir = a_r[h:h + 1, :]
        m_prev = m_scr[h][:, 0:1]

        dm = jnp.where(causal, bt - bs + ir, -jnp.inf)
        inter = bt + m_prev
        m_t = jnp.maximum(inter, jnp.max(dm, axis=-1, keepdims=True))
        w_in = jnp.exp(dm - m_t)
        w_st = jnp.exp(inter - m_t)
        s = lax.dot_general(qb, kb, (((1,), (1,)), ((), ())), preferred_element_type=F32) * w_in
        cmat = c_scr[h]
        nvec = n_scr[h]
        sb = s.astype(BF16)
        num = w_st * jnp.dot(qb, cmat.astype(BF16), preferred_element_type=F32) \
            + jnp.dot(sb, vb, preferred_element_type=F32)
        nt_dims = (((1,), (1,)), ((), ()))
        qn = lax.dot_general(qb, jnp.broadcast_to(nvec, (8, M_HD)).astype(BF16), nt_dims,
                             preferred_element_type=F32)[:, 0:1]
        ssum = lax.dot_general(sb, jnp.ones((8, L), BF16), nt_dims, preferred_element_type=F32)[:, 0:1]
        den = w_st * qn + ssum
        hh = num / jnp.maximum(jnp.abs(den), jnp.exp(-m_t))

        b_last = bt[L - 1:L, :]
        w_end = b_last - bt + ic
        m_new = jnp.maximum(b_last + m_prev, jnp.max(w_end, axis=0, keepdims=True))
        decay = jnp.exp(b_last + m_prev - m_new)
        kwt = k * jnp.exp(w_end - m_new)
        c_scr[h] = decay * cmat + lax.dot_general(kwt.astype(BF16), vb, (((0,), (0,)), ((), ())),
                                                  preferred_element_type=F32)
        n_scr[h] = decay * nvec + jnp.sum(kwt, axis=0, keepdims=True)
        m_scr[h] = jnp.broadcast_to(m_new, (1, LANES))

        mu = jnp.mean(hh, axis=-1, keepdims=True)
        hc = hh - mu
        var = jnp.mean(hc * hc, axis=-1, keepdims=True)
        hn = hc * lax.rsqrt(var + EPS) * nw_ref[:, sl]
        out = jax.nn.sigmoid(o_ref[0, :, sl]) * hn + sk_ref[:, sl] * xh
        y_ref[0, :, sl] = (out * _silu(z_ref[0, :, sl])).astype(BF16)


def _mlstm(p, conv_w, conv_b, wq, wk, norm_w, skip, fb_row):
    bsz, t, _ = p.shape
    L = M_CHUNK
    cb = lambda c: (lambda bi, i: (bi, i, c))
    full2 = lambda bi, i: (0, 0)
    full3 = lambda bi, i: (0, 0, 0)
    return pl.pallas_call(
        _mlstm_kernel,
        grid=(bsz, t // L),
        in_specs=[pl.BlockSpec((1, L, M_WIDTH), cb(COL_MX // M_WIDTH)),
                  pl.BlockSpec((1, L, M_WIDTH), cb(COL_MV // M_WIDTH)),
                  pl.BlockSpec((1, L, M_WIDTH), cb(COL_MO // M_WIDTH)),
                  pl.BlockSpec((1, L, M_WIDTH), cb(COL_MZ // M_WIDTH)),
                  pl.BlockSpec((1, L, LANES), cb(COL_GATES // LANES)),
                  pl.BlockSpec((CONV_K, M_WIDTH), full2),
                  pl.BlockSpec((1, M_WIDTH), full2),
                  pl.BlockSpec((M_HEADS, M_HD, M_HD), full3),
                  pl.BlockSpec((M_HEADS, M_HD, M_HD), full3),
                  pl.BlockSpec((1, M_WIDTH), full2),
                  pl.BlockSpec((1, M_WIDTH), full2),
                  pl.BlockSpec((1, LANES), full2)],
        out_specs=pl.BlockSpec((1, L, M_WIDTH), lambda bi, i: (bi, i, 0)),
        out_shape=jax.ShapeDtypeStruct((bsz, t, M_WIDTH), BF16),
        scratch_shapes=[pltpu.VMEM((M_HEADS, M_HD, M_HD), F32),
                        pltpu.VMEM((M_HEADS, 1, M_HD), F32),
                        pltpu.VMEM((M_HEADS, 1, LANES), F32),
                        pltpu.VMEM((8, M_WIDTH), F32)],
        compiler_params=_cparams(("parallel", "arbitrary")),
        name="mlstm_group",
    )(p, p, p, p, p, conv_w, conv_b, wq, wk, norm_w, skip, fb_row)


def _compress_kernel(kc0_ref, kc1_ref, vc0_ref, vc1_ref, posk_ref, posv_ref, w1k_ref, w1kc_ref, w2k_ref,
                     w1v_ref, w1vc_ref, w2v_ref, kcmp_ref, vcmpt_ref):
    nb = kcmp_ref.shape[2]

    def hidden(src_refs, pos_ref, w1_ref, w1c_ref):
        halves = [jnp.zeros((nb, 4 * CMP_HIDDEN), F32) for _ in src_refs]
        for l in range(CMP_STRIDE):
            for hf, r in enumerate(src_refs):
                xl = r[0, pl.ds(l, nb, stride=CMP_STRIDE), :]
                halves[hf] = halves[hf] + jnp.dot(xl.astype(BF16), w1c_ref[l], preferred_element_type=F32)
        acc = jnp.concatenate([hv[:, c * 2 * CMP_HIDDEN:(c + 1) * 2 * CMP_HIDDEN] for hv in halves for c in range(2)],
                              axis=0)
        first = acc[:, :CMP_HIDDEN]
        second = acc[:, CMP_HIDDEN:]
        posb = jnp.dot(jnp.broadcast_to(pos_ref[...], (8, CMP_LEN * N_HD)).astype(BF16), w1_ref[...],
                       preferred_element_type=F32)[0:1]
        hid = first + pltpu.roll(second, N_KV * nb - 1, 0) + posb
        return jax.nn.gelu(hid).astype(BF16)

    hk = hidden((kc0_ref, kc1_ref), posk_ref, w1k_ref, w1kc_ref)
    kc = jnp.dot(hk, w2k_ref[...], preferred_element_type=F32)
    hv = hidden((vc0_ref, vc1_ref), posv_ref, w1v_ref, w1vc_ref)
    vc = jnp.dot(hv, w2v_ref[...], preferred_element_type=F32)
    j = lax.broadcasted_iota(jnp.int32, (nb, 1), 0)
    lane = lax.broadcasted_iota(jnp.int32, (nb, N_HD), 1)
    n_real = (nb * CMP_STRIDE - CMP_LEN) // CMP_STRIDE + 1
    aux = jnp.where(lane < 3, (j * CMP_STRIDE).astype(F32),
                    jnp.where(lane < 6, (CMP_LEN - 1) * 0.5,
                              jnp.where((lane == 6) & (j >= n_real), 1.0, 0.0))).astype(BF16)
    ones_blk = jnp.where(lax.broadcasted_iota(jnp.int32, (VROWS - N_HD, nb), 0) == 0, 1.0, 0.0)
    for g in range(N_KV):
        kg = kc[g * nb:(g + 1) * nb, 0:N_HD]
        k_hi = kg.astype(BF16)
        k_lo = (kg - k_hi.astype(F32)).astype(BF16)
        kcmp_ref[0, g] = jnp.concatenate([k_hi, k_lo, k_hi, aux], axis=1)
        vt = vc[g * nb:(g + 1) * nb].T
        vcmpt_ref[0, g] = jnp.concatenate([vt[0:N_HD], ones_blk], axis=0).astype(BF16)


def _compress(p, posk, posv, w1k, w1kc, w2k, w1v, w1vc, w2v):
    bsz, t, _ = p.shape
    nb = t // CMP_STRIDE
    cb = lambda c: (lambda bi: (bi, 0, c))
    f2 = lambda bi: (0, 0)
    f3 = lambda bi: (0, 0, 0)
    wspecs = [pl.BlockSpec((CMP_LEN * N_HD, CMP_HIDDEN), f2),
              pl.BlockSpec((CMP_STRIDE, LANES, 4 * CMP_HIDDEN), f3),
              pl.BlockSpec((CMP_HIDDEN, LANES), f2)]
    return pl.pallas_call(
        _compress_kernel,
        grid=(bsz,),
        in_specs=[pl.BlockSpec((1, t, LANES), cb(COL_KC // LANES)),
                  pl.BlockSpec((1, t, LANES), cb(COL_KC // LANES + 1)),
                  pl.BlockSpec((1, t, LANES), cb(COL_VC // LANES)),
                  pl.BlockSpec((1, t, LANES), cb(COL_VC // LANES + 1)),
                  pl.BlockSpec((1, CMP_LEN * N_HD), f2),
                  pl.BlockSpec((1, CMP_LEN * N_HD), f2)] + wspecs + wspecs,
        out_specs=[pl.BlockSpec((1, N_KV, nb, 4 * N_HD), lambda bi: (bi, 0, 0, 0)),
                   pl.BlockSpec((1, N_KV, VROWS, nb), lambda bi: (bi, 0, 0, 0))],
        out_shape=[jax.ShapeDtypeStruct((bsz, N_KV, nb, 4 * N_HD), BF16),
                   jax.ShapeDtypeStruct((bsz, N_KV, VROWS, nb), BF16)],
        compiler_params=_cparams(("parallel",)),
        name="nsa_compress",
    )(p, p, p, p, posk, posv, w1k, w1kc, w2k, w1v, w1vc, w2v)


def _relayout_kernel(ks_ref, vs_ref, kw_ref, vw_ref, ksa_ref, vsa_ref, kwa_ref, vwa_ref):
    i = pl.program_id(1)
    is_pad = i == 0
    flag = jnp.where(is_pad, 1.0, 0.0)
    row = lax.broadcasted_iota(jnp.int32, (SKT, 1), 0)
    lane = lax.broadcasted_iota(jnp.int32, (SKT, N_HD), 1)
    al = lane - AUX_ROWS
    ones_blk = jnp.where(lax.broadcasted_iota(jnp.int32, (VROWS - N_HD, SKT), 0) == 0, 1.0, 0.0)
    for u in range(PAD_TILES):
        rows = slice(u * SKT, (u + 1) * SKT)
        base = ((i - 1) * PAD_TILES + u) * SKT
        pos = jnp.where(is_pad, 0, base + row)
        blk = pos // SLC_LEN
        p_hi = (blk * SLC_LEN).astype(F32)
        p_lo = (pos - blk * SLC_LEN).astype(F32)
        mid_w = jnp.where((al >= 0) & (al < 3), p_hi,
                          jnp.where((al >= 3) & (al < 6), p_lo, jnp.where(al == 6, flag, 0.0)))
        mid_s = jnp.where(lane == blk % SEL_GRP, 1.0, mid_w)
        ks = ks_ref[0, rows, :]
        kw = kw_ref[0, rows, :]
        for g in range(N_KV):
            sl = slice(g * N_HD, (g + 1) * N_HD)
            ksa_ref[0, g, rows, :] = jnp.concatenate([ks[:, sl], mid_s], axis=1).astype(BF16)
            kwa_ref[0, g, rows, :] = jnp.concatenate([kw[:, sl], mid_w], axis=1).astype(BF16)
        vst = vs_ref[0, rows, :].T
        vwt = vw_ref[0, rows, :].T
        for g in range(N_KV):
            sl = slice(g * N_HD, (g + 1) * N_HD)
            vsa_ref[0, g, u] = jnp.concatenate([vst[sl], ones_blk], axis=0).astype(BF16)
            vwg = jnp.concatenate([vwt[sl], ones_blk], axis=0).astype(BF16)
            for j in range(SKT // WKT):
                vwa_ref[0, g, u * (SKT // WKT) + j] = vwg[:, j * WKT:(j + 1) * WKT]


def _relayout(p):
    bsz, t, _ = p.shape
    nt = t // SKT + PAD_TILES
    wpt = SKT // WKT
    rt = PAD_TILES
    cb = lambda c: (lambda bi, i: (bi, jnp.maximum(i - 1, 0), c))
    return pl.pallas_call(
        _relayout_kernel,
        grid=(bsz, nt // rt),
        in_specs=[pl.BlockSpec((1, rt * SKT, KV_W), cb(COL_KS // KV_W)),
                  pl.BlockSpec((1, rt * SKT, KV_W), cb(COL_VS // KV_W)),
                  pl.BlockSpec((1, rt * SKT, KV_W), cb(COL_KW // KV_W)),
                  pl.BlockSpec((1, rt * SKT, KV_W), cb(COL_VW // KV_W))],
        out_specs=[pl.BlockSpec((1, N_KV, rt * SKT, KA_SEL), lambda bi, i: (bi, 0, i, 0)),
                   pl.BlockSpec((1, N_KV, rt, VROWS, SKT), lambda bi, i: (bi, 0, i, 0, 0)),
                   pl.BlockSpec((1, N_KV, rt * SKT, KA_WIN), lambda bi, i: (bi, 0, i, 0)),
                   pl.BlockSpec((1, N_KV, rt * wpt, VROWS, WKT), lambda bi, i: (bi, 0, i, 0, 0))],
        out_shape=[jax.ShapeDtypeStruct((bsz, N_KV, nt * SKT, KA_SEL), BF16),
                   jax.ShapeDtypeStruct((bsz, N_KV, nt, VROWS, SKT), BF16),
                   jax.ShapeDtypeStruct((bsz, N_KV, nt * SKT, KA_WIN), BF16),
                   jax.ShapeDtypeStruct((bsz, N_KV, nt * wpt, VROWS, WKT), BF16)],
        compiler_params=_cparams(("parallel", "parallel")),
        name="nsa_relayout",
    )(p, p, p, p)


def _nsa_batch_kernel(sp_ref, q_ref, gt_ref, z_ref, kcmp_ref, vcmpt_ref, ovt_ref, ks_ref, vs_ref, kw_ref,
                      vw_ref, tri_ref, wb_ref, y_ref, sa_scr, sb_scr, g_scr, imp_scr, rank_scr, part_scr, selb_scr,
                      tile_idx):
    g = pl.program_id(0)
    qb = pl.program_id(1)
    n_items = q_ref.shape[0]
    start = qb * QBLK
    nb = kcmp_ref.shape[2]
    n_slc = ovt_ref.shape[0]
    nw = N_HPG * QBLK
    n_top = min(SLC_TOPN, n_slc)
    t_row = start + lax.broadcasted_iota(jnp.int32, (1, QBLK), 1)
    s_bufs = (sa_scr, sb_scr)
    last = qb // (SKT // QBLK)

    def tile4(a):
        return jnp.concatenate([a] * N_HPG, axis=1)

    def col_reduce8(s, op):
        out = s[0:8]
        for r in range(1, s.shape[0] // 8):
            out = op(out, s[8 * r:8 * r + 8])
        return out

    def col_max(s):
        return jnp.max(col_reduce8(s, jnp.maximum), axis=0, keepdims=True)

    def normalise(acc):
        den = acc[N_HD:N_HD + 1]
        return acc[0:N_HD] / jnp.where(den > 0, den, 1.0)

    def gate_row(i, h, branch):
        return g_scr[i, pl.ds(GATE_NG + (g * N_HPG + h) * 3 + branch, 1), :]

    hl = lax.broadcasted_iota(jnp.int32, (AUX_ROWS, nw), 1) // QBLK
    ar = lax.broadcasted_iota(jnp.int32, (AUX_ROWS, nw), 0)

    def per_head(vals):
        out = jnp.full((AUX_ROWS, nw), vals[N_HPG - 1], F32)
        for h in range(N_HPG - 2, -1, -1):
            out = jnp.where(hl == h, vals[h], out)
        return out

    pieces = [per_head([sp_ref[(g * N_HPG + h) * 3 + j] for h in range(N_HPG)]) for j in range(3)]
    aux = jnp.where(ar == 6, NEG, 0.0)
    for j in range(3):
        aux = jnp.where((ar == j) | (ar == j + 3), pieces[j], aux)
    aux_b = aux.astype(BF16)
    q_tail = jnp.zeros((KA_SEL - N_HD - 2 * AUX_ROWS, nw), BF16)
    sidx = lax.broadcasted_iota(jnp.int32, (n_slc, QBLK), 0)
    cur = t_row // SLC_LEN
    valid = sidx <= cur
    sub8 = lax.broadcasted_iota(jnp.int32, (8, QBLK), 0)
    last_blk = (start + QBLK - 1) // SLC_LEN
    cmp_end = lax.broadcasted_iota(jnp.int32, (nb, 1), 0) * CMP_STRIDE + (CMP_LEN - 1)
    cmp_bias = tile4(jnp.where(cmp_end <= t_row, 0.0, NEG))
    wrows = WIN + QBLK

    def prepare(i):
        qt = (q_ref[i] * (N_HD ** -0.5)).T
        q4l = jnp.concatenate([qt[h * N_HD:(h + 1) * N_HD] for h in range(N_HPG)], axis=1) * LOG2E
        q4s = q4l.astype(BF16)
        q4lo = (q4l - q4s.astype(F32)).astype(BF16)
        g_scr[i] = jax.nn.sigmoid(gt_ref[i]).T

        q_cmp = jnp.concatenate([q4s, q4s, q4lo, aux_b, jnp.zeros((N_HD - AUX_ROWS, nw), BF16)], axis=0)
        s_c = jnp.dot(kcmp_ref[i, 0], q_cmp, preferred_element_type=F32) + cmp_bias
        e_c = jnp.exp2(s_c - col_max(s_c))
        z_c = jnp.sum(col_reduce8(e_c, jnp.add), axis=0, keepdims=True)
        inv_c = jnp.where(tile4(t_row >= CMP_LEN - 1) & (z_c > 0), 1.0 / z_c, 0.0)
        o_c = jnp.dot(vcmpt_ref[i, 0], e_c.astype(BF16), preferred_element_type=F32)[0:N_HD] * inv_c
        p_c = e_c * inv_c
        psum = p_c[:, 0:QBLK]
        for h in range(1, N_HPG):
            psum = psum + p_c[:, h * QBLK:(h + 1) * QBLK]
        parts, rest = [], psum
        for _ in range(3):
            parts.append(rest.astype(BF16))
            rest = rest - parts[-1].astype(F32)
        imp = jnp.dot(ovt_ref[...], jnp.concatenate(parts, axis=0), preferred_element_type=F32)
        imp = jnp.where(valid, imp, -jnp.inf)
        imp_scr[...] = jnp.where((sidx == 0) | (sidx == cur), jnp.inf, imp)

        q_win = jnp.concatenate([q4s, jnp.zeros((AUX_ROWS, nw), BF16), aux_b,
                                 jnp.zeros((KA_WIN - N_HD - 2 * AUX_ROWS, nw), BF16)], axis=0)
        s_w = jnp.dot(kw_ref[i, 0, pl.ds(pl.multiple_of(start, QBLK), wrows), :], q_win,
                      preferred_element_type=F32)
        s_w = jnp.concatenate([s_w[0:WKT] + tile4(wb_ref[0]), s_w[WKT:wrows - WKT],
                               s_w[wrows - WKT:] + tile4(wb_ref[1])], axis=0)
        p_w = jnp.exp2(s_w - col_max(s_w)).astype(BF16)
        v_w = jnp.concatenate([vw_ref[i, 0, qb * (QBLK // WKT) + j] for j in range(wrows // WKT)], axis=1)
        o_w = normalise(jnp.dot(v_w, p_w, preferred_element_type=F32))
        for h in range(N_HPG):
            hs = slice(h * QBLK, (h + 1) * QBLK)
            part_scr[i, :, hs] = gate_row(i, h, 0) * o_c[:, hs] + gate_row(i, h, 2) * o_w[:, hs]

        rank_scr[...] = jnp.zeros_like(rank_scr)
        for ri in range(n_slc // 8):
            @pl.when((last_blk >= n_top) & (8 * ri <= last_blk))
            def _():
                rows = imp_scr[8 * ri:8 * ri + 8]
                for r in range(n_slc // 8):
                    blk8 = imp_scr[8 * r:8 * r + 8]
                    acc = rank_scr[8 * r:8 * r + 8]
                    for ii in range(8):
                        row = rows[ii:ii + 1]
                        if ri < r:
                            before = row >= blk8
                        elif ri > r:
                            before = row > blk8
                        else:
                            before = (row > blk8) | ((row == blk8) & (sub8 > ii))
                        acc = acc + jnp.where(before, 1.0, 0.0)
                    rank_scr[8 * r:8 * r + 8] = acc
        chosen = (rank_scr[...] < n_top) & valid
        selb_scr[i] = tile4(jnp.where(chosen, 0.0, NEG))

        any_q = jnp.max(jnp.where(chosen, 1.0, 0.0), axis=1, keepdims=True) > 0
        s_io = lax.broadcasted_iota(jnp.int32, (n_slc, 1), 0)
        bits = jnp.where(any_q, jnp.left_shift(1, s_io % 32), 0)
        words = [jnp.sum(jnp.where(s_io // 32 == w, bits, 0)) for w in range((n_slc + 31) // 32)]
        bpt = SKT // SLC_LEN
        cnt = jnp.int32(0)
        for tile in range(n_slc // bpt):
            nib = lax.shift_right_logical(words[tile * bpt // 32], jnp.int32(tile * bpt % 32)) & (2 ** bpt - 1)
            tile_idx[i, cnt] = tile
            cnt = cnt + ((nib != 0) & (tile < last)).astype(jnp.int32)
        return q4s, cnt

    q4, n_act = zip(*[prepare(i) for i in range(n_items)])


    def sel_scores(i, kt):
        grp = pl.multiple_of((kt * SKT // SLC_LEN) // SEL_GRP * SEL_GRP, SEL_GRP)
        sel_rows = jnp.concatenate([selb_scr[i, pl.ds(grp, SEL_GRP), :], jnp.zeros((AUX_ROWS - SEL_GRP, nw), F32)],
                                   axis=0)
        q_sel = jnp.concatenate([q4[i], sel_rows.astype(BF16), aux_b, q_tail], axis=0)
        krow = pl.multiple_of((kt + PAD_TILES) * SKT, SKT)
        return jnp.dot(ks_ref[i, 0, pl.ds(krow, SKT), :], q_sel, preferred_element_type=F32)

    def tile_rows(kt):
        return pl.ds(pl.multiple_of(kt * SKT, SKT), SKT)

    def pass1(i):
        def fn(kt, mrun):
            s = sel_scores(i, kt)
            s_bufs[i % 2][tile_rows(kt), :] = s
            return jnp.maximum(mrun, col_reduce8(s, jnp.maximum))
        return fn

    def pass2(i, m_sel):
        def fn(kt, acc):
            pr = jnp.exp2(s_bufs[i % 2][tile_rows(kt), :] - m_sel).astype(BF16)
            return acc + jnp.dot(vs_ref[i, 0, kt + PAD_TILES], pr, preferred_element_type=F32)
        return fn

    def listed(i, fn):
        return lambda j, carry: fn(tile_idx[i, j], carry)

    def both(f1, f2):
        def fn(j, carry):
            return f1(j, carry[0]), f2(j, carry[1])
        return fn

    def run_tiles(fn, lo, n, init):
        def group(size, first):
            def body(j, carry):
                for u in range(size):
                    carry = fn(first + j * size + u, carry)
                return carry
            return body

        n_u = n // UNROLL
        carry = lax.fori_loop(0, n_u, group(UNROLL, lo), init)
        done = lo + n_u * UNROLL
        size = UNROLL // 2
        while size >= 1:
            take = (n // size) % 2
            carry = lax.fori_loop(0, take, group(size, done), carry)
            done = done + take * size
            size //= 2
        return carry

    def diag_tile(i, mrun):
        s = sel_scores(i, last) + tile4(tri_ref[qb % (SKT // QBLK)])
        s_bufs[i % 2][tile_rows(last), :] = s
        return jnp.max(jnp.maximum(mrun, col_reduce8(s, jnp.maximum)), axis=0, keepdims=True)

    mrun0 = jnp.full((8, nw), NEG, F32)
    acc0 = jnp.zeros((VROWS, nw), F32)
    m_prev = diag_tile(0, run_tiles(listed(0, pass1(0)), 0, n_act[0], mrun0))
    outs = []
    for i in range(1, n_items):
        p2 = pass2(i - 1, m_prev)
        f1, f2 = listed(i, pass1(i)), listed(i - 1, p2)
        n_both = jnp.minimum(n_act[i], n_act[i - 1])
        mrun, acc = run_tiles(both(f1, f2), 0, n_both, (mrun0, acc0))
        mrun = run_tiles(f1, n_both, n_act[i] - n_both, mrun)
        acc = run_tiles(f2, n_both, n_act[i - 1] - n_both, acc)
        m_cur = diag_tile(i, mrun)
        outs.append(normalise(p2(last, acc)))
        m_prev = m_cur
    p2 = pass2(n_items - 1, m_prev)
    acc = run_tiles(listed(n_items - 1, p2), 0, n_act[n_items - 1], acc0)
    outs.append(normalise(p2(last, acc)))

    for i in range(n_items):
        tot = [part_scr[i, :, h * QBLK:(h + 1) * QBLK] + gate_row(i, h, 1) * outs[i][:, h * QBLK:(h + 1) * QBLK]
               for h in range(N_HPG)]
        o = jnp.concatenate(tot, axis=0).T
        y_ref[i] = (o * _silu(z_ref[i])).astype(BF16)


def _edge_biases():
    kl = np.arange(SKT)[:, None]
    ql = np.arange(QBLK)[None, :]
    tri = np.stack([np.where(kl <= par * QBLK + ql, 0.0, NEG) for par in range(SKT // QBLK)])
    kk = np.arange(WKT)[:, None]
    wb = np.stack([np.where(kk > ql, 0.0, NEG), np.where(kk <= ql + WKT - QBLK, 0.0, NEG)])
    return jnp.asarray(tri, F32), jnp.asarray(wb, F32)


def _nsa(p, spieces, kcmp, vcmpt, ovt, ks, vs, kw, vw):
    bsz, t, _ = p.shape
    n_slc = t // SLC_LEN
    gw = N_HPG * N_HD
    tri, wb = _edge_biases()
    nw = N_HPG * QBLK
    per_g = lambda a: pl.BlockSpec((bsz, 1) + a.shape[2:], lambda g, i, sp: (0, g) + (0,) * (a.ndim - 2),
                                   pipeline_mode=pl.Buffered(1))
    const = lambda a: pl.BlockSpec(a.shape, lambda g, i, sp: (0,) * a.ndim)
    grid_spec = pltpu.PrefetchScalarGridSpec(
        num_scalar_prefetch=1,
        grid=(N_KV, t // QBLK),
        in_specs=[pl.BlockSpec((bsz, QBLK, gw), lambda g, i, sp: (0, i, COL_NQ // gw + g)),
                  pl.BlockSpec((bsz, QBLK, LANES), lambda g, i, sp: (0, i, COL_GATES // LANES)),
                  pl.BlockSpec((bsz, QBLK, gw), lambda g, i, sp: (0, i, COL_NZ // gw + g)),
                  per_g(kcmp), per_g(vcmpt), const(ovt),
                  per_g(ks), per_g(vs), per_g(kw), per_g(vw), const(tri), const(wb)],
        out_specs=pl.BlockSpec((bsz, QBLK, gw), lambda g, i, sp: (0, i, g)),
        scratch_shapes=[pltpu.VMEM((t, nw), F32),
                        pltpu.VMEM((t, nw), F32),
                        pltpu.VMEM((bsz, LANES, QBLK), F32),
                        pltpu.VMEM((n_slc, QBLK), F32),
                        pltpu.VMEM((n_slc, QBLK), F32),
                        pltpu.VMEM((bsz, N_HD, nw), F32),
                        pltpu.VMEM((bsz, n_slc, nw), F32),
                        pltpu.SMEM((bsz, t // SKT), jnp.int32)],
    )
    return pl.pallas_call(
        _nsa_batch_kernel,
        grid_spec=grid_spec,
        out_shape=jax.ShapeDtypeStruct((bsz, t, N_WIDTH), BF16),
        compiler_params=_cparams(("arbitrary", "arbitrary")),
        name="nsa_attention",
    )(spieces, p, p, p, kcmp, vcmpt, ovt, ks, vs, kw, vw, tri, wb)


def _outproj_kernel(ym_ref, yn_ref, w_ref, x_ref, gate_ref, fg_ref, o_ref, wb_scr, *, final):
    @pl.when((pl.program_id(0) == 0) & (pl.program_id(1) == 0))
    def _():
        rows = w_ref.shape[1]
        for c in range(rows // WP_TILE):
            cs = slice(c * WP_TILE, (c + 1) * WP_TILE)
            wb_scr[cs, :] = w_ref[0, cs, :].astype(BF16)

    y = jnp.dot(ym_ref[0], wb_scr[0:M_WIDTH, :], preferred_element_type=F32)
    y = y + jnp.dot(yn_ref[0], wb_scr[M_WIDTH:, :], preferred_element_type=F32)
    hres = x_ref[0] + gate_ref[0] * y
    if final:
        ms = jnp.mean(hres * hres, axis=-1, keepdims=True)
        hres = hres * lax.rsqrt(ms + EPS) * fg_ref[...]
    o_ref[0] = hres


def _outproj(ym, yn, w, layer, x, gate, fg, final):
    bsz, t, d = x.shape
    tm = 512
    return pl.pallas_call(
        functools.partial(_outproj_kernel, final=final),
        grid=(bsz, t // tm),
        in_specs=[pl.BlockSpec((1, tm, M_WIDTH), lambda bi, i: (bi, i, 0)),
                  pl.BlockSpec((1, tm, N_WIDTH), lambda bi, i: (bi, i, 0)),
                  pl.BlockSpec((1, M_WIDTH + N_WIDTH, d), lambda bi, i: (layer, 0, 0),
                               pipeline_mode=pl.Buffered(1)),
                  pl.BlockSpec((1, tm, d), lambda bi, i: (bi, i, 0)),
                  pl.BlockSpec((1, 1, d), lambda bi, i: (bi, 0, 0)),
                  pl.BlockSpec((1, d), lambda bi, i: (0, 0))],
        out_specs=pl.BlockSpec((1, tm, d), lambda bi, i: (bi, i, 0)),
        out_shape=jax.ShapeDtypeStruct((bsz, t, d), F32),
        scratch_shapes=[pltpu.VMEM((M_WIDTH + N_WIDTH, d), BF16)],
        compiler_params=_cparams(("arbitrary", "arbitrary")),
        name="outproj_residual",
    )(ym, yn, w, x, gate, fg)


SRC_MI = 4 * M_WIDTH
SRC_NQ = SRC_MI + 2 * M_HEADS
SRC_NG = SRC_NQ + N_WIDTH + 6 * KV_W
SRC_NZ = SRC_NG + 3 * N_HEADS


def _reorder_cols(a):
    parts = [a[..., 0:SRC_MI], a[..., SRC_NQ:SRC_NG], a[..., SRC_NZ:SRC_NZ + N_WIDTH], a[..., SRC_MI:SRC_NQ],
             a[..., SRC_NG:SRC_NZ]]
    used = sum(x.shape[-1] for x in parts)
    parts.append(jnp.zeros(a.shape[:-1] + (NP_PAD - used,), a.dtype))
    return jnp.concatenate(parts, axis=-1)


WP_TILE = 512
WP_GATE_TILE = COL_GATES // WP_TILE


def _wprep_kernel(w_ref, g1_ref, g2_ref, o_ref):
    j = pl.program_id(0)
    d = o_ref.shape[0]

    @pl.when(j < WP_GATE_TILE)
    def _():
        for c in range(d // WP_TILE):
            cs = slice(c * WP_TILE, (c + 1) * WP_TILE)
            o_ref[cs, :] = w_ref[0, :, cs].T.astype(BF16)

    @pl.when(j == WP_GATE_TILE)
    def _():
        n_gate = g1_ref.shape[1] + g2_ref.shape[1]
        gt = jnp.concatenate([g1_ref[0], g2_ref[0], jnp.zeros((LANES - n_gate, d), F32)], axis=0)
        for c in range(d // WP_TILE):
            cs = slice(c * WP_TILE, (c + 1) * WP_TILE)
            o_ref[cs, 0:LANES] = gt[:, cs].T.astype(BF16)
        o_ref[:, LANES:] = jnp.zeros((d, WP_TILE - LANES), BF16)


def _wprep(w_t, layer):
    _, n, d = w_t.shape
    n_big = COL_GATES // WP_TILE
    assert n == SRC_NZ + N_WIDTH and NP_PAD // WP_TILE == n_big + 1

    def src_row(j):
        return jnp.where(j < COL_NQ // WP_TILE, j * WP_TILE,
                         jnp.where(j < COL_NZ // WP_TILE, SRC_NQ + (j - COL_NQ // WP_TILE) * WP_TILE,
                                   jnp.where(j < n_big, SRC_NZ + (j - COL_NZ // WP_TILE) * WP_TILE, 0)))

    el = pl.Element
    return pl.pallas_call(
        _wprep_kernel,
        grid=(NP_PAD // WP_TILE,),
        in_specs=[pl.BlockSpec((el(1), el(WP_TILE), el(d)), lambda j: (layer, pl.multiple_of(src_row(j), 8), 0)),
                  pl.BlockSpec((el(1), el(SRC_NQ - SRC_MI), el(d)), lambda j: (layer, SRC_MI, 0)),
                  pl.BlockSpec((el(1), el(SRC_NZ - SRC_NG), el(d)), lambda j: (layer, SRC_NG, 0))],
        out_specs=pl.BlockSpec((d, WP_TILE), lambda j: (0, j)),
        out_shape=jax.ShapeDtypeStruct((d, NP_PAD), BF16),
        compiler_params=_cparams(("parallel",)),
        name="inproj_weight_prep",
    )(w_t, w_t, w_t)


def _overlap_t(t):
    n_cmp_rows = t // CMP_STRIDE
    n_slc = t // SLC_LEN
    c0 = np.arange(n_cmp_rows) * CMP_STRIDE
    s0 = np.arange(n_slc) * SLC_LEN
    ov = (c0[None, :] <= s0[:, None] + SLC_LEN - 1) & (c0[None, :] + CMP_LEN - 1 >= s0[:, None])
    ov[:, (t - CMP_LEN) // CMP_STRIDE + 1:] = False
    return jnp.asarray(np.concatenate([ov] * 3, axis=1), BF16)


def kernel(x, c, ln_g, w_ada, b_ada, w_in, b_in, m_conv_w, m_conv_b, m_wq, m_wk, m_norm_w, m_skip, m_f_bias,
           n_pos_k, n_pos_v, n_w1_k, n_w2_k, n_w1_v, n_w2_v, w_out, final_g):
    out_dtype = x.dtype
    bsz, t, d = x.shape
    depth = ln_g.shape[0]
    h_res = x.astype(F32)
    assert bsz <= 2 and t % 1024 == 0 and t // CMP_STRIDE <= 256
    c_t =jnp.zeros((d, 8), F32).at[:, :bsz].set(c.astype(F32).T)
    slopes_np = np.array([2.0 ** (-8.0 * (h + 1) / N_HEADS) for h in range(N_HEADS)], np.float32)
    rest = (slopes_np.astype(np.float64) * LOG2E).astype(np.float32)
    pieces = []
    for _ in range(3):
        pieces.append(rest.astype(jnp.bfloat16).astype(np.float32))
        rest = rest - pieces[-1]
    spieces = jnp.asarray(np.stack(pieces, axis=1).reshape(-1))
    ovt = _overlap_t(t)

    def w1cat(w1):
        w = w1.reshape(2, CMP_STRIDE, N_HD, CMP_HIDDEN)
        w = jnp.concatenate([w[0], w[1]], axis=-1).astype(BF16)
        z = jnp.zeros_like(w)
        return jnp.concatenate([jnp.concatenate([w, z], axis=-1), jnp.concatenate([z, w], axis=-1)], axis=1)

    def w2pad(w2):
        return jnp.pad(w2, ((0, 0), (0, LANES - N_HD))).astype(BF16)

    for l in range(depth):
        mod = _ada(c_t, w_ada, b_ada[l][None, :], l, bsz)[:bsz]
        shift, scale, gate = mod[:, None, 0:d], mod[:, None, d:2 * d], mod[:, None, 2 * d:3 * d]
        p = _inproj(h_res, ln_g[l][None, :], scale, shift,
                    _wprep(jnp.swapaxes(w_in, 1, 2), l), _reorder_cols(b_in[l])[None, :])
        fb_row = jnp.zeros((1, LANES), F32).at[0, M_HEADS:2 * M_HEADS].set(m_f_bias[l])
        y_m = _mlstm(p, m_conv_w[l], m_conv_b[l][None, :], m_wq[l].astype(BF16), m_wk[l].astype(BF16),
                     m_norm_w[l][None, :], m_skip[l][None, :], fb_row)
        kcmp, vcmpt = _compress(p, n_pos_k[l].reshape(1, -1), n_pos_v[l].reshape(1, -1),
                                n_w1_k[l].astype(BF16), w1cat(n_w1_k[l]), w2pad(n_w2_k[l]),
                                n_w1_v[l].astype(BF16), w1cat(n_w1_v[l]), w2pad(n_w2_v[l]))
        ks, vs, kw, vw = _relayout(p)
        y_n = _nsa(p, spieces, kcmp, vcmpt, ovt, ks, vs, kw, vw)
        h_res = _outproj(y_m, y_n, w_out, l, h_res, gate, final_g[None, :], l == depth - 1)
    return h_res.astype(out_dtype)
```

```python
import functools

import numpy as np
import jax
import jax.numpy as jnp
from jax import lax
from jax.experimental import pallas as pl
from jax.experimental.pallas import tpu as pltpu

F32 = jnp.float32
BF16 = jnp.bfloat16

EPS = 1e-6
M_HEADS = 4
M_HD = 256
M_WIDTH = M_HEADS * M_HD
CONV_K = 4
M_CHUNK = 256
N_HEADS = 16
N_HD = 64
N_KV = 4
N_HPG = N_HEADS // N_KV
N_WIDTH = N_HEADS * N_HD
KV_W = N_KV * N_HD
CMP_LEN = 32
CMP_STRIDE = 16
CMP_HIDDEN = 2 * N_HD
SLC_LEN = 64
SLC_TOPN = 16
WIN = 512
QBLK = 256
SKT = 256
WKT = 256
PAD_TILES = WIN // SKT
SEL_GRP = 8
KA_SEL = 128
KA_WIN = 128
VROWS = 80
UNROLL = 8
AUX_ROWS = 16
LOG2E = 1.4426950408889634

COL_MX, COL_MV, COL_MO, COL_MZ = 0, 1024, 2048, 3072
COL_NQ = 4096
COL_KC, COL_VC, COL_KS, COL_VS, COL_KW, COL_VW = 5120, 5376, 5632, 5888, 6144, 6400
COL_NZ = 6656
COL_GATES = 7680
GATE_NG = 2 * M_HEADS
NP_PAD = 8192
NORM_CHUNKS = 4
LANES = 128
NEG = -1e30
VMEM_LIMIT = 56 * 1024 * 1024


def _cparams(sem):
    return pltpu.CompilerParams(dimension_semantics=sem, vmem_limit_bytes=VMEM_LIMIT)


def _silu(x):
    return x * jax.nn.sigmoid(x)


def _log_sigmoid(x):
    return jnp.minimum(x, 0.0) - jnp.log1p(jnp.exp(-jnp.abs(x)))


def _ada_kernel(ct_ref, w_ref, b_ref, o_ref, *, bsz):
    s_t = _silu(ct_ref[...])
    w = w_ref[0]
    row = lax.broadcasted_iota(jnp.int32, o_ref.shape, 0)
    out = jnp.zeros(o_ref.shape, F32)
    for b in range(bsz):
        prod = w * s_t[:, b:b + 1]
        acc = prod[0:8]
        for r in range(1, prod.shape[0] // 8):
            acc = acc + prod[8 * r:8 * r + 8]
        out = jnp.where(row == b, jnp.sum(acc, axis=0, keepdims=True) + b_ref[...], out)
    o_ref[...] = out


def _ada(c_t, w, b, layer, bsz):
    _, d, n = w.shape
    tn = 1024
    return pl.pallas_call(
        functools.partial(_ada_kernel, bsz=bsz),
        grid=(n // tn,),
        in_specs=[pl.BlockSpec((d, 8), lambda j: (0, 0)),
                  pl.BlockSpec((1, d, tn), lambda j: (layer, 0, j)),
                  pl.BlockSpec((1, tn), lambda j: (0, j))],
        out_specs=pl.BlockSpec((8, tn), lambda j: (0, j)),
        out_shape=jax.ShapeDtypeStruct((8, n), F32),
        compiler_params=_cparams(("parallel",)),
        name="ada_mod",
    )(c_t, w, b)


def _inproj_kernel(x_ref, g_ref, sc_ref, sh_ref, w_ref, b_ref, o_ref, h_ref):
    first = pl.program_id(2) == 0

    @pl.when(first)
    def _():
        tm = x_ref.shape[1]
        ck = tm // NORM_CHUNKS
        for c in range(NORM_CHUNKS):
            rows = slice(c * ck, (c + 1) * ck)
            x = x_ref[0, rows, :]
            ms = jnp.mean(x * x, axis=-1, keepdims=True)
            h = x * lax.rsqrt(ms + EPS) * g_ref[...]
            h = (h * (1.0 + sc_ref[0]) + sh_ref[0]).astype(BF16)
            h_ref[rows, :] = h
            o_ref[0, rows, :] = jnp.dot(h, w_ref[...], preferred_element_type=F32) + b_ref[...]

    @pl.when(jnp.logical_not(first))
    def _():
        o_ref[0] = jnp.dot(h_ref[...], w_ref[...], preferred_element_type=F32) + b_ref[...]


def _inproj(x, g, scale, shift, w, b):
    bsz, t, d = x.shape
    n = w.shape[1]
    tm, tn = 1024, 1024
    return pl.pallas_call(
        _inproj_kernel,
        grid=(bsz, t // tm, n // tn),
        in_specs=[pl.BlockSpec((1, tm, d), lambda bi, i, j: (bi, i, 0)),
                  pl.BlockSpec((1, d), lambda bi, i, j: (0, 0)),
                  pl.BlockSpec((1, 1, d), lambda bi, i, j: (bi, 0, 0)),
                  pl.BlockSpec((1, 1, d), lambda bi, i, j: (bi, 0, 0)),
                  pl.BlockSpec((d, tn), lambda bi, i, j: (0, j)),
                  pl.BlockSpec((1, tn), lambda bi, i, j: (0, j))],
        out_specs=pl.BlockSpec((1, tm, tn), lambda bi, i, j: (bi, i, j)),
        out_shape=jax.ShapeDtypeStruct((bsz, t, n), F32),
        scratch_shapes=[pltpu.VMEM((tm, d), BF16)],
        compiler_params=_cparams(("parallel", "parallel", "arbitrary")),
        name="norm_inproj",
    )(x, g, scale, shift, w, b)


def _mlstm_kernel(x_ref, v_ref, o_ref, z_ref, gt_ref, cw_ref, cb_ref, wq_ref, wk_ref, nw_ref, sk_ref, fb_ref,
                  y_ref, c_scr, n_scr, m_scr, xp_scr):
    L = M_CHUNK

    @pl.when(pl.program_id(1) == 0)
    def _():
        c_scr[...] = jnp.zeros_like(c_scr)
        n_scr[...] = jnp.zeros_like(n_scr)
        m_scr[...] = jnp.zeros_like(m_scr)
        xp_scr[...] = jnp.zeros_like(xp_scr)

    x = x_ref[0]
    prev = xp_scr[...]
    row8 = lax.broadcasted_iota(jnp.int32, (8, M_WIDTH), 0)
    cw = cw_ref[...]
    xc = cb_ref[...] + x * cw[CONV_K - 1:CONV_K, :]
    for sft in range(1, CONV_K):
        xr = pltpu.roll(x, sft, 0)
        top = jnp.where(row8 < sft, pltpu.roll(prev, sft, 0), xr[0:8])
        xs = jnp.concatenate([top, xr[8:]], axis=0)
        xc = xc + xs * cw[CONV_K - 1 - sft:CONV_K - sft, :]
    xp_scr[...] = x[L - 8:L]
    xc = _silu(xc)

    gt = gt_ref[0]
    col = lax.broadcasted_iota(jnp.int32, (L, LANES), 1)
    logf = _log_sigmoid(gt + fb_ref[...])
    a_c = jnp.where((col >= M_HEADS) & (col < 2 * M_HEADS), logf, gt)
    ri = lax.broadcasted_iota(jnp.int32, (L, L), 0)
    ci = lax.broadcasted_iota(jnp.int32, (L, L), 1)
    causal = ri >= ci
    hp = lax.Precision.HIGHEST
    tri = causal.astype(F32)
    tri_t = (ri <= ci).astype(F32)
    b_c = jnp.dot(tri, a_c, precision=hp, preferred_element_type=F32)
    a_r = a_c.T
    b_r = jnp.dot(a_r[0:8], tri_t, precision=hp, preferred_element_type=F32)

    for h in range(M_HEADS):
        sl = slice(h * M_HD, (h + 1) * M_HD)
        xh = xc[:, sl]
        xb = xh.astype(BF16)
        q = jnp.dot(xb, wq_ref[h], preferred_element_type=F32)
        k = jnp.dot(xb, wk_ref[h], preferred_element_type=F32) * (M_HD ** -0.5)
        vb = v_ref[0, :, sl].astype(BF16)
        qb = q.astype(BF16)
        kb = k.astype(BF16)

        bt = b_c[:, M_HEADS + h:M_HEADS + h + 1]
        ic = a_c[:, h:h + 1]
        bs = b_r[M_HEADS + h:M_HEADS + h + 1, :]
        ir = a_r[h:h + 1, :]
        m_prev = m_scr[h][:, 0:1]

        dm = jnp.where(causal, bt - bs + ir, -jnp.inf)
        inter = bt + m_prev
        m_t = jnp.maximum(inter, jnp.max(dm, axis=-1, keepdims=True))
        w_in = jnp.exp(dm - m_t)
        w_st = jnp.exp(inter - m_t)
        s = lax.dot_general(qb, kb, (((1,), (1,)), ((), ())), preferred_element_type=F32) * w_in
        cmat = c_scr[h]
        nvec = n_scr[h]
        sb = s.astype(BF16)
        num = w_st * jnp.dot(qb, cmat.astype(BF16), preferred_element_type=F32) \
            + jnp.dot(sb, vb, preferred_element_type=F32)
        nt_dims = (((1,), (1,)), ((), ()))
        qn = lax.dot_general(qb, jnp.broadcast_to(nvec, (8, M_HD)).astype(BF16), nt_dims,
                             preferred_element_type=F32)[:, 0:1]
        ssum = lax.dot_general(sb, jnp.ones((8, L), BF16), nt_dims, preferred_element_type=F32)[:, 0:1]
        den = w_st * qn + ssum
        hh = num / jnp.maximum(jnp.abs(den), jnp.exp(-m_t))

        b_last = bt[L - 1:L, :]
        w_end = b_last - bt + ic
        m_new = jnp.maximum(b_last + m_prev, jnp.max(w_end, axis=0, keepdims=True))
        decay = jnp.exp(b_last + m_prev - m_new)
        kwt = k * jnp.exp(w_end - m_new)
        c_scr[h] = decay * cmat + lax.dot_general(kwt.astype(BF16), vb, (((0,), (0,)), ((), ())),
                                                  preferred_element_type=F32)
        n_scr[h] = decay * nvec + jnp.sum(kwt, axis=0, keepdims=True)
        m_scr[h] = jnp.broadcast_to(m_new, (1, LANES))

        mu = jnp.mean(hh, axis=-1, keepdims=True)
        hc = hh - mu
        var = jnp.mean(hc * hc, axis=-1, keepdims=True)
        hn = hc * lax.rsqrt(var + EPS) * nw_ref[:, sl]
        out = jax.nn.sigmoid(o_ref[0, :, sl]) * hn + sk_ref[:, sl] * xh
        y_ref[0, :, sl] = (out * _silu(z_ref[0, :, sl])).astype(BF16)


def _mlstm(p, conv_w, conv_b, wq, wk, norm_w, skip, fb_row):
    bsz, t, _ = p.shape
    L = M_CHUNK
    cb = lambda c: (lambda bi, i: (bi, i, c))
    full2 = lambda bi, i: (0, 0)
    full3 = lambda bi, i: (0, 0, 0)
    return pl.pallas_call(
        _mlstm_kernel,
        grid=(bsz, t // L),
        in_specs=[pl.BlockSpec((1, L, M_WIDTH), cb(COL_MX // M_WIDTH)),
                  pl.BlockSpec((1, L, M_WIDTH), cb(COL_MV // M_WIDTH)),
                  pl.BlockSpec((1, L, M_WIDTH), cb(COL_MO // M_WIDTH)),
                  pl.BlockSpec((1, L, M_WIDTH), cb(COL_MZ // M_WIDTH)),
                  pl.BlockSpec((1, L, LANES), cb(COL_GATES // LANES)),
                  pl.BlockSpec((CONV_K, M_WIDTH), full2),
                  pl.BlockSpec((1, M_WIDTH), full2),
                  pl.BlockSpec((M_HEADS, M_HD, M_HD), full3),
                  pl.BlockSpec((M_HEADS, M_HD, M_HD), full3),
                  pl.BlockSpec((1, M_WIDTH), full2),
                  pl.BlockSpec((1, M_WIDTH), full2),
                  pl.BlockSpec((1, LANES), full2)],
        out_specs=pl.BlockSpec((1, L, M_WIDTH), lambda bi, i: (bi, i, 0)),
        out_shape=jax.ShapeDtypeStruct((bsz, t, M_WIDTH), BF16),
        scratch_shapes=[pltpu.VMEM((M_HEADS, M_HD, M_HD), F32),
                        pltpu.VMEM((M_HEADS, 1, M_HD), F32),
                        pltpu.VMEM((M_HEADS, 1, LANES), F32),
                        pltpu.VMEM((8, M_WIDTH), F32)],
        compiler_params=_cparams(("parallel", "arbitrary")),
        name="mlstm_group",
    )(p, p, p, p, p, conv_w, conv_b, wq, wk, norm_w, skip, fb_row)


def _compress_kernel(kc0_ref, kc1_ref, vc0_ref, vc1_ref, posk_ref, posv_ref, w1k_ref, w1kc_ref, w2k_ref,
                     w1v_ref, w1vc_ref, w2v_ref, kcmp_ref, vcmpt_ref):
    nb = kcmp_ref.shape[2]

    def hidden(src_refs, pos_ref, w1_ref, w1c_ref):
        halves = [jnp.zeros((nb, 4 * CMP_HIDDEN), F32) for _ in src_refs]
        for l in range(CMP_STRIDE):
            for hf, r in enumerate(src_refs):
                xl = r[0, pl.ds(l, nb, stride=CMP_STRIDE), :]
                halves[hf] = halves[hf] + jnp.dot(xl.astype(BF16), w1c_ref[l], preferred_element_type=F32)
        acc = jnp.concatenate([hv[:, c * 2 * CMP_HIDDEN:(c + 1) * 2 * CMP_HIDDEN] for hv in halves for c in range(2)],
                              axis=0)
        first = acc[:, :CMP_HIDDEN]
        second = acc[:, CMP_HIDDEN:]
        posb = jnp.dot(jnp.broadcast_to(pos_ref[...], (8, CMP_LEN * N_HD)).astype(BF16), w1_ref[...],
                       preferred_element_type=F32)[0:1]
        hid = first + pltpu.roll(second, N_KV * nb - 1, 0) + posb
        return jax.nn.gelu(hid).astype(BF16)

    hk = hidden((kc0_ref, kc1_ref), posk_ref, w1k_ref, w1kc_ref)
    kc = jnp.dot(hk, w2k_ref[...], preferred_element_type=F32)
    hv = hidden((vc0_ref, vc1_ref), posv_ref, w1v_ref, w1vc_ref)
    vc = jnp.dot(hv, w2v_ref[...], preferred_element_type=F32)
    j = lax.broadcasted_iota(jnp.int32, (nb, 1), 0)
    lane = lax.broadcasted_iota(jnp.int32, (nb, N_HD), 1)
    n_real = (nb * CMP_STRIDE - CMP_LEN) // CMP_STRIDE + 1
    aux = jnp.where(lane < 3, (j * CMP_STRIDE).astype(F32),
                    jnp.where(lane < 6, (CMP_LEN - 1) * 0.5,
                              jnp.where((lane == 6) & (j >= n_real), 1.0, 0.0))).astype(BF16)
    ones_blk = jnp.where(lax.broadcasted_iota(jnp.int32, (VROWS - N_HD, nb), 0) == 0, 1.0, 0.0)
    for g in range(N_KV):
        kg = kc[g * nb:(g + 1) * nb, 0:N_HD]
        k_hi = kg.astype(BF16)
        k_lo = (kg - k_hi.astype(F32)).astype(BF16)
        kcmp_ref[0, g] = jnp.concatenate([k_hi, k_lo, k_hi, aux], axis=1)
        vt = vc[g * nb:(g + 1) * nb].T
        vcmpt_ref[0, g] = jnp.concatenate([vt[0:N_HD], ones_blk], axis=0).astype(BF16)


def _compress(p, posk, posv, w1k, w1kc, w2k, w1v, w1vc, w2v):
    bsz, t, _ = p.shape
    nb = t // CMP_STRIDE
    cb = lambda c: (lambda bi: (bi, 0, c))
    f2 = lambda bi: (0, 0)
    f3 = lambda bi: (0, 0, 0)
    wspecs = [pl.BlockSpec((CMP_LEN * N_HD, CMP_HIDDEN), f2),
              pl.BlockSpec((CMP_STRIDE, LANES, 4 * CMP_HIDDEN), f3),
              pl.BlockSpec((CMP_HIDDEN, LANES), f2)]
    return pl.pallas_call(
        _compress_kernel,
        grid=(bsz,),
        in_specs=[pl.BlockSpec((1, t, LANES), cb(COL_KC // LANES)),
                  pl.BlockSpec((1, t, LANES), cb(COL_KC // LANES + 1)),
                  pl.BlockSpec((1, t, LANES), cb(COL_VC // LANES)),
                  pl.BlockSpec((1, t, LANES), cb(COL_VC // LANES + 1)),
                  pl.BlockSpec((1, CMP_LEN * N_HD), f2),
                  pl.BlockSpec((1, CMP_LEN * N_HD), f2)] + wspecs + wspecs,
        out_specs=[pl.BlockSpec((1, N_KV, nb, 4 * N_HD), lambda bi: (bi, 0, 0, 0)),
                   pl.BlockSpec((1, N_KV, VROWS, nb), lambda bi: (bi, 0, 0, 0))],
        out_shape=[jax.ShapeDtypeStruct((bsz, N_KV, nb, 4 * N_HD), BF16),
                   jax.ShapeDtypeStruct((bsz, N_KV, VROWS, nb), BF16)],
        compiler_params=_cparams(("parallel",)),
        name="nsa_compress",
    )(p, p, p, p, posk, posv, w1k, w1kc, w2k, w1v, w1vc, w2v)


def _relayout_kernel(ks_ref, vs_ref, kw_ref, vw_ref, ksa_ref, vsa_ref, kwa_ref, vwa_ref):
    i = pl.program_id(1)
    is_pad = i == 0
    flag = jnp.where(is_pad, 1.0, 0.0)
    row = lax.broadcasted_iota(jnp.int32, (SKT, 1), 0)
    lane = lax.broadcasted_iota(jnp.int32, (SKT, N_HD), 1)
    al = lane - AUX_ROWS
    ones_blk = jnp.where(lax.broadcasted_iota(jnp.int32, (VROWS - N_HD, SKT), 0) == 0, 1.0, 0.0)
    for u in range(PAD_TILES):
        rows = slice(u * SKT, (u + 1) * SKT)
        base = ((i - 1) * PAD_TILES + u) * SKT
        pos = jnp.where(is_pad, 0, base + row)
        blk = pos // SLC_LEN
        p_hi = (blk * SLC_LEN).astype(F32)
        p_lo = (pos - blk * SLC_LEN).astype(F32)
        mid_w = jnp.where((al >= 0) & (al < 3), p_hi,
                          jnp.where((al >= 3) & (al < 6), p_lo, jnp.where(al == 6, flag, 0.0)))
        mid_s = jnp.where(lane == blk % SEL_GRP, 1.0, mid_w)
        ks = ks_ref[0, rows, :]
        kw = kw_ref[0, rows, :]
        for g in range(N_KV):
            sl = slice(g * N_HD, (g + 1) * N_HD)
            ksa_ref[0, g, rows, :] = jnp.concatenate([ks[:, sl], mid_s], axis=1).astype(BF16)
            kwa_ref[0, g, rows, :] = jnp.concatenate([kw[:, sl], mid_w], axis=1).astype(BF16)
        vst = vs_ref[0, rows, :].T
        vwt = vw_ref[0, rows, :].T
        for g in range(N_KV):
            sl = slice(g * N_HD, (g + 1) * N_HD)
            vsa_ref[0, g, u] = jnp.concatenate([vst[sl], ones_blk], axis=0).astype(BF16)
            vwg = jnp.concatenate([vwt[sl], ones_blk], axis=0).astype(BF16)
            for j in range(SKT // WKT):
                vwa_ref[0, g, u * (SKT // WKT) + j] = vwg[:, j * WKT:(j + 1) * WKT]


def _relayout(p):
    bsz, t, _ = p.shape
    nt = t // SKT + PAD_TILES
    wpt = SKT // WKT
    rt = PAD_TILES
    cb = lambda c: (lambda bi, i: (bi, jnp.maximum(i - 1, 0), c))
    return pl.pallas_call(
        _relayout_kernel,
        grid=(bsz, nt // rt),
        in_specs=[pl.BlockSpec((1, rt * SKT, KV_W), cb(COL_KS // KV_W)),
                  pl.BlockSpec((1, rt * SKT, KV_W), cb(COL_VS // KV_W)),
                  pl.BlockSpec((1, rt * SKT, KV_W), cb(COL_KW // KV_W)),
                  pl.BlockSpec((1, rt * SKT, KV_W), cb(COL_VW // KV_W))],
        out_specs=[pl.BlockSpec((1, N_KV, rt * SKT, KA_SEL), lambda bi, i: (bi, 0, i, 0)),
                   pl.BlockSpec((1, N_KV, rt, VROWS, SKT), lambda bi, i: (bi, 0, i, 0, 0)),
                   pl.BlockSpec((1, N_KV, rt * SKT, KA_WIN), lambda bi, i: (bi, 0, i, 0)),
                   pl.BlockSpec((1, N_KV, rt * wpt, VROWS, WKT), lambda bi, i: (bi, 0, i, 0, 0))],
        out_shape=[jax.ShapeDtypeStruct((bsz, N_KV, nt * SKT, KA_SEL), BF16),
                   jax.ShapeDtypeStruct((bsz, N_KV, nt, VROWS, SKT), BF16),
                   jax.ShapeDtypeStruct((bsz, N_KV, nt * SKT, KA_WIN), BF16),
                   jax.ShapeDtypeStruct((bsz, N_KV, nt * wpt, VROWS, WKT), BF16)],
        compiler_params=_cparams(("parallel", "parallel")),
        name="nsa_relayout",
    )(p, p, p, p)


def _nsa_batch_kernel(sp_ref, q_ref, gt_ref, z_ref, kcmp_ref, vcmpt_ref, ovt_ref, ks_ref, vs_ref, kw_ref,
                      vw_ref, tri_ref, wb_ref, y_ref, sa_scr, sb_scr, g_scr, imp_scr, rank_scr, part_scr, selb_scr,
                      tile_idx):
    g = pl.program_id(0)
    qb = pl.program_id(1)
    n_items = q_ref.shape[0]
    start = qb * QBLK
    nb = kcmp_ref.shape[2]
    n_slc = ovt_ref.shape[0]
    nw = N_HPG * QBLK
    n_top = min(SLC_TOPN, n_slc)
    t_row = start + lax.broadcasted_iota(jnp.int32, (1, QBLK), 1)
    s_bufs = (sa_scr, sb_scr)
    last = qb // (SKT // QBLK)

    def tile4(a):
        return jnp.concatenate([a] * N_HPG, axis=1)

    def col_reduce8(s, op):
        out = s[0:8]
        for r in range(1, s.shape[0] // 8):
            out = op(out, s[8 * r:8 * r + 8])
        return out

    def col_max(s):
        return jnp.max(col_reduce8(s, jnp.maximum), axis=0, keepdims=True)

    def normalise(acc):
        den = acc[N_HD:N_HD + 1]
        return acc[0:N_HD] / jnp.where(den > 0, den, 1.0)

    def gate_row(i, h, branch):
        return g_scr[i, pl.ds(GATE_NG + (g * N_HPG + h) * 3 + branch, 1), :]

    hl = lax.broadcasted_iota(jnp.int32, (AUX_ROWS, nw), 1) // QBLK
    ar = lax.broadcasted_iota(jnp.int32, (AUX_ROWS, nw), 0)

    def per_head(vals):
        out = jnp.full((AUX_ROWS, nw), vals[N_HPG - 1], F32)
        for h in range(N_HPG - 2, -1, -1):
            out = jnp.where(hl == h, vals[h], out)
        return out

    pieces = [per_head([sp_ref[(g * N_HPG + h) * 3 + j] for h in range(N_HPG)]) for j in range(3)]
    aux = jnp.where(ar == 6, NEG, 0.0)
    for j in range(3):
        aux = jnp.where((ar == j) | (ar == j + 3), pieces[j], aux)
    aux_b = aux.astype(BF16)
    q_tail = jnp.zeros((KA_SEL - N_HD - 2 * AUX_ROWS, nw), BF16)
    sidx = lax.broadcasted_iota(jnp.int32, (n_slc, QBLK), 0)
    cur = t_row // SLC_LEN
    valid = sidx <= cur
    sub8 = lax.broadcasted_iota(jnp.int32, (8, QBLK), 0)
    last_blk = (start + QBLK - 1) // SLC_LEN
    cmp_end = lax.broadcasted_iota(jnp.int32, (nb, 1), 0) * CMP_STRIDE + (CMP_LEN - 1)
    cmp_bias = tile4(jnp.where(cmp_end <= t_row, 0.0, NEG))
    wrows = WIN + QBLK

    def prepare(i):
        qt = (q_ref[i] * (N_HD ** -0.5)).T
        q4l = jnp.concatenate([qt[h * N_HD:(h + 1) * N_HD] for h in range(N_HPG)], axis=1) * LOG2E
        q4s = q4l.astype(BF16)
        q4lo = (q4l - q4s.astype(F32)).astype(BF16)
        g_scr[i] = jax.nn.sigmoid(gt_ref[i]).T

        q_cmp = jnp.concatenate([q4s, q4s, q4lo, aux_b, jnp.zeros((N_HD - AUX_ROWS, nw), BF16)], axis=0)

        def compressed(rows):
            s_c = jnp.dot(kcmp_ref[i, 0, 0:rows, :], q_cmp, preferred_element_type=F32) + cmp_bias[0:rows]
            e_c = jnp.exp2(s_c - col_max(s_c))
            z_c = jnp.sum(col_reduce8(e_c, jnp.add), axis=0, keepdims=True)
            inv_c = jnp.where(tile4(t_row >= CMP_LEN - 1) & (z_c > 0), 1.0 / z_c, 0.0)
            o_cmp = jnp.dot(vcmpt_ref[i, 0, :, 0:rows], e_c.astype(BF16),
                            preferred_element_type=F32)[0:N_HD] * inv_c
            p_c = e_c * inv_c
            psum = p_c[:, 0:QBLK]
            for h in range(1, N_HPG):
                psum = psum + p_c[:, h * QBLK:(h + 1) * QBLK]
            parts, rest = [], psum
            for _ in range(3):
                parts.append(rest.astype(BF16))
                rest = rest - parts[-1].astype(F32)
            ov = jnp.concatenate([ovt_ref[:, k * nb:k * nb + rows] for k in range(3)], axis=1)
            return o_cmp, jnp.dot(ov, jnp.concatenate(parts, axis=0), preferred_element_type=F32)

        need_all = (start + QBLK - CMP_LEN) // CMP_STRIDE >= nb // 2
        o_c, imp = lax.cond(need_all, lambda: compressed(nb), lambda: compressed(nb // 2))
        imp = jnp.where(valid, imp, -jnp.inf)
        imp_scr[...] = jnp.where((sidx == 0) | (sidx == cur), jnp.inf, imp)

        q_win = jnp.concatenate([q4s, jnp.zeros((AUX_ROWS, nw), BF16), aux_b,
                                 jnp.zeros((KA_WIN - N_HD - 2 * AUX_ROWS, nw), BF16)], axis=0)
        s_w = jnp.dot(kw_ref[i, 0, pl.ds(pl.multiple_of(start, QBLK), wrows), :], q_win,
                      preferred_element_type=F32)
        s_w = jnp.concatenate([s_w[0:WKT] + tile4(wb_ref[0]), s_w[WKT:wrows - WKT],
                               s_w[wrows - WKT:] + tile4(wb_ref[1])], axis=0)
        p_w = jnp.exp2(s_w - col_max(s_w)).astype(BF16)
        v_w = jnp.concatenate([vw_ref[i, 0, qb * (QBLK // WKT) + j] for j in range(wrows // WKT)], axis=1)
        o_w = normalise(jnp.dot(v_w, p_w, preferred_element_type=F32))
        for h in range(N_HPG):
            hs = slice(h * QBLK, (h + 1) * QBLK)
            part_scr[i, :, hs] = gate_row(i, h, 0) * o_c[:, hs] + gate_row(i, h, 2) * o_w[:, hs]

        rank_scr[...] = jnp.zeros_like(rank_scr)
        for ri in range(n_slc // 8):
            @pl.when((last_blk >= n_top) & (8 * ri <= last_blk))
            def _():
                rows = imp_scr[8 * ri:8 * ri + 8]
                for r in range(n_slc // 8):
                    blk8 = imp_scr[8 * r:8 * r + 8]
                    acc = rank_scr[8 * r:8 * r + 8]
                    for ii in range(8):
                        row = rows[ii:ii + 1]
                        if ri < r:
                            before = row >= blk8
                        elif ri > r:
                            before = row > blk8
                        else:
                            before = (row > blk8) | ((row == blk8) & (sub8 > ii))
                        acc = acc + jnp.where(before, 1.0, 0.0)
                    rank_scr[8 * r:8 * r + 8] = acc
        chosen = (rank_scr[...] < n_top) & valid
        selb_scr[i] = tile4(jnp.where(chosen, 0.0, NEG))

        any_q = jnp.max(jnp.where(chosen, 1.0, 0.0), axis=1, keepdims=True) > 0
        s_io = lax.broadcasted_iota(jnp.int32, (n_slc, 1), 0)
        bits = jnp.where(any_q, jnp.left_shift(1, s_io % 32), 0)
        words = [jnp.sum(jnp.where(s_io // 32 == w, bits, 0)) for w in range((n_slc + 31) // 32)]
        bpt = SKT // SLC_LEN
        cnt = jnp.int32(0)
        for tile in range(n_slc // bpt):
            nib = lax.shift_right_logical(words[tile * bpt // 32], jnp.int32(tile * bpt % 32)) & (2 ** bpt - 1)
            tile_idx[i, cnt] = tile
            cnt = cnt + ((nib != 0) & (tile < last)).astype(jnp.int32)
        return q4s, cnt

    q4, n_act = zip(*[prepare(i) for i in range(n_items)])


    def sel_scores(i, kt):
        grp = pl.multiple_of((kt * SKT // SLC_LEN) // SEL_GRP * SEL_GRP, SEL_GRP)
        sel_rows = jnp.concatenate([selb_scr[i, pl.ds(grp, SEL_GRP), :], jnp.zeros((AUX_ROWS - SEL_GRP, nw), F32)],
                                   axis=0)
        q_sel = jnp.concatenate([q4[i], sel_rows.astype(BF16), aux_b, q_tail], axis=0)
        krow = pl.multiple_of((kt + PAD_TILES) * SKT, SKT)
        return jnp.dot(ks_ref[i, 0, pl.ds(krow, SKT), :], q_sel, preferred_element_type=F32)

    def tile_rows(kt):
        return pl.ds(pl.multiple_of(kt * SKT, SKT), SKT)

    def pass1(i):
        def fn(kt, mrun):
            s = sel_scores(i, kt)
            s_bufs[i % 2][tile_rows(kt), :] = s
            return jnp.maximum(mrun, col_reduce8(s, jnp.maximum))
        return fn

    def pass2(i, m_sel):
        def fn(kt, acc):
            pr = jnp.exp2(s_bufs[i % 2][tile_rows(kt), :] - m_sel).astype(BF16)
            return acc + jnp.dot(vs_ref[i, 0, kt + PAD_TILES], pr, preferred_element_type=F32)
        return fn

    def listed(i, fn):
        return lambda j, carry: fn(tile_idx[i, j], carry)

    def both(f1, f2):
        def fn(j, carry):
            return f1(j, carry[0]), f2(j, carry[1])
        return fn

    def run_tiles(fn, lo, n, init):
        def group(size, first):
            def body(j, carry):
                for u in range(size):
                    carry = fn(first + j * size + u, carry)
                return carry
            return body

        n_u = n // UNROLL
        carry = lax.fori_loop(0, n_u, group(UNROLL, lo), init)
        done = lo + n_u * UNROLL
        size = UNROLL // 2
        while size >= 1:
            take = (n // size) % 2
            carry = lax.fori_loop(0, take, group(size, done), carry)
            done = done + take * size
            size //= 2
        return carry

    def diag_tile(i, mrun):
        s = sel_scores(i, last) + tile4(tri_ref[qb % (SKT // QBLK)])
        s_bufs[i % 2][tile_rows(last), :] = s
        return jnp.max(jnp.maximum(mrun, col_reduce8(s, jnp.maximum)), axis=0, keepdims=True)

    mrun0 = jnp.full((8, nw), NEG, F32)
    acc0 = jnp.zeros((VROWS, nw), F32)
    m_prev = diag_tile(0, run_tiles(listed(0, pass1(0)), 0, n_act[0], mrun0))
    outs = []
    for i in range(1, n_items):
        p2 = pass2(i - 1, m_prev)
        f1, f2 = listed(i, pass1(i)), listed(i - 1, p2)
        n_both = jnp.minimum(n_act[i], n_act[i - 1])
        mrun, acc = run_tiles(both(f1, f2), 0, n_both, (mrun0, acc0))
        mrun = run_tiles(f1, n_both, n_act[i] - n_both, mrun)
        acc = run_tiles(f2, n_both, n_act[i - 1] - n_both, acc)
        m_cur = diag_tile(i, mrun)
        outs.append(normalise(p2(last, acc)))
        m_prev = m_cur
    p2 = pass2(n_items - 1, m_prev)
    acc = run_tiles(listed(n_items - 1, p2), 0, n_act[n_items - 1], acc0)
    outs.append(normalise(p2(last, acc)))

    for i in range(n_items):
        tot = [part_scr[i, :, h * QBLK:(h + 1) * QBLK] + gate_row(i, h, 1) * outs[i][:, h * QBLK:(h + 1) * QBLK]
               for h in range(N_HPG)]
        o = jnp.concatenate(tot, axis=0).T
        y_ref[i] = (o * _silu(z_ref[i])).astype(BF16)


def _edge_biases():
    kl = np.arange(SKT)[:, None]
    ql = np.arange(QBLK)[None, :]
    tri = np.stack([np.where(kl <= par * QBLK + ql, 0.0, NEG) for par in range(SKT // QBLK)])
    kk = np.arange(WKT)[:, None]
    wb = np.stack([np.where(kk > ql, 0.0, NEG), np.where(kk <= ql + WKT - QBLK, 0.0, NEG)])
    return jnp.asarray(tri, F32), jnp.asarray(wb, F32)


def _nsa(p, spieces, kcmp, vcmpt, ovt, ks, vs, kw, vw):
    bsz, t, _ = p.shape
    n_slc = t // SLC_LEN
    gw = N_HPG * N_HD
    tri, wb = _edge_biases()
    nw = N_HPG * QBLK
    per_g = lambda a: pl.BlockSpec((bsz, 1) + a.shape[2:], lambda g, i, sp: (0, g) + (0,) * (a.ndim - 2),
                                   pipeline_mode=pl.Buffered(1))
    const = lambda a: pl.BlockSpec(a.shape, lambda g, i, sp: (0,) * a.ndim)
    grid_spec = pltpu.PrefetchScalarGridSpec(
        num_scalar_prefetch=1,
        grid=(N_KV, t // QBLK),
        in_specs=[pl.BlockSpec((bsz, QBLK, gw), lambda g, i, sp: (0, i, COL_NQ // gw + g)),
                  pl.BlockSpec((bsz, QBLK, LANES), lambda g, i, sp: (0, i, COL_GATES // LANES)),
                  pl.BlockSpec((bsz, QBLK, gw), lambda g, i, sp: (0, i, COL_NZ // gw + g)),
                  per_g(kcmp), per_g(vcmpt), const(ovt),
                  per_g(ks), per_g(vs), per_g(kw), per_g(vw), const(tri), const(wb)],
        out_specs=pl.BlockSpec((bsz, QBLK, gw), lambda g, i, sp: (0, i, g)),
        scratch_shapes=[pltpu.VMEM((t, nw), F32),
                        pltpu.VMEM((t, nw), F32),
                        pltpu.VMEM((bsz, LANES, QBLK), F32),
                        pltpu.VMEM((n_slc, QBLK), F32),
                        pltpu.VMEM((n_slc, QBLK), F32),
                        pltpu.VMEM((bsz, N_HD, nw), F32),
                        pltpu.VMEM((bsz, n_slc, nw), F32),
                        pltpu.SMEM((bsz, t // SKT), jnp.int32)],
    )
    return pl.pallas_call(
        _nsa_batch_kernel,
        grid_spec=grid_spec,
        out_shape=jax.ShapeDtypeStruct((bsz, t, N_WIDTH), BF16),
        compiler_params=_cparams(("arbitrary", "arbitrary")),
        name="nsa_attention",
    )(spieces, p, p, p, kcmp, vcmpt, ovt, ks, vs, kw, vw, tri, wb)


def _outproj_kernel(ym_ref, yn_ref, w_ref, x_ref, gate_ref, fg_ref, o_ref, wb_scr, *, final):
    @pl.when((pl.program_id(0) == 0) & (pl.program_id(1) == 0))
    def _():
        rows = w_ref.shape[1]
        for c in range(rows // WP_TILE):
            cs = slice(c * WP_TILE, (c + 1) * WP_TILE)
            wb_scr[cs, :] = w_ref[0, cs, :].astype(BF16)

    y = jnp.dot(ym_ref[0], wb_scr[0:M_WIDTH, :], preferred_element_type=F32)
    y = y + jnp.dot(yn_ref[0], wb_scr[M_WIDTH:, :], preferred_element_type=F32)
    hres = x_ref[0] + gate_ref[0] * y
    if final:
        ms = jnp.mean(hres * hres, axis=-1, keepdims=True)
        hres = hres * lax.rsqrt(ms + EPS) * fg_ref[...]
    o_ref[0] = hres


def _outproj(ym, yn, w, layer, x, gate, fg, final):
    bsz, t, d = x.shape
    tm = 512
    return pl.pallas_call(
        functools.partial(_outproj_kernel, final=final),
        grid=(bsz, t // tm),
        in_specs=[pl.BlockSpec((1, tm, M_WIDTH), lambda bi, i: (bi, i, 0)),
                  pl.BlockSpec((1, tm, N_WIDTH), lambda bi, i: (bi, i, 0)),
                  pl.BlockSpec((1, M_WIDTH + N_WIDTH, d), lambda bi, i: (layer, 0, 0),
                               pipeline_mode=pl.Buffered(1)),
                  pl.BlockSpec((1, tm, d), lambda bi, i: (bi, i, 0)),
                  pl.BlockSpec((1, 1, d), lambda bi, i: (bi, 0, 0)),
                  pl.BlockSpec((1, d), lambda bi, i: (0, 0))],
        out_specs=pl.BlockSpec((1, tm, d), lambda bi, i: (bi, i, 0)),
        out_shape=jax.ShapeDtypeStruct((bsz, t, d), F32),
        scratch_shapes=[pltpu.VMEM((M_WIDTH + N_WIDTH, d), BF16)],
        compiler_params=_cparams(("arbitrary", "arbitrary")),
        name="outproj_residual",
    )(ym, yn, w, x, gate, fg)


SRC_MI = 4 * M_WIDTH
SRC_NQ = SRC_MI + 2 * M_HEADS
SRC_NG = SRC_NQ + N_WIDTH + 6 * KV_W
SRC_NZ = SRC_NG + 3 * N_HEADS


def _reorder_cols(a):
    parts = [a[..., 0:SRC_MI], a[..., SRC_NQ:SRC_NG], a[..., SRC_NZ:SRC_NZ + N_WIDTH], a[..., SRC_MI:SRC_NQ],
             a[..., SRC_NG:SRC_NZ]]
    used = sum(x.shape[-1] for x in parts)
    parts.append(jnp.zeros(a.shape[:-1] + (NP_PAD - used,), a.dtype))
    return jnp.concatenate(parts, axis=-1)


WP_TILE = 512
WP_GATE_TILE = COL_GATES // WP_TILE


def _wprep_kernel(w_ref, g1_ref, g2_ref, o_ref):
    j = pl.program_id(0)
    d = o_ref.shape[0]

    @pl.when(j < WP_GATE_TILE)
    def _():
        for c in range(d // WP_TILE):
            cs = slice(c * WP_TILE, (c + 1) * WP_TILE)
            o_ref[cs, :] = w_ref[0, :, cs].T.astype(BF16)

    @pl.when(j == WP_GATE_TILE)
    def _():
        n_gate = g1_ref.shape[1] + g2_ref.shape[1]
        gt = jnp.concatenate([g1_ref[0], g2_ref[0], jnp.zeros((LANES - n_gate, d), F32)], axis=0)
        for c in range(d // WP_TILE):
            cs = slice(c * WP_TILE, (c + 1) * WP_TILE)
            o_ref[cs, 0:LANES] = gt[:, cs].T.astype(BF16)
        o_ref[:, LANES:] = jnp.zeros((d, WP_TILE - LANES), BF16)


def _wprep(w_t, layer):
    _, n, d = w_t.shape
    n_big = COL_GATES // WP_TILE
    assert n == SRC_NZ + N_WIDTH and NP_PAD // WP_TILE == n_big + 1

    def src_row(j):
        return jnp.where(j < COL_NQ // WP_TILE, j * WP_TILE,
                         jnp.where(j < COL_NZ // WP_TILE, SRC_NQ + (j - COL_NQ // WP_TILE) * WP_TILE,
                                   jnp.where(j < n_big, SRC_NZ + (j - COL_NZ // WP_TILE) * WP_TILE, 0)))

    el = pl.Element
    return pl.pallas_call(
        _wprep_kernel,
        grid=(NP_PAD // WP_TILE,),
        in_specs=[pl.BlockSpec((el(1), el(WP_TILE), el(d)), lambda j: (layer, pl.multiple_of(src_row(j), 8), 0)),
                  pl.BlockSpec((el(1), el(SRC_NQ - SRC_MI), el(d)), lambda j: (layer, SRC_MI, 0)),
                  pl.BlockSpec((el(1), el(SRC_NZ - SRC_NG), el(d)), lambda j: (layer, SRC_NG, 0))],
        out_specs=pl.BlockSpec((d, WP_TILE), lambda j: (0, j)),
        out_shape=jax.ShapeDtypeStruct((d, NP_PAD), BF16),
        compiler_params=_cparams(("parallel",)),
        name="inproj_weight_prep",
    )(w_t, w_t, w_t)


def _overlap_t(t):
    n_cmp_rows = t // CMP_STRIDE
    n_slc = t // SLC_LEN
    c0 = np.arange(n_cmp_rows) * CMP_STRIDE
    s0 = np.arange(n_slc) * SLC_LEN
    ov = (c0[None, :] <= s0[:, None] + SLC_LEN - 1) & (c0[None, :] + CMP_LEN - 1 >= s0[:, None])
    ov[:, (t - CMP_LEN) // CMP_STRIDE + 1:] = False
    return jnp.asarray(np.concatenate([ov] * 3, axis=1), BF16)


def kernel(x, c, ln_g, w_ada, b_ada, w_in, b_in, m_conv_w, m_conv_b, m_wq, m_wk, m_norm_w, m_skip, m_f_bias,
           n_pos_k, n_pos_v, n_w1_k, n_w2_k, n_w1_v, n_w2_v, w_out, final_g):
    out_dtype = x.dtype
    bsz, t, d = x.shape
    depth = ln_g.shape[0]
    h_res = x.astype(F32)
    assert bsz <= 2 and t % 1024 == 0 and t // CMP_STRIDE <= 256
    c_t =jnp.zeros((d, 8), F32).at[:, :bsz].set(c.astype(F32).T)
    slopes_np = np.array([2.0 ** (-8.0 * (h + 1) / N_HEADS) for h in range(N_HEADS)], np.float32)
    rest = (slopes_np.astype(np.float64) * LOG2E).astype(np.float32)
    pieces = []
    for _ in range(3):
        pieces.append(rest.astype(jnp.bfloat16).astype(np.float32))
        rest = rest - pieces[-1]
    spieces = jnp.asarray(np.stack(pieces, axis=1).reshape(-1))
    ovt = _overlap_t(t)

    def w1cat(w1):
        w = w1.reshape(2, CMP_STRIDE, N_HD, CMP_HIDDEN)
        w = jnp.concatenate([w[0], w[1]], axis=-1).astype(BF16)
        z = jnp.zeros_like(w)
        return jnp.concatenate([jnp.concatenate([w, z], axis=-1), jnp.concatenate([z, w], axis=-1)], axis=1)

    def w2pad(w2):
        return jnp.pad(w2, ((0, 0), (0, LANES - N_HD))).astype(BF16)

    for l in range(depth):
        mod = _ada(c_t, w_ada, b_ada[l][None, :], l, bsz)[:bsz]
        shift, scale, gate = mod[:, None, 0:d], mod[:, None, d:2 * d], mod[:, None, 2 * d:3 * d]
        p = _inproj(h_res, ln_g[l][None, :], scale, shift,
                    _wprep(jnp.swapaxes(w_in, 1, 2), l), _reorder_cols(b_in[l])[None, :])
        fb_row = jnp.zeros((1, LANES), F32).at[0, M_HEADS:2 * M_HEADS].set(m_f_bias[l])
        y_m = _mlstm(p, m_conv_w[l], m_conv_b[l][None, :], m_wq[l].astype(BF16), m_wk[l].astype(BF16),
                     m_norm_w[l][None, :], m_skip[l][None, :], fb_row)
        kcmp, vcmpt = _compress(p, n_pos_k[l].reshape(1, -1), n_pos_v[l].reshape(1, -1),
                                n_w1_k[l].astype(BF16), w1cat(n_w1_k[l]), w2pad(n_w2_k[l]),
                                n_w1_v[l].astype(BF16), w1cat(n_w1_v[l]), w2pad(n_w2_v[l]))
        ks, vs, kw, vw = _relayout(p)
        y_n = _nsa(p, spieces, kcmp, vcmpt, ovt, ks, vs, kw, vw)
        h_res = _outproj(y_m, y_n, w_out, l, h_res, gate, final_g[None, :], l == depth - 1)
    return h_res.astype(out_dtype)
```

```python
import functools

import numpy as np
import jax
import jax.numpy as jnp
from jax import lax
from jax.experimental import pallas as pl
from jax.experimental.pallas import tpu as pltpu

F32 = jnp.float32
BF16 = jnp.bfloat16

EPS = 1e-6
M_HEADS = 4
M_HD = 256
M_WIDTH = M_HEADS * M_HD
CONV_K = 4
M_CHUNK = 256
N_HEADS = 16
N_HD = 64
N_KV = 4
N_HPG = N_HEADS // N_KV
N_WIDTH = N_HEADS * N_HD
KV_W = N_KV * N_HD
CMP_LEN = 32
CMP_STRIDE = 16
CMP_HIDDEN = 2 * N_HD
SLC_LEN = 64
SLC_TOPN = 16
WIN = 512
QBLK = 256
SKT = 256
WKT = 256
PAD_TILES = WIN // SKT
SEL_GRP = 8
KA_SEL = 128
KA_WIN = 128
VROWS = 80
UNROLL = 8
REST_GROUPS = {1: (1,), 2: (2,), 3: (3,), 4: (4,), 5: (5,), 6: (4, 2), 7: (4, 3)}
AUX_ROWS = 16
LOG2E = 1.4426950408889634

COL_MX, COL_MV, COL_MO, COL_MZ = 0, 1024, 2048, 3072
COL_NQ = 4096
COL_KC, COL_VC, COL_KS, COL_VS, COL_KW, COL_VW = 5120, 5376, 5632, 5888, 6144, 6400
COL_NZ = 6656
COL_GATES = 7680
GATE_NG = 2 * M_HEADS
NP_PAD = 8192
NORM_CHUNKS = 4
LANES = 128
NEG = -1e30
VMEM_LIMIT = 56 * 1024 * 1024


def _cparams(sem):
    return pltpu.CompilerParams(dimension_semantics=sem, vmem_limit_bytes=VMEM_LIMIT)


def _silu(x):
    return x * jax.nn.sigmoid(x)


def _log_sigmoid(x):
    return jnp.minimum(x, 0.0) - jnp.log1p(jnp.exp(-jnp.abs(x)))


def _ada_kernel(ct_ref, w_ref, b_ref, o_ref, *, bsz):
    s_t = _silu(ct_ref[...])
    w = w_ref[0]
    row = lax.broadcasted_iota(jnp.int32, o_ref.shape, 0)
    out = jnp.zeros(o_ref.shape, F32)
    for b in range(bsz):
        prod = w * s_t[:, b:b + 1]
        acc = prod[0:8]
        for r in range(1, prod.shape[0] // 8):
            acc = acc + prod[8 * r:8 * r + 8]
        out = jnp.where(row == b, jnp.sum(acc, axis=0, keepdims=True) + b_ref[...], out)
    o_ref[...] = out


def _ada(c_t, w, b, layer, bsz):
    _, d, n = w.shape
    tn = 1024
    return pl.pallas_call(
        functools.partial(_ada_kernel, bsz=bsz),
        grid=(n // tn,),
        in_specs=[pl.BlockSpec((d, 8), lambda j: (0, 0)),
                  pl.BlockSpec((1, d, tn), lambda j: (layer, 0, j)),
                  pl.BlockSpec((1, tn), lambda j: (0, j))],
        out_specs=pl.BlockSpec((8, tn), lambda j: (0, j)),
        out_shape=jax.ShapeDtypeStruct((8, n), F32),
        compiler_params=_cparams(("parallel",)),
        name="ada_mod",
    )(c_t, w, b)


def _inproj_kernel(x_ref, g_ref, sc_ref, sh_ref, w_ref, b_ref, o_ref, h_ref):
    first = pl.program_id(2) == 0

    @pl.when(first)
    def _():
        tm = x_ref.shape[1]
        ck = tm // NORM_CHUNKS
        for c in range(NORM_CHUNKS):
            rows = slice(c * ck, (c + 1) * ck)
            x = x_ref[0, rows, :]
            ms = jnp.mean(x * x, axis=-1, keepdims=True)
            h = x * lax.rsqrt(ms + EPS) * g_ref[...]
            h = (h * (1.0 + sc_ref[0]) + sh_ref[0]).astype(BF16)
            h_ref[rows, :] = h
            o_ref[0, rows, :] = jnp.dot(h, w_ref[...], preferred_element_type=F32) + b_ref[...]

    @pl.when(jnp.logical_not(first))
    def _():
        o_ref[0] = jnp.dot(h_ref[...], w_ref[...], preferred_element_type=F32) + b_ref[...]


def _inproj(x, g, scale, shift, w, b):
    bsz, t, d = x.shape
    n = w.shape[1]
    tm, tn = 1024, 1024
    return pl.pallas_call(
        _inproj_kernel,
        grid=(bsz, t // tm, n // tn),
        in_specs=[pl.BlockSpec((1, tm, d), lambda bi, i, j: (bi, i, 0)),
                  pl.BlockSpec((1, d), lambda bi, i, j: (0, 0)),
                  pl.BlockSpec((1, 1, d), lambda bi, i, j: (bi, 0, 0)),
                  pl.BlockSpec((1, 1, d), lambda bi, i, j: (bi, 0, 0)),
                  pl.BlockSpec((d, tn), lambda bi, i, j: (0, j)),
                  pl.BlockSpec((1, tn), lambda bi, i, j: (0, j))],
        out_specs=pl.BlockSpec((1, tm, tn), lambda bi, i, j: (bi, i, j)),
        out_shape=jax.ShapeDtypeStruct((bsz, t, n), F32),
        scratch_shapes=[pltpu.VMEM((tm, d), BF16)],
        compiler_params=_cparams(("parallel", "parallel", "arbitrary")),
        name="norm_inproj",
    )(x, g, scale, shift, w, b)


def _mlstm_kernel(x_ref, v_ref, o_ref, z_ref, gt_ref, cw_ref, cb_ref, wq_ref, wk_ref, nw_ref, sk_ref, fb_ref,
                  y_ref, c_scr, n_scr, m_scr, xp_scr):
    L = M_CHUNK

    @pl.when(pl.program_id(1) == 0)
    def _():
        c_scr[...] = jnp.zeros_like(c_scr)
        n_scr[...] = jnp.zeros_like(n_scr)
        m_scr[...] = jnp.zeros_like(m_scr)
        xp_scr[...] = jnp.zeros_like(xp_scr)

    x = x_ref[0]
    prev = xp_scr[...]
    row8 = lax.broadcasted_iota(jnp.int32, (8, M_WIDTH), 0)
    cw = cw_ref[...]
    xc = cb_ref[...] + x * cw[CONV_K - 1:CONV_K, :]
    for sft in range(1, CONV_K):
        xr = pltpu.roll(x, sft, 0)
        top = jnp.where(row8 < sft, pltpu.roll(prev, sft, 0), xr[0:8])
        xs = jnp.concatenate([top, xr[8:]], axis=0)
        xc = xc + xs * cw[CONV_K - 1 - sft:CONV_K - sft, :]
    xp_scr[...] = x[L - 8:L]
    xc = _silu(xc)

    gt = gt_ref[0]
    col = lax.broadcasted_iota(jnp.int32, (L, LANES), 1)
    logf = _log_sigmoid(gt + fb_ref[...])
    a_c = jnp.where((col >= M_HEADS) & (col < 2 * M_HEADS), logf, gt)
    ri = lax.broadcasted_iota(jnp.int32, (L, L), 0)
    ci = lax.broadcasted_iota(jnp.int32, (L, L), 1)
    causal = ri >= ci
    hp = lax.Precision.HIGHEST
    tri = causal.astype(F32)
    tri_t = (ri <= ci).astype(F32)
    b_c = jnp.dot(tri, a_c, precision=hp, preferred_element_type=F32)
    a_r = a_c.T
    b_r = jnp.dot(a_r[0:8], tri_t, precision=hp, preferred_element_type=F32)

    for h in range(M_HEADS):
        sl = slice(h * M_HD, (h + 1) * M_HD)
        xh = xc[:, sl]
        xb = xh.astype(BF16)
        q = jnp.dot(xb, wq_ref[h], preferred_element_type=F32)
        k = jnp.dot(xb, wk_ref[h], preferred_element_type=F32) * (M_HD ** -0.5)
        vb = v_ref[0, :, sl].astype(BF16)
        qb = q.astype(BF16)
        kb = k.astype(BF16)

        bt = b_c[:, M_HEADS + h:M_HEADS + h + 1]
        ic = a_c[:, h:h + 1]
        bs = b_r[M_HEADS + h:M_HEADS + h + 1, :]
        ir = a_r[h:h + 1, :]
        m_prev = m_scr[h][:, 0:1]

        dm = jnp.where(causal, bt - bs + ir, -jnp.inf)
        inter = bt + m_prev
        m_t = jnp.maximum(inter, jnp.max(dm, axis=-1, keepdims=True))
        w_in = jnp.exp(dm - m_t)
        w_st = jnp.exp(inter - m_t)
        s = lax.dot_general(qb, kb, (((1,), (1,)), ((), ())), preferred_element_type=F32) * w_in
        cmat = c_scr[h]
        nvec = n_scr[h]
        sb = s.astype(BF16)
        num = w_st * jnp.dot(qb, cmat.astype(BF16), preferred_element_type=F32) \
            + jnp.dot(sb, vb, preferred_element_type=F32)
        nt_dims = (((1,), (1,)), ((), ()))
        qn = lax.dot_general(qb, jnp.broadcast_to(nvec, (8, M_HD)).astype(BF16), nt_dims,
                             preferred_element_type=F32)[:, 0:1]
        ssum = lax.dot_general(sb, jnp.ones((8, L), BF16), nt_dims, preferred_element_type=F32)[:, 0:1]
        den = w_st * qn + ssum
        hh = num / jnp.maximum(jnp.abs(den), jnp.exp(-m_t))

        b_last = bt[L - 1:L, :]
        w_end = b_last - bt + ic
        m_new = jnp.maximum(b_last + m_prev, jnp.max(w_end, axis=0, keepdims=True))
        decay = jnp.exp(b_last + m_prev - m_new)
        kwt = k * jnp.exp(w_end - m_new)
        c_scr[h] = decay * cmat + lax.dot_general(kwt.astype(BF16), vb, (((0,), (0,)), ((), ())),
                                                  preferred_element_type=F32)
        n_scr[h] = decay * nvec + jnp.sum(kwt, axis=0, keepdims=True)
        m_scr[h] = jnp.broadcast_to(m_new, (1, LANES))

        mu = jnp.mean(hh, axis=-1, keepdims=True)
        hc = hh - mu
        var = jnp.mean(hc * hc, axis=-1, keepdims=True)
        hn = hc * lax.rsqrt(var + EPS) * nw_ref[:, sl]
        out = jax.nn.sigmoid(o_ref[0, :, sl]) * hn + sk_ref[:, sl] * xh
        y_ref[0, :, sl] = (out * _silu(z_ref[0, :, sl])).astype(BF16)


def _mlstm(p, conv_w, conv_b, wq, wk, norm_w, skip, fb_row):
    bsz, t, _ = p.shape
    L = M_CHUNK
    cb = lambda c: (lambda bi, i: (bi, i, c))
    full2 = lambda bi, i: (0, 0)
    full3 = lambda bi, i: (0, 0, 0)
    return pl.pallas_call(
        _mlstm_kernel,
        grid=(bsz, t // L),
        in_specs=[pl.BlockSpec((1, L, M_WIDTH), cb(COL_MX // M_WIDTH)),
                  pl.BlockSpec((1, L, M_WIDTH), cb(COL_MV // M_WIDTH)),
                  pl.BlockSpec((1, L, M_WIDTH), cb(COL_MO // M_WIDTH)),
                  pl.BlockSpec((1, L, M_WIDTH), cb(COL_MZ // M_WIDTH)),
                  pl.BlockSpec((1, L, LANES), cb(COL_GATES // LANES)),
                  pl.BlockSpec((CONV_K, M_WIDTH), full2),
                  pl.BlockSpec((1, M_WIDTH), full2),
                  pl.BlockSpec((M_HEADS, M_HD, M_HD), full3),
                  pl.BlockSpec((M_HEADS, M_HD, M_HD), full3),
                  pl.BlockSpec((1, M_WIDTH), full2),
                  pl.BlockSpec((1, M_WIDTH), full2),
                  pl.BlockSpec((1, LANES), full2)],
        out_specs=pl.BlockSpec((1, L, M_WIDTH), lambda bi, i: (bi, i, 0)),
        out_shape=jax.ShapeDtypeStruct((bsz, t, M_WIDTH), BF16),
        scratch_shapes=[pltpu.VMEM((M_HEADS, M_HD, M_HD), F32),
                        pltpu.VMEM((M_HEADS, 1, M_HD), F32),
                        pltpu.VMEM((M_HEADS, 1, LANES), F32),
                        pltpu.VMEM((8, M_WIDTH), F32)],
        compiler_params=_cparams(("parallel", "arbitrary")),
        name="mlstm_group",
    )(p, p, p, p, p, conv_w, conv_b, wq, wk, norm_w, skip, fb_row)


def _compress_kernel(kc0_ref, kc1_ref, vc0_ref, vc1_ref, posk_ref, posv_ref, w1k_ref, w1kc_ref, w2k_ref,
                     w1v_ref, w1vc_ref, w2v_ref, kcmp_ref, vcmpt_ref):
    nb = kcmp_ref.shape[2]

    def hidden(src_refs, pos_ref, w1_ref, w1c_ref):
        halves = [jnp.zeros((nb, 4 * CMP_HIDDEN), F32) for _ in src_refs]
        for l in range(CMP_STRIDE):
            for hf, r in enumerate(src_refs):
                xl = r[0, pl.ds(l, nb, stride=CMP_STRIDE), :]
                halves[hf] = halves[hf] + jnp.dot(xl.astype(BF16), w1c_ref[l], preferred_element_type=F32)
        acc = jnp.concatenate([hv[:, c * 2 * CMP_HIDDEN:(c + 1) * 2 * CMP_HIDDEN] for hv in halves for c in range(2)],
                              axis=0)
        first = acc[:, :CMP_HIDDEN]
        second = acc[:, CMP_HIDDEN:]
        posb = jnp.dot(jnp.broadcast_to(pos_ref[...], (8, CMP_LEN * N_HD)).astype(BF16), w1_ref[...],
                       preferred_element_type=F32)[0:1]
        hid = first + pltpu.roll(second, N_KV * nb - 1, 0) + posb
        return jax.nn.gelu(hid).astype(BF16)

    hk = hidden((kc0_ref, kc1_ref), posk_ref, w1k_ref, w1kc_ref)
    kc = jnp.dot(hk, w2k_ref[...], preferred_element_type=F32)
    hv = hidden((vc0_ref, vc1_ref), posv_ref, w1v_ref, w1vc_ref)
    vc = jnp.dot(hv, w2v_ref[...], preferred_element_type=F32)
    j = lax.broadcasted_iota(jnp.int32, (nb, 1), 0)
    lane = lax.broadcasted_iota(jnp.int32, (nb, N_HD), 1)
    n_real = (nb * CMP_STRIDE - CMP_LEN) // CMP_STRIDE + 1
    aux = jnp.where(lane < 3, (j * CMP_STRIDE).astype(F32),
                    jnp.where(lane < 6, (CMP_LEN - 1) * 0.5,
                              jnp.where((lane == 6) & (j >= n_real), 1.0, 0.0))).astype(BF16)
    ones_blk = jnp.where(lax.broadcasted_iota(jnp.int32, (VROWS - N_HD, nb), 0) == 0, 1.0, 0.0)
    for g in range(N_KV):
        kg = kc[g * nb:(g + 1) * nb, 0:N_HD]
        k_hi = kg.astype(BF16)
        k_lo = (kg - k_hi.astype(F32)).astype(BF16)
        kcmp_ref[0, g] = jnp.concatenate([k_hi, k_lo, k_hi, aux], axis=1)
        vt = vc[g * nb:(g + 1) * nb].T
        vcmpt_ref[0, g] = jnp.concatenate([vt[0:N_HD], ones_blk], axis=0).astype(BF16)


def _compress(p, posk, posv, w1k, w1kc, w2k, w1v, w1vc, w2v):
    bsz, t, _ = p.shape
    nb = t // CMP_STRIDE
    cb = lambda c: (lambda bi: (bi, 0, c))
    f2 = lambda bi: (0, 0)
    f3 = lambda bi: (0, 0, 0)
    wspecs = [pl.BlockSpec((CMP_LEN * N_HD, CMP_HIDDEN), f2),
              pl.BlockSpec((CMP_STRIDE, LANES, 4 * CMP_HIDDEN), f3),
              pl.BlockSpec((CMP_HIDDEN, LANES), f2)]
    return pl.pallas_call(
        _compress_kernel,
        grid=(bsz,),
        in_specs=[pl.BlockSpec((1, t, LANES), cb(COL_KC // LANES)),
                  pl.BlockSpec((1, t, LANES), cb(COL_KC // LANES + 1)),
                  pl.BlockSpec((1, t, LANES), cb(COL_VC // LANES)),
                  pl.BlockSpec((1, t, LANES), cb(COL_VC // LANES + 1)),
                  pl.BlockSpec((1, CMP_LEN * N_HD), f2),
                  pl.BlockSpec((1, CMP_LEN * N_HD), f2)] + wspecs + wspecs,
        out_specs=[pl.BlockSpec((1, N_KV, nb, 4 * N_HD), lambda bi: (bi, 0, 0, 0)),
                   pl.BlockSpec((1, N_KV, VROWS, nb), lambda bi: (bi, 0, 0, 0))],
        out_shape=[jax.ShapeDtypeStruct((bsz, N_KV, nb, 4 * N_HD), BF16),
                   jax.ShapeDtypeStruct((bsz, N_KV, VROWS, nb), BF16)],
        compiler_params=_cparams(("parallel",)),
        name="nsa_compress",
    )(p, p, p, p, posk, posv, w1k, w1kc, w2k, w1v, w1vc, w2v)


def _relayout_kernel(ks_ref, vs_ref, kw_ref, vw_ref, ksa_ref, vsa_ref, kwa_ref, vwa_ref):
    i = pl.program_id(1)
    is_pad = i == 0
    flag = jnp.where(is_pad, 1.0, 0.0)
    row = lax.broadcasted_iota(jnp.int32, (SKT, 1), 0)
    lane = lax.broadcasted_iota(jnp.int32, (SKT, N_HD), 1)
    al = lane - AUX_ROWS
    ones_blk = jnp.where(lax.broadcasted_iota(jnp.int32, (VROWS - N_HD, SKT), 0) == 0, 1.0, 0.0)
    for u in range(PAD_TILES):
        rows = slice(u * SKT, (u + 1) * SKT)
        base = ((i - 1) * PAD_TILES + u) * SKT
        pos = jnp.where(is_pad, 0, base + row)
        blk = pos // SLC_LEN
        p_hi = (blk * SLC_LEN).astype(F32)
        p_lo = (pos - blk * SLC_LEN).astype(F32)
        mid_w = jnp.where((al >= 0) & (al < 3), p_hi,
                          jnp.where((al >= 3) & (al < 6), p_lo, jnp.where(al == 6, flag, 0.0)))
        mid_s = jnp.where(lane == blk % SEL_GRP, 1.0, mid_w)
        ks = ks_ref[0, rows, :]
        kw = kw_ref[0, rows, :]
        for g in range(N_KV):
            sl = slice(g * N_HD, (g + 1) * N_HD)
            ksa_ref[0, g, rows, :] = jnp.concatenate([ks[:, sl], mid_s], axis=1).astype(BF16)
            kwa_ref[0, g, rows, :] = jnp.concatenate([kw[:, sl], mid_w], axis=1).astype(BF16)
        vst = vs_ref[0, rows, :].T
        vwt = vw_ref[0, rows, :].T
        for g in range(N_KV):
            sl = slice(g * N_HD, (g + 1) * N_HD)
            vsa_ref[0, g, u] = jnp.concatenate([vst[sl], ones_blk], axis=0).astype(BF16)
            vwg = jnp.concatenate([vwt[sl], ones_blk], axis=0).astype(BF16)
            for j in range(SKT // WKT):
                vwa_ref[0, g, u * (SKT // WKT) + j] = vwg[:, j * WKT:(j + 1) * WKT]


def _relayout(p):
    bsz, t, _ = p.shape
    nt = t // SKT + PAD_TILES
    wpt = SKT // WKT
    rt = PAD_TILES
    cb = lambda c: (lambda bi, i: (bi, jnp.maximum(i - 1, 0), c))
    return pl.pallas_call(
        _relayout_kernel,
        grid=(bsz, nt // rt),
        in_specs=[pl.BlockSpec((1, rt * SKT, KV_W), cb(COL_KS // KV_W)),
                  pl.BlockSpec((1, rt * SKT, KV_W), cb(COL_VS // KV_W)),
                  pl.BlockSpec((1, rt * SKT, KV_W), cb(COL_KW // KV_W)),
                  pl.BlockSpec((1, rt * SKT, KV_W), cb(COL_VW // KV_W))],
        out_specs=[pl.BlockSpec((1, N_KV, rt * SKT, KA_SEL), lambda bi, i: (bi, 0, i, 0)),
                   pl.BlockSpec((1, N_KV, rt, VROWS, SKT), lambda bi, i: (bi, 0, i, 0, 0)),
                   pl.BlockSpec((1, N_KV, rt * SKT, KA_WIN), lambda bi, i: (bi, 0, i, 0)),
                   pl.BlockSpec((1, N_KV, rt * wpt, VROWS, WKT), lambda bi, i: (bi, 0, i, 0, 0))],
        out_shape=[jax.ShapeDtypeStruct((bsz, N_KV, nt * SKT, KA_SEL), BF16),
                   jax.ShapeDtypeStruct((bsz, N_KV, nt, VROWS, SKT), BF16),
                   jax.ShapeDtypeStruct((bsz, N_KV, nt * SKT, KA_WIN), BF16),
                   jax.ShapeDtypeStruct((bsz, N_KV, nt * wpt, VROWS, WKT), BF16)],
        compiler_params=_cparams(("parallel", "parallel")),
        name="nsa_relayout",
    )(p, p, p, p)


def _nsa_batch_kernel(sp_ref, q_ref, gt_ref, z_ref, kcmp_ref, vcmpt_ref, ovt_ref, ks_ref, vs_ref, kw_ref,
                      vw_ref, tri_ref, wb_ref, y_ref, sa_scr, sb_scr, g_scr, imp_scr, rank_scr, part_scr, selb_scr,
                      tile_idx):
    g = pl.program_id(0)
    qb = pl.program_id(1)
    n_items = q_ref.shape[0]
    start = qb * QBLK
    nb = kcmp_ref.shape[2]
    n_slc = ovt_ref.shape[0]
    nw = N_HPG * QBLK
    n_top = min(SLC_TOPN, n_slc)
    t_row = start + lax.broadcasted_iota(jnp.int32, (1, QBLK), 1)
    s_bufs = (sa_scr, sb_scr)
    last = qb // (SKT // QBLK)

    def tile4(a):
        return jnp.concatenate([a] * N_HPG, axis=1)

    def col_reduce8(s, op):
        out = s[0:8]
        for r in range(1, s.shape[0] // 8):
            out = op(out, s[8 * r:8 * r + 8])
        return out

    def col_max(s):
        return jnp.max(col_reduce8(s, jnp.maximum), axis=0, keepdims=True)

    def normalise(acc):
        den = acc[N_HD:N_HD + 1]
        return acc[0:N_HD] / jnp.where(den > 0, den, 1.0)

    def gate_row(i, h, branch):
        return g_scr[i, pl.ds(GATE_NG + (g * N_HPG + h) * 3 + branch, 1), :]

    hl = lax.broadcasted_iota(jnp.int32, (AUX_ROWS, nw), 1) // QBLK
    ar = lax.broadcasted_iota(jnp.int32, (AUX_ROWS, nw), 0)

    def per_head(vals):
        out = jnp.full((AUX_ROWS, nw), vals[N_HPG - 1], F32)
        for h in range(N_HPG - 2, -1, -1):
            out = jnp.where(hl == h, vals[h], out)
        return out

    pieces = [per_head([sp_ref[(g * N_HPG + h) * 3 + j] for h in range(N_HPG)]) for j in range(3)]
    aux = jnp.where(ar == 6, NEG, 0.0)
    for j in range(3):
        aux = jnp.where((ar == j) | (ar == j + 3), pieces[j], aux)
    aux_b = aux.astype(BF16)
    q_tail = jnp.zeros((KA_SEL - N_HD - 2 * AUX_ROWS, nw), BF16)
    sidx = lax.broadcasted_iota(jnp.int32, (n_slc, QBLK), 0)
    cur = t_row // SLC_LEN
    valid = sidx <= cur
    sub8 = lax.broadcasted_iota(jnp.int32, (8, QBLK), 0)
    last_blk = (start + QBLK - 1) // SLC_LEN
    cmp_end = lax.broadcasted_iota(jnp.int32, (nb, 1), 0) * CMP_STRIDE + (CMP_LEN - 1)
    cmp_bias = tile4(jnp.where(cmp_end <= t_row, 0.0, NEG))
    wrows = WIN + QBLK

    def prepare(i):
        qt = (q_ref[i] * (N_HD ** -0.5)).T
        q4l = jnp.concatenate([qt[h * N_HD:(h + 1) * N_HD] for h in range(N_HPG)], axis=1) * LOG2E
        q4s = q4l.astype(BF16)
        q4lo = (q4l - q4s.astype(F32)).astype(BF16)
        g_scr[i] = jax.nn.sigmoid(gt_ref[i]).T

        q_cmp = jnp.concatenate([q4s, q4s, q4lo, aux_b, jnp.zeros((N_HD - AUX_ROWS, nw), BF16)], axis=0)
        s_c = jnp.dot(kcmp_ref[i, 0], q_cmp, preferred_element_type=F32) + cmp_bias
        e_c = jnp.exp2(s_c - col_max(s_c))
        z_c = jnp.sum(col_reduce8(e_c, jnp.add), axis=0, keepdims=True)
        inv_c = jnp.where(tile4(t_row >= CMP_LEN - 1) & (z_c > 0), 1.0 / z_c, 0.0)
        o_c = jnp.dot(vcmpt_ref[i, 0], e_c.astype(BF16), preferred_element_type=F32)[0:N_HD] * inv_c
        p_c = e_c * inv_c
        psum = p_c[:, 0:QBLK]
        for h in range(1, N_HPG):
            psum = psum + p_c[:, h * QBLK:(h + 1) * QBLK]
        parts, rest = [], psum
        for _ in range(3):
            parts.append(rest.astype(BF16))
            rest = rest - parts[-1].astype(F32)
        imp = jnp.dot(ovt_ref[...], jnp.concatenate(parts, axis=0), preferred_element_type=F32)
        imp = jnp.where(valid, imp, -jnp.inf)
        imp_scr[...] = jnp.where((sidx == 0) | (sidx == cur), jnp.inf, imp)

        q_win = jnp.concatenate([q4s, jnp.zeros((AUX_ROWS, nw), BF16), aux_b,
                                 jnp.zeros((KA_WIN - N_HD - 2 * AUX_ROWS, nw), BF16)], axis=0)
        s_w = jnp.dot(kw_ref[i, 0, pl.ds(pl.multiple_of(start, QBLK), wrows), :], q_win,
                      preferred_element_type=F32)
        s_w = jnp.concatenate([s_w[0:WKT] + tile4(wb_ref[0]), s_w[WKT:wrows - WKT],
                               s_w[wrows - WKT:] + tile4(wb_ref[1])], axis=0)
        p_w = jnp.exp2(s_w - col_max(s_w)).astype(BF16)
        v_w = jnp.concatenate([vw_ref[i, 0, qb * (QBLK // WKT) + j] for j in range(wrows // WKT)], axis=1)
        o_w = normalise(jnp.dot(v_w, p_w, preferred_element_type=F32))
        for h in range(N_HPG):
            hs = slice(h * QBLK, (h + 1) * QBLK)
            part_scr[i, :, hs] = gate_row(i, h, 0) * o_c[:, hs] + gate_row(i, h, 2) * o_w[:, hs]

        rank_scr[...] = jnp.zeros_like(rank_scr)
        for ri in range(n_slc // 8):
            @pl.when((last_blk >= n_top) & (8 * ri <= last_blk))
            def _():
                rows = imp_scr[8 * ri:8 * ri + 8]
                for r in range(n_slc // 8):
                    blk8 = imp_scr[8 * r:8 * r + 8]
                    acc = rank_scr[8 * r:8 * r + 8]
                    for ii in range(8):
                        row = rows[ii:ii + 1]
                        if ri < r:
                            before = row >= blk8
                        elif ri > r:
                            before = row > blk8
                        else:
                            before = (row > blk8) | ((row == blk8) & (sub8 > ii))
                        acc = acc + jnp.where(before, 1.0, 0.0)
                    rank_scr[8 * r:8 * r + 8] = acc
        chosen = (rank_scr[...] < n_top) & valid
        selb_scr[i] = tile4(jnp.where(chosen, 0.0, NEG))

        any_q = jnp.max(jnp.where(chosen, 1.0, 0.0), axis=1, keepdims=True) > 0
        s_io = lax.broadcasted_iota(jnp.int32, (n_slc, 1), 0)
        bits = jnp.where(any_q, jnp.left_shift(1, s_io % 32), 0)
        words = [jnp.sum(jnp.where(s_io // 32 == w, bits, 0)) for w in range((n_slc + 31) // 32)]
        bpt = SKT // SLC_LEN
        cnt = jnp.int32(0)
        for tile in range(n_slc // bpt):
            nib = lax.shift_right_logical(words[tile * bpt // 32], jnp.int32(tile * bpt % 32)) & (2 ** bpt - 1)
            tile_idx[i, cnt] = tile
            cnt = cnt + ((nib != 0) & (tile < last)).astype(jnp.int32)
        return q4s, cnt

    q4, n_act = zip(*[prepare(i) for i in range(n_items)])


    def sel_scores(i, kt):
        grp = pl.multiple_of((kt * SKT // SLC_LEN) // SEL_GRP * SEL_GRP, SEL_GRP)
        sel_rows = jnp.concatenate([selb_scr[i, pl.ds(grp, SEL_GRP), :], jnp.zeros((AUX_ROWS - SEL_GRP, nw), F32)],
                                   axis=0)
        q_sel = jnp.concatenate([q4[i], sel_rows.astype(BF16), aux_b, q_tail], axis=0)
        krow = pl.multiple_of((kt + PAD_TILES) * SKT, SKT)
        return jnp.dot(ks_ref[i, 0, pl.ds(krow, SKT), :], q_sel, preferred_element_type=F32)

    def tile_rows(kt):
        return pl.ds(pl.multiple_of(kt * SKT, SKT), SKT)

    def pass1(i):
        def fn(kt, mrun):
            s = sel_scores(i, kt)
            s_bufs[i % 2][tile_rows(kt), :] = s
            return jnp.maximum(mrun, col_reduce8(s, jnp.maximum))
        return fn

    def pass2(i, m_sel):
        def fn(kt, acc):
            pr = jnp.exp2(s_bufs[i % 2][tile_rows(kt), :] - m_sel).astype(BF16)
            return acc + jnp.dot(vs_ref[i, 0, kt + PAD_TILES], pr, preferred_element_type=F32)
        return fn

    def listed(i, fn):
        return lambda j, carry: fn(tile_idx[i, j], carry)

    def both(f1, f2):
        def fn(j, carry):
            return f1(j, carry[0]), f2(j, carry[1])
        return fn

    def run_tiles(fn, lo, n, init):
        def group(size, first):
            def body(j, carry):
                for u in range(size):
                    carry = fn(first + j * size + u, carry)
                return carry
            return body

        n_u = n // UNROLL
        carry = lax.fori_loop(0, n_u, group(UNROLL, lo), init)
        done = lo + n_u * UNROLL
        rest = n - n_u * UNROLL
        for size in sorted({s for plan in REST_GROUPS.values() for s in plan}, reverse=True):
            take = jnp.int32(0)
            for r, plan in REST_GROUPS.items():
                if size in plan:
                    take = take + (rest == r).astype(jnp.int32)
            carry = lax.fori_loop(0, take, group(size, done), carry)
            done = done + take * size
        return carry

    def diag_tile(i, mrun):
        s = sel_scores(i, last) + tile4(tri_ref[qb % (SKT // QBLK)])
        s_bufs[i % 2][tile_rows(last), :] = s
        return jnp.max(jnp.maximum(mrun, col_reduce8(s, jnp.maximum)), axis=0, keepdims=True)

    mrun0 = jnp.full((8, nw), NEG, F32)
    acc0 = jnp.zeros((VROWS, nw), F32)
    m_prev = diag_tile(0, run_tiles(listed(0, pass1(0)), 0, n_act[0], mrun0))
    outs = []
    for i in range(1, n_items):
        p2 = pass2(i - 1, m_prev)
        f1, f2 = listed(i, pass1(i)), listed(i - 1, p2)
        n_both = jnp.minimum(n_act[i], n_act[i - 1])
        mrun, acc = run_tiles(both(f1, f2), 0, n_both, (mrun0, acc0))
        mrun = run_tiles(f1, n_both, n_act[i] - n_both, mrun)
        acc = run_tiles(f2, n_both, n_act[i - 1] - n_both, acc)
        m_cur = diag_tile(i, mrun)
        outs.append(normalise(p2(last, acc)))
        m_prev = m_cur
    p2 = pass2(n_items - 1, m_prev)
    acc = run_tiles(listed(n_items - 1, p2), 0, n_act[n_items - 1], acc0)
    outs.append(normalise(p2(last, acc)))

    for i in range(n_items):
        tot = [part_scr[i, :, h * QBLK:(h + 1) * QBLK] + gate_row(i, h, 1) * outs[i][:, h * QBLK:(h + 1) * QBLK]
               for h in range(N_HPG)]
        o = jnp.concatenate(tot, axis=0).T
        y_ref[i] = (o * _silu(z_ref[i])).astype(BF16)


def _edge_biases():
    kl = np.arange(SKT)[:, None]
    ql = np.arange(QBLK)[None, :]
    tri = np.stack([np.where(kl <= par * QBLK + ql, 0.0, NEG) for par in range(SKT // QBLK)])
    kk = np.arange(WKT)[:, None]
    wb = np.stack([np.where(kk > ql, 0.0, NEG), np.where(kk <= ql + WKT - QBLK, 0.0, NEG)])
    return jnp.asarray(tri, F32), jnp.asarray(wb, F32)


def _nsa(p, spieces, kcmp, vcmpt, ovt, ks, vs, kw, vw):
    bsz, t, _ = p.shape
    n_slc = t // SLC_LEN
    gw = N_HPG * N_HD
    tri, wb = _edge_biases()
    nw = N_HPG * QBLK
    per_g = lambda a: pl.BlockSpec((bsz, 1) + a.shape[2:], lambda g, i, sp: (0, g) + (0,) * (a.ndim - 2),
                                   pipeline_mode=pl.Buffered(1))
    const = lambda a: pl.BlockSpec(a.shape, lambda g, i, sp: (0,) * a.ndim)
    grid_spec = pltpu.PrefetchScalarGridSpec(
        num_scalar_prefetch=1,
        grid=(N_KV, t // QBLK),
        in_specs=[pl.BlockSpec((bsz, QBLK, gw), lambda g, i, sp: (0, i, COL_NQ // gw + g)),
                  pl.BlockSpec((bsz, QBLK, LANES), lambda g, i, sp: (0, i, COL_GATES // LANES)),
                  pl.BlockSpec((bsz, QBLK, gw), lambda g, i, sp: (0, i, COL_NZ // gw + g)),
                  per_g(kcmp), per_g(vcmpt), const(ovt),
                  per_g(ks), per_g(vs), per_g(kw), per_g(vw), const(tri), const(wb)],
        out_specs=pl.BlockSpec((bsz, QBLK, gw), lambda g, i, sp: (0, i, g)),
        scratch_shapes=[pltpu.VMEM((t, nw), F32),
                        pltpu.VMEM((t, nw), F32),
                        pltpu.VMEM((bsz, LANES, QBLK), F32),
                        pltpu.VMEM((n_slc, QBLK), F32),
                        pltpu.VMEM((n_slc, QBLK), F32),
                        pltpu.VMEM((bsz, N_HD, nw), F32),
                        pltpu.VMEM((bsz, n_slc, nw), F32),
                        pltpu.SMEM((bsz, t // SKT), jnp.int32)],
    )
    return pl.pallas_call(
        _nsa_batch_kernel,
        grid_spec=grid_spec,
        out_shape=jax.ShapeDtypeStruct((bsz, t, N_WIDTH), BF16),
        compiler_params=_cparams(("arbitrary", "arbitrary")),
        name="nsa_attention",
    )(spieces, p, p, p, kcmp, vcmpt, ovt, ks, vs, kw, vw, tri, wb)


def _outproj_kernel(ym_ref, yn_ref, w_ref, x_ref, gate_ref, fg_ref, o_ref, wb_scr, *, final):
    @pl.when((pl.program_id(0) == 0) & (pl.program_id(1) == 0))
    def _():
        rows = w_ref.shape[1]
        for c in range(rows // WP_TILE):
            cs = slice(c * WP_TILE, (c + 1) * WP_TILE)
            wb_scr[cs, :] = w_ref[0, cs, :].astype(BF16)

    y = jnp.dot(ym_ref[0], wb_scr[0:M_WIDTH, :], preferred_element_type=F32)
    y = y + jnp.dot(yn_ref[0], wb_scr[M_WIDTH:, :], preferred_element_type=F32)
    hres = x_ref[0] + gate_ref[0] * y
    if final:
        ms = jnp.mean(hres * hres, axis=-1, keepdims=True)
        hres = hres * lax.rsqrt(ms + EPS) * fg_ref[...]
    o_ref[0] = hres


def _outproj(ym, yn, w, layer, x, gate, fg, final):
    bsz, t, d = x.shape
    tm = 512
    return pl.pallas_call(
        functools.partial(_outproj_kernel, final=final),
        grid=(bsz, t // tm),
        in_specs=[pl.BlockSpec((1, tm, M_WIDTH), lambda bi, i: (bi, i, 0)),
                  pl.BlockSpec((1, tm, N_WIDTH), lambda bi, i: (bi, i, 0)),
                  pl.BlockSpec((1, M_WIDTH + N_WIDTH, d), lambda bi, i: (layer, 0, 0),
                               pipeline_mode=pl.Buffered(1)),
                  pl.BlockSpec((1, tm, d), lambda bi, i: (bi, i, 0)),
                  pl.BlockSpec((1, 1, d), lambda bi, i: (bi, 0, 0)),
                  pl.BlockSpec((1, d), lambda bi, i: (0, 0))],
        out_specs=pl.BlockSpec((1, tm, d), lambda bi, i: (bi, i, 0)),
        out_shape=jax.ShapeDtypeStruct((bsz, t, d), F32),
        scratch_shapes=[pltpu.VMEM((M_WIDTH + N_WIDTH, d), BF16)],
        compiler_params=_cparams(("arbitrary", "arbitrary")),
        name="outproj_residual",
    )(ym, yn, w, x, gate, fg)


SRC_MI = 4 * M_WIDTH
SRC_NQ = SRC_MI + 2 * M_HEADS
SRC_NG = SRC_NQ + N_WIDTH + 6 * KV_W
SRC_NZ = SRC_NG + 3 * N_HEADS


def _reorder_cols(a):
    parts = [a[..., 0:SRC_MI], a[..., SRC_NQ:SRC_NG], a[..., SRC_NZ:SRC_NZ + N_WIDTH], a[..., SRC_MI:SRC_NQ],
             a[..., SRC_NG:SRC_NZ]]
    used = sum(x.shape[-1] for x in parts)
    parts.append(jnp.zeros(a.shape[:-1] + (NP_PAD - used,), a.dtype))
    return jnp.concatenate(parts, axis=-1)


WP_TILE = 512
WP_GATE_TILE = COL_GATES // WP_TILE


def _wprep_kernel(w_ref, g1_ref, g2_ref, o_ref):
    j = pl.program_id(0)
    d = o_ref.shape[0]

    @pl.when(j < WP_GATE_TILE)
    def _():
        for c in range(d // WP_TILE):
            cs = slice(c * WP_TILE, (c + 1) * WP_TILE)
            o_ref[cs, :] = w_ref[0, :, cs].T.astype(BF16)

    @pl.when(j == WP_GATE_TILE)
    def _():
        n_gate = g1_ref.shape[1] + g2_ref.shape[1]
        gt = jnp.concatenate([g1_ref[0], g2_ref[0], jnp.zeros((LANES - n_gate, d), F32)], axis=0)
        for c in range(d // WP_TILE):
            cs = slice(c * WP_TILE, (c + 1) * WP_TILE)
            o_ref[cs, 0:LANES] = gt[:, cs].T.astype(BF16)
        o_ref[:, LANES:] = jnp.zeros((d, WP_TILE - LANES), BF16)


def _wprep(w_t, layer):
    _, n, d = w_t.shape
    n_big = COL_GATES // WP_TILE
    assert n == SRC_NZ + N_WIDTH and NP_PAD // WP_TILE == n_big + 1

    def src_row(j):
        return jnp.where(j < COL_NQ // WP_TILE, j * WP_TILE,
                         jnp.where(j < COL_NZ // WP_TILE, SRC_NQ + (j - COL_NQ // WP_TILE) * WP_TILE,
                                   jnp.where(j < n_big, SRC_NZ + (j - COL_NZ // WP_TILE) * WP_TILE, 0)))

    el = pl.Element
    return pl.pallas_call(
        _wprep_kernel,
        grid=(NP_PAD // WP_TILE,),
        in_specs=[pl.BlockSpec((el(1), el(WP_TILE), el(d)), lambda j: (layer, pl.multiple_of(src_row(j), 8), 0)),
                  pl.BlockSpec((el(1), el(SRC_NQ - SRC_MI), el(d)), lambda j: (layer, SRC_MI, 0)),
                  pl.BlockSpec((el(1), el(SRC_NZ - SRC_NG), el(d)), lambda j: (layer, SRC_NG, 0))],
        out_specs=pl.BlockSpec((d, WP_TILE), lambda j: (0, j)),
        out_shape=jax.ShapeDtypeStruct((d, NP_PAD), BF16),
        compiler_params=_cparams(("parallel",)),
        name="inproj_weight_prep",
    )(w_t, w_t, w_t)


def _overlap_t(t):
    n_cmp_rows = t // CMP_STRIDE
    n_slc = t // SLC_LEN
    c0 = np.arange(n_cmp_rows) * CMP_STRIDE
    s0 = np.arange(n_slc) * SLC_LEN
    ov = (c0[None, :] <= s0[:, None] + SLC_LEN - 1) & (c0[None, :] + CMP_LEN - 1 >= s0[:, None])
    ov[:, (t - CMP_LEN) // CMP_STRIDE + 1:] = False
    return jnp.asarray(np.concatenate([ov] * 3, axis=1), BF16)


def kernel(x, c, ln_g, w_ada, b_ada, w_in, b_in, m_conv_w, m_conv_b, m_wq, m_wk, m_norm_w, m_skip, m_f_bias,
           n_pos_k, n_pos_v, n_w1_k, n_w2_k, n_w1_v, n_w2_v, w_out, final_g):
    out_dtype = x.dtype
    bsz, t, d = x.shape
    depth = ln_g.shape[0]
    h_res = x.astype(F32)
    assert bsz <= 2 and t % 1024 == 0 and t // CMP_STRIDE <= 256
    c_t =jnp.zeros((d, 8), F32).at[:, :bsz].set(c.astype(F32).T)
    slopes_np = np.array([2.0 ** (-8.0 * (h + 1) / N_HEADS) for h in range(N_HEADS)], np.float32)
    rest = (slopes_np.astype(np.float64) * LOG2E).astype(np.float32)
    pieces = []
    for _ in range(3):
        pieces.append(rest.astype(jnp.bfloat16).astype(np.float32))
        rest = rest - pieces[-1]
    spieces = jnp.asarray(np.stack(pieces, axis=1).reshape(-1))
    ovt = _overlap_t(t)

    def w1cat(w1):
        w = w1.reshape(2, CMP_STRIDE, N_HD, CMP_HIDDEN)
        w = jnp.concatenate([w[0], w[1]], axis=-1).astype(BF16)
        z = jnp.zeros_like(w)
        return jnp.concatenate([jnp.concatenate([w, z], axis=-1), jnp.concatenate([z, w], axis=-1)], axis=1)

    def w2pad(w2):
        return jnp.pad(w2, ((0, 0), (0, LANES - N_HD))).astype(BF16)

    for l in range(depth):
        mod = _ada(c_t, w_ada, b_ada[l][None, :], l, bsz)[:bsz]
        shift, scale, gate = mod[:, None, 0:d], mod[:, None, d:2 * d], mod[:, None, 2 * d:3 * d]
        p = _inproj(h_res, ln_g[l][None, :], scale, shift,
                    _wprep(jnp.swapaxes(w_in, 1, 2), l), _reorder_cols(b_in[l])[None, :])
        fb_row = jnp.zeros((1, LANES), F32).at[0, M_HEADS:2 * M_HEADS].set(m_f_bias[l])
        y_m = _mlstm(p, m_conv_w[l], m_conv_b[l][None, :], m_wq[l].astype(BF16), m_wk[l].astype(BF16),
                     m_norm_w[l][None, :], m_skip[l][None, :], fb_row)
        kcmp, vcmpt = _compress(p, n_pos_k[l].reshape(1, -1), n_pos_v[l].reshape(1, -1),
                                n_w1_k[l].astype(BF16), w1cat(n_w1_k[l]), w2pad(n_w2_k[l]),
                                n_w1_v[l].astype(BF16), w1cat(n_w1_v[l]), w2pad(n_w2_v[l]))
        ks, vs, kw, vw = _relayout(p)
        y_n = _nsa(p, spieces, kcmp, vcmpt, ovt, ks, vs, kw, vw)
        h_res = _outproj(y_m, y_n, w_out, l, h_res, gate, final_g[None, :], l == depth - 1)
    return h_res.astype(out_dtype)
```

```python
import functools

import numpy as np
import jax
import jax.numpy as jnp
from jax import lax
from jax.experimental import pallas as pl
from jax.experimental.pallas import tpu as pltpu

F32 = jnp.float32
BF16 = jnp.bfloat16

EPS = 1e-6
M_HEADS = 4
M_HD = 256
M_WIDTH = M_HEADS * M_HD
CONV_K = 4
M_CHUNK = 256
N_HEADS = 16
N_HD = 64
N_KV = 4
N_HPG = N_HEADS // N_KV
N_WIDTH = N_HEADS * N_HD
KV_W = N_KV * N_HD
CMP_LEN = 32
CMP_STRIDE = 16
CMP_HIDDEN = 2 * N_HD
SLC_LEN = 64
SLC_TOPN = 16
WIN = 512
QBLK = 256
SKT = 256
WKT = 256
PAD_TILES = WIN // SKT
SEL_GRP = 8
KA_SEL = 128
KA_WIN = 128
VROWS = 80
UNROLL = 8
AUX_ROWS = 16
LOG2E = 1.4426950408889634

COL_MX, COL_MV, COL_MO, COL_MZ = 0, 1024, 2048, 3072
COL_NQ = 4096
COL_KC, COL_VC, COL_KS, COL_VS, COL_KW, COL_VW = 5120, 5376, 5632, 5888, 6144, 6400
COL_NZ = 6656
COL_GATES = 7680
GATE_NG = 2 * M_HEADS
NP_PAD = 8192
NORM_CHUNKS = 4
LANES = 128
NEG = -1e30
VMEM_LIMIT = 58 * 1024 * 1024


def _cparams(sem):
    return pltpu.CompilerParams(dimension_semantics=sem, vmem_limit_bytes=VMEM_LIMIT)


def _silu(x):
    return x * jax.nn.sigmoid(x)


def _log_sigmoid(x):
    return jnp.minimum(x, 0.0) - jnp.log1p(jnp.exp(-jnp.abs(x)))


def _ada_kernel(ct_ref, w_ref, b_ref, o_ref, *, bsz):
    s_t = _silu(ct_ref[...])
    w = w_ref[0]
    row = lax.broadcasted_iota(jnp.int32, o_ref.shape, 0)
    out = jnp.zeros(o_ref.shape, F32)
    for b in range(bsz):
        prod = w * s_t[:, b:b + 1]
        acc = prod[0:8]
        for r in range(1, prod.shape[0] // 8):
            acc = acc + prod[8 * r:8 * r + 8]
        out = jnp.where(row == b, jnp.sum(acc, axis=0, keepdims=True) + b_ref[...], out)
    o_ref[...] = out


def _ada(c_t, w, b, layer, bsz):
    _, d, n = w.shape
    tn = 1024
    return pl.pallas_call(
        functools.partial(_ada_kernel, bsz=bsz),
        grid=(n // tn,),
        in_specs=[pl.BlockSpec((d, 8), lambda j: (0, 0)),
                  pl.BlockSpec((1, d, tn), lambda j: (layer, 0, j)),
                  pl.BlockSpec((1, tn), lambda j: (0, j))],
        out_specs=pl.BlockSpec((8, tn), lambda j: (0, j)),
        out_shape=jax.ShapeDtypeStruct((8, n), F32),
        compiler_params=_cparams(("parallel",)),
        name="ada_mod",
    )(c_t, w, b)


def _inproj_kernel(x_ref, g_ref, sc_ref, sh_ref, w_ref, b_ref, o_ref, h_ref):
    first = pl.program_id(2) == 0

    @pl.when(first)
    def _():
        tm = x_ref.shape[1]
        ck = tm // NORM_CHUNKS
        for c in range(NORM_CHUNKS):
            rows = slice(c * ck, (c + 1) * ck)
            x = x_ref[0, rows, :]
            ms = jnp.mean(x * x, axis=-1, keepdims=True)
            h = x * lax.rsqrt(ms + EPS) * g_ref[...]
            h = (h * (1.0 + sc_ref[0]) + sh_ref[0]).astype(BF16)
            h_ref[rows, :] = h
            o_ref[0, rows, :] = jnp.dot(h, w_ref[...], preferred_element_type=F32) + b_ref[...]

    @pl.when(jnp.logical_not(first))
    def _():
        o_ref[0] = jnp.dot(h_ref[...], w_ref[...], preferred_element_type=F32) + b_ref[...]


def _inproj(x, g, scale, shift, w, b):
    bsz, t, d = x.shape
    n = w.shape[1]
    tm, tn = 1024, 1024
    return pl.pallas_call(
        _inproj_kernel,
        grid=(bsz, t // tm, n // tn),
        in_specs=[pl.BlockSpec((1, tm, d), lambda bi, i, j: (bi, i, 0)),
                  pl.BlockSpec((1, d), lambda bi, i, j: (0, 0)),
                  pl.BlockSpec((1, 1, d), lambda bi, i, j: (bi, 0, 0)),
                  pl.BlockSpec((1, 1, d), lambda bi, i, j: (bi, 0, 0)),
                  pl.BlockSpec((d, tn), lambda bi, i, j: (0, j)),
                  pl.BlockSpec((1, tn), lambda bi, i, j: (0, j))],
        out_specs=pl.BlockSpec((1, tm, tn), lambda bi, i, j: (bi, i, j)),
        out_shape=jax.ShapeDtypeStruct((bsz, t, n), F32),
        scratch_shapes=[pltpu.VMEM((tm, d), BF16)],
        compiler_params=_cparams(("parallel", "parallel", "arbitrary")),
        name="norm_inproj",
    )(x, g, scale, shift, w, b)


def _mlstm_kernel(x_ref, v_ref, o_ref, z_ref, gt_ref, cw_ref, cb_ref, wq_ref, wk_ref, nw_ref, sk_ref, fb_ref,
                  y_ref, c_scr, n_scr, m_scr, xp_scr):
    L = M_CHUNK

    @pl.when(pl.program_id(1) == 0)
    def _():
        c_scr[...] = jnp.zeros_like(c_scr)
        n_scr[...] = jnp.zeros_like(n_scr)
        m_scr[...] = jnp.zeros_like(m_scr)
        xp_scr[...] = jnp.zeros_like(xp_scr)

    x = x_ref[0]
    prev = xp_scr[...]
    row8 = lax.broadcasted_iota(jnp.int32, (8, M_WIDTH), 0)
    cw = cw_ref[...]
    xc = cb_ref[...] + x * cw[CONV_K - 1:CONV_K, :]
    for sft in range(1, CONV_K):
        xr = pltpu.roll(x, sft, 0)
        top = jnp.where(row8 < sft, pltpu.roll(prev, sft, 0), xr[0:8])
        xs = jnp.concatenate([top, xr[8:]], axis=0)
        xc = xc + xs * cw[CONV_K - 1 - sft:CONV_K - sft, :]
    xp_scr[...] = x[L - 8:L]
    xc = _silu(xc)

    gt = gt_ref[0]
    col = lax.broadcasted_iota(jnp.int32, (L, LANES), 1)
    logf = _log_sigmoid(gt + fb_ref[...])
    a_c = jnp.where((col >= M_HEADS) & (col < 2 * M_HEADS), logf, gt)
    ri = lax.broadcasted_iota(jnp.int32, (L, L), 0)
    ci = lax.broadcasted_iota(jnp.int32, (L, L), 1)
    causal = ri >= ci
    hp = lax.Precision.HIGHEST
    tri = causal.astype(F32)
    tri_t = (ri <= ci).astype(F32)
    b_c = jnp.dot(tri, a_c, precision=hp, preferred_element_type=F32)
    a_r = a_c.T
    b_r = jnp.dot(a_r[0:8], tri_t, precision=hp, preferred_element_type=F32)

    for h in range(M_HEADS):
        sl = slice(h * M_HD, (h + 1) * M_HD)
        xh = xc[:, sl]
        xb = xh.astype(BF16)
        q = jnp.dot(xb, wq_ref[h], preferred_element_type=F32)
        k = jnp.dot(xb, wk_ref[h], preferred_element_type=F32) * (M_HD ** -0.5)
        vb = v_ref[0, :, sl].astype(BF16)
        qb = q.astype(BF16)
        kb = k.astype(BF16)

        bt = b_c[:, M_HEADS + h:M_HEADS + h + 1]
        ic = a_c[:, h:h + 1]
        bs = b_r[M_HEADS + h:M_HEADS + h + 1, :]
        ir = a_r[h:h + 1, :]
        m_prev = m_scr[h][:, 0:1]

        dm = jnp.where(causal, bt - bs + ir, -jnp.inf)
        inter = bt + m_prev
        m_t = jnp.maximum(inter, jnp.max(dm, axis=-1, keepdims=True))
        w_in = jnp.exp(dm - m_t)
        w_st = jnp.exp(inter - m_t)
        s = lax.dot_general(qb, kb, (((1,), (1,)), ((), ())), preferred_element_type=F32) * w_in
        cmat = c_scr[h]
        nvec = n_scr[h]
        sb = s.astype(BF16)
        num = w_st * jnp.dot(qb, cmat.astype(BF16), preferred_element_type=F32) \
            + jnp.dot(sb, vb, preferred_element_type=F32)
        nt_dims = (((1,), (1,)), ((), ()))
        qn = lax.dot_general(qb, jnp.broadcast_to(nvec, (8, M_HD)).astype(BF16), nt_dims,
                             preferred_element_type=F32)[:, 0:1]
        ssum = lax.dot_general(sb, jnp.ones((8, L), BF16), nt_dims, preferred_element_type=F32)[:, 0:1]
        den = w_st * qn + ssum
        hh = num / jnp.maximum(jnp.abs(den), jnp.exp(-m_t))

        b_last = bt[L - 1:L, :]
        w_end = b_last - bt + ic
        m_new = jnp.maximum(b_last + m_prev, jnp.max(w_end, axis=0, keepdims=True))
        decay = jnp.exp(b_last + m_prev - m_new)
        kwt = k * jnp.exp(w_end - m_new)
        c_scr[h] = decay * cmat + lax.dot_general(kwt.astype(BF16), vb, (((0,), (0,)), ((), ())),
                                                  preferred_element_type=F32)
        n_scr[h] = decay * nvec + jnp.sum(kwt, axis=0, keepdims=True)
        m_scr[h] = jnp.broadcast_to(m_new, (1, LANES))

        mu = jnp.mean(hh, axis=-1, keepdims=True)
        hc = hh - mu
        var = jnp.mean(hc * hc, axis=-1, keepdims=True)
        hn = hc * lax.rsqrt(var + EPS) * nw_ref[:, sl]
        out = jax.nn.sigmoid(o_ref[0, :, sl]) * hn + sk_ref[:, sl] * xh
        y_ref[0, :, sl] = (out * _silu(z_ref[0, :, sl])).astype(BF16)


def _mlstm(p, conv_w, conv_b, wq, wk, norm_w, skip, fb_row):
    bsz, t, _ = p.shape
    L = M_CHUNK
    cb = lambda c: (lambda bi, i: (bi, i, c))
    full2 = lambda bi, i: (0, 0)
    full3 = lambda bi, i: (0, 0, 0)
    return pl.pallas_call(
        _mlstm_kernel,
        grid=(bsz, t // L),
        in_specs=[pl.BlockSpec((1, L, M_WIDTH), cb(COL_MX // M_WIDTH)),
                  pl.BlockSpec((1, L, M_WIDTH), cb(COL_MV // M_WIDTH)),
                  pl.BlockSpec((1, L, M_WIDTH), cb(COL_MO // M_WIDTH)),
                  pl.BlockSpec((1, L, M_WIDTH), cb(COL_MZ // M_WIDTH)),
                  pl.BlockSpec((1, L, LANES), cb(COL_GATES // LANES)),
                  pl.BlockSpec((CONV_K, M_WIDTH), full2),
                  pl.BlockSpec((1, M_WIDTH), full2),
                  pl.BlockSpec((M_HEADS, M_HD, M_HD), full3),
                  pl.BlockSpec((M_HEADS, M_HD, M_HD), full3),
                  pl.BlockSpec((1, M_WIDTH), full2),
                  pl.BlockSpec((1, M_WIDTH), full2),
                  pl.BlockSpec((1, LANES), full2)],
        out_specs=pl.BlockSpec((1, L, M_WIDTH), lambda bi, i: (bi, i, 0)),
        out_shape=jax.ShapeDtypeStruct((bsz, t, M_WIDTH), BF16),
        scratch_shapes=[pltpu.VMEM((M_HEADS, M_HD, M_HD), F32),
                        pltpu.VMEM((M_HEADS, 1, M_HD), F32),
                        pltpu.VMEM((M_HEADS, 1, LANES), F32),
                        pltpu.VMEM((8, M_WIDTH), F32)],
        compiler_params=_cparams(("parallel", "arbitrary")),
        name="mlstm_group",
    )(p, p, p, p, p, conv_w, conv_b, wq, wk, norm_w, skip, fb_row)


def _compress_kernel(kc0_ref, kc1_ref, vc0_ref, vc1_ref, posk_ref, posv_ref, w1k_ref, w1kc_ref, w2k_ref,
                     w1v_ref, w1vc_ref, w2v_ref, kcmp_ref, vcmpt_ref):
    nb = kcmp_ref.shape[2]

    def hidden(src_refs, pos_ref, w1_ref, w1c_ref):
        halves = [jnp.zeros((nb, 4 * CMP_HIDDEN), F32) for _ in src_refs]
        for l in range(CMP_STRIDE):
            for hf, r in enumerate(src_refs):
                xl = r[0, pl.ds(l, nb, stride=CMP_STRIDE), :]
                halves[hf] = halves[hf] + jnp.dot(xl.astype(BF16), w1c_ref[l], preferred_element_type=F32)
        acc = jnp.concatenate([hv[:, c * 2 * CMP_HIDDEN:(c + 1) * 2 * CMP_HIDDEN] for hv in halves for c in range(2)],
                              axis=0)
        first = acc[:, :CMP_HIDDEN]
        second = acc[:, CMP_HIDDEN:]
        posb = jnp.dot(jnp.broadcast_to(pos_ref[...], (8, CMP_LEN * N_HD)).astype(BF16), w1_ref[...],
                       preferred_element_type=F32)[0:1]
        hid = first + pltpu.roll(second, N_KV * nb - 1, 0) + posb
        return jax.nn.gelu(hid).astype(BF16)

    hk = hidden((kc0_ref, kc1_ref), posk_ref, w1k_ref, w1kc_ref)
    kc = jnp.dot(hk, w2k_ref[...], preferred_element_type=F32)
    hv = hidden((vc0_ref, vc1_ref), posv_ref, w1v_ref, w1vc_ref)
    vc = jnp.dot(hv, w2v_ref[...], preferred_element_type=F32)
    j = lax.broadcasted_iota(jnp.int32, (nb, 1), 0)
    lane = lax.broadcasted_iota(jnp.int32, (nb, N_HD), 1)
    n_real = (nb * CMP_STRIDE - CMP_LEN) // CMP_STRIDE + 1
    aux = jnp.where(lane < 3, (j * CMP_STRIDE).astype(F32),
                    jnp.where(lane < 6, (CMP_LEN - 1) * 0.5,
                              jnp.where((lane == 6) & (j >= n_real), 1.0, 0.0))).astype(BF16)
    ones_blk = jnp.where(lax.broadcasted_iota(jnp.int32, (VROWS - N_HD, nb), 0) == 0, 1.0, 0.0)
    for g in range(N_KV):
        kg = kc[g * nb:(g + 1) * nb, 0:N_HD]
        k_hi = kg.astype(BF16)
        k_lo = (kg - k_hi.astype(F32)).astype(BF16)
        kcmp_ref[0, g] = jnp.concatenate([k_hi, k_lo, k_hi, aux], axis=1)
        vt = vc[g * nb:(g + 1) * nb].T
        vcmpt_ref[0, g] = jnp.concatenate([vt[0:N_HD], ones_blk], axis=0).astype(BF16)


def _compress(p, posk, posv, w1k, w1kc, w2k, w1v, w1vc, w2v):
    bsz, t, _ = p.shape
    nb = t // CMP_STRIDE
    cb = lambda c: (lambda bi: (bi, 0, c))
    f2 = lambda bi: (0, 0)
    f3 = lambda bi: (0, 0, 0)
    wspecs = [pl.BlockSpec((CMP_LEN * N_HD, CMP_HIDDEN), f2),
              pl.BlockSpec((CMP_STRIDE, LANES, 4 * CMP_HIDDEN), f3),
              pl.BlockSpec((CMP_HIDDEN, LANES), f2)]
    return pl.pallas_call(
        _compress_kernel,
        grid=(bsz,),
        in_specs=[pl.BlockSpec((1, t, LANES), cb(COL_KC // LANES)),
                  pl.BlockSpec((1, t, LANES), cb(COL_KC // LANES + 1)),
                  pl.BlockSpec((1, t, LANES), cb(COL_VC // LANES)),
                  pl.BlockSpec((1, t, LANES), cb(COL_VC // LANES + 1)),
                  pl.BlockSpec((1, CMP_LEN * N_HD), f2),
                  pl.BlockSpec((1, CMP_LEN * N_HD), f2)] + wspecs + wspecs,
        out_specs=[pl.BlockSpec((1, N_KV, nb, 4 * N_HD), lambda bi: (bi, 0, 0, 0)),
                   pl.BlockSpec((1, N_KV, VROWS, nb), lambda bi: (bi, 0, 0, 0))],
        out_shape=[jax.ShapeDtypeStruct((bsz, N_KV, nb, 4 * N_HD), BF16),
                   jax.ShapeDtypeStruct((bsz, N_KV, VROWS, nb), BF16)],
        compiler_params=_cparams(("parallel",)),
        name="nsa_compress",
    )(p, p, p, p, posk, posv, w1k, w1kc, w2k, w1v, w1vc, w2v)


def _relayout_kernel(ks_ref, vs_ref, kw_ref, vw_ref, ksa_ref, vsa_ref, kwa_ref, vwa_ref):
    i = pl.program_id(1)
    is_pad = i == 0
    flag = jnp.where(is_pad, 1.0, 0.0)
    row = lax.broadcasted_iota(jnp.int32, (SKT, 1), 0)
    lane = lax.broadcasted_iota(jnp.int32, (SKT, N_HD), 1)
    al = lane - AUX_ROWS
    ones_blk = jnp.where(lax.broadcasted_iota(jnp.int32, (VROWS - N_HD, SKT), 0) == 0, 1.0, 0.0)
    for u in range(PAD_TILES):
        rows = slice(u * SKT, (u + 1) * SKT)
        base = ((i - 1) * PAD_TILES + u) * SKT
        pos = jnp.where(is_pad, 0, base + row)
        blk = pos // SLC_LEN
        p_hi = (blk * SLC_LEN).astype(F32)
        p_lo = (pos - blk * SLC_LEN).astype(F32)
        mid_w = jnp.where((al >= 0) & (al < 3), p_hi,
                          jnp.where((al >= 3) & (al < 6), p_lo, jnp.where(al == 6, flag, 0.0)))
        mid_s = jnp.where(lane == blk % SEL_GRP, 1.0, mid_w)
        ks = ks_ref[0, rows, :]
        kw = kw_ref[0, rows, :]
        for g in range(N_KV):
            sl = slice(g * N_HD, (g + 1) * N_HD)
            ksa_ref[0, g, rows, :] = jnp.concatenate([ks[:, sl], mid_s], axis=1).astype(BF16)
            kwa_ref[0, g, rows, :] = jnp.concatenate([kw[:, sl], mid_w], axis=1).astype(BF16)
        vst = vs_ref[0, rows, :].T
        vwt = vw_ref[0, rows, :].T
        for g in range(N_KV):
            sl = slice(g * N_HD, (g + 1) * N_HD)
            vsa_ref[0, g, u] = jnp.concatenate([vst[sl], ones_blk], axis=0).astype(BF16)
            vwg = jnp.concatenate([vwt[sl], ones_blk], axis=0).astype(BF16)
            for j in range(SKT // WKT):
                vwa_ref[0, g, u * (SKT // WKT) + j] = vwg[:, j * WKT:(j + 1) * WKT]


def _relayout(p):
    bsz, t, _ = p.shape
    nt = t // SKT + PAD_TILES
    wpt = SKT // WKT
    rt = PAD_TILES
    cb = lambda c: (lambda bi, i: (bi, jnp.maximum(i - 1, 0), c))
    return pl.pallas_call(
        _relayout_kernel,
        grid=(bsz, nt // rt),
        in_specs=[pl.BlockSpec((1, rt * SKT, KV_W), cb(COL_KS // KV_W)),
                  pl.BlockSpec((1, rt * SKT, KV_W), cb(COL_VS // KV_W)),
                  pl.BlockSpec((1, rt * SKT, KV_W), cb(COL_KW // KV_W)),
                  pl.BlockSpec((1, rt * SKT, KV_W), cb(COL_VW // KV_W))],
        out_specs=[pl.BlockSpec((1, N_KV, rt * SKT, KA_SEL), lambda bi, i: (bi, 0, i, 0)),
                   pl.BlockSpec((1, N_KV, rt, VROWS, SKT), lambda bi, i: (bi, 0, i, 0, 0)),
                   pl.BlockSpec((1, N_KV, rt * SKT, KA_WIN), lambda bi, i: (bi, 0, i, 0)),
                   pl.BlockSpec((1, N_KV, rt * wpt, VROWS, WKT), lambda bi, i: (bi, 0, i, 0, 0))],
        out_shape=[jax.ShapeDtypeStruct((bsz, N_KV, nt * SKT, KA_SEL), BF16),
                   jax.ShapeDtypeStruct((bsz, N_KV, nt, VROWS, SKT), BF16),
                   jax.ShapeDtypeStruct((bsz, N_KV, nt * SKT, KA_WIN), BF16),
                   jax.ShapeDtypeStruct((bsz, N_KV, nt * wpt, VROWS, WKT), BF16)],
        compiler_params=_cparams(("parallel", "parallel")),
        name="nsa_relayout",
    )(p, p, p, p)


def _nsa_batch_kernel(sp_ref, q_ref, gt_ref, z_ref, kcmp_ref, vcmpt_ref, ovt_ref, ks_ref, vs_ref, kw_ref,
                      vw_ref, tri_ref, wb_ref, y_ref, sa_scr, sb_scr, g_scr, imp_scr, rank_scr, part_scr, selb_scr,
                      tile_idx):
    g = pl.program_id(0)
    qb = pl.program_id(1)
    n_items = q_ref.shape[0]
    start = qb * QBLK
    nb = kcmp_ref.shape[2]
    n_slc = ovt_ref.shape[0]
    nw = N_HPG * QBLK
    n_top = min(SLC_TOPN, n_slc)
    t_row = start + lax.broadcasted_iota(jnp.int32, (1, QBLK), 1)
    s_bufs = (sa_scr, sb_scr)
    last = qb // (SKT // QBLK)

    def tile4(a):
        return jnp.concatenate([a] * N_HPG, axis=1)

    def col_reduce8(s, op):
        out = s[0:8]
        for r in range(1, s.shape[0] // 8):
            out = op(out, s[8 * r:8 * r + 8])
        return out

    def col_max(s):
        return jnp.max(col_reduce8(s, jnp.maximum), axis=0, keepdims=True)

    def normalise(acc):
        den = acc[N_HD:N_HD + 1]
        return acc[0:N_HD] / jnp.where(den > 0, den, 1.0)

    def gate_row(i, h, branch):
        return g_scr[i, pl.ds(GATE_NG + (g * N_HPG + h) * 3 + branch, 1), :]

    hl = lax.broadcasted_iota(jnp.int32, (AUX_ROWS, nw), 1) // QBLK
    ar = lax.broadcasted_iota(jnp.int32, (AUX_ROWS, nw), 0)

    def per_head(vals):
        out = jnp.full((AUX_ROWS, nw), vals[N_HPG - 1], F32)
        for h in range(N_HPG - 2, -1, -1):
            out = jnp.where(hl == h, vals[h], out)
        return out

    pieces = [per_head([sp_ref[(g * N_HPG + h) * 3 + j] for h in range(N_HPG)]) for j in range(3)]
    aux = jnp.where(ar == 6, NEG, 0.0)
    for j in range(3):
        aux = jnp.where((ar == j) | (ar == j + 3), pieces[j], aux)
    aux_b = aux.astype(BF16)
    q_tail = jnp.zeros((KA_SEL - N_HD - 2 * AUX_ROWS, nw), BF16)
    sidx = lax.broadcasted_iota(jnp.int32, (n_slc, QBLK), 0)
    cur = t_row // SLC_LEN
    valid = sidx <= cur
    sub8 = lax.broadcasted_iota(jnp.int32, (8, QBLK), 0)
    last_blk = (start + QBLK - 1) // SLC_LEN
    cmp_end = lax.broadcasted_iota(jnp.int32, (nb, 1), 0) * CMP_STRIDE + (CMP_LEN - 1)
    cmp_bias = tile4(jnp.where(cmp_end <= t_row, 0.0, NEG))
    wrows = WIN + QBLK

    def prepare(i):
        qt = (q_ref[i] * (N_HD ** -0.5)).T
        q4l = jnp.concatenate([qt[h * N_HD:(h + 1) * N_HD] for h in range(N_HPG)], axis=1) * LOG2E
        q4s = q4l.astype(BF16)
        q4lo = (q4l - q4s.astype(F32)).astype(BF16)
        g_scr[i] = jax.nn.sigmoid(gt_ref[i]).T

        q_cmp = jnp.concatenate([q4s, q4s, q4lo, aux_b, jnp.zeros((N_HD - AUX_ROWS, nw), BF16)], axis=0)
        s_c = jnp.dot(kcmp_ref[i, 0], q_cmp, preferred_element_type=F32) + cmp_bias
        e_c = jnp.exp2(s_c - col_max(s_c))
        z_c = jnp.sum(col_reduce8(e_c, jnp.add), axis=0, keepdims=True)
        inv_c = jnp.where(tile4(t_row >= CMP_LEN - 1) & (z_c > 0), 1.0 / z_c, 0.0)
        o_c = jnp.dot(vcmpt_ref[i, 0], e_c.astype(BF16), preferred_element_type=F32)[0:N_HD] * inv_c
        p_c = e_c * inv_c
        psum = p_c[:, 0:QBLK]
        for h in range(1, N_HPG):
            psum = psum + p_c[:, h * QBLK:(h + 1) * QBLK]
        parts, rest = [], psum
        for _ in range(3):
            parts.append(rest.astype(BF16))
            rest = rest - parts[-1].astype(F32)
        imp = jnp.dot(ovt_ref[...], jnp.concatenate(parts, axis=0), preferred_element_type=F32)
        imp = jnp.where(valid, imp, -jnp.inf)
        imp_scr[...] = jnp.where((sidx == 0) | (sidx == cur), jnp.inf, imp)

        q_win = jnp.concatenate([q4s, jnp.zeros((AUX_ROWS, nw), BF16), aux_b,
                                 jnp.zeros((KA_WIN - N_HD - 2 * AUX_ROWS, nw), BF16)], axis=0)
        s_w = jnp.dot(kw_ref[i, 0, pl.ds(pl.multiple_of(start, QBLK), wrows), :], q_win,
                      preferred_element_type=F32)
        s_w = jnp.concatenate([s_w[0:WKT] + tile4(wb_ref[0]), s_w[WKT:wrows - WKT],
                               s_w[wrows - WKT:] + tile4(wb_ref[1])], axis=0)
        p_w = jnp.exp2(s_w - col_max(s_w)).astype(BF16)
        v_w = jnp.concatenate([vw_ref[i, 0, qb * (QBLK // WKT) + j] for j in range(wrows // WKT)], axis=1)
        o_w = normalise(jnp.dot(v_w, p_w, preferred_element_type=F32))
        for h in range(N_HPG):
            hs = slice(h * QBLK, (h + 1) * QBLK)
            part_scr[i, :, hs] = gate_row(i, h, 0) * o_c[:, hs] + gate_row(i, h, 2) * o_w[:, hs]

        rank_scr[...] = jnp.zeros_like(rank_scr)
        for ri in range(n_slc // 8):
            @pl.when((last_blk >= n_top) & (8 * ri <= last_blk))
            def _():
                rows = imp_scr[8 * ri:8 * ri + 8]
                for r in range(n_slc // 8):
                    blk8 = imp_scr[8 * r:8 * r + 8]
                    acc = rank_scr[8 * r:8 * r + 8]
                    for ii in range(8):
                        row = rows[ii:ii + 1]
                        if ri < r:
                            before = row >= blk8
                        elif ri > r:
                            before = row > blk8
                        else:
                            before = (row > blk8) | ((row == blk8) & (sub8 > ii))
                        acc = acc + jnp.where(before, 1.0, 0.0)
                    rank_scr[8 * r:8 * r + 8] = acc
        chosen = (rank_scr[...] < n_top) & valid
        selb_scr[i] = tile4(jnp.where(chosen, 0.0, NEG))

        any_q = jnp.max(jnp.where(chosen, 1.0, 0.0), axis=1, keepdims=True) > 0
        s_io = lax.broadcasted_iota(jnp.int32, (n_slc, 1), 0)
        bits = jnp.where(any_q, jnp.left_shift(1, s_io % 32), 0)
        words = [jnp.sum(jnp.where(s_io // 32 == w, bits, 0)) for w in range((n_slc + 31) // 32)]
        bpt = SKT // SLC_LEN
        cnt = jnp.int32(0)
        for tile in range(n_slc // bpt):
            nib = lax.shift_right_logical(words[tile * bpt // 32], jnp.int32(tile * bpt % 32)) & (2 ** bpt - 1)
            tile_idx[i, cnt] = tile
            cnt = cnt + ((nib != 0) & (tile < last)).astype(jnp.int32)
        return q4s, cnt

    q4, n_act = zip(*[prepare(i) for i in range(n_items)])


    def sel_scores(i, kt):
        grp = pl.multiple_of((kt * SKT // SLC_LEN) // SEL_GRP * SEL_GRP, SEL_GRP)
        sel_rows = jnp.concatenate([selb_scr[i, pl.ds(grp, SEL_GRP), :], jnp.zeros((AUX_ROWS - SEL_GRP, nw), F32)],
                                   axis=0)
        q_sel = jnp.concatenate([q4[i], sel_rows.astype(BF16), aux_b, q_tail], axis=0)
        krow = pl.multiple_of((kt + PAD_TILES) * SKT, SKT)
        return jnp.dot(ks_ref[i, 0, pl.ds(krow, SKT), :], q_sel, preferred_element_type=F32)

    def tile_rows(kt):
        return pl.ds(pl.multiple_of(kt * SKT, SKT), SKT)

    def pass1(i):
        def fn(kt, mrun):
            s = sel_scores(i, kt)
            s_bufs[i % 2][tile_rows(kt), :] = s
            return jnp.maximum(mrun, col_reduce8(s, jnp.maximum))
        return fn

    def pass2(i, m_sel):
        def fn(kt, acc):
            pr = jnp.exp2(s_bufs[i % 2][tile_rows(kt), :] - m_sel).astype(BF16)
            return acc + jnp.dot(vs_ref[i, 0, kt + PAD_TILES], pr, preferred_element_type=F32)
        return fn

    def listed(i, fn):
        return lambda j, carry: fn(tile_idx[i, j], carry)

    def both(f1, f2):
        def fn(j, carry):
            return f1(j, carry[0]), f2(j, carry[1])
        return fn

    def run_tiles(fn, lo, n, init):
        def group(size, first):
            def body(j, carry):
                for u in range(size):
                    carry = fn(first + j * size + u, carry)
                return carry
            return body

        n_u = n // UNROLL
        carry = lax.fori_loop(0, n_u, group(UNROLL, lo), init)
        done = lo + n_u * UNROLL
        size = UNROLL // 2
        while size >= 1:
            take = (n // size) % 2
            carry = lax.fori_loop(0, take, group(size, done), carry)
            done = done + take * size
            size //= 2
        return carry

    def diag_tile(i, mrun):
        s = sel_scores(i, last) + tile4(tri_ref[qb % (SKT // QBLK)])
        s_bufs[i % 2][tile_rows(last), :] = s
        return jnp.max(jnp.maximum(mrun, col_reduce8(s, jnp.maximum)), axis=0, keepdims=True)

    mrun0 = jnp.full((8, nw), NEG, F32)
    acc0 = jnp.zeros((VROWS, nw), F32)
    m_prev = diag_tile(0, run_tiles(listed(0, pass1(0)), 0, n_act[0], mrun0))
    outs = []
    for i in range(1, n_items):
        p2 = pass2(i - 1, m_prev)
        f1, f2 = listed(i, pass1(i)), listed(i - 1, p2)
        n_both = jnp.minimum(n_act[i], n_act[i - 1])
        mrun, acc = run_tiles(both(f1, f2), 0, n_both, (mrun0, acc0))
        mrun = run_tiles(f1, n_both, n_act[i] - n_both, mrun)
        acc = run_tiles(f2, n_both, n_act[i - 1] - n_both, acc)
        m_cur = diag_tile(i, mrun)
        outs.append(normalise(p2(last, acc)))
        m_prev = m_cur
    p2 = pass2(n_items - 1, m_prev)
    acc = run_tiles(listed(n_items - 1, p2), 0, n_act[n_items - 1], acc0)
    outs.append(normalise(p2(last, acc)))

    for i in range(n_items):
        tot = [part_scr[i, :, h * QBLK:(h + 1) * QBLK] + gate_row(i, h, 1) * outs[i][:, h * QBLK:(h + 1) * QBLK]
               for h in range(N_HPG)]
        o = jnp.concatenate(tot, axis=0).T
        y_ref[i] = (o * _silu(z_ref[i])).astype(BF16)


def _edge_biases():
    kl = np.arange(SKT)[:, None]
    ql = np.arange(QBLK)[None, :]
    tri = np.stack([np.where(kl <= par * QBLK + ql, 0.0, NEG) for par in range(SKT // QBLK)])
    kk = np.arange(WKT)[:, None]
    wb = np.stack([np.where(kk > ql, 0.0, NEG), np.where(kk <= ql + WKT - QBLK, 0.0, NEG)])
    return jnp.asarray(tri, F32), jnp.asarray(wb, F32)


def _nsa(p, spieces, kcmp, vcmpt, ovt, ks, vs, kw, vw):
    bsz, t, _ = p.shape
    n_slc = t // SLC_LEN
    gw = N_HPG * N_HD
    tri, wb = _edge_biases()
    nw = N_HPG * QBLK
    per_g = lambda a: pl.BlockSpec((bsz, 1) + a.shape[2:], lambda g, i, sp: (0, g) + (0,) * (a.ndim - 2))
    const = lambda a: pl.BlockSpec(a.shape, lambda g, i, sp: (0,) * a.ndim)
    grid_spec = pltpu.PrefetchScalarGridSpec(
        num_scalar_prefetch=1,
        grid=(N_KV, t // QBLK),
        in_specs=[pl.BlockSpec((bsz, QBLK, gw), lambda g, i, sp: (0, i, COL_NQ // gw + g)),
                  pl.BlockSpec((bsz, QBLK, LANES), lambda g, i, sp: (0, i, COL_GATES // LANES)),
                  pl.BlockSpec((bsz, QBLK, gw), lambda g, i, sp: (0, i, COL_NZ // gw + g)),
                  per_g(kcmp), per_g(vcmpt), const(ovt),
                  per_g(ks), per_g(vs), per_g(kw), per_g(vw), const(tri), const(wb)],
        out_specs=pl.BlockSpec((bsz, QBLK, gw), lambda g, i, sp: (0, i, g)),
        scratch_shapes=[pltpu.VMEM((t, nw), F32),
                        pltpu.VMEM((t, nw), F32),
                        pltpu.VMEM((bsz, LANES, QBLK), F32),
                        pltpu.VMEM((n_slc, QBLK), F32),
                        pltpu.VMEM((n_slc, QBLK), F32),
                        pltpu.VMEM((bsz, N_HD, nw), F32),
                        pltpu.VMEM((bsz, n_slc, nw), F32),
                        pltpu.SMEM((bsz, t // SKT), jnp.int32)],
    )
    return pl.pallas_call(
        _nsa_batch_kernel,
        grid_spec=grid_spec,
        out_shape=jax.ShapeDtypeStruct((bsz, t, N_WIDTH), BF16),
        compiler_params=_cparams(("arbitrary", "arbitrary")),
        name="nsa_attention",
    )(spieces, p, p, p, kcmp, vcmpt, ovt, ks, vs, kw, vw, tri, wb)


def _outproj_kernel(ym_ref, yn_ref, w_ref, x_ref, gate_ref, fg_ref, o_ref, wb_scr, *, final):
    @pl.when((pl.program_id(0) == 0) & (pl.program_id(1) == 0))
    def _():
        rows = w_ref.shape[1]
        for c in range(rows // WP_TILE):
            cs = slice(c * WP_TILE, (c + 1) * WP_TILE)
            wb_scr[cs, :] = w_ref[0, cs, :].astype(BF16)

    y = jnp.dot(ym_ref[0], wb_scr[0:M_WIDTH, :], preferred_element_type=F32)
    y = y + jnp.dot(yn_ref[0], wb_scr[M_WIDTH:, :], preferred_element_type=F32)
    hres = x_ref[0] + gate_ref[0] * y
    if final:
        ms = jnp.mean(hres * hres, axis=-1, keepdims=True)
        hres = hres * lax.rsqrt(ms + EPS) * fg_ref[...]
    o_ref[0] = hres


def _outproj(ym, yn, w, layer, x, gate, fg, final):
    bsz, t, d = x.shape
    tm = 512
    return pl.pallas_call(
        functools.partial(_outproj_kernel, final=final),
        grid=(bsz, t // tm),
        in_specs=[pl.BlockSpec((1, tm, M_WIDTH), lambda bi, i: (bi, i, 0)),
                  pl.BlockSpec((1, tm, N_WIDTH), lambda bi, i: (bi, i, 0)),
                  pl.BlockSpec((1, M_WIDTH + N_WIDTH, d), lambda bi, i: (layer, 0, 0),
                               pipeline_mode=pl.Buffered(1)),
                  pl.BlockSpec((1, tm, d), lambda bi, i: (bi, i, 0)),
                  pl.BlockSpec((1, 1, d), lambda bi, i: (bi, 0, 0)),
                  pl.BlockSpec((1, d), lambda bi, i: (0, 0))],
        out_specs=pl.BlockSpec((1, tm, d), lambda bi, i: (bi, i, 0)),
        out_shape=jax.ShapeDtypeStruct((bsz, t, d), F32),
        scratch_shapes=[pltpu.VMEM((M_WIDTH + N_WIDTH, d), BF16)],
        compiler_params=_cparams(("arbitrary", "arbitrary")),
        name="outproj_residual",
    )(ym, yn, w, x, gate, fg)


SRC_MI = 4 * M_WIDTH
SRC_NQ = SRC_MI + 2 * M_HEADS
SRC_NG = SRC_NQ + N_WIDTH + 6 * KV_W
SRC_NZ = SRC_NG + 3 * N_HEADS


def _reorder_cols(a):
    parts = [a[..., 0:SRC_MI], a[..., SRC_NQ:SRC_NG], a[..., SRC_NZ:SRC_NZ + N_WIDTH], a[..., SRC_MI:SRC_NQ],
             a[..., SRC_NG:SRC_NZ]]
    used = sum(x.shape[-1] for x in parts)
    parts.append(jnp.zeros(a.shape[:-1] + (NP_PAD - used,), a.dtype))
    return jnp.concatenate(parts, axis=-1)


WP_TILE = 512
WP_GATE_TILE = COL_GATES // WP_TILE


def _wprep_kernel(w_ref, g1_ref, g2_ref, o_ref):
    j = pl.program_id(0)
    d = o_ref.shape[0]

    @pl.when(j < WP_GATE_TILE)
    def _():
        for c in range(d // WP_TILE):
            cs = slice(c * WP_TILE, (c + 1) * WP_TILE)
            o_ref[cs, :] = w_ref[0, :, cs].T.astype(BF16)

    @pl.when(j == WP_GATE_TILE)
    def _():
        n_gate = g1_ref.shape[1] + g2_ref.shape[1]
        gt = jnp.concatenate([g1_ref[0], g2_ref[0], jnp.zeros((LANES - n_gate, d), F32)], axis=0)
        for c in range(d // WP_TILE):
            cs = slice(c * WP_TILE, (c + 1) * WP_TILE)
            o_ref[cs, 0:LANES] = gt[:, cs].T.astype(BF16)
        o_ref[:, LANES:] = jnp.zeros((d, WP_TILE - LANES), BF16)


def _wprep(w_t, layer):
    _, n, d = w_t.shape
    n_big = COL_GATES // WP_TILE
    assert n == SRC_NZ + N_WIDTH and NP_PAD // WP_TILE == n_big + 1

    def src_row(j):
        return jnp.where(j < COL_NQ // WP_TILE, j * WP_TILE,
                         jnp.where(j < COL_NZ // WP_TILE, SRC_NQ + (j - COL_NQ // WP_TILE) * WP_TILE,
                                   jnp.where(j < n_big, SRC_NZ + (j - COL_NZ // WP_TILE) * WP_TILE, 0)))

    el = pl.Element
    return pl.pallas_call(
        _wprep_kernel,
        grid=(NP_PAD // WP_TILE,),
        in_specs=[pl.BlockSpec((el(1), el(WP_TILE), el(d)), lambda j: (layer, pl.multiple_of(src_row(j), 8), 0)),
                  pl.BlockSpec((el(1), el(SRC_NQ - SRC_MI), el(d)), lambda j: (layer, SRC_MI, 0)),
                  pl.BlockSpec((el(1), el(SRC_NZ - SRC_NG), el(d)), lambda j: (layer, SRC_NG, 0))],
        out_specs=pl.BlockSpec((d, WP_TILE), lambda j: (0, j)),
        out_shape=jax.ShapeDtypeStruct((d, NP_PAD), BF16),
        compiler_params=_cparams(("parallel",)),
        name="inproj_weight_prep",
    )(w_t, w_t, w_t)


def _overlap_t(t):
    n_cmp_rows = t // CMP_STRIDE
    n_slc = t // SLC_LEN
    c0 = np.arange(n_cmp_rows) * CMP_STRIDE
    s0 = np.arange(n_slc) * SLC_LEN
    ov = (c0[None, :] <= s0[:, None] + SLC_LEN - 1) & (c0[None, :] + CMP_LEN - 1 >= s0[:, None])
    ov[:, (t - CMP_LEN) // CMP_STRIDE + 1:] = False
    return jnp.asarray(np.concatenate([ov] * 3, axis=1), BF16)


def kernel(x, c, ln_g, w_ada, b_ada, w_in, b_in, m_conv_w, m_conv_b, m_wq, m_wk, m_norm_w, m_skip, m_f_bias,
           n_pos_k, n_pos_v, n_w1_k, n_w2_k, n_w1_v, n_w2_v, w_out, final_g):
    out_dtype = x.dtype
    bsz, t, d = x.shape
    depth = ln_g.shape[0]
    h_res = x.astype(F32)
    assert bsz <= 2 and t % 1024 == 0 and t // CMP_STRIDE <= 256
    c_t =jnp.zeros((d, 8), F32).at[:, :bsz].set(c.astype(F32).T)
    slopes_np = np.array([2.0 ** (-8.0 * (h + 1) / N_HEADS) for h in range(N_HEADS)], np.float32)
    rest = (slopes_np.astype(np.float64) * LOG2E).astype(np.float32)
    pieces = []
    for _ in range(3):
        pieces.append(rest.astype(jnp.bfloat16).astype(np.float32))
        rest = rest - pieces[-1]
    spieces = jnp.asarray(np.stack(pieces, axis=1).reshape(-1))
    ovt = _overlap_t(t)

    def w1cat(w1):
        w = w1.reshape(2, CMP_STRIDE, N_HD, CMP_HIDDEN)
        w = jnp.concatenate([w[0], w[1]], axis=-1).astype(BF16)
        z = jnp.zeros_like(w)
        return jnp.concatenate([jnp.concatenate([w, z], axis=-1), jnp.concatenate([z, w], axis=-1)], axis=1)

    def w2pad(w2):
        return jnp.pad(w2, ((0, 0), (0, LANES - N_HD))).astype(BF16)

    for l in range(depth):
        mod = _ada(c_t, w_ada, b_ada[l][None, :], l, bsz)[:bsz]
        shift, scale, gate = mod[:, None, 0:d], mod[:, None, d:2 * d], mod[:, None, 2 * d:3 * d]
        p = _inproj(h_res, ln_g[l][None, :], scale, shift,
                    _wprep(jnp.swapaxes(w_in, 1, 2), l), _reorder_cols(b_in[l])[None, :])
        fb_row = jnp.zeros((1, LANES), F32).at[0, M_HEADS:2 * M_HEADS].set(m_f_bias[l])
        y_m = _mlstm(p, m_conv_w[l], m_conv_b[l][None, :], m_wq[l].astype(BF16), m_wk[l].astype(BF16),
                     m_norm_w[l][None, :], m_skip[l][None, :], fb_row)
        kcmp, vcmpt = _compress(p, n_pos_k[l].reshape(1, -1), n_pos_v[l].reshape(1, -1),
                                n_w1_k[l].astype(BF16), w1cat(n_w1_k[l]), w2pad(n_w2_k[l]),
                                n_w1_v[l].astype(BF16), w1cat(n_w1_v[l]), w2pad(n_w2_v[l]))
        ks, vs, kw, vw = _relayout(p)
        y_n = _nsa(p, spieces, kcmp, vcmpt, ovt, ks, vs, kw, vw)
        h_res = _outproj(y_m, y_n, w_out, l, h_res, gate, final_g[None, :], l == depth - 1)
    return h_res.astype(out_dtype)
```

```python
import functools

import numpy as np
import jax
import jax.numpy as jnp
from jax import lax
from jax.experimental import pallas as pl
from jax.experimental.pallas import tpu as pltpu

F32 = jnp.float32
BF16 = jnp.bfloat16

EPS = 1e-6
M_HEADS = 4
M_HD = 256
M_WIDTH = M_HEADS * M_HD
CONV_K = 4
M_CHUNK = 256
N_HEADS = 16
N_HD = 64
N_KV = 4
N_HPG = N_HEADS // N_KV
N_WIDTH = N_HEADS * N_HD
KV_W = N_KV * N_HD
CMP_LEN = 32
CMP_STRIDE = 16
CMP_HIDDEN = 2 * N_HD
SLC_LEN = 64
SLC_TOPN = 16
WIN = 512
QBLK = 256
SKT = 256
WKT = 256
PAD_TILES = WIN // SKT
SEL_GRP = 8
KA_SEL = 128
KA_WIN = 128
VROWS = 80
UNROLL = 8
AUX_ROWS = 16
LOG2E = 1.4426950408889634

COL_MX, COL_MV, COL_MO, COL_MZ = 0, 1024, 2048, 3072
COL_NQ = 4096
COL_KC, COL_VC, COL_KS, COL_VS, COL_KW, COL_VW = 5120, 5376, 5632, 5888, 6144, 6400
COL_NZ = 6656
COL_GATES = 7680
GATE_NG = 2 * M_HEADS
NP_PAD = 8192
NORM_CHUNKS = 4
LANES = 128
NEG = -1e30
VMEM_LIMIT = 58 * 1024 * 1024


def _cparams(sem):
    return pltpu.CompilerParams(dimension_semantics=sem, vmem_limit_bytes=VMEM_LIMIT)


def _silu(x):
    return x * jax.nn.sigmoid(x)


def _log_sigmoid(x):
    return jnp.minimum(x, 0.0) - jnp.log1p(jnp.exp(-jnp.abs(x)))


def _ada_kernel(ct_ref, w_ref, b_ref, o_ref, *, bsz):
    s_t = _silu(ct_ref[...])
    w = w_ref[0]
    row = lax.broadcasted_iota(jnp.int32, o_ref.shape, 0)
    out = jnp.zeros(o_ref.shape, F32)
    for b in range(bsz):
        prod = w * s_t[:, b:b + 1]
        acc = prod[0:8]
        for r in range(1, prod.shape[0] // 8):
            acc = acc + prod[8 * r:8 * r + 8]
        out = jnp.where(row == b, jnp.sum(acc, axis=0, keepdims=True) + b_ref[...], out)
    o_ref[...] = out


def _ada(c_t, w, b, layer, bsz):
    _, d, n = w.shape
    tn = 1024
    return pl.pallas_call(
        functools.partial(_ada_kernel, bsz=bsz),
        grid=(n // tn,),
        in_specs=[pl.BlockSpec((d, 8), lambda j: (0, 0)),
                  pl.BlockSpec((1, d, tn), lambda j: (layer, 0, j)),
                  pl.BlockSpec((1, tn), lambda j: (0, j))],
        out_specs=pl.BlockSpec((8, tn), lambda j: (0, j)),
        out_shape=jax.ShapeDtypeStruct((8, n), F32),
        compiler_params=_cparams(("parallel",)),
        name="ada_mod",
    )(c_t, w, b)


def _inproj_kernel(x_ref, g_ref, sc_ref, sh_ref, w_ref, b_ref, o_ref, h_ref):
    first = pl.program_id(2) == 0

    @pl.when(first)
    def _():
        tm = x_ref.shape[1]
        ck = tm // NORM_CHUNKS
        for c in range(NORM_CHUNKS):
            rows = slice(c * ck, (c + 1) * ck)
            x = x_ref[0, rows, :]
            ms = jnp.mean(x * x, axis=-1, keepdims=True)
            h = x * lax.rsqrt(ms + EPS) * g_ref[...]
            h = (h * (1.0 + sc_ref[0]) + sh_ref[0]).astype(BF16)
            h_ref[rows, :] = h
            o_ref[0, rows, :] = jnp.dot(h, w_ref[...], preferred_element_type=F32) + b_ref[...]

    @pl.when(jnp.logical_not(first))
    def _():
        o_ref[0] = jnp.dot(h_ref[...], w_ref[...], preferred_element_type=F32) + b_ref[...]


def _inproj(x, g, scale, shift, w, b):
    bsz, t, d = x.shape
    n = w.shape[1]
    tm, tn = 1024, 2048
    return pl.pallas_call(
        _inproj_kernel,
        grid=(bsz, t // tm, n // tn),
        in_specs=[pl.BlockSpec((1, tm, d), lambda bi, i, j: (bi, i, 0)),
                  pl.BlockSpec((1, d), lambda bi, i, j: (0, 0)),
                  pl.BlockSpec((1, 1, d), lambda bi, i, j: (bi, 0, 0)),
                  pl.BlockSpec((1, 1, d), lambda bi, i, j: (bi, 0, 0)),
                  pl.BlockSpec((d, tn), lambda bi, i, j: (0, j)),
                  pl.BlockSpec((1, tn), lambda bi, i, j: (0, j))],
        out_specs=pl.BlockSpec((1, tm, tn), lambda bi, i, j: (bi, i, j)),
        out_shape=jax.ShapeDtypeStruct((bsz, t, n), F32),
        scratch_shapes=[pltpu.VMEM((tm, d), BF16)],
        compiler_params=_cparams(("parallel", "parallel", "arbitrary")),
        name="norm_inproj",
    )(x, g, scale, shift, w, b)


def _mlstm_kernel(x_ref, v_ref, o_ref, z_ref, gt_ref, cw_ref, cb_ref, wq_ref, wk_ref, nw_ref, sk_ref, fb_ref,
                  y_ref, c_scr, n_scr, m_scr, xp_scr):
    L = M_CHUNK

    @pl.when(pl.program_id(1) == 0)
    def _():
        c_scr[...] = jnp.zeros_like(c_scr)
        n_scr[...] = jnp.zeros_like(n_scr)
        m_scr[...] = jnp.zeros_like(m_scr)
        xp_scr[...] = jnp.zeros_like(xp_scr)

    x = x_ref[0]
    prev = xp_scr[...]
    row8 = lax.broadcasted_iota(jnp.int32, (8, M_WIDTH), 0)
    cw = cw_ref[...]
    xc = cb_ref[...] + x * cw[CONV_K - 1:CONV_K, :]
    for sft in range(1, CONV_K):
        xr = pltpu.roll(x, sft, 0)
        top = jnp.where(row8 < sft, pltpu.roll(prev, sft, 0), xr[0:8])
        xs = jnp.concatenate([top, xr[8:]], axis=0)
        xc = xc + xs * cw[CONV_K - 1 - sft:CONV_K - sft, :]
    xp_scr[...] = x[L - 8:L]
    xc = _silu(xc)

    gt = gt_ref[0]
    col = lax.broadcasted_iota(jnp.int32, (L, LANES), 1)
    logf = _log_sigmoid(gt + fb_ref[...])
    a_c = jnp.where((col >= M_HEADS) & (col < 2 * M_HEADS), logf, gt)
    ri = lax.broadcasted_iota(jnp.int32, (L, L), 0)
    ci = lax.broadcasted_iota(jnp.int32, (L, L), 1)
    causal = ri >= ci
    hp = lax.Precision.HIGHEST
    tri = causal.astype(F32)
    tri_t = (ri <= ci).astype(F32)
    b_c = jnp.dot(tri, a_c, precision=hp, preferred_element_type=F32)
    a_r = a_c.T
    b_r = jnp.dot(a_r[0:8], tri_t, precision=hp, preferred_element_type=F32)

    for h in range(M_HEADS):
        sl = slice(h * M_HD, (h + 1) * M_HD)
        xh = xc[:, sl]
        xb = xh.astype(BF16)
        q = jnp.dot(xb, wq_ref[h], preferred_element_type=F32)
        k = jnp.dot(xb, wk_ref[h], preferred_element_type=F32) * (M_HD ** -0.5)
        vb = v_ref[0, :, sl].astype(BF16)
        qb = q.astype(BF16)
        kb = k.astype(BF16)

        bt = b_c[:, M_HEADS + h:M_HEADS + h + 1]
        ic = a_c[:, h:h + 1]
        bs = b_r[M_HEADS + h:M_HEADS + h + 1, :]
        ir = a_r[h:h + 1, :]
        m_prev = m_scr[h][:, 0:1]

        dm = jnp.where(causal, bt - bs + ir, -jnp.inf)
        inter = bt + m_prev
        m_t = jnp.maximum(inter, jnp.max(dm, axis=-1, keepdims=True))
        w_in = jnp.exp(dm - m_t)
        w_st = jnp.exp(inter - m_t)
        s = lax.dot_general(qb, kb, (((1,), (1,)), ((), ())), preferred_element_type=F32) * w_in
        cmat = c_scr[h]
        nvec = n_scr[h]
        sb = s.astype(BF16)
        num = w_st * jnp.dot(qb, cmat.astype(BF16), preferred_element_type=F32) \
            + jnp.dot(sb, vb, preferred_element_type=F32)
        nt_dims = (((1,), (1,)), ((), ()))
        qn = lax.dot_general(qb, jnp.broadcast_to(nvec, (8, M_HD)).astype(BF16), nt_dims,
                             preferred_element_type=F32)[:, 0:1]
        ssum = lax.dot_general(sb, jnp.ones((8, L), BF16), nt_dims, preferred_element_type=F32)[:, 0:1]
        den = w_st * qn + ssum
        hh = num / jnp.maximum(jnp.abs(den), jnp.exp(-m_t))

        b_last = bt[L - 1:L, :]
        w_end = b_last - bt + ic
        m_new = jnp.maximum(b_last + m_prev, jnp.max(w_end, axis=0, keepdims=True))
        decay = jnp.exp(b_last + m_prev - m_new)
        kwt = k * jnp.exp(w_end - m_new)
        c_scr[h] = decay * cmat + lax.dot_general(kwt.astype(BF16), vb, (((0,), (0,)), ((), ())),
                                                  preferred_element_type=F32)
        n_scr[h] = decay * nvec + jnp.sum(kwt, axis=0, keepdims=True)
        m_scr[h] = jnp.broadcast_to(m_new, (1, LANES))

        mu = jnp.mean(hh, axis=-1, keepdims=True)
        hc = hh - mu
        var = jnp.mean(hc * hc, axis=-1, keepdims=True)
        hn = hc * lax.rsqrt(var + EPS) * nw_ref[:, sl]
        out = jax.nn.sigmoid(o_ref[0, :, sl]) * hn + sk_ref[:, sl] * xh
        y_ref[0, :, sl] = (out * _silu(z_ref[0, :, sl])).astype(BF16)


def _mlstm(p, conv_w, conv_b, wq, wk, norm_w, skip, fb_row):
    bsz, t, _ = p.shape
    L = M_CHUNK
    cb = lambda c: (lambda bi, i: (bi, i, c))
    full2 = lambda bi, i: (0, 0)
    full3 = lambda bi, i: (0, 0, 0)
    return pl.pallas_call(
        _mlstm_kernel,
        grid=(bsz, t // L),
        in_specs=[pl.BlockSpec((1, L, M_WIDTH), cb(COL_MX // M_WIDTH)),
                  pl.BlockSpec((1, L, M_WIDTH), cb(COL_MV // M_WIDTH)),
                  pl.BlockSpec((1, L, M_WIDTH), cb(COL_MO // M_WIDTH)),
                  pl.BlockSpec((1, L, M_WIDTH), cb(COL_MZ // M_WIDTH)),
                  pl.BlockSpec((1, L, LANES), cb(COL_GATES // LANES)),
                  pl.BlockSpec((CONV_K, M_WIDTH), full2),
                  pl.BlockSpec((1, M_WIDTH), full2),
                  pl.BlockSpec((M_HEADS, M_HD, M_HD), full3),
                  pl.BlockSpec((M_HEADS, M_HD, M_HD), full3),
                  pl.BlockSpec((1, M_WIDTH), full2),
                  pl.BlockSpec((1, M_WIDTH), full2),
                  pl.BlockSpec((1, LANES), full2)],
        out_specs=pl.BlockSpec((1, L, M_WIDTH), lambda bi, i: (bi, i, 0)),
        out_shape=jax.ShapeDtypeStruct((bsz, t, M_WIDTH), BF16),
        scratch_shapes=[pltpu.VMEM((M_HEADS, M_HD, M_HD), F32),
                        pltpu.VMEM((M_HEADS, 1, M_HD), F32),
                        pltpu.VMEM((M_HEADS, 1, LANES), F32),
                        pltpu.VMEM((8, M_WIDTH), F32)],
        compiler_params=_cparams(("parallel", "arbitrary")),
        name="mlstm_group",
    )(p, p, p, p, p, conv_w, conv_b, wq, wk, norm_w, skip, fb_row)


def _compress_kernel(kc0_ref, kc1_ref, vc0_ref, vc1_ref, posk_ref, posv_ref, w1k_ref, w1kc_ref, w2k_ref,
                     w1v_ref, w1vc_ref, w2v_ref, kcmp_ref, vcmpt_ref):
    nb = kcmp_ref.shape[2]

    def hidden(src_refs, pos_ref, w1_ref, w1c_ref):
        halves = [jnp.zeros((nb, 4 * CMP_HIDDEN), F32) for _ in src_refs]
        for l in range(CMP_STRIDE):
            for hf, r in enumerate(src_refs):
                xl = r[0, pl.ds(l, nb, stride=CMP_STRIDE), :]
                halves[hf] = halves[hf] + jnp.dot(xl.astype(BF16), w1c_ref[l], preferred_element_type=F32)
        acc = jnp.concatenate([hv[:, c * 2 * CMP_HIDDEN:(c + 1) * 2 * CMP_HIDDEN] for hv in halves for c in range(2)],
                              axis=0)
        first = acc[:, :CMP_HIDDEN]
        second = acc[:, CMP_HIDDEN:]
        posb = jnp.dot(jnp.broadcast_to(pos_ref[...], (8, CMP_LEN * N_HD)).astype(BF16), w1_ref[...],
                       preferred_element_type=F32)[0:1]
        hid = first + pltpu.roll(second, N_KV * nb - 1, 0) + posb
        return jax.nn.gelu(hid).astype(BF16)

    hk = hidden((kc0_ref, kc1_ref), posk_ref, w1k_ref, w1kc_ref)
    kc = jnp.dot(hk, w2k_ref[...], preferred_element_type=F32)
    hv = hidden((vc0_ref, vc1_ref), posv_ref, w1v_ref, w1vc_ref)
    vc = jnp.dot(hv, w2v_ref[...], preferred_element_type=F32)
    j = lax.broadcasted_iota(jnp.int32, (nb, 1), 0)
    lane = lax.broadcasted_iota(jnp.int32, (nb, N_HD), 1)
    n_real = (nb * CMP_STRIDE - CMP_LEN) // CMP_STRIDE + 1
    aux = jnp.where(lane < 3, (j * CMP_STRIDE).astype(F32),
                    jnp.where(lane < 6, (CMP_LEN - 1) * 0.5,
                              jnp.where((lane == 6) & (j >= n_real), 1.0, 0.0))).astype(BF16)
    ones_blk = jnp.where(lax.broadcasted_iota(jnp.int32, (VROWS - N_HD, nb), 0) == 0, 1.0, 0.0)
    for g in range(N_KV):
        kg = kc[g * nb:(g + 1) * nb, 0:N_HD]
        k_hi = kg.astype(BF16)
        k_lo = (kg - k_hi.astype(F32)).astype(BF16)
        kcmp_ref[0, g] = jnp.concatenate([k_hi, k_lo, k_hi, aux], axis=1)
        vt = vc[g * nb:(g + 1) * nb].T
        vcmpt_ref[0, g] = jnp.concatenate([vt[0:N_HD], ones_blk], axis=0).astype(BF16)


def _compress(p, posk, posv, w1k, w1kc, w2k, w1v, w1vc, w2v):
    bsz, t, _ = p.shape
    nb = t // CMP_STRIDE
    cb = lambda c: (lambda bi: (bi, 0, c))
    f2 = lambda bi: (0, 0)
    f3 = lambda bi: (0, 0, 0)
    wspecs = [pl.BlockSpec((CMP_LEN * N_HD, CMP_HIDDEN), f2),
              pl.BlockSpec((CMP_STRIDE, LANES, 4 * CMP_HIDDEN), f3),
              pl.BlockSpec((CMP_HIDDEN, LANES), f2)]
    return pl.pallas_call(
        _compress_kernel,
        grid=(bsz,),
        in_specs=[pl.BlockSpec((1, t, LANES), cb(COL_KC // LANES)),
                  pl.BlockSpec((1, t, LANES), cb(COL_KC // LANES + 1)),
                  pl.BlockSpec((1, t, LANES), cb(COL_VC // LANES)),
                  pl.BlockSpec((1, t, LANES), cb(COL_VC // LANES + 1)),
                  pl.BlockSpec((1, CMP_LEN * N_HD), f2),
                  pl.BlockSpec((1, CMP_LEN * N_HD), f2)] + wspecs + wspecs,
        out_specs=[pl.BlockSpec((1, N_KV, nb, 4 * N_HD), lambda bi: (bi, 0, 0, 0)),
                   pl.BlockSpec((1, N_KV, VROWS, nb), lambda bi: (bi, 0, 0, 0))],
        out_shape=[jax.ShapeDtypeStruct((bsz, N_KV, nb, 4 * N_HD), BF16),
                   jax.ShapeDtypeStruct((bsz, N_KV, VROWS, nb), BF16)],
        compiler_params=_cparams(("parallel",)),
        name="nsa_compress",
    )(p, p, p, p, posk, posv, w1k, w1kc, w2k, w1v, w1vc, w2v)


def _relayout_kernel(ks_ref, vs_ref, kw_ref, vw_ref, ksa_ref, vsa_ref, kwa_ref, vwa_ref):
    i = pl.program_id(1)
    is_pad = i == 0
    flag = jnp.where(is_pad, 1.0, 0.0)
    row = lax.broadcasted_iota(jnp.int32, (SKT, 1), 0)
    lane = lax.broadcasted_iota(jnp.int32, (SKT, N_HD), 1)
    al = lane - AUX_ROWS
    ones_blk = jnp.where(lax.broadcasted_iota(jnp.int32, (VROWS - N_HD, SKT), 0) == 0, 1.0, 0.0)
    for u in range(PAD_TILES):
        rows = slice(u * SKT, (u + 1) * SKT)
        base = ((i - 1) * PAD_TILES + u) * SKT
        pos = jnp.where(is_pad, 0, base + row)
        blk = pos // SLC_LEN
        p_hi = (blk * SLC_LEN).astype(F32)
        p_lo = (pos - blk * SLC_LEN).astype(F32)
        mid_w = jnp.where((al >= 0) & (al < 3), p_hi,
                          jnp.where((al >= 3) & (al < 6), p_lo, jnp.where(al == 6, flag, 0.0)))
        mid_s = jnp.where(lane == blk % SEL_GRP, 1.0, mid_w)
        ks = ks_ref[0, rows, :]
        kw = kw_ref[0, rows, :]
        for g in range(N_KV):
            sl = slice(g * N_HD, (g + 1) * N_HD)
            ksa_ref[0, g, rows, :] = jnp.concatenate([ks[:, sl], mid_s], axis=1).astype(BF16)
            kwa_ref[0, g, rows, :] = jnp.concatenate([kw[:, sl], mid_w], axis=1).astype(BF16)
        vst = vs_ref[0, rows, :].T
        vwt = vw_ref[0, rows, :].T
        for g in range(N_KV):
            sl = slice(g * N_HD, (g + 1) * N_HD)
            vsa_ref[0, g, u] = jnp.concatenate([vst[sl], ones_blk], axis=0).astype(BF16)
            vwg = jnp.concatenate([vwt[sl], ones_blk], axis=0).astype(BF16)
            for j in range(SKT // WKT):
                vwa_ref[0, g, u * (SKT // WKT) + j] = vwg[:, j * WKT:(j + 1) * WKT]


def _relayout(p):
    bsz, t, _ = p.shape
    nt = t // SKT + PAD_TILES
    wpt = SKT // WKT
    rt = PAD_TILES
    cb = lambda c: (lambda bi, i: (bi, jnp.maximum(i - 1, 0), c))
    return pl.pallas_call(
        _relayout_kernel,
        grid=(bsz, nt // rt),
        in_specs=[pl.BlockSpec((1, rt * SKT, KV_W), cb(COL_KS // KV_W)),
                  pl.BlockSpec((1, rt * SKT, KV_W), cb(COL_VS // KV_W)),
                  pl.BlockSpec((1, rt * SKT, KV_W), cb(COL_KW // KV_W)),
                  pl.BlockSpec((1, rt * SKT, KV_W), cb(COL_VW // KV_W))],
        out_specs=[pl.BlockSpec((1, N_KV, rt * SKT, KA_SEL), lambda bi, i: (bi, 0, i, 0)),
                   pl.BlockSpec((1, N_KV, rt, VROWS, SKT), lambda bi, i: (bi, 0, i, 0, 0)),
                   pl.BlockSpec((1, N_KV, rt * SKT, KA_WIN), lambda bi, i: (bi, 0, i, 0)),
                   pl.BlockSpec((1, N_KV, rt * wpt, VROWS, WKT), lambda bi, i: (bi, 0, i, 0, 0))],
        out_shape=[jax.ShapeDtypeStruct((bsz, N_KV, nt * SKT, KA_SEL), BF16),
                   jax.ShapeDtypeStruct((bsz, N_KV, nt, VROWS, SKT), BF16),
                   jax.ShapeDtypeStruct((bsz, N_KV, nt * SKT, KA_WIN), BF16),
                   jax.ShapeDtypeStruct((bsz, N_KV, nt * wpt, VROWS, WKT), BF16)],
        compiler_params=_cparams(("parallel", "parallel")),
        name="nsa_relayout",
    )(p, p, p, p)


def _nsa_batch_kernel(sp_ref, q_ref, gt_ref, z_ref, kcmp_ref, vcmpt_ref, ovt_ref, ks_ref, vs_ref, kw_ref,
                      vw_ref, tri_ref, wb_ref, y_ref, sa_scr, sb_scr, g_scr, imp_scr, rank_scr, part_scr, selb_scr,
                      tile_idx):
    g = pl.program_id(0)
    qb = pl.program_id(1)
    n_items = q_ref.shape[0]
    start = qb * QBLK
    nb = kcmp_ref.shape[2]
    n_slc = ovt_ref.shape[0]
    nw = N_HPG * QBLK
    n_top = min(SLC_TOPN, n_slc)
    t_row = start + lax.broadcasted_iota(jnp.int32, (1, QBLK), 1)
    s_bufs = (sa_scr, sb_scr)
    last = qb // (SKT // QBLK)

    def tile4(a):
        return jnp.concatenate([a] * N_HPG, axis=1)

    def col_reduce8(s, op):
        out = s[0:8]
        for r in range(1, s.shape[0] // 8):
            out = op(out, s[8 * r:8 * r + 8])
        return out

    def col_max(s):
        return jnp.max(col_reduce8(s, jnp.maximum), axis=0, keepdims=True)

    def normalise(acc):
        den = acc[N_HD:N_HD + 1]
        return acc[0:N_HD] / jnp.where(den > 0, den, 1.0)

    def gate_row(i, h, branch):
        return g_scr[i, pl.ds(GATE_NG + (g * N_HPG + h) * 3 + branch, 1), :]

    hl = lax.broadcasted_iota(jnp.int32, (AUX_ROWS, nw), 1) // QBLK
    ar = lax.broadcasted_iota(jnp.int32, (AUX_ROWS, nw), 0)

    def per_head(vals):
        out = jnp.full((AUX_ROWS, nw), vals[N_HPG - 1], F32)
        for h in range(N_HPG - 2, -1, -1):
            out = jnp.where(hl == h, vals[h], out)
        return out

    pieces = [per_head([sp_ref[(g * N_HPG + h) * 3 + j] for h in range(N_HPG)]) for j in range(3)]
    aux = jnp.where(ar == 6, NEG, 0.0)
    for j in range(3):
        aux = jnp.where((ar == j) | (ar == j + 3), pieces[j], aux)
    aux_b = aux.astype(BF16)
    q_tail = jnp.zeros((KA_SEL - N_HD - 2 * AUX_ROWS, nw), BF16)
    sidx = lax.broadcasted_iota(jnp.int32, (n_slc, QBLK), 0)
    cur = t_row // SLC_LEN
    valid = sidx <= cur
    sub8 = lax.broadcasted_iota(jnp.int32, (8, QBLK), 0)
    last_blk = (start + QBLK - 1) // SLC_LEN
    cmp_end = lax.broadcasted_iota(jnp.int32, (nb, 1), 0) * CMP_STRIDE + (CMP_LEN - 1)
    cmp_bias = tile4(jnp.where(cmp_end <= t_row, 0.0, NEG))
    wrows = WIN + QBLK

    def prepare(i):
        qt = (q_ref[i] * (N_HD ** -0.5)).T
        q4l = jnp.concatenate([qt[h * N_HD:(h + 1) * N_HD] for h in range(N_HPG)], axis=1) * LOG2E
        q4s = q4l.astype(BF16)
        q4lo = (q4l - q4s.astype(F32)).astype(BF16)
        g_scr[i] = jax.nn.sigmoid(gt_ref[i]).T

        q_cmp = jnp.concatenate([q4s, q4s, q4lo, aux_b, jnp.zeros((N_HD - AUX_ROWS, nw), BF16)], axis=0)
        s_c = jnp.dot(kcmp_ref[i, 0], q_cmp, preferred_element_type=F32) + cmp_bias
        e_c = jnp.exp2(s_c - col_max(s_c))
        z_c = jnp.sum(col_reduce8(e_c, jnp.add), axis=0, keepdims=True)
        inv_c = jnp.where(tile4(t_row >= CMP_LEN - 1) & (z_c > 0), 1.0 / z_c, 0.0)
        o_c = jnp.dot(vcmpt_ref[i, 0], e_c.astype(BF16), preferred_element_type=F32)[0:N_HD] * inv_c
        p_c = e_c * inv_c
        psum = p_c[:, 0:QBLK]
        for h in range(1, N_HPG):
            psum = psum + p_c[:, h * QBLK:(h + 1) * QBLK]
        parts, rest = [], psum
        for _ in range(3):
            parts.append(rest.astype(BF16))
            rest = rest - parts[-1].astype(F32)
        imp = jnp.dot(ovt_ref[...], jnp.concatenate(parts, axis=0), preferred_element_type=F32)
        imp = jnp.where(valid, imp, -jnp.inf)
        imp_scr[...] = jnp.where((sidx == 0) | (sidx == cur), jnp.inf, imp)

        q_win = jnp.concatenate([q4s, jnp.zeros((AUX_ROWS, nw), BF16), aux_b,
                                 jnp.zeros((KA_WIN - N_HD - 2 * AUX_ROWS, nw), BF16)], axis=0)
        s_w = jnp.dot(kw_ref[i, 0, pl.ds(pl.multiple_of(start, QBLK), wrows), :], q_win,
                      preferred_element_type=F32)
        s_w = jnp.concatenate([s_w[0:WKT] + tile4(wb_ref[0]), s_w[WKT:wrows - WKT],
                               s_w[wrows - WKT:] + tile4(wb_ref[1])], axis=0)
        p_w = jnp.exp2(s_w - col_max(s_w)).astype(BF16)
        v_w = jnp.concatenate([vw_ref[i, 0, qb * (QBLK // WKT) + j] for j in range(wrows // WKT)], axis=1)
        o_w = normalise(jnp.dot(v_w, p_w, preferred_element_type=F32))
        for h in range(N_HPG):
            hs = slice(h * QBLK, (h + 1) * QBLK)
            part_scr[i, :, hs] = gate_row(i, h, 0) * o_c[:, hs] + gate_row(i, h, 2) * o_w[:, hs]

        rank_scr[...] = jnp.zeros_like(rank_scr)
        for ri in range(n_slc // 8):
            @pl.when((last_blk >= n_top) & (8 * ri <= last_blk))
            def _():
                rows = imp_scr[8 * ri:8 * ri + 8]
                for r in range(n_slc // 8):
                    blk8 = imp_scr[8 * r:8 * r + 8]
                    acc = rank_scr[8 * r:8 * r + 8]
                    for ii in range(8):
                        row = rows[ii:ii + 1]
                        if ri < r:
                            before = row >= blk8
                        elif ri > r:
                            before = row > blk8
                        else:
                            before = (row > blk8) | ((row == blk8) & (sub8 > ii))
                        acc = acc + jnp.where(before, 1.0, 0.0)
                    rank_scr[8 * r:8 * r + 8] = acc
        chosen = (rank_scr[...] < n_top) & valid
        selb_scr[i] = tile4(jnp.where(chosen, 0.0, NEG))

        any_q = jnp.max(jnp.where(chosen, 1.0, 0.0), axis=1, keepdims=True) > 0
        s_io = lax.broadcasted_iota(jnp.int32, (n_slc, 1), 0)
        bits = jnp.where(any_q, jnp.left_shift(1, s_io % 32), 0)
        words = [jnp.sum(jnp.where(s_io // 32 == w, bits, 0)) for w in range((n_slc + 31) // 32)]
        bpt = SKT // SLC_LEN
        cnt = jnp.int32(0)
        for tile in range(n_slc // bpt):
            nib = lax.shift_right_logical(words[tile * bpt // 32], jnp.int32(tile * bpt % 32)) & (2 ** bpt - 1)
            tile_idx[i, cnt] = tile
            cnt = cnt + ((nib != 0) & (tile < last)).astype(jnp.int32)
        return q4s, cnt

    q4, n_act = zip(*[prepare(i) for i in range(n_items)])


    def sel_scores(i, kt):
        grp = pl.multiple_of((kt * SKT // SLC_LEN) // SEL_GRP * SEL_GRP, SEL_GRP)
        sel_rows = jnp.concatenate([selb_scr[i, pl.ds(grp, SEL_GRP), :], jnp.zeros((AUX_ROWS - SEL_GRP, nw), F32)],
                                   axis=0)
        q_sel = jnp.concatenate([q4[i], sel_rows.astype(BF16), aux_b, q_tail], axis=0)
        krow = pl.multiple_of((kt + PAD_TILES) * SKT, SKT)
        return jnp.dot(ks_ref[i, 0, pl.ds(krow, SKT), :], q_sel, preferred_element_type=F32)

    def tile_rows(kt):
        return pl.ds(pl.multiple_of(kt * SKT, SKT), SKT)

    def pass1(i):
        def fn(kt, mrun):
            s = sel_scores(i, kt)
            s_bufs[i % 2][tile_rows(kt), :] = s
            return jnp.maximum(mrun, col_reduce8(s, jnp.maximum))
        return fn

    def pass2(i, m_sel):
        def fn(kt, acc):
            pr = jnp.exp2(s_bufs[i % 2][tile_rows(kt), :] - m_sel).astype(BF16)
            return acc + jnp.dot(vs_ref[i, 0, kt + PAD_TILES], pr, preferred_element_type=F32)
        return fn

    def listed(i, fn):
        return lambda j, carry: fn(tile_idx[i, j], carry)

    def both(f1, f2):
        def fn(j, carry):
            return f1(j, carry[0]), f2(j, carry[1])
        return fn

    def run_tiles(fn, lo, n, init):
        def group(size, first):
            def body(j, carry):
                for u in range(size):
                    carry = fn(first + j * size + u, carry)
                return carry
            return body

        n_u = n // UNROLL
        carry = lax.fori_loop(0, n_u, group(UNROLL, lo), init)
        done = lo + n_u * UNROLL
        size = UNROLL // 2
        while size >= 1:
            take = (n // size) % 2
            carry = lax.fori_loop(0, take, group(size, done), carry)
            done = done + take * size
            size //= 2
        return carry

    def diag_tile(i, mrun):
        s = sel_scores(i, last) + tile4(tri_ref[qb % (SKT // QBLK)])
        s_bufs[i % 2][tile_rows(last), :] = s
        return jnp.max(jnp.maximum(mrun, col_reduce8(s, jnp.maximum)), axis=0, keepdims=True)

    mrun0 = jnp.full((8, nw), NEG, F32)
    acc0 = jnp.zeros((VROWS, nw), F32)
    m_prev = diag_tile(0, run_tiles(listed(0, pass1(0)), 0, n_act[0], mrun0))
    outs = []
    for i in range(1, n_items):
        p2 = pass2(i - 1, m_prev)
        f1, f2 = listed(i, pass1(i)), listed(i - 1, p2)
        n_both = jnp.minimum(n_act[i], n_act[i - 1])
        mrun, acc = run_tiles(both(f1, f2), 0, n_both, (mrun0, acc0))
        mrun = run_tiles(f1, n_both, n_act[i] - n_both, mrun)
        acc = run_tiles(f2, n_both, n_act[i - 1] - n_both, acc)
        m_cur = diag_tile(i, mrun)
        outs.append(normalise(p2(last, acc)))
        m_prev = m_cur
    p2 = pass2(n_items - 1, m_prev)
    acc = run_tiles(listed(n_items - 1, p2), 0, n_act[n_items - 1], acc0)
    outs.append(normalise(p2(last, acc)))

    for i in range(n_items):
        tot = [part_scr[i, :, h * QBLK:(h + 1) * QBLK] + gate_row(i, h, 1) * outs[i][:, h * QBLK:(h + 1) * QBLK]
               for h in range(N_HPG)]
        o = jnp.concatenate(tot, axis=0).T
        y_ref[i] = (o * _silu(z_ref[i])).astype(BF16)


def _edge_biases():
    kl = np.arange(SKT)[:, None]
    ql = np.arange(QBLK)[None, :]
    tri = np.stack([np.where(kl <= par * QBLK + ql, 0.0, NEG) for par in range(SKT // QBLK)])
    kk = np.arange(WKT)[:, None]
    wb = np.stack([np.where(kk > ql, 0.0, NEG), np.where(kk <= ql + WKT - QBLK, 0.0, NEG)])
    return jnp.asarray(tri, F32), jnp.asarray(wb, F32)


def _nsa(p, spieces, kcmp, vcmpt, ovt, ks, vs, kw, vw):
    bsz, t, _ = p.shape
    n_slc = t // SLC_LEN
    gw = N_HPG * N_HD
    tri, wb = _edge_biases()
    nw = N_HPG * QBLK
    per_g = lambda a: pl.BlockSpec((bsz, 1) + a.shape[2:], lambda g, i, sp: (0, g) + (0,) * (a.ndim - 2))
    const = lambda a: pl.BlockSpec(a.shape, lambda g, i, sp: (0,) * a.ndim)
    grid_spec = pltpu.PrefetchScalarGridSpec(
        num_scalar_prefetch=1,
        grid=(N_KV, t // QBLK),
        in_specs=[pl.BlockSpec((bsz, QBLK, gw), lambda g, i, sp: (0, i, COL_NQ // gw + g)),
                  pl.BlockSpec((bsz, QBLK, LANES), lambda g, i, sp: (0, i, COL_GATES // LANES)),
                  pl.BlockSpec((bsz, QBLK, gw), lambda g, i, sp: (0, i, COL_NZ // gw + g)),
                  per_g(kcmp), per_g(vcmpt), const(ovt),
                  per_g(ks), per_g(vs), per_g(kw), per_g(vw), const(tri), const(wb)],
        out_specs=pl.BlockSpec((bsz, QBLK, gw), lambda g, i, sp: (0, i, g)),
        scratch_shapes=[pltpu.VMEM((t, nw), F32),
                        pltpu.VMEM((t, nw), F32),
                        pltpu.VMEM((bsz, LANES, QBLK), F32),
                        pltpu.VMEM((n_slc, QBLK), F32),
                        pltpu.VMEM((n_slc, QBLK), F32),
                        pltpu.VMEM((bsz, N_HD, nw), F32),
                        pltpu.VMEM((bsz, n_slc, nw), F32),
                        pltpu.SMEM((bsz, t // SKT), jnp.int32)],
    )
    return pl.pallas_call(
        _nsa_batch_kernel,
        grid_spec=grid_spec,
        out_shape=jax.ShapeDtypeStruct((bsz, t, N_WIDTH), BF16),
        compiler_params=_cparams(("arbitrary", "arbitrary")),
        name="nsa_attention",
    )(spieces, p, p, p, kcmp, vcmpt, ovt, ks, vs, kw, vw, tri, wb)


def _outproj_kernel(ym_ref, yn_ref, w_ref, x_ref, gate_ref, fg_ref, o_ref, wb_scr, *, final):
    @pl.when((pl.program_id(0) == 0) & (pl.program_id(1) == 0))
    def _():
        rows = w_ref.shape[1]
        for c in range(rows // WP_TILE):
            cs = slice(c * WP_TILE, (c + 1) * WP_TILE)
            wb_scr[cs, :] = w_ref[0, cs, :].astype(BF16)

    y = jnp.dot(ym_ref[0], wb_scr[0:M_WIDTH, :], preferred_element_type=F32)
    y = y + jnp.dot(yn_ref[0], wb_scr[M_WIDTH:, :], preferred_element_type=F32)
    hres = x_ref[0] + gate_ref[0] * y
    if final:
        ms = jnp.mean(hres * hres, axis=-1, keepdims=True)
        hres = hres * lax.rsqrt(ms + EPS) * fg_ref[...]
    o_ref[0] = hres


def _outproj(ym, yn, w, layer, x, gate, fg, final):
    bsz, t, d = x.shape
    tm = 512
    return pl.pallas_call(
        functools.partial(_outproj_kernel, final=final),
        grid=(bsz, t // tm),
        in_specs=[pl.BlockSpec((1, tm, M_WIDTH), lambda bi, i: (bi, i, 0)),
                  pl.BlockSpec((1, tm, N_WIDTH), lambda bi, i: (bi, i, 0)),
                  pl.BlockSpec((1, M_WIDTH + N_WIDTH, d), lambda bi, i: (layer, 0, 0),
                               pipeline_mode=pl.Buffered(1)),
                  pl.BlockSpec((1, tm, d), lambda bi, i: (bi, i, 0)),
                  pl.BlockSpec((1, 1, d), lambda bi, i: (bi, 0, 0)),
                  pl.BlockSpec((1, d), lambda bi, i: (0, 0))],
        out_specs=pl.BlockSpec((1, tm, d), lambda bi, i: (bi, i, 0)),
        out_shape=jax.ShapeDtypeStruct((bsz, t, d), F32),
        scratch_shapes=[pltpu.VMEM((M_WIDTH + N_WIDTH, d), BF16)],
        compiler_params=_cparams(("arbitrary", "arbitrary")),
        name="outproj_residual",
    )(ym, yn, w, x, gate, fg)


SRC_MI = 4 * M_WIDTH
SRC_NQ = SRC_MI + 2 * M_HEADS
SRC_NG = SRC_NQ + N_WIDTH + 6 * KV_W
SRC_NZ = SRC_NG + 3 * N_HEADS


def _reorder_cols(a):
    parts = [a[..., 0:SRC_MI], a[..., SRC_NQ:SRC_NG], a[..., SRC_NZ:SRC_NZ + N_WIDTH], a[..., SRC_MI:SRC_NQ],
             a[..., SRC_NG:SRC_NZ]]
    used = sum(x.shape[-1] for x in parts)
    parts.append(jnp.zeros(a.shape[:-1] + (NP_PAD - used,), a.dtype))
    return jnp.concatenate(parts, axis=-1)


WP_TILE = 512
WP_GATE_TILE = COL_GATES // WP_TILE


def _wprep_kernel(w_ref, g1_ref, g2_ref, o_ref):
    j = pl.program_id(0)
    d = o_ref.shape[0]

    @pl.when(j < WP_GATE_TILE)
    def _():
        for c in range(d // WP_TILE):
            cs = slice(c * WP_TILE, (c + 1) * WP_TILE)
            o_ref[cs, :] = w_ref[0, :, cs].T.astype(BF16)

    @pl.when(j == WP_GATE_TILE)
    def _():
        n_gate = g1_ref.shape[1] + g2_ref.shape[1]
        gt = jnp.concatenate([g1_ref[0], g2_ref[0], jnp.zeros((LANES - n_gate, d), F32)], axis=0)
        for c in range(d // WP_TILE):
            cs = slice(c * WP_TILE, (c + 1) * WP_TILE)
            o_ref[cs, 0:LANES] = gt[:, cs].T.astype(BF16)
        o_ref[:, LANES:] = jnp.zeros((d, WP_TILE - LANES), BF16)


def _wprep(w_t, layer):
    _, n, d = w_t.shape
    n_big = COL_GATES // WP_TILE
    assert n == SRC_NZ + N_WIDTH and NP_PAD // WP_TILE == n_big + 1

    def src_row(j):
        return jnp.where(j < COL_NQ // WP_TILE, j * WP_TILE,
                         jnp.where(j < COL_NZ // WP_TILE, SRC_NQ + (j - COL_NQ // WP_TILE) * WP_TILE,
                                   jnp.where(j < n_big, SRC_NZ + (j - COL_NZ // WP_TILE) * WP_TILE, 0)))

    el = pl.Element
    return pl.pallas_call(
        _wprep_kernel,
        grid=(NP_PAD // WP_TILE,),
        in_specs=[pl.BlockSpec((el(1), el(WP_TILE), el(d)), lambda j: (layer, pl.multiple_of(src_row(j), 8), 0)),
                  pl.BlockSpec((el(1), el(SRC_NQ - SRC_MI), el(d)), lambda j: (layer, SRC_MI, 0)),
                  pl.BlockSpec((el(1), el(SRC_NZ - SRC_NG), el(d)), lambda j: (layer, SRC_NG, 0))],
        out_specs=pl.BlockSpec((d, WP_TILE), lambda j: (0, j)),
        out_shape=jax.ShapeDtypeStruct((d, NP_PAD), BF16),
        compiler_params=_cparams(("parallel",)),
        name="inproj_weight_prep",
    )(w_t, w_t, w_t)


def _overlap_t(t):
    n_cmp_rows = t // CMP_STRIDE
    n_slc = t // SLC_LEN
    c0 = np.arange(n_cmp_rows) * CMP_STRIDE
    s0 = np.arange(n_slc) * SLC_LEN
    ov = (c0[None, :] <= s0[:, None] + SLC_LEN - 1) & (c0[None, :] + CMP_LEN - 1 >= s0[:, None])
    ov[:, (t - CMP_LEN) // CMP_STRIDE + 1:] = False
    return jnp.asarray(np.concatenate([ov] * 3, axis=1), BF16)


def kernel(x, c, ln_g, w_ada, b_ada, w_in, b_in, m_conv_w, m_conv_b, m_wq, m_wk, m_norm_w, m_skip, m_f_bias,
           n_pos_k, n_pos_v, n_w1_k, n_w2_k, n_w1_v, n_w2_v, w_out, final_g):
    out_dtype = x.dtype
    bsz, t, d = x.shape
    depth = ln_g.shape[0]
    h_res = x.astype(F32)
    assert bsz <= 2 and t % 1024 == 0 and t // CMP_STRIDE <= 256
    c_t =jnp.zeros((d, 8), F32).at[:, :bsz].set(c.astype(F32).T)
    slopes_np = np.array([2.0 ** (-8.0 * (h + 1) / N_HEADS) for h in range(N_HEADS)], np.float32)
    rest = (slopes_np.astype(np.float64) * LOG2E).astype(np.float32)
    pieces = []
    for _ in range(3):
        pieces.append(rest.astype(jnp.bfloat16).astype(np.float32))
        rest = rest - pieces[-1]
    spieces = jnp.asarray(np.stack(pieces, axis=1).reshape(-1))
    ovt = _overlap_t(t)

    def w1cat(w1):
        w = w1.reshape(2, CMP_STRIDE, N_HD, CMP_HIDDEN)
        w = jnp.concatenate([w[0], w[1]], axis=-1).astype(BF16)
        z = jnp.zeros_like(w)
        return jnp.concatenate([jnp.concatenate([w, z], axis=-1), jnp.concatenate([z, w], axis=-1)], axis=1)

    def w2pad(w2):
        return jnp.pad(w2, ((0, 0), (0, LANES - N_HD))).astype(BF16)

    for l in range(depth):
        mod = _ada(c_t, w_ada, b_ada[l][None, :], l, bsz)[:bsz]
        shift, scale, gate = mod[:, None, 0:d], mod[:, None, d:2 * d], mod[:, None, 2 * d:3 * d]
        p = _inproj(h_res, ln_g[l][None, :], scale, shift,
                    _wprep(jnp.swapaxes(w_in, 1, 2), l), _reorder_cols(b_in[l])[None, :])
        fb_row = jnp.zeros((1, LANES), F32).at[0, M_HEADS:2 * M_HEADS].set(m_f_bias[l])
        y_m = _mlstm(p, m_conv_w[l], m_conv_b[l][None, :], m_wq[l].astype(BF16), m_wk[l].astype(BF16),
                     m_norm_w[l][None, :], m_skip[l][None, :], fb_row)
        kcmp, vcmpt = _compress(p, n_pos_k[l].reshape(1, -1), n_pos_v[l].reshape(1, -1),
                                n_w1_k[l].astype(BF16), w1cat(n_w1_k[l]), w2pad(n_w2_k[l]),
                                n_w1_v[l].astype(BF16), w1cat(n_w1_v[l]), w2pad(n_w2_v[l]))
        ks, vs, kw, vw = _relayout(p)
        y_n = _nsa(p, spieces, kcmp, vcmpt, ovt, ks, vs, kw, vw)
        h_res = _outproj(y_m, y_n, w_out, l, h_res, gate, final_g[None, :], l == depth - 1)
    return h_res.astype(out_dtype)
```

```python
import functools

import numpy as np
import jax
import jax.numpy as jnp
from jax import lax
from jax.experimental import pallas as pl
from jax.experimental.pallas import tpu as pltpu

F32 = jnp.float32
BF16 = jnp.bfloat16

EPS = 1e-6
M_HEADS = 4
M_HD = 256
M_WIDTH = M_HEADS * M_HD
CONV_K = 4
M_CHUNK = 256
N_HEADS = 16
N_HD = 64
N_KV = 4
N_HPG = N_HEADS // N_KV
N_WIDTH = N_HEADS * N_HD
KV_W = N_KV * N_HD
CMP_LEN = 32
CMP_STRIDE = 16
CMP_HIDDEN = 2 * N_HD
SLC_LEN = 64
SLC_TOPN = 16
WIN = 512
QBLK = 256
SKT = 256
WKT = 256
PAD_TILES = WIN // SKT
SEL_GRP = 8
KA_SEL = 128
KA_WIN = 128
VROWS = 80
UNROLL = 4
AUX_ROWS = 16
LOG2E = 1.4426950408889634

COL_MX, COL_MV, COL_MO, COL_MZ = 0, 1024, 2048, 3072
COL_NQ = 4096
COL_KC, COL_VC, COL_KS, COL_VS, COL_KW, COL_VW = 5120, 5376, 5632, 5888, 6144, 6400
COL_NZ = 6656
COL_GATES = 7680
GATE_NG = 2 * M_HEADS
NP_PAD = 8192
NORM_CHUNKS = 4
LANES = 128
NEG = -1e30
VMEM_LIMIT = 58 * 1024 * 1024


def _cparams(sem):
    return pltpu.CompilerParams(dimension_semantics=sem, vmem_limit_bytes=VMEM_LIMIT)


def _silu(x):
    return x * jax.nn.sigmoid(x)


def _log_sigmoid(x):
    return jnp.minimum(x, 0.0) - jnp.log1p(jnp.exp(-jnp.abs(x)))


def _ada_kernel(ct_ref, w_ref, b_ref, o_ref, *, bsz):
    s_t = _silu(ct_ref[...])
    w = w_ref[0]
    row = lax.broadcasted_iota(jnp.int32, o_ref.shape, 0)
    out = jnp.zeros(o_ref.shape, F32)
    for b in range(bsz):
        prod = w * s_t[:, b:b + 1]
        acc = prod[0:8]
        for r in range(1, prod.shape[0] // 8):
            acc = acc + prod[8 * r:8 * r + 8]
        out = jnp.where(row == b, jnp.sum(acc, axis=0, keepdims=True) + b_ref[...], out)
    o_ref[...] = out


def _ada(c_t, w, b, layer, bsz):
    _, d, n = w.shape
    tn = 1024
    return pl.pallas_call(
        functools.partial(_ada_kernel, bsz=bsz),
        grid=(n // tn,),
        in_specs=[pl.BlockSpec((d, 8), lambda j: (0, 0)),
                  pl.BlockSpec((1, d, tn), lambda j: (layer, 0, j)),
                  pl.BlockSpec((1, tn), lambda j: (0, j))],
        out_specs=pl.BlockSpec((8, tn), lambda j: (0, j)),
        out_shape=jax.ShapeDtypeStruct((8, n), F32),
        compiler_params=_cparams(("parallel",)),
        name="ada_mod",
    )(c_t, w, b)


def _inproj_kernel(x_ref, g_ref, sc_ref, sh_ref, w_ref, b_ref, o_ref, h_ref):
    first = pl.program_id(2) == 0

    @pl.when(first)
    def _():
        tm = x_ref.shape[1]
        ck = tm // NORM_CHUNKS
        for c in range(NORM_CHUNKS):
            rows = slice(c * ck, (c + 1) * ck)
            x = x_ref[0, rows, :]
            ms = jnp.mean(x * x, axis=-1, keepdims=True)
            h = x * lax.rsqrt(ms + EPS) * g_ref[...]
            h = (h * (1.0 + sc_ref[0]) + sh_ref[0]).astype(BF16)
            h_ref[rows, :] = h
            o_ref[0, rows, :] = jnp.dot(h, w_ref[...], preferred_element_type=F32) + b_ref[...]

    @pl.when(jnp.logical_not(first))
    def _():
        o_ref[0] = jnp.dot(h_ref[...], w_ref[...], preferred_element_type=F32) + b_ref[...]


def _inproj(x, g, scale, shift, w, b):
    bsz, t, d = x.shape
    n = w.shape[1]
    tm, tn = 1024, 2048
    return pl.pallas_call(
        _inproj_kernel,
        grid=(bsz, t // tm, n // tn),
        in_specs=[pl.BlockSpec((1, tm, d), lambda bi, i, j: (bi, i, 0)),
                  pl.BlockSpec((1, d), lambda bi, i, j: (0, 0)),
                  pl.BlockSpec((1, 1, d), lambda bi, i, j: (bi, 0, 0)),
                  pl.BlockSpec((1, 1, d), lambda bi, i, j: (bi, 0, 0)),
                  pl.BlockSpec((d, tn), lambda bi, i, j: (0, j)),
                  pl.BlockSpec((1, tn), lambda bi, i, j: (0, j))],
        out_specs=pl.BlockSpec((1, tm, tn), lambda bi, i, j: (bi, i, j)),
        out_shape=jax.ShapeDtypeStruct((bsz, t, n), F32),
        scratch_shapes=[pltpu.VMEM((tm, d), BF16)],
        compiler_params=_cparams(("parallel", "parallel", "arbitrary")),
        name="norm_inproj",
    )(x, g, scale, shift, w, b)


def _mlstm_kernel(x_ref, v_ref, o_ref, z_ref, gt_ref, cw_ref, cb_ref, wq_ref, wk_ref, nw_ref, sk_ref, fb_ref,
                  y_ref, c_scr, n_scr, m_scr, xp_scr):
    L = M_CHUNK

    @pl.when(pl.program_id(1) == 0)
    def _():
        c_scr[...] = jnp.zeros_like(c_scr)
        n_scr[...] = jnp.zeros_like(n_scr)
        m_scr[...] = jnp.zeros_like(m_scr)
        xp_scr[...] = jnp.zeros_like(xp_scr)

    x = x_ref[0]
    prev = xp_scr[...]
    row8 = lax.broadcasted_iota(jnp.int32, (8, M_WIDTH), 0)
    cw = cw_ref[...]
    xc = cb_ref[...] + x * cw[CONV_K - 1:CONV_K, :]
    for sft in range(1, CONV_K):
        xr = pltpu.roll(x, sft, 0)
        top = jnp.where(row8 < sft, pltpu.roll(prev, sft, 0), xr[0:8])
        xs = jnp.concatenate([top, xr[8:]], axis=0)
        xc = xc + xs * cw[CONV_K - 1 - sft:CONV_K - sft, :]
    xp_scr[...] = x[L - 8:L]
    xc = _silu(xc)

    gt = gt_ref[0]
    col = lax.broadcasted_iota(jnp.int32, (L, LANES), 1)
    logf = _log_sigmoid(gt + fb_ref[...])
    a_c = jnp.where((col >= M_HEADS) & (col < 2 * M_HEADS), logf, gt)
    ri = lax.broadcasted_iota(jnp.int32, (L, L), 0)
    ci = lax.broadcasted_iota(jnp.int32, (L, L), 1)
    causal = ri >= ci
    hp = lax.Precision.HIGHEST
    tri = causal.astype(F32)
    tri_t = (ri <= ci).astype(F32)
    b_c = jnp.dot(tri, a_c, precision=hp, preferred_element_type=F32)
    a_r = a_c.T
    b_r = jnp.dot(a_r[0:8], tri_t, precision=hp, preferred_element_type=F32)

    for h in range(M_HEADS):
        sl = slice(h * M_HD, (h + 1) * M_HD)
        xh = xc[:, sl]
        xb = xh.astype(BF16)
        q = jnp.dot(xb, wq_ref[h], preferred_element_type=F32)
        k = jnp.dot(xb, wk_ref[h], preferred_element_type=F32) * (M_HD ** -0.5)
        vb = v_ref[0, :, sl].astype(BF16)
        qb = q.astype(BF16)
        kb = k.astype(BF16)

        bt = b_c[:, M_HEADS + h:M_HEADS + h + 1]
        ic = a_c[:, h:h + 1]
        bs = b_r[M_HEADS + h:M_HEADS + h + 1, :]
        ir = a_r[h:h + 1, :]
        m_prev = m_scr[h][:, 0:1]

        dm = jnp.where(causal, bt - bs + ir, -jnp.inf)
        inter = bt + m_prev
        m_t = jnp.maximum(inter, jnp.max(dm, axis=-1, keepdims=True))
        w_in = jnp.exp(dm - m_t)
        w_st = jnp.exp(inter - m_t)
        s = lax.dot_general(qb, kb, (((1,), (1,)), ((), ())), preferred_element_type=F32) * w_in
        cmat = c_scr[h]
        nvec = n_scr[h]
        sb = s.astype(BF16)
        num = w_st * jnp.dot(qb, cmat.astype(BF16), preferred_element_type=F32) \
            + jnp.dot(sb, vb, preferred_element_type=F32)
        nt_dims = (((1,), (1,)), ((), ()))
        qn = lax.dot_general(qb, jnp.broadcast_to(nvec, (8, M_HD)).astype(BF16), nt_dims,
                             preferred_element_type=F32)[:, 0:1]
        ssum = lax.dot_general(sb, jnp.ones((8, L), BF16), nt_dims, preferred_element_type=F32)[:, 0:1]
        den = w_st * qn + ssum
        hh = num / jnp.maximum(jnp.abs(den), jnp.exp(-m_t))

        b_last = bt[L - 1:L, :]
        w_end = b_last - bt + ic
        m_new = jnp.maximum(b_last + m_prev, jnp.max(w_end, axis=0, keepdims=True))
        decay = jnp.exp(b_last + m_prev - m_new)
        kwt = k * jnp.exp(w_end - m_new)
        c_scr[h] = decay * cmat + lax.dot_general(kwt.astype(BF16), vb, (((0,), (0,)), ((), ())),
                                                  preferred_element_type=F32)
        n_scr[h] = decay * nvec + jnp.sum(kwt, axis=0, keepdims=True)
        m_scr[h] = jnp.broadcast_to(m_new, (1, LANES))

        mu = jnp.mean(hh, axis=-1, keepdims=True)
        hc = hh - mu
        var = jnp.mean(hc * hc, axis=-1, keepdims=True)
        hn = hc * lax.rsqrt(var + EPS) * nw_ref[:, sl]
        out = jax.nn.sigmoid(o_ref[0, :, sl]) * hn + sk_ref[:, sl] * xh
        y_ref[0, :, sl] = (out * _silu(z_ref[0, :, sl])).astype(BF16)


def _mlstm(p, conv_w, conv_b, wq, wk, norm_w, skip, fb_row):
    bsz, t, _ = p.shape
    L = M_CHUNK
    cb = lambda c: (lambda bi, i: (bi, i, c))
    full2 = lambda bi, i: (0, 0)
    full3 = lambda bi, i: (0, 0, 0)
    return pl.pallas_call(
        _mlstm_kernel,
        grid=(bsz, t // L),
        in_specs=[pl.BlockSpec((1, L, M_WIDTH), cb(COL_MX // M_WIDTH)),
                  pl.BlockSpec((1, L, M_WIDTH), cb(COL_MV // M_WIDTH)),
                  pl.BlockSpec((1, L, M_WIDTH), cb(COL_MO // M_WIDTH)),
                  pl.BlockSpec((1, L, M_WIDTH), cb(COL_MZ // M_WIDTH)),
                  pl.BlockSpec((1, L, LANES), cb(COL_GATES // LANES)),
                  pl.BlockSpec((CONV_K, M_WIDTH), full2),
                  pl.BlockSpec((1, M_WIDTH), full2),
                  pl.BlockSpec((M_HEADS, M_HD, M_HD), full3),
                  pl.BlockSpec((M_HEADS, M_HD, M_HD), full3),
                  pl.BlockSpec((1, M_WIDTH), full2),
                  pl.BlockSpec((1, M_WIDTH), full2),
                  pl.BlockSpec((1, LANES), full2)],
        out_specs=pl.BlockSpec((1, L, M_WIDTH), lambda bi, i: (bi, i, 0)),
        out_shape=jax.ShapeDtypeStruct((bsz, t, M_WIDTH), BF16),
        scratch_shapes=[pltpu.VMEM((M_HEADS, M_HD, M_HD), F32),
                        pltpu.VMEM((M_HEADS, 1, M_HD), F32),
                        pltpu.VMEM((M_HEADS, 1, LANES), F32),
                        pltpu.VMEM((8, M_WIDTH), F32)],
        compiler_params=_cparams(("parallel", "arbitrary")),
        name="mlstm_group",
    )(p, p, p, p, p, conv_w, conv_b, wq, wk, norm_w, skip, fb_row)


def _compress_kernel(kc0_ref, kc1_ref, vc0_ref, vc1_ref, posk_ref, posv_ref, w1k_ref, w1kc_ref, w2k_ref,
                     w1v_ref, w1vc_ref, w2v_ref, kcmp_ref, vcmpt_ref):
    nb = kcmp_ref.shape[2]

    def hidden(src_refs, pos_ref, w1_ref, w1c_ref):
        halves = [jnp.zeros((nb, 4 * CMP_HIDDEN), F32) for _ in src_refs]
        for l in range(CMP_STRIDE):
            for hf, r in enumerate(src_refs):
                xl = r[0, pl.ds(l, nb, stride=CMP_STRIDE), :]
                halves[hf] = halves[hf] + jnp.dot(xl.astype(BF16), w1c_ref[l], preferred_element_type=F32)
        acc = jnp.concatenate([hv[:, c * 2 * CMP_HIDDEN:(c + 1) * 2 * CMP_HIDDEN] for hv in halves for c in range(2)],
                              axis=0)
        first = acc[:, :CMP_HIDDEN]
        second = acc[:, CMP_HIDDEN:]
        posb = jnp.dot(jnp.broadcast_to(pos_ref[...], (8, CMP_LEN * N_HD)).astype(BF16), w1_ref[...],
                       preferred_element_type=F32)[0:1]
        hid = first + pltpu.roll(second, N_KV * nb - 1, 0) + posb
        return jax.nn.gelu(hid).astype(BF16)

    hk = hidden((kc0_ref, kc1_ref), posk_ref, w1k_ref, w1kc_ref)
    kc = jnp.dot(hk, w2k_ref[...], preferred_element_type=F32)
    hv = hidden((vc0_ref, vc1_ref), posv_ref, w1v_ref, w1vc_ref)
    vc = jnp.dot(hv, w2v_ref[...], preferred_element_type=F32)
    j = lax.broadcasted_iota(jnp.int32, (nb, 1), 0)
    lane = lax.broadcasted_iota(jnp.int32, (nb, N_HD), 1)
    n_real = (nb * CMP_STRIDE - CMP_LEN) // CMP_STRIDE + 1
    aux = jnp.where(lane < 3, (j * CMP_STRIDE).astype(F32),
                    jnp.where(lane < 6, (CMP_LEN - 1) * 0.5,
                              jnp.where((lane == 6) & (j >= n_real), 1.0, 0.0))).astype(BF16)
    ones_blk = jnp.where(lax.broadcasted_iota(jnp.int32, (VROWS - N_HD, nb), 0) == 0, 1.0, 0.0)
    for g in range(N_KV):
        kg = kc[g * nb:(g + 1) * nb, 0:N_HD]
        k_hi = kg.astype(BF16)
        k_lo = (kg - k_hi.astype(F32)).astype(BF16)
        kcmp_ref[0, g] = jnp.concatenate([k_hi, k_lo, k_hi, aux], axis=1)
        vt = vc[g * nb:(g + 1) * nb].T
        vcmpt_ref[0, g] = jnp.concatenate([vt[0:N_HD], ones_blk], axis=0).astype(BF16)


def _compress(p, posk, posv, w1k, w1kc, w2k, w1v, w1vc, w2v):
    bsz, t, _ = p.shape
    nb = t // CMP_STRIDE
    cb = lambda c: (lambda bi: (bi, 0, c))
    f2 = lambda bi: (0, 0)
    f3 = lambda bi: (0, 0, 0)
    wspecs = [pl.BlockSpec((CMP_LEN * N_HD, CMP_HIDDEN), f2),
              pl.BlockSpec((CMP_STRIDE, LANES, 4 * CMP_HIDDEN), f3),
              pl.BlockSpec((CMP_HIDDEN, LANES), f2)]
    return pl.pallas_call(
        _compress_kernel,
        grid=(bsz,),
        in_specs=[pl.BlockSpec((1, t, LANES), cb(COL_KC // LANES)),
                  pl.BlockSpec((1, t, LANES), cb(COL_KC // LANES + 1)),
                  pl.BlockSpec((1, t, LANES), cb(COL_VC // LANES)),
                  pl.BlockSpec((1, t, LANES), cb(COL_VC // LANES + 1)),
                  pl.BlockSpec((1, CMP_LEN * N_HD), f2),
                  pl.BlockSpec((1, CMP_LEN * N_HD), f2)] + wspecs + wspecs,
        out_specs=[pl.BlockSpec((1, N_KV, nb, 4 * N_HD), lambda bi: (bi, 0, 0, 0)),
                   pl.BlockSpec((1, N_KV, VROWS, nb), lambda bi: (bi, 0, 0, 0))],
        out_shape=[jax.ShapeDtypeStruct((bsz, N_KV, nb, 4 * N_HD), BF16),
                   jax.ShapeDtypeStruct((bsz, N_KV, VROWS, nb), BF16)],
        compiler_params=_cparams(("parallel",)),
        name="nsa_compress",
    )(p, p, p, p, posk, posv, w1k, w1kc, w2k, w1v, w1vc, w2v)


def _relayout_kernel(ks_ref, vs_ref, kw_ref, vw_ref, ksa_ref, vsa_ref, kwa_ref, vwa_ref):
    i = pl.program_id(1)
    is_pad = i == 0
    flag = jnp.where(is_pad, 1.0, 0.0)
    row = lax.broadcasted_iota(jnp.int32, (SKT, 1), 0)
    lane = lax.broadcasted_iota(jnp.int32, (SKT, N_HD), 1)
    al = lane - AUX_ROWS
    ones_blk = jnp.where(lax.broadcasted_iota(jnp.int32, (VROWS - N_HD, SKT), 0) == 0, 1.0, 0.0)
    for u in range(PAD_TILES):
        rows = slice(u * SKT, (u + 1) * SKT)
        base = ((i - 1) * PAD_TILES + u) * SKT
        pos = jnp.where(is_pad, 0, base + row)
        blk = pos // SLC_LEN
        p_hi = (blk * SLC_LEN).astype(F32)
        p_lo = (pos - blk * SLC_LEN).astype(F32)
        mid_w = jnp.where((al >= 0) & (al < 3), p_hi,
                          jnp.where((al >= 3) & (al < 6), p_lo, jnp.where(al == 6, flag, 0.0)))
        mid_s = jnp.where(lane == blk % SEL_GRP, 1.0, mid_w)
        ks = ks_ref[0, rows, :]
        kw = kw_ref[0, rows, :]
        for g in range(N_KV):
            sl = slice(g * N_HD, (g + 1) * N_HD)
            ksa_ref[0, g, rows, :] = jnp.concatenate([ks[:, sl], mid_s], axis=1).astype(BF16)
            kwa_ref[0, g, rows, :] = jnp.concatenate([kw[:, sl], mid_w], axis=1).astype(BF16)
        vst = vs_ref[0, rows, :].T
        vwt = vw_ref[0, rows, :].T
        for g in range(N_KV):
            sl = slice(g * N_HD, (g + 1) * N_HD)
            vsa_ref[0, g, u] = jnp.concatenate([vst[sl], ones_blk], axis=0).astype(BF16)
            vwg = jnp.concatenate([vwt[sl], ones_blk], axis=0).astype(BF16)
            for j in range(SKT // WKT):
                vwa_ref[0, g, u * (SKT // WKT) + j] = vwg[:, j * WKT:(j + 1) * WKT]


def _relayout(p):
    bsz, t, _ = p.shape
    nt = t // SKT + PAD_TILES
    wpt = SKT // WKT
    rt = PAD_TILES
    cb = lambda c: (lambda bi, i: (bi, jnp.maximum(i - 1, 0), c))
    return pl.pallas_call(
        _relayout_kernel,
        grid=(bsz, nt // rt),
        in_specs=[pl.BlockSpec((1, rt * SKT, KV_W), cb(COL_KS // KV_W)),
                  pl.BlockSpec((1, rt * SKT, KV_W), cb(COL_VS // KV_W)),
                  pl.BlockSpec((1, rt * SKT, KV_W), cb(COL_KW // KV_W)),
                  pl.BlockSpec((1, rt * SKT, KV_W), cb(COL_VW // KV_W))],
        out_specs=[pl.BlockSpec((1, N_KV, rt * SKT, KA_SEL), lambda bi, i: (bi, 0, i, 0)),
                   pl.BlockSpec((1, N_KV, rt, VROWS, SKT), lambda bi, i: (bi, 0, i, 0, 0)),
                   pl.BlockSpec((1, N_KV, rt * SKT, KA_WIN), lambda bi, i: (bi, 0, i, 0)),
                   pl.BlockSpec((1, N_KV, rt * wpt, VROWS, WKT), lambda bi, i: (bi, 0, i, 0, 0))],
        out_shape=[jax.ShapeDtypeStruct((bsz, N_KV, nt * SKT, KA_SEL), BF16),
                   jax.ShapeDtypeStruct((bsz, N_KV, nt, VROWS, SKT), BF16),
                   jax.ShapeDtypeStruct((bsz, N_KV, nt * SKT, KA_WIN), BF16),
                   jax.ShapeDtypeStruct((bsz, N_KV, nt * wpt, VROWS, WKT), BF16)],
        compiler_params=_cparams(("parallel", "parallel")),
        name="nsa_relayout",
    )(p, p, p, p)


def _nsa_batch_kernel(sp_ref, q_ref, gt_ref, z_ref, kcmp_ref, vcmpt_ref, ovt_ref, ks_ref, vs_ref, kw_ref,
                      vw_ref, tri_ref, wb_ref, y_ref, sa_scr, sb_scr, g_scr, imp_scr, rank_scr, part_scr, selb_scr,
                      tile_idx):
    g = pl.program_id(0)
    qb = pl.program_id(1)
    n_items = q_ref.shape[0]
    start = qb * QBLK
    nb = kcmp_ref.shape[2]
    n_slc = ovt_ref.shape[0]
    nw = N_HPG * QBLK
    n_top = min(SLC_TOPN, n_slc)
    t_row = start + lax.broadcasted_iota(jnp.int32, (1, QBLK), 1)
    s_bufs = (sa_scr, sb_scr)
    last = qb // (SKT // QBLK)

    def tile4(a):
        return jnp.concatenate([a] * N_HPG, axis=1)

    def col_reduce8(s, op):
        out = s[0:8]
        for r in range(1, s.shape[0] // 8):
            out = op(out, s[8 * r:8 * r + 8])
        return out

    def col_max(s):
        return jnp.max(col_reduce8(s, jnp.maximum), axis=0, keepdims=True)

    def normalise(acc):
        den = acc[N_HD:N_HD + 1]
        return acc[0:N_HD] / jnp.where(den > 0, den, 1.0)

    def gate_row(i, h, branch):
        return g_scr[i, pl.ds(GATE_NG + (g * N_HPG + h) * 3 + branch, 1), :]

    hl = lax.broadcasted_iota(jnp.int32, (AUX_ROWS, nw), 1) // QBLK
    ar = lax.broadcasted_iota(jnp.int32, (AUX_ROWS, nw), 0)

    def per_head(vals):
        out = jnp.full((AUX_ROWS, nw), vals[N_HPG - 1], F32)
        for h in range(N_HPG - 2, -1, -1):
            out = jnp.where(hl == h, vals[h], out)
        return out

    pieces = [per_head([sp_ref[(g * N_HPG + h) * 3 + j] for h in range(N_HPG)]) for j in range(3)]
    aux = jnp.where(ar == 6, NEG, 0.0)
    for j in range(3):
        aux = jnp.where((ar == j) | (ar == j + 3), pieces[j], aux)
    aux_b = aux.astype(BF16)
    q_tail = jnp.zeros((KA_SEL - N_HD - 2 * AUX_ROWS, nw), BF16)
    sidx = lax.broadcasted_iota(jnp.int32, (n_slc, QBLK), 0)
    cur = t_row // SLC_LEN
    valid = sidx <= cur
    sub8 = lax.broadcasted_iota(jnp.int32, (8, QBLK), 0)
    last_blk = (start + QBLK - 1) // SLC_LEN
    cmp_end = lax.broadcasted_iota(jnp.int32, (nb, 1), 0) * CMP_STRIDE + (CMP_LEN - 1)
    cmp_bias = tile4(jnp.where(cmp_end <= t_row, 0.0, NEG))
    wrows = WIN + QBLK

    def prepare(i):
        qt = (q_ref[i] * (N_HD ** -0.5)).T
        q4l = jnp.concatenate([qt[h * N_HD:(h + 1) * N_HD] for h in range(N_HPG)], axis=1) * LOG2E
        q4s = q4l.astype(BF16)
        q4lo = (q4l - q4s.astype(F32)).astype(BF16)
        g_scr[i] = jax.nn.sigmoid(gt_ref[i]).T

        q_cmp = jnp.concatenate([q4s, q4s, q4lo, aux_b, jnp.zeros((N_HD - AUX_ROWS, nw), BF16)], axis=0)
        s_c = jnp.dot(kcmp_ref[i, 0], q_cmp, preferred_element_type=F32) + cmp_bias
        e_c = jnp.exp2(s_c - col_max(s_c))
        z_c = jnp.sum(col_reduce8(e_c, jnp.add), axis=0, keepdims=True)
        inv_c = jnp.where(tile4(t_row >= CMP_LEN - 1) & (z_c > 0), 1.0 / z_c, 0.0)
        o_c = jnp.dot(vcmpt_ref[i, 0], e_c.astype(BF16), preferred_element_type=F32)[0:N_HD] * inv_c
        p_c = e_c * inv_c
        psum = p_c[:, 0:QBLK]
        for h in range(1, N_HPG):
            psum = psum + p_c[:, h * QBLK:(h + 1) * QBLK]
        parts, rest = [], psum
        for _ in range(3):
            parts.append(rest.astype(BF16))
            rest = rest - parts[-1].astype(F32)
        imp = jnp.dot(ovt_ref[...], jnp.concatenate(parts, axis=0), preferred_element_type=F32)
        imp = jnp.where(valid, imp, -jnp.inf)
        imp_scr[...] = jnp.where((sidx == 0) | (sidx == cur), jnp.inf, imp)

        q_win = jnp.concatenate([q4s, jnp.zeros((AUX_ROWS, nw), BF16), aux_b,
                                 jnp.zeros((KA_WIN - N_HD - 2 * AUX_ROWS, nw), BF16)], axis=0)
        s_w = jnp.dot(kw_ref[i, 0, pl.ds(pl.multiple_of(start, QBLK), wrows), :], q_win,
                      preferred_element_type=F32)
        s_w = jnp.concatenate([s_w[0:WKT] + tile4(wb_ref[0]), s_w[WKT:wrows - WKT],
                               s_w[wrows - WKT:] + tile4(wb_ref[1])], axis=0)
        p_w = jnp.exp2(s_w - col_max(s_w)).astype(BF16)
        v_w = jnp.concatenate([vw_ref[i, 0, qb * (QBLK // WKT) + j] for j in range(wrows // WKT)], axis=1)
        o_w = normalise(jnp.dot(v_w, p_w, preferred_element_type=F32))
        for h in range(N_HPG):
            hs = slice(h * QBLK, (h + 1) * QBLK)
            part_scr[i, :, hs] = gate_row(i, h, 0) * o_c[:, hs] + gate_row(i, h, 2) * o_w[:, hs]

        rank_scr[...] = jnp.zeros_like(rank_scr)
        for ri in range(n_slc // 8):
            @pl.when((last_blk >= n_top) & (8 * ri <= last_blk))
            def _():
                rows = imp_scr[8 * ri:8 * ri + 8]
                for r in range(n_slc // 8):
                    blk8 = imp_scr[8 * r:8 * r + 8]
                    acc = rank_scr[8 * r:8 * r + 8]
                    for ii in range(8):
                        row = rows[ii:ii + 1]
                        if ri < r:
                            before = row >= blk8
                        elif ri > r:
                            before = row > blk8
                        else:
                            before = (row > blk8) | ((row == blk8) & (sub8 > ii))
                        acc = acc + jnp.where(before, 1.0, 0.0)
                    rank_scr[8 * r:8 * r + 8] = acc
        chosen = (rank_scr[...] < n_top) & valid
        selb_scr[i] = tile4(jnp.where(chosen, 0.0, NEG))

        any_q = jnp.max(jnp.where(chosen, 1.0, 0.0), axis=1, keepdims=True) > 0
        s_io = lax.broadcasted_iota(jnp.int32, (n_slc, 1), 0)
        bits = jnp.where(any_q, jnp.left_shift(1, s_io % 32), 0)
        words = [jnp.sum(jnp.where(s_io // 32 == w, bits, 0)) for w in range((n_slc + 31) // 32)]
        bpt = SKT // SLC_LEN
        cnt = jnp.int32(0)
        for tile in range(n_slc // bpt):
            nib = lax.shift_right_logical(words[tile * bpt // 32], jnp.int32(tile * bpt % 32)) & (2 ** bpt - 1)
            tile_idx[i, cnt] = tile
            cnt = cnt + ((nib != 0) & (tile < last)).astype(jnp.int32)
        return q4s, cnt

    q4, n_act = zip(*[prepare(i) for i in range(n_items)])


    def sel_scores(i, kt):
        grp = pl.multiple_of((kt * SKT // SLC_LEN) // SEL_GRP * SEL_GRP, SEL_GRP)
        sel_rows = jnp.concatenate([selb_scr[i, pl.ds(grp, SEL_GRP), :], jnp.zeros((AUX_ROWS - SEL_GRP, nw), F32)],
                                   axis=0)
        q_sel = jnp.concatenate([q4[i], sel_rows.astype(BF16), aux_b, q_tail], axis=0)
        krow = pl.multiple_of((kt + PAD_TILES) * SKT, SKT)
        return jnp.dot(ks_ref[i, 0, pl.ds(krow, SKT), :], q_sel, preferred_element_type=F32)

    def tile_rows(kt):
        return pl.ds(pl.multiple_of(kt * SKT, SKT), SKT)

    def pass1(i):
        def fn(kt, mrun):
            s = sel_scores(i, kt)
            s_bufs[i % 2][tile_rows(kt), :] = s
            return jnp.maximum(mrun, col_reduce8(s, jnp.maximum))
        return fn

    def pass2(i, m_sel):
        def fn(kt, acc):
            pr = jnp.exp2(s_bufs[i % 2][tile_rows(kt), :] - m_sel).astype(BF16)
            return acc + jnp.dot(vs_ref[i, 0, kt + PAD_TILES], pr, preferred_element_type=F32)
        return fn

    def listed(i, fn):
        return lambda j, carry: fn(tile_idx[i, j], carry)

    def both(f1, f2):
        def fn(j, carry):
            return f1(j, carry[0]), f2(j, carry[1])
        return fn

    def run_tiles(fn, lo, n, init):
        def group(size, first):
            def body(j, carry):
                for u in range(size):
                    carry = fn(first + j * size + u, carry)
                return carry
            return body

        n_u = n // UNROLL
        carry = lax.fori_loop(0, n_u, group(UNROLL, lo), init)
        done = lo + n_u * UNROLL
        size = UNROLL // 2
        while size >= 1:
            take = (n // size) % 2
            carry = lax.fori_loop(0, take, group(size, done), carry)
            done = done + take * size
            size //= 2
        return carry

    def diag_tile(i, mrun):
        s = sel_scores(i, last) + tile4(tri_ref[qb % (SKT // QBLK)])
        s_bufs[i % 2][tile_rows(last), :] = s
        return jnp.max(jnp.maximum(mrun, col_reduce8(s, jnp.maximum)), axis=0, keepdims=True)

    mrun0 = jnp.full((8, nw), NEG, F32)
    acc0 = jnp.zeros((VROWS, nw), F32)
    m_prev = diag_tile(0, run_tiles(listed(0, pass1(0)), 0, n_act[0], mrun0))
    outs = []
    for i in range(1, n_items):
        p2 = pass2(i - 1, m_prev)
        f1, f2 = listed(i, pass1(i)), listed(i - 1, p2)
        n_both = jnp.minimum(n_act[i], n_act[i - 1])
        mrun, acc = run_tiles(both(f1, f2), 0, n_both, (mrun0, acc0))
        mrun = run_tiles(f1, n_both, n_act[i] - n_both, mrun)
        acc = run_tiles(f2, n_both, n_act[i - 1] - n_both, acc)
        m_cur = diag_tile(i, mrun)
        outs.append(normalise(p2(last, acc)))
        m_prev = m_cur
    p2 = pass2(n_items - 1, m_prev)
    acc = run_tiles(listed(n_items - 1, p2), 0, n_act[n_items - 1], acc0)
    outs.append(normalise(p2(last, acc)))

    for i in range(n_items):
        tot = [part_scr[i, :, h * QBLK:(h + 1) * QBLK] + gate_row(i, h, 1) * outs[i][:, h * QBLK:(h + 1) * QBLK]
               for h in range(N_HPG)]
        o = jnp.concatenate(tot, axis=0).T
        y_ref[i] = (o * _silu(z_ref[i])).astype(BF16)


def _edge_biases():
    kl = np.arange(SKT)[:, None]
    ql = np.arange(QBLK)[None, :]
    tri = np.stack([np.where(kl <= par * QBLK + ql, 0.0, NEG) for par in range(SKT // QBLK)])
    kk = np.arange(WKT)[:, None]
    wb = np.stack([np.where(kk > ql, 0.0, NEG), np.where(kk <= ql + WKT - QBLK, 0.0, NEG)])
    return jnp.asarray(tri, F32), jnp.asarray(wb, F32)


def _nsa(p, spieces, kcmp, vcmpt, ovt, ks, vs, kw, vw):
    bsz, t, _ = p.shape
    n_slc = t // SLC_LEN
    gw = N_HPG * N_HD
    tri, wb = _edge_biases()
    nw = N_HPG * QBLK
    per_g = lambda a: pl.BlockSpec((bsz, 1) + a.shape[2:], lambda g, i, sp: (0, g) + (0,) * (a.ndim - 2))
    const = lambda a: pl.BlockSpec(a.shape, lambda g, i, sp: (0,) * a.ndim)
    grid_spec = pltpu.PrefetchScalarGridSpec(
        num_scalar_prefetch=1,
        grid=(N_KV, t // QBLK),
        in_specs=[pl.BlockSpec((bsz, QBLK, gw), lambda g, i, sp: (0, i, COL_NQ // gw + g)),
                  pl.BlockSpec((bsz, QBLK, LANES), lambda g, i, sp: (0, i, COL_GATES // LANES)),
                  pl.BlockSpec((bsz, QBLK, gw), lambda g, i, sp: (0, i, COL_NZ // gw + g)),
                  per_g(kcmp), per_g(vcmpt), const(ovt),
                  per_g(ks), per_g(vs), per_g(kw), per_g(vw), const(tri), const(wb)],
        out_specs=pl.BlockSpec((bsz, QBLK, gw), lambda g, i, sp: (0, i, g)),
        scratch_shapes=[pltpu.VMEM((t, nw), F32),
                        pltpu.VMEM((t, nw), F32),
                        pltpu.VMEM((bsz, LANES, QBLK), F32),
                        pltpu.VMEM((n_slc, QBLK), F32),
                        pltpu.VMEM((n_slc, QBLK), F32),
                        pltpu.VMEM((bsz, N_HD, nw), F32),
                        pltpu.VMEM((bsz, n_slc, nw), F32),
                        pltpu.SMEM((bsz, t // SKT), jnp.int32)],
    )
    return pl.pallas_call(
        _nsa_batch_kernel,
        grid_spec=grid_spec,
        out_shape=jax.ShapeDtypeStruct((bsz, t, N_WIDTH), BF16),
        compiler_params=_cparams(("arbitrary", "arbitrary")),
        name="nsa_attention",
    )(spieces, p, p, p, kcmp, vcmpt, ovt, ks, vs, kw, vw, tri, wb)


def _outproj_kernel(ym_ref, yn_ref, w_ref, x_ref, gate_ref, fg_ref, o_ref, wb_scr, *, final):
    @pl.when((pl.program_id(0) == 0) & (pl.program_id(1) == 0))
    def _():
        rows = w_ref.shape[1]
        for c in range(rows // WP_TILE):
            cs = slice(c * WP_TILE, (c + 1) * WP_TILE)
            wb_scr[cs, :] = w_ref[0, cs, :].astype(BF16)

    y = jnp.dot(ym_ref[0], wb_scr[0:M_WIDTH, :], preferred_element_type=F32)
    y = y + jnp.dot(yn_ref[0], wb_scr[M_WIDTH:, :], preferred_element_type=F32)
    hres = x_ref[0] + gate_ref[0] * y
    if final:
        ms = jnp.mean(hres * hres, axis=-1, keepdims=True)
        hres = hres * lax.rsqrt(ms + EPS) * fg_ref[...]
    o_ref[0] = hres


def _outproj(ym, yn, w, layer, x, gate, fg, final):
    bsz, t, d = x.shape
    tm = 512
    return pl.pallas_call(
        functools.partial(_outproj_kernel, final=final),
        grid=(bsz, t // tm),
        in_specs=[pl.BlockSpec((1, tm, M_WIDTH), lambda bi, i: (bi, i, 0)),
                  pl.BlockSpec((1, tm, N_WIDTH), lambda bi, i: (bi, i, 0)),
                  pl.BlockSpec((1, M_WIDTH + N_WIDTH, d), lambda bi, i: (layer, 0, 0),
                               pipeline_mode=pl.Buffered(1)),
                  pl.BlockSpec((1, tm, d), lambda bi, i: (bi, i, 0)),
                  pl.BlockSpec((1, 1, d), lambda bi, i: (bi, 0, 0)),
                  pl.BlockSpec((1, d), lambda bi, i: (0, 0))],
        out_specs=pl.BlockSpec((1, tm, d), lambda bi, i: (bi, i, 0)),
        out_shape=jax.ShapeDtypeStruct((bsz, t, d), F32),
        scratch_shapes=[pltpu.VMEM((M_WIDTH + N_WIDTH, d), BF16)],
        compiler_params=_cparams(("arbitrary", "arbitrary")),
        name="outproj_residual",
    )(ym, yn, w, x, gate, fg)


SRC_MI = 4 * M_WIDTH
SRC_NQ = SRC_MI + 2 * M_HEADS
SRC_NG = SRC_NQ + N_WIDTH + 6 * KV_W
SRC_NZ = SRC_NG + 3 * N_HEADS


def _reorder_cols(a):
    parts = [a[..., 0:SRC_MI], a[..., SRC_NQ:SRC_NG], a[..., SRC_NZ:SRC_NZ + N_WIDTH], a[..., SRC_MI:SRC_NQ],
             a[..., SRC_NG:SRC_NZ]]
    used = sum(x.shape[-1] for x in parts)
    parts.append(jnp.zeros(a.shape[:-1] + (NP_PAD - used,), a.dtype))
    return jnp.concatenate(parts, axis=-1)


WP_TILE = 512
WP_GATE_TILE = COL_GATES // WP_TILE


def _wprep_kernel(w_ref, g1_ref, g2_ref, o_ref):
    j = pl.program_id(0)
    d = o_ref.shape[0]

    @pl.when(j < WP_GATE_TILE)
    def _():
        for c in range(d // WP_TILE):
            cs = slice(c * WP_TILE, (c + 1) * WP_TILE)
            o_ref[cs, :] = w_ref[0, :, cs].T.astype(BF16)

    @pl.when(j == WP_GATE_TILE)
    def _():
        n_gate = g1_ref.shape[1] + g2_ref.shape[1]
        gt = jnp.concatenate([g1_ref[0], g2_ref[0], jnp.zeros((LANES - n_gate, d), F32)], axis=0)
        for c in range(d // WP_TILE):
            cs = slice(c * WP_TILE, (c + 1) * WP_TILE)
            o_ref[cs, 0:LANES] = gt[:, cs].T.astype(BF16)
        o_ref[:, LANES:] = jnp.zeros((d, WP_TILE - LANES), BF16)


def _wprep(w_t, layer):
    _, n, d = w_t.shape
    n_big = COL_GATES // WP_TILE
    assert n == SRC_NZ + N_WIDTH and NP_PAD // WP_TILE == n_big + 1

    def src_row(j):
        return jnp.where(j < COL_NQ // WP_TILE, j * WP_TILE,
                         jnp.where(j < COL_NZ // WP_TILE, SRC_NQ + (j - COL_NQ // WP_TILE) * WP_TILE,
                                   jnp.where(j < n_big, SRC_NZ + (j - COL_NZ // WP_TILE) * WP_TILE, 0)))

    el = pl.Element
    return pl.pallas_call(
        _wprep_kernel,
        grid=(NP_PAD // WP_TILE,),
        in_specs=[pl.BlockSpec((el(1), el(WP_TILE), el(d)), lambda j: (layer, pl.multiple_of(src_row(j), 8), 0)),
                  pl.BlockSpec((el(1), el(SRC_NQ - SRC_MI), el(d)), lambda j: (layer, SRC_MI, 0)),
                  pl.BlockSpec((el(1), el(SRC_NZ - SRC_NG), el(d)), lambda j: (layer, SRC_NG, 0))],
        out_specs=pl.BlockSpec((d, WP_TILE), lambda j: (0, j)),
        out_shape=jax.ShapeDtypeStruct((d, NP_PAD), BF16),
        compiler_params=_cparams(("parallel",)),
        name="inproj_weight_prep",
    )(w_t, w_t, w_t)


def _overlap_t(t):
    n_cmp_rows = t // CMP_STRIDE
    n_slc = t // SLC_LEN
    c0 = np.arange(n_cmp_rows) * CMP_STRIDE
    s0 = np.arange(n_slc) * SLC_LEN
    ov = (c0[None, :] <= s0[:, None] + SLC_LEN - 1) & (c0[None, :] + CMP_LEN - 1 >= s0[:, None])
    ov[:, (t - CMP_LEN) // CMP_STRIDE + 1:] = False
    return jnp.asarray(np.concatenate([ov] * 3, axis=1), BF16)


def kernel(x, c, ln_g, w_ada, b_ada, w_in, b_in, m_conv_w, m_conv_b, m_wq, m_wk, m_norm_w, m_skip, m_f_bias,
           n_pos_k, n_pos_v, n_w1_k, n_w2_k, n_w1_v, n_w2_v, w_out, final_g):
    out_dtype = x.dtype
    bsz, t, d = x.shape
    depth = ln_g.shape[0]
    h_res = x.astype(F32)
    assert bsz <= 2 and t % 1024 == 0 and t // CMP_STRIDE <= 256
    c_t =jnp.zeros((d, 8), F32).at[:, :bsz].set(c.astype(F32).T)
    slopes_np = np.array([2.0 ** (-8.0 * (h + 1) / N_HEADS) for h in range(N_HEADS)], np.float32)
    rest = (slopes_np.astype(np.float64) * LOG2E).astype(np.float32)
    pieces = []
    for _ in range(3):
        pieces.append(rest.astype(jnp.bfloat16).astype(np.float32))
        rest = rest - pieces[-1]
    spieces = jnp.asarray(np.stack(pieces, axis=1).reshape(-1))
    ovt = _overlap_t(t)

    def w1cat(w1):
        w = w1.reshape(2, CMP_STRIDE, N_HD, CMP_HIDDEN)
        w = jnp.concatenate([w[0], w[1]], axis=-1).astype(BF16)
        z = jnp.zeros_like(w)
        return jnp.concatenate([jnp.concatenate([w, z], axis=-1), jnp.concatenate([z, w], axis=-1)], axis=1)

    def w2pad(w2):
        return jnp.pad(w2, ((0, 0), (0, LANES - N_HD))).astype(BF16)

    for l in range(depth):
        mod = _ada(c_t, w_ada, b_ada[l][None, :], l, bsz)[:bsz]
        shift, scale, gate = mod[:, None, 0:d], mod[:, None, d:2 * d], mod[:, None, 2 * d:3 * d]
        p = _inproj(h_res, ln_g[l][None, :], scale, shift,
                    _wprep(jnp.swapaxes(w_in, 1, 2), l), _reorder_cols(b_in[l])[None, :])
        fb_row = jnp.zeros((1, LANES), F32).at[0, M_HEADS:2 * M_HEADS].set(m_f_bias[l])
        y_m = _mlstm(p, m_conv_w[l], m_conv_b[l][None, :], m_wq[l].astype(BF16), m_wk[l].astype(BF16),
                     m_norm_w[l][None, :], m_skip[l][None, :], fb_row)
        kcmp, vcmpt = _compress(p, n_pos_k[l].reshape(1, -1), n_pos_v[l].reshape(1, -1),
                                n_w1_k[l].astype(BF16), w1cat(n_w1_k[l]), w2pad(n_w2_k[l]),
                                n_w1_v[l].astype(BF16), w1cat(n_w1_v[l]), w2pad(n_w2_v[l]))
        ks, vs, kw, vw = _relayout(p)
        y_n = _nsa(p, spieces, kcmp, vcmpt, ovt, ks, vs, kw, vw)
        h_res = _outproj(y_m, y_n, w_out, l, h_res, gate, final_g[None, :], l == depth - 1)
    return h_res.astype(out_dtype)
```

```python
import functools

import numpy as np
import jax
import jax.numpy as jnp
from jax import lax
from jax.experimental import pallas as pl
from jax.experimental.pallas import tpu as pltpu

F32 = jnp.float32
BF16 = jnp.bfloat16

EPS = 1e-6
M_HEADS = 4
M_HD = 256
M_WIDTH = M_HEADS * M_HD
CONV_K = 4
M_CHUNK = 256
N_HEADS = 16
N_HD = 64
N_KV = 4
N_HPG = N_HEADS // N_KV
N_WIDTH = N_HEADS * N_HD
KV_W = N_KV * N_HD
CMP_LEN = 32
CMP_STRIDE = 16
CMP_HIDDEN = 2 * N_HD
SLC_LEN = 64
SLC_TOPN = 16
WIN = 512
QBLK = 256
SKT = 256
WKT = 256
PAD_TILES = WIN // SKT
SEL_GRP = 8
KA_SEL = 128
KA_WIN = 128
VROWS = 80
UNROLL = 4
AUX_ROWS = 16
LOG2E = 1.4426950408889634

COL_MX, COL_MV, COL_MO, COL_MZ = 0, 1024, 2048, 3072
COL_NQ = 4096
COL_KC, COL_VC, COL_KS, COL_VS, COL_KW, COL_VW = 5120, 5376, 5632, 5888, 6144, 6400
COL_NZ = 6656
COL_GATES = 7680
GATE_NG = 2 * M_HEADS
NP_PAD = 8192
NORM_CHUNKS = 4
LANES = 128
NEG = -1e30
VMEM_LIMIT = 58 * 1024 * 1024


def _cparams(sem):
    return pltpu.CompilerParams(dimension_semantics=sem, vmem_limit_bytes=VMEM_LIMIT)


def _silu(x):
    return x * jax.nn.sigmoid(x)


def _log_sigmoid(x):
    return jnp.minimum(x, 0.0) - jnp.log1p(jnp.exp(-jnp.abs(x)))


def _ada_kernel(ct_ref, w_ref, b_ref, o_ref, *, bsz):
    s_t = _silu(ct_ref[...])
    w = w_ref[0]
    row = lax.broadcasted_iota(jnp.int32, o_ref.shape, 0)
    out = jnp.zeros(o_ref.shape, F32)
    for b in range(bsz):
        prod = w * s_t[:, b:b + 1]
        acc = prod[0:8]
        for r in range(1, prod.shape[0] // 8):
            acc = acc + prod[8 * r:8 * r + 8]
        out = jnp.where(row == b, jnp.sum(acc, axis=0, keepdims=True) + b_ref[...], out)
    o_ref[...] = out


def _ada(c_t, w, b, layer, bsz):
    _, d, n = w.shape
    tn = 1024
    return pl.pallas_call(
        functools.partial(_ada_kernel, bsz=bsz),
        grid=(n // tn,),
        in_specs=[pl.BlockSpec((d, 8), lambda j: (0, 0)),
                  pl.BlockSpec((1, d, tn), lambda j: (layer, 0, j)),
                  pl.BlockSpec((1, tn), lambda j: (0, j))],
        out_specs=pl.BlockSpec((8, tn), lambda j: (0, j)),
        out_shape=jax.ShapeDtypeStruct((8, n), F32),
        compiler_params=_cparams(("parallel",)),
        name="ada_mod",
    )(c_t, w, b)


def _inproj_kernel(x_ref, g_ref, sc_ref, sh_ref, w_ref, b_ref, o_ref, h_ref):
    first = pl.program_id(2) == 0

    @pl.when(first)
    def _():
        tm = x_ref.shape[1]
        ck = tm // NORM_CHUNKS
        for c in range(NORM_CHUNKS):
            rows = slice(c * ck, (c + 1) * ck)
            x = x_ref[0, rows, :]
            ms = jnp.mean(x * x, axis=-1, keepdims=True)
            h = x * lax.rsqrt(ms + EPS) * g_ref[...]
            h = (h * (1.0 + sc_ref[0]) + sh_ref[0]).astype(BF16)
            h_ref[rows, :] = h
            o_ref[0, rows, :] = jnp.dot(h, w_ref[...], preferred_element_type=F32) + b_ref[...]

    @pl.when(jnp.logical_not(first))
    def _():
        o_ref[0] = jnp.dot(h_ref[...], w_ref[...], preferred_element_type=F32) + b_ref[...]


def _inproj(x, g, scale, shift, w, b):
    bsz, t, d = x.shape
    n = w.shape[1]
    tm, tn = 1024, 2048
    return pl.pallas_call(
        _inproj_kernel,
        grid=(bsz, t // tm, n // tn),
        in_specs=[pl.BlockSpec((1, tm, d), lambda bi, i, j: (bi, i, 0)),
                  pl.BlockSpec((1, d), lambda bi, i, j: (0, 0)),
                  pl.BlockSpec((1, 1, d), lambda bi, i, j: (bi, 0, 0)),
                  pl.BlockSpec((1, 1, d), lambda bi, i, j: (bi, 0, 0)),
                  pl.BlockSpec((d, tn), lambda bi, i, j: (0, j)),
                  pl.BlockSpec((1, tn), lambda bi, i, j: (0, j))],
        out_specs=pl.BlockSpec((1, tm, tn), lambda bi, i, j: (bi, i, j)),
        out_shape=jax.ShapeDtypeStruct((bsz, t, n), F32),
        scratch_shapes=[pltpu.VMEM((tm, d), BF16)],
        compiler_params=_cparams(("parallel", "parallel", "arbitrary")),
        name="norm_inproj",
    )(x, g, scale, shift, w, b)


def _mlstm_kernel(x_ref, v_ref, o_ref, z_ref, gt_ref, cw_ref, cb_ref, wq_ref, wk_ref, nw_ref, sk_ref, fb_ref,
                  y_ref, c_scr, n_scr, m_scr, xp_scr):
    L = M_CHUNK

    @pl.when(pl.program_id(1) == 0)
    def _():
        c_scr[...] = jnp.zeros_like(c_scr)
        n_scr[...] = jnp.zeros_like(n_scr)
        m_scr[...] = jnp.zeros_like(m_scr)
        xp_scr[...] = jnp.zeros_like(xp_scr)

    x = x_ref[0]
    prev = xp_scr[...]
    row8 = lax.broadcasted_iota(jnp.int32, (8, M_WIDTH), 0)
    cw = cw_ref[...]
    xc = cb_ref[...] + x * cw[CONV_K - 1:CONV_K, :]
    for sft in range(1, CONV_K):
        xr = pltpu.roll(x, sft, 0)
        top = jnp.where(row8 < sft, pltpu.roll(prev, sft, 0), xr[0:8])
        xs = jnp.concatenate([top, xr[8:]], axis=0)
        xc = xc + xs * cw[CONV_K - 1 - sft:CONV_K - sft, :]
    xp_scr[...] = x[L - 8:L]
    xc = _silu(xc)

    gt = gt_ref[0]
    col = lax.broadcasted_iota(jnp.int32, (L, LANES), 1)
    logf = _log_sigmoid(gt + fb_ref[...])
    a_c = jnp.where((col >= M_HEADS) & (col < 2 * M_HEADS), logf, gt)
    ri = lax.broadcasted_iota(jnp.int32, (L, L), 0)
    ci = lax.broadcasted_iota(jnp.int32, (L, L), 1)
    causal = ri >= ci
    hp = lax.Precision.HIGHEST
    tri = causal.astype(F32)
    tri_t = (ri <= ci).astype(F32)
    b_c = jnp.dot(tri, a_c, precision=hp, preferred_element_type=F32)
    a_r = a_c.T
    b_r = jnp.dot(a_r[0:8], tri_t, precision=hp, preferred_element_type=F32)

    for h in range(M_HEADS):
        sl = slice(h * M_HD, (h + 1) * M_HD)
        xh = xc[:, sl]
        xb = xh.astype(BF16)
        q = jnp.dot(xb, wq_ref[h], preferred_element_type=F32)
        k = jnp.dot(xb, wk_ref[h], preferred_element_type=F32) * (M_HD ** -0.5)
        vb = v_ref[0, :, sl].astype(BF16)
        qb = q.astype(BF16)
        kb = k.astype(BF16)

        bt = b_c[:, M_HEADS + h:M_HEADS + h + 1]
        ic = a_c[:, h:h + 1]
        bs = b_r[M_HEADS + h:M_HEADS + h + 1, :]
        ir = a_r[h:h + 1, :]
        m_prev = m_scr[h][:, 0:1]

        dm = jnp.where(causal, bt - bs + ir, -jnp.inf)
        inter = bt + m_prev
        m_t = jnp.maximum(inter, jnp.max(dm, axis=-1, keepdims=True))
        w_in = jnp.exp(dm - m_t)
        w_st = jnp.exp(inter - m_t)
        s = lax.dot_general(qb, kb, (((1,), (1,)), ((), ())), preferred_element_type=F32) * w_in
        cmat = c_scr[h]
        nvec = n_scr[h]
        sb = s.astype(BF16)
        num = w_st * jnp.dot(qb, cmat.astype(BF16), preferred_element_type=F32) \
            + jnp.dot(sb, vb, preferred_element_type=F32)
        nt_dims = (((1,), (1,)), ((), ()))
        qn = lax.dot_general(qb, jnp.broadcast_to(nvec, (8, M_HD)).astype(BF16), nt_dims,
                             preferred_element_type=F32)[:, 0:1]
        ssum = lax.dot_general(sb, jnp.ones((8, L), BF16), nt_dims, preferred_element_type=F32)[:, 0:1]
        den = w_st * qn + ssum
        hh = num / jnp.maximum(jnp.abs(den), jnp.exp(-m_t))

        b_last = bt[L - 1:L, :]
        w_end = b_last - bt + ic
        m_new = jnp.maximum(b_last + m_prev, jnp.max(w_end, axis=0, keepdims=True))
        decay = jnp.exp(b_last + m_prev - m_new)
        kwt = k * jnp.exp(w_end - m_new)
        c_scr[h] = decay * cmat + lax.dot_general(kwt.astype(BF16), vb, (((0,), (0,)), ((), ())),
                                                  preferred_element_type=F32)
        n_scr[h] = decay * nvec + jnp.sum(kwt, axis=0, keepdims=True)
        m_scr[h] = jnp.broadcast_to(m_new, (1, LANES))

        mu = jnp.mean(hh, axis=-1, keepdims=True)
        hc = hh - mu
        var = jnp.mean(hc * hc, axis=-1, keepdims=True)
        hn = hc * lax.rsqrt(var + EPS) * nw_ref[:, sl]
        out = jax.nn.sigmoid(o_ref[0, :, sl]) * hn + sk_ref[:, sl] * xh
        y_ref[0, :, sl] = (out * _silu(z_ref[0, :, sl])).astype(BF16)


def _mlstm(p, conv_w, conv_b, wq, wk, norm_w, skip, fb_row):
    bsz, t, _ = p.shape
    L = M_CHUNK
    cb = lambda c: (lambda bi, i: (bi, i, c))
    full2 = lambda bi, i: (0, 0)
    full3 = lambda bi, i: (0, 0, 0)
    return pl.pallas_call(
        _mlstm_kernel,
        grid=(bsz, t // L),
        in_specs=[pl.BlockSpec((1, L, M_WIDTH), cb(COL_MX // M_WIDTH)),
                  pl.BlockSpec((1, L, M_WIDTH), cb(COL_MV // M_WIDTH)),
                  pl.BlockSpec((1, L, M_WIDTH), cb(COL_MO // M_WIDTH)),
                  pl.BlockSpec((1, L, M_WIDTH), cb(COL_MZ // M_WIDTH)),
                  pl.BlockSpec((1, L, LANES), cb(COL_GATES // LANES)),
                  pl.BlockSpec((CONV_K, M_WIDTH), full2),
                  pl.BlockSpec((1, M_WIDTH), full2),
                  pl.BlockSpec((M_HEADS, M_HD, M_HD), full3),
                  pl.BlockSpec((M_HEADS, M_HD, M_HD), full3),
                  pl.BlockSpec((1, M_WIDTH), full2),
                  pl.BlockSpec((1, M_WIDTH), full2),
                  pl.BlockSpec((1, LANES), full2)],
        out_specs=pl.BlockSpec((1, L, M_WIDTH), lambda bi, i: (bi, i, 0)),
        out_shape=jax.ShapeDtypeStruct((bsz, t, M_WIDTH), BF16),
        scratch_shapes=[pltpu.VMEM((M_HEADS, M_HD, M_HD), F32),
                        pltpu.VMEM((M_HEADS, 1, M_HD), F32),
                        pltpu.VMEM((M_HEADS, 1, LANES), F32),
                        pltpu.VMEM((8, M_WIDTH), F32)],
        compiler_params=_cparams(("parallel", "arbitrary")),
        name="mlstm_group",
    )(p, p, p, p, p, conv_w, conv_b, wq, wk, norm_w, skip, fb_row)


def _compress_kernel(kc0_ref, kc1_ref, vc0_ref, vc1_ref, posk_ref, posv_ref, w1k_ref, w1kc_ref, w2k_ref,
                     w1v_ref, w1vc_ref, w2v_ref, kcmp_ref, vcmpt_ref):
    nb = kcmp_ref.shape[2]

    def hidden(src_refs, pos_ref, w1_ref, w1c_ref):
        halves = [jnp.zeros((nb, 4 * CMP_HIDDEN), F32) for _ in src_refs]
        for l in range(CMP_STRIDE):
            for hf, r in enumerate(src_refs):
                xl = r[0, pl.ds(l, nb, stride=CMP_STRIDE), :]
                halves[hf] = halves[hf] + jnp.dot(xl.astype(BF16), w1c_ref[l], preferred_element_type=F32)
        acc = jnp.concatenate([hv[:, c * 2 * CMP_HIDDEN:(c + 1) * 2 * CMP_HIDDEN] for hv in halves for c in range(2)],
                              axis=0)
        first = acc[:, :CMP_HIDDEN]
        second = acc[:, CMP_HIDDEN:]
        posb = jnp.dot(jnp.broadcast_to(pos_ref[...], (8, CMP_LEN * N_HD)).astype(BF16), w1_ref[...],
                       preferred_element_type=F32)[0:1]
        hid = first + pltpu.roll(second, N_KV * nb - 1, 0) + posb
        return jax.nn.gelu(hid).astype(BF16)

    hk = hidden((kc0_ref, kc1_ref), posk_ref, w1k_ref, w1kc_ref)
    kc = jnp.dot(hk, w2k_ref[...], preferred_element_type=F32)
    hv = hidden((vc0_ref, vc1_ref), posv_ref, w1v_ref, w1vc_ref)
    vc = jnp.dot(hv, w2v_ref[...], preferred_element_type=F32)
    j = lax.broadcasted_iota(jnp.int32, (nb, 1), 0)
    lane = lax.broadcasted_iota(jnp.int32, (nb, N_HD), 1)
    n_real = (nb * CMP_STRIDE - CMP_LEN) // CMP_STRIDE + 1
    aux = jnp.where(lane < 3, (j * CMP_STRIDE).astype(F32),
                    jnp.where(lane < 6, (CMP_LEN - 1) * 0.5,
                              jnp.where((lane == 6) & (j >= n_real), 1.0, 0.0))).astype(BF16)
    ones_blk = jnp.where(lax.broadcasted_iota(jnp.int32, (VROWS - N_HD, nb), 0) == 0, 1.0, 0.0)
    for g in range(N_KV):
        kg = kc[g * nb:(g + 1) * nb, 0:N_HD]
        k_hi = kg.astype(BF16)
        k_lo = (kg - k_hi.astype(F32)).astype(BF16)
        kcmp_ref[0, g] = jnp.concatenate([k_hi, k_lo, k_hi, aux], axis=1)
        vt = vc[g * nb:(g + 1) * nb].T
        vcmpt_ref[0, g] = jnp.concatenate([vt[0:N_HD], ones_blk], axis=0).astype(BF16)


def _compress(p, posk, posv, w1k, w1kc, w2k, w1v, w1vc, w2v):
    bsz, t, _ = p.shape
    nb = t // CMP_STRIDE
    cb = lambda c: (lambda bi: (bi, 0, c))
    f2 = lambda bi: (0, 0)
    f3 = lambda bi: (0, 0, 0)
    wspecs = [pl.BlockSpec((CMP_LEN * N_HD, CMP_HIDDEN), f2),
              pl.BlockSpec((CMP_STRIDE, LANES, 4 * CMP_HIDDEN), f3),
              pl.BlockSpec((CMP_HIDDEN, LANES), f2)]
    return pl.pallas_call(
        _compress_kernel,
        grid=(bsz,),
        in_specs=[pl.BlockSpec((1, t, LANES), cb(COL_KC // LANES)),
                  pl.BlockSpec((1, t, LANES), cb(COL_KC // LANES + 1)),
                  pl.BlockSpec((1, t, LANES), cb(COL_VC // LANES)),
                  pl.BlockSpec((1, t, LANES), cb(COL_VC // LANES + 1)),
                  pl.BlockSpec((1, CMP_LEN * N_HD), f2),
                  pl.BlockSpec((1, CMP_LEN * N_HD), f2)] + wspecs + wspecs,
        out_specs=[pl.BlockSpec((1, N_KV, nb, 4 * N_HD), lambda bi: (bi, 0, 0, 0)),
                   pl.BlockSpec((1, N_KV, VROWS, nb), lambda bi: (bi, 0, 0, 0))],
        out_shape=[jax.ShapeDtypeStruct((bsz, N_KV, nb, 4 * N_HD), BF16),
                   jax.ShapeDtypeStruct((bsz, N_KV, VROWS, nb), BF16)],
        compiler_params=_cparams(("parallel",)),
        name="nsa_compress",
    )(p, p, p, p, posk, posv, w1k, w1kc, w2k, w1v, w1vc, w2v)


def _relayout_kernel(ks_ref, vs_ref, kw_ref, vw_ref, ksa_ref, vsa_ref, kwa_ref, vwa_ref):
    i = pl.program_id(1)
    is_pad = i == 0
    flag = jnp.where(is_pad, 1.0, 0.0)
    row = lax.broadcasted_iota(jnp.int32, (SKT, 1), 0)
    lane = lax.broadcasted_iota(jnp.int32, (SKT, N_HD), 1)
    al = lane - AUX_ROWS
    ones_blk = jnp.where(lax.broadcasted_iota(jnp.int32, (VROWS - N_HD, SKT), 0) == 0, 1.0, 0.0)
    for u in range(PAD_TILES):
        rows = slice(u * SKT, (u + 1) * SKT)
        base = ((i - 1) * PAD_TILES + u) * SKT
        pos = jnp.where(is_pad, 0, base + row)
        blk = pos // SLC_LEN
        p_hi = (blk * SLC_LEN).astype(F32)
        p_lo = (pos - blk * SLC_LEN).astype(F32)
        mid_w = jnp.where((al >= 0) & (al < 3), p_hi,
                          jnp.where((al >= 3) & (al < 6), p_lo, jnp.where(al == 6, flag, 0.0)))
        mid_s = jnp.where(lane == blk % SEL_GRP, 1.0, mid_w)
        ks = ks_ref[0, rows, :]
        kw = kw_ref[0, rows, :]
        for g in range(N_KV):
            sl = slice(g * N_HD, (g + 1) * N_HD)
            ksa_ref[0, g, rows, :] = jnp.concatenate([ks[:, sl], mid_s], axis=1).astype(BF16)
            kwa_ref[0, g, rows, :] = jnp.concatenate([kw[:, sl], mid_w], axis=1).astype(BF16)
        vst = vs_ref[0, rows, :].T
        vwt = vw_ref[0, rows, :].T
        for g in range(N_KV):
            sl = slice(g * N_HD, (g + 1) * N_HD)
            vsa_ref[0, g, u] = jnp.concatenate([vst[sl], ones_blk], axis=0).astype(BF16)
            vwg = jnp.concatenate([vwt[sl], ones_blk], axis=0).astype(BF16)
            for j in range(SKT // WKT):
                vwa_ref[0, g, u * (SKT // WKT) + j] = vwg[:, j * WKT:(j + 1) * WKT]


def _relayout(p):
    bsz, t, _ = p.shape
    nt = t // SKT + PAD_TILES
    wpt = SKT // WKT
    rt = PAD_TILES
    cb = lambda c: (lambda bi, i: (bi, jnp.maximum(i - 1, 0), c))
    return pl.pallas_call(
        _relayout_kernel,
        grid=(bsz, nt // rt),
        in_specs=[pl.BlockSpec((1, rt * SKT, KV_W), cb(COL_KS // KV_W)),
                  pl.BlockSpec((1, rt * SKT, KV_W), cb(COL_VS // KV_W)),
                  pl.BlockSpec((1, rt * SKT, KV_W), cb(COL_KW // KV_W)),
                  pl.BlockSpec((1, rt * SKT, KV_W), cb(COL_VW // KV_W))],
        out_specs=[pl.BlockSpec((1, N_KV, rt * SKT, KA_SEL), lambda bi, i: (bi, 0, i, 0)),
                   pl.BlockSpec((1, N_KV, rt, VROWS, SKT), lambda bi, i: (bi, 0, i, 0, 0)),
                   pl.BlockSpec((1, N_KV, rt * SKT, KA_WIN), lambda bi, i: (bi, 0, i, 0)),
                   pl.BlockSpec((1, N_KV, rt * wpt, VROWS, WKT), lambda bi, i: (bi, 0, i, 0, 0))],
        out_shape=[jax.ShapeDtypeStruct((bsz, N_KV, nt * SKT, KA_SEL), BF16),
                   jax.ShapeDtypeStruct((bsz, N_KV, nt, VROWS, SKT), BF16),
                   jax.ShapeDtypeStruct((bsz, N_KV, nt * SKT, KA_WIN), BF16),
                   jax.ShapeDtypeStruct((bsz, N_KV, nt * wpt, VROWS, WKT), BF16)],
        compiler_params=_cparams(("parallel", "parallel")),
        name="nsa_relayout",
    )(p, p, p, p)


def _nsa_batch_kernel(sp_ref, q_ref, gt_ref, z_ref, kcmp_ref, vcmpt_ref, ovt_ref, ks_ref, vs_ref, kw_ref,
                      vw_ref, tri_ref, wb_ref, y_ref, sa_scr, sb_scr, g_scr, imp_scr, rank_scr, part_scr, selb_scr,
                      tile_idx, q4_scr, n_act_ref):
    g = pl.program_id(0)
    qb = pl.program_id(1)
    n_items = q_ref.shape[0]
    start = qb * QBLK
    nb = kcmp_ref.shape[2]
    n_slc = ovt_ref.shape[0]
    nw = N_HPG * QBLK
    n_top = min(SLC_TOPN, n_slc)
    t_row = start + lax.broadcasted_iota(jnp.int32, (1, QBLK), 1)
    s_bufs = (sa_scr, sb_scr)
    last = qb // (SKT // QBLK)

    def tile4(a):
        return jnp.concatenate([a] * N_HPG, axis=1)

    def col_reduce8(s, op):
        out = s[0:8]
        for r in range(1, s.shape[0] // 8):
            out = op(out, s[8 * r:8 * r + 8])
        return out

    def col_max(s):
        return jnp.max(col_reduce8(s, jnp.maximum), axis=0, keepdims=True)

    def normalise(acc):
        den = acc[N_HD:N_HD + 1]
        return acc[0:N_HD] / jnp.where(den > 0, den, 1.0)

    def gate_row(i, h, branch):
        return g_scr[i, pl.ds(GATE_NG + (g * N_HPG + h) * 3 + branch, 1), :]

    hl = lax.broadcasted_iota(jnp.int32, (AUX_ROWS, nw), 1) // QBLK
    ar = lax.broadcasted_iota(jnp.int32, (AUX_ROWS, nw), 0)

    def per_head(vals):
        out = jnp.full((AUX_ROWS, nw), vals[N_HPG - 1], F32)
        for h in range(N_HPG - 2, -1, -1):
            out = jnp.where(hl == h, vals[h], out)
        return out

    pieces = [per_head([sp_ref[(g * N_HPG + h) * 3 + j] for h in range(N_HPG)]) for j in range(3)]
    aux = jnp.where(ar == 6, NEG, 0.0)
    for j in range(3):
        aux = jnp.where((ar == j) | (ar == j + 3), pieces[j], aux)
    aux_b = aux.astype(BF16)
    q_tail = jnp.zeros((KA_SEL - N_HD - 2 * AUX_ROWS, nw), BF16)
    sidx = lax.broadcasted_iota(jnp.int32, (n_slc, QBLK), 0)
    cur = t_row // SLC_LEN
    valid = sidx <= cur
    sub8 = lax.broadcasted_iota(jnp.int32, (8, QBLK), 0)
    last_blk = (start + QBLK - 1) // SLC_LEN
    cmp_end = lax.broadcasted_iota(jnp.int32, (nb, 1), 0) * CMP_STRIDE + (CMP_LEN - 1)
    cmp_bias = tile4(jnp.where(cmp_end <= t_row, 0.0, NEG))
    wrows = WIN + QBLK

    def prepare(i):
        qt = (q_ref[i] * (N_HD ** -0.5)).T
        q4l = jnp.concatenate([qt[h * N_HD:(h + 1) * N_HD] for h in range(N_HPG)], axis=1) * LOG2E
        q4s = q4l.astype(BF16)
        q4lo = (q4l - q4s.astype(F32)).astype(BF16)
        g_scr[i] = jax.nn.sigmoid(gt_ref[i]).T

        q_cmp = jnp.concatenate([q4s, q4s, q4lo, aux_b, jnp.zeros((N_HD - AUX_ROWS, nw), BF16)], axis=0)
        s_c = jnp.dot(kcmp_ref[i, 0], q_cmp, preferred_element_type=F32) + cmp_bias
        e_c = jnp.exp2(s_c - col_max(s_c))
        z_c = jnp.sum(col_reduce8(e_c, jnp.add), axis=0, keepdims=True)
        inv_c = jnp.where(tile4(t_row >= CMP_LEN - 1) & (z_c > 0), 1.0 / z_c, 0.0)
        o_c = jnp.dot(vcmpt_ref[i, 0], e_c.astype(BF16), preferred_element_type=F32)[0:N_HD] * inv_c
        p_c = e_c * inv_c
        psum = p_c[:, 0:QBLK]
        for h in range(1, N_HPG):
            psum = psum + p_c[:, h * QBLK:(h + 1) * QBLK]
        parts, rest = [], psum
        for _ in range(3):
            parts.append(rest.astype(BF16))
            rest = rest - parts[-1].astype(F32)
        imp = jnp.dot(ovt_ref[...], jnp.concatenate(parts, axis=0), preferred_element_type=F32)
        imp = jnp.where(valid, imp, -jnp.inf)
        imp_scr[...] = jnp.where((sidx == 0) | (sidx == cur), jnp.inf, imp)

        q_win = jnp.concatenate([q4s, jnp.zeros((AUX_ROWS, nw), BF16), aux_b,
                                 jnp.zeros((KA_WIN - N_HD - 2 * AUX_ROWS, nw), BF16)], axis=0)
        s_w = jnp.dot(kw_ref[i, 0, pl.ds(pl.multiple_of(start, QBLK), wrows), :], q_win,
                      preferred_element_type=F32)
        s_w = jnp.concatenate([s_w[0:WKT] + tile4(wb_ref[0]), s_w[WKT:wrows - WKT],
                               s_w[wrows - WKT:] + tile4(wb_ref[1])], axis=0)
        p_w = jnp.exp2(s_w - col_max(s_w)).astype(BF16)
        v_w = jnp.concatenate([vw_ref[i, 0, qb * (QBLK // WKT) + j] for j in range(wrows // WKT)], axis=1)
        o_w = normalise(jnp.dot(v_w, p_w, preferred_element_type=F32))
        for h in range(N_HPG):
            hs = slice(h * QBLK, (h + 1) * QBLK)
            part_scr[i, :, hs] = gate_row(i, h, 0) * o_c[:, hs] + gate_row(i, h, 2) * o_w[:, hs]

        rank_scr[...] = jnp.zeros_like(rank_scr)
        for ri in range(n_slc // 8):
            @pl.when((last_blk >= n_top) & (8 * ri <= last_blk))
            def _():
                rows = imp_scr[8 * ri:8 * ri + 8]
                for r in range(n_slc // 8):
                    blk8 = imp_scr[8 * r:8 * r + 8]
                    acc = rank_scr[8 * r:8 * r + 8]
                    for ii in range(8):
                        row = rows[ii:ii + 1]
                        if ri < r:
                            before = row >= blk8
                        elif ri > r:
                            before = row > blk8
                        else:
                            before = (row > blk8) | ((row == blk8) & (sub8 > ii))
                        acc = acc + jnp.where(before, 1.0, 0.0)
                    rank_scr[8 * r:8 * r + 8] = acc
        chosen = (rank_scr[...] < n_top) & valid
        selb_scr[i] = tile4(jnp.where(chosen, 0.0, NEG))

        any_q = jnp.max(jnp.where(chosen, 1.0, 0.0), axis=1, keepdims=True) > 0
        s_io = lax.broadcasted_iota(jnp.int32, (n_slc, 1), 0)
        bits = jnp.where(any_q, jnp.left_shift(1, s_io % 32), 0)
        words = [jnp.sum(jnp.where(s_io // 32 == w, bits, 0)) for w in range((n_slc + 31) // 32)]
        bpt = SKT // SLC_LEN
        cnt = jnp.int32(0)
        for tile in range(n_slc // bpt):
            nib = lax.shift_right_logical(words[tile * bpt // 32], jnp.int32(tile * bpt % 32)) & (2 ** bpt - 1)
            tile_idx[i, cnt] = tile
            cnt = cnt + ((nib != 0) & (tile < last)).astype(jnp.int32)
        q4_scr[i] = q4s
        n_act_ref[i] = cnt

    def prepare_step(i, carry):
        prepare(i)
        return carry

    lax.fori_loop(0, n_items, prepare_step, 0)
    q4 = [q4_scr[i] for i in range(n_items)]
    n_act = [n_act_ref[i] for i in range(n_items)]


    def sel_scores(i, kt):
        grp = pl.multiple_of((kt * SKT // SLC_LEN) // SEL_GRP * SEL_GRP, SEL_GRP)
        sel_rows = jnp.concatenate([selb_scr[i, pl.ds(grp, SEL_GRP), :], jnp.zeros((AUX_ROWS - SEL_GRP, nw), F32)],
                                   axis=0)
        q_sel = jnp.concatenate([q4[i], sel_rows.astype(BF16), aux_b, q_tail], axis=0)
        krow = pl.multiple_of((kt + PAD_TILES) * SKT, SKT)
        return jnp.dot(ks_ref[i, 0, pl.ds(krow, SKT), :], q_sel, preferred_element_type=F32)

    def tile_rows(kt):
        return pl.ds(pl.multiple_of(kt * SKT, SKT), SKT)

    def pass1(i):
        def fn(kt, mrun):
            s = sel_scores(i, kt)
            s_bufs[i % 2][tile_rows(kt), :] = s
            return jnp.maximum(mrun, col_reduce8(s, jnp.maximum))
        return fn

    def pass2(i, m_sel):
        def fn(kt, acc):
            pr = jnp.exp2(s_bufs[i % 2][tile_rows(kt), :] - m_sel).astype(BF16)
            return acc + jnp.dot(vs_ref[i, 0, kt + PAD_TILES], pr, preferred_element_type=F32)
        return fn

    def listed(i, fn):
        return lambda j, carry: fn(tile_idx[i, j], carry)

    def both(f1, f2):
        def fn(j, carry):
            return f1(j, carry[0]), f2(j, carry[1])
        return fn

    def run_tiles(fn, lo, n, init):
        def group(size, first):
            def body(j, carry):
                for u in range(size):
                    carry = fn(first + j * size + u, carry)
                return carry
            return body

        n_u = n // UNROLL
        carry = lax.fori_loop(0, n_u, group(UNROLL, lo), init)
        done = lo + n_u * UNROLL
        size = UNROLL // 2
        while size >= 1:
            take = (n // size) % 2
            carry = lax.fori_loop(0, take, group(size, done), carry)
            done = done + take * size
            size //= 2
        return carry

    def diag_tile(i, mrun):
        s = sel_scores(i, last) + tile4(tri_ref[qb % (SKT // QBLK)])
        s_bufs[i % 2][tile_rows(last), :] = s
        return jnp.max(jnp.maximum(mrun, col_reduce8(s, jnp.maximum)), axis=0, keepdims=True)

    mrun0 = jnp.full((8, nw), NEG, F32)
    acc0 = jnp.zeros((VROWS, nw), F32)
    m_prev = diag_tile(0, run_tiles(listed(0, pass1(0)), 0, n_act[0], mrun0))
    outs = []
    for i in range(1, n_items):
        p2 = pass2(i - 1, m_prev)
        f1, f2 = listed(i, pass1(i)), listed(i - 1, p2)
        n_both = jnp.minimum(n_act[i], n_act[i - 1])
        mrun, acc = run_tiles(both(f1, f2), 0, n_both, (mrun0, acc0))
        mrun = run_tiles(f1, n_both, n_act[i] - n_both, mrun)
        acc = run_tiles(f2, n_both, n_act[i - 1] - n_both, acc)
        m_cur = diag_tile(i, mrun)
        outs.append(normalise(p2(last, acc)))
        m_prev = m_cur
    p2 = pass2(n_items - 1, m_prev)
    acc = run_tiles(listed(n_items - 1, p2), 0, n_act[n_items - 1], acc0)
    outs.append(normalise(p2(last, acc)))

    for i in range(n_items):
        tot = [part_scr[i, :, h * QBLK:(h + 1) * QBLK] + gate_row(i, h, 1) * outs[i][:, h * QBLK:(h + 1) * QBLK]
               for h in range(N_HPG)]
        o = jnp.concatenate(tot, axis=0).T
        y_ref[i] = (o * _silu(z_ref[i])).astype(BF16)


def _edge_biases():
    kl = np.arange(SKT)[:, None]
    ql = np.arange(QBLK)[None, :]
    tri = np.stack([np.where(kl <= par * QBLK + ql, 0.0, NEG) for par in range(SKT // QBLK)])
    kk = np.arange(WKT)[:, None]
    wb = np.stack([np.where(kk > ql, 0.0, NEG), np.where(kk <= ql + WKT - QBLK, 0.0, NEG)])
    return jnp.asarray(tri, F32), jnp.asarray(wb, F32)


def _nsa(p, spieces, kcmp, vcmpt, ovt, ks, vs, kw, vw):
    bsz, t, _ = p.shape
    n_slc = t // SLC_LEN
    gw = N_HPG * N_HD
    tri, wb = _edge_biases()
    nw = N_HPG * QBLK
    per_g = lambda a: pl.BlockSpec((bsz, 1) + a.shape[2:], lambda g, i, sp: (0, g) + (0,) * (a.ndim - 2))
    const = lambda a: pl.BlockSpec(a.shape, lambda g, i, sp: (0,) * a.ndim)
    grid_spec = pltpu.PrefetchScalarGridSpec(
        num_scalar_prefetch=1,
        grid=(N_KV, t // QBLK),
        in_specs=[pl.BlockSpec((bsz, QBLK, gw), lambda g, i, sp: (0, i, COL_NQ // gw + g)),
                  pl.BlockSpec((bsz, QBLK, LANES), lambda g, i, sp: (0, i, COL_GATES // LANES)),
                  pl.BlockSpec((bsz, QBLK, gw), lambda g, i, sp: (0, i, COL_NZ // gw + g)),
                  per_g(kcmp), per_g(vcmpt), const(ovt),
                  per_g(ks), per_g(vs), per_g(kw), per_g(vw), const(tri), const(wb)],
        out_specs=pl.BlockSpec((bsz, QBLK, gw), lambda g, i, sp: (0, i, g)),
        scratch_shapes=[pltpu.VMEM((t, nw), F32),
                        pltpu.VMEM((t, nw), F32),
                        pltpu.VMEM((bsz, LANES, QBLK), F32),
                        pltpu.VMEM((n_slc, QBLK), F32),
                        pltpu.VMEM((n_slc, QBLK), F32),
                        pltpu.VMEM((bsz, N_HD, nw), F32),
                        pltpu.VMEM((bsz, n_slc, nw), F32),
                        pltpu.SMEM((bsz, t // SKT), jnp.int32),
                        pltpu.VMEM((bsz, N_HD, nw), BF16),
                        pltpu.SMEM((bsz,), jnp.int32)],
    )
    return pl.pallas_call(
        _nsa_batch_kernel,
        grid_spec=grid_spec,
        out_shape=jax.ShapeDtypeStruct((bsz, t, N_WIDTH), BF16),
        compiler_params=_cparams(("arbitrary", "arbitrary")),
        name="nsa_attention",
    )(spieces, p, p, p, kcmp, vcmpt, ovt, ks, vs, kw, vw, tri, wb)


def _outproj_kernel(ym_ref, yn_ref, w_ref, x_ref, gate_ref, fg_ref, o_ref, wb_scr, *, final):
    @pl.when((pl.program_id(0) == 0) & (pl.program_id(1) == 0))
    def _():
        rows = w_ref.shape[1]
        for c in range(rows // WP_TILE):
            cs = slice(c * WP_TILE, (c + 1) * WP_TILE)
            wb_scr[cs, :] = w_ref[0, cs, :].astype(BF16)

    y = jnp.dot(ym_ref[0], wb_scr[0:M_WIDTH, :], preferred_element_type=F32)
    y = y + jnp.dot(yn_ref[0], wb_scr[M_WIDTH:, :], preferred_element_type=F32)
    hres = x_ref[0] + gate_ref[0] * y
    if final:
        ms = jnp.mean(hres * hres, axis=-1, keepdims=True)
        hres = hres * lax.rsqrt(ms + EPS) * fg_ref[...]
    o_ref[0] = hres


def _outproj(ym, yn, w, layer, x, gate, fg, final):
    bsz, t, d = x.shape
    tm = 512
    return pl.pallas_call(
        functools.partial(_outproj_kernel, final=final),
        grid=(bsz, t // tm),
        in_specs=[pl.BlockSpec((1, tm, M_WIDTH), lambda bi, i: (bi, i, 0)),
                  pl.BlockSpec((1, tm, N_WIDTH), lambda bi, i: (bi, i, 0)),
                  pl.BlockSpec((1, M_WIDTH + N_WIDTH, d), lambda bi, i: (layer, 0, 0),
                               pipeline_mode=pl.Buffered(1)),
                  pl.BlockSpec((1, tm, d), lambda bi, i: (bi, i, 0)),
                  pl.BlockSpec((1, 1, d), lambda bi, i: (bi, 0, 0)),
                  pl.BlockSpec((1, d), lambda bi, i: (0, 0))],
        out_specs=pl.BlockSpec((1, tm, d), lambda bi, i: (bi, i, 0)),
        out_shape=jax.ShapeDtypeStruct((bsz, t, d), F32),
        scratch_shapes=[pltpu.VMEM((M_WIDTH + N_WIDTH, d), BF16)],
        compiler_params=_cparams(("arbitrary", "arbitrary")),
        name="outproj_residual",
    )(ym, yn, w, x, gate, fg)


SRC_MI = 4 * M_WIDTH
SRC_NQ = SRC_MI + 2 * M_HEADS
SRC_NG = SRC_NQ + N_WIDTH + 6 * KV_W
SRC_NZ = SRC_NG + 3 * N_HEADS


def _reorder_cols(a):
    parts = [a[..., 0:SRC_MI], a[..., SRC_NQ:SRC_NG], a[..., SRC_NZ:SRC_NZ + N_WIDTH], a[..., SRC_MI:SRC_NQ],
             a[..., SRC_NG:SRC_NZ]]
    used = sum(x.shape[-1] for x in parts)
    parts.append(jnp.zeros(a.shape[:-1] + (NP_PAD - used,), a.dtype))
    return jnp.concatenate(parts, axis=-1)


WP_TILE = 512
WP_GATE_TILE = COL_GATES // WP_TILE


def _wprep_kernel(w_ref, g1_ref, g2_ref, o_ref):
    j = pl.program_id(0)
    d = o_ref.shape[0]

    @pl.when(j < WP_GATE_TILE)
    def _():
        for c in range(d // WP_TILE):
            cs = slice(c * WP_TILE, (c + 1) * WP_TILE)
            o_ref[cs, :] = w_ref[0, :, cs].T.astype(BF16)

    @pl.when(j == WP_GATE_TILE)
    def _():
        n_gate = g1_ref.shape[1] + g2_ref.shape[1]
        gt = jnp.concatenate([g1_ref[0], g2_ref[0], jnp.zeros((LANES - n_gate, d), F32)], axis=0)
        for c in range(d // WP_TILE):
            cs = slice(c * WP_TILE, (c + 1) * WP_TILE)
            o_ref[cs, 0:LANES] = gt[:, cs].T.astype(BF16)
        o_ref[:, LANES:] = jnp.zeros((d, WP_TILE - LANES), BF16)


def _wprep(w_t, layer):
    _, n, d = w_t.shape
    n_big = COL_GATES // WP_TILE
    assert n == SRC_NZ + N_WIDTH and NP_PAD // WP_TILE == n_big + 1

    def src_row(j):
        return jnp.where(j < COL_NQ // WP_TILE, j * WP_TILE,
                         jnp.where(j < COL_NZ // WP_TILE, SRC_NQ + (j - COL_NQ // WP_TILE) * WP_TILE,
                                   jnp.where(j < n_big, SRC_NZ + (j - COL_NZ // WP_TILE) * WP_TILE, 0)))

    el = pl.Element
    return pl.pallas_call(
        _wprep_kernel,
        grid=(NP_PAD // WP_TILE,),
        in_specs=[pl.BlockSpec((el(1), el(WP_TILE), el(d)), lambda j: (layer, pl.multiple_of(src_row(j), 8), 0)),
                  pl.BlockSpec((el(1), el(SRC_NQ - SRC_MI), el(d)), lambda j: (layer, SRC_MI, 0)),
                  pl.BlockSpec((el(1), el(SRC_NZ - SRC_NG), el(d)), lambda j: (layer, SRC_NG, 0))],
        out_specs=pl.BlockSpec((d, WP_TILE), lambda j: (0, j)),
        out_shape=jax.ShapeDtypeStruct((d, NP_PAD), BF16),
        compiler_params=_cparams(("parallel",)),
        name="inproj_weight_prep",
    )(w_t, w_t, w_t)


def _overlap_t(t):
    n_cmp_rows = t // CMP_STRIDE
    n_slc = t // SLC_LEN
    c0 = np.arange(n_cmp_rows) * CMP_STRIDE
    s0 = np.arange(n_slc) * SLC_LEN
    ov = (c0[None, :] <= s0[:, None] + SLC_LEN - 1) & (c0[None, :] + CMP_LEN - 1 >= s0[:, None])
    ov[:, (t - CMP_LEN) // CMP_STRIDE + 1:] = False
    return jnp.asarray(np.concatenate([ov] * 3, axis=1), BF16)


def kernel(x, c, ln_g, w_ada, b_ada, w_in, b_in, m_conv_w, m_conv_b, m_wq, m_wk, m_norm_w, m_skip, m_f_bias,
           n_pos_k, n_pos_v, n_w1_k, n_w2_k, n_w1_v, n_w2_v, w_out, final_g):
    out_dtype = x.dtype
    bsz, t, d = x.shape
    depth = ln_g.shape[0]
    h_res = x.astype(F32)
    assert bsz <= 2 and t % 1024 == 0 and t // CMP_STRIDE <= 256
    c_t =jnp.zeros((d, 8), F32).at[:, :bsz].set(c.astype(F32).T)
    slopes_np = np.array([2.0 ** (-8.0 * (h + 1) / N_HEADS) for h in range(N_HEADS)], np.float32)
    rest = (slopes_np.astype(np.float64) * LOG2E).astype(np.float32)
    pieces = []
    for _ in range(3):
        pieces.append(rest.astype(jnp.bfloat16).astype(np.float32))
        rest = rest - pieces[-1]
    spieces = jnp.asarray(np.stack(pieces, axis=1).reshape(-1))
    ovt = _overlap_t(t)

    def w1cat(w1):
        w = w1.reshape(2, CMP_STRIDE, N_HD, CMP_HIDDEN)
        w = jnp.concatenate([w[0], w[1]], axis=-1).astype(BF16)
        z = jnp.zeros_like(w)
        return jnp.concatenate([jnp.concatenate([w, z], axis=-1), jnp.concatenate([z, w], axis=-1)], axis=1)

    def w2pad(w2):
        return jnp.pad(w2, ((0, 0), (0, LANES - N_HD))).astype(BF16)

    for l in range(depth):
        mod = _ada(c_t, w_ada, b_ada[l][None, :], l, bsz)[:bsz]
        shift, scale, gate = mod[:, None, 0:d], mod[:, None, d:2 * d], mod[:, None, 2 * d:3 * d]
        p = _inproj(h_res, ln_g[l][None, :], scale, shift,
                    _wprep(jnp.swapaxes(w_in, 1, 2), l), _reorder_cols(b_in[l])[None, :])
        fb_row = jnp.zeros((1, LANES), F32).at[0, M_HEADS:2 * M_HEADS].set(m_f_bias[l])
        y_m = _mlstm(p, m_conv_w[l], m_conv_b[l][None, :], m_wq[l].astype(BF16), m_wk[l].astype(BF16),
                     m_norm_w[l][None, :], m_skip[l][None, :], fb_row)
        kcmp, vcmpt = _compress(p, n_pos_k[l].reshape(1, -1), n_pos_v[l].reshape(1, -1),
                                n_w1_k[l].astype(BF16), w1cat(n_w1_k[l]), w2pad(n_w2_k[l]),
                                n_w1_v[l].astype(BF16), w1cat(n_w1_v[l]), w2pad(n_w2_v[l]))
        ks, vs, kw, vw = _relayout(p)
        y_n = _nsa(p, spieces, kcmp, vcmpt, ovt, ks, vs, kw, vw)
        h_res = _outproj(y_m, y_n, w_out, l, h_res, gate, final_g[None, :], l == depth - 1)
    return h_res.astype(out_dtype)
```

```python
import functools

import numpy as np
import jax
import jax.numpy as jnp
from jax import lax
from jax.experimental import pallas as pl
from jax.experimental.pallas import tpu as pltpu

F32 = jnp.float32
BF16 = jnp.bfloat16

EPS = 1e-6
M_HEADS = 4
M_HD = 256
M_WIDTH = M_HEADS * M_HD
CONV_K = 4
M_CHUNK = 256
N_HEADS = 16
N_HD = 64
N_KV = 4
N_HPG = N_HEADS // N_KV
N_WIDTH = N_HEADS * N_HD
KV_W = N_KV * N_HD
CMP_LEN = 32
CMP_STRIDE = 16
CMP_HIDDEN = 2 * N_HD
SLC_LEN = 64
SLC_TOPN = 16
WIN = 512
QBLK = 256
SKT = 256
WKT = 256
PAD_TILES = WIN // SKT
SEL_GRP = 8
KA_SEL = 128
KA_WIN = 128
VROWS = 80
UNROLL = 4
AUX_ROWS = 16
LOG2E = 1.4426950408889634

COL_MX, COL_MV, COL_MO, COL_MZ = 0, 1024, 2048, 3072
COL_NQ = 4096
COL_KC, COL_VC, COL_KS, COL_VS, COL_KW, COL_VW = 5120, 5376, 5632, 5888, 6144, 6400
COL_NZ = 6656
COL_GATES = 7680
GATE_NG = 2 * M_HEADS
NP_PAD = 8192
NORM_CHUNKS = 8
LANES = 128
NEG = -1e30
VMEM_LIMIT = 58 * 1024 * 1024


def _cparams(sem):
    return pltpu.CompilerParams(dimension_semantics=sem, vmem_limit_bytes=VMEM_LIMIT)


def _silu(x):
    return x * jax.nn.sigmoid(x)


def _log_sigmoid(x):
    return jnp.minimum(x, 0.0) - jnp.log1p(jnp.exp(-jnp.abs(x)))


def _ada_kernel(ct_ref, w_ref, b_ref, o_ref, *, bsz):
    s_t = _silu(ct_ref[...])
    w = w_ref[0]
    row = lax.broadcasted_iota(jnp.int32, o_ref.shape, 0)
    out = jnp.zeros(o_ref.shape, F32)
    for b in range(bsz):
        prod = w * s_t[:, b:b + 1]
        acc = prod[0:8]
        for r in range(1, prod.shape[0] // 8):
            acc = acc + prod[8 * r:8 * r + 8]
        out = jnp.where(row == b, jnp.sum(acc, axis=0, keepdims=True) + b_ref[...], out)
    o_ref[...] = out


def _ada(c_t, w, b, layer, bsz):
    _, d, n = w.shape
    tn = 2048
    return pl.pallas_call(
        functools.partial(_ada_kernel, bsz=bsz),
        grid=(n // tn,),
        in_specs=[pl.BlockSpec((d, 8), lambda j: (0, 0)),
                  pl.BlockSpec((1, d, tn), lambda j: (layer, 0, j)),
                  pl.BlockSpec((1, tn), lambda j: (0, j))],
        out_specs=pl.BlockSpec((8, tn), lambda j: (0, j)),
        out_shape=jax.ShapeDtypeStruct((8, n), F32),
        compiler_params=_cparams(("parallel",)),
        name="ada_mod",
    )(c_t, w, b)


def _inproj_kernel(x_ref, g_ref, sc_ref, sh_ref, w_ref, b_ref, o_ref, h_ref):
    first = pl.program_id(2) == 0

    @pl.when(first)
    def _():
        tm = x_ref.shape[1]
        ck = tm // NORM_CHUNKS
        for c in range(NORM_CHUNKS):
            rows = slice(c * ck, (c + 1) * ck)
            x = x_ref[0, rows, :]
            ms = jnp.mean(x * x, axis=-1, keepdims=True)
            h = x * lax.rsqrt(ms + EPS) * g_ref[...]
            h = (h * (1.0 + sc_ref[0]) + sh_ref[0]).astype(BF16)
            h_ref[rows, :] = h
            o_ref[0, rows, :] = jnp.dot(h, w_ref[...], preferred_element_type=F32) + b_ref[...]

    @pl.when(jnp.logical_not(first))
    def _():
        o_ref[0] = jnp.dot(h_ref[...], w_ref[...], preferred_element_type=F32) + b_ref[...]


def _inproj(x, g, scale, shift, w, b):
    bsz, t, d = x.shape
    n = w.shape[1]
    tm, tn = 1024, 2048
    return pl.pallas_call(
        _inproj_kernel,
        grid=(bsz, t // tm, n // tn),
        in_specs=[pl.BlockSpec((1, tm, d), lambda bi, i, j: (bi, i, 0)),
                  pl.BlockSpec((1, d), lambda bi, i, j: (0, 0)),
                  pl.BlockSpec((1, 1, d), lambda bi, i, j: (bi, 0, 0)),
                  pl.BlockSpec((1, 1, d), lambda bi, i, j: (bi, 0, 0)),
                  pl.BlockSpec((d, tn), lambda bi, i, j: (0, j)),
                  pl.BlockSpec((1, tn), lambda bi, i, j: (0, j))],
        out_specs=pl.BlockSpec((1, tm, tn), lambda bi, i, j: (bi, i, j)),
        out_shape=jax.ShapeDtypeStruct((bsz, t, n), F32),
        scratch_shapes=[pltpu.VMEM((tm, d), BF16)],
        compiler_params=_cparams(("parallel", "parallel", "arbitrary")),
        name="norm_inproj",
    )(x, g, scale, shift, w, b)


def _mlstm_kernel(x_ref, v_ref, o_ref, z_ref, gt_ref, cw_ref, cb_ref, wq_ref, wk_ref, nw_ref, sk_ref, fb_ref,
                  y_ref, c_scr, n_scr, m_scr, xp_scr):
    L = M_CHUNK

    @pl.when(pl.program_id(1) == 0)
    def _():
        c_scr[...] = jnp.zeros_like(c_scr)
        n_scr[...] = jnp.zeros_like(n_scr)
        m_scr[...] = jnp.zeros_like(m_scr)
        xp_scr[...] = jnp.zeros_like(xp_scr)

    x = x_ref[0]
    prev = xp_scr[...]
    row8 = lax.broadcasted_iota(jnp.int32, (8, M_WIDTH), 0)
    cw = cw_ref[...]
    xc = cb_ref[...] + x * cw[CONV_K - 1:CONV_K, :]
    for sft in range(1, CONV_K):
        xr = pltpu.roll(x, sft, 0)
        top = jnp.where(row8 < sft, pltpu.roll(prev, sft, 0), xr[0:8])
        xs = jnp.concatenate([top, xr[8:]], axis=0)
        xc = xc + xs * cw[CONV_K - 1 - sft:CONV_K - sft, :]
    xp_scr[...] = x[L - 8:L]
    xc = _silu(xc)

    gt = gt_ref[0]
    col = lax.broadcasted_iota(jnp.int32, (L, LANES), 1)
    logf = _log_sigmoid(gt + fb_ref[...])
    a_c = jnp.where((col >= M_HEADS) & (col < 2 * M_HEADS), logf, gt)
    ri = lax.broadcasted_iota(jnp.int32, (L, L), 0)
    ci = lax.broadcasted_iota(jnp.int32, (L, L), 1)
    causal = ri >= ci
    hp = lax.Precision.HIGHEST
    tri = causal.astype(F32)
    tri_t = (ri <= ci).astype(F32)
    b_c = jnp.dot(tri, a_c, precision=hp, preferred_element_type=F32)
    a_r = a_c.T
    b_r = jnp.dot(a_r[0:8], tri_t, precision=hp, preferred_element_type=F32)

    for h in range(M_HEADS):
        sl = slice(h * M_HD, (h + 1) * M_HD)
        xh = xc[:, sl]
        xb = xh.astype(BF16)
        q = jnp.dot(xb, wq_ref[h], preferred_element_type=F32)
        k = jnp.dot(xb, wk_ref[h], preferred_element_type=F32) * (M_HD ** -0.5)
        vb = v_ref[0, :, sl].astype(BF16)
        qb = q.astype(BF16)
        kb = k.astype(BF16)

        bt = b_c[:, M_HEADS + h:M_HEADS + h + 1]
        ic = a_c[:, h:h + 1]
        bs = b_r[M_HEADS + h:M_HEADS + h + 1, :]
        ir = a_r[h:h + 1, :]
        m_prev = m_scr[h][:, 0:1]

        dm = jnp.where(causal, bt - bs + ir, -jnp.inf)
        inter = bt + m_prev
        m_t = jnp.maximum(inter, jnp.max(dm, axis=-1, keepdims=True))
        w_in = jnp.exp(dm - m_t)
        w_st = jnp.exp(inter - m_t)
        s = lax.dot_general(qb, kb, (((1,), (1,)), ((), ())), preferred_element_type=F32) * w_in
        cmat = c_scr[h]
        nvec = n_scr[h]
        sb = s.astype(BF16)
        num = w_st * jnp.dot(qb, cmat.astype(BF16), preferred_element_type=F32) \
            + jnp.dot(sb, vb, preferred_element_type=F32)
        nt_dims = (((1,), (1,)), ((), ()))
        qn = lax.dot_general(qb, jnp.broadcast_to(nvec, (8, M_HD)).astype(BF16), nt_dims,
                             preferred_element_type=F32)[:, 0:1]
        ssum = lax.dot_general(sb, jnp.ones((8, L), BF16), nt_dims, preferred_element_type=F32)[:, 0:1]
        den = w_st * qn + ssum
        hh = num / jnp.maximum(jnp.abs(den), jnp.exp(-m_t))

        b_last = bt[L - 1:L, :]
        w_end = b_last - bt + ic
        m_new = jnp.maximum(b_last + m_prev, jnp.max(w_end, axis=0, keepdims=True))
        decay = jnp.exp(b_last + m_prev - m_new)
        kwt = k * jnp.exp(w_end - m_new)
        c_scr[h] = decay * cmat + lax.dot_general(kwt.astype(BF16), vb, (((0,), (0,)), ((), ())),
                                                  preferred_element_type=F32)
        n_scr[h] = decay * nvec + jnp.sum(kwt, axis=0, keepdims=True)
        m_scr[h] = jnp.broadcast_to(m_new, (1, LANES))

        mu = jnp.mean(hh, axis=-1, keepdims=True)
        hc = hh - mu
        var = jnp.mean(hc * hc, axis=-1, keepdims=True)
        hn = hc * lax.rsqrt(var + EPS) * nw_ref[:, sl]
        out = jax.nn.sigmoid(o_ref[0, :, sl]) * hn + sk_ref[:, sl] * xh
        y_ref[0, :, sl] = (out * _silu(z_ref[0, :, sl])).astype(BF16)


def _mlstm(p, conv_w, conv_b, wq, wk, norm_w, skip, fb_row):
    bsz, t, _ = p.shape
    L = M_CHUNK
    cb = lambda c: (lambda bi, i: (bi, i, c))
    full2 = lambda bi, i: (0, 0)
    full3 = lambda bi, i: (0, 0, 0)
    return pl.pallas_call(
        _mlstm_kernel,
        grid=(bsz, t // L),
        in_specs=[pl.BlockSpec((1, L, M_WIDTH), cb(COL_MX // M_WIDTH)),
                  pl.BlockSpec((1, L, M_WIDTH), cb(COL_MV // M_WIDTH)),
                  pl.BlockSpec((1, L, M_WIDTH), cb(COL_MO // M_WIDTH)),
                  pl.BlockSpec((1, L, M_WIDTH), cb(COL_MZ // M_WIDTH)),
                  pl.BlockSpec((1, L, LANES), cb(COL_GATES // LANES)),
                  pl.BlockSpec((CONV_K, M_WIDTH), full2),
                  pl.BlockSpec((1, M_WIDTH), full2),
                  pl.BlockSpec((M_HEADS, M_HD, M_HD), full3),
                  pl.BlockSpec((M_HEADS, M_HD, M_HD), full3),
                  pl.BlockSpec((1, M_WIDTH), full2),
                  pl.BlockSpec((1, M_WIDTH), full2),
                  pl.BlockSpec((1, LANES), full2)],
        out_specs=pl.BlockSpec((1, L, M_WIDTH), lambda bi, i: (bi, i, 0)),
        out_shape=jax.ShapeDtypeStruct((bsz, t, M_WIDTH), BF16),
        scratch_shapes=[pltpu.VMEM((M_HEADS, M_HD, M_HD), F32),
                        pltpu.VMEM((M_HEADS, 1, M_HD), F32),
                        pltpu.VMEM((M_HEADS, 1, LANES), F32),
                        pltpu.VMEM((8, M_WIDTH), F32)],
        compiler_params=_cparams(("parallel", "arbitrary")),
        name="mlstm_group",
    )(p, p, p, p, p, conv_w, conv_b, wq, wk, norm_w, skip, fb_row)


def _compress_kernel(kc0_ref, kc1_ref, vc0_ref, vc1_ref, posk_ref, posv_ref, w1k_ref, w1kc_ref, w2k_ref,
                     w1v_ref, w1vc_ref, w2v_ref, kcmp_ref, vcmpt_ref):
    nb = kcmp_ref.shape[2]

    def hidden(src_refs, pos_ref, w1_ref, w1c_ref):
        halves = [jnp.zeros((nb, 4 * CMP_HIDDEN), F32) for _ in src_refs]
        for l in range(CMP_STRIDE):
            for hf, r in enumerate(src_refs):
                xl = r[0, pl.ds(l, nb, stride=CMP_STRIDE), :]
                halves[hf] = halves[hf] + jnp.dot(xl.astype(BF16), w1c_ref[l], preferred_element_type=F32)
        acc = jnp.concatenate([hv[:, c * 2 * CMP_HIDDEN:(c + 1) * 2 * CMP_HIDDEN] for hv in halves for c in range(2)],
                              axis=0)
        first = acc[:, :CMP_HIDDEN]
        second = acc[:, CMP_HIDDEN:]
        posb = jnp.dot(jnp.broadcast_to(pos_ref[...], (8, CMP_LEN * N_HD)).astype(BF16), w1_ref[...],
                       preferred_element_type=F32)[0:1]
        hid = first + pltpu.roll(second, N_KV * nb - 1, 0) + posb
        return jax.nn.gelu(hid).astype(BF16)

    hk = hidden((kc0_ref, kc1_ref), posk_ref, w1k_ref, w1kc_ref)
    kc = jnp.dot(hk, w2k_ref[...], preferred_element_type=F32)
    hv = hidden((vc0_ref, vc1_ref), posv_ref, w1v_ref, w1vc_ref)
    vc = jnp.dot(hv, w2v_ref[...], preferred_element_type=F32)
    j = lax.broadcasted_iota(jnp.int32, (nb, 1), 0)
    lane = lax.broadcasted_iota(jnp.int32, (nb, N_HD), 1)
    n_real = (nb * CMP_STRIDE - CMP_LEN) // CMP_STRIDE + 1
    aux = jnp.where(lane < 3, (j * CMP_STRIDE).astype(F32),
                    jnp.where(lane < 6, (CMP_LEN - 1) * 0.5,
                              jnp.where((lane == 6) & (j >= n_real), 1.0, 0.0))).astype(BF16)
    ones_blk = jnp.where(lax.broadcasted_iota(jnp.int32, (VROWS - N_HD, nb), 0) == 0, 1.0, 0.0)
    for g in range(N_KV):
        kg = kc[g * nb:(g + 1) * nb, 0:N_HD]
        k_hi = kg.astype(BF16)
        k_lo = (kg - k_hi.astype(F32)).astype(BF16)
        kcmp_ref[0, g] = jnp.concatenate([k_hi, k_lo, k_hi, aux], axis=1)
        vt = vc[g * nb:(g + 1) * nb].T
        vcmpt_ref[0, g] = jnp.concatenate([vt[0:N_HD], ones_blk], axis=0).astype(BF16)


def _compress(p, posk, posv, w1k, w1kc, w2k, w1v, w1vc, w2v):
    bsz, t, _ = p.shape
    nb = t // CMP_STRIDE
    cb = lambda c: (lambda bi: (bi, 0, c))
    f2 = lambda bi: (0, 0)
    f3 = lambda bi: (0, 0, 0)
    wspecs = [pl.BlockSpec((CMP_LEN * N_HD, CMP_HIDDEN), f2),
              pl.BlockSpec((CMP_STRIDE, LANES, 4 * CMP_HIDDEN), f3),
              pl.BlockSpec((CMP_HIDDEN, LANES), f2)]
    return pl.pallas_call(
        _compress_kernel,
        grid=(bsz,),
        in_specs=[pl.BlockSpec((1, t, LANES), cb(COL_KC // LANES)),
                  pl.BlockSpec((1, t, LANES), cb(COL_KC // LANES + 1)),
                  pl.BlockSpec((1, t, LANES), cb(COL_VC // LANES)),
                  pl.BlockSpec((1, t, LANES), cb(COL_VC // LANES + 1)),
                  pl.BlockSpec((1, CMP_LEN * N_HD), f2),
                  pl.BlockSpec((1, CMP_LEN * N_HD), f2)] + wspecs + wspecs,
        out_specs=[pl.BlockSpec((1, N_KV, nb, 4 * N_HD), lambda bi: (bi, 0, 0, 0)),
                   pl.BlockSpec((1, N_KV, VROWS, nb), lambda bi: (bi, 0, 0, 0))],
        out_shape=[jax.ShapeDtypeStruct((bsz, N_KV, nb, 4 * N_HD), BF16),
                   jax.ShapeDtypeStruct((bsz, N_KV, VROWS, nb), BF16)],
        compiler_params=_cparams(("parallel",)),
        name="nsa_compress",
    )(p, p, p, p, posk, posv, w1k, w1kc, w2k, w1v, w1vc, w2v)


def _relayout_kernel(ks_ref, vs_ref, kw_ref, vw_ref, ksa_ref, vsa_ref, kwa_ref, vwa_ref):
    i = pl.program_id(1)
    is_pad = i == 0
    flag = jnp.where(is_pad, 1.0, 0.0)
    row = lax.broadcasted_iota(jnp.int32, (SKT, 1), 0)
    lane = lax.broadcasted_iota(jnp.int32, (SKT, N_HD), 1)
    al = lane - AUX_ROWS
    ones_blk = jnp.where(lax.broadcasted_iota(jnp.int32, (VROWS - N_HD, SKT), 0) == 0, 1.0, 0.0)
    for u in range(PAD_TILES):
        rows = slice(u * SKT, (u + 1) * SKT)
        base = ((i - 1) * PAD_TILES + u) * SKT
        pos = jnp.where(is_pad, 0, base + row)
        blk = pos // SLC_LEN
        p_hi = (blk * SLC_LEN).astype(F32)
        p_lo = (pos - blk * SLC_LEN).astype(F32)
        mid_w = jnp.where((al >= 0) & (al < 3), p_hi,
                          jnp.where((al >= 3) & (al < 6), p_lo, jnp.where(al == 6, flag, 0.0)))
        mid_s = jnp.where(lane == blk % SEL_GRP, 1.0, mid_w)
        ks = ks_ref[0, rows, :]
        kw = kw_ref[0, rows, :]
        for g in range(N_KV):
            sl = slice(g * N_HD, (g + 1) * N_HD)
            ksa_ref[0, g, rows, :] = jnp.concatenate([ks[:, sl], mid_s], axis=1).astype(BF16)
            kwa_ref[0, g, rows, :] = jnp.concatenate([kw[:, sl], mid_w], axis=1).astype(BF16)
        vst = vs_ref[0, rows, :].T
        vwt = vw_ref[0, rows, :].T
        for g in range(N_KV):
            sl = slice(g * N_HD, (g + 1) * N_HD)
            vsa_ref[0, g, u] = jnp.concatenate([vst[sl], ones_blk], axis=0).astype(BF16)
            vwg = jnp.concatenate([vwt[sl], ones_blk], axis=0).astype(BF16)
            for j in range(SKT // WKT):
                vwa_ref[0, g, u * (SKT // WKT) + j] = vwg[:, j * WKT:(j + 1) * WKT]


def _relayout(p):
    bsz, t, _ = p.shape
    nt = t // SKT + PAD_TILES
    wpt = SKT // WKT
    rt = PAD_TILES
    cb = lambda c: (lambda bi, i: (bi, jnp.maximum(i - 1, 0), c))
    return pl.pallas_call(
        _relayout_kernel,
        grid=(bsz, nt // rt),
        in_specs=[pl.BlockSpec((1, rt * SKT, KV_W), cb(COL_KS // KV_W)),
                  pl.BlockSpec((1, rt * SKT, KV_W), cb(COL_VS // KV_W)),
                  pl.BlockSpec((1, rt * SKT, KV_W), cb(COL_KW // KV_W)),
                  pl.BlockSpec((1, rt * SKT, KV_W), cb(COL_VW // KV_W))],
        out_specs=[pl.BlockSpec((1, N_KV, rt * SKT, KA_SEL), lambda bi, i: (bi, 0, i, 0)),
                   pl.BlockSpec((1, N_KV, rt, VROWS, SKT), lambda bi, i: (bi, 0, i, 0, 0)),
                   pl.BlockSpec((1, N_KV, rt * SKT, KA_WIN), lambda bi, i: (bi, 0, i, 0)),
                   pl.BlockSpec((1, N_KV, rt * wpt, VROWS, WKT), lambda bi, i: (bi, 0, i, 0, 0))],
        out_shape=[jax.ShapeDtypeStruct((bsz, N_KV, nt * SKT, KA_SEL), BF16),
                   jax.ShapeDtypeStruct((bsz, N_KV, nt, VROWS, SKT), BF16),
                   jax.ShapeDtypeStruct((bsz, N_KV, nt * SKT, KA_WIN), BF16),
                   jax.ShapeDtypeStruct((bsz, N_KV, nt * wpt, VROWS, WKT), BF16)],
        compiler_params=_cparams(("parallel", "parallel")),
        name="nsa_relayout",
    )(p, p, p, p)


def _nsa_batch_kernel(sp_ref, q_ref, gt_ref, z_ref, kcmp_ref, vcmpt_ref, ovt_ref, ks_ref, vs_ref, kw_ref,
                      vw_ref, tri_ref, wb_ref, y_ref, sa_scr, sb_scr, g_scr, imp_scr, rank_scr, part_scr, selb_scr,
                      tile_idx):
    g = pl.program_id(0)
    qb = pl.program_id(1)
    n_items = q_ref.shape[0]
    start = qb * QBLK
    nb = kcmp_ref.shape[2]
    n_slc = ovt_ref.shape[0]
    nw = N_HPG * QBLK
    n_top = min(SLC_TOPN, n_slc)
    t_row = start + lax.broadcasted_iota(jnp.int32, (1, QBLK), 1)
    s_bufs = (sa_scr, sb_scr)
    last = qb // (SKT // QBLK)

    def tile4(a):
        return jnp.concatenate([a] * N_HPG, axis=1)

    def col_reduce8(s, op):
        out = s[0:8]
        for r in range(1, s.shape[0] // 8):
            out = op(out, s[8 * r:8 * r + 8])
        return out

    def col_max(s):
        return jnp.max(col_reduce8(s, jnp.maximum), axis=0, keepdims=True)

    def normalise(acc):
        den = acc[N_HD:N_HD + 1]
        return acc[0:N_HD] / jnp.where(den > 0, den, 1.0)

    def gate_row(i, h, branch):
        return g_scr[i, pl.ds(GATE_NG + (g * N_HPG + h) * 3 + branch, 1), :]

    hl = lax.broadcasted_iota(jnp.int32, (AUX_ROWS, nw), 1) // QBLK
    ar = lax.broadcasted_iota(jnp.int32, (AUX_ROWS, nw), 0)

    def per_head(vals):
        out = jnp.full((AUX_ROWS, nw), vals[N_HPG - 1], F32)
        for h in range(N_HPG - 2, -1, -1):
            out = jnp.where(hl == h, vals[h], out)
        return out

    pieces = [per_head([sp_ref[(g * N_HPG + h) * 3 + j] for h in range(N_HPG)]) for j in range(3)]
    aux = jnp.where(ar == 6, NEG, 0.0)
    for j in range(3):
        aux = jnp.where((ar == j) | (ar == j + 3), pieces[j], aux)
    aux_b = aux.astype(BF16)
    q_tail = jnp.zeros((KA_SEL - N_HD - 2 * AUX_ROWS, nw), BF16)
    sidx = lax.broadcasted_iota(jnp.int32, (n_slc, QBLK), 0)
    cur = t_row // SLC_LEN
    valid = sidx <= cur
    sub8 = lax.broadcasted_iota(jnp.int32, (8, QBLK), 0)
    last_blk = (start + QBLK - 1) // SLC_LEN
    cmp_end = lax.broadcasted_iota(jnp.int32, (nb, 1), 0) * CMP_STRIDE + (CMP_LEN - 1)
    cmp_bias = tile4(jnp.where(cmp_end <= t_row, 0.0, NEG))
    wrows = WIN + QBLK

    def prepare(i):
        qt = (q_ref[i] * (N_HD ** -0.5)).T
        q4l = jnp.concatenate([qt[h * N_HD:(h + 1) * N_HD] for h in range(N_HPG)], axis=1) * LOG2E
        q4s = q4l.astype(BF16)
        q4lo = (q4l - q4s.astype(F32)).astype(BF16)
        g_scr[i] = jax.nn.sigmoid(gt_ref[i]).T

        q_cmp = jnp.concatenate([q4s, q4s, q4lo, aux_b, jnp.zeros((N_HD - AUX_ROWS, nw), BF16)], axis=0)
        s_c = jnp.dot(kcmp_ref[i, 0], q_cmp, preferred_element_type=F32) + cmp_bias
        e_c = jnp.exp2(s_c - col_max(s_c))
        z_c = jnp.sum(col_reduce8(e_c, jnp.add), axis=0, keepdims=True)
        inv_c = jnp.where(tile4(t_row >= CMP_LEN - 1) & (z_c > 0), 1.0 / z_c, 0.0)
        o_c = jnp.dot(vcmpt_ref[i, 0], e_c.astype(BF16), preferred_element_type=F32)[0:N_HD] * inv_c
        p_c = e_c * inv_c
        psum = p_c[:, 0:QBLK]
        for h in range(1, N_HPG):
            psum = psum + p_c[:, h * QBLK:(h + 1) * QBLK]
        parts, rest = [], psum
        for _ in range(3):
            parts.append(rest.astype(BF16))
            rest = rest - parts[-1].astype(F32)
        imp = jnp.dot(ovt_ref[...], jnp.concatenate(parts, axis=0), preferred_element_type=F32)
        imp = jnp.where(valid, imp, -jnp.inf)
        imp_scr[...] = jnp.where((sidx == 0) | (sidx == cur), jnp.inf, imp)

        q_win = jnp.concatenate([q4s, jnp.zeros((AUX_ROWS, nw), BF16), aux_b,
                                 jnp.zeros((KA_WIN - N_HD - 2 * AUX_ROWS, nw), BF16)], axis=0)
        s_w = jnp.dot(kw_ref[i, 0, pl.ds(pl.multiple_of(start, QBLK), wrows), :], q_win,
                      preferred_element_type=F32)
        s_w = jnp.concatenate([s_w[0:WKT] + tile4(wb_ref[0]), s_w[WKT:wrows - WKT],
                               s_w[wrows - WKT:] + tile4(wb_ref[1])], axis=0)
        p_w = jnp.exp2(s_w - col_max(s_w)).astype(BF16)
        v_w = jnp.concatenate([vw_ref[i, 0, qb * (QBLK // WKT) + j] for j in range(wrows // WKT)], axis=1)
        o_w = normalise(jnp.dot(v_w, p_w, preferred_element_type=F32))
        for h in range(N_HPG):
            hs = slice(h * QBLK, (h + 1) * QBLK)
            part_scr[i, :, hs] = gate_row(i, h, 0) * o_c[:, hs] + gate_row(i, h, 2) * o_w[:, hs]

        rank_scr[...] = jnp.zeros_like(rank_scr)
        for ri in range(n_slc // 8):
            @pl.when((last_blk >= n_top) & (8 * ri <= last_blk))
            def _():
                rows = imp_scr[8 * ri:8 * ri + 8]
                for r in range(n_slc // 8):
                    blk8 = imp_scr[8 * r:8 * r + 8]
                    acc = rank_scr[8 * r:8 * r + 8]
                    for ii in range(8):
                        row = rows[ii:ii + 1]
                        if ri < r:
                            before = row >= blk8
                        elif ri > r:
                            before = row > blk8
                        else:
                            before = (row > blk8) | ((row == blk8) & (sub8 > ii))
                        acc = acc + jnp.where(before, 1.0, 0.0)
                    rank_scr[8 * r:8 * r + 8] = acc
        chosen = (rank_scr[...] < n_top) & valid
        selb_scr[i] = tile4(jnp.where(chosen, 0.0, NEG))

        any_q = jnp.max(jnp.where(chosen, 1.0, 0.0), axis=1, keepdims=True) > 0
        s_io = lax.broadcasted_iota(jnp.int32, (n_slc, 1), 0)
        bits = jnp.where(any_q, jnp.left_shift(1, s_io % 32), 0)
        words = [jnp.sum(jnp.where(s_io // 32 == w, bits, 0)) for w in range((n_slc + 31) // 32)]
        bpt = SKT // SLC_LEN
        cnt = jnp.int32(0)
        for tile in range(n_slc // bpt):
            nib = lax.shift_right_logical(words[tile * bpt // 32], jnp.int32(tile * bpt % 32)) & (2 ** bpt - 1)
            tile_idx[i, cnt] = tile
            cnt = cnt + ((nib != 0) & (tile < last)).astype(jnp.int32)
        return q4s, cnt

    q4, n_act = zip(*[prepare(i) for i in range(n_items)])


    def sel_scores(i, kt):
        grp = pl.multiple_of((kt * SKT // SLC_LEN) // SEL_GRP * SEL_GRP, SEL_GRP)
        sel_rows = jnp.concatenate([selb_scr[i, pl.ds(grp, SEL_GRP), :], jnp.zeros((AUX_ROWS - SEL_GRP, nw), F32)],
                                   axis=0)
        q_sel = jnp.concatenate([q4[i], sel_rows.astype(BF16), aux_b, q_tail], axis=0)
        krow = pl.multiple_of((kt + PAD_TILES) * SKT, SKT)
        return jnp.dot(ks_ref[i, 0, pl.ds(krow, SKT), :], q_sel, preferred_element_type=F32)

    def tile_rows(kt):
        return pl.ds(pl.multiple_of(kt * SKT, SKT), SKT)

    def pass1(i):
        def fn(kt, mrun):
            s = sel_scores(i, kt)
            s_bufs[i % 2][tile_rows(kt), :] = s
            return jnp.maximum(mrun, col_reduce8(s, jnp.maximum))
        return fn

    def pass2(i, m_sel):
        def fn(kt, acc):
            pr = jnp.exp2(s_bufs[i % 2][tile_rows(kt), :] - m_sel).astype(BF16)
            return acc + jnp.dot(vs_ref[i, 0, kt + PAD_TILES], pr, preferred_element_type=F32)
        return fn

    def listed(i, fn):
        return lambda j, carry: fn(tile_idx[i, j], carry)

    def both(f1, f2):
        def fn(j, carry):
            return f1(j, carry[0]), f2(j, carry[1])
        return fn

    def run_tiles(fn, lo, n, init):
        def group(size, first):
            def body(j, carry):
                for u in range(size):
                    carry = fn(first + j * size + u, carry)
                return carry
            return body

        n_u = n // UNROLL
        carry = lax.fori_loop(0, n_u, group(UNROLL, lo), init)
        done = lo + n_u * UNROLL
        size = UNROLL // 2
        while size >= 1:
            take = (n // size) % 2
            carry = lax.fori_loop(0, take, group(size, done), carry)
            done = done + take * size
            size //= 2
        return carry

    def diag_tile(i, mrun):
        s = sel_scores(i, last) + tile4(tri_ref[qb % (SKT // QBLK)])
        s_bufs[i % 2][tile_rows(last), :] = s
        return jnp.max(jnp.maximum(mrun, col_reduce8(s, jnp.maximum)), axis=0, keepdims=True)

    mrun0 = jnp.full((8, nw), NEG, F32)
    acc0 = jnp.zeros((VROWS, nw), F32)
    m_prev = diag_tile(0, run_tiles(listed(0, pass1(0)), 0, n_act[0], mrun0))
    outs = []
    for i in range(1, n_items):
        p2 = pass2(i - 1, m_prev)
        f1, f2 = listed(i, pass1(i)), listed(i - 1, p2)
        n_both = jnp.minimum(n_act[i], n_act[i - 1])
        mrun, acc = run_tiles(both(f1, f2), 0, n_both, (mrun0, acc0))
        mrun = run_tiles(f1, n_both, n_act[i] - n_both, mrun)
        acc = run_tiles(f2, n_both, n_act[i - 1] - n_both, acc)
        m_cur = diag_tile(i, mrun)
        outs.append(normalise(p2(last, acc)))
        m_prev = m_cur
    p2 = pass2(n_items - 1, m_prev)
    acc = run_tiles(listed(n_items - 1, p2), 0, n_act[n_items - 1], acc0)
    outs.append(normalise(p2(last, acc)))

    for i in range(n_items):
        tot = [part_scr[i, :, h * QBLK:(h + 1) * QBLK] + gate_row(i, h, 1) * outs[i][:, h * QBLK:(h + 1) * QBLK]
               for h in range(N_HPG)]
        o = jnp.concatenate(tot, axis=0).T
        y_ref[i] = (o * _silu(z_ref[i])).astype(BF16)


def _edge_biases():
    kl = np.arange(SKT)[:, None]
    ql = np.arange(QBLK)[None, :]
    tri = np.stack([np.where(kl <= par * QBLK + ql, 0.0, NEG) for par in range(SKT // QBLK)])
    kk = np.arange(WKT)[:, None]
    wb = np.stack([np.where(kk > ql, 0.0, NEG), np.where(kk <= ql + WKT - QBLK, 0.0, NEG)])
    return jnp.asarray(tri, F32), jnp.asarray(wb, F32)


def _nsa(p, spieces, kcmp, vcmpt, ovt, ks, vs, kw, vw):
    bsz, t, _ = p.shape
    n_slc = t // SLC_LEN
    gw = N_HPG * N_HD
    tri, wb = _edge_biases()
    nw = N_HPG * QBLK
    per_g = lambda a: pl.BlockSpec((bsz, 1) + a.shape[2:], lambda g, i, sp: (0, g) + (0,) * (a.ndim - 2))
    const = lambda a: pl.BlockSpec(a.shape, lambda g, i, sp: (0,) * a.ndim)
    grid_spec = pltpu.PrefetchScalarGridSpec(
        num_scalar_prefetch=1,
        grid=(N_KV, t // QBLK),
        in_specs=[pl.BlockSpec((bsz, QBLK, gw), lambda g, i, sp: (0, i, COL_NQ // gw + g)),
                  pl.BlockSpec((bsz, QBLK, LANES), lambda g, i, sp: (0, i, COL_GATES // LANES)),
                  pl.BlockSpec((bsz, QBLK, gw), lambda g, i, sp: (0, i, COL_NZ // gw + g)),
                  per_g(kcmp), per_g(vcmpt), const(ovt),
                  per_g(ks), per_g(vs), per_g(kw), per_g(vw), const(tri), const(wb)],
        out_specs=pl.BlockSpec((bsz, QBLK, gw), lambda g, i, sp: (0, i, g)),
        scratch_shapes=[pltpu.VMEM((t, nw), F32),
                        pltpu.VMEM((t, nw), F32),
                        pltpu.VMEM((bsz, LANES, QBLK), F32),
                        pltpu.VMEM((n_slc, QBLK), F32),
                        pltpu.VMEM((n_slc, QBLK), F32),
                        pltpu.VMEM((bsz, N_HD, nw), F32),
                        pltpu.VMEM((bsz, n_slc, nw), F32),
                        pltpu.SMEM((bsz, t // SKT), jnp.int32)],
    )
    return pl.pallas_call(
        _nsa_batch_kernel,
        grid_spec=grid_spec,
        out_shape=jax.ShapeDtypeStruct((bsz, t, N_WIDTH), BF16),
        compiler_params=_cparams(("arbitrary", "arbitrary")),
        name="nsa_attention",
    )(spieces, p, p, p, kcmp, vcmpt, ovt, ks, vs, kw, vw, tri, wb)


def _outproj_kernel(ym_ref, yn_ref, w_ref, x_ref, gate_ref, fg_ref, o_ref, wb_scr, *, final):
    @pl.when((pl.program_id(0) == 0) & (pl.program_id(1) == 0))
    def _():
        rows = w_ref.shape[1]
        for c in range(rows // WP_TILE):
            cs = slice(c * WP_TILE, (c + 1) * WP_TILE)
            wb_scr[cs, :] = w_ref[0, cs, :].astype(BF16)

    y = jnp.dot(ym_ref[0], wb_scr[0:M_WIDTH, :], preferred_element_type=F32)
    y = y + jnp.dot(yn_ref[0], wb_scr[M_WIDTH:, :], preferred_element_type=F32)
    hres = x_ref[0] + gate_ref[0] * y
    if final:
        ms = jnp.mean(hres * hres, axis=-1, keepdims=True)
        hres = hres * lax.rsqrt(ms + EPS) * fg_ref[...]
    o_ref[0] = hres


def _outproj(ym, yn, w, layer, x, gate, fg, final):
    bsz, t, d = x.shape
    tm = 512
    return pl.pallas_call(
        functools.partial(_outproj_kernel, final=final),
        grid=(bsz, t // tm),
        in_specs=[pl.BlockSpec((1, tm, M_WIDTH), lambda bi, i: (bi, i, 0)),
                  pl.BlockSpec((1, tm, N_WIDTH), lambda bi, i: (bi, i, 0)),
                  pl.BlockSpec((1, M_WIDTH + N_WIDTH, d), lambda bi, i: (layer, 0, 0),
                               pipeline_mode=pl.Buffered(1)),
                  pl.BlockSpec((1, tm, d), lambda bi, i: (bi, i, 0)),
                  pl.BlockSpec((1, 1, d), lambda bi, i: (bi, 0, 0)),
                  pl.BlockSpec((1, d), lambda bi, i: (0, 0))],
        out_specs=pl.BlockSpec((1, tm, d), lambda bi, i: (bi, i, 0)),
        out_shape=jax.ShapeDtypeStruct((bsz, t, d), F32),
        scratch_shapes=[pltpu.VMEM((M_WIDTH + N_WIDTH, d), BF16)],
        compiler_params=_cparams(("arbitrary", "arbitrary")),
        name="outproj_residual",
    )(ym, yn, w, x, gate, fg)


SRC_MI = 4 * M_WIDTH
SRC_NQ = SRC_MI + 2 * M_HEADS
SRC_NG = SRC_NQ + N_WIDTH + 6 * KV_W
SRC_NZ = SRC_NG + 3 * N_HEADS


def _reorder_cols(a):
    parts = [a[..., 0:SRC_MI], a[..., SRC_NQ:SRC_NG], a[..., SRC_NZ:SRC_NZ + N_WIDTH], a[..., SRC_MI:SRC_NQ],
             a[..., SRC_NG:SRC_NZ]]
    used = sum(x.shape[-1] for x in parts)
    parts.append(jnp.zeros(a.shape[:-1] + (NP_PAD - used,), a.dtype))
    return jnp.concatenate(parts, axis=-1)


WP_TILE = 512
WP_GATE_TILE = COL_GATES // WP_TILE


def _wprep_kernel(w_ref, g1_ref, g2_ref, o_ref):
    j = pl.program_id(0)
    d = o_ref.shape[0]

    @pl.when(j < WP_GATE_TILE)
    def _():
        for c in range(d // WP_TILE):
            cs = slice(c * WP_TILE, (c + 1) * WP_TILE)
            o_ref[cs, :] = w_ref[0, :, cs].T.astype(BF16)

    @pl.when(j == WP_GATE_TILE)
    def _():
        n_gate = g1_ref.shape[1] + g2_ref.shape[1]
        gt = jnp.concatenate([g1_ref[0], g2_ref[0], jnp.zeros((LANES - n_gate, d), F32)], axis=0)
        for c in range(d // WP_TILE):
            cs = slice(c * WP_TILE, (c + 1) * WP_TILE)
            o_ref[cs, 0:LANES] = gt[:, cs].T.astype(BF16)
        o_ref[:, LANES:] = jnp.zeros((d, WP_TILE - LANES), BF16)


def _wprep(w_t, layer):
    _, n, d = w_t.shape
    n_big = COL_GATES // WP_TILE
    assert n == SRC_NZ + N_WIDTH and NP_PAD // WP_TILE == n_big + 1

    def src_row(j):
        return jnp.where(j < COL_NQ // WP_TILE, j * WP_TILE,
                         jnp.where(j < COL_NZ // WP_TILE, SRC_NQ + (j - COL_NQ // WP_TILE) * WP_TILE,
                                   jnp.where(j < n_big, SRC_NZ + (j - COL_NZ // WP_TILE) * WP_TILE, 0)))

    el = pl.Element
    return pl.pallas_call(
        _wprep_kernel,
        grid=(NP_PAD // WP_TILE,),
        in_specs=[pl.BlockSpec((el(1), el(WP_TILE), el(d)), lambda j: (layer, pl.multiple_of(src_row(j), 8), 0)),
                  pl.BlockSpec((el(1), el(SRC_NQ - SRC_MI), el(d)), lambda j: (layer, SRC_MI, 0)),
                  pl.BlockSpec((el(1), el(SRC_NZ - SRC_NG), el(d)), lambda j: (layer, SRC_NG, 0))],
        out_specs=pl.BlockSpec((d, WP_TILE), lambda j: (0, j)),
        out_shape=jax.ShapeDtypeStruct((d, NP_PAD), BF16),
        compiler_params=_cparams(("parallel",)),
        name="inproj_weight_prep",
    )(w_t, w_t, w_t)


def _overlap_t(t):
    n_cmp_rows = t // CMP_STRIDE
    n_slc = t // SLC_LEN
    c0 = np.arange(n_cmp_rows) * CMP_STRIDE
    s0 = np.arange(n_slc) * SLC_LEN
    ov = (c0[None, :] <= s0[:, None] + SLC_LEN - 1) & (c0[None, :] + CMP_LEN - 1 >= s0[:, None])
    ov[:, (t - CMP_LEN) // CMP_STRIDE + 1:] = False
    return jnp.asarray(np.concatenate([ov] * 3, axis=1), BF16)


def kernel(x, c, ln_g, w_ada, b_ada, w_in, b_in, m_conv_w, m_conv_b, m_wq, m_wk, m_norm_w, m_skip, m_f_bias,
           n_pos_k, n_pos_v, n_w1_k, n_w2_k, n_w1_v, n_w2_v, w_out, final_g):
    out_dtype = x.dtype
    bsz, t, d = x.shape
    depth = ln_g.shape[0]
    h_res = x.astype(F32)
    assert bsz <= 2 and t % 1024 == 0 and t // CMP_STRIDE <= 256
    c_t =jnp.zeros((d, 8), F32).at[:, :bsz].set(c.astype(F32).T)
    slopes_np = np.array([2.0 ** (-8.0 * (h + 1) / N_HEADS) for h in range(N_HEADS)], np.float32)
    rest = (slopes_np.astype(np.float64) * LOG2E).astype(np.float32)
    pieces = []
    for _ in range(3):
        pieces.append(rest.astype(jnp.bfloat16).astype(np.float32))
        rest = rest - pieces[-1]
    spieces = jnp.asarray(np.stack(pieces, axis=1).reshape(-1))
    ovt = _overlap_t(t)

    def w1cat(w1):
        w = w1.reshape(2, CMP_STRIDE, N_HD, CMP_HIDDEN)
        w = jnp.concatenate([w[0], w[1]], axis=-1).astype(BF16)
        z = jnp.zeros_like(w)
        return jnp.concatenate([jnp.concatenate([w, z], axis=-1), jnp.concatenate([z, w], axis=-1)], axis=1)

    def w2pad(w2):
        return jnp.pad(w2, ((0, 0), (0, LANES - N_HD))).astype(BF16)

    for l in range(depth):
        mod = _ada(c_t, w_ada, b_ada[l][None, :], l, bsz)[:bsz]
        shift, scale, gate = mod[:, None, 0:d], mod[:, None, d:2 * d], mod[:, None, 2 * d:3 * d]
        p = _inproj(h_res, ln_g[l][None, :], scale, shift,
                    _wprep(jnp.swapaxes(w_in, 1, 2), l), _reorder_cols(b_in[l])[None, :])
        fb_row = jnp.zeros((1, LANES), F32).at[0, M_HEADS:2 * M_HEADS].set(m_f_bias[l])
        y_m = _mlstm(p, m_conv_w[l], m_conv_b[l][None, :], m_wq[l].astype(BF16), m_wk[l].astype(BF16),
                     m_norm_w[l][None, :], m_skip[l][None, :], fb_row)
        kcmp, vcmpt = _compress(p, n_pos_k[l].reshape(1, -1), n_pos_v[l].reshape(1, -1),
                                n_w1_k[l].astype(BF16), w1cat(n_w1_k[l]), w2pad(n_w2_k[l]),
                                n_w1_v[l].astype(BF16), w1cat(n_w1_v[l]), w2pad(n_w2_v[l]))
        ks, vs, kw, vw = _relayout(p)
        y_n = _nsa(p, spieces, kcmp, vcmpt, ovt, ks, vs, kw, vw)
        h_res = _outproj(y_m, y_n, w_out, l, h_res, gate, final_g[None, :], l == depth - 1)
    return h_res.astype(out_dtype)
```

```python
import functools

import numpy as np
import jax
import jax.numpy as jnp
from jax import lax
from jax.experimental import pallas as pl
from jax.experimental.pallas import tpu as pltpu

F32 = jnp.float32
BF16 = jnp.bfloat16

EPS = 1e-6
M_HEADS = 4
M_HD = 256
M_WIDTH = M_HEADS * M_HD
CONV_K = 4
M_CHUNK = 256
N_HEADS = 16
N_HD = 64
N_KV = 4
N_HPG = N_HEADS // N_KV
N_WIDTH = N_HEADS * N_HD
KV_W = N_KV * N_HD
CMP_LEN = 32
CMP_STRIDE = 16
CMP_HIDDEN = 2 * N_HD
SLC_LEN = 64
SLC_TOPN = 16
WIN = 512
QBLK = 256
SKT = 256
WKT = 256
PAD_TILES = WIN // SKT
SEL_GRP = 8
KA_SEL = 128
KA_WIN = 128
VROWS = 80
UNROLL = 4
AUX_ROWS = 16
LOG2E = 1.4426950408889634

COL_MX, COL_MV, COL_MO, COL_MZ = 0, 1024, 2048, 3072
COL_NQ = 4096
COL_KC, COL_VC, COL_KS, COL_VS, COL_KW, COL_VW = 5120, 5376, 5632, 5888, 6144, 6400
COL_NZ = 6656
COL_GATES = 7680
GATE_NG = 2 * M_HEADS
NP_PAD = 8192
NORM_CHUNKS = 2
LANES = 128
NEG = -1e30
VMEM_LIMIT = 58 * 1024 * 1024


def _cparams(sem):
    return pltpu.CompilerParams(dimension_semantics=sem, vmem_limit_bytes=VMEM_LIMIT)


def _silu(x):
    return x * jax.nn.sigmoid(x)


def _log_sigmoid(x):
    return jnp.minimum(x, 0.0) - jnp.log1p(jnp.exp(-jnp.abs(x)))


def _ada_kernel(ct_ref, w_ref, b_ref, o_ref, *, bsz):
    s_t = _silu(ct_ref[...])
    w = w_ref[0]
    row = lax.broadcasted_iota(jnp.int32, o_ref.shape, 0)
    out = jnp.zeros(o_ref.shape, F32)
    for b in range(bsz):
        prod = w * s_t[:, b:b + 1]
        acc = prod[0:8]
        for r in range(1, prod.shape[0] // 8):
            acc = acc + prod[8 * r:8 * r + 8]
        out = jnp.where(row == b, jnp.sum(acc, axis=0, keepdims=True) + b_ref[...], out)
    o_ref[...] = out


def _ada(c_t, w, b, layer, bsz):
    _, d, n = w.shape
    tn = 1024
    return pl.pallas_call(
        functools.partial(_ada_kernel, bsz=bsz),
        grid=(n // tn,),
        in_specs=[pl.BlockSpec((d, 8), lambda j: (0, 0)),
                  pl.BlockSpec((1, d, tn), lambda j: (layer, 0, j)),
                  pl.BlockSpec((1, tn), lambda j: (0, j))],
        out_specs=pl.BlockSpec((8, tn), lambda j: (0, j)),
        out_shape=jax.ShapeDtypeStruct((8, n), F32),
        compiler_params=_cparams(("parallel",)),
        name="ada_mod",
    )(c_t, w, b)


def _inproj_kernel(x_ref, g_ref, sc_ref, sh_ref, w_ref, b_ref, o_ref, h_ref):
    first = pl.program_id(2) == 0

    @pl.when(first)
    def _():
        tm = x_ref.shape[1]
        ck = tm // NORM_CHUNKS
        for c in range(NORM_CHUNKS):
            rows = slice(c * ck, (c + 1) * ck)
            x = x_ref[0, rows, :]
            ms = jnp.mean(x * x, axis=-1, keepdims=True)
            h = x * lax.rsqrt(ms + EPS) * g_ref[...]
            h = (h * (1.0 + sc_ref[0]) + sh_ref[0]).astype(BF16)
            h_ref[rows, :] = h
            o_ref[0, rows, :] = jnp.dot(h, w_ref[...], preferred_element_type=F32) + b_ref[...]

    @pl.when(jnp.logical_not(first))
    def _():
        o_ref[0] = jnp.dot(h_ref[...], w_ref[...], preferred_element_type=F32) + b_ref[...]


def _inproj(x, g, scale, shift, w, b):
    bsz, t, d = x.shape
    n = w.shape[1]
    tm, tn = 1024, 2048
    return pl.pallas_call(
        _inproj_kernel,
        grid=(bsz, t // tm, n // tn),
        in_specs=[pl.BlockSpec((1, tm, d), lambda bi, i, j: (bi, i, 0)),
                  pl.BlockSpec((1, d), lambda bi, i, j: (0, 0)),
                  pl.BlockSpec((1, 1, d), lambda bi, i, j: (bi, 0, 0)),
                  pl.BlockSpec((1, 1, d), lambda bi, i, j: (bi, 0, 0)),
                  pl.BlockSpec((d, tn), lambda bi, i, j: (0, j)),
                  pl.BlockSpec((1, tn), lambda bi, i, j: (0, j))],
        out_specs=pl.BlockSpec((1, tm, tn), lambda bi, i, j: (bi, i, j)),
        out_shape=jax.ShapeDtypeStruct((bsz, t, n), F32),
        scratch_shapes=[pltpu.VMEM((tm, d), BF16)],
        compiler_params=_cparams(("parallel", "parallel", "arbitrary")),
        name="norm_inproj",
    )(x, g, scale, shift, w, b)


def _mlstm_kernel(x_ref, v_ref, o_ref, z_ref, gt_ref, cw_ref, cb_ref, wq_ref, wk_ref, nw_ref, sk_ref, fb_ref,
                  y_ref, c_scr, n_scr, m_scr, xp_scr):
    L = M_CHUNK

    @pl.when(pl.program_id(1) == 0)
    def _():
        c_scr[...] = jnp.zeros_like(c_scr)
        n_scr[...] = jnp.zeros_like(n_scr)
        m_scr[...] = jnp.zeros_like(m_scr)
        xp_scr[...] = jnp.zeros_like(xp_scr)

    x = x_ref[0]
    prev = xp_scr[...]
    row8 = lax.broadcasted_iota(jnp.int32, (8, M_WIDTH), 0)
    cw = cw_ref[...]
    xc = cb_ref[...] + x * cw[CONV_K - 1:CONV_K, :]
    for sft in range(1, CONV_K):
        xr = pltpu.roll(x, sft, 0)
        top = jnp.where(row8 < sft, pltpu.roll(prev, sft, 0), xr[0:8])
        xs = jnp.concatenate([top, xr[8:]], axis=0)
        xc = xc + xs * cw[CONV_K - 1 - sft:CONV_K - sft, :]
    xp_scr[...] = x[L - 8:L]
    xc = _silu(xc)

    gt = gt_ref[0]
    col = lax.broadcasted_iota(jnp.int32, (L, LANES), 1)
    logf = _log_sigmoid(gt + fb_ref[...])
    a_c = jnp.where((col >= M_HEADS) & (col < 2 * M_HEADS), logf, gt)
    ri = lax.broadcasted_iota(jnp.int32, (L, L), 0)
    ci = lax.broadcasted_iota(jnp.int32, (L, L), 1)
    causal = ri >= ci
    hp = lax.Precision.HIGHEST
    tri = causal.astype(F32)
    tri_t = (ri <= ci).astype(F32)
    b_c = jnp.dot(tri, a_c, precision=hp, preferred_element_type=F32)
    a_r = a_c.T
    b_r = jnp.dot(a_r[0:8], tri_t, precision=hp, preferred_element_type=F32)

    for h in range(M_HEADS):
        sl = slice(h * M_HD, (h + 1) * M_HD)
        xh = xc[:, sl]
        xb = xh.astype(BF16)
        q = jnp.dot(xb, wq_ref[h], preferred_element_type=F32)
        k = jnp.dot(xb, wk_ref[h], preferred_element_type=F32) * (M_HD ** -0.5)
        vb = v_ref[0, :, sl].astype(BF16)
        qb = q.astype(BF16)
        kb = k.astype(BF16)

        bt = b_c[:, M_HEADS + h:M_HEADS + h + 1]
        ic = a_c[:, h:h + 1]
        bs = b_r[M_HEADS + h:M_HEADS + h + 1, :]
        ir = a_r[h:h + 1, :]
        m_prev = m_scr[h][:, 0:1]

        dm = jnp.where(causal, bt - bs + ir, -jnp.inf)
        inter = bt + m_prev
        m_t = jnp.maximum(inter, jnp.max(dm, axis=-1, keepdims=True))
        w_in = jnp.exp(dm - m_t)
        w_st = jnp.exp(inter - m_t)
        s = lax.dot_general(qb, kb, (((1,), (1,)), ((), ())), preferred_element_type=F32) * w_in
        cmat = c_scr[h]
        nvec = n_scr[h]
        sb = s.astype(BF16)
        num = w_st * jnp.dot(qb, cmat.astype(BF16), preferred_element_type=F32) \
            + jnp.dot(sb, vb, preferred_element_type=F32)
        nt_dims = (((1,), (1,)), ((), ()))
        qn = lax.dot_general(qb, jnp.broadcast_to(nvec, (8, M_HD)).astype(BF16), nt_dims,
                             preferred_element_type=F32)[:, 0:1]
        ssum = lax.dot_general(sb, jnp.ones((8, L), BF16), nt_dims, preferred_element_type=F32)[:, 0:1]
        den = w_st * qn + ssum
        hh = num / jnp.maximum(jnp.abs(den), jnp.exp(-m_t))

        b_last = bt[L - 1:L, :]
        w_end = b_last - bt + ic
        m_new = jnp.maximum(b_last + m_prev, jnp.max(w_end, axis=0, keepdims=True))
        decay = jnp.exp(b_last + m_prev - m_new)
        kwt = k * jnp.exp(w_end - m_new)
        c_scr[h] = decay * cmat + lax.dot_general(kwt.astype(BF16), vb, (((0,), (0,)), ((), ())),
                                                  preferred_element_type=F32)
        n_scr[h] = decay * nvec + jnp.sum(kwt, axis=0, keepdims=True)
        m_scr[h] = jnp.broadcast_to(m_new, (1, LANES))

        mu = jnp.mean(hh, axis=-1, keepdims=True)
        hc = hh - mu
        var = jnp.mean(hc * hc, axis=-1, keepdims=True)
        hn = hc * lax.rsqrt(var + EPS) * nw_ref[:, sl]
        out = jax.nn.sigmoid(o_ref[0, :, sl]) * hn + sk_ref[:, sl] * xh
        y_ref[0, :, sl] = (out * _silu(z_ref[0, :, sl])).astype(BF16)


def _mlstm(p, conv_w, conv_b, wq, wk, norm_w, skip, fb_row):
    bsz, t, _ = p.shape
    L = M_CHUNK
    cb = lambda c: (lambda bi, i: (bi, i, c))
    full2 = lambda bi, i: (0, 0)
    full3 = lambda bi, i: (0, 0, 0)
    return pl.pallas_call(
        _mlstm_kernel,
        grid=(bsz, t // L),
        in_specs=[pl.BlockSpec((1, L, M_WIDTH), cb(COL_MX // M_WIDTH)),
                  pl.BlockSpec((1, L, M_WIDTH), cb(COL_MV // M_WIDTH)),
                  pl.BlockSpec((1, L, M_WIDTH), cb(COL_MO // M_WIDTH)),
                  pl.BlockSpec((1, L, M_WIDTH), cb(COL_MZ // M_WIDTH)),
                  pl.BlockSpec((1, L, LANES), cb(COL_GATES // LANES)),
                  pl.BlockSpec((CONV_K, M_WIDTH), full2),
                  pl.BlockSpec((1, M_WIDTH), full2),
                  pl.BlockSpec((M_HEADS, M_HD, M_HD), full3),
                  pl.BlockSpec((M_HEADS, M_HD, M_HD), full3),
                  pl.BlockSpec((1, M_WIDTH), full2),
                  pl.BlockSpec((1, M_WIDTH), full2),
                  pl.BlockSpec((1, LANES), full2)],
        out_specs=pl.BlockSpec((1, L, M_WIDTH), lambda bi, i: (bi, i, 0)),
        out_shape=jax.ShapeDtypeStruct((bsz, t, M_WIDTH), BF16),
        scratch_shapes=[pltpu.VMEM((M_HEADS, M_HD, M_HD), F32),
                        pltpu.VMEM((M_HEADS, 1, M_HD), F32),
                        pltpu.VMEM((M_HEADS, 1, LANES), F32),
                        pltpu.VMEM((8, M_WIDTH), F32)],
        compiler_params=_cparams(("parallel", "arbitrary")),
        name="mlstm_group",
    )(p, p, p, p, p, conv_w, conv_b, wq, wk, norm_w, skip, fb_row)


def _compress_kernel(kc0_ref, kc1_ref, vc0_ref, vc1_ref, posk_ref, posv_ref, w1k_ref, w1kc_ref, w2k_ref,
                     w1v_ref, w1vc_ref, w2v_ref, kcmp_ref, vcmpt_ref):
    nb = kcmp_ref.shape[2]

    def hidden(src_refs, pos_ref, w1_ref, w1c_ref):
        halves = [jnp.zeros((nb, 4 * CMP_HIDDEN), F32) for _ in src_refs]
        for l in range(CMP_STRIDE):
            for hf, r in enumerate(src_refs):
                xl = r[0, pl.ds(l, nb, stride=CMP_STRIDE), :]
                halves[hf] = halves[hf] + jnp.dot(xl.astype(BF16), w1c_ref[l], preferred_element_type=F32)
        acc = jnp.concatenate([hv[:, c * 2 * CMP_HIDDEN:(c + 1) * 2 * CMP_HIDDEN] for hv in halves for c in range(2)],
                              axis=0)
        first = acc[:, :CMP_HIDDEN]
        second = acc[:, CMP_HIDDEN:]
        posb = jnp.dot(jnp.broadcast_to(pos_ref[...], (8, CMP_LEN * N_HD)).astype(BF16), w1_ref[...],
                       preferred_element_type=F32)[0:1]
        hid = first + pltpu.roll(second, N_KV * nb - 1, 0) + posb
        return jax.nn.gelu(hid).astype(BF16)

    hk = hidden((kc0_ref, kc1_ref), posk_ref, w1k_ref, w1kc_ref)
    kc = jnp.dot(hk, w2k_ref[...], preferred_element_type=F32)
    hv = hidden((vc0_ref, vc1_ref), posv_ref, w1v_ref, w1vc_ref)
    vc = jnp.dot(hv, w2v_ref[...], preferred_element_type=F32)
    j = lax.broadcasted_iota(jnp.int32, (nb, 1), 0)
    lane = lax.broadcasted_iota(jnp.int32, (nb, N_HD), 1)
    n_real = (nb * CMP_STRIDE - CMP_LEN) // CMP_STRIDE + 1
    aux = jnp.where(lane < 3, (j * CMP_STRIDE).astype(F32),
                    jnp.where(lane < 6, (CMP_LEN - 1) * 0.5,
                              jnp.where((lane == 6) & (j >= n_real), 1.0, 0.0))).astype(BF16)
    ones_blk = jnp.where(lax.broadcasted_iota(jnp.int32, (VROWS - N_HD, nb), 0) == 0, 1.0, 0.0)
    for g in range(N_KV):
        kg = kc[g * nb:(g + 1) * nb, 0:N_HD]
        k_hi = kg.astype(BF16)
        k_lo = (kg - k_hi.astype(F32)).astype(BF16)
        kcmp_ref[0, g] = jnp.concatenate([k_hi, k_lo, k_hi, aux], axis=1)
        vt = vc[g * nb:(g + 1) * nb].T
        vcmpt_ref[0, g] = jnp.concatenate([vt[0:N_HD], ones_blk], axis=0).astype(BF16)


def _compress(p, posk, posv, w1k, w1kc, w2k, w1v, w1vc, w2v):
    bsz, t, _ = p.shape
    nb = t // CMP_STRIDE
    cb = lambda c: (lambda bi: (bi, 0, c))
    f2 = lambda bi: (0, 0)
    f3 = lambda bi: (0, 0, 0)
    wspecs = [pl.BlockSpec((CMP_LEN * N_HD, CMP_HIDDEN), f2),
              pl.BlockSpec((CMP_STRIDE, LANES, 4 * CMP_HIDDEN), f3),
              pl.BlockSpec((CMP_HIDDEN, LANES), f2)]
    return pl.pallas_call(
        _compress_kernel,
        grid=(bsz,),
        in_specs=[pl.BlockSpec((1, t, LANES), cb(COL_KC // LANES)),
                  pl.BlockSpec((1, t, LANES), cb(COL_KC // LANES + 1)),
                  pl.BlockSpec((1, t, LANES), cb(COL_VC // LANES)),
                  pl.BlockSpec((1, t, LANES), cb(COL_VC // LANES + 1)),
                  pl.BlockSpec((1, CMP_LEN * N_HD), f2),
                  pl.BlockSpec((1, CMP_LEN * N_HD), f2)] + wspecs + wspecs,
        out_specs=[pl.BlockSpec((1, N_KV, nb, 4 * N_HD), lambda bi: (bi, 0, 0, 0)),
                   pl.BlockSpec((1, N_KV, VROWS, nb), lambda bi: (bi, 0, 0, 0))],
        out_shape=[jax.ShapeDtypeStruct((bsz, N_KV, nb, 4 * N_HD), BF16),
                   jax.ShapeDtypeStruct((bsz, N_KV, VROWS, nb), BF16)],
        compiler_params=_cparams(("parallel",)),
        name="nsa_compress",
    )(p, p, p, p, posk, posv, w1k, w1kc, w2k, w1v, w1vc, w2v)


def _relayout_kernel(ks_ref, vs_ref, kw_ref, vw_ref, ksa_ref, vsa_ref, kwa_ref, vwa_ref):
    i = pl.program_id(1)
    is_pad = i == 0
    flag = jnp.where(is_pad, 1.0, 0.0)
    row = lax.broadcasted_iota(jnp.int32, (SKT, 1), 0)
    lane = lax.broadcasted_iota(jnp.int32, (SKT, N_HD), 1)
    al = lane - AUX_ROWS
    ones_blk = jnp.where(lax.broadcasted_iota(jnp.int32, (VROWS - N_HD, SKT), 0) == 0, 1.0, 0.0)
    for u in range(PAD_TILES):
        rows = slice(u * SKT, (u + 1) * SKT)
        base = ((i - 1) * PAD_TILES + u) * SKT
        pos = jnp.where(is_pad, 0, base + row)
        blk = pos // SLC_LEN
        p_hi = (blk * SLC_LEN).astype(F32)
        p_lo = (pos - blk * SLC_LEN).astype(F32)
        mid_w = jnp.where((al >= 0) & (al < 3), p_hi,
                          jnp.where((al >= 3) & (al < 6), p_lo, jnp.where(al == 6, flag, 0.0)))
        mid_s = jnp.where(lane == blk % SEL_GRP, 1.0, mid_w)
        ks = ks_ref[0, rows, :]
        kw = kw_ref[0, rows, :]
        for g in range(N_KV):
            sl = slice(g * N_HD, (g + 1) * N_HD)
            ksa_ref[0, g, rows, :] = jnp.concatenate([ks[:, sl], mid_s], axis=1).astype(BF16)
            kwa_ref[0, g, rows, :] = jnp.concatenate([kw[:, sl], mid_w], axis=1).astype(BF16)
        vst = vs_ref[0, rows, :].T
        vwt = vw_ref[0, rows, :].T
        for g in range(N_KV):
            sl = slice(g * N_HD, (g + 1) * N_HD)
            vsa_ref[0, g, u] = jnp.concatenate([vst[sl], ones_blk], axis=0).astype(BF16)
            vwg = jnp.concatenate([vwt[sl], ones_blk], axis=0).astype(BF16)
            for j in range(SKT // WKT):
                vwa_ref[0, g, u * (SKT // WKT) + j] = vwg[:, j * WKT:(j + 1) * WKT]


def _relayout(p):
    bsz, t, _ = p.shape
    nt = t // SKT + PAD_TILES
    wpt = SKT // WKT
    rt = PAD_TILES
    cb = lambda c: (lambda bi, i: (bi, jnp.maximum(i - 1, 0), c))
    return pl.pallas_call(
        _relayout_kernel,
        grid=(bsz, nt // rt),
        in_specs=[pl.BlockSpec((1, rt * SKT, KV_W), cb(COL_KS // KV_W)),
                  pl.BlockSpec((1, rt * SKT, KV_W), cb(COL_VS // KV_W)),
                  pl.BlockSpec((1, rt * SKT, KV_W), cb(COL_KW // KV_W)),
                  pl.BlockSpec((1, rt * SKT, KV_W), cb(COL_VW // KV_W))],
        out_specs=[pl.BlockSpec((1, N_KV, rt * SKT, KA_SEL), lambda bi, i: (bi, 0, i, 0)),
                   pl.BlockSpec((1, N_KV, rt, VROWS, SKT), lambda bi, i: (bi, 0, i, 0, 0)),
                   pl.BlockSpec((1, N_KV, rt * SKT, KA_WIN), lambda bi, i: (bi, 0, i, 0)),
                   pl.BlockSpec((1, N_KV, rt * wpt, VROWS, WKT), lambda bi, i: (bi, 0, i, 0, 0))],
        out_shape=[jax.ShapeDtypeStruct((bsz, N_KV, nt * SKT, KA_SEL), BF16),
                   jax.ShapeDtypeStruct((bsz, N_KV, nt, VROWS, SKT), BF16),
                   jax.ShapeDtypeStruct((bsz, N_KV, nt * SKT, KA_WIN), BF16),
                   jax.ShapeDtypeStruct((bsz, N_KV, nt * wpt, VROWS, WKT), BF16)],
        compiler_params=_cparams(("parallel", "parallel")),
        name="nsa_relayout",
    )(p, p, p, p)


def _nsa_batch_kernel(sp_ref, q_ref, gt_ref, z_ref, kcmp_ref, vcmpt_ref, ovt_ref, ks_ref, vs_ref, kw_ref,
                      vw_ref, tri_ref, wb_ref, y_ref, sa_scr, sb_scr, g_scr, imp_scr, rank_scr, part_scr, selb_scr,
                      tile_idx):
    g = pl.program_id(0)
    qb = pl.program_id(1)
    n_items = q_ref.shape[0]
    start = qb * QBLK
    nb = kcmp_ref.shape[2]
    n_slc = ovt_ref.shape[0]
    nw = N_HPG * QBLK
    n_top = min(SLC_TOPN, n_slc)
    t_row = start + lax.broadcasted_iota(jnp.int32, (1, QBLK), 1)
    s_bufs = (sa_scr, sb_scr)
    last = qb // (SKT // QBLK)

    def tile4(a):
        return jnp.concatenate([a] * N_HPG, axis=1)

    def col_reduce8(s, op):
        out = s[0:8]
        for r in range(1, s.shape[0] // 8):
            out = op(out, s[8 * r:8 * r + 8])
        return out

    def col_max(s):
        return jnp.max(col_reduce8(s, jnp.maximum), axis=0, keepdims=True)

    def normalise(acc):
        den = acc[N_HD:N_HD + 1]
        return acc[0:N_HD] / jnp.where(den > 0, den, 1.0)

    def gate_row(i, h, branch):
        return g_scr[i, pl.ds(GATE_NG + (g * N_HPG + h) * 3 + branch, 1), :]

    hl = lax.broadcasted_iota(jnp.int32, (AUX_ROWS, nw), 1) // QBLK
    ar = lax.broadcasted_iota(jnp.int32, (AUX_ROWS, nw), 0)

    def per_head(vals):
        out = jnp.full((AUX_ROWS, nw), vals[N_HPG - 1], F32)
        for h in range(N_HPG - 2, -1, -1):
            out = jnp.where(hl == h, vals[h], out)
        return out

    pieces = [per_head([sp_ref[(g * N_HPG + h) * 3 + j] for h in range(N_HPG)]) for j in range(3)]
    aux = jnp.where(ar == 6, NEG, 0.0)
    for j in range(3):
        aux = jnp.where((ar == j) | (ar == j + 3), pieces[j], aux)
    aux_b = aux.astype(BF16)
    q_tail = jnp.zeros((KA_SEL - N_HD - 2 * AUX_ROWS, nw), BF16)
    sidx = lax.broadcasted_iota(jnp.int32, (n_slc, QBLK), 0)
    cur = t_row // SLC_LEN
    valid = sidx <= cur
    sub8 = lax.broadcasted_iota(jnp.int32, (8, QBLK), 0)
    last_blk = (start + QBLK - 1) // SLC_LEN
    cmp_end = lax.broadcasted_iota(jnp.int32, (nb, 1), 0) * CMP_STRIDE + (CMP_LEN - 1)
    cmp_bias = tile4(jnp.where(cmp_end <= t_row, 0.0, NEG))
    wrows = WIN + QBLK

    def prepare(i):
        qt = (q_ref[i] * (N_HD ** -0.5)).T
        q4l = jnp.concatenate([qt[h * N_HD:(h + 1) * N_HD] for h in range(N_HPG)], axis=1) * LOG2E
        q4s = q4l.astype(BF16)
        q4lo = (q4l - q4s.astype(F32)).astype(BF16)
        g_scr[i] = jax.nn.sigmoid(gt_ref[i]).T

        q_cmp = jnp.concatenate([q4s, q4s, q4lo, aux_b, jnp.zeros((N_HD - AUX_ROWS, nw), BF16)], axis=0)
        s_c = jnp.dot(kcmp_ref[i, 0], q_cmp, preferred_element_type=F32) + cmp_bias
        e_c = jnp.exp2(s_c - col_max(s_c))
        z_c = jnp.sum(col_reduce8(e_c, jnp.add), axis=0, keepdims=True)
        inv_c = jnp.where(tile4(t_row >= CMP_LEN - 1) & (z_c > 0), 1.0 / z_c, 0.0)
        o_c = jnp.dot(vcmpt_ref[i, 0], e_c.astype(BF16), preferred_element_type=F32)[0:N_HD] * inv_c
        p_c = e_c * inv_c
        psum = p_c[:, 0:QBLK]
        for h in range(1, N_HPG):
            psum = psum + p_c[:, h * QBLK:(h + 1) * QBLK]
        parts, rest = [], psum
        for _ in range(3):
            parts.append(rest.astype(BF16))
            rest = rest - parts[-1].astype(F32)
        imp = jnp.dot(ovt_ref[...], jnp.concatenate(parts, axis=0), preferred_element_type=F32)
        imp = jnp.where(valid, imp, -jnp.inf)
        imp_scr[...] = jnp.where((sidx == 0) | (sidx == cur), jnp.inf, imp)

        q_win = jnp.concatenate([q4s, jnp.zeros((AUX_ROWS, nw), BF16), aux_b,
                                 jnp.zeros((KA_WIN - N_HD - 2 * AUX_ROWS, nw), BF16)], axis=0)
        s_w = jnp.dot(kw_ref[i, 0, pl.ds(pl.multiple_of(start, QBLK), wrows), :], q_win,
                      preferred_element_type=F32)
        s_w = jnp.concatenate([s_w[0:WKT] + tile4(wb_ref[0]), s_w[WKT:wrows - WKT],
                               s_w[wrows - WKT:] + tile4(wb_ref[1])], axis=0)
        p_w = jnp.exp2(s_w - col_max(s_w)).astype(BF16)
        v_w = jnp.concatenate([vw_ref[i, 0, qb * (QBLK // WKT) + j] for j in range(wrows // WKT)], axis=1)
        o_w = normalise(jnp.dot(v_w, p_w, preferred_element_type=F32))
        for h in range(N_HPG):
            hs = slice(h * QBLK, (h + 1) * QBLK)
            part_scr[i, :, hs] = gate_row(i, h, 0) * o_c[:, hs] + gate_row(i, h, 2) * o_w[:, hs]

        rank_scr[...] = jnp.zeros_like(rank_scr)
        for ri in range(n_slc // 8):
            @pl.when((last_blk >= n_top) & (8 * ri <= last_blk))
            def _():
                rows = imp_scr[8 * ri:8 * ri + 8]
                for r in range(n_slc // 8):
                    blk8 = imp_scr[8 * r:8 * r + 8]
                    acc = rank_scr[8 * r:8 * r + 8]
                    for ii in range(8):
                        row = rows[ii:ii + 1]
                        if ri < r:
                            before = row >= blk8
                        elif ri > r:
                            before = row > blk8
                        else:
                            before = (row > blk8) | ((row == blk8) & (sub8 > ii))
                        acc = acc + jnp.where(before, 1.0, 0.0)
                    rank_scr[8 * r:8 * r + 8] = acc
        chosen = (rank_scr[...] < n_top) & valid
        selb_scr[i] = tile4(jnp.where(chosen, 0.0, NEG))

        any_q = jnp.max(jnp.where(chosen, 1.0, 0.0), axis=1, keepdims=True) > 0
        s_io = lax.broadcasted_iota(jnp.int32, (n_slc, 1), 0)
        bits = jnp.where(any_q, jnp.left_shift(1, s_io % 32), 0)
        words = [jnp.sum(jnp.where(s_io // 32 == w, bits, 0)) for w in range((n_slc + 31) // 32)]
        bpt = SKT // SLC_LEN
        cnt = jnp.int32(0)
        for tile in range(n_slc // bpt):
            nib = lax.shift_right_logical(words[tile * bpt // 32], jnp.int32(tile * bpt % 32)) & (2 ** bpt - 1)
            tile_idx[i, cnt] = tile
            cnt = cnt + ((nib != 0) & (tile < last)).astype(jnp.int32)
        return q4s, cnt

    q4, n_act = zip(*[prepare(i) for i in range(n_items)])


    def sel_scores(i, kt):
        grp = pl.multiple_of((kt * SKT // SLC_LEN) // SEL_GRP * SEL_GRP, SEL_GRP)
        sel_rows = jnp.concatenate([selb_scr[i, pl.ds(grp, SEL_GRP), :], jnp.zeros((AUX_ROWS - SEL_GRP, nw), F32)],
                                   axis=0)
        q_sel = jnp.concatenate([q4[i], sel_rows.astype(BF16), aux_b, q_tail], axis=0)
        krow = pl.multiple_of((kt + PAD_TILES) * SKT, SKT)
        return jnp.dot(ks_ref[i, 0, pl.ds(krow, SKT), :], q_sel, preferred_element_type=F32)

    def tile_rows(kt):
        return pl.ds(pl.multiple_of(kt * SKT, SKT), SKT)

    def pass1(i):
        def fn(kt, mrun):
            s = sel_scores(i, kt)
            s_bufs[i % 2][tile_rows(kt), :] = s
            return jnp.maximum(mrun, col_reduce8(s, jnp.maximum))
        return fn

    def pass2(i, m_sel):
        def fn(kt, acc):
            pr = jnp.exp2(s_bufs[i % 2][tile_rows(kt), :] - m_sel).astype(BF16)
            return acc + jnp.dot(vs_ref[i, 0, kt + PAD_TILES], pr, preferred_element_type=F32)
        return fn

    def listed(i, fn):
        return lambda j, carry: fn(tile_idx[i, j], carry)

    def both(f1, f2):
        def fn(j, carry):
            return f1(j, carry[0]), f2(j, carry[1])
        return fn

    def run_tiles(fn, lo, n, init):
        def group(size, first):
            def body(j, carry):
                for u in range(size):
                    carry = fn(first + j * size + u, carry)
                return carry
            return body

        n_u = n // UNROLL
        carry = lax.fori_loop(0, n_u, group(UNROLL, lo), init)
        done = lo + n_u * UNROLL
        size = UNROLL // 2
        while size >= 1:
            take = (n // size) % 2
            carry = lax.fori_loop(0, take, group(size, done), carry)
            done = done + take * size
            size //= 2
        return carry

    def diag_tile(i, mrun):
        s = sel_scores(i, last) + tile4(tri_ref[qb % (SKT // QBLK)])
        s_bufs[i % 2][tile_rows(last), :] = s
        return jnp.max(jnp.maximum(mrun, col_reduce8(s, jnp.maximum)), axis=0, keepdims=True)

    mrun0 = jnp.full((8, nw), NEG, F32)
    acc0 = jnp.zeros((VROWS, nw), F32)
    m_prev = diag_tile(0, run_tiles(listed(0, pass1(0)), 0, n_act[0], mrun0))
    outs = []
    for i in range(1, n_items):
        p2 = pass2(i - 1, m_prev)
        f1, f2 = listed(i, pass1(i)), listed(i - 1, p2)
        n_both = jnp.minimum(n_act[i], n_act[i - 1])
        mrun, acc = run_tiles(both(f1, f2), 0, n_both, (mrun0, acc0))
        mrun = run_tiles(f1, n_both, n_act[i] - n_both, mrun)
        acc = run_tiles(f2, n_both, n_act[i - 1] - n_both, acc)
        m_cur = diag_tile(i, mrun)
        outs.append(normalise(p2(last, acc)))
        m_prev = m_cur
    p2 = pass2(n_items - 1, m_prev)
    acc = run_tiles(listed(n_items - 1, p2), 0, n_act[n_items - 1], acc0)
    outs.append(normalise(p2(last, acc)))

    for i in range(n_items):
        tot = [part_scr[i, :, h * QBLK:(h + 1) * QBLK] + gate_row(i, h, 1) * outs[i][:, h * QBLK:(h + 1) * QBLK]
               for h in range(N_HPG)]
        o = jnp.concatenate(tot, axis=0).T
        y_ref[i] = (o * _silu(z_ref[i])).astype(BF16)


def _edge_biases():
    kl = np.arange(SKT)[:, None]
    ql = np.arange(QBLK)[None, :]
    tri = np.stack([np.where(kl <= par * QBLK + ql, 0.0, NEG) for par in range(SKT // QBLK)])
    kk = np.arange(WKT)[:, None]
    wb = np.stack([np.where(kk > ql, 0.0, NEG), np.where(kk <= ql + WKT - QBLK, 0.0, NEG)])
    return jnp.asarray(tri, F32), jnp.asarray(wb, F32)


def _nsa(p, spieces, kcmp, vcmpt, ovt, ks, vs, kw, vw):
    bsz, t, _ = p.shape
    n_slc = t // SLC_LEN
    gw = N_HPG * N_HD
    tri, wb = _edge_biases()
    nw = N_HPG * QBLK
    per_g = lambda a: pl.BlockSpec((bsz, 1) + a.shape[2:], lambda g, i, sp: (0, g) + (0,) * (a.ndim - 2))
    const = lambda a: pl.BlockSpec(a.shape, lambda g, i, sp: (0,) * a.ndim)
    grid_spec = pltpu.PrefetchScalarGridSpec(
        num_scalar_prefetch=1,
        grid=(N_KV, t // QBLK),
        in_specs=[pl.BlockSpec((bsz, QBLK, gw), lambda g, i, sp: (0, i, COL_NQ // gw + g)),
                  pl.BlockSpec((bsz, QBLK, LANES), lambda g, i, sp: (0, i, COL_GATES // LANES)),
                  pl.BlockSpec((bsz, QBLK, gw), lambda g, i, sp: (0, i, COL_NZ // gw + g)),
                  per_g(kcmp), per_g(vcmpt), const(ovt),
                  per_g(ks), per_g(vs), per_g(kw), per_g(vw), const(tri), const(wb)],
        out_specs=pl.BlockSpec((bsz, QBLK, gw), lambda g, i, sp: (0, i, g)),
        scratch_shapes=[pltpu.VMEM((t, nw), F32),
                        pltpu.VMEM((t, nw), F32),
                        pltpu.VMEM((bsz, LANES, QBLK), F32),
                        pltpu.VMEM((n_slc, QBLK), F32),
                        pltpu.VMEM((n_slc, QBLK), F32),
                        pltpu.VMEM((bsz, N_HD, nw), F32),
                        pltpu.VMEM((bsz, n_slc, nw), F32),
                        pltpu.SMEM((bsz, t // SKT), jnp.int32)],
    )
    return pl.pallas_call(
        _nsa_batch_kernel,
        grid_spec=grid_spec,
        out_shape=jax.ShapeDtypeStruct((bsz, t, N_WIDTH), BF16),
        compiler_params=_cparams(("arbitrary", "arbitrary")),
        name="nsa_attention",
    )(spieces, p, p, p, kcmp, vcmpt, ovt, ks, vs, kw, vw, tri, wb)


def _outproj_kernel(ym_ref, yn_ref, w_ref, x_ref, gate_ref, fg_ref, o_ref, wb_scr, *, final):
    @pl.when((pl.program_id(0) == 0) & (pl.program_id(1) == 0))
    def _():
        rows = w_ref.shape[1]
        for c in range(rows // WP_TILE):
            cs = slice(c * WP_TILE, (c + 1) * WP_TILE)
            wb_scr[cs, :] = w_ref[0, cs, :].astype(BF16)

    y = jnp.dot(ym_ref[0], wb_scr[0:M_WIDTH, :], preferred_element_type=F32)
    y = y + jnp.dot(yn_ref[0], wb_scr[M_WIDTH:, :], preferred_element_type=F32)
    hres = x_ref[0] + gate_ref[0] * y
    if final:
        ms = jnp.mean(hres * hres, axis=-1, keepdims=True)
        hres = hres * lax.rsqrt(ms + EPS) * fg_ref[...]
    o_ref[0] = hres


def _outproj(ym, yn, w, layer, x, gate, fg, final):
    bsz, t, d = x.shape
    tm = 512
    return pl.pallas_call(
        functools.partial(_outproj_kernel, final=final),
        grid=(bsz, t // tm),
        in_specs=[pl.BlockSpec((1, tm, M_WIDTH), lambda bi, i: (bi, i, 0)),
                  pl.BlockSpec((1, tm, N_WIDTH), lambda bi, i: (bi, i, 0)),
                  pl.BlockSpec((1, M_WIDTH + N_WIDTH, d), lambda bi, i: (layer, 0, 0),
                               pipeline_mode=pl.Buffered(1)),
                  pl.BlockSpec((1, tm, d), lambda bi, i: (bi, i, 0)),
                  pl.BlockSpec((1, 1, d), lambda bi, i: (bi, 0, 0)),
                  pl.BlockSpec((1, d), lambda bi, i: (0, 0))],
        out_specs=pl.BlockSpec((1, tm, d), lambda bi, i: (bi, i, 0)),
        out_shape=jax.ShapeDtypeStruct((bsz, t, d), F32),
        scratch_shapes=[pltpu.VMEM((M_WIDTH + N_WIDTH, d), BF16)],
        compiler_params=_cparams(("arbitrary", "arbitrary")),
        name="outproj_residual",
    )(ym, yn, w, x, gate, fg)


SRC_MI = 4 * M_WIDTH
SRC_NQ = SRC_MI + 2 * M_HEADS
SRC_NG = SRC_NQ + N_WIDTH + 6 * KV_W
SRC_NZ = SRC_NG + 3 * N_HEADS


def _reorder_cols(a):
    parts = [a[..., 0:SRC_MI], a[..., SRC_NQ:SRC_NG], a[..., SRC_NZ:SRC_NZ + N_WIDTH], a[..., SRC_MI:SRC_NQ],
             a[..., SRC_NG:SRC_NZ]]
    used = sum(x.shape[-1] for x in parts)
    parts.append(jnp.zeros(a.shape[:-1] + (NP_PAD - used,), a.dtype))
    return jnp.concatenate(parts, axis=-1)


WP_TILE = 512
WP_GATE_TILE = COL_GATES // WP_TILE


def _wprep_kernel(w_ref, g1_ref, g2_ref, o_ref):
    j = pl.program_id(0)
    d = o_ref.shape[0]

    @pl.when(j < WP_GATE_TILE)
    def _():
        for c in range(d // WP_TILE):
            cs = slice(c * WP_TILE, (c + 1) * WP_TILE)
            o_ref[cs, :] = w_ref[0, :, cs].T.astype(BF16)

    @pl.when(j == WP_GATE_TILE)
    def _():
        n_gate = g1_ref.shape[1] + g2_ref.shape[1]
        gt = jnp.concatenate([g1_ref[0], g2_ref[0], jnp.zeros((LANES - n_gate, d), F32)], axis=0)
        for c in range(d // WP_TILE):
            cs = slice(c * WP_TILE, (c + 1) * WP_TILE)
            o_ref[cs, 0:LANES] = gt[:, cs].T.astype(BF16)
        o_ref[:, LANES:] = jnp.zeros((d, WP_TILE - LANES), BF16)


def _wprep(w_t, layer):
    _, n, d = w_t.shape
    n_big = COL_GATES // WP_TILE
    assert n == SRC_NZ + N_WIDTH and NP_PAD // WP_TILE == n_big + 1

    def src_row(j):
        return jnp.where(j < COL_NQ // WP_TILE, j * WP_TILE,
                         jnp.where(j < COL_NZ // WP_TILE, SRC_NQ + (j - COL_NQ // WP_TILE) * WP_TILE,
                                   jnp.where(j < n_big, SRC_NZ + (j - COL_NZ // WP_TILE) * WP_TILE, 0)))

    el = pl.Element
    return pl.pallas_call(
        _wprep_kernel,
        grid=(NP_PAD // WP_TILE,),
        in_specs=[pl.BlockSpec((el(1), el(WP_TILE), el(d)), lambda j: (layer, pl.multiple_of(src_row(j), 8), 0)),
                  pl.BlockSpec((el(1), el(SRC_NQ - SRC_MI), el(d)), lambda j: (layer, SRC_MI, 0)),
                  pl.BlockSpec((el(1), el(SRC_NZ - SRC_NG), el(d)), lambda j: (layer, SRC_NG, 0))],
        out_specs=pl.BlockSpec((d, WP_TILE), lambda j: (0, j)),
        out_shape=jax.ShapeDtypeStruct((d, NP_PAD), BF16),
        compiler_params=_cparams(("parallel",)),
        name="inproj_weight_prep",
    )(w_t, w_t, w_t)


def _overlap_t(t):
    n_cmp_rows = t // CMP_STRIDE
    n_slc = t // SLC_LEN
    c0 = np.arange(n_cmp_rows) * CMP_STRIDE
    s0 = np.arange(n_slc) * SLC_LEN
    ov = (c0[None, :] <= s0[:, None] + SLC_LEN - 1) & (c0[None, :] + CMP_LEN - 1 >= s0[:, None])
    ov[:, (t - CMP_LEN) // CMP_STRIDE + 1:] = False
    return jnp.asarray(np.concatenate([ov] * 3, axis=1), BF16)


def kernel(x, c, ln_g, w_ada, b_ada, w_in, b_in, m_conv_w, m_conv_b, m_wq, m_wk, m_norm_w, m_skip, m_f_bias,
           n_pos_k, n_pos_v, n_w1_k, n_w2_k, n_w1_v, n_w2_v, w_out, final_g):
    out_dtype = x.dtype
    bsz, t, d = x.shape
    depth = ln_g.shape[0]
    h_res = x.astype(F32)
    assert bsz <= 2 and t % 1024 == 0 and t // CMP_STRIDE <= 256
    c_t =jnp.zeros((d, 8), F32).at[:, :bsz].set(c.astype(F32).T)
    slopes_np = np.array([2.0 ** (-8.0 * (h + 1) / N_HEADS) for h in range(N_HEADS)], np.float32)
    rest = (slopes_np.astype(np.float64) * LOG2E).astype(np.float32)
    pieces = []
    for _ in range(3):
        pieces.append(rest.astype(jnp.bfloat16).astype(np.float32))
        rest = rest - pieces[-1]
    spieces = jnp.asarray(np.stack(pieces, axis=1).reshape(-1))
    ovt = _overlap_t(t)

    def w1cat(w1):
        w = w1.reshape(2, CMP_STRIDE, N_HD, CMP_HIDDEN)
        w = jnp.concatenate([w[0], w[1]], axis=-1).astype(BF16)
        z = jnp.zeros_like(w)
        return jnp.concatenate([jnp.concatenate([w, z], axis=-1), jnp.concatenate([z, w], axis=-1)], axis=1)

    def w2pad(w2):
        return jnp.pad(w2, ((0, 0), (0, LANES - N_HD))).astype(BF16)

    for l in range(depth):
        mod = _ada(c_t, w_ada, b_ada[l][None, :], l, bsz)[:bsz]
        shift, scale, gate = mod[:, None, 0:d], mod[:, None, d:2 * d], mod[:, None, 2 * d:3 * d]
        p = _inproj(h_res, ln_g[l][None, :], scale, shift,
                    _wprep(jnp.swapaxes(w_in, 1, 2), l), _reorder_cols(b_in[l])[None, :])
        fb_row = jnp.zeros((1, LANES), F32).at[0, M_HEADS:2 * M_HEADS].set(m_f_bias[l])
        y_m = _mlstm(p, m_conv_w[l], m_conv_b[l][None, :], m_wq[l].astype(BF16), m_wk[l].astype(BF16),
                     m_norm_w[l][None, :], m_skip[l][None, :], fb_row)
        kcmp, vcmpt = _compress(p, n_pos_k[l].reshape(1, -1), n_pos_v[l].reshape(1, -1),
                                n_w1_k[l].astype(BF16), w1cat(n_w1_k[l]), w2pad(n_w2_k[l]),
                                n_w1_v[l].astype(BF16), w1cat(n_w1_v[l]), w2pad(n_w2_v[l]))
        ks, vs, kw, vw = _relayout(p)
        y_n = _nsa(p, spieces, kcmp, vcmpt, ovt, ks, vs, kw, vw)
        h_res = _outproj(y_m, y_n, w_out, l, h_res, gate, final_g[None, :], l == depth - 1)
    return h_res.astype(out_dtype)
```

```python
import functools

import numpy as np
import jax
import jax.numpy as jnp
from jax import lax
from jax.experimental import pallas as pl
from jax.experimental.pallas import tpu as pltpu

F32 = jnp.float32
BF16 = jnp.bfloat16

EPS = 1e-6
M_HEADS = 4
M_HD = 256
M_WIDTH = M_HEADS * M_HD
CONV_K = 4
M_CHUNK = 256
N_HEADS = 16
N_HD = 64
N_KV = 4
N_HPG = N_HEADS // N_KV
N_WIDTH = N_HEADS * N_HD
KV_W = N_KV * N_HD
CMP_LEN = 32
CMP_STRIDE = 16
CMP_HIDDEN = 2 * N_HD
SLC_LEN = 64
SLC_TOPN = 16
WIN = 512
QBLK = 256
SKT = 256
WKT = 256
PAD_TILES = WIN // SKT
SEL_GRP = 8
KA_SEL = 128
KA_WIN = 128
VROWS = 80
UNROLL = 4
AUX_ROWS = 16
LOG2E = 1.4426950408889634

COL_MX, COL_MV, COL_MO, COL_MZ = 0, 1024, 2048, 3072
COL_NQ = 4096
COL_KC, COL_VC, COL_KS, COL_VS, COL_KW, COL_VW = 5120, 5376, 5632, 5888, 6144, 6400
COL_NZ = 6656
COL_GATES = 7680
GATE_NG = 2 * M_HEADS
NP_PAD = 8192
NORM_CHUNKS = 2
LANES = 128
NEG = -1e30
VMEM_LIMIT = 58 * 1024 * 1024


def _cparams(sem):
    return pltpu.CompilerParams(dimension_semantics=sem, vmem_limit_bytes=VMEM_LIMIT)


def _silu(x):
    return x * jax.nn.sigmoid(x)


def _log_sigmoid(x):
    return jnp.minimum(x, 0.0) - jnp.log1p(jnp.exp(-jnp.abs(x)))


def _ada_kernel(ct_ref, w_ref, b_ref, o_ref, *, bsz):
    row = lax.broadcasted_iota(jnp.int32, o_ref.shape, 0)

    @pl.when(pl.program_id(0) == 0)
    def _():
        o_ref[...] = jnp.where(row < bsz, b_ref[...], 0.0)

    s_t = _silu(ct_ref[...])
    w = w_ref[0]
    part = jnp.zeros(o_ref.shape, F32)
    for b in range(bsz):
        prod = w * s_t[:, b:b + 1]
        acc = prod[0:8]
        for r in range(1, prod.shape[0] // 8):
            acc = acc + prod[8 * r:8 * r + 8]
        part = jnp.where(row == b, jnp.sum(acc, axis=0, keepdims=True), part)
    o_ref[...] += part


def _ada(c_t, w, b, layer, bsz):
    _, d, n = w.shape
    tr = 256
    return pl.pallas_call(
        functools.partial(_ada_kernel, bsz=bsz),
        grid=(d // tr,),
        in_specs=[pl.BlockSpec((tr, 8), lambda k: (k, 0)),
                  pl.BlockSpec((1, tr, n), lambda k: (layer, k, 0)),
                  pl.BlockSpec((1, n), lambda k: (0, 0))],
        out_specs=pl.BlockSpec((8, n), lambda k: (0, 0)),
        out_shape=jax.ShapeDtypeStruct((8, n), F32),
        compiler_params=_cparams(("arbitrary",)),
        name="ada_mod",
    )(c_t, w, b)


def _inproj_kernel(x_ref, g_ref, sc_ref, sh_ref, w_ref, b_ref, o_ref, h_ref):
    first = pl.program_id(2) == 0

    @pl.when(first)
    def _():
        tm = x_ref.shape[1]
        ck = tm // NORM_CHUNKS
        for c in range(NORM_CHUNKS):
            rows = slice(c * ck, (c + 1) * ck)
            x = x_ref[0, rows, :]
            ms = jnp.mean(x * x, axis=-1, keepdims=True)
            h = x * lax.rsqrt(ms + EPS) * g_ref[...]
            h = (h * (1.0 + sc_ref[0]) + sh_ref[0]).astype(BF16)
            h_ref[rows, :] = h
            o_ref[0, rows, :] = jnp.dot(h, w_ref[...], preferred_element_type=F32) + b_ref[...]

    @pl.when(jnp.logical_not(first))
    def _():
        o_ref[0] = jnp.dot(h_ref[...], w_ref[...], preferred_element_type=F32) + b_ref[...]


def _inproj(x, g, scale, shift, w, b):
    bsz, t, d = x.shape
    n = w.shape[1]
    tm, tn = 1024, 2048
    return pl.pallas_call(
        _inproj_kernel,
        grid=(bsz, t // tm, n // tn),
        in_specs=[pl.BlockSpec((1, tm, d), lambda bi, i, j: (bi, i, 0)),
                  pl.BlockSpec((1, d), lambda bi, i, j: (0, 0)),
                  pl.BlockSpec((1, 1, d), lambda bi, i, j: (bi, 0, 0)),
                  pl.BlockSpec((1, 1, d), lambda bi, i, j: (bi, 0, 0)),
                  pl.BlockSpec((d, tn), lambda bi, i, j: (0, j)),
                  pl.BlockSpec((1, tn), lambda bi, i, j: (0, j))],
        out_specs=pl.BlockSpec((1, tm, tn), lambda bi, i, j: (bi, i, j)),
        out_shape=jax.ShapeDtypeStruct((bsz, t, n), F32),
        scratch_shapes=[pltpu.VMEM((tm, d), BF16)],
        compiler_params=_cparams(("parallel", "parallel", "arbitrary")),
        name="norm_inproj",
    )(x, g, scale, shift, w, b)


def _mlstm_kernel(x_ref, v_ref, o_ref, z_ref, gt_ref, cw_ref, cb_ref, wq_ref, wk_ref, nw_ref, sk_ref, fb_ref,
                  y_ref, c_scr, n_scr, m_scr, xp_scr):
    L = M_CHUNK

    @pl.when(pl.program_id(1) == 0)
    def _():
        c_scr[...] = jnp.zeros_like(c_scr)
        n_scr[...] = jnp.zeros_like(n_scr)
        m_scr[...] = jnp.zeros_like(m_scr)
        xp_scr[...] = jnp.zeros_like(xp_scr)

    x = x_ref[0]
    prev = xp_scr[...]
    row8 = lax.broadcasted_iota(jnp.int32, (8, M_WIDTH), 0)
    cw = cw_ref[...]
    xc = cb_ref[...] + x * cw[CONV_K - 1:CONV_K, :]
    for sft in range(1, CONV_K):
        xr = pltpu.roll(x, sft, 0)
        top = jnp.where(row8 < sft, pltpu.roll(prev, sft, 0), xr[0:8])
        xs = jnp.concatenate([top, xr[8:]], axis=0)
        xc = xc + xs * cw[CONV_K - 1 - sft:CONV_K - sft, :]
    xp_scr[...] = x[L - 8:L]
    xc = _silu(xc)

    gt = gt_ref[0]
    col = lax.broadcasted_iota(jnp.int32, (L, LANES), 1)
    logf = _log_sigmoid(gt + fb_ref[...])
    a_c = jnp.where((col >= M_HEADS) & (col < 2 * M_HEADS), logf, gt)
    ri = lax.broadcasted_iota(jnp.int32, (L, L), 0)
    ci = lax.broadcasted_iota(jnp.int32, (L, L), 1)
    causal = ri >= ci
    hp = lax.Precision.HIGHEST
    tri = causal.astype(F32)
    tri_t = (ri <= ci).astype(F32)
    b_c = jnp.dot(tri, a_c, precision=hp, preferred_element_type=F32)
    a_r = a_c.T
    b_r = jnp.dot(a_r[0:8], tri_t, precision=hp, preferred_element_type=F32)

    for h in range(M_HEADS):
        sl = slice(h * M_HD, (h + 1) * M_HD)
        xh = xc[:, sl]
        xb = xh.astype(BF16)
        q = jnp.dot(xb, wq_ref[h], preferred_element_type=F32)
        k = jnp.dot(xb, wk_ref[h], preferred_element_type=F32) * (M_HD ** -0.5)
        vb = v_ref[0, :, sl].astype(BF16)
        qb = q.astype(BF16)
        kb = k.astype(BF16)

        bt = b_c[:, M_HEADS + h:M_HEADS + h + 1]
        ic = a_c[:, h:h + 1]
        bs = b_r[M_HEADS + h:M_HEADS + h + 1, :]
        ir = a_r[h:h + 1, :]
        m_prev = m_scr[h][:, 0:1]

        dm = jnp.where(causal, bt - bs + ir, -jnp.inf)
        inter = bt + m_prev
        m_t = jnp.maximum(inter, jnp.max(dm, axis=-1, keepdims=True))
        w_in = jnp.exp(dm - m_t)
        w_st = jnp.exp(inter - m_t)
        s = lax.dot_general(qb, kb, (((1,), (1,)), ((), ())), preferred_element_type=F32) * w_in
        cmat = c_scr[h]
        nvec = n_scr[h]
        sb = s.astype(BF16)
        num = w_st * jnp.dot(qb, cmat.astype(BF16), preferred_element_type=F32) \
            + jnp.dot(sb, vb, preferred_element_type=F32)
        nt_dims = (((1,), (1,)), ((), ()))
        qn = lax.dot_general(qb, jnp.broadcast_to(nvec, (8, M_HD)).astype(BF16), nt_dims,
                             preferred_element_type=F32)[:, 0:1]
        ssum = lax.dot_general(sb, jnp.ones((8, L), BF16), nt_dims, preferred_element_type=F32)[:, 0:1]
        den = w_st * qn + ssum
        hh = num / jnp.maximum(jnp.abs(den), jnp.exp(-m_t))

        b_last = bt[L - 1:L, :]
        w_end = b_last - bt + ic
        m_new = jnp.maximum(b_last + m_prev, jnp.max(w_end, axis=0, keepdims=True))
        decay = jnp.exp(b_last + m_prev - m_new)
        kwt = k * jnp.exp(w_end - m_new)
        c_scr[h] = decay * cmat + lax.dot_general(kwt.astype(BF16), vb, (((0,), (0,)), ((), ())),
                                                  preferred_element_type=F32)
        n_scr[h] = decay * nvec + jnp.sum(kwt, axis=0, keepdims=True)
        m_scr[h] = jnp.broadcast_to(m_new, (1, LANES))

        mu = jnp.mean(hh, axis=-1, keepdims=True)
        hc = hh - mu
        var = jnp.mean(hc * hc, axis=-1, keepdims=True)
        hn = hc * lax.rsqrt(var + EPS) * nw_ref[:, sl]
        out = jax.nn.sigmoid(o_ref[0, :, sl]) * hn + sk_ref[:, sl] * xh
        y_ref[0, :, sl] = (out * _silu(z_ref[0, :, sl])).astype(BF16)


def _mlstm(p, conv_w, conv_b, wq, wk, norm_w, skip, fb_row):
    bsz, t, _ = p.shape
    L = M_CHUNK
    cb = lambda c: (lambda bi, i: (bi, i, c))
    full2 = lambda bi, i: (0, 0)
    full3 = lambda bi, i: (0, 0, 0)
    return pl.pallas_call(
        _mlstm_kernel,
        grid=(bsz, t // L),
        in_specs=[pl.BlockSpec((1, L, M_WIDTH), cb(COL_MX // M_WIDTH)),
                  pl.BlockSpec((1, L, M_WIDTH), cb(COL_MV // M_WIDTH)),
                  pl.BlockSpec((1, L, M_WIDTH), cb(COL_MO // M_WIDTH)),
                  pl.BlockSpec((1, L, M_WIDTH), cb(COL_MZ // M_WIDTH)),
                  pl.BlockSpec((1, L, LANES), cb(COL_GATES // LANES)),
                  pl.BlockSpec((CONV_K, M_WIDTH), full2),
                  pl.BlockSpec((1, M_WIDTH), full2),
                  pl.BlockSpec((M_HEADS, M_HD, M_HD), full3),
                  pl.BlockSpec((M_HEADS, M_HD, M_HD), full3),
                  pl.BlockSpec((1, M_WIDTH), full2),
                  pl.BlockSpec((1, M_WIDTH), full2),
                  pl.BlockSpec((1, LANES), full2)],
        out_specs=pl.BlockSpec((1, L, M_WIDTH), lambda bi, i: (bi, i, 0)),
        out_shape=jax.ShapeDtypeStruct((bsz, t, M_WIDTH), BF16),
        scratch_shapes=[pltpu.VMEM((M_HEADS, M_HD, M_HD), F32),
                        pltpu.VMEM((M_HEADS, 1, M_HD), F32),
                        pltpu.VMEM((M_HEADS, 1, LANES), F32),
                        pltpu.VMEM((8, M_WIDTH), F32)],
        compiler_params=_cparams(("parallel", "arbitrary")),
        name="mlstm_group",
    )(p, p, p, p, p, conv_w, conv_b, wq, wk, norm_w, skip, fb_row)


def _compress_kernel(kc0_ref, kc1_ref, vc0_ref, vc1_ref, posk_ref, posv_ref, w1k_ref, w1kc_ref, w2k_ref,
                     w1v_ref, w1vc_ref, w2v_ref, kcmp_ref, vcmpt_ref):
    nb = kcmp_ref.shape[2]

    def hidden(src_refs, pos_ref, w1_ref, w1c_ref):
        halves = [jnp.zeros((nb, 4 * CMP_HIDDEN), F32) for _ in src_refs]
        for l in range(CMP_STRIDE):
            for hf, r in enumerate(src_refs):
                xl = r[0, pl.ds(l, nb, stride=CMP_STRIDE), :]
                halves[hf] = halves[hf] + jnp.dot(xl.astype(BF16), w1c_ref[l], preferred_element_type=F32)
        acc = jnp.concatenate([hv[:, c * 2 * CMP_HIDDEN:(c + 1) * 2 * CMP_HIDDEN] for hv in halves for c in range(2)],
                              axis=0)
        first = acc[:, :CMP_HIDDEN]
        second = acc[:, CMP_HIDDEN:]
        posb = jnp.dot(jnp.broadcast_to(pos_ref[...], (8, CMP_LEN * N_HD)).astype(BF16), w1_ref[...],
                       preferred_element_type=F32)[0:1]
        hid = first + pltpu.roll(second, N_KV * nb - 1, 0) + posb
        return jax.nn.gelu(hid).astype(BF16)

    hk = hidden((kc0_ref, kc1_ref), posk_ref, w1k_ref, w1kc_ref)
    kc = jnp.dot(hk, w2k_ref[...], preferred_element_type=F32)
    hv = hidden((vc0_ref, vc1_ref), posv_ref, w1v_ref, w1vc_ref)
    vc = jnp.dot(hv, w2v_ref[...], preferred_element_type=F32)
    j = lax.broadcasted_iota(jnp.int32, (nb, 1), 0)
    lane = lax.broadcasted_iota(jnp.int32, (nb, N_HD), 1)
    n_real = (nb * CMP_STRIDE - CMP_LEN) // CMP_STRIDE + 1
    aux = jnp.where(lane < 3, (j * CMP_STRIDE).astype(F32),
                    jnp.where(lane < 6, (CMP_LEN - 1) * 0.5,
                              jnp.where((lane == 6) & (j >= n_real), 1.0, 0.0))).astype(BF16)
    ones_blk = jnp.where(lax.broadcasted_iota(jnp.int32, (VROWS - N_HD, nb), 0) == 0, 1.0, 0.0)
    for g in range(N_KV):
        kg = kc[g * nb:(g + 1) * nb, 0:N_HD]
        k_hi = kg.astype(BF16)
        k_lo = (kg - k_hi.astype(F32)).astype(BF16)
        kcmp_ref[0, g] = jnp.concatenate([k_hi, k_lo, k_hi, aux], axis=1)
        vt = vc[g * nb:(g + 1) * nb].T
        vcmpt_ref[0, g] = jnp.concatenate([vt[0:N_HD], ones_blk], axis=0).astype(BF16)


def _compress(p, posk, posv, w1k, w1kc, w2k, w1v, w1vc, w2v):
    bsz, t, _ = p.shape
    nb = t // CMP_STRIDE
    cb = lambda c: (lambda bi: (bi, 0, c))
    f2 = lambda bi: (0, 0)
    f3 = lambda bi: (0, 0, 0)
    wspecs = [pl.BlockSpec((CMP_LEN * N_HD, CMP_HIDDEN), f2),
              pl.BlockSpec((CMP_STRIDE, LANES, 4 * CMP_HIDDEN), f3),
              pl.BlockSpec((CMP_HIDDEN, LANES), f2)]
    return pl.pallas_call(
        _compress_kernel,
        grid=(bsz,),
        in_specs=[pl.BlockSpec((1, t, LANES), cb(COL_KC // LANES)),
                  pl.BlockSpec((1, t, LANES), cb(COL_KC // LANES + 1)),
                  pl.BlockSpec((1, t, LANES), cb(COL_VC // LANES)),
                  pl.BlockSpec((1, t, LANES), cb(COL_VC // LANES + 1)),
                  pl.BlockSpec((1, CMP_LEN * N_HD), f2),
                  pl.BlockSpec((1, CMP_LEN * N_HD), f2)] + wspecs + wspecs,
        out_specs=[pl.BlockSpec((1, N_KV, nb, 4 * N_HD), lambda bi: (bi, 0, 0, 0)),
                   pl.BlockSpec((1, N_KV, VROWS, nb), lambda bi: (bi, 0, 0, 0))],
        out_shape=[jax.ShapeDtypeStruct((bsz, N_KV, nb, 4 * N_HD), BF16),
                   jax.ShapeDtypeStruct((bsz, N_KV, VROWS, nb), BF16)],
        compiler_params=_cparams(("parallel",)),
        name="nsa_compress",
    )(p, p, p, p, posk, posv, w1k, w1kc, w2k, w1v, w1vc, w2v)


def _relayout_kernel(ks_ref, vs_ref, kw_ref, vw_ref, ksa_ref, vsa_ref, kwa_ref, vwa_ref):
    i = pl.program_id(1)
    is_pad = i == 0
    flag = jnp.where(is_pad, 1.0, 0.0)
    row = lax.broadcasted_iota(jnp.int32, (SKT, 1), 0)
    lane = lax.broadcasted_iota(jnp.int32, (SKT, N_HD), 1)
    al = lane - AUX_ROWS
    ones_blk = jnp.where(lax.broadcasted_iota(jnp.int32, (VROWS - N_HD, SKT), 0) == 0, 1.0, 0.0)
    for u in range(PAD_TILES):
        rows = slice(u * SKT, (u + 1) * SKT)
        base = ((i - 1) * PAD_TILES + u) * SKT
        pos = jnp.where(is_pad, 0, base + row)
        blk = pos // SLC_LEN
        p_hi = (blk * SLC_LEN).astype(F32)
        p_lo = (pos - blk * SLC_LEN).astype(F32)
        mid_w = jnp.where((al >= 0) & (al < 3), p_hi,
                          jnp.where((al >= 3) & (al < 6), p_lo, jnp.where(al == 6, flag, 0.0)))
        mid_s = jnp.where(lane == blk % SEL_GRP, 1.0, mid_w)
        ks = ks_ref[0, rows, :]
        kw = kw_ref[0, rows, :]
        for g in range(N_KV):
            sl = slice(g * N_HD, (g + 1) * N_HD)
            ksa_ref[0, g, rows, :] = jnp.concatenate([ks[:, sl], mid_s], axis=1).astype(BF16)
            kwa_ref[0, g, rows, :] = jnp.concatenate([kw[:, sl], mid_w], axis=1).astype(BF16)
        vst = vs_ref[0, rows, :].T
        vwt = vw_ref[0, rows, :].T
        for g in range(N_KV):
            sl = slice(g * N_HD, (g + 1) * N_HD)
            vsa_ref[0, g, u] = jnp.concatenate([vst[sl], ones_blk], axis=0).astype(BF16)
            vwg = jnp.concatenate([vwt[sl], ones_blk], axis=0).astype(BF16)
            for j in range(SKT // WKT):
                vwa_ref[0, g, u * (SKT // WKT) + j] = vwg[:, j * WKT:(j + 1) * WKT]


def _relayout(p):
    bsz, t, _ = p.shape
    nt = t // SKT + PAD_TILES
    wpt = SKT // WKT
    rt = PAD_TILES
    cb = lambda c: (lambda bi, i: (bi, jnp.maximum(i - 1, 0), c))
    return pl.pallas_call(
        _relayout_kernel,
        grid=(bsz, nt // rt),
        in_specs=[pl.BlockSpec((1, rt * SKT, KV_W), cb(COL_KS // KV_W)),
                  pl.BlockSpec((1, rt * SKT, KV_W), cb(COL_VS // KV_W)),
                  pl.BlockSpec((1, rt * SKT, KV_W), cb(COL_KW // KV_W)),
                  pl.BlockSpec((1, rt * SKT, KV_W), cb(COL_VW // KV_W))],
        out_specs=[pl.BlockSpec((1, N_KV, rt * SKT, KA_SEL), lambda bi, i: (bi, 0, i, 0)),
                   pl.BlockSpec((1, N_KV, rt, VROWS, SKT), lambda bi, i: (bi, 0, i, 0, 0)),
                   pl.BlockSpec((1, N_KV, rt * SKT, KA_WIN), lambda bi, i: (bi, 0, i, 0)),
                   pl.BlockSpec((1, N_KV, rt * wpt, VROWS, WKT), lambda bi, i: (bi, 0, i, 0, 0))],
        out_shape=[jax.ShapeDtypeStruct((bsz, N_KV, nt * SKT, KA_SEL), BF16),
                   jax.ShapeDtypeStruct((bsz, N_KV, nt, VROWS, SKT), BF16),
                   jax.ShapeDtypeStruct((bsz, N_KV, nt * SKT, KA_WIN), BF16),
                   jax.ShapeDtypeStruct((bsz, N_KV, nt * wpt, VROWS, WKT), BF16)],
        compiler_params=_cparams(("parallel", "parallel")),
        name="nsa_relayout",
    )(p, p, p, p)


def _nsa_batch_kernel(sp_ref, q_ref, gt_ref, z_ref, kcmp_ref, vcmpt_ref, ovt_ref, ks_ref, vs_ref, kw_ref,
                      vw_ref, tri_ref, wb_ref, y_ref, sa_scr, sb_scr, g_scr, imp_scr, rank_scr, part_scr, selb_scr,
                      tile_idx):
    g = pl.program_id(0)
    qb = pl.program_id(1)
    n_items = q_ref.shape[0]
    start = qb * QBLK
    nb = kcmp_ref.shape[2]
    n_slc = ovt_ref.shape[0]
    nw = N_HPG * QBLK
    n_top = min(SLC_TOPN, n_slc)
    t_row = start + lax.broadcasted_iota(jnp.int32, (1, QBLK), 1)
    s_bufs = (sa_scr, sb_scr)
    last = qb // (SKT // QBLK)

    def tile4(a):
        return jnp.concatenate([a] * N_HPG, axis=1)

    def col_reduce8(s, op):
        out = s[0:8]
        for r in range(1, s.shape[0] // 8):
            out = op(out, s[8 * r:8 * r + 8])
        return out

    def col_max(s):
        return jnp.max(col_reduce8(s, jnp.maximum), axis=0, keepdims=True)

    def normalise(acc):
        den = acc[N_HD:N_HD + 1]
        return acc[0:N_HD] / jnp.where(den > 0, den, 1.0)

    def gate_row(i, h, branch):
        return g_scr[i, pl.ds(GATE_NG + (g * N_HPG + h) * 3 + branch, 1), :]

    hl = lax.broadcasted_iota(jnp.int32, (AUX_ROWS, nw), 1) // QBLK
    ar = lax.broadcasted_iota(jnp.int32, (AUX_ROWS, nw), 0)

    def per_head(vals):
        out = jnp.full((AUX_ROWS, nw), vals[N_HPG - 1], F32)
        for h in range(N_HPG - 2, -1, -1):
            out = jnp.where(hl == h, vals[h], out)
        return out

    pieces = [per_head([sp_ref[(g * N_HPG + h) * 3 + j] for h in range(N_HPG)]) for j in range(3)]
    aux = jnp.where(ar == 6, NEG, 0.0)
    for j in range(3):
        aux = jnp.where((ar == j) | (ar == j + 3), pieces[j], aux)
    aux_b = aux.astype(BF16)
    q_tail = jnp.zeros((KA_SEL - N_HD - 2 * AUX_ROWS, nw), BF16)
    sidx = lax.broadcasted_iota(jnp.int32, (n_slc, QBLK), 0)
    cur = t_row // SLC_LEN
    valid = sidx <= cur
    sub8 = lax.broadcasted_iota(jnp.int32, (8, QBLK), 0)
    last_blk = (start + QBLK - 1) // SLC_LEN
    cmp_end = lax.broadcasted_iota(jnp.int32, (nb, 1), 0) * CMP_STRIDE + (CMP_LEN - 1)
    cmp_bias = tile4(jnp.where(cmp_end <= t_row, 0.0, NEG))
    wrows = WIN + QBLK

    def prepare(i):
        qt = (q_ref[i] * (N_HD ** -0.5)).T
        q4l = jnp.concatenate([qt[h * N_HD:(h + 1) * N_HD] for h in range(N_HPG)], axis=1) * LOG2E
        q4s = q4l.astype(BF16)
        q4lo = (q4l - q4s.astype(F32)).astype(BF16)
        g_scr[i] = jax.nn.sigmoid(gt_ref[i]).T

        q_cmp = jnp.concatenate([q4s, q4s, q4lo, aux_b, jnp.zeros((N_HD - AUX_ROWS, nw), BF16)], axis=0)
        s_c = jnp.dot(kcmp_ref[i, 0], q_cmp, preferred_element_type=F32) + cmp_bias
        e_c = jnp.exp2(s_c - col_max(s_c))
        z_c = jnp.sum(col_reduce8(e_c, jnp.add), axis=0, keepdims=True)
        inv_c = jnp.where(tile4(t_row >= CMP_LEN - 1) & (z_c > 0), 1.0 / z_c, 0.0)
        o_c = jnp.dot(vcmpt_ref[i, 0], e_c.astype(BF16), preferred_element_type=F32)[0:N_HD] * inv_c
        p_c = e_c * inv_c
        psum = p_c[:, 0:QBLK]
        for h in range(1, N_HPG):
            psum = psum + p_c[:, h * QBLK:(h + 1) * QBLK]
        parts, rest = [], psum
        for _ in range(3):
            parts.append(rest.astype(BF16))
            rest = rest - parts[-1].astype(F32)
        imp = jnp.dot(ovt_ref[...], jnp.concatenate(parts, axis=0), preferred_element_type=F32)
        imp = jnp.where(valid, imp, -jnp.inf)
        imp_scr[...] = jnp.where((sidx == 0) | (sidx == cur), jnp.inf, imp)

        q_win = jnp.concatenate([q4s, jnp.zeros((AUX_ROWS, nw), BF16), aux_b,
                                 jnp.zeros((KA_WIN - N_HD - 2 * AUX_ROWS, nw), BF16)], axis=0)
        s_w = jnp.dot(kw_ref[i, 0, pl.ds(pl.multiple_of(start, QBLK), wrows), :], q_win,
                      preferred_element_type=F32)
        s_w = jnp.concatenate([s_w[0:WKT] + tile4(wb_ref[0]), s_w[WKT:wrows - WKT],
                               s_w[wrows - WKT:] + tile4(wb_ref[1])], axis=0)
        p_w = jnp.exp2(s_w - col_max(s_w)).astype(BF16)
        v_w = jnp.concatenate([vw_ref[i, 0, qb * (QBLK // WKT) + j] for j in range(wrows // WKT)], axis=1)
        o_w = normalise(jnp.dot(v_w, p_w, preferred_element_type=F32))
        for h in range(N_HPG):
            hs = slice(h * QBLK, (h + 1) * QBLK)
            part_scr[i, :, hs] = gate_row(i, h, 0) * o_c[:, hs] + gate_row(i, h, 2) * o_w[:, hs]

        rank_scr[...] = jnp.zeros_like(rank_scr)
        for ri in range(n_slc // 8):
            @pl.when((last_blk >= n_top) & (8 * ri <= last_blk))
            def _():
                rows = imp_scr[8 * ri:8 * ri + 8]
                for r in range(n_slc // 8):
                    blk8 = imp_scr[8 * r:8 * r + 8]
                    acc = rank_scr[8 * r:8 * r + 8]
                    for ii in range(8):
                        row = rows[ii:ii + 1]
                        if ri < r:
                            before = row >= blk8
                        elif ri > r:
                            before = row > blk8
                        else:
                            before = (row > blk8) | ((row == blk8) & (sub8 > ii))
                        acc = acc + jnp.where(before, 1.0, 0.0)
                    rank_scr[8 * r:8 * r + 8] = acc
        chosen = (rank_scr[...] < n_top) & valid
        selb_scr[i] = tile4(jnp.where(chosen, 0.0, NEG))

        any_q = jnp.max(jnp.where(chosen, 1.0, 0.0), axis=1, keepdims=True) > 0
        s_io = lax.broadcasted_iota(jnp.int32, (n_slc, 1), 0)
        bits = jnp.where(any_q, jnp.left_shift(1, s_io % 32), 0)
        words = [jnp.sum(jnp.where(s_io // 32 == w, bits, 0)) for w in range((n_slc + 31) // 32)]
        bpt = SKT // SLC_LEN
        cnt = jnp.int32(0)
        for tile in range(n_slc // bpt):
            nib = lax.shift_right_logical(words[tile * bpt // 32], jnp.int32(tile * bpt % 32)) & (2 ** bpt - 1)
            tile_idx[i, cnt] = tile
            cnt = cnt + ((nib != 0) & (tile < last)).astype(jnp.int32)
        return q4s, cnt

    q4, n_act = zip(*[prepare(i) for i in range(n_items)])


    def sel_scores(i, kt):
        grp = pl.multiple_of((kt * SKT // SLC_LEN) // SEL_GRP * SEL_GRP, SEL_GRP)
        sel_rows = jnp.concatenate([selb_scr[i, pl.ds(grp, SEL_GRP), :], jnp.zeros((AUX_ROWS - SEL_GRP, nw), F32)],
                                   axis=0)
        q_sel = jnp.concatenate([q4[i], sel_rows.astype(BF16), aux_b, q_tail], axis=0)
        krow = pl.multiple_of((kt + PAD_TILES) * SKT, SKT)
        return jnp.dot(ks_ref[i, 0, pl.ds(krow, SKT), :], q_sel, preferred_element_type=F32)

    def tile_rows(kt):
        return pl.ds(pl.multiple_of(kt * SKT, SKT), SKT)

    def pass1(i):
        def fn(kt, mrun):
            s = sel_scores(i, kt)
            s_bufs[i % 2][tile_rows(kt), :] = s
            return jnp.maximum(mrun, col_reduce8(s, jnp.maximum))
        return fn

    def pass2(i, m_sel):
        def fn(kt, acc):
            pr = jnp.exp2(s_bufs[i % 2][tile_rows(kt), :] - m_sel).astype(BF16)
            return acc + jnp.dot(vs_ref[i, 0, kt + PAD_TILES], pr, preferred_element_type=F32)
        return fn

    def listed(i, fn):
        return lambda j, carry: fn(tile_idx[i, j], carry)

    def both(f1, f2):
        def fn(j, carry):
            return f1(j, carry[0]), f2(j, carry[1])
        return fn

    def run_tiles(fn, lo, n, init):
        def group(size, first):
            def body(j, carry):
                for u in range(size):
                    carry = fn(first + j * size + u, carry)
                return carry
            return body

        n_u = n // UNROLL
        carry = lax.fori_loop(0, n_u, group(UNROLL, lo), init)
        done = lo + n_u * UNROLL
        size = UNROLL // 2
        while size >= 1:
            take = (n // size) % 2
            carry = lax.fori_loop(0, take, group(size, done), carry)
            done = done + take * size
            size //= 2
        return carry

    def diag_tile(i, mrun):
        s = sel_scores(i, last) + tile4(tri_ref[qb % (SKT // QBLK)])
        s_bufs[i % 2][tile_rows(last), :] = s
        return jnp.max(jnp.maximum(mrun, col_reduce8(s, jnp.maximum)), axis=0, keepdims=True)

    mrun0 = jnp.full((8, nw), NEG, F32)
    acc0 = jnp.zeros((VROWS, nw), F32)
    m_prev = diag_tile(0, run_tiles(listed(0, pass1(0)), 0, n_act[0], mrun0))
    outs = []
    for i in range(1, n_items):
        p2 = pass2(i - 1, m_prev)
        f1, f2 = listed(i, pass1(i)), listed(i - 1, p2)
        n_both = jnp.minimum(n_act[i], n_act[i - 1])
        mrun, acc = run_tiles(both(f1, f2), 0, n_both, (mrun0, acc0))
        mrun = run_tiles(f1, n_both, n_act[i] - n_both, mrun)
        acc = run_tiles(f2, n_both, n_act[i - 1] - n_both, acc)
        m_cur = diag_tile(i, mrun)
        outs.append(normalise(p2(last, acc)))
        m_prev = m_cur
    p2 = pass2(n_items - 1, m_prev)
    acc = run_tiles(listed(n_items - 1, p2), 0, n_act[n_items - 1], acc0)
    outs.append(normalise(p2(last, acc)))

    for i in range(n_items):
        tot = [part_scr[i, :, h * QBLK:(h + 1) * QBLK] + gate_row(i, h, 1) * outs[i][:, h * QBLK:(h + 1) * QBLK]
               for h in range(N_HPG)]
        o = jnp.concatenate(tot, axis=0).T
        y_ref[i] = (o * _silu(z_ref[i])).astype(BF16)


def _edge_biases():
    kl = np.arange(SKT)[:, None]
    ql = np.arange(QBLK)[None, :]
    tri = np.stack([np.where(kl <= par * QBLK + ql, 0.0, NEG) for par in range(SKT // QBLK)])
    kk = np.arange(WKT)[:, None]
    wb = np.stack([np.where(kk > ql, 0.0, NEG), np.where(kk <= ql + WKT - QBLK, 0.0, NEG)])
    return jnp.asarray(tri, F32), jnp.asarray(wb, F32)


def _nsa(p, spieces, kcmp, vcmpt, ovt, ks, vs, kw, vw):
    bsz, t, _ = p.shape
    n_slc = t // SLC_LEN
    gw = N_HPG * N_HD
    tri, wb = _edge_biases()
    nw = N_HPG * QBLK
    per_g = lambda a: pl.BlockSpec((bsz, 1) + a.shape[2:], lambda g, i, sp: (0, g) + (0,) * (a.ndim - 2))
    const = lambda a: pl.BlockSpec(a.shape, lambda g, i, sp: (0,) * a.ndim)
    grid_spec = pltpu.PrefetchScalarGridSpec(
        num_scalar_prefetch=1,
        grid=(N_KV, t // QBLK),
        in_specs=[pl.BlockSpec((bsz, QBLK, gw), lambda g, i, sp: (0, i, COL_NQ // gw + g)),
                  pl.BlockSpec((bsz, QBLK, LANES), lambda g, i, sp: (0, i, COL_GATES // LANES)),
                  pl.BlockSpec((bsz, QBLK, gw), lambda g, i, sp: (0, i, COL_NZ // gw + g)),
                  per_g(kcmp), per_g(vcmpt), const(ovt),
                  per_g(ks), per_g(vs), per_g(kw), per_g(vw), const(tri), const(wb)],
        out_specs=pl.BlockSpec((bsz, QBLK, gw), lambda g, i, sp: (0, i, g)),
        scratch_shapes=[pltpu.VMEM((t, nw), F32),
                        pltpu.VMEM((t, nw), F32),
                        pltpu.VMEM((bsz, LANES, QBLK), F32),
                        pltpu.VMEM((n_slc, QBLK), F32),
                        pltpu.VMEM((n_slc, QBLK), F32),
                        pltpu.VMEM((bsz, N_HD, nw), F32),
                        pltpu.VMEM((bsz, n_slc, nw), F32),
                        pltpu.SMEM((bsz, t // SKT), jnp.int32)],
    )
    return pl.pallas_call(
        _nsa_batch_kernel,
        grid_spec=grid_spec,
        out_shape=jax.ShapeDtypeStruct((bsz, t, N_WIDTH), BF16),
        compiler_params=_cparams(("arbitrary", "arbitrary")),
        name="nsa_attention",
    )(spieces, p, p, p, kcmp, vcmpt, ovt, ks, vs, kw, vw, tri, wb)


def _outproj_kernel(ym_ref, yn_ref, w_ref, x_ref, gate_ref, fg_ref, o_ref, wb_scr, *, final):
    @pl.when((pl.program_id(0) == 0) & (pl.program_id(1) == 0))
    def _():
        rows = w_ref.shape[1]
        for c in range(rows // WP_TILE):
            cs = slice(c * WP_TILE, (c + 1) * WP_TILE)
            wb_scr[cs, :] = w_ref[0, cs, :].astype(BF16)

    y = jnp.dot(ym_ref[0], wb_scr[0:M_WIDTH, :], preferred_element_type=F32)
    y = y + jnp.dot(yn_ref[0], wb_scr[M_WIDTH:, :], preferred_element_type=F32)
    hres = x_ref[0] + gate_ref[0] * y
    if final:
        ms = jnp.mean(hres * hres, axis=-1, keepdims=True)
        hres = hres * lax.rsqrt(ms + EPS) * fg_ref[...]
    o_ref[0] = hres


def _outproj(ym, yn, w, layer, x, gate, fg, final):
    bsz, t, d = x.shape
    tm = 512
    return pl.pallas_call(
        functools.partial(_outproj_kernel, final=final),
        grid=(bsz, t // tm),
        in_specs=[pl.BlockSpec((1, tm, M_WIDTH), lambda bi, i: (bi, i, 0)),
                  pl.BlockSpec((1, tm, N_WIDTH), lambda bi, i: (bi, i, 0)),
                  pl.BlockSpec((1, M_WIDTH + N_WIDTH, d), lambda bi, i: (layer, 0, 0),
                               pipeline_mode=pl.Buffered(1)),
                  pl.BlockSpec((1, tm, d), lambda bi, i: (bi, i, 0)),
                  pl.BlockSpec((1, 1, d), lambda bi, i: (bi, 0, 0)),
                  pl.BlockSpec((1, d), lambda bi, i: (0, 0))],
        out_specs=pl.BlockSpec((1, tm, d), lambda bi, i: (bi, i, 0)),
        out_shape=jax.ShapeDtypeStruct((bsz, t, d), F32),
        scratch_shapes=[pltpu.VMEM((M_WIDTH + N_WIDTH, d), BF16)],
        compiler_params=_cparams(("arbitrary", "arbitrary")),
        name="outproj_residual",
    )(ym, yn, w, x, gate, fg)


SRC_MI = 4 * M_WIDTH
SRC_NQ = SRC_MI + 2 * M_HEADS
SRC_NG = SRC_NQ + N_WIDTH + 6 * KV_W
SRC_NZ = SRC_NG + 3 * N_HEADS


def _reorder_cols(a):
    parts = [a[..., 0:SRC_MI], a[..., SRC_NQ:SRC_NG], a[..., SRC_NZ:SRC_NZ + N_WIDTH], a[..., SRC_MI:SRC_NQ],
             a[..., SRC_NG:SRC_NZ]]
    used = sum(x.shape[-1] for x in parts)
    parts.append(jnp.zeros(a.shape[:-1] + (NP_PAD - used,), a.dtype))
    return jnp.concatenate(parts, axis=-1)


WP_TILE = 512
WP_GATE_TILE = COL_GATES // WP_TILE


def _wprep_kernel(w_ref, g1_ref, g2_ref, o_ref):
    j = pl.program_id(0)
    d = o_ref.shape[0]

    @pl.when(j < WP_GATE_TILE)
    def _():
        for c in range(d // WP_TILE):
            cs = slice(c * WP_TILE, (c + 1) * WP_TILE)
            o_ref[cs, :] = w_ref[0, :, cs].T.astype(BF16)

    @pl.when(j == WP_GATE_TILE)
    def _():
        n_gate = g1_ref.shape[1] + g2_ref.shape[1]
        gt = jnp.concatenate([g1_ref[0], g2_ref[0], jnp.zeros((LANES - n_gate, d), F32)], axis=0)
        for c in range(d // WP_TILE):
            cs = slice(c * WP_TILE, (c + 1) * WP_TILE)
            o_ref[cs, 0:LANES] = gt[:, cs].T.astype(BF16)
        o_ref[:, LANES:] = jnp.zeros((d, WP_TILE - LANES), BF16)


def _wprep(w_t, layer):
    _, n, d = w_t.shape
    n_big = COL_GATES // WP_TILE
    assert n == SRC_NZ + N_WIDTH and NP_PAD // WP_TILE == n_big + 1

    def src_row(j):
        return jnp.where(j < COL_NQ // WP_TILE, j * WP_TILE,
                         jnp.where(j < COL_NZ // WP_TILE, SRC_NQ + (j - COL_NQ // WP_TILE) * WP_TILE,
                                   jnp.where(j < n_big, SRC_NZ + (j - COL_NZ // WP_TILE) * WP_TILE, 0)))

    el = pl.Element
    return pl.pallas_call(
        _wprep_kernel,
        grid=(NP_PAD // WP_TILE,),
        in_specs=[pl.BlockSpec((el(1), el(WP_TILE), el(d)), lambda j: (layer, pl.multiple_of(src_row(j), 8), 0)),
                  pl.BlockSpec((el(1), el(SRC_NQ - SRC_MI), el(d)), lambda j: (layer, SRC_MI, 0)),
                  pl.BlockSpec((el(1), el(SRC_NZ - SRC_NG), el(d)), lambda j: (layer, SRC_NG, 0))],
        out_specs=pl.BlockSpec((d, WP_TILE), lambda j: (0, j)),
        out_shape=jax.ShapeDtypeStruct((d, NP_PAD), BF16),
        compiler_params=_cparams(("parallel",)),
        name="inproj_weight_prep",
    )(w_t, w_t, w_t)


def _overlap_t(t):
    n_cmp_rows = t // CMP_STRIDE
    n_slc = t // SLC_LEN
    c0 = np.arange(n_cmp_rows) * CMP_STRIDE
    s0 = np.arange(n_slc) * SLC_LEN
    ov = (c0[None, :] <= s0[:, None] + SLC_LEN - 1) & (c0[None, :] + CMP_LEN - 1 >= s0[:, None])
    ov[:, (t - CMP_LEN) // CMP_STRIDE + 1:] = False
    return jnp.asarray(np.concatenate([ov] * 3, axis=1), BF16)


def kernel(x, c, ln_g, w_ada, b_ada, w_in, b_in, m_conv_w, m_conv_b, m_wq, m_wk, m_norm_w, m_skip, m_f_bias,
           n_pos_k, n_pos_v, n_w1_k, n_w2_k, n_w1_v, n_w2_v, w_out, final_g):
    out_dtype = x.dtype
    bsz, t, d = x.shape
    depth = ln_g.shape[0]
    h_res = x.astype(F32)
    assert bsz <= 2 and t % 1024 == 0 and t // CMP_STRIDE <= 256
    c_t =jnp.zeros((d, 8), F32).at[:, :bsz].set(c.astype(F32).T)
    slopes_np = np.array([2.0 ** (-8.0 * (h + 1) / N_HEADS) for h in range(N_HEADS)], np.float32)
    rest = (slopes_np.astype(np.float64) * LOG2E).astype(np.float32)
    pieces = []
    for _ in range(3):
        pieces.append(rest.astype(jnp.bfloat16).astype(np.float32))
        rest = rest - pieces[-1]
    spieces = jnp.asarray(np.stack(pieces, axis=1).reshape(-1))
    ovt = _overlap_t(t)

    def w1cat(w1):
        w = w1.reshape(2, CMP_STRIDE, N_HD, CMP_HIDDEN)
        w = jnp.concatenate([w[0], w[1]], axis=-1).astype(BF16)
        z = jnp.zeros_like(w)
        return jnp.concatenate([jnp.concatenate([w, z], axis=-1), jnp.concatenate([z, w], axis=-1)], axis=1)

    def w2pad(w2):
        return jnp.pad(w2, ((0, 0), (0, LANES - N_HD))).astype(BF16)

    for l in range(depth):
        mod = _ada(c_t, w_ada, b_ada[l][None, :], l, bsz)[:bsz]
        shift, scale, gate = mod[:, None, 0:d], mod[:, None, d:2 * d], mod[:, None, 2 * d:3 * d]
        p = _inproj(h_res, ln_g[l][None, :], scale, shift,
                    _wprep(jnp.swapaxes(w_in, 1, 2), l), _reorder_cols(b_in[l])[None, :])
        fb_row = jnp.zeros((1, LANES), F32).at[0, M_HEADS:2 * M_HEADS].set(m_f_bias[l])
        y_m = _mlstm(p, m_conv_w[l], m_conv_b[l][None, :], m_wq[l].astype(BF16), m_wk[l].astype(BF16),
                     m_norm_w[l][None, :], m_skip[l][None, :], fb_row)
        kcmp, vcmpt = _compress(p, n_pos_k[l].reshape(1, -1), n_pos_v[l].reshape(1, -1),
                                n_w1_k[l].astype(BF16), w1cat(n_w1_k[l]), w2pad(n_w2_k[l]),
                                n_w1_v[l].astype(BF16), w1cat(n_w1_v[l]), w2pad(n_w2_v[l]))
        ks, vs, kw, vw = _relayout(p)
        y_n = _nsa(p, spieces, kcmp, vcmpt, ovt, ks, vs, kw, vw)
        h_res = _outproj(y_m, y_n, w_out, l, h_res, gate, final_g[None, :], l == depth - 1)
    return h_res.astype(out_dtype)
```

```python
import functools

import numpy as np
import jax
import jax.numpy as jnp
from jax import lax
from jax.experimental import pallas as pl
from jax.experimental.pallas import tpu as pltpu

F32 = jnp.float32
BF16 = jnp.bfloat16

EPS = 1e-6
M_HEADS = 4
M_HD = 256
M_WIDTH = M_HEADS * M_HD
CONV_K = 4
M_CHUNK = 256
N_HEADS = 16
N_HD = 64
N_KV = 4
N_HPG = N_HEADS // N_KV
N_WIDTH = N_HEADS * N_HD
KV_W = N_KV * N_HD
CMP_LEN = 32
CMP_STRIDE = 16
CMP_HIDDEN = 2 * N_HD
SLC_LEN = 64
SLC_TOPN = 16
WIN = 512
QBLK = 256
SKT = 256
WKT = 256
PAD_TILES = WIN // SKT
SEL_GRP = 8
KA_SEL = 128
KA_WIN = 128
VROWS = 80
UNROLL = 4
AUX_ROWS = 16
LOG2E = 1.4426950408889634

COL_MX, COL_MV, COL_MO, COL_MZ = 0, 1024, 2048, 3072
COL_NQ = 4096
COL_KC, COL_VC, COL_KS, COL_VS, COL_KW, COL_VW = 5120, 5376, 5632, 5888, 6144, 6400
COL_NZ = 6656
COL_GATES = 7680
GATE_NG = 2 * M_HEADS
NP_PAD = 8192
NORM_CHUNKS = 2
LANES = 128
NEG = -1e30
VMEM_LIMIT = 58 * 1024 * 1024


def _cparams(sem):
    return pltpu.CompilerParams(dimension_semantics=sem, vmem_limit_bytes=VMEM_LIMIT)


def _silu(x):
    return x * jax.nn.sigmoid(x)


def _log_sigmoid(x):
    return jnp.minimum(x, 0.0) - jnp.log1p(jnp.exp(-jnp.abs(x)))


def _ada_kernel(ct_ref, w_ref, b_ref, o_ref, *, bsz):
    s_t = _silu(ct_ref[...])
    w = w_ref[0]
    row = lax.broadcasted_iota(jnp.int32, o_ref.shape, 0)
    out = jnp.zeros(o_ref.shape, F32)
    for b in range(bsz):
        prod = w * s_t[:, b:b + 1]
        acc = prod[0:8]
        for r in range(1, prod.shape[0] // 8):
            acc = acc + prod[8 * r:8 * r + 8]
        out = jnp.where(row == b, jnp.sum(acc, axis=0, keepdims=True) + b_ref[...], out)
    o_ref[...] = out


def _ada(c_t, w, b, layer, bsz):
    _, d, n = w.shape
    tn = 1024
    return pl.pallas_call(
        functools.partial(_ada_kernel, bsz=bsz),
        grid=(n // tn,),
        in_specs=[pl.BlockSpec((d, 8), lambda j: (0, 0)),
                  pl.BlockSpec((1, d, tn), lambda j: (layer, 0, j)),
                  pl.BlockSpec((1, tn), lambda j: (0, j))],
        out_specs=pl.BlockSpec((8, tn), lambda j: (0, j)),
        out_shape=jax.ShapeDtypeStruct((8, n), F32),
        compiler_params=_cparams(("parallel",)),
        name="ada_mod",
    )(c_t, w, b)


def _inproj_kernel(x_ref, g_ref, sc_ref, sh_ref, w_ref, b_ref, o_ref, h_ref):
    first = pl.program_id(2) == 0

    @pl.when(first)
    def _():
        tm = x_ref.shape[1]
        ck = tm // NORM_CHUNKS
        for c in range(NORM_CHUNKS):
            rows = slice(c * ck, (c + 1) * ck)
            x = x_ref[0, rows, :]
            ms = jnp.mean(x * x, axis=-1, keepdims=True)
            h = x * lax.rsqrt(ms + EPS) * g_ref[...]
            h = (h * (1.0 + sc_ref[0]) + sh_ref[0]).astype(BF16)
            h_ref[rows, :] = h
            o_ref[0, rows, :] = jnp.dot(h, w_ref[...], preferred_element_type=F32) + b_ref[...]

    @pl.when(jnp.logical_not(first))
    def _():
        o_ref[0] = jnp.dot(h_ref[...], w_ref[...], preferred_element_type=F32) + b_ref[...]


def _inproj(x, g, scale, shift, w, b):
    bsz, t, d = x.shape
    n = w.shape[1]
    tm, tn = 1024, 2048
    return pl.pallas_call(
        _inproj_kernel,
        grid=(bsz, t // tm, n // tn),
        in_specs=[pl.BlockSpec((1, tm, d), lambda bi, i, j: (bi, i, 0)),
                  pl.BlockSpec((1, d), lambda bi, i, j: (0, 0)),
                  pl.BlockSpec((1, 1, d), lambda bi, i, j: (bi, 0, 0)),
                  pl.BlockSpec((1, 1, d), lambda bi, i, j: (bi, 0, 0)),
                  pl.BlockSpec((d, tn), lambda bi, i, j: (0, j)),
                  pl.BlockSpec((1, tn), lambda bi, i, j: (0, j))],
        out_specs=pl.BlockSpec((1, tm, tn), lambda bi, i, j: (bi, i, j)),
        out_shape=jax.ShapeDtypeStruct((bsz, t, n), F32),
        scratch_shapes=[pltpu.VMEM((tm, d), BF16)],
        compiler_params=_cparams(("parallel", "parallel", "arbitrary")),
        name="norm_inproj",
    )(x, g, scale, shift, w, b)


def _mlstm_kernel(x_ref, v_ref, o_ref, z_ref, gt_ref, cw_ref, cb_ref, wq_ref, wk_ref, nw_ref, sk_ref, fb_ref,
                  y_ref, c_scr, n_scr, m_scr, xp_scr):
    L = M_CHUNK

    @pl.when(pl.program_id(1) == 0)
    def _():
        c_scr[...] = jnp.zeros_like(c_scr)
        n_scr[...] = jnp.zeros_like(n_scr)
        m_scr[...] = jnp.zeros_like(m_scr)
        xp_scr[...] = jnp.zeros_like(xp_scr)

    x = x_ref[0]
    prev = xp_scr[...]
    row8 = lax.broadcasted_iota(jnp.int32, (8, M_WIDTH), 0)
    cw = cw_ref[...]
    xc = cb_ref[...] + x * cw[CONV_K - 1:CONV_K, :]
    for sft in range(1, CONV_K):
        xr = pltpu.roll(x, sft, 0)
        top = jnp.where(row8 < sft, pltpu.roll(prev, sft, 0), xr[0:8])
        xs = jnp.concatenate([top, xr[8:]], axis=0)
        xc = xc + xs * cw[CONV_K - 1 - sft:CONV_K - sft, :]
    xp_scr[...] = x[L - 8:L]
    xc = _silu(xc)

    gt = gt_ref[0]
    col = lax.broadcasted_iota(jnp.int32, (L, LANES), 1)
    logf = _log_sigmoid(gt + fb_ref[...])
    a_c = jnp.where((col >= M_HEADS) & (col < 2 * M_HEADS), logf, gt)
    ri = lax.broadcasted_iota(jnp.int32, (L, L), 0)
    ci = lax.broadcasted_iota(jnp.int32, (L, L), 1)
    causal = ri >= ci
    hp = lax.Precision.HIGHEST
    tri = causal.astype(F32)
    tri_t = (ri <= ci).astype(F32)
    b_c = jnp.dot(tri, a_c, precision=hp, preferred_element_type=F32)
    a_r = a_c.T
    b_r = jnp.dot(a_r[0:8], tri_t, precision=hp, preferred_element_type=F32)

    for h in range(M_HEADS):
        sl = slice(h * M_HD, (h + 1) * M_HD)
        xh = xc[:, sl]
        xb = xh.astype(BF16)
        q = jnp.dot(xb, wq_ref[h], preferred_element_type=F32)
        k = jnp.dot(xb, wk_ref[h], preferred_element_type=F32) * (M_HD ** -0.5)
        vb = v_ref[0, :, sl].astype(BF16)
        qb = q.astype(BF16)
        kb = k.astype(BF16)

        bt = b_c[:, M_HEADS + h:M_HEADS + h + 1]
        ic = a_c[:, h:h + 1]
        bs = b_r[M_HEADS + h:M_HEADS + h + 1, :]
        ir = a_r[h:h + 1, :]
        m_prev = m_scr[h][:, 0:1]

        dm = jnp.where(causal, bt - bs + ir, -jnp.inf)
        inter = bt + m_prev
        m_t = jnp.maximum(inter, jnp.max(dm, axis=-1, keepdims=True))
        w_in = jnp.exp(dm - m_t)
        w_st = jnp.exp(inter - m_t)
        s = lax.dot_general(qb, kb, (((1,), (1,)), ((), ())), preferred_element_type=F32) * w_in
        cmat = c_scr[h]
        nvec = n_scr[h]
        sb = s.astype(BF16)
        num = w_st * jnp.dot(qb, cmat.astype(BF16), preferred_element_type=F32) \
            + jnp.dot(sb, vb, preferred_element_type=F32)
        nt_dims = (((1,), (1,)), ((), ()))
        qn = lax.dot_general(qb, jnp.broadcast_to(nvec, (8, M_HD)).astype(BF16), nt_dims,
                             preferred_element_type=F32)[:, 0:1]
        ssum = lax.dot_general(sb, jnp.ones((8, L), BF16), nt_dims, preferred_element_type=F32)[:, 0:1]
        den = w_st * qn + ssum
        hh = num / jnp.maximum(jnp.abs(den), jnp.exp(-m_t))

        b_last = bt[L - 1:L, :]
        w_end = b_last - bt + ic
        m_new = jnp.maximum(b_last + m_prev, jnp.max(w_end, axis=0, keepdims=True))
        decay = jnp.exp(b_last + m_prev - m_new)
        kwt = k * jnp.exp(w_end - m_new)
        c_scr[h] = decay * cmat + lax.dot_general(kwt.astype(BF16), vb, (((0,), (0,)), ((), ())),
                                                  preferred_element_type=F32)
        n_scr[h] = decay * nvec + jnp.sum(kwt, axis=0, keepdims=True)
        m_scr[h] = jnp.broadcast_to(m_new, (1, LANES))

        mu = jnp.mean(hh, axis=-1, keepdims=True)
        hc = hh - mu
        var = jnp.mean(hc * hc, axis=-1, keepdims=True)
        hn = hc * lax.rsqrt(var + EPS) * nw_ref[:, sl]
        out = jax.nn.sigmoid(o_ref[0, :, sl]) * hn + sk_ref[:, sl] * xh
        y_ref[0, :, sl] = (out * _silu(z_ref[0, :, sl])).astype(BF16)


def _mlstm(p, conv_w, conv_b, wq, wk, norm_w, skip, fb_row):
    bsz, t, _ = p.shape
    L = M_CHUNK
    cb = lambda c: (lambda bi, i: (bi, i, c))
    full2 = lambda bi, i: (0, 0)
    full3 = lambda bi, i: (0, 0, 0)
    return pl.pallas_call(
        _mlstm_kernel,
        grid=(bsz, t // L),
        in_specs=[pl.BlockSpec((1, L, M_WIDTH), cb(COL_MX // M_WIDTH)),
                  pl.BlockSpec((1, L, M_WIDTH), cb(COL_MV // M_WIDTH)),
                  pl.BlockSpec((1, L, M_WIDTH), cb(COL_MO // M_WIDTH)),
                  pl.BlockSpec((1, L, M_WIDTH), cb(COL_MZ // M_WIDTH)),
                  pl.BlockSpec((1, L, LANES), cb(COL_GATES // LANES)),
                  pl.BlockSpec((CONV_K, M_WIDTH), full2),
                  pl.BlockSpec((1, M_WIDTH), full2),
                  pl.BlockSpec((M_HEADS, M_HD, M_HD), full3),
                  pl.BlockSpec((M_HEADS, M_HD, M_HD), full3),
                  pl.BlockSpec((1, M_WIDTH), full2),
                  pl.BlockSpec((1, M_WIDTH), full2),
                  pl.BlockSpec((1, LANES), full2)],
        out_specs=pl.BlockSpec((1, L, M_WIDTH), lambda bi, i: (bi, i, 0)),
        out_shape=jax.ShapeDtypeStruct((bsz, t, M_WIDTH), BF16),
        scratch_shapes=[pltpu.VMEM((M_HEADS, M_HD, M_HD), F32),
                        pltpu.VMEM((M_HEADS, 1, M_HD), F32),
                        pltpu.VMEM((M_HEADS, 1, LANES), F32),
                        pltpu.VMEM((8, M_WIDTH), F32)],
        compiler_params=_cparams(("parallel", "arbitrary")),
        name="mlstm_group",
    )(p, p, p, p, p, conv_w, conv_b, wq, wk, norm_w, skip, fb_row)


def _compress_kernel(kc0_ref, kc1_ref, vc0_ref, vc1_ref, posk_ref, posv_ref, w1k_ref, w1kc_ref, w2k_ref,
                     w1v_ref, w1vc_ref, w2v_ref, kcmp_ref, vcmpt_ref):
    nb = kcmp_ref.shape[2]

    def hidden(src_refs, pos_ref, w1_ref, w1c_ref):
        halves = [jnp.zeros((nb, 4 * CMP_HIDDEN), F32) for _ in src_refs]
        for l in range(CMP_STRIDE):
            for hf, r in enumerate(src_refs):
                xl = r[0, pl.ds(l, nb, stride=CMP_STRIDE), :]
                halves[hf] = halves[hf] + jnp.dot(xl.astype(BF16), w1c_ref[l], preferred_element_type=F32)
        acc = jnp.concatenate([hv[:, c * 2 * CMP_HIDDEN:(c + 1) * 2 * CMP_HIDDEN] for hv in halves for c in range(2)],
                              axis=0)
        first = acc[:, :CMP_HIDDEN]
        second = acc[:, CMP_HIDDEN:]
        posb = jnp.dot(jnp.broadcast_to(pos_ref[...], (8, CMP_LEN * N_HD)).astype(BF16), w1_ref[...],
                       preferred_element_type=F32)[0:1]
        hid = first + pltpu.roll(second, N_KV * nb - 1, 0) + posb
        return jax.nn.gelu(hid).astype(BF16)

    hk = hidden((kc0_ref, kc1_ref), posk_ref, w1k_ref, w1kc_ref)
    kc = jnp.dot(hk, w2k_ref[...], preferred_element_type=F32)
    hv = hidden((vc0_ref, vc1_ref), posv_ref, w1v_ref, w1vc_ref)
    vc = jnp.dot(hv, w2v_ref[...], preferred_element_type=F32)
    j = lax.broadcasted_iota(jnp.int32, (nb, 1), 0)
    lane = lax.broadcasted_iota(jnp.int32, (nb, N_HD), 1)
    n_real = (nb * CMP_STRIDE - CMP_LEN) // CMP_STRIDE + 1
    aux = jnp.where(lane < 3, (j * CMP_STRIDE).astype(F32),
                    jnp.where(lane < 6, (CMP_LEN - 1) * 0.5,
                              jnp.where((lane == 6) & (j >= n_real), 1.0, 0.0))).astype(BF16)
    ones_blk = jnp.where(lax.broadcasted_iota(jnp.int32, (VROWS - N_HD, nb), 0) == 0, 1.0, 0.0)
    for g in range(N_KV):
        kg = kc[g * nb:(g + 1) * nb, 0:N_HD]
        k_hi = kg.astype(BF16)
        k_lo = (kg - k_hi.astype(F32)).astype(BF16)
        kcmp_ref[0, g] = jnp.concatenate([k_hi, k_lo, k_hi, aux], axis=1)
        vt = vc[g * nb:(g + 1) * nb].T
        vcmpt_ref[0, g] = jnp.concatenate([vt[0:N_HD], ones_blk], axis=0).astype(BF16)


def _compress(p, posk, posv, w1k, w1kc, w2k, w1v, w1vc, w2v):
    bsz, t, _ = p.shape
    nb = t // CMP_STRIDE
    cb = lambda c: (lambda bi: (bi, 0, c))
    f2 = lambda bi: (0, 0)
    f3 = lambda bi: (0, 0, 0)
    wspecs = [pl.BlockSpec((CMP_LEN * N_HD, CMP_HIDDEN), f2),
              pl.BlockSpec((CMP_STRIDE, LANES, 4 * CMP_HIDDEN), f3),
              pl.BlockSpec((CMP_HIDDEN, LANES), f2)]
    return pl.pallas_call(
        _compress_kernel,
        grid=(bsz,),
        in_specs=[pl.BlockSpec((1, t, LANES), cb(COL_KC // LANES)),
                  pl.BlockSpec((1, t, LANES), cb(COL_KC // LANES + 1)),
                  pl.BlockSpec((1, t, LANES), cb(COL_VC // LANES)),
                  pl.BlockSpec((1, t, LANES), cb(COL_VC // LANES + 1)),
                  pl.BlockSpec((1, CMP_LEN * N_HD), f2),
                  pl.BlockSpec((1, CMP_LEN * N_HD), f2)] + wspecs + wspecs,
        out_specs=[pl.BlockSpec((1, N_KV, nb, 4 * N_HD), lambda bi: (bi, 0, 0, 0)),
                   pl.BlockSpec((1, N_KV, VROWS, nb), lambda bi: (bi, 0, 0, 0))],
        out_shape=[jax.ShapeDtypeStruct((bsz, N_KV, nb, 4 * N_HD), BF16),
                   jax.ShapeDtypeStruct((bsz, N_KV, VROWS, nb), BF16)],
        compiler_params=_cparams(("parallel",)),
        name="nsa_compress",
    )(p, p, p, p, posk, posv, w1k, w1kc, w2k, w1v, w1vc, w2v)


def _relayout_kernel(ks_ref, vs_ref, kw_ref, vw_ref, ksa_ref, vsa_ref, kwa_ref, vwa_ref):
    i = pl.program_id(1)
    is_pad = i == 0
    flag = jnp.where(is_pad, 1.0, 0.0)
    row = lax.broadcasted_iota(jnp.int32, (SKT, 1), 0)
    lane = lax.broadcasted_iota(jnp.int32, (SKT, N_HD), 1)
    al = lane - AUX_ROWS
    ones_blk = jnp.where(lax.broadcasted_iota(jnp.int32, (VROWS - N_HD, SKT), 0) == 0, 1.0, 0.0)
    for u in range(PAD_TILES):
        rows = slice(u * SKT, (u + 1) * SKT)
        base = ((i - 1) * PAD_TILES + u) * SKT
        pos = jnp.where(is_pad, 0, base + row)
        blk = pos // SLC_LEN
        p_hi = (blk * SLC_LEN).astype(F32)
        p_lo = (pos - blk * SLC_LEN).astype(F32)
        mid_w = jnp.where((al >= 0) & (al < 3), p_hi,
                          jnp.where((al >= 3) & (al < 6), p_lo, jnp.where(al == 6, flag, 0.0)))
        mid_s = jnp.where(lane == blk % SEL_GRP, 1.0, mid_w)
        ks = ks_ref[0, rows, :]
        kw = kw_ref[0, rows, :]
        for g in range(N_KV):
            sl = slice(g * N_HD, (g + 1) * N_HD)
            ksa_ref[0, g, rows, :] = jnp.concatenate([ks[:, sl], mid_s], axis=1).astype(BF16)
            kwa_ref[0, g, rows, :] = jnp.concatenate([kw[:, sl], mid_w], axis=1).astype(BF16)
        vst = vs_ref[0, rows, :].T
        vwt = vw_ref[0, rows, :].T
        for g in range(N_KV):
            sl = slice(g * N_HD, (g + 1) * N_HD)
            vsa_ref[0, g, u] = jnp.concatenate([vst[sl], ones_blk], axis=0).astype(BF16)
            vwg = jnp.concatenate([vwt[sl], ones_blk], axis=0).astype(BF16)
            for j in range(SKT // WKT):
                vwa_ref[0, g, u * (SKT // WKT) + j] = vwg[:, j * WKT:(j + 1) * WKT]


def _relayout(p):
    bsz, t, _ = p.shape
    nt = t // SKT + PAD_TILES
    wpt = SKT // WKT
    rt = PAD_TILES
    cb = lambda c: (lambda bi, i: (bi, jnp.maximum(i - 1, 0), c))
    return pl.pallas_call(
        _relayout_kernel,
        grid=(bsz, nt // rt),
        in_specs=[pl.BlockSpec((1, rt * SKT, KV_W), cb(COL_KS // KV_W)),
                  pl.BlockSpec((1, rt * SKT, KV_W), cb(COL_VS // KV_W)),
                  pl.BlockSpec((1, rt * SKT, KV_W), cb(COL_KW // KV_W)),
                  pl.BlockSpec((1, rt * SKT, KV_W), cb(COL_VW // KV_W))],
        out_specs=[pl.BlockSpec((1, N_KV, rt * SKT, KA_SEL), lambda bi, i: (bi, 0, i, 0)),
                   pl.BlockSpec((1, N_KV, rt, VROWS, SKT), lambda bi, i: (bi, 0, i, 0, 0)),
                   pl.BlockSpec((1, N_KV, rt * SKT, KA_WIN), lambda bi, i: (bi, 0, i, 0)),
                   pl.BlockSpec((1, N_KV, rt * wpt, VROWS, WKT), lambda bi, i: (bi, 0, i, 0, 0))],
        out_shape=[jax.ShapeDtypeStruct((bsz, N_KV, nt * SKT, KA_SEL), BF16),
                   jax.ShapeDtypeStruct((bsz, N_KV, nt, VROWS, SKT), BF16),
                   jax.ShapeDtypeStruct((bsz, N_KV, nt * SKT, KA_WIN), BF16),
                   jax.ShapeDtypeStruct((bsz, N_KV, nt * wpt, VROWS, WKT), BF16)],
        compiler_params=_cparams(("parallel", "parallel")),
        name="nsa_relayout",
    )(p, p, p, p)


def _nsa_batch_kernel(sp_ref, q_ref, gt_ref, z_ref, kcmp_ref, vcmpt_ref, ovt_ref, ks_ref, vs_ref, kw_ref,
                      vw_ref, tri_ref, wb_ref, y_ref, sa_scr, sb_scr, g_scr, imp_scr, rank_scr, part_scr, selb_scr,
                      tile_idx):
    g = pl.program_id(0)
    qb = pl.program_id(1)
    n_items = q_ref.shape[0]
    start = qb * QBLK
    nb = kcmp_ref.shape[2]
    n_slc = ovt_ref.shape[0]
    nw = N_HPG * QBLK
    n_top = min(SLC_TOPN, n_slc)
    t_row = start + lax.broadcasted_iota(jnp.int32, (1, QBLK), 1)
    s_bufs = (sa_scr, sb_scr)
    last = qb // (SKT // QBLK)

    def tile4(a):
        return jnp.concatenate([a] * N_HPG, axis=1)

    def col_reduce8(s, op):
        out = s[0:8]
        for r in range(1, s.shape[0] // 8):
            out = op(out, s[8 * r:8 * r + 8])
        return out

    def col_max(s):
        return jnp.max(col_reduce8(s, jnp.maximum), axis=0, keepdims=True)

    def normalise(acc):
        den = acc[N_HD:N_HD + 1]
        return acc[0:N_HD] / jnp.where(den > 0, den, 1.0)

    def gate_row(i, h, branch):
        return g_scr[i, pl.ds(GATE_NG + (g * N_HPG + h) * 3 + branch, 1), :]

    hl = lax.broadcasted_iota(jnp.int32, (AUX_ROWS, nw), 1) // QBLK
    ar = lax.broadcasted_iota(jnp.int32, (AUX_ROWS, nw), 0)

    def per_head(vals):
        out = jnp.full((AUX_ROWS, nw), vals[N_HPG - 1], F32)
        for h in range(N_HPG - 2, -1, -1):
            out = jnp.where(hl == h, vals[h], out)
        return out

    pieces = [per_head([sp_ref[(g * N_HPG + h) * 3 + j] for h in range(N_HPG)]) for j in range(3)]
    aux = jnp.where(ar == 6, NEG, 0.0)
    for j in range(3):
        aux = jnp.where((ar == j) | (ar == j + 3), pieces[j], aux)
    aux_b = aux.astype(BF16)
    q_tail = jnp.zeros((KA_SEL - N_HD - 2 * AUX_ROWS, nw), BF16)
    sidx = lax.broadcasted_iota(jnp.int32, (n_slc, QBLK), 0)
    cur = t_row // SLC_LEN
    valid = sidx <= cur
    sub8 = lax.broadcasted_iota(jnp.int32, (8, QBLK), 0)
    last_blk = (start + QBLK - 1) // SLC_LEN
    cmp_end = lax.broadcasted_iota(jnp.int32, (nb, 1), 0) * CMP_STRIDE + (CMP_LEN - 1)
    cmp_bias = tile4(jnp.where(cmp_end <= t_row, 0.0, NEG))
    wrows = WIN + QBLK

    def branches(i):
        qt = (q_ref[i] * (N_HD ** -0.5)).T
        q4l = jnp.concatenate([qt[h * N_HD:(h + 1) * N_HD] for h in range(N_HPG)], axis=1) * LOG2E
        q4s = q4l.astype(BF16)
        q4lo = (q4l - q4s.astype(F32)).astype(BF16)
        g_scr[i] = jax.nn.sigmoid(gt_ref[i]).T

        q_cmp = jnp.concatenate([q4s, q4s, q4lo, aux_b, jnp.zeros((N_HD - AUX_ROWS, nw), BF16)], axis=0)
        s_c = jnp.dot(kcmp_ref[i, 0], q_cmp, preferred_element_type=F32) + cmp_bias
        e_c = jnp.exp2(s_c - col_max(s_c))
        z_c = jnp.sum(col_reduce8(e_c, jnp.add), axis=0, keepdims=True)
        inv_c = jnp.where(tile4(t_row >= CMP_LEN - 1) & (z_c > 0), 1.0 / z_c, 0.0)
        o_c = jnp.dot(vcmpt_ref[i, 0], e_c.astype(BF16), preferred_element_type=F32)[0:N_HD] * inv_c
        p_c = e_c * inv_c
        psum = p_c[:, 0:QBLK]
        for h in range(1, N_HPG):
            psum = psum + p_c[:, h * QBLK:(h + 1) * QBLK]
        parts, rest = [], psum
        for _ in range(3):
            parts.append(rest.astype(BF16))
            rest = rest - parts[-1].astype(F32)
        imp = jnp.dot(ovt_ref[...], jnp.concatenate(parts, axis=0), preferred_element_type=F32)
        imp = jnp.where(valid, imp, -jnp.inf)
        imp_scr[i] = jnp.where((sidx == 0) | (sidx == cur), jnp.inf, imp)

        q_win = jnp.concatenate([q4s, jnp.zeros((AUX_ROWS, nw), BF16), aux_b,
                                 jnp.zeros((KA_WIN - N_HD - 2 * AUX_ROWS, nw), BF16)], axis=0)
        s_w = jnp.dot(kw_ref[i, 0, pl.ds(pl.multiple_of(start, QBLK), wrows), :], q_win,
                      preferred_element_type=F32)
        s_w = jnp.concatenate([s_w[0:WKT] + tile4(wb_ref[0]), s_w[WKT:wrows - WKT],
                               s_w[wrows - WKT:] + tile4(wb_ref[1])], axis=0)
        p_w = jnp.exp2(s_w - col_max(s_w)).astype(BF16)
        v_w = jnp.concatenate([vw_ref[i, 0, qb * (QBLK // WKT) + j] for j in range(wrows // WKT)], axis=1)
        o_w = normalise(jnp.dot(v_w, p_w, preferred_element_type=F32))
        for h in range(N_HPG):
            hs = slice(h * QBLK, (h + 1) * QBLK)
            part_scr[i, :, hs] = gate_row(i, h, 0) * o_c[:, hs] + gate_row(i, h, 2) * o_w[:, hs]
        return q4s

    q4 = [branches(i) for i in range(n_items)]

    rank_scr[...] = jnp.zeros_like(rank_scr)
    for ri in range(n_slc // 8):
        @pl.when((last_blk >= n_top) & (8 * ri <= last_blk))
        def _():
            for i in range(n_items):
                rows = imp_scr[i, 8 * ri:8 * ri + 8]
                for r in range(n_slc // 8):
                    blk8 = imp_scr[i, 8 * r:8 * r + 8]
                    acc = rank_scr[i, 8 * r:8 * r + 8]
                    for ii in range(8):
                        row = rows[ii:ii + 1]
                        if ri < r:
                            before = row >= blk8
                        elif ri > r:
                            before = row > blk8
                        else:
                            before = (row > blk8) | ((row == blk8) & (sub8 > ii))
                        acc = acc + jnp.where(before, 1.0, 0.0)
                    rank_scr[i, 8 * r:8 * r + 8] = acc

    def select(i):
        chosen = (rank_scr[i] < n_top) & valid
        selb_scr[i] = tile4(jnp.where(chosen, 0.0, NEG))

        any_q = jnp.max(jnp.where(chosen, 1.0, 0.0), axis=1, keepdims=True) > 0
        s_io = lax.broadcasted_iota(jnp.int32, (n_slc, 1), 0)
        bits = jnp.where(any_q, jnp.left_shift(1, s_io % 32), 0)
        words = [jnp.sum(jnp.where(s_io // 32 == w, bits, 0)) for w in range((n_slc + 31) // 32)]
        bpt = SKT // SLC_LEN
        cnt = jnp.int32(0)
        for tile in range(n_slc // bpt):
            nib = lax.shift_right_logical(words[tile * bpt // 32], jnp.int32(tile * bpt % 32)) & (2 ** bpt - 1)
            tile_idx[i, cnt] = tile
            cnt = cnt + ((nib != 0) & (tile < last)).astype(jnp.int32)
        return cnt

    n_act = [select(i) for i in range(n_items)]


    def sel_scores(i, kt):
        grp = pl.multiple_of((kt * SKT // SLC_LEN) // SEL_GRP * SEL_GRP, SEL_GRP)
        sel_rows = jnp.concatenate([selb_scr[i, pl.ds(grp, SEL_GRP), :], jnp.zeros((AUX_ROWS - SEL_GRP, nw), F32)],
                                   axis=0)
        q_sel = jnp.concatenate([q4[i], sel_rows.astype(BF16), aux_b, q_tail], axis=0)
        krow = pl.multiple_of((kt + PAD_TILES) * SKT, SKT)
        return jnp.dot(ks_ref[i, 0, pl.ds(krow, SKT), :], q_sel, preferred_element_type=F32)

    def tile_rows(kt):
        return pl.ds(pl.multiple_of(kt * SKT, SKT), SKT)

    def pass1(i):
        def fn(kt, mrun):
            s = sel_scores(i, kt)
            s_bufs[i % 2][tile_rows(kt), :] = s
            return jnp.maximum(mrun, col_reduce8(s, jnp.maximum))
        return fn

    def pass2(i, m_sel):
        def fn(kt, acc):
            pr = jnp.exp2(s_bufs[i % 2][tile_rows(kt), :] - m_sel).astype(BF16)
            return acc + jnp.dot(vs_ref[i, 0, kt + PAD_TILES], pr, preferred_element_type=F32)
        return fn

    def listed(i, fn):
        return lambda j, carry: fn(tile_idx[i, j], carry)

    def both(f1, f2):
        def fn(j, carry):
            return f1(j, carry[0]), f2(j, carry[1])
        return fn

    def run_tiles(fn, lo, n, init):
        def group(size, first):
            def body(j, carry):
                for u in range(size):
                    carry = fn(first + j * size + u, carry)
                return carry
            return body

        n_u = n // UNROLL
        carry = lax.fori_loop(0, n_u, group(UNROLL, lo), init)
        done = lo + n_u * UNROLL
        size = UNROLL // 2
        while size >= 1:
            take = (n // size) % 2
            carry = lax.fori_loop(0, take, group(size, done), carry)
            done = done + take * size
            size //= 2
        return carry

    def diag_tile(i, mrun):
        s = sel_scores(i, last) + tile4(tri_ref[qb % (SKT // QBLK)])
        s_bufs[i % 2][tile_rows(last), :] = s
        return jnp.max(jnp.maximum(mrun, col_reduce8(s, jnp.maximum)), axis=0, keepdims=True)

    mrun0 = jnp.full((8, nw), NEG, F32)
    acc0 = jnp.zeros((VROWS, nw), F32)
    m_prev = diag_tile(0, run_tiles(listed(0, pass1(0)), 0, n_act[0], mrun0))
    outs = []
    for i in range(1, n_items):
        p2 = pass2(i - 1, m_prev)
        f1, f2 = listed(i, pass1(i)), listed(i - 1, p2)
        n_both = jnp.minimum(n_act[i], n_act[i - 1])
        mrun, acc = run_tiles(both(f1, f2), 0, n_both, (mrun0, acc0))
        mrun = run_tiles(f1, n_both, n_act[i] - n_both, mrun)
        acc = run_tiles(f2, n_both, n_act[i - 1] - n_both, acc)
        m_cur = diag_tile(i, mrun)
        outs.append(normalise(p2(last, acc)))
        m_prev = m_cur
    p2 = pass2(n_items - 1, m_prev)
    acc = run_tiles(listed(n_items - 1, p2), 0, n_act[n_items - 1], acc0)
    outs.append(normalise(p2(last, acc)))

    for i in range(n_items):
        tot = [part_scr[i, :, h * QBLK:(h + 1) * QBLK] + gate_row(i, h, 1) * outs[i][:, h * QBLK:(h + 1) * QBLK]
               for h in range(N_HPG)]
        o = jnp.concatenate(tot, axis=0).T
        y_ref[i] = (o * _silu(z_ref[i])).astype(BF16)


def _edge_biases():
    kl = np.arange(SKT)[:, None]
    ql = np.arange(QBLK)[None, :]
    tri = np.stack([np.where(kl <= par * QBLK + ql, 0.0, NEG) for par in range(SKT // QBLK)])
    kk = np.arange(WKT)[:, None]
    wb = np.stack([np.where(kk > ql, 0.0, NEG), np.where(kk <= ql + WKT - QBLK, 0.0, NEG)])
    return jnp.asarray(tri, F32), jnp.asarray(wb, F32)


def _nsa(p, spieces, kcmp, vcmpt, ovt, ks, vs, kw, vw):
    bsz, t, _ = p.shape
    n_slc = t // SLC_LEN
    gw = N_HPG * N_HD
    tri, wb = _edge_biases()
    nw = N_HPG * QBLK
    per_g = lambda a: pl.BlockSpec((bsz, 1) + a.shape[2:], lambda g, i, sp: (0, g) + (0,) * (a.ndim - 2))
    const = lambda a: pl.BlockSpec(a.shape, lambda g, i, sp: (0,) * a.ndim)
    grid_spec = pltpu.PrefetchScalarGridSpec(
        num_scalar_prefetch=1,
        grid=(N_KV, t // QBLK),
        in_specs=[pl.BlockSpec((bsz, QBLK, gw), lambda g, i, sp: (0, i, COL_NQ // gw + g)),
                  pl.BlockSpec((bsz, QBLK, LANES), lambda g, i, sp: (0, i, COL_GATES // LANES)),
                  pl.BlockSpec((bsz, QBLK, gw), lambda g, i, sp: (0, i, COL_NZ // gw + g)),
                  per_g(kcmp), per_g(vcmpt), const(ovt),
                  per_g(ks), per_g(vs), per_g(kw), per_g(vw), const(tri), const(wb)],
        out_specs=pl.BlockSpec((bsz, QBLK, gw), lambda g, i, sp: (0, i, g)),
        scratch_shapes=[pltpu.VMEM((t, nw), F32),
                        pltpu.VMEM((t, nw), F32),
                        pltpu.VMEM((bsz, LANES, QBLK), F32),
                        pltpu.VMEM((bsz, n_slc, QBLK), F32),
                        pltpu.VMEM((bsz, n_slc, QBLK), F32),
                        pltpu.VMEM((bsz, N_HD, nw), F32),
                        pltpu.VMEM((bsz, n_slc, nw), F32),
                        pltpu.SMEM((bsz, t // SKT), jnp.int32)],
    )
    return pl.pallas_call(
        _nsa_batch_kernel,
        grid_spec=grid_spec,
        out_shape=jax.ShapeDtypeStruct((bsz, t, N_WIDTH), BF16),
        compiler_params=_cparams(("arbitrary", "arbitrary")),
        name="nsa_attention",
    )(spieces, p, p, p, kcmp, vcmpt, ovt, ks, vs, kw, vw, tri, wb)


def _outproj_kernel(ym_ref, yn_ref, w_ref, x_ref, gate_ref, fg_ref, o_ref, wb_scr, *, final):
    @pl.when((pl.program_id(0) == 0) & (pl.program_id(1) == 0))
    def _():
        rows = w_ref.shape[1]
        for c in range(rows // WP_TILE):
            cs = slice(c * WP_TILE, (c + 1) * WP_TILE)
            wb_scr[cs, :] = w_ref[0, cs, :].astype(BF16)

    y = jnp.dot(ym_ref[0], wb_scr[0:M_WIDTH, :], preferred_element_type=F32)
    y = y + jnp.dot(yn_ref[0], wb_scr[M_WIDTH:, :], preferred_element_type=F32)
    hres = x_ref[0] + gate_ref[0] * y
    if final:
        ms = jnp.mean(hres * hres, axis=-1, keepdims=True)
        hres = hres * lax.rsqrt(ms + EPS) * fg_ref[...]
    o_ref[0] = hres


def _outproj(ym, yn, w, layer, x, gate, fg, final):
    bsz, t, d = x.shape
    tm = 512
    return pl.pallas_call(
        functools.partial(_outproj_kernel, final=final),
        grid=(bsz, t // tm),
        in_specs=[pl.BlockSpec((1, tm, M_WIDTH), lambda bi, i: (bi, i, 0)),
                  pl.BlockSpec((1, tm, N_WIDTH), lambda bi, i: (bi, i, 0)),
                  pl.BlockSpec((1, M_WIDTH + N_WIDTH, d), lambda bi, i: (layer, 0, 0),
                               pipeline_mode=pl.Buffered(1)),
                  pl.BlockSpec((1, tm, d), lambda bi, i: (bi, i, 0)),
                  pl.BlockSpec((1, 1, d), lambda bi, i: (bi, 0, 0)),
                  pl.BlockSpec((1, d), lambda bi, i: (0, 0))],
        out_specs=pl.BlockSpec((1, tm, d), lambda bi, i: (bi, i, 0)),
        out_shape=jax.ShapeDtypeStruct((bsz, t, d), F32),
        scratch_shapes=[pltpu.VMEM((M_WIDTH + N_WIDTH, d), BF16)],
        compiler_params=_cparams(("arbitrary", "arbitrary")),
        name="outproj_residual",
    )(ym, yn, w, x, gate, fg)


SRC_MI = 4 * M_WIDTH
SRC_NQ = SRC_MI + 2 * M_HEADS
SRC_NG = SRC_NQ + N_WIDTH + 6 * KV_W
SRC_NZ = SRC_NG + 3 * N_HEADS


def _reorder_cols(a):
    parts = [a[..., 0:SRC_MI], a[..., SRC_NQ:SRC_NG], a[..., SRC_NZ:SRC_NZ + N_WIDTH], a[..., SRC_MI:SRC_NQ],
             a[..., SRC_NG:SRC_NZ]]
    used = sum(x.shape[-1] for x in parts)
    parts.append(jnp.zeros(a.shape[:-1] + (NP_PAD - used,), a.dtype))
    return jnp.concatenate(parts, axis=-1)


WP_TILE = 512
WP_GATE_TILE = COL_GATES // WP_TILE


def _wprep_kernel(w_ref, g1_ref, g2_ref, o_ref):
    j = pl.program_id(0)
    d = o_ref.shape[0]

    @pl.when(j < WP_GATE_TILE)
    def _():
        for c in range(d // WP_TILE):
            cs = slice(c * WP_TILE, (c + 1) * WP_TILE)
            o_ref[cs, :] = w_ref[0, :, cs].T.astype(BF16)

    @pl.when(j == WP_GATE_TILE)
    def _():
        n_gate = g1_ref.shape[1] + g2_ref.shape[1]
        gt = jnp.concatenate([g1_ref[0], g2_ref[0], jnp.zeros((LANES - n_gate, d), F32)], axis=0)
        for c in range(d // WP_TILE):
            cs = slice(c * WP_TILE, (c + 1) * WP_TILE)
            o_ref[cs, 0:LANES] = gt[:, cs].T.astype(BF16)
        o_ref[:, LANES:] = jnp.zeros((d, WP_TILE - LANES), BF16)


def _wprep(w_t, layer):
    _, n, d = w_t.shape
    n_big = COL_GATES // WP_TILE
    assert n == SRC_NZ + N_WIDTH and NP_PAD // WP_TILE == n_big + 1

    def src_row(j):
        return jnp.where(j < COL_NQ // WP_TILE, j * WP_TILE,
                         jnp.where(j < COL_NZ // WP_TILE, SRC_NQ + (j - COL_NQ // WP_TILE) * WP_TILE,
                                   jnp.where(j < n_big, SRC_NZ + (j - COL_NZ // WP_TILE) * WP_TILE, 0)))

    el = pl.Element
    return pl.pallas_call(
        _wprep_kernel,
        grid=(NP_PAD // WP_TILE,),
        in_specs=[pl.BlockSpec((el(1), el(WP_TILE), el(d)), lambda j: (layer, pl.multiple_of(src_row(j), 8), 0)),
                  pl.BlockSpec((el(1), el(SRC_NQ - SRC_MI), el(d)), lambda j: (layer, SRC_MI, 0)),
                  pl.BlockSpec((el(1), el(SRC_NZ - SRC_NG), el(d)), lambda j: (layer, SRC_NG, 0))],
        out_specs=pl.BlockSpec((d, WP_TILE), lambda j: (0, j)),
        out_shape=jax.ShapeDtypeStruct((d, NP_PAD), BF16),
        compiler_params=_cparams(("parallel",)),
        name="inproj_weight_prep",
    )(w_t, w_t, w_t)


def _overlap_t(t):
    n_cmp_rows = t // CMP_STRIDE
    n_slc = t // SLC_LEN
    c0 = np.arange(n_cmp_rows) * CMP_STRIDE
    s0 = np.arange(n_slc) * SLC_LEN
    ov = (c0[None, :] <= s0[:, None] + SLC_LEN - 1) & (c0[None, :] + CMP_LEN - 1 >= s0[:, None])
    ov[:, (t - CMP_LEN) // CMP_STRIDE + 1:] = False
    return jnp.asarray(np.concatenate([ov] * 3, axis=1), BF16)


def kernel(x, c, ln_g, w_ada, b_ada, w_in, b_in, m_conv_w, m_conv_b, m_wq, m_wk, m_norm_w, m_skip, m_f_bias,
           n_pos_k, n_pos_v, n_w1_k, n_w2_k, n_w1_v, n_w2_v, w_out, final_g):
    out_dtype = x.dtype
    bsz, t, d = x.shape
    depth = ln_g.shape[0]
    h_res = x.astype(F32)
    assert bsz <= 2 and t % 1024 == 0 and t // CMP_STRIDE <= 256
    c_t =jnp.zeros((d, 8), F32).at[:, :bsz].set(c.astype(F32).T)
    slopes_np = np.array([2.0 ** (-8.0 * (h + 1) / N_HEADS) for h in range(N_HEADS)], np.float32)
    rest = (slopes_np.astype(np.float64) * LOG2E).astype(np.float32)
    pieces = []
    for _ in range(3):
        pieces.append(rest.astype(jnp.bfloat16).astype(np.float32))
        rest = rest - pieces[-1]
    spieces = jnp.asarray(np.stack(pieces, axis=1).reshape(-1))
    ovt = _overlap_t(t)

    def w1cat(w1):
        w = w1.reshape(2, CMP_STRIDE, N_HD, CMP_HIDDEN)
        w = jnp.concatenate([w[0], w[1]], axis=-1).astype(BF16)
        z = jnp.zeros_like(w)
        return jnp.concatenate([jnp.concatenate([w, z], axis=-1), jnp.concatenate([z, w], axis=-1)], axis=1)

    def w2pad(w2):
        return jnp.pad(w2, ((0, 0), (0, LANES - N_HD))).astype(BF16)

    for l in range(depth):
        mod = _ada(c_t, w_ada, b_ada[l][None, :], l, bsz)[:bsz]
        shift, scale, gate = mod[:, None, 0:d], mod[:, None, d:2 * d], mod[:, None, 2 * d:3 * d]
        p = _inproj(h_res, ln_g[l][None, :], scale, shift,
                    _wprep(jnp.swapaxes(w_in, 1, 2), l), _reorder_cols(b_in[l])[None, :])
        fb_row = jnp.zeros((1, LANES), F32).at[0, M_HEADS:2 * M_HEADS].set(m_f_bias[l])
        y_m = _mlstm(p, m_conv_w[l], m_conv_b[l][None, :], m_wq[l].astype(BF16), m_wk[l].astype(BF16),
                     m_norm_w[l][None, :], m_skip[l][None, :], fb_row)
        kcmp, vcmpt = _compress(p, n_pos_k[l].reshape(1, -1), n_pos_v[l].reshape(1, -1),
                                n_w1_k[l].astype(BF16), w1cat(n_w1_k[l]), w2pad(n_w2_k[l]),
                                n_w1_v[l].astype(BF16), w1cat(n_w1_v[l]), w2pad(n_w2_v[l]))
        ks, vs, kw, vw = _relayout(p)
        y_n = _nsa(p, spieces, kcmp, vcmpt, ovt, ks, vs, kw, vw)
        h_res = _outproj(y_m, y_n, w_out, l, h_res, gate, final_g[None, :], l == depth - 1)
    return h_res.astype(out_dtype)
```

```python
import functools

import numpy as np
import jax
import jax.numpy as jnp
from jax import lax
from jax.experimental import pallas as pl
from jax.experimental.pallas import tpu as pltpu

F32 = jnp.float32
BF16 = jnp.bfloat16

EPS = 1e-6
M_HEADS = 4
M_HD = 256
M_WIDTH = M_HEADS * M_HD
CONV_K = 4
M_CHUNK = 256
N_HEADS = 16
N_HD = 64
N_KV = 4
N_HPG = N_HEADS // N_KV
N_WIDTH = N_HEADS * N_HD
KV_W = N_KV * N_HD
CMP_LEN = 32
CMP_STRIDE = 16
CMP_HIDDEN = 2 * N_HD
SLC_LEN = 64
SLC_TOPN = 16
WIN = 512
QBLK = 256
SKT = 256
WKT = 256
PAD_TILES = WIN // SKT
SEL_GRP = 8
KA_SEL = 128
KA_WIN = 128
VROWS = 80
UNROLL = 4
AUX_ROWS = 16
LOG2E = 1.4426950408889634

COL_MX, COL_MV, COL_MO, COL_MZ = 0, 1024, 2048, 3072
COL_NQ = 4096
COL_KC, COL_VC, COL_KS, COL_VS, COL_KW, COL_VW = 5120, 5376, 5632, 5888, 6144, 6400
COL_NZ = 6656
COL_GATES = 7680
GATE_NG = 2 * M_HEADS
NP_PAD = 8192
NORM_CHUNKS = 2
LANES = 128
NEG = -1e30
VMEM_LIMIT = 58 * 1024 * 1024


def _cparams(sem):
    return pltpu.CompilerParams(dimension_semantics=sem, vmem_limit_bytes=VMEM_LIMIT)


def _silu(x):
    return x * jax.nn.sigmoid(x)


def _log_sigmoid(x):
    return jnp.minimum(x, 0.0) - jnp.log1p(jnp.exp(-jnp.abs(x)))


def _ada_kernel(ct_ref, w_ref, b_ref, o_ref, *, bsz):
    s_t = _silu(ct_ref[...])
    w = w_ref[0]
    row = lax.broadcasted_iota(jnp.int32, o_ref.shape, 0)
    out = jnp.zeros(o_ref.shape, F32)
    for b in range(bsz):
        prod = w * s_t[:, b:b + 1]
        acc = prod[0:8]
        for r in range(1, prod.shape[0] // 8):
            acc = acc + prod[8 * r:8 * r + 8]
        out = jnp.where(row == b, jnp.sum(acc, axis=0, keepdims=True) + b_ref[...], out)
    o_ref[...] = out


def _ada(c_t, w, b, layer, bsz):
    _, d, n = w.shape
    tn = 1024
    return pl.pallas_call(
        functools.partial(_ada_kernel, bsz=bsz),
        grid=(n // tn,),
        in_specs=[pl.BlockSpec((d, 8), lambda j: (0, 0)),
                  pl.BlockSpec((1, d, tn), lambda j: (layer, 0, j)),
                  pl.BlockSpec((1, tn), lambda j: (0, j))],
        out_specs=pl.BlockSpec((8, tn), lambda j: (0, j)),
        out_shape=jax.ShapeDtypeStruct((8, n), F32),
        compiler_params=_cparams(("parallel",)),
        name="ada_mod",
    )(c_t, w, b)


def _inproj_kernel(x_ref, g_ref, sc_ref, sh_ref, w_ref, b_ref, o_ref, h_ref):
    first = pl.program_id(2) == 0

    @pl.when(first)
    def _():
        tm = x_ref.shape[1]
        ck = tm // NORM_CHUNKS
        for c in range(NORM_CHUNKS):
            rows = slice(c * ck, (c + 1) * ck)
            x = x_ref[0, rows, :]
            ms = jnp.mean(x * x, axis=-1, keepdims=True)
            h = x * lax.rsqrt(ms + EPS) * g_ref[...]
            h = (h * (1.0 + sc_ref[0]) + sh_ref[0]).astype(BF16)
            h_ref[rows, :] = h
            o_ref[0, rows, :] = jnp.dot(h, w_ref[...], preferred_element_type=F32) + b_ref[...]

    @pl.when(jnp.logical_not(first))
    def _():
        o_ref[0] = jnp.dot(h_ref[...], w_ref[...], preferred_element_type=F32) + b_ref[...]


def _inproj(x, g, scale, shift, w, b):
    bsz, t, d = x.shape
    n = w.shape[1]
    tm, tn = 1024, 2048
    return pl.pallas_call(
        _inproj_kernel,
        grid=(bsz, t // tm, n // tn),
        in_specs=[pl.BlockSpec((1, tm, d), lambda bi, i, j: (bi, i, 0)),
                  pl.BlockSpec((1, d), lambda bi, i, j: (0, 0)),
                  pl.BlockSpec((1, 1, d), lambda bi, i, j: (bi, 0, 0)),
                  pl.BlockSpec((1, 1, d), lambda bi, i, j: (bi, 0, 0)),
                  pl.BlockSpec((d, tn), lambda bi, i, j: (0, j)),
                  pl.BlockSpec((1, tn), lambda bi, i, j: (0, j))],
        out_specs=pl.BlockSpec((1, tm, tn), lambda bi, i, j: (bi, i, j)),
        out_shape=jax.ShapeDtypeStruct((bsz, t, n), F32),
        scratch_shapes=[pltpu.VMEM((tm, d), BF16)],
        compiler_params=_cparams(("parallel", "parallel", "arbitrary")),
        name="norm_inproj",
    )(x, g, scale, shift, w, b)


def _mlstm_kernel(x_ref, v_ref, o_ref, z_ref, gt_ref, cw_ref, cb_ref, wq_ref, wk_ref, nw_ref, sk_ref, fb_ref,
                  y_ref, c_scr, n_scr, m_scr, xp_scr):
    L = M_CHUNK

    @pl.when(pl.program_id(1) == 0)
    def _():
        c_scr[...] = jnp.zeros_like(c_scr)
        n_scr[...] = jnp.zeros_like(n_scr)
        m_scr[...] = jnp.zeros_like(m_scr)
        xp_scr[...] = jnp.zeros_like(xp_scr)

    x = x_ref[0]
    prev = xp_scr[...]
    row8 = lax.broadcasted_iota(jnp.int32, (8, M_WIDTH), 0)
    cw = cw_ref[...]
    xc = cb_ref[...] + x * cw[CONV_K - 1:CONV_K, :]
    for sft in range(1, CONV_K):
        xr = pltpu.roll(x, sft, 0)
        top = jnp.where(row8 < sft, pltpu.roll(prev, sft, 0), xr[0:8])
        xs = jnp.concatenate([top, xr[8:]], axis=0)
        xc = xc + xs * cw[CONV_K - 1 - sft:CONV_K - sft, :]
    xp_scr[...] = x[L - 8:L]
    xc = _silu(xc)

    gt = gt_ref[0]
    col = lax.broadcasted_iota(jnp.int32, (L, LANES), 1)
    logf = _log_sigmoid(gt + fb_ref[...])
    a_c = jnp.where((col >= M_HEADS) & (col < 2 * M_HEADS), logf, gt)
    ri = lax.broadcasted_iota(jnp.int32, (L, L), 0)
    ci = lax.broadcasted_iota(jnp.int32, (L, L), 1)
    causal = ri >= ci
    hp = lax.Precision.HIGHEST
    tri = causal.astype(F32)
    tri_t = (ri <= ci).astype(F32)
    b_c = jnp.dot(tri, a_c, precision=hp, preferred_element_type=F32)
    a_r = a_c.T
    b_r = jnp.dot(a_r[0:8], tri_t, precision=hp, preferred_element_type=F32)

    for h in range(M_HEADS):
        sl = slice(h * M_HD, (h + 1) * M_HD)
        xh = xc[:, sl]
        xb = xh.astype(BF16)
        q = jnp.dot(xb, wq_ref[h], preferred_element_type=F32)
        k = jnp.dot(xb, wk_ref[h], preferred_element_type=F32) * (M_HD ** -0.5)
        vb = v_ref[0, :, sl].astype(BF16)
        qb = q.astype(BF16)
        kb = k.astype(BF16)

        bt = b_c[:, M_HEADS + h:M_HEADS + h + 1]
        ic = a_c[:, h:h + 1]
        bs = b_r[M_HEADS + h:M_HEADS + h + 1, :]
        ir = a_r[h:h + 1, :]
        m_prev = m_scr[h][:, 0:1]

        dm = jnp.where(causal, bt - bs + ir, -jnp.inf)
        inter = bt + m_prev
        m_t = jnp.maximum(inter, jnp.max(dm, axis=-1, keepdims=True))
        w_in = jnp.exp(dm - m_t)
        w_st = jnp.exp(inter - m_t)
        s = lax.dot_general(qb, kb, (((1,), (1,)), ((), ())), preferred_element_type=F32) * w_in
        cmat = c_scr[h]
        nvec = n_scr[h]
        sb = s.astype(BF16)
        num = w_st * jnp.dot(qb, cmat.astype(BF16), preferred_element_type=F32) \
            + jnp.dot(sb, vb, preferred_element_type=F32)
        nt_dims = (((1,), (1,)), ((), ()))
        qn = lax.dot_general(qb, jnp.broadcast_to(nvec, (8, M_HD)).astype(BF16), nt_dims,
                             preferred_element_type=F32)[:, 0:1]
        ssum = lax.dot_general(sb, jnp.ones((8, L), BF16), nt_dims, preferred_element_type=F32)[:, 0:1]
        den = w_st * qn + ssum
        hh = num / jnp.maximum(jnp.abs(den), jnp.exp(-m_t))

        b_last = bt[L - 1:L, :]
        w_end = b_last - bt + ic
        m_new = jnp.maximum(b_last + m_prev, jnp.max(w_end, axis=0, keepdims=True))
        decay = jnp.exp(b_last + m_prev - m_new)
        kwt = k * jnp.exp(w_end - m_new)
        c_scr[h] = decay * cmat + lax.dot_general(kwt.astype(BF16), vb, (((0,), (0,)), ((), ())),
                                                  preferred_element_type=F32)
        n_scr[h] = decay * nvec + jnp.sum(kwt, axis=0, keepdims=True)
        m_scr[h] = jnp.broadcast_to(m_new, (1, LANES))

        mu = jnp.mean(hh, axis=-1, keepdims=True)
        hc = hh - mu
        var = jnp.mean(hc * hc, axis=-1, keepdims=True)
        hn = hc * lax.rsqrt(var + EPS) * nw_ref[:, sl]
        out = jax.nn.sigmoid(o_ref[0, :, sl]) * hn + sk_ref[:, sl] * xh
        y_ref[0, :, sl] = (out * _silu(z_ref[0, :, sl])).astype(BF16)


def _mlstm(p, conv_w, conv_b, wq, wk, norm_w, skip, fb_row):
    bsz, t, _ = p.shape
    L = M_CHUNK
    cb = lambda c: (lambda bi, i: (bi, i, c))
    full2 = lambda bi, i: (0, 0)
    full3 = lambda bi, i: (0, 0, 0)
    return pl.pallas_call(
        _mlstm_kernel,
        grid=(bsz, t // L),
        in_specs=[pl.BlockSpec((1, L, M_WIDTH), cb(COL_MX // M_WIDTH)),
                  pl.BlockSpec((1, L, M_WIDTH), cb(COL_MV // M_WIDTH)),
                  pl.BlockSpec((1, L, M_WIDTH), cb(COL_MO // M_WIDTH)),
                  pl.BlockSpec((1, L, M_WIDTH), cb(COL_MZ // M_WIDTH)),
                  pl.BlockSpec((1, L, LANES), cb(COL_GATES // LANES)),
                  pl.BlockSpec((CONV_K, M_WIDTH), full2),
                  pl.BlockSpec((1, M_WIDTH), full2),
                  pl.BlockSpec((M_HEADS, M_HD, M_HD), full3),
                  pl.BlockSpec((M_HEADS, M_HD, M_HD), full3),
                  pl.BlockSpec((1, M_WIDTH), full2),
                  pl.BlockSpec((1, M_WIDTH), full2),
                  pl.BlockSpec((1, LANES), full2)],
        out_specs=pl.BlockSpec((1, L, M_WIDTH), lambda bi, i: (bi, i, 0)),
        out_shape=jax.ShapeDtypeStruct((bsz, t, M_WIDTH), BF16),
        scratch_shapes=[pltpu.VMEM((M_HEADS, M_HD, M_HD), F32),
                        pltpu.VMEM((M_HEADS, 1, M_HD), F32),
                        pltpu.VMEM((M_HEADS, 1, LANES), F32),
                        pltpu.VMEM((8, M_WIDTH), F32)],
        compiler_params=_cparams(("parallel", "arbitrary")),
        name="mlstm_group",
    )(p, p, p, p, p, conv_w, conv_b, wq, wk, norm_w, skip, fb_row)


def _compress_kernel(kc0_ref, kc1_ref, vc0_ref, vc1_ref, posk_ref, posv_ref, w1k_ref, w1kc_ref, w2k_ref,
                     w1v_ref, w1vc_ref, w2v_ref, kcmp_ref, vcmpt_ref):
    nb = kcmp_ref.shape[2]

    def hidden(src_refs, pos_ref, w1_ref, w1c_ref):
        halves = [jnp.zeros((nb, 4 * CMP_HIDDEN), F32) for _ in src_refs]
        for l in range(CMP_STRIDE):
            for hf, r in enumerate(src_refs):
                xl = r[0, pl.ds(l, nb, stride=CMP_STRIDE), :]
                halves[hf] = halves[hf] + jnp.dot(xl.astype(BF16), w1c_ref[l], preferred_element_type=F32)
        acc = jnp.concatenate([hv[:, c * 2 * CMP_HIDDEN:(c + 1) * 2 * CMP_HIDDEN] for hv in halves for c in range(2)],
                              axis=0)
        first = acc[:, :CMP_HIDDEN]
        second = acc[:, CMP_HIDDEN:]
        posb = jnp.dot(jnp.broadcast_to(pos_ref[...], (8, CMP_LEN * N_HD)).astype(BF16), w1_ref[...],
                       preferred_element_type=F32)[0:1]
        hid = first + pltpu.roll(second, N_KV * nb - 1, 0) + posb
        return jax.nn.gelu(hid).astype(BF16)

    hk = hidden((kc0_ref, kc1_ref), posk_ref, w1k_ref, w1kc_ref)
    kc = jnp.dot(hk, w2k_ref[...], preferred_element_type=F32)
    hv = hidden((vc0_ref, vc1_ref), posv_ref, w1v_ref, w1vc_ref)
    vc = jnp.dot(hv, w2v_ref[...], preferred_element_type=F32)
    j = lax.broadcasted_iota(jnp.int32, (nb, 1), 0)
    lane = lax.broadcasted_iota(jnp.int32, (nb, N_HD), 1)
    n_real = (nb * CMP_STRIDE - CMP_LEN) // CMP_STRIDE + 1
    aux = jnp.where(lane < 3, (j * CMP_STRIDE).astype(F32),
                    jnp.where(lane < 6, (CMP_LEN - 1) * 0.5,
                              jnp.where((lane == 6) & (j >= n_real), 1.0, 0.0))).astype(BF16)
    ones_blk = jnp.where(lax.broadcasted_iota(jnp.int32, (VROWS - N_HD, nb), 0) == 0, 1.0, 0.0)
    for g in range(N_KV):
        kg = kc[g * nb:(g + 1) * nb, 0:N_HD]
        k_hi = kg.astype(BF16)
        k_lo = (kg - k_hi.astype(F32)).astype(BF16)
        kcmp_ref[0, g] = jnp.concatenate([k_hi, k_lo, k_hi, aux], axis=1)
        vt = vc[g * nb:(g + 1) * nb].T
        vcmpt_ref[0, g] = jnp.concatenate([vt[0:N_HD], ones_blk], axis=0).astype(BF16)


def _compress(p, posk, posv, w1k, w1kc, w2k, w1v, w1vc, w2v):
    bsz, t, _ = p.shape
    nb = t // CMP_STRIDE
    cb = lambda c: (lambda bi: (bi, 0, c))
    f2 = lambda bi: (0, 0)
    f3 = lambda bi: (0, 0, 0)
    wspecs = [pl.BlockSpec((CMP_LEN * N_HD, CMP_HIDDEN), f2),
              pl.BlockSpec((CMP_STRIDE, LANES, 4 * CMP_HIDDEN), f3),
              pl.BlockSpec((CMP_HIDDEN, LANES), f2)]
    return pl.pallas_call(
        _compress_kernel,
        grid=(bsz,),
        in_specs=[pl.BlockSpec((1, t, LANES), cb(COL_KC // LANES)),
                  pl.BlockSpec((1, t, LANES), cb(COL_KC // LANES + 1)),
                  pl.BlockSpec((1, t, LANES), cb(COL_VC // LANES)),
                  pl.BlockSpec((1, t, LANES), cb(COL_VC // LANES + 1)),
                  pl.BlockSpec((1, CMP_LEN * N_HD), f2),
                  pl.BlockSpec((1, CMP_LEN * N_HD), f2)] + wspecs + wspecs,
        out_specs=[pl.BlockSpec((1, N_KV, nb, 4 * N_HD), lambda bi: (bi, 0, 0, 0)),
                   pl.BlockSpec((1, N_KV, VROWS, nb), lambda bi: (bi, 0, 0, 0))],
        out_shape=[jax.ShapeDtypeStruct((bsz, N_KV, nb, 4 * N_HD), BF16),
                   jax.ShapeDtypeStruct((bsz, N_KV, VROWS, nb), BF16)],
        compiler_params=_cparams(("parallel",)),
        name="nsa_compress",
    )(p, p, p, p, posk, posv, w1k, w1kc, w2k, w1v, w1vc, w2v)


def _relayout_kernel(ks_ref, vs_ref, kw_ref, vw_ref, ksa_ref, vsa_ref, kwa_ref, vwa_ref):
    i = pl.program_id(1)
    is_pad = i == 0
    flag = jnp.where(is_pad, 1.0, 0.0)
    row = lax.broadcasted_iota(jnp.int32, (SKT, 1), 0)
    lane = lax.broadcasted_iota(jnp.int32, (SKT, N_HD), 1)
    al = lane - AUX_ROWS
    ones_blk = jnp.where(lax.broadcasted_iota(jnp.int32, (VROWS - N_HD, SKT), 0) == 0, 1.0, 0.0)
    for u in range(PAD_TILES):
        rows = slice(u * SKT, (u + 1) * SKT)
        base = ((i - 1) * PAD_TILES + u) * SKT
        pos = jnp.where(is_pad, 0, base + row)
        blk = pos // SLC_LEN
        p_hi = (blk * SLC_LEN).astype(F32)
        p_lo = (pos - blk * SLC_LEN).astype(F32)
        mid_w = jnp.where((al >= 0) & (al < 3), p_hi,
                          jnp.where((al >= 3) & (al < 6), p_lo, jnp.where(al == 6, flag, 0.0)))
        mid_s = jnp.where(lane == blk % SEL_GRP, 1.0, mid_w)
        ks = ks_ref[0, rows, :]
        kw = kw_ref[0, rows, :]
        for g in range(N_KV):
            sl = slice(g * N_HD, (g + 1) * N_HD)
            ksa_ref[0, g, rows, :] = jnp.concatenate([ks[:, sl], mid_s], axis=1).astype(BF16)
            kwa_ref[0, g, rows, :] = jnp.concatenate([kw[:, sl], mid_w], axis=1).astype(BF16)
        vst = vs_ref[0, rows, :].T
        vwt = vw_ref[0, rows, :].T
        for g in range(N_KV):
            sl = slice(g * N_HD, (g + 1) * N_HD)
            vsa_ref[0, g, u] = jnp.concatenate([vst[sl], ones_blk], axis=0).astype(BF16)
            vwg = jnp.concatenate([vwt[sl], ones_blk], axis=0).astype(BF16)
            for j in range(SKT // WKT):
                vwa_ref[0, g, u * (SKT // WKT) + j] = vwg[:, j * WKT:(j + 1) * WKT]


def _relayout(p):
    bsz, t, _ = p.shape
    nt = t // SKT + PAD_TILES
    wpt = SKT // WKT
    rt = PAD_TILES
    cb = lambda c: (lambda bi, i: (bi, jnp.maximum(i - 1, 0), c))
    return pl.pallas_call(
        _relayout_kernel,
        grid=(bsz, nt // rt),
        in_specs=[pl.BlockSpec((1, rt * SKT, KV_W), cb(COL_KS // KV_W)),
                  pl.BlockSpec((1, rt * SKT, KV_W), cb(COL_VS // KV_W)),
                  pl.BlockSpec((1, rt * SKT, KV_W), cb(COL_KW // KV_W)),
                  pl.BlockSpec((1, rt * SKT, KV_W), cb(COL_VW // KV_W))],
        out_specs=[pl.BlockSpec((1, N_KV, rt * SKT, KA_SEL), lambda bi, i: (bi, 0, i, 0)),
                   pl.BlockSpec((1, N_KV, rt, VROWS, SKT), lambda bi, i: (bi, 0, i, 0, 0)),
                   pl.BlockSpec((1, N_KV, rt * SKT, KA_WIN), lambda bi, i: (bi, 0, i, 0)),
                   pl.BlockSpec((1, N_KV, rt * wpt, VROWS, WKT), lambda bi, i: (bi, 0, i, 0, 0))],
        out_shape=[jax.ShapeDtypeStruct((bsz, N_KV, nt * SKT, KA_SEL), BF16),
                   jax.ShapeDtypeStruct((bsz, N_KV, nt, VROWS, SKT), BF16),
                   jax.ShapeDtypeStruct((bsz, N_KV, nt * SKT, KA_WIN), BF16),
                   jax.ShapeDtypeStruct((bsz, N_KV, nt * wpt, VROWS, WKT), BF16)],
        compiler_params=_cparams(("parallel", "parallel")),
        name="nsa_relayout",
    )(p, p, p, p)


def _nsa_batch_kernel(sp_ref, q_ref, gt_ref, z_ref, kcmp_ref, vcmpt_ref, ovt_ref, ks_ref, vs_ref, kw_ref,
                      vw_ref, tri_ref, wb_ref, y_ref, sa_scr, sb_scr, g_scr, imp_scr, rank_scr, part_scr, selb_scr,
                      tile_idx):
    g = pl.program_id(0)
    qb = pl.program_id(1)
    n_items = q_ref.shape[0]
    start = qb * QBLK
    nb = kcmp_ref.shape[2]
    n_slc = ovt_ref.shape[0]
    nw = N_HPG * QBLK
    n_top = min(SLC_TOPN, n_slc)
    t_row = start + lax.broadcasted_iota(jnp.int32, (1, QBLK), 1)
    s_bufs = (sa_scr, sb_scr)
    last = qb // (SKT // QBLK)

    def tile4(a):
        return jnp.concatenate([a] * N_HPG, axis=1)

    def col_reduce8(s, op):
        out = s[0:8]
        for r in range(1, s.shape[0] // 8):
            out = op(out, s[8 * r:8 * r + 8])
        return out

    def col_max(s):
        return jnp.max(col_reduce8(s, jnp.maximum), axis=0, keepdims=True)

    def normalise(acc):
        den = acc[N_HD:N_HD + 1]
        return acc[0:N_HD] / jnp.where(den > 0, den, 1.0)

    def gate_row(i, h, branch):
        return g_scr[i, pl.ds(GATE_NG + (g * N_HPG + h) * 3 + branch, 1), :]

    hl = lax.broadcasted_iota(jnp.int32, (AUX_ROWS, nw), 1) // QBLK
    ar = lax.broadcasted_iota(jnp.int32, (AUX_ROWS, nw), 0)

    def per_head(vals):
        out = jnp.full((AUX_ROWS, nw), vals[N_HPG - 1], F32)
        for h in range(N_HPG - 2, -1, -1):
            out = jnp.where(hl == h, vals[h], out)
        return out

    pieces = [per_head([sp_ref[(g * N_HPG + h) * 3 + j] for h in range(N_HPG)]) for j in range(3)]
    aux = jnp.where(ar == 6, NEG, 0.0)
    for j in range(3):
        aux = jnp.where((ar == j) | (ar == j + 3), pieces[j], aux)
    aux_b = aux.astype(BF16)
    q_tail = jnp.zeros((KA_SEL - N_HD - 2 * AUX_ROWS, nw), BF16)
    sidx = lax.broadcasted_iota(jnp.int32, (n_slc, QBLK), 0)
    cur = t_row // SLC_LEN
    valid = sidx <= cur
    sub8 = lax.broadcasted_iota(jnp.int32, (8, QBLK), 0)
    last_blk = (start + QBLK - 1) // SLC_LEN
    cmp_end = lax.broadcasted_iota(jnp.int32, (nb, 1), 0) * CMP_STRIDE + (CMP_LEN - 1)
    cmp_bias = tile4(jnp.where(cmp_end <= t_row, 0.0, NEG))
    wrows = WIN + QBLK

    def queries(i):
        qt = (q_ref[i] * (N_HD ** -0.5)).T
        q4l = jnp.concatenate([qt[h * N_HD:(h + 1) * N_HD] for h in range(N_HPG)], axis=1) * LOG2E
        q4s = q4l.astype(BF16)
        g_scr[i] = jax.nn.sigmoid(gt_ref[i]).T
        return q4s, (q4l - q4s.astype(F32)).astype(BF16)

    def compressed(i, q4s, q4lo):
        q_cmp = jnp.concatenate([q4s, q4s, q4lo, aux_b, jnp.zeros((N_HD - AUX_ROWS, nw), BF16)], axis=0)
        s_c = jnp.dot(kcmp_ref[i, 0], q_cmp, preferred_element_type=F32) + cmp_bias
        e_c = jnp.exp2(s_c - col_max(s_c))
        z_c = jnp.sum(col_reduce8(e_c, jnp.add), axis=0, keepdims=True)
        inv_c = jnp.where(tile4(t_row >= CMP_LEN - 1) & (z_c > 0), 1.0 / z_c, 0.0)
        o_c = jnp.dot(vcmpt_ref[i, 0], e_c.astype(BF16), preferred_element_type=F32)[0:N_HD] * inv_c
        p_c = e_c * inv_c
        psum = p_c[:, 0:QBLK]
        for h in range(1, N_HPG):
            psum = psum + p_c[:, h * QBLK:(h + 1) * QBLK]
        parts, rest = [], psum
        for _ in range(3):
            parts.append(rest.astype(BF16))
            rest = rest - parts[-1].astype(F32)
        imp = jnp.dot(ovt_ref[...], jnp.concatenate(parts, axis=0), preferred_element_type=F32)
        imp = jnp.where(valid, imp, -jnp.inf)
        imp_scr[i] = jnp.where((sidx == 0) | (sidx == cur), jnp.inf, imp)
        return o_c

    def window(i, q4s):
        q_win = jnp.concatenate([q4s, jnp.zeros((AUX_ROWS, nw), BF16), aux_b,
                                 jnp.zeros((KA_WIN - N_HD - 2 * AUX_ROWS, nw), BF16)], axis=0)
        s_w = jnp.dot(kw_ref[i, 0, pl.ds(pl.multiple_of(start, QBLK), wrows), :], q_win,
                      preferred_element_type=F32)
        s_w = jnp.concatenate([s_w[0:WKT] + tile4(wb_ref[0]), s_w[WKT:wrows - WKT],
                               s_w[wrows - WKT:] + tile4(wb_ref[1])], axis=0)
        p_w = jnp.exp2(s_w - col_max(s_w)).astype(BF16)
        v_w = jnp.concatenate([vw_ref[i, 0, qb * (QBLK // WKT) + j] for j in range(wrows // WKT)], axis=1)
        return normalise(jnp.dot(v_w, p_w, preferred_element_type=F32))

    q4, q4_lo = zip(*[queries(i) for i in range(n_items)])
    o_cmp = [compressed(i, q4[i], q4_lo[i]) for i in range(n_items)]
    o_win = [window(i, q4[i]) for i in range(n_items)]
    for i in range(n_items):
        for h in range(N_HPG):
            hs = slice(h * QBLK, (h + 1) * QBLK)
            part_scr[i, :, hs] = gate_row(i, h, 0) * o_cmp[i][:, hs] + gate_row(i, h, 2) * o_win[i][:, hs]

    rank_scr[...] = jnp.zeros_like(rank_scr)
    for ri in range(n_slc // 8):
        @pl.when((last_blk >= n_top) & (8 * ri <= last_blk))
        def _():
            for i in range(n_items):
                rows = imp_scr[i, 8 * ri:8 * ri + 8]
                for r in range(n_slc // 8):
                    blk8 = imp_scr[i, 8 * r:8 * r + 8]
                    acc = rank_scr[i, 8 * r:8 * r + 8]
                    for ii in range(8):
                        row = rows[ii:ii + 1]
                        if ri < r:
                            before = row >= blk8
                        elif ri > r:
                            before = row > blk8
                        else:
                            before = (row > blk8) | ((row == blk8) & (sub8 > ii))
                        acc = acc + jnp.where(before, 1.0, 0.0)
                    rank_scr[i, 8 * r:8 * r + 8] = acc

    def select(i):
        chosen = (rank_scr[i] < n_top) & valid
        selb_scr[i] = tile4(jnp.where(chosen, 0.0, NEG))

        any_q = jnp.max(jnp.where(chosen, 1.0, 0.0), axis=1, keepdims=True) > 0
        s_io = lax.broadcasted_iota(jnp.int32, (n_slc, 1), 0)
        bits = jnp.where(any_q, jnp.left_shift(1, s_io % 32), 0)
        words = [jnp.sum(jnp.where(s_io // 32 == w, bits, 0)) for w in range((n_slc + 31) // 32)]
        bpt = SKT // SLC_LEN
        cnt = jnp.int32(0)
        for tile in range(n_slc // bpt):
            nib = lax.shift_right_logical(words[tile * bpt // 32], jnp.int32(tile * bpt % 32)) & (2 ** bpt - 1)
            tile_idx[i, cnt] = tile
            cnt = cnt + ((nib != 0) & (tile < last)).astype(jnp.int32)
        return cnt

    n_act = [select(i) for i in range(n_items)]


    def sel_scores(i, kt):
        grp = pl.multiple_of((kt * SKT // SLC_LEN) // SEL_GRP * SEL_GRP, SEL_GRP)
        sel_rows = jnp.concatenate([selb_scr[i, pl.ds(grp, SEL_GRP), :], jnp.zeros((AUX_ROWS - SEL_GRP, nw), F32)],
                                   axis=0)
        q_sel = jnp.concatenate([q4[i], sel_rows.astype(BF16), aux_b, q_tail], axis=0)
        krow = pl.multiple_of((kt + PAD_TILES) * SKT, SKT)
        return jnp.dot(ks_ref[i, 0, pl.ds(krow, SKT), :], q_sel, preferred_element_type=F32)

    def tile_rows(kt):
        return pl.ds(pl.multiple_of(kt * SKT, SKT), SKT)

    def pass1(i):
        def fn(kt, mrun):
            s = sel_scores(i, kt)
            s_bufs[i % 2][tile_rows(kt), :] = s
            return jnp.maximum(mrun, col_reduce8(s, jnp.maximum))
        return fn

    def pass2(i, m_sel):
        def fn(kt, acc):
            pr = jnp.exp2(s_bufs[i % 2][tile_rows(kt), :] - m_sel).astype(BF16)
            return acc + jnp.dot(vs_ref[i, 0, kt + PAD_TILES], pr, preferred_element_type=F32)
        return fn

    def listed(i, fn):
        return lambda j, carry: fn(tile_idx[i, j], carry)

    def both(f1, f2):
        def fn(j, carry):
            return f1(j, carry[0]), f2(j, carry[1])
        return fn

    def run_tiles(fn, lo, n, init):
        def group(size, first):
            def body(j, carry):
                for u in range(size):
                    carry = fn(first + j * size + u, carry)
                return carry
            return body

        n_u = n // UNROLL
        carry = lax.fori_loop(0, n_u, group(UNROLL, lo), init)
        done = lo + n_u * UNROLL
        size = UNROLL // 2
        while size >= 1:
            take = (n // size) % 2
            carry = lax.fori_loop(0, take, group(size, done), carry)
            done = done + take * size
            size //= 2
        return carry

    def diag_tile(i, mrun):
        s = sel_scores(i, last) + tile4(tri_ref[qb % (SKT // QBLK)])
        s_bufs[i % 2][tile_rows(last), :] = s
        return jnp.max(jnp.maximum(mrun, col_reduce8(s, jnp.maximum)), axis=0, keepdims=True)

    mrun0 = jnp.full((8, nw), NEG, F32)
    acc0 = jnp.zeros((VROWS, nw), F32)
    m_prev = diag_tile(0, run_tiles(listed(0, pass1(0)), 0, n_act[0], mrun0))
    outs = []
    for i in range(1, n_items):
        p2 = pass2(i - 1, m_prev)
        f1, f2 = listed(i, pass1(i)), listed(i - 1, p2)
        n_both = jnp.minimum(n_act[i], n_act[i - 1])
        mrun, acc = run_tiles(both(f1, f2), 0, n_both, (mrun0, acc0))
        mrun = run_tiles(f1, n_both, n_act[i] - n_both, mrun)
        acc = run_tiles(f2, n_both, n_act[i - 1] - n_both, acc)
        m_cur = diag_tile(i, mrun)
        outs.append(normalise(p2(last, acc)))
        m_prev = m_cur
    p2 = pass2(n_items - 1, m_prev)
    acc = run_tiles(listed(n_items - 1, p2), 0, n_act[n_items - 1], acc0)
    outs.append(normalise(p2(last, acc)))

    for i in range(n_items):
        tot = [part_scr[i, :, h * QBLK:(h + 1) * QBLK] + gate_row(i, h, 1) * outs[i][:, h * QBLK:(h + 1) * QBLK]
               for h in range(N_HPG)]
        o = jnp.concatenate(tot, axis=0).T
        y_ref[i] = (o * _silu(z_ref[i])).astype(BF16)


def _edge_biases():
    kl = np.arange(SKT)[:, None]
    ql = np.arange(QBLK)[None, :]
    tri = np.stack([np.where(kl <= par * QBLK + ql, 0.0, NEG) for par in range(SKT // QBLK)])
    kk = np.arange(WKT)[:, None]
    wb = np.stack([np.where(kk > ql, 0.0, NEG), np.where(kk <= ql + WKT - QBLK, 0.0, NEG)])
    return jnp.asarray(tri, F32), jnp.asarray(wb, F32)


def _nsa(p, spieces, kcmp, vcmpt, ovt, ks, vs, kw, vw):
    bsz, t, _ = p.shape
    n_slc = t // SLC_LEN
    gw = N_HPG * N_HD
    tri, wb = _edge_biases()
    nw = N_HPG * QBLK
    per_g = lambda a: pl.BlockSpec((bsz, 1) + a.shape[2:], lambda g, i, sp: (0, g) + (0,) * (a.ndim - 2))
    const = lambda a: pl.BlockSpec(a.shape, lambda g, i, sp: (0,) * a.ndim)
    grid_spec = pltpu.PrefetchScalarGridSpec(
        num_scalar_prefetch=1,
        grid=(N_KV, t // QBLK),
        in_specs=[pl.BlockSpec((bsz, QBLK, gw), lambda g, i, sp: (0, i, COL_NQ // gw + g)),
                  pl.BlockSpec((bsz, QBLK, LANES), lambda g, i, sp: (0, i, COL_GATES // LANES)),
                  pl.BlockSpec((bsz, QBLK, gw), lambda g, i, sp: (0, i, COL_NZ // gw + g)),
                  per_g(kcmp), per_g(vcmpt), const(ovt),
                  per_g(ks), per_g(vs), per_g(kw), per_g(vw), const(tri), const(wb)],
        out_specs=pl.BlockSpec((bsz, QBLK, gw), lambda g, i, sp: (0, i, g)),
        scratch_shapes=[pltpu.VMEM((t, nw), F32),
                        pltpu.VMEM((t, nw), F32),
                        pltpu.VMEM((bsz, LANES, QBLK), F32),
                        pltpu.VMEM((bsz, n_slc, QBLK), F32),
                        pltpu.VMEM((bsz, n_slc, QBLK), F32),
                        pltpu.VMEM((bsz, N_HD, nw), F32),
                        pltpu.VMEM((bsz, n_slc, nw), F32),
                        pltpu.SMEM((bsz, t // SKT), jnp.int32)],
    )
    return pl.pallas_call(
        _nsa_batch_kernel,
        grid_spec=grid_spec,
        out_shape=jax.ShapeDtypeStruct((bsz, t, N_WIDTH), BF16),
        compiler_params=_cparams(("arbitrary", "arbitrary")),
        name="nsa_attention",
    )(spieces, p, p, p, kcmp, vcmpt, ovt, ks, vs, kw, vw, tri, wb)


def _outproj_kernel(ym_ref, yn_ref, w_ref, x_ref, gate_ref, fg_ref, o_ref, wb_scr, *, final):
    @pl.when((pl.program_id(0) == 0) & (pl.program_id(1) == 0))
    def _():
        rows = w_ref.shape[1]
        for c in range(rows // WP_TILE):
            cs = slice(c * WP_TILE, (c + 1) * WP_TILE)
            wb_scr[cs, :] = w_ref[0, cs, :].astype(BF16)

    y = jnp.dot(ym_ref[0], wb_scr[0:M_WIDTH, :], preferred_element_type=F32)
    y = y + jnp.dot(yn_ref[0], wb_scr[M_WIDTH:, :], preferred_element_type=F32)
    hres = x_ref[0] + gate_ref[0] * y
    if final:
        ms = jnp.mean(hres * hres, axis=-1, keepdims=True)
        hres = hres * lax.rsqrt(ms + EPS) * fg_ref[...]
    o_ref[0] = hres


def _outproj(ym, yn, w, layer, x, gate, fg, final):
    bsz, t, d = x.shape
    tm = 512
    return pl.pallas_call(
        functools.partial(_outproj_kernel, final=final),
        grid=(bsz, t // tm),
        in_specs=[pl.BlockSpec((1, tm, M_WIDTH), lambda bi, i: (bi, i, 0)),
                  pl.BlockSpec((1, tm, N_WIDTH), lambda bi, i: (bi, i, 0)),
                  pl.BlockSpec((1, M_WIDTH + N_WIDTH, d), lambda bi, i: (layer, 0, 0),
                               pipeline_mode=pl.Buffered(1)),
                  pl.BlockSpec((1, tm, d), lambda bi, i: (bi, i, 0)),
                  pl.BlockSpec((1, 1, d), lambda bi, i: (bi, 0, 0)),
                  pl.BlockSpec((1, d), lambda bi, i: (0, 0))],
        out_specs=pl.BlockSpec((1, tm, d), lambda bi, i: (bi, i, 0)),
        out_shape=jax.ShapeDtypeStruct((bsz, t, d), F32),
        scratch_shapes=[pltpu.VMEM((M_WIDTH + N_WIDTH, d), BF16)],
        compiler_params=_cparams(("arbitrary", "arbitrary")),
        name="outproj_residual",
    )(ym, yn, w, x, gate, fg)


SRC_MI = 4 * M_WIDTH
SRC_NQ = SRC_MI + 2 * M_HEADS
SRC_NG = SRC_NQ + N_WIDTH + 6 * KV_W
SRC_NZ = SRC_NG + 3 * N_HEADS


def _reorder_cols(a):
    parts = [a[..., 0:SRC_MI], a[..., SRC_NQ:SRC_NG], a[..., SRC_NZ:SRC_NZ + N_WIDTH], a[..., SRC_MI:SRC_NQ],
             a[..., SRC_NG:SRC_NZ]]
    used = sum(x.shape[-1] for x in parts)
    parts.append(jnp.zeros(a.shape[:-1] + (NP_PAD - used,), a.dtype))
    return jnp.concatenate(parts, axis=-1)


WP_TILE = 512
WP_GATE_TILE = COL_GATES // WP_TILE


def _wprep_kernel(w_ref, g1_ref, g2_ref, o_ref):
    j = pl.program_id(0)
    d = o_ref.shape[0]

    @pl.when(j < WP_GATE_TILE)
    def _():
        for c in range(d // WP_TILE):
            cs = slice(c * WP_TILE, (c + 1) * WP_TILE)
            o_ref[cs, :] = w_ref[0, :, cs].T.astype(BF16)

    @pl.when(j == WP_GATE_TILE)
    def _():
        n_gate = g1_ref.shape[1] + g2_ref.shape[1]
        gt = jnp.concatenate([g1_ref[0], g2_ref[0], jnp.zeros((LANES - n_gate, d), F32)], axis=0)
        for c in range(d // WP_TILE):
            cs = slice(c * WP_TILE, (c + 1) * WP_TILE)
            o_ref[cs, 0:LANES] = gt[:, cs].T.astype(BF16)
        o_ref[:, LANES:] = jnp.zeros((d, WP_TILE - LANES), BF16)


def _wprep(w_t, layer):
    _, n, d = w_t.shape
    n_big = COL_GATES // WP_TILE
    assert n == SRC_NZ + N_WIDTH and NP_PAD // WP_TILE == n_big + 1

    def src_row(j):
        return jnp.where(j < COL_NQ // WP_TILE, j * WP_TILE,
                         jnp.where(j < COL_NZ // WP_TILE, SRC_NQ + (j - COL_NQ // WP_TILE) * WP_TILE,
                                   jnp.where(j < n_big, SRC_NZ + (j - COL_NZ // WP_TILE) * WP_TILE, 0)))

    el = pl.Element
    return pl.pallas_call(
        _wprep_kernel,
        grid=(NP_PAD // WP_TILE,),
        in_specs=[pl.BlockSpec((el(1), el(WP_TILE), el(d)), lambda j: (layer, pl.multiple_of(src_row(j), 8), 0)),
                  pl.BlockSpec((el(1), el(SRC_NQ - SRC_MI), el(d)), lambda j: (layer, SRC_MI, 0)),
                  pl.BlockSpec((el(1), el(SRC_NZ - SRC_NG), el(d)), lambda j: (layer, SRC_NG, 0))],
        out_specs=pl.BlockSpec((d, WP_TILE), lambda j: (0, j)),
        out_shape=jax.ShapeDtypeStruct((d, NP_PAD), BF16),
        compiler_params=_cparams(("parallel",)),
        name="inproj_weight_prep",
    )(w_t, w_t, w_t)


def _overlap_t(t):
    n_cmp_rows = t // CMP_STRIDE
    n_slc = t // SLC_LEN
    c0 = np.arange(n_cmp_rows) * CMP_STRIDE
    s0 = np.arange(n_slc) * SLC_LEN
    ov = (c0[None, :] <= s0[:, None] + SLC_LEN - 1) & (c0[None, :] + CMP_LEN - 1 >= s0[:, None])
    ov[:, (t - CMP_LEN) // CMP_STRIDE + 1:] = False
    return jnp.asarray(np.concatenate([ov] * 3, axis=1), BF16)


def kernel(x, c, ln_g, w_ada, b_ada, w_in, b_in, m_conv_w, m_conv_b, m_wq, m_wk, m_norm_w, m_skip, m_f_bias,
           n_pos_k, n_pos_v, n_w1_k, n_w2_k, n_w1_v, n_w2_v, w_out, final_g):
    out_dtype = x.dtype
    bsz, t, d = x.shape
    depth = ln_g.shape[0]
    h_res = x.astype(F32)
    assert bsz <= 2 and t % 1024 == 0 and t // CMP_STRIDE <= 256
    c_t =jnp.zeros((d, 8), F32).at[:, :bsz].set(c.astype(F32).T)
    slopes_np = np.array([2.0 ** (-8.0 * (h + 1) / N_HEADS) for h in range(N_HEADS)], np.float32)
    rest = (slopes_np.astype(np.float64) * LOG2E).astype(np.float32)
    pieces = []
    for _ in range(3):
        pieces.append(rest.astype(jnp.bfloat16).astype(np.float32))
        rest = rest - pieces[-1]
    spieces = jnp.asarray(np.stack(pieces, axis=1).reshape(-1))
    ovt = _overlap_t(t)

    def w1cat(w1):
        w = w1.reshape(2, CMP_STRIDE, N_HD, CMP_HIDDEN)
        w = jnp.concatenate([w[0], w[1]], axis=-1).astype(BF16)
        z = jnp.zeros_like(w)
        return jnp.concatenate([jnp.concatenate([w, z], axis=-1), jnp.concatenate([z, w], axis=-1)], axis=1)

    def w2pad(w2):
        return jnp.pad(w2, ((0, 0), (0, LANES - N_HD))).astype(BF16)

    for l in range(depth):
        mod = _ada(c_t, w_ada, b_ada[l][None, :], l, bsz)[:bsz]
        shift, scale, gate = mod[:, None, 0:d], mod[:, None, d:2 * d], mod[:, None, 2 * d:3 * d]
        p = _inproj(h_res, ln_g[l][None, :], scale, shift,
                    _wprep(jnp.swapaxes(w_in, 1, 2), l), _reorder_cols(b_in[l])[None, :])
        fb_row = jnp.zeros((1, LANES), F32).at[0, M_HEADS:2 * M_HEADS].set(m_f_bias[l])
        y_m = _mlstm(p, m_conv_w[l], m_conv_b[l][None, :], m_wq[l].astype(BF16), m_wk[l].astype(BF16),
                     m_norm_w[l][None, :], m_skip[l][None, :], fb_row)
        kcmp, vcmpt = _compress(p, n_pos_k[l].reshape(1, -1), n_pos_v[l].reshape(1, -1),
                                n_w1_k[l].astype(BF16), w1cat(n_w1_k[l]), w2pad(n_w2_k[l]),
                                n_w1_v[l].astype(BF16), w1cat(n_w1_v[l]), w2pad(n_w2_v[l]))
        ks, vs, kw, vw = _relayout(p)
        y_n = _nsa(p, spieces, kcmp, vcmpt, ovt, ks, vs, kw, vw)
        h_res = _outproj(y_m, y_n, w_out, l, h_res, gate, final_g[None, :], l == depth - 1)
    return h_res.astype(out_dtype)
```

```python
import functools

import numpy as np
import jax
import jax.numpy as jnp
from jax import lax
from jax.experimental import pallas as pl
from jax.experimental.pallas import tpu as pltpu

F32 = jnp.float32
BF16 = jnp.bfloat16

EPS = 1e-6
M_HEADS = 4
M_HD = 256
M_WIDTH = M_HEADS * M_HD
CONV_K = 4
M_CHUNK = 256
N_HEADS = 16
N_HD = 64
N_KV = 4
N_HPG = N_HEADS // N_KV
N_WIDTH = N_HEADS * N_HD
KV_W = N_KV * N_HD
CMP_LEN = 32
CMP_STRIDE = 16
CMP_HIDDEN = 2 * N_HD
SLC_LEN = 64
SLC_TOPN = 16
WIN = 512
QBLK = 256
SKT = 256
WKT = 256
PAD_TILES = WIN // SKT
SEL_GRP = 8
KA_SEL = 128
KA_WIN = 128
VROWS = 80
UNROLL = 4
AUX_ROWS = 16
LOG2E = 1.4426950408889634

COL_MX, COL_MV, COL_MO, COL_MZ = 0, 1024, 2048, 3072
COL_NQ = 4096
COL_KC, COL_VC, COL_KS, COL_VS, COL_KW, COL_VW = 5120, 5376, 5632, 5888, 6144, 6400
COL_NZ = 6656
COL_GATES = 7680
GATE_NG = 2 * M_HEADS
NP_PAD = 8192
NORM_CHUNKS = 2
LANES = 128
NEG = -1e30
VMEM_LIMIT = 58 * 1024 * 1024


def _cparams(sem):
    return pltpu.CompilerParams(dimension_semantics=sem, vmem_limit_bytes=VMEM_LIMIT)


def _silu(x):
    return x * jax.nn.sigmoid(x)


def _log_sigmoid(x):
    return jnp.minimum(x, 0.0) - jnp.log1p(jnp.exp(-jnp.abs(x)))


def _ada_kernel(ct_ref, w_ref, b_ref, o_ref, *, bsz):
    s_t = _silu(ct_ref[...])
    w = w_ref[0]
    row = lax.broadcasted_iota(jnp.int32, o_ref.shape, 0)
    out = jnp.zeros(o_ref.shape, F32)
    for b in range(bsz):
        prod = w * s_t[:, b:b + 1]
        acc = prod[0:8]
        for r in range(1, prod.shape[0] // 8):
            acc = acc + prod[8 * r:8 * r + 8]
        out = jnp.where(row == b, jnp.sum(acc, axis=0, keepdims=True) + b_ref[...], out)
    o_ref[...] = out


def _ada(c_t, w, b, layer, bsz):
    _, d, n = w.shape
    tn = 1024
    return pl.pallas_call(
        functools.partial(_ada_kernel, bsz=bsz),
        grid=(n // tn,),
        in_specs=[pl.BlockSpec((d, 8), lambda j: (0, 0)),
                  pl.BlockSpec((1, d, tn), lambda j: (layer, 0, j)),
                  pl.BlockSpec((1, tn), lambda j: (0, j))],
        out_specs=pl.BlockSpec((8, tn), lambda j: (0, j)),
        out_shape=jax.ShapeDtypeStruct((8, n), F32),
        compiler_params=_cparams(("parallel",)),
        name="ada_mod",
    )(c_t, w, b)


def _inproj_kernel(x_ref, g_ref, sc_ref, sh_ref, w_ref, b_ref, o_ref, h_ref):
    first = pl.program_id(2) == 0

    @pl.when(first)
    def _():
        tm = x_ref.shape[1]
        ck = tm // NORM_CHUNKS
        for c in range(NORM_CHUNKS):
            rows = slice(c * ck, (c + 1) * ck)
            x = x_ref[0, rows, :]
            ms = jnp.mean(x * x, axis=-1, keepdims=True)
            h = x * lax.rsqrt(ms + EPS) * g_ref[...]
            h = (h * (1.0 + sc_ref[0]) + sh_ref[0]).astype(BF16)
            h_ref[rows, :] = h
            o_ref[0, rows, :] = jnp.dot(h, w_ref[...], preferred_element_type=F32) + b_ref[...]

    @pl.when(jnp.logical_not(first))
    def _():
        o_ref[0] = jnp.dot(h_ref[...], w_ref[...], preferred_element_type=F32) + b_ref[...]


def _inproj(x, g, scale, shift, w, b):
    bsz, t, d = x.shape
    n = w.shape[1]
    tm, tn = 1024, 2048
    return pl.pallas_call(
        _inproj_kernel,
        grid=(bsz, t // tm, n // tn),
        in_specs=[pl.BlockSpec((1, tm, d), lambda bi, i, j: (bi, i, 0)),
                  pl.BlockSpec((1, d), lambda bi, i, j: (0, 0)),
                  pl.BlockSpec((1, 1, d), lambda bi, i, j: (bi, 0, 0)),
                  pl.BlockSpec((1, 1, d), lambda bi, i, j: (bi, 0, 0)),
                  pl.BlockSpec((d, tn), lambda bi, i, j: (0, j)),
                  pl.BlockSpec((1, tn), lambda bi, i, j: (0, j))],
        out_specs=pl.BlockSpec((1, tm, tn), lambda bi, i, j: (bi, i, j)),
        out_shape=jax.ShapeDtypeStruct((bsz, t, n), F32),
        scratch_shapes=[pltpu.VMEM((tm, d), BF16)],
        compiler_params=_cparams(("parallel", "parallel", "arbitrary")),
        name="norm_inproj",
    )(x, g, scale, shift, w, b)


def _mlstm_kernel(x_ref, v_ref, o_ref, z_ref, gt_ref, cw_ref, cb_ref, wq_ref, wk_ref, nw_ref, sk_ref, fb_ref,
                  y_ref, c_scr, n_scr, m_scr, xp_scr):
    L = M_CHUNK

    @pl.when(pl.program_id(1) == 0)
    def _():
        c_scr[...] = jnp.zeros_like(c_scr)
        n_scr[...] = jnp.zeros_like(n_scr)
        m_scr[...] = jnp.zeros_like(m_scr)
        xp_scr[...] = jnp.zeros_like(xp_scr)

    x = x_ref[0]
    prev = xp_scr[...]
    row8 = lax.broadcasted_iota(jnp.int32, (8, M_WIDTH), 0)
    cw = cw_ref[...]
    xc = cb_ref[...] + x * cw[CONV_K - 1:CONV_K, :]
    for sft in range(1, CONV_K):
        xr = pltpu.roll(x, sft, 0)
        top = jnp.where(row8 < sft, pltpu.roll(prev, sft, 0), xr[0:8])
        xs = jnp.concatenate([top, xr[8:]], axis=0)
        xc = xc + xs * cw[CONV_K - 1 - sft:CONV_K - sft, :]
    xp_scr[...] = x[L - 8:L]
    xc = _silu(xc)

    gt = gt_ref[0]
    col = lax.broadcasted_iota(jnp.int32, (L, LANES), 1)
    logf = _log_sigmoid(gt + fb_ref[...])
    a_c = jnp.where((col >= M_HEADS) & (col < 2 * M_HEADS), logf, gt)
    ri = lax.broadcasted_iota(jnp.int32, (L, L), 0)
    ci = lax.broadcasted_iota(jnp.int32, (L, L), 1)
    causal = ri >= ci
    hp = lax.Precision.HIGHEST
    tri = causal.astype(F32)
    tri_t = (ri <= ci).astype(F32)
    b_c = jnp.dot(tri, a_c, precision=hp, preferred_element_type=F32)
    a_r = a_c.T
    b_r = jnp.dot(a_r[0:8], tri_t, precision=hp, preferred_element_type=F32)

    for h in range(M_HEADS):
        sl = slice(h * M_HD, (h + 1) * M_HD)
        xh = xc[:, sl]
        xb = xh.astype(BF16)
        q = jnp.dot(xb, wq_ref[h], preferred_element_type=F32)
        k = jnp.dot(xb, wk_ref[h], preferred_element_type=F32) * (M_HD ** -0.5)
        vb = v_ref[0, :, sl].astype(BF16)
        qb = q.astype(BF16)
        kb = k.astype(BF16)

        bt = b_c[:, M_HEADS + h:M_HEADS + h + 1]
        ic = a_c[:, h:h + 1]
        bs = b_r[M_HEADS + h:M_HEADS + h + 1, :]
        ir = a_r[h:h + 1, :]
        m_prev = m_scr[h][:, 0:1]

        dm = jnp.where(causal, bt - bs + ir, -jnp.inf)
        inter = bt + m_prev
        m_t = jnp.maximum(inter, jnp.max(dm, axis=-1, keepdims=True))
        w_in = jnp.exp(dm - m_t)
        w_st = jnp.exp(inter - m_t)
        s = lax.dot_general(qb, kb, (((1,), (1,)), ((), ())), preferred_element_type=F32) * w_in
        cmat = c_scr[h]
        nvec = n_scr[h]
        sb = s.astype(BF16)
        num = w_st * jnp.dot(qb, cmat.astype(BF16), preferred_element_type=F32) \
            + jnp.dot(sb, vb, preferred_element_type=F32)
        nt_dims = (((1,), (1,)), ((), ()))
        qn = lax.dot_general(qb, jnp.broadcast_to(nvec, (8, M_HD)).astype(BF16), nt_dims,
                             preferred_element_type=F32)[:, 0:1]
        ssum = lax.dot_general(sb, jnp.ones((8, L), BF16), nt_dims, preferred_element_type=F32)[:, 0:1]
        den = w_st * qn + ssum
        hh = num / jnp.maximum(jnp.abs(den), jnp.exp(-m_t))

        b_last = bt[L - 1:L, :]
        w_end = b_last - bt + ic
        m_new = jnp.maximum(b_last + m_prev, jnp.max(w_end, axis=0, keepdims=True))
        decay = jnp.exp(b_last + m_prev - m_new)
        kwt = k * jnp.exp(w_end - m_new)
        c_scr[h] = decay * cmat + lax.dot_general(kwt.astype(BF16), vb, (((0,), (0,)), ((), ())),
                                                  preferred_element_type=F32)
        n_scr[h] = decay * nvec + jnp.sum(kwt, axis=0, keepdims=True)
        m_scr[h] = jnp.broadcast_to(m_new, (1, LANES))

        mu = jnp.mean(hh, axis=-1, keepdims=True)
        hc = hh - mu
        var = jnp.mean(hc * hc, axis=-1, keepdims=True)
        hn = hc * lax.rsqrt(var + EPS) * nw_ref[:, sl]
        out = jax.nn.sigmoid(o_ref[0, :, sl]) * hn + sk_ref[:, sl] * xh
        y_ref[0, :, sl] = (out * _silu(z_ref[0, :, sl])).astype(BF16)


def _mlstm(p, conv_w, conv_b, wq, wk, norm_w, skip, fb_row):
    bsz, t, _ = p.shape
    L = M_CHUNK
    cb = lambda c: (lambda bi, i: (bi, i, c))
    full2 = lambda bi, i: (0, 0)
    full3 = lambda bi, i: (0, 0, 0)
    return pl.pallas_call(
        _mlstm_kernel,
        grid=(bsz, t // L),
        in_specs=[pl.BlockSpec((1, L, M_WIDTH), cb(COL_MX // M_WIDTH)),
                  pl.BlockSpec((1, L, M_WIDTH), cb(COL_MV // M_WIDTH)),
                  pl.BlockSpec((1, L, M_WIDTH), cb(COL_MO // M_WIDTH)),
                  pl.BlockSpec((1, L, M_WIDTH), cb(COL_MZ // M_WIDTH)),
                  pl.BlockSpec((1, L, LANES), cb(COL_GATES // LANES)),
                  pl.BlockSpec((CONV_K, M_WIDTH), full2),
                  pl.BlockSpec((1, M_WIDTH), full2),
                  pl.BlockSpec((M_HEADS, M_HD, M_HD), full3),
                  pl.BlockSpec((M_HEADS, M_HD, M_HD), full3),
                  pl.BlockSpec((1, M_WIDTH), full2),
                  pl.BlockSpec((1, M_WIDTH), full2),
                  pl.BlockSpec((1, LANES), full2)],
        out_specs=pl.BlockSpec((1, L, M_WIDTH), lambda bi, i: (bi, i, 0)),
        out_shape=jax.ShapeDtypeStruct((bsz, t, M_WIDTH), BF16),
        scratch_shapes=[pltpu.VMEM((M_HEADS, M_HD, M_HD), F32),
                        pltpu.VMEM((M_HEADS, 1, M_HD), F32),
                        pltpu.VMEM((M_HEADS, 1, LANES), F32),
                        pltpu.VMEM((8, M_WIDTH), F32)],
        compiler_params=_cparams(("parallel", "arbitrary")),
        name="mlstm_group",
    )(p, p, p, p, p, conv_w, conv_b, wq, wk, norm_w, skip, fb_row)


def _compress_kernel(kc0_ref, kc1_ref, vc0_ref, vc1_ref, posk_ref, posv_ref, w1k_ref, w1kc_ref, w2k_ref,
                     w1v_ref, w1vc_ref, w2v_ref, kcmp_ref, vcmpt_ref):
    nb = kcmp_ref.shape[2]

    def hidden(src_refs, pos_ref, w1_ref, w1c_ref):
        halves = [jnp.zeros((nb, 4 * CMP_HIDDEN), F32) for _ in src_refs]
        for l in range(CMP_STRIDE):
            for hf, r in enumerate(src_refs):
                xl = r[0, pl.ds(l, nb, stride=CMP_STRIDE), :]
                halves[hf] = halves[hf] + jnp.dot(xl.astype(BF16), w1c_ref[l], preferred_element_type=F32)
        acc = jnp.concatenate([hv[:, c * 2 * CMP_HIDDEN:(c + 1) * 2 * CMP_HIDDEN] for hv in halves for c in range(2)],
                              axis=0)
        first = acc[:, :CMP_HIDDEN]
        second = acc[:, CMP_HIDDEN:]
        posb = jnp.dot(jnp.broadcast_to(pos_ref[...], (8, CMP_LEN * N_HD)).astype(BF16), w1_ref[...],
                       preferred_element_type=F32)[0:1]
        hid = first + pltpu.roll(second, N_KV * nb - 1, 0) + posb
        return jax.nn.gelu(hid).astype(BF16)

    hk = hidden((kc0_ref, kc1_ref), posk_ref, w1k_ref, w1kc_ref)
    kc = jnp.dot(hk, w2k_ref[...], preferred_element_type=F32)
    hv = hidden((vc0_ref, vc1_ref), posv_ref, w1v_ref, w1vc_ref)
    vc = jnp.dot(hv, w2v_ref[...], preferred_element_type=F32)
    j = lax.broadcasted_iota(jnp.int32, (nb, 1), 0)
    lane = lax.broadcasted_iota(jnp.int32, (nb, N_HD), 1)
    n_real = (nb * CMP_STRIDE - CMP_LEN) // CMP_STRIDE + 1
    aux = jnp.where(lane < 3, (j * CMP_STRIDE).astype(F32),
                    jnp.where(lane < 6, (CMP_LEN - 1) * 0.5,
                              jnp.where((lane == 6) & (j >= n_real), 1.0, 0.0))).astype(BF16)
    ones_blk = jnp.where(lax.broadcasted_iota(jnp.int32, (VROWS - N_HD, nb), 0) == 0, 1.0, 0.0)
    for g in range(N_KV):
        kg = kc[g * nb:(g + 1) * nb, 0:N_HD]
        k_hi = kg.astype(BF16)
        k_lo = (kg - k_hi.astype(F32)).astype(BF16)
        kcmp_ref[0, g] = jnp.concatenate([k_hi, k_lo, k_hi, aux], axis=1)
        vt = vc[g * nb:(g + 1) * nb].T
        vcmpt_ref[0, g] = jnp.concatenate([vt[0:N_HD], ones_blk], axis=0).astype(BF16)


def _compress(p, posk, posv, w1k, w1kc, w2k, w1v, w1vc, w2v):
    bsz, t, _ = p.shape
    nb = t // CMP_STRIDE
    cb = lambda c: (lambda bi: (bi, 0, c))
    f2 = lambda bi: (0, 0)
    f3 = lambda bi: (0, 0, 0)
    wspecs = [pl.BlockSpec((CMP_LEN * N_HD, CMP_HIDDEN), f2),
              pl.BlockSpec((CMP_STRIDE, LANES, 4 * CMP_HIDDEN), f3),
              pl.BlockSpec((CMP_HIDDEN, LANES), f2)]
    return pl.pallas_call(
        _compress_kernel,
        grid=(bsz,),
        in_specs=[pl.BlockSpec((1, t, LANES), cb(COL_KC // LANES)),
                  pl.BlockSpec((1, t, LANES), cb(COL_KC // LANES + 1)),
                  pl.BlockSpec((1, t, LANES), cb(COL_VC // LANES)),
                  pl.BlockSpec((1, t, LANES), cb(COL_VC // LANES + 1)),
                  pl.BlockSpec((1, CMP_LEN * N_HD), f2),
                  pl.BlockSpec((1, CMP_LEN * N_HD), f2)] + wspecs + wspecs,
        out_specs=[pl.BlockSpec((1, N_KV, nb, 4 * N_HD), lambda bi: (bi, 0, 0, 0)),
                   pl.BlockSpec((1, N_KV, VROWS, nb), lambda bi: (bi, 0, 0, 0))],
        out_shape=[jax.ShapeDtypeStruct((bsz, N_KV, nb, 4 * N_HD), BF16),
                   jax.ShapeDtypeStruct((bsz, N_KV, VROWS, nb), BF16)],
        compiler_params=_cparams(("parallel",)),
        name="nsa_compress",
    )(p, p, p, p, posk, posv, w1k, w1kc, w2k, w1v, w1vc, w2v)


def _relayout_kernel(ks_ref, vs_ref, kw_ref, vw_ref, ksa_ref, vsa_ref, kwa_ref, vwa_ref):
    i = pl.program_id(1)
    is_pad = i == 0
    flag = jnp.where(is_pad, 1.0, 0.0)
    row = lax.broadcasted_iota(jnp.int32, (SKT, 1), 0)
    lane = lax.broadcasted_iota(jnp.int32, (SKT, N_HD), 1)
    al = lane - AUX_ROWS
    ones_blk = jnp.where(lax.broadcasted_iota(jnp.int32, (VROWS - N_HD, SKT), 0) == 0, 1.0, 0.0)
    for u in range(PAD_TILES):
        rows = slice(u * SKT, (u + 1) * SKT)
        base = ((i - 1) * PAD_TILES + u) * SKT
        pos = jnp.where(is_pad, 0, base + row)
        blk = pos // SLC_LEN
        p_hi = (blk * SLC_LEN).astype(F32)
        p_lo = (pos - blk * SLC_LEN).astype(F32)
        mid_w = jnp.where((al >= 0) & (al < 3), p_hi,
                          jnp.where((al >= 3) & (al < 6), p_lo, jnp.where(al == 6, flag, 0.0)))
        mid_s = jnp.where(lane == blk % SEL_GRP, 1.0, mid_w)
        ks = ks_ref[0, rows, :]
        kw = kw_ref[0, rows, :]
        for g in range(N_KV):
            sl = slice(g * N_HD, (g + 1) * N_HD)
            ksa_ref[0, g, rows, :] = jnp.concatenate([ks[:, sl], mid_s], axis=1).astype(BF16)
            kwa_ref[0, g, rows, :] = jnp.concatenate([kw[:, sl], mid_w], axis=1).astype(BF16)
        vst = vs_ref[0, rows, :].T
        vwt = vw_ref[0, rows, :].T
        for g in range(N_KV):
            sl = slice(g * N_HD, (g + 1) * N_HD)
            vsa_ref[0, g, u] = jnp.concatenate([vst[sl], ones_blk], axis=0).astype(BF16)
            vwg = jnp.concatenate([vwt[sl], ones_blk], axis=0).astype(BF16)
            for j in range(SKT // WKT):
                vwa_ref[0, g, u * (SKT // WKT) + j] = vwg[:, j * WKT:(j + 1) * WKT]


def _relayout(p):
    bsz, t, _ = p.shape
    nt = t // SKT + PAD_TILES
    wpt = SKT // WKT
    rt = PAD_TILES
    cb = lambda c: (lambda bi, i: (bi, jnp.maximum(i - 1, 0), c))
    return pl.pallas_call(
        _relayout_kernel,
        grid=(bsz, nt // rt),
        in_specs=[pl.BlockSpec((1, rt * SKT, KV_W), cb(COL_KS // KV_W)),
                  pl.BlockSpec((1, rt * SKT, KV_W), cb(COL_VS // KV_W)),
                  pl.BlockSpec((1, rt * SKT, KV_W), cb(COL_KW // KV_W)),
                  pl.BlockSpec((1, rt * SKT, KV_W), cb(COL_VW // KV_W))],
        out_specs=[pl.BlockSpec((1, N_KV, rt * SKT, KA_SEL), lambda bi, i: (bi, 0, i, 0)),
                   pl.BlockSpec((1, N_KV, rt, VROWS, SKT), lambda bi, i: (bi, 0, i, 0, 0)),
                   pl.BlockSpec((1, N_KV, rt * SKT, KA_WIN), lambda bi, i: (bi, 0, i, 0)),
                   pl.BlockSpec((1, N_KV, rt * wpt, VROWS, WKT), lambda bi, i: (bi, 0, i, 0, 0))],
        out_shape=[jax.ShapeDtypeStruct((bsz, N_KV, nt * SKT, KA_SEL), BF16),
                   jax.ShapeDtypeStruct((bsz, N_KV, nt, VROWS, SKT), BF16),
                   jax.ShapeDtypeStruct((bsz, N_KV, nt * SKT, KA_WIN), BF16),
                   jax.ShapeDtypeStruct((bsz, N_KV, nt * wpt, VROWS, WKT), BF16)],
        compiler_params=_cparams(("parallel", "parallel")),
        name="nsa_relayout",
    )(p, p, p, p)


def _nsa_batch_kernel(sp_ref, q_ref, gt_ref, z_ref, kcmp_ref, vcmpt_ref, ovt_ref, ks_ref, vs_ref, kw_ref,
                      vw_ref, tri_ref, wb_ref, y_ref, sa_scr, sb_scr, g_scr, imp_scr, rank_scr, part_scr, selb_scr,
                      tile_idx):
    g = pl.program_id(0)
    qb = pl.program_id(1)
    n_items = q_ref.shape[0]
    start = qb * QBLK
    nb = kcmp_ref.shape[2]
    n_slc = ovt_ref.shape[0]
    nw = N_HPG * QBLK
    n_top = min(SLC_TOPN, n_slc)
    t_row = start + lax.broadcasted_iota(jnp.int32, (1, QBLK), 1)
    s_bufs = (sa_scr, sb_scr)
    last = qb // (SKT // QBLK)

    def tile4(a):
        return jnp.concatenate([a] * N_HPG, axis=1)

    def col_reduce8(s, op):
        out = s[0:8]
        for r in range(1, s.shape[0] // 8):
            out = op(out, s[8 * r:8 * r + 8])
        return out

    def col_max(s):
        return jnp.max(col_reduce8(s, jnp.maximum), axis=0, keepdims=True)

    def normalise(acc):
        den = acc[N_HD:N_HD + 1]
        return acc[0:N_HD] / jnp.where(den > 0, den, 1.0)

    def gate_row(i, h, branch):
        return g_scr[i, pl.ds(GATE_NG + (g * N_HPG + h) * 3 + branch, 1), :]

    hl = lax.broadcasted_iota(jnp.int32, (AUX_ROWS, nw), 1) // QBLK
    ar = lax.broadcasted_iota(jnp.int32, (AUX_ROWS, nw), 0)

    def per_head(vals):
        out = jnp.full((AUX_ROWS, nw), vals[N_HPG - 1], F32)
        for h in range(N_HPG - 2, -1, -1):
            out = jnp.where(hl == h, vals[h], out)
        return out

    pieces = [per_head([sp_ref[(g * N_HPG + h) * 3 + j] for h in range(N_HPG)]) for j in range(3)]
    aux = jnp.where(ar == 6, NEG, 0.0)
    for j in range(3):
        aux = jnp.where((ar == j) | (ar == j + 3), pieces[j], aux)
    aux_b = aux.astype(BF16)
    q_tail = jnp.zeros((KA_SEL - N_HD - 2 * AUX_ROWS, nw), BF16)
    sidx = lax.broadcasted_iota(jnp.int32, (n_slc, QBLK), 0)
    cur = t_row // SLC_LEN
    valid = sidx <= cur
    sub8 = lax.broadcasted_iota(jnp.int32, (8, QBLK), 0)
    last_blk = (start + QBLK - 1) // SLC_LEN
    cmp_end = lax.broadcasted_iota(jnp.int32, (nb, 1), 0) * CMP_STRIDE + (CMP_LEN - 1)
    cmp_bias = tile4(jnp.where(cmp_end <= t_row, 0.0, NEG))
    wrows = WIN + QBLK

    def branches(i):
        qt = (q_ref[i] * (N_HD ** -0.5)).T
        q4l = jnp.concatenate([qt[h * N_HD:(h + 1) * N_HD] for h in range(N_HPG)], axis=1) * LOG2E
        q4s = q4l.astype(BF16)
        q4lo = (q4l - q4s.astype(F32)).astype(BF16)
        g_scr[i] = jax.nn.sigmoid(gt_ref[i]).T

        q_cmp = jnp.concatenate([q4s, q4s, q4lo, aux_b, jnp.zeros((N_HD - AUX_ROWS, nw), BF16)], axis=0)
        s_c = jnp.dot(kcmp_ref[i, 0], q_cmp, preferred_element_type=F32) + cmp_bias
        e_c = jnp.exp2(s_c - col_max(s_c))
        z_c = jnp.sum(col_reduce8(e_c, jnp.add), axis=0, keepdims=True)
        inv_c = jnp.where(tile4(t_row >= CMP_LEN - 1) & (z_c > 0), 1.0 / z_c, 0.0)
        o_c = jnp.dot(vcmpt_ref[i, 0], e_c.astype(BF16), preferred_element_type=F32)[0:N_HD] * inv_c
        p_c = e_c * inv_c
        psum = p_c[:, 0:QBLK]
        for h in range(1, N_HPG):
            psum = psum + p_c[:, h * QBLK:(h + 1) * QBLK]
        parts, rest = [], psum
        for _ in range(3):
            parts.append(rest.astype(BF16))
            rest = rest - parts[-1].astype(F32)
        imp = jnp.dot(ovt_ref[...], jnp.concatenate(parts, axis=0), preferred_element_type=F32)
        imp = jnp.where(valid, imp, -jnp.inf)
        imp_scr[i] = jnp.where((sidx == 0) | (sidx == cur), jnp.inf, imp)

        q_win = jnp.concatenate([q4s, jnp.zeros((AUX_ROWS, nw), BF16), aux_b,
                                 jnp.zeros((KA_WIN - N_HD - 2 * AUX_ROWS, nw), BF16)], axis=0)
        s_w = jnp.dot(kw_ref[i, 0, pl.ds(pl.multiple_of(start, QBLK), wrows), :], q_win,
                      preferred_element_type=F32)
        s_w = jnp.concatenate([s_w[0:WKT] + tile4(wb_ref[0]), s_w[WKT:wrows - WKT],
                               s_w[wrows - WKT:] + tile4(wb_ref[1])], axis=0)
        p_w = jnp.exp2(s_w - col_max(s_w)).astype(BF16)
        v_w = jnp.concatenate([vw_ref[i, 0, qb * (QBLK // WKT) + j] for j in range(wrows // WKT)], axis=1)
        o_w = normalise(jnp.dot(v_w, p_w, preferred_element_type=F32))
        for h in range(N_HPG):
            hs = slice(h * QBLK, (h + 1) * QBLK)
            part_scr[i, :, hs] = gate_row(i, h, 0) * o_c[:, hs] + gate_row(i, h, 2) * o_w[:, hs]
        return q4s

    q4 = [branches(i) for i in range(n_items)]

    rank_scr[...] = jnp.zeros_like(rank_scr)
    for ri in range(n_slc // 8):
        @pl.when((last_blk >= n_top) & (8 * ri <= last_blk))
        def _():
            for i in range(n_items):
                rows = imp_scr[i, 8 * ri:8 * ri + 8]
                for r in range(n_slc // 8):
                    blk8 = imp_scr[i, 8 * r:8 * r + 8]
                    acc = rank_scr[i, 8 * r:8 * r + 8]
                    for ii in range(8):
                        row = rows[ii:ii + 1]
                        if ri < r:
                            before = row >= blk8
                        elif ri > r:
                            before = row > blk8
                        else:
                            before = (row > blk8) | ((row == blk8) & (sub8 > ii))
                        acc = acc + jnp.where(before, 1.0, 0.0)
                    rank_scr[i, 8 * r:8 * r + 8] = acc

    def select(i):
        chosen = (rank_scr[i] < n_top) & valid
        selb_scr[i] = tile4(jnp.where(chosen, 0.0, NEG))

        any_q = jnp.max(jnp.where(chosen, 1.0, 0.0), axis=1, keepdims=True) > 0
        s_io = lax.broadcasted_iota(jnp.int32, (n_slc, 1), 0)
        bits = jnp.where(any_q, jnp.left_shift(1, s_io % 32), 0)
        words = [jnp.sum(jnp.where(s_io // 32 == w, bits, 0)) for w in range((n_slc + 31) // 32)]
        bpt = SKT // SLC_LEN
        cnt = jnp.int32(0)
        for tile in range(n_slc // bpt):
            nib = lax.shift_right_logical(words[tile * bpt // 32], jnp.int32(tile * bpt % 32)) & (2 ** bpt - 1)
            tile_idx[i, cnt] = tile
            cnt = cnt + ((nib != 0) & (tile < last)).astype(jnp.int32)
        return cnt

    n_act = [select(i) for i in range(n_items)]


    def sel_scores(i, kt):
        grp = pl.multiple_of((kt * SKT // SLC_LEN) // SEL_GRP * SEL_GRP, SEL_GRP)
        sel_rows = jnp.concatenate([selb_scr[i, pl.ds(grp, SEL_GRP), :], jnp.zeros((AUX_ROWS - SEL_GRP, nw), F32)],
                                   axis=0)
        q_sel = jnp.concatenate([q4[i], sel_rows.astype(BF16), aux_b, q_tail], axis=0)
        krow = pl.multiple_of((kt + PAD_TILES) * SKT, SKT)
        return jnp.dot(ks_ref[i, 0, pl.ds(krow, SKT), :], q_sel, preferred_element_type=F32)

    def tile_rows(kt):
        return pl.ds(pl.multiple_of(kt * SKT, SKT), SKT)

    def pass1(i):
        def fn(kt, mrun):
            s = sel_scores(i, kt)
            s_bufs[i % 2][tile_rows(kt), :] = s
            return jnp.maximum(mrun, col_reduce8(s, jnp.maximum))
        return fn

    def pass2(i, m_sel):
        def fn(kt, acc):
            pr = jnp.exp2(s_bufs[i % 2][tile_rows(kt), :] - m_sel).astype(BF16)
            return acc + jnp.dot(vs_ref[i, 0, kt + PAD_TILES], pr, preferred_element_type=F32)
        return fn

    def listed(i, fn):
        return lambda j, carry: fn(tile_idx[i, j], carry)

    def both(f1, f2):
        def fn(j, carry):
            return f1(j, carry[0]), f2(j, carry[1])
        return fn

    def run_tiles(fn, lo, n, init):
        def group(size, first):
            def body(j, carry):
                for u in range(size):
                    carry = fn(first + j * size + u, carry)
                return carry
            return body

        n_u = n // UNROLL
        carry = lax.fori_loop(0, n_u, group(UNROLL, lo), init)
        done = lo + n_u * UNROLL
        size = UNROLL // 2
        while size >= 1:
            take = (n // size) % 2
            carry = lax.fori_loop(0, take, group(size, done), carry)
            done = done + take * size
            size //= 2
        return carry

    def diag_scores(i):
        s = sel_scores(i, last) + tile4(tri_ref[qb % (SKT // QBLK)])
        s_bufs[i % 2][tile_rows(last), :] = s
        return col_reduce8(s, jnp.maximum)

    assert n_items <= len(s_bufs)
    acc0 = jnp.zeros((VROWS, nw), F32)
    m_diag = [diag_scores(i) for i in range(n_items)]
    m_prev = jnp.max(run_tiles(listed(0, pass1(0)), 0, n_act[0], m_diag[0]), axis=0, keepdims=True)
    closing = []
    for i in range(1, n_items):
        p2 = pass2(i - 1, m_prev)
        f1, f2 = listed(i, pass1(i)), listed(i - 1, p2)
        n_both = jnp.minimum(n_act[i], n_act[i - 1])
        mrun, acc = run_tiles(both(f1, f2), 0, n_both, (m_diag[i], acc0))
        mrun = run_tiles(f1, n_both, n_act[i] - n_both, mrun)
        closing.append((p2, run_tiles(f2, n_both, n_act[i - 1] - n_both, acc)))
        m_prev = jnp.max(mrun, axis=0, keepdims=True)
    p2 = pass2(n_items - 1, m_prev)
    closing.append((p2, run_tiles(listed(n_items - 1, p2), 0, n_act[n_items - 1], acc0)))
    outs = [normalise(p2(last, acc)) for p2, acc in closing]

    for i in range(n_items):
        tot = [part_scr[i, :, h * QBLK:(h + 1) * QBLK] + gate_row(i, h, 1) * outs[i][:, h * QBLK:(h + 1) * QBLK]
               for h in range(N_HPG)]
        o = jnp.concatenate(tot, axis=0).T
        y_ref[i] = (o * _silu(z_ref[i])).astype(BF16)


def _edge_biases():
    kl = np.arange(SKT)[:, None]
    ql = np.arange(QBLK)[None, :]
    tri = np.stack([np.where(kl <= par * QBLK + ql, 0.0, NEG) for par in range(SKT // QBLK)])
    kk = np.arange(WKT)[:, None]
    wb = np.stack([np.where(kk > ql, 0.0, NEG), np.where(kk <= ql + WKT - QBLK, 0.0, NEG)])
    return jnp.asarray(tri, F32), jnp.asarray(wb, F32)


def _nsa(p, spieces, kcmp, vcmpt, ovt, ks, vs, kw, vw):
    bsz, t, _ = p.shape
    n_slc = t // SLC_LEN
    gw = N_HPG * N_HD
    tri, wb = _edge_biases()
    nw = N_HPG * QBLK
    per_g = lambda a: pl.BlockSpec((bsz, 1) + a.shape[2:], lambda g, i, sp: (0, g) + (0,) * (a.ndim - 2))
    const = lambda a: pl.BlockSpec(a.shape, lambda g, i, sp: (0,) * a.ndim)
    grid_spec = pltpu.PrefetchScalarGridSpec(
        num_scalar_prefetch=1,
        grid=(N_KV, t // QBLK),
        in_specs=[pl.BlockSpec((bsz, QBLK, gw), lambda g, i, sp: (0, i, COL_NQ // gw + g)),
                  pl.BlockSpec((bsz, QBLK, LANES), lambda g, i, sp: (0, i, COL_GATES // LANES)),
                  pl.BlockSpec((bsz, QBLK, gw), lambda g, i, sp: (0, i, COL_NZ // gw + g)),
                  per_g(kcmp), per_g(vcmpt), const(ovt),
                  per_g(ks), per_g(vs), per_g(kw), per_g(vw), const(tri), const(wb)],
        out_specs=pl.BlockSpec((bsz, QBLK, gw), lambda g, i, sp: (0, i, g)),
        scratch_shapes=[pltpu.VMEM((t, nw), F32),
                        pltpu.VMEM((t, nw), F32),
                        pltpu.VMEM((bsz, LANES, QBLK), F32),
                        pltpu.VMEM((bsz, n_slc, QBLK), F32),
                        pltpu.VMEM((bsz, n_slc, QBLK), F32),
                        pltpu.VMEM((bsz, N_HD, nw), F32),
                        pltpu.VMEM((bsz, n_slc, nw), F32),
                        pltpu.SMEM((bsz, t // SKT), jnp.int32)],
    )
    return pl.pallas_call(
        _nsa_batch_kernel,
        grid_spec=grid_spec,
        out_shape=jax.ShapeDtypeStruct((bsz, t, N_WIDTH), BF16),
        compiler_params=_cparams(("arbitrary", "arbitrary")),
        name="nsa_attention",
    )(spieces, p, p, p, kcmp, vcmpt, ovt, ks, vs, kw, vw, tri, wb)


def _outproj_kernel(ym_ref, yn_ref, w_ref, x_ref, gate_ref, fg_ref, o_ref, wb_scr, *, final):
    @pl.when((pl.program_id(0) == 0) & (pl.program_id(1) == 0))
    def _():
        rows = w_ref.shape[1]
        for c in range(rows // WP_TILE):
            cs = slice(c * WP_TILE, (c + 1) * WP_TILE)
            wb_scr[cs, :] = w_ref[0, cs, :].astype(BF16)

    y = jnp.dot(ym_ref[0], wb_scr[0:M_WIDTH, :], preferred_element_type=F32)
    y = y + jnp.dot(yn_ref[0], wb_scr[M_WIDTH:, :], preferred_element_type=F32)
    hres = x_ref[0] + gate_ref[0] * y
    if final:
        ms = jnp.mean(hres * hres, axis=-1, keepdims=True)
        hres = hres * lax.rsqrt(ms + EPS) * fg_ref[...]
    o_ref[0] = hres


def _outproj(ym, yn, w, layer, x, gate, fg, final):
    bsz, t, d = x.shape
    tm = 512
    return pl.pallas_call(
        functools.partial(_outproj_kernel, final=final),
        grid=(bsz, t // tm),
        in_specs=[pl.BlockSpec((1, tm, M_WIDTH), lambda bi, i: (bi, i, 0)),
                  pl.BlockSpec((1, tm, N_WIDTH), lambda bi, i: (bi, i, 0)),
                  pl.BlockSpec((1, M_WIDTH + N_WIDTH, d), lambda bi, i: (layer, 0, 0),
                               pipeline_mode=pl.Buffered(1)),
                  pl.BlockSpec((1, tm, d), lambda bi, i: (bi, i, 0)),
                  pl.BlockSpec((1, 1, d), lambda bi, i: (bi, 0, 0)),
                  pl.BlockSpec((1, d), lambda bi, i: (0, 0))],
        out_specs=pl.BlockSpec((1, tm, d), lambda bi, i: (bi, i, 0)),
        out_shape=jax.ShapeDtypeStruct((bsz, t, d), F32),
        scratch_shapes=[pltpu.VMEM((M_WIDTH + N_WIDTH, d), BF16)],
        compiler_params=_cparams(("arbitrary", "arbitrary")),
        name="outproj_residual",
    )(ym, yn, w, x, gate, fg)


SRC_MI = 4 * M_WIDTH
SRC_NQ = SRC_MI + 2 * M_HEADS
SRC_NG = SRC_NQ + N_WIDTH + 6 * KV_W
SRC_NZ = SRC_NG + 3 * N_HEADS


def _reorder_cols(a):
    parts = [a[..., 0:SRC_MI], a[..., SRC_NQ:SRC_NG], a[..., SRC_NZ:SRC_NZ + N_WIDTH], a[..., SRC_MI:SRC_NQ],
             a[..., SRC_NG:SRC_NZ]]
    used = sum(x.shape[-1] for x in parts)
    parts.append(jnp.zeros(a.shape[:-1] + (NP_PAD - used,), a.dtype))
    return jnp.concatenate(parts, axis=-1)


WP_TILE = 512
WP_GATE_TILE = COL_GATES // WP_TILE


def _wprep_kernel(w_ref, g1_ref, g2_ref, o_ref):
    j = pl.program_id(0)
    d = o_ref.shape[0]

    @pl.when(j < WP_GATE_TILE)
    def _():
        for c in range(d // WP_TILE):
            cs = slice(c * WP_TILE, (c + 1) * WP_TILE)
            o_ref[cs, :] = w_ref[0, :, cs].T.astype(BF16)

    @pl.when(j == WP_GATE_TILE)
    def _():
        n_gate = g1_ref.shape[1] + g2_ref.shape[1]
        gt = jnp.concatenate([g1_ref[0], g2_ref[0], jnp.zeros((LANES - n_gate, d), F32)], axis=0)
        for c in range(d // WP_TILE):
            cs = slice(c * WP_TILE, (c + 1) * WP_TILE)
            o_ref[cs, 0:LANES] = gt[:, cs].T.astype(BF16)
        o_ref[:, LANES:] = jnp.zeros((d, WP_TILE - LANES), BF16)


def _wprep(w_t, layer):
    _, n, d = w_t.shape
    n_big = COL_GATES // WP_TILE
    assert n == SRC_NZ + N_WIDTH and NP_PAD // WP_TILE == n_big + 1

    def src_row(j):
        return jnp.where(j < COL_NQ // WP_TILE, j * WP_TILE,
                         jnp.where(j < COL_NZ // WP_TILE, SRC_NQ + (j - COL_NQ // WP_TILE) * WP_TILE,
                                   jnp.where(j < n_big, SRC_NZ + (j - COL_NZ // WP_TILE) * WP_TILE, 0)))

    el = pl.Element
    return pl.pallas_call(
        _wprep_kernel,
        grid=(NP_PAD // WP_TILE,),
        in_specs=[pl.BlockSpec((el(1), el(WP_TILE), el(d)), lambda j: (layer, pl.multiple_of(src_row(j), 8), 0)),
                  pl.BlockSpec((el(1), el(SRC_NQ - SRC_MI), el(d)), lambda j: (layer, SRC_MI, 0)),
                  pl.BlockSpec((el(1), el(SRC_NZ - SRC_NG), el(d)), lambda j: (layer, SRC_NG, 0))],
        out_specs=pl.BlockSpec((d, WP_TILE), lambda j: (0, j)),
        out_shape=jax.ShapeDtypeStruct((d, NP_PAD), BF16),
        compiler_params=_cparams(("parallel",)),
        name="inproj_weight_prep",
    )(w_t, w_t, w_t)


def _overlap_t(t):
    n_cmp_rows = t // CMP_STRIDE
    n_slc = t // SLC_LEN
    c0 = np.arange(n_cmp_rows) * CMP_STRIDE
    s0 = np.arange(n_slc) * SLC_LEN
    ov = (c0[None, :] <= s0[:, None] + SLC_LEN - 1) & (c0[None, :] + CMP_LEN - 1 >= s0[:, None])
    ov[:, (t - CMP_LEN) // CMP_STRIDE + 1:] = False
    return jnp.asarray(np.concatenate([ov] * 3, axis=1), BF16)


def kernel(x, c, ln_g, w_ada, b_ada, w_in, b_in, m_conv_w, m_conv_b, m_wq, m_wk, m_norm_w, m_skip, m_f_bias,
           n_pos_k, n_pos_v, n_w1_k, n_w2_k, n_w1_v, n_w2_v, w_out, final_g):
    out_dtype = x.dtype
    bsz, t, d = x.shape
    depth = ln_g.shape[0]
    h_res = x.astype(F32)
    assert bsz <= 2 and t % 1024 == 0 and t // CMP_STRIDE <= 256
    c_t =jnp.zeros((d, 8), F32).at[:, :bsz].set(c.astype(F32).T)
    slopes_np = np.array([2.0 ** (-8.0 * (h + 1) / N_HEADS) for h in range(N_HEADS)], np.float32)
    rest = (slopes_np.astype(np.float64) * LOG2E).astype(np.float32)
    pieces = []
    for _ in range(3):
        pieces.append(rest.astype(jnp.bfloat16).astype(np.float32))
        rest = rest - pieces[-1]
    spieces = jnp.asarray(np.stack(pieces, axis=1).reshape(-1))
    ovt = _overlap_t(t)

    def w1cat(w1):
        w = w1.reshape(2, CMP_STRIDE, N_HD, CMP_HIDDEN)
        w = jnp.concatenate([w[0], w[1]], axis=-1).astype(BF16)
        z = jnp.zeros_like(w)
        return jnp.concatenate([jnp.concatenate([w, z], axis=-1), jnp.concatenate([z, w], axis=-1)], axis=1)

    def w2pad(w2):
        return jnp.pad(w2, ((0, 0), (0, LANES - N_HD))).astype(BF16)

    for l in range(depth):
        mod = _ada(c_t, w_ada, b_ada[l][None, :], l, bsz)[:bsz]
        shift, scale, gate = mod[:, None, 0:d], mod[:, None, d:2 * d], mod[:, None, 2 * d:3 * d]
        p = _inproj(h_res, ln_g[l][None, :], scale, shift,
                    _wprep(jnp.swapaxes(w_in, 1, 2), l), _reorder_cols(b_in[l])[None, :])
        fb_row = jnp.zeros((1, LANES), F32).at[0, M_HEADS:2 * M_HEADS].set(m_f_bias[l])
        y_m = _mlstm(p, m_conv_w[l], m_conv_b[l][None, :], m_wq[l].astype(BF16), m_wk[l].astype(BF16),
                     m_norm_w[l][None, :], m_skip[l][None, :], fb_row)
        kcmp, vcmpt = _compress(p, n_pos_k[l].reshape(1, -1), n_pos_v[l].reshape(1, -1),
                                n_w1_k[l].astype(BF16), w1cat(n_w1_k[l]), w2pad(n_w2_k[l]),
                                n_w1_v[l].astype(BF16), w1cat(n_w1_v[l]), w2pad(n_w2_v[l]))
        ks, vs, kw, vw = _relayout(p)
        y_n = _nsa(p, spieces, kcmp, vcmpt, ovt, ks, vs, kw, vw)
        h_res = _outproj(y_m, y_n, w_out, l, h_res, gate, final_g[None, :], l == depth - 1)
    return h_res.astype(out_dtype)
```

```python
import functools

import numpy as np
import jax
import jax.numpy as jnp
from jax import lax
from jax.experimental import pallas as pl
from jax.experimental.pallas import tpu as pltpu

F32 = jnp.float32
BF16 = jnp.bfloat16

EPS = 1e-6
M_HEADS = 4
M_HD = 256
M_WIDTH = M_HEADS * M_HD
CONV_K = 4
M_CHUNK = 256
N_HEADS = 16
N_HD = 64
N_KV = 4
N_HPG = N_HEADS // N_KV
N_WIDTH = N_HEADS * N_HD
KV_W = N_KV * N_HD
CMP_LEN = 32
CMP_STRIDE = 16
CMP_HIDDEN = 2 * N_HD
SLC_LEN = 64
SLC_TOPN = 16
WIN = 512
QBLK = 256
SKT = 256
WKT = 256
PAD_TILES = WIN // SKT
SEL_GRP = 8
KA_SEL = 128
KA_WIN = 128
VROWS = 80
UNROLL = 4
AUX_ROWS = 16
LOG2E = 1.4426950408889634

COL_MX, COL_MV, COL_MO, COL_MZ = 0, 1024, 2048, 3072
COL_NQ = 4096
COL_KC, COL_VC, COL_KS, COL_VS, COL_KW, COL_VW = 5120, 5376, 5632, 5888, 6144, 6400
COL_NZ = 6656
COL_GATES = 7680
GATE_NG = 2 * M_HEADS
NP_PAD = 8192
NORM_CHUNKS = 2
LANES = 128
NEG = -1e30
VMEM_LIMIT = 58 * 1024 * 1024


def _cparams(sem):
    return pltpu.CompilerParams(dimension_semantics=sem, vmem_limit_bytes=VMEM_LIMIT)


def _silu(x):
    return x * jax.nn.sigmoid(x)


def _log_sigmoid(x):
    return jnp.minimum(x, 0.0) - jnp.log1p(jnp.exp(-jnp.abs(x)))


def _ada_kernel(ct_ref, w_ref, b_ref, o_ref, *, bsz):
    s_t = _silu(ct_ref[...])
    w = w_ref[0]
    row = lax.broadcasted_iota(jnp.int32, o_ref.shape, 0)
    out = jnp.zeros(o_ref.shape, F32)
    for b in range(bsz):
        prod = w * s_t[:, b:b + 1]
        acc = prod[0:8]
        for r in range(1, prod.shape[0] // 8):
            acc = acc + prod[8 * r:8 * r + 8]
        out = jnp.where(row == b, jnp.sum(acc, axis=0, keepdims=True) + b_ref[...], out)
    o_ref[...] = out


def _ada(c_t, w, b, layer, bsz):
    _, d, n = w.shape
    tn = 1024
    return pl.pallas_call(
        functools.partial(_ada_kernel, bsz=bsz),
        grid=(n // tn,),
        in_specs=[pl.BlockSpec((d, 8), lambda j: (0, 0)),
                  pl.BlockSpec((1, d, tn), lambda j: (layer, 0, j)),
                  pl.BlockSpec((1, tn), lambda j: (0, j))],
        out_specs=pl.BlockSpec((8, tn), lambda j: (0, j)),
        out_shape=jax.ShapeDtypeStruct((8, n), F32),
        compiler_params=_cparams(("parallel",)),
        name="ada_mod",
    )(c_t, w, b)


def _inproj_kernel(x_ref, g_ref, sc_ref, sh_ref, w_ref, b_ref, o_ref, h_ref):
    first = pl.program_id(2) == 0

    @pl.when(first)
    def _():
        tm = x_ref.shape[1]
        ck = tm // NORM_CHUNKS
        for c in range(NORM_CHUNKS):
            rows = slice(c * ck, (c + 1) * ck)
            x = x_ref[0, rows, :]
            ms = jnp.mean(x * x, axis=-1, keepdims=True)
            h = x * lax.rsqrt(ms + EPS) * g_ref[...]
            h = (h * (1.0 + sc_ref[0]) + sh_ref[0]).astype(BF16)
            h_ref[rows, :] = h
            o_ref[0, rows, :] = jnp.dot(h, w_ref[...], preferred_element_type=F32) + b_ref[...]

    @pl.when(jnp.logical_not(first))
    def _():
        o_ref[0] = jnp.dot(h_ref[...], w_ref[...], preferred_element_type=F32) + b_ref[...]


def _inproj(x, g, scale, shift, w, b):
    bsz, t, d = x.shape
    n = w.shape[1]
    tm, tn = 1024, 2048
    return pl.pallas_call(
        _inproj_kernel,
        grid=(bsz, t // tm, n // tn),
        in_specs=[pl.BlockSpec((1, tm, d), lambda bi, i, j: (bi, i, 0)),
                  pl.BlockSpec((1, d), lambda bi, i, j: (0, 0)),
                  pl.BlockSpec((1, 1, d), lambda bi, i, j: (bi, 0, 0)),
                  pl.BlockSpec((1, 1, d), lambda bi, i, j: (bi, 0, 0)),
                  pl.BlockSpec((d, tn), lambda bi, i, j: (0, j)),
                  pl.BlockSpec((1, tn), lambda bi, i, j: (0, j))],
        out_specs=pl.BlockSpec((1, tm, tn), lambda bi, i, j: (bi, i, j)),
        out_shape=jax.ShapeDtypeStruct((bsz, t, n), F32),
        scratch_shapes=[pltpu.VMEM((tm, d), BF16)],
        compiler_params=_cparams(("parallel", "parallel", "arbitrary")),
        name="norm_inproj",
    )(x, g, scale, shift, w, b)


def _mlstm_kernel(x_ref, v_ref, o_ref, z_ref, gt_ref, cw_ref, cb_ref, wq_ref, wk_ref, nw_ref, sk_ref, fb_ref,
                  y_ref, c_scr, n_scr, m_scr, xp_scr):
    L = M_CHUNK

    @pl.when(pl.program_id(1) == 0)
    def _():
        c_scr[...] = jnp.zeros_like(c_scr)
        n_scr[...] = jnp.zeros_like(n_scr)
        m_scr[...] = jnp.zeros_like(m_scr)
        xp_scr[...] = jnp.zeros_like(xp_scr)

    x = x_ref[0]
    prev = xp_scr[...]
    row8 = lax.broadcasted_iota(jnp.int32, (8, M_WIDTH), 0)
    cw = cw_ref[...]
    xc = cb_ref[...] + x * cw[CONV_K - 1:CONV_K, :]
    for sft in range(1, CONV_K):
        xr = pltpu.roll(x, sft, 0)
        top = jnp.where(row8 < sft, pltpu.roll(prev, sft, 0), xr[0:8])
        xs = jnp.concatenate([top, xr[8:]], axis=0)
        xc = xc + xs * cw[CONV_K - 1 - sft:CONV_K - sft, :]
    xp_scr[...] = x[L - 8:L]
    xc = _silu(xc)

    gt = gt_ref[0]
    col = lax.broadcasted_iota(jnp.int32, (L, LANES), 1)
    logf = _log_sigmoid(gt + fb_ref[...])
    a_c = jnp.where((col >= M_HEADS) & (col < 2 * M_HEADS), logf, gt)
    ri = lax.broadcasted_iota(jnp.int32, (L, L), 0)
    ci = lax.broadcasted_iota(jnp.int32, (L, L), 1)
    causal = ri >= ci
    hp = lax.Precision.HIGHEST
    tri = causal.astype(F32)
    tri_t = (ri <= ci).astype(F32)
    b_c = jnp.dot(tri, a_c, precision=hp, preferred_element_type=F32)
    a_r = a_c.T
    b_r = jnp.dot(a_r[0:8], tri_t, precision=hp, preferred_element_type=F32)

    for h in range(M_HEADS):
        sl = slice(h * M_HD, (h + 1) * M_HD)
        xh = xc[:, sl]
        xb = xh.astype(BF16)
        q = jnp.dot(xb, wq_ref[h], preferred_element_type=F32)
        k = jnp.dot(xb, wk_ref[h], preferred_element_type=F32) * (M_HD ** -0.5)
        vb = v_ref[0, :, sl].astype(BF16)
        qb = q.astype(BF16)
        kb = k.astype(BF16)

        bt = b_c[:, M_HEADS + h:M_HEADS + h + 1]
        ic = a_c[:, h:h + 1]
        bs = b_r[M_HEADS + h:M_HEADS + h + 1, :]
        ir = a_r[h:h + 1, :]
        m_prev = m_scr[h][:, 0:1]

        dm = jnp.where(causal, bt - bs + ir, -jnp.inf)
        inter = bt + m_prev
        m_t = jnp.maximum(inter, jnp.max(dm, axis=-1, keepdims=True))
        w_in = jnp.exp(dm - m_t)
        w_st = jnp.exp(inter - m_t)
        s = lax.dot_general(qb, kb, (((1,), (1,)), ((), ())), preferred_element_type=F32) * w_in
        cmat = c_scr[h]
        nvec = n_scr[h]
        sb = s.astype(BF16)
        num = w_st * jnp.dot(qb, cmat.astype(BF16), preferred_element_type=F32) \
            + jnp.dot(sb, vb, preferred_element_type=F32)
        nt_dims = (((1,), (1,)), ((), ()))
        qn = lax.dot_general(qb, jnp.broadcast_to(nvec, (8, M_HD)).astype(BF16), nt_dims,
                             preferred_element_type=F32)[:, 0:1]
        ssum = lax.dot_general(sb, jnp.ones((8, L), BF16), nt_dims, preferred_element_type=F32)[:, 0:1]
        den = w_st * qn + ssum
        hh = num / jnp.maximum(jnp.abs(den), jnp.exp(-m_t))

        b_last = bt[L - 1:L, :]
        w_end = b_last - bt + ic
        m_new = jnp.maximum(b_last + m_prev, jnp.max(w_end, axis=0, keepdims=True))
        decay = jnp.exp(b_last + m_prev - m_new)
        kwt = k * jnp.exp(w_end - m_new)
        c_scr[h] = decay * cmat + lax.dot_general(kwt.astype(BF16), vb, (((0,), (0,)), ((), ())),
                                                  preferred_element_type=F32)
        n_scr[h] = decay * nvec + jnp.sum(kwt, axis=0, keepdims=True)
        m_scr[h] = jnp.broadcast_to(m_new, (1, LANES))

        mu = jnp.mean(hh, axis=-1, keepdims=True)
        hc = hh - mu
        var = jnp.mean(hc * hc, axis=-1, keepdims=True)
        hn = hc * lax.rsqrt(var + EPS) * nw_ref[:, sl]
        out = jax.nn.sigmoid(o_ref[0, :, sl]) * hn + sk_ref[:, sl] * xh
        y_ref[0, :, sl] = (out * _silu(z_ref[0, :, sl])).astype(BF16)


def _mlstm(p, conv_w, conv_b, wq, wk, norm_w, skip, fb_row):
    bsz, t, _ = p.shape
    L = M_CHUNK
    cb = lambda c: (lambda bi, i: (bi, i, c))
    full2 = lambda bi, i: (0, 0)
    full3 = lambda bi, i: (0, 0, 0)
    return pl.pallas_call(
        _mlstm_kernel,
        grid=(bsz, t // L),
        in_specs=[pl.BlockSpec((1, L, M_WIDTH), cb(COL_MX // M_WIDTH)),
                  pl.BlockSpec((1, L, M_WIDTH), cb(COL_MV // M_WIDTH)),
                  pl.BlockSpec((1, L, M_WIDTH), cb(COL_MO // M_WIDTH)),
                  pl.BlockSpec((1, L, M_WIDTH), cb(COL_MZ // M_WIDTH)),
                  pl.BlockSpec((1, L, LANES), cb(COL_GATES // LANES)),
                  pl.BlockSpec((CONV_K, M_WIDTH), full2),
                  pl.BlockSpec((1, M_WIDTH), full2),
                  pl.BlockSpec((M_HEADS, M_HD, M_HD), full3),
                  pl.BlockSpec((M_HEADS, M_HD, M_HD), full3),
                  pl.BlockSpec((1, M_WIDTH), full2),
                  pl.BlockSpec((1, M_WIDTH), full2),
                  pl.BlockSpec((1, LANES), full2)],
        out_specs=pl.BlockSpec((1, L, M_WIDTH), lambda bi, i: (bi, i, 0)),
        out_shape=jax.ShapeDtypeStruct((bsz, t, M_WIDTH), BF16),
        scratch_shapes=[pltpu.VMEM((M_HEADS, M_HD, M_HD), F32),
                        pltpu.VMEM((M_HEADS, 1, M_HD), F32),
                        pltpu.VMEM((M_HEADS, 1, LANES), F32),
                        pltpu.VMEM((8, M_WIDTH), F32)],
        compiler_params=_cparams(("parallel", "arbitrary")),
        name="mlstm_group",
    )(p, p, p, p, p, conv_w, conv_b, wq, wk, norm_w, skip, fb_row)


def _compress_kernel(kc0_ref, kc1_ref, vc0_ref, vc1_ref, posk_ref, posv_ref, w1k_ref, w1kc_ref, w2k_ref,
                     w1v_ref, w1vc_ref, w2v_ref, kcmp_ref, vcmpt_ref):
    nb = kcmp_ref.shape[2]

    def hidden(src_refs, pos_ref, w1_ref, w1c_ref):
        halves = [jnp.zeros((nb, 4 * CMP_HIDDEN), F32) for _ in src_refs]
        for l in range(CMP_STRIDE):
            for hf, r in enumerate(src_refs):
                xl = r[0, pl.ds(l, nb, stride=CMP_STRIDE), :]
                halves[hf] = halves[hf] + jnp.dot(xl.astype(BF16), w1c_ref[l], preferred_element_type=F32)
        acc = jnp.concatenate([hv[:, c * 2 * CMP_HIDDEN:(c + 1) * 2 * CMP_HIDDEN] for hv in halves for c in range(2)],
                              axis=0)
        first = acc[:, :CMP_HIDDEN]
        second = acc[:, CMP_HIDDEN:]
        posb = jnp.dot(jnp.broadcast_to(pos_ref[...], (8, CMP_LEN * N_HD)).astype(BF16), w1_ref[...],
                       preferred_element_type=F32)[0:1]
        hid = first + pltpu.roll(second, N_KV * nb - 1, 0) + posb
        return jax.nn.gelu(hid).astype(BF16)

    hk = hidden((kc0_ref, kc1_ref), posk_ref, w1k_ref, w1kc_ref)
    kc = jnp.dot(hk, w2k_ref[...], preferred_element_type=F32)
    hv = hidden((vc0_ref, vc1_ref), posv_ref, w1v_ref, w1vc_ref)
    vc = jnp.dot(hv, w2v_ref[...], preferred_element_type=F32)
    j = lax.broadcasted_iota(jnp.int32, (nb, 1), 0)
    lane = lax.broadcasted_iota(jnp.int32, (nb, N_HD), 1)
    n_real = (nb * CMP_STRIDE - CMP_LEN) // CMP_STRIDE + 1
    aux = jnp.where(lane < 3, (j * CMP_STRIDE).astype(F32),
                    jnp.where(lane < 6, (CMP_LEN - 1) * 0.5,
                              jnp.where((lane == 6) & (j >= n_real), 1.0, 0.0))).astype(BF16)
    ones_blk = jnp.where(lax.broadcasted_iota(jnp.int32, (VROWS - N_HD, nb), 0) == 0, 1.0, 0.0)
    for g in range(N_KV):
        kg = kc[g * nb:(g + 1) * nb, 0:N_HD]
        k_hi = kg.astype(BF16)
        k_lo = (kg - k_hi.astype(F32)).astype(BF16)
        kcmp_ref[0, g] = jnp.concatenate([k_hi, k_lo, k_hi, aux], axis=1)
        vt = vc[g * nb:(g + 1) * nb].T
        vcmpt_ref[0, g] = jnp.concatenate([vt[0:N_HD], ones_blk], axis=0).astype(BF16)


def _compress(p, posk, posv, w1k, w1kc, w2k, w1v, w1vc, w2v):
    bsz, t, _ = p.shape
    nb = t // CMP_STRIDE
    cb = lambda c: (lambda bi: (bi, 0, c))
    f2 = lambda bi: (0, 0)
    f3 = lambda bi: (0, 0, 0)
    wspecs = [pl.BlockSpec((CMP_LEN * N_HD, CMP_HIDDEN), f2),
              pl.BlockSpec((CMP_STRIDE, LANES, 4 * CMP_HIDDEN), f3),
              pl.BlockSpec((CMP_HIDDEN, LANES), f2)]
    return pl.pallas_call(
        _compress_kernel,
        grid=(bsz,),
        in_specs=[pl.BlockSpec((1, t, LANES), cb(COL_KC // LANES)),
                  pl.BlockSpec((1, t, LANES), cb(COL_KC // LANES + 1)),
                  pl.BlockSpec((1, t, LANES), cb(COL_VC // LANES)),
                  pl.BlockSpec((1, t, LANES), cb(COL_VC // LANES + 1)),
                  pl.BlockSpec((1, CMP_LEN * N_HD), f2),
                  pl.BlockSpec((1, CMP_LEN * N_HD), f2)] + wspecs + wspecs,
        out_specs=[pl.BlockSpec((1, N_KV, nb, 4 * N_HD), lambda bi: (bi, 0, 0, 0)),
                   pl.BlockSpec((1, N_KV, VROWS, nb), lambda bi: (bi, 0, 0, 0))],
        out_shape=[jax.ShapeDtypeStruct((bsz, N_KV, nb, 4 * N_HD), BF16),
                   jax.ShapeDtypeStruct((bsz, N_KV, VROWS, nb), BF16)],
        compiler_params=_cparams(("parallel",)),
        name="nsa_compress",
    )(p, p, p, p, posk, posv, w1k, w1kc, w2k, w1v, w1vc, w2v)


def _relayout_kernel(ks_ref, vs_ref, kw_ref, vw_ref, ksa_ref, vsa_ref, kwa_ref, vwa_ref):
    i = pl.program_id(1)
    is_pad = i == 0
    flag = jnp.where(is_pad, 1.0, 0.0)
    row = lax.broadcasted_iota(jnp.int32, (SKT, 1), 0)
    lane = lax.broadcasted_iota(jnp.int32, (SKT, N_HD), 1)
    al = lane - AUX_ROWS
    ones_blk = jnp.where(lax.broadcasted_iota(jnp.int32, (VROWS - N_HD, SKT), 0) == 0, 1.0, 0.0)
    for u in range(PAD_TILES):
        rows = slice(u * SKT, (u + 1) * SKT)
        base = ((i - 1) * PAD_TILES + u) * SKT
        pos = jnp.where(is_pad, 0, base + row)
        blk = pos // SLC_LEN
        p_hi = (blk * SLC_LEN).astype(F32)
        p_lo = (pos - blk * SLC_LEN).astype(F32)
        mid_w = jnp.where((al >= 0) & (al < 3), p_hi,
                          jnp.where((al >= 3) & (al < 6), p_lo, jnp.where(al == 6, flag, 0.0)))
        mid_s = jnp.where(lane == blk % SEL_GRP, 1.0, mid_w)
        ks = ks_ref[0, rows, :]
        kw = kw_ref[0, rows, :]
        for g in range(N_KV):
            sl = slice(g * N_HD, (g + 1) * N_HD)
            ksa_ref[0, g, rows, :] = jnp.concatenate([ks[:, sl], mid_s], axis=1).astype(BF16)
            kwa_ref[0, g, rows, :] = jnp.concatenate([kw[:, sl], mid_w], axis=1).astype(BF16)
        vst = vs_ref[0, rows, :].T
        vwt = vw_ref[0, rows, :].T
        for g in range(N_KV):
            sl = slice(g * N_HD, (g + 1) * N_HD)
            vsa_ref[0, g, u] = jnp.concatenate([vst[sl], ones_blk], axis=0).astype(BF16)
            vwg = jnp.concatenate([vwt[sl], ones_blk], axis=0).astype(BF16)
            for j in range(SKT // WKT):
                vwa_ref[0, g, u * (SKT // WKT) + j] = vwg[:, j * WKT:(j + 1) * WKT]


def _relayout(p):
    bsz, t, _ = p.shape
    nt = t // SKT + PAD_TILES
    wpt = SKT // WKT
    rt = PAD_TILES
    cb = lambda c: (lambda bi, i: (bi, jnp.maximum(i - 1, 0), c))
    return pl.pallas_call(
        _relayout_kernel,
        grid=(bsz, nt // rt),
        in_specs=[pl.BlockSpec((1, rt * SKT, KV_W), cb(COL_KS // KV_W)),
                  pl.BlockSpec((1, rt * SKT, KV_W), cb(COL_VS // KV_W)),
                  pl.BlockSpec((1, rt * SKT, KV_W), cb(COL_KW // KV_W)),
                  pl.BlockSpec((1, rt * SKT, KV_W), cb(COL_VW // KV_W))],
        out_specs=[pl.BlockSpec((1, N_KV, rt * SKT, KA_SEL), lambda bi, i: (bi, 0, i, 0)),
                   pl.BlockSpec((1, N_KV, rt, VROWS, SKT), lambda bi, i: (bi, 0, i, 0, 0)),
                   pl.BlockSpec((1, N_KV, rt * SKT, KA_WIN), lambda bi, i: (bi, 0, i, 0)),
                   pl.BlockSpec((1, N_KV, rt * wpt, VROWS, WKT), lambda bi, i: (bi, 0, i, 0, 0))],
        out_shape=[jax.ShapeDtypeStruct((bsz, N_KV, nt * SKT, KA_SEL), BF16),
                   jax.ShapeDtypeStruct((bsz, N_KV, nt, VROWS, SKT), BF16),
                   jax.ShapeDtypeStruct((bsz, N_KV, nt * SKT, KA_WIN), BF16),
                   jax.ShapeDtypeStruct((bsz, N_KV, nt * wpt, VROWS, WKT), BF16)],
        compiler_params=_cparams(("parallel", "parallel")),
        name="nsa_relayout",
    )(p, p, p, p)


def _nsa_batch_kernel(sp_ref, q_ref, gt_ref, z_ref, kcmp_ref, vcmpt_ref, ovt_ref, ks_ref, vs_ref, kw_ref,
                      vw_ref, tri_ref, wb_ref, y_ref, sa_scr, sb_scr, g_scr, imp_scr, rank_scr, part_scr, selb_scr,
                      tile_idx):
    g = pl.program_id(0)
    qb = pl.program_id(1)
    n_items = q_ref.shape[0]
    start = qb * QBLK
    nb = kcmp_ref.shape[2]
    n_slc = ovt_ref.shape[0]
    nw = N_HPG * QBLK
    n_top = min(SLC_TOPN, n_slc)
    t_row = start + lax.broadcasted_iota(jnp.int32, (1, QBLK), 1)
    s_bufs = (sa_scr, sb_scr)
    last = qb // (SKT // QBLK)

    def tile4(a):
        return jnp.concatenate([a] * N_HPG, axis=1)

    def col_reduce8(s, op):
        out = s[0:8]
        for r in range(1, s.shape[0] // 8):
            out = op(out, s[8 * r:8 * r + 8])
        return out

    def col_max(s):
        return jnp.max(col_reduce8(s, jnp.maximum), axis=0, keepdims=True)

    def normalise(acc):
        den = acc[N_HD:N_HD + 1]
        return acc[0:N_HD] / jnp.where(den > 0, den, 1.0)

    def gate_row(i, h, branch):
        return g_scr[i, pl.ds(GATE_NG + (g * N_HPG + h) * 3 + branch, 1), :]

    hl = lax.broadcasted_iota(jnp.int32, (AUX_ROWS, nw), 1) // QBLK
    ar = lax.broadcasted_iota(jnp.int32, (AUX_ROWS, nw), 0)

    def per_head(vals):
        out = jnp.full((AUX_ROWS, nw), vals[N_HPG - 1], F32)
        for h in range(N_HPG - 2, -1, -1):
            out = jnp.where(hl == h, vals[h], out)
        return out

    pieces = [per_head([sp_ref[(g * N_HPG + h) * 3 + j] for h in range(N_HPG)]) for j in range(3)]
    aux = jnp.where(ar == 6, NEG, 0.0)
    for j in range(3):
        aux = jnp.where((ar == j) | (ar == j + 3), pieces[j], aux)
    aux_b = aux.astype(BF16)
    q_tail = jnp.zeros((KA_SEL - N_HD - 2 * AUX_ROWS, nw), BF16)
    sidx = lax.broadcasted_iota(jnp.int32, (n_slc, QBLK), 0)
    cur = t_row // SLC_LEN
    valid = sidx <= cur
    sub8 = lax.broadcasted_iota(jnp.int32, (8, QBLK), 0)
    last_blk = (start + QBLK - 1) // SLC_LEN
    cmp_end = lax.broadcasted_iota(jnp.int32, (nb, 1), 0) * CMP_STRIDE + (CMP_LEN - 1)
    cmp_bias = tile4(jnp.where(cmp_end <= t_row, 0.0, NEG))
    wrows = WIN + QBLK

    def branches(i):
        qt = (q_ref[i] * (N_HD ** -0.5)).T
        q4l = jnp.concatenate([qt[h * N_HD:(h + 1) * N_HD] for h in range(N_HPG)], axis=1) * LOG2E
        q4s = q4l.astype(BF16)
        q4lo = (q4l - q4s.astype(F32)).astype(BF16)
        g_scr[i] = jax.nn.sigmoid(gt_ref[i]).T

        q_cmp = jnp.concatenate([q4s, q4s, q4lo, aux_b, jnp.zeros((N_HD - AUX_ROWS, nw), BF16)], axis=0)
        s_c = jnp.dot(kcmp_ref[i, 0], q_cmp, preferred_element_type=F32) + cmp_bias
        e_c = jnp.exp2(s_c - col_max(s_c))
        z_c = jnp.sum(col_reduce8(e_c, jnp.add), axis=0, keepdims=True)
        inv_c = jnp.where(tile4(t_row >= CMP_LEN - 1) & (z_c > 0), 1.0 / z_c, 0.0)
        o_c = jnp.dot(vcmpt_ref[i, 0], e_c.astype(BF16), preferred_element_type=F32)[0:N_HD] * inv_c
        p_c = e_c * inv_c
        psum = p_c[:, 0:QBLK]
        for h in range(1, N_HPG):
            psum = psum + p_c[:, h * QBLK:(h + 1) * QBLK]
        parts, rest = [], psum
        for _ in range(3):
            parts.append(rest.astype(BF16))
            rest = rest - parts[-1].astype(F32)
        imp = jnp.dot(ovt_ref[...], jnp.concatenate(parts, axis=0), preferred_element_type=F32)
        imp = jnp.where(valid, imp, -jnp.inf)
        imp_scr[i] = jnp.where((sidx == 0) | (sidx == cur), jnp.inf, imp)

        q_win = jnp.concatenate([q4s, jnp.zeros((AUX_ROWS, nw), BF16), aux_b,
                                 jnp.zeros((KA_WIN - N_HD - 2 * AUX_ROWS, nw), BF16)], axis=0)
        s_w = jnp.dot(kw_ref[i, 0, pl.ds(pl.multiple_of(start, QBLK), wrows), :], q_win,
                      preferred_element_type=F32)
        s_w = jnp.concatenate([s_w[0:WKT] + tile4(wb_ref[0]), s_w[WKT:wrows - WKT],
                               s_w[wrows - WKT:] + tile4(wb_ref[1])], axis=0)
        p_w = jnp.exp2(s_w - col_max(s_w)).astype(BF16)
        v_w = jnp.concatenate([vw_ref[i, 0, qb * (QBLK // WKT) + j] for j in range(wrows // WKT)], axis=1)
        o_w = normalise(jnp.dot(v_w, p_w, preferred_element_type=F32))
        for h in range(N_HPG):
            hs = slice(h * QBLK, (h + 1) * QBLK)
            part_scr[i, :, hs] = gate_row(i, h, 0) * o_c[:, hs] + gate_row(i, h, 2) * o_w[:, hs]
        return q4s

    q4 = [branches(i) for i in range(n_items)]

    rank_scr[...] = jnp.zeros_like(rank_scr)
    for ri in range(n_slc // 8):
        @pl.when((last_blk >= n_top) & (8 * ri <= last_blk))
        def _():
            for i in range(n_items):
                rows = imp_scr[i, 8 * ri:8 * ri + 8]
                for r in range(n_slc // 8):
                    blk8 = imp_scr[i, 8 * r:8 * r + 8]
                    acc = rank_scr[i, 8 * r:8 * r + 8]
                    for ii in range(8):
                        row = rows[ii:ii + 1]
                        if ri < r:
                            before = row >= blk8
                        elif ri > r:
                            before = row > blk8
                        else:
                            before = (row > blk8) | ((row == blk8) & (sub8 > ii))
                        acc = acc + jnp.where(before, 1.0, 0.0)
                    rank_scr[i, 8 * r:8 * r + 8] = acc

    def select(i):
        chosen = (rank_scr[i] < n_top) & valid
        selb_scr[i] = jnp.where(chosen, 0.0, NEG)

        any_q = jnp.max(jnp.where(chosen, 1.0, 0.0), axis=1, keepdims=True) > 0
        s_io = lax.broadcasted_iota(jnp.int32, (n_slc, 1), 0)
        bits = jnp.where(any_q, jnp.left_shift(1, s_io % 32), 0)
        words = [jnp.sum(jnp.where(s_io // 32 == w, bits, 0)) for w in range((n_slc + 31) // 32)]
        bpt = SKT // SLC_LEN
        cnt = jnp.int32(0)
        for tile in range(n_slc // bpt):
            nib = lax.shift_right_logical(words[tile * bpt // 32], jnp.int32(tile * bpt % 32)) & (2 ** bpt - 1)
            tile_idx[i, cnt] = tile
            cnt = cnt + ((nib != 0) & (tile < last)).astype(jnp.int32)
        return cnt

    n_act = [select(i) for i in range(n_items)]


    def sel_scores(i, kt):
        grp = pl.multiple_of((kt * SKT // SLC_LEN) // SEL_GRP * SEL_GRP, SEL_GRP)
        sel_rows = jnp.concatenate([tile4(selb_scr[i, pl.ds(grp, SEL_GRP), :]), jnp.zeros((AUX_ROWS - SEL_GRP, nw), F32)],
                                   axis=0)
        q_sel = jnp.concatenate([q4[i], sel_rows.astype(BF16), aux_b, q_tail], axis=0)
        krow = pl.multiple_of((kt + PAD_TILES) * SKT, SKT)
        return jnp.dot(ks_ref[i, 0, pl.ds(krow, SKT), :], q_sel, preferred_element_type=F32)

    def tile_rows(kt):
        return pl.ds(pl.multiple_of(kt * SKT, SKT), SKT)

    def pass1(i):
        def fn(kt, mrun):
            s = sel_scores(i, kt)
            s_bufs[i % 2][tile_rows(kt), :] = s
            return jnp.maximum(mrun, col_reduce8(s, jnp.maximum))
        return fn

    def pass2(i, m_sel):
        def fn(kt, acc):
            pr = jnp.exp2(s_bufs[i % 2][tile_rows(kt), :] - m_sel).astype(BF16)
            return acc + jnp.dot(vs_ref[i, 0, kt + PAD_TILES], pr, preferred_element_type=F32)
        return fn

    def listed(i, fn):
        return lambda j, carry: fn(tile_idx[i, j], carry)

    def both(f1, f2):
        def fn(j, carry):
            return f1(j, carry[0]), f2(j, carry[1])
        return fn

    def run_tiles(fn, lo, n, init):
        def group(size, first):
            def body(j, carry):
                for u in range(size):
                    carry = fn(first + j * size + u, carry)
                return carry
            return body

        n_u = n // UNROLL
        carry = lax.fori_loop(0, n_u, group(UNROLL, lo), init)
        done = lo + n_u * UNROLL
        size = UNROLL // 2
        while size >= 1:
            take = (n // size) % 2
            carry = lax.fori_loop(0, take, group(size, done), carry)
            done = done + take * size
            size //= 2
        return carry

    def diag_scores(i):
        s = sel_scores(i, last) + tile4(tri_ref[qb % (SKT // QBLK)])
        s_bufs[i % 2][tile_rows(last), :] = s
        return col_reduce8(s, jnp.maximum)

    assert n_items <= len(s_bufs)
    acc0 = jnp.zeros((VROWS, nw), F32)
    m_diag = [diag_scores(i) for i in range(n_items)]
    m_prev = jnp.max(run_tiles(listed(0, pass1(0)), 0, n_act[0], m_diag[0]), axis=0, keepdims=True)
    closing = []
    for i in range(1, n_items):
        p2 = pass2(i - 1, m_prev)
        f1, f2 = listed(i, pass1(i)), listed(i - 1, p2)
        n_both = jnp.minimum(n_act[i], n_act[i - 1])
        mrun, acc = run_tiles(both(f1, f2), 0, n_both, (m_diag[i], acc0))
        mrun = run_tiles(f1, n_both, n_act[i] - n_both, mrun)
        closing.append((p2, run_tiles(f2, n_both, n_act[i - 1] - n_both, acc)))
        m_prev = jnp.max(mrun, axis=0, keepdims=True)
    p2 = pass2(n_items - 1, m_prev)
    closing.append((p2, run_tiles(listed(n_items - 1, p2), 0, n_act[n_items - 1], acc0)))
    outs = [normalise(p2(last, acc)) for p2, acc in closing]

    for i in range(n_items):
        tot = [part_scr[i, :, h * QBLK:(h + 1) * QBLK] + gate_row(i, h, 1) * outs[i][:, h * QBLK:(h + 1) * QBLK]
               for h in range(N_HPG)]
        o = jnp.concatenate(tot, axis=0).T
        y_ref[i] = (o * _silu(z_ref[i])).astype(BF16)


def _edge_biases():
    kl = np.arange(SKT)[:, None]
    ql = np.arange(QBLK)[None, :]
    tri = np.stack([np.where(kl <= par * QBLK + ql, 0.0, NEG) for par in range(SKT // QBLK)])
    kk = np.arange(WKT)[:, None]
    wb = np.stack([np.where(kk > ql, 0.0, NEG), np.where(kk <= ql + WKT - QBLK, 0.0, NEG)])
    return jnp.asarray(tri, F32), jnp.asarray(wb, F32)


def _nsa(p, spieces, kcmp, vcmpt, ovt, ks, vs, kw, vw):
    bsz, t, _ = p.shape
    n_slc = t // SLC_LEN
    gw = N_HPG * N_HD
    tri, wb = _edge_biases()
    nw = N_HPG * QBLK
    per_g = lambda a: pl.BlockSpec((bsz, 1) + a.shape[2:], lambda g, i, sp: (0, g) + (0,) * (a.ndim - 2))
    const = lambda a: pl.BlockSpec(a.shape, lambda g, i, sp: (0,) * a.ndim)
    grid_spec = pltpu.PrefetchScalarGridSpec(
        num_scalar_prefetch=1,
        grid=(N_KV, t // QBLK),
        in_specs=[pl.BlockSpec((bsz, QBLK, gw), lambda g, i, sp: (0, i, COL_NQ // gw + g)),
                  pl.BlockSpec((bsz, QBLK, LANES), lambda g, i, sp: (0, i, COL_GATES // LANES)),
                  pl.BlockSpec((bsz, QBLK, gw), lambda g, i, sp: (0, i, COL_NZ // gw + g)),
                  per_g(kcmp), per_g(vcmpt), const(ovt),
                  per_g(ks), per_g(vs), per_g(kw), per_g(vw), const(tri), const(wb)],
        out_specs=pl.BlockSpec((bsz, QBLK, gw), lambda g, i, sp: (0, i, g)),
        scratch_shapes=[pltpu.VMEM((t, nw), F32),
                        pltpu.VMEM((t, nw), F32),
                        pltpu.VMEM((bsz, LANES, QBLK), F32),
                        pltpu.VMEM((bsz, n_slc, QBLK), F32),
                        pltpu.VMEM((bsz, n_slc, QBLK), F32),
                        pltpu.VMEM((bsz, N_HD, nw), F32),
                        pltpu.VMEM((bsz, n_slc, QBLK), F32),
                        pltpu.SMEM((bsz, t // SKT), jnp.int32)],
    )
    return pl.pallas_call(
        _nsa_batch_kernel,
        grid_spec=grid_spec,
        out_shape=jax.ShapeDtypeStruct((bsz, t, N_WIDTH), BF16),
        compiler_params=_cparams(("arbitrary", "arbitrary")),
        name="nsa_attention",
    )(spieces, p, p, p, kcmp, vcmpt, ovt, ks, vs, kw, vw, tri, wb)


def _outproj_kernel(ym_ref, yn_ref, w_ref, x_ref, gate_ref, fg_ref, o_ref, wb_scr, *, final):
    @pl.when((pl.program_id(0) == 0) & (pl.program_id(1) == 0))
    def _():
        rows = w_ref.shape[1]
        for c in range(rows // WP_TILE):
            cs = slice(c * WP_TILE, (c + 1) * WP_TILE)
            wb_scr[cs, :] = w_ref[0, cs, :].astype(BF16)

    y = jnp.dot(ym_ref[0], wb_scr[0:M_WIDTH, :], preferred_element_type=F32)
    y = y + jnp.dot(yn_ref[0], wb_scr[M_WIDTH:, :], preferred_element_type=F32)
    hres = x_ref[0] + gate_ref[0] * y
    if final:
        ms = jnp.mean(hres * hres, axis=-1, keepdims=True)
        hres = hres * lax.rsqrt(ms + EPS) * fg_ref[...]
    o_ref[0] = hres


def _outproj(ym, yn, w, layer, x, gate, fg, final):
    bsz, t, d = x.shape
    tm = 512
    return pl.pallas_call(
        functools.partial(_outproj_kernel, final=final),
        grid=(bsz, t // tm),
        in_specs=[pl.BlockSpec((1, tm, M_WIDTH), lambda bi, i: (bi, i, 0)),
                  pl.BlockSpec((1, tm, N_WIDTH), lambda bi, i: (bi, i, 0)),
                  pl.BlockSpec((1, M_WIDTH + N_WIDTH, d), lambda bi, i: (layer, 0, 0),
                               pipeline_mode=pl.Buffered(1)),
                  pl.BlockSpec((1, tm, d), lambda bi, i: (bi, i, 0)),
                  pl.BlockSpec((1, 1, d), lambda bi, i: (bi, 0, 0)),
                  pl.BlockSpec((1, d), lambda bi, i: (0, 0))],
        out_specs=pl.BlockSpec((1, tm, d), lambda bi, i: (bi, i, 0)),
        out_shape=jax.ShapeDtypeStruct((bsz, t, d), F32),
        scratch_shapes=[pltpu.VMEM((M_WIDTH + N_WIDTH, d), BF16)],
        compiler_params=_cparams(("arbitrary", "arbitrary")),
        name="outproj_residual",
    )(ym, yn, w, x, gate, fg)


SRC_MI = 4 * M_WIDTH
SRC_NQ = SRC_MI + 2 * M_HEADS
SRC_NG = SRC_NQ + N_WIDTH + 6 * KV_W
SRC_NZ = SRC_NG + 3 * N_HEADS


def _reorder_cols(a):
    parts = [a[..., 0:SRC_MI], a[..., SRC_NQ:SRC_NG], a[..., SRC_NZ:SRC_NZ + N_WIDTH], a[..., SRC_MI:SRC_NQ],
             a[..., SRC_NG:SRC_NZ]]
    used = sum(x.shape[-1] for x in parts)
    parts.append(jnp.zeros(a.shape[:-1] + (NP_PAD - used,), a.dtype))
    return jnp.concatenate(parts, axis=-1)


WP_TILE = 512
WP_GATE_TILE = COL_GATES // WP_TILE


def _wprep_kernel(w_ref, g1_ref, g2_ref, o_ref):
    j = pl.program_id(0)
    d = o_ref.shape[0]

    @pl.when(j < WP_GATE_TILE)
    def _():
        for c in range(d // WP_TILE):
            cs = slice(c * WP_TILE, (c + 1) * WP_TILE)
            o_ref[cs, :] = w_ref[0, :, cs].T.astype(BF16)

    @pl.when(j == WP_GATE_TILE)
    def _():
        n_gate = g1_ref.shape[1] + g2_ref.shape[1]
        gt = jnp.concatenate([g1_ref[0], g2_ref[0], jnp.zeros((LANES - n_gate, d), F32)], axis=0)
        for c in range(d // WP_TILE):
            cs = slice(c * WP_TILE, (c + 1) * WP_TILE)
            o_ref[cs, 0:LANES] = gt[:, cs].T.astype(BF16)
        o_ref[:, LANES:] = jnp.zeros((d, WP_TILE - LANES), BF16)


def _wprep(w_t, layer):
    _, n, d = w_t.shape
    n_big = COL_GATES // WP_TILE
    assert n == SRC_NZ + N_WIDTH and NP_PAD // WP_TILE == n_big + 1

    def src_row(j):
        return jnp.where(j < COL_NQ // WP_TILE, j * WP_TILE,
                         jnp.where(j < COL_NZ // WP_TILE, SRC_NQ + (j - COL_NQ // WP_TILE) * WP_TILE,
                                   jnp.where(j < n_big, SRC_NZ + (j - COL_NZ // WP_TILE) * WP_TILE, 0)))

    el = pl.Element
    return pl.pallas_call(
        _wprep_kernel,
        grid=(NP_PAD // WP_TILE,),
        in_specs=[pl.BlockSpec((el(1), el(WP_TILE), el(d)), lambda j: (layer, pl.multiple_of(src_row(j), 8), 0)),
                  pl.BlockSpec((el(1), el(SRC_NQ - SRC_MI), el(d)), lambda j: (layer, SRC_MI, 0)),
                  pl.BlockSpec((el(1), el(SRC_NZ - SRC_NG), el(d)), lambda j: (layer, SRC_NG, 0))],
        out_specs=pl.BlockSpec((d, WP_TILE), lambda j: (0, j)),
        out_shape=jax.ShapeDtypeStruct((d, NP_PAD), BF16),
        compiler_params=_cparams(("parallel",)),
        name="inproj_weight_prep",
    )(w_t, w_t, w_t)


def _overlap_t(t):
    n_cmp_rows = t // CMP_STRIDE
    n_slc = t // SLC_LEN
    c0 = np.arange(n_cmp_rows) * CMP_STRIDE
    s0 = np.arange(n_slc) * SLC_LEN
    ov = (c0[None, :] <= s0[:, None] + SLC_LEN - 1) & (c0[None, :] + CMP_LEN - 1 >= s0[:, None])
    ov[:, (t - CMP_LEN) // CMP_STRIDE + 1:] = False
    return jnp.asarray(np.concatenate([ov] * 3, axis=1), BF16)


def kernel(x, c, ln_g, w_ada, b_ada, w_in, b_in, m_conv_w, m_conv_b, m_wq, m_wk, m_norm_w, m_skip, m_f_bias,
           n_pos_k, n_pos_v, n_w1_k, n_w2_k, n_w1_v, n_w2_v, w_out, final_g):
    out_dtype = x.dtype
    bsz, t, d = x.shape
    depth = ln_g.shape[0]
    h_res = x.astype(F32)
    assert bsz <= 2 and t % 1024 == 0 and t // CMP_STRIDE <= 256
    c_t =jnp.zeros((d, 8), F32).at[:, :bsz].set(c.astype(F32).T)
    slopes_np = np.array([2.0 ** (-8.0 * (h + 1) / N_HEADS) for h in range(N_HEADS)], np.float32)
    rest = (slopes_np.astype(np.float64) * LOG2E).astype(np.float32)
    pieces = []
    for _ in range(3):
        pieces.append(rest.astype(jnp.bfloat16).astype(np.float32))
        rest = rest - pieces[-1]
    spieces = jnp.asarray(np.stack(pieces, axis=1).reshape(-1))
    ovt = _overlap_t(t)

    def w1cat(w1):
        w = w1.reshape(2, CMP_STRIDE, N_HD, CMP_HIDDEN)
        w = jnp.concatenate([w[0], w[1]], axis=-1).astype(BF16)
        z = jnp.zeros_like(w)
        return jnp.concatenate([jnp.concatenate([w, z], axis=-1), jnp.concatenate([z, w], axis=-1)], axis=1)

    def w2pad(w2):
        return jnp.pad(w2, ((0, 0), (0, LANES - N_HD))).astype(BF16)

    for l in range(depth):
        mod = _ada(c_t, w_ada, b_ada[l][None, :], l, bsz)[:bsz]
        shift, scale, gate = mod[:, None, 0:d], mod[:, None, d:2 * d], mod[:, None, 2 * d:3 * d]
        p = _inproj(h_res, ln_g[l][None, :], scale, shift,
                    _wprep(jnp.swapaxes(w_in, 1, 2), l), _reorder_cols(b_in[l])[None, :])
        fb_row = jnp.zeros((1, LANES), F32).at[0, M_HEADS:2 * M_HEADS].set(m_f_bias[l])
        y_m = _mlstm(p, m_conv_w[l], m_conv_b[l][None, :], m_wq[l].astype(BF16), m_wk[l].astype(BF16),
                     m_norm_w[l][None, :], m_skip[l][None, :], fb_row)
        kcmp, vcmpt = _compress(p, n_pos_k[l].reshape(1, -1), n_pos_v[l].reshape(1, -1),
                                n_w1_k[l].astype(BF16), w1cat(n_w1_k[l]), w2pad(n_w2_k[l]),
                                n_w1_v[l].astype(BF16), w1cat(n_w1_v[l]), w2pad(n_w2_v[l]))
        ks, vs, kw, vw = _relayout(p)
        y_n = _nsa(p, spieces, kcmp, vcmpt, ovt, ks, vs, kw, vw)
        h_res = _outproj(y_m, y_n, w_out, l, h_res, gate, final_g[None, :], l == depth - 1)
    return h_res.astype(out_dtype)
```
